```python
import jax, jax.numpy as jnp
from jax import lax
import numpy as np

D_MODEL = 1024
BATCH = 8
SEQ = 8192
DEPTH = 4

CHUNK = 64
N_LEFT_CHUNKS = 8
BAND = (N_LEFT_CHUNKS + 1) * CHUNK
HEAD_DIM = 64
N_HEADS_A = 8
N_HEADS_B = 8
A_W = N_HEADS_A * HEAD_DIM
B_W = N_HEADS_B * HEAD_DIM
MIX_W = A_W + B_W
REL_CLIP = 256
Q_BLOCK = 128
LRU_WIDTH = D_MODEL
LRU_BLOCKS = 4
LRU_BLOCK_W = LRU_WIDTH // LRU_BLOCKS
CONV_WIDTH = 4
LRU_C = 8.0
D_FF = -(-(8 * D_MODEL) // (3 * 256)) * 256
RMS_EPS = 1e-6
N_ATTN_LAYERS = (DEPTH + 1) // 2
N_REC_LAYERS = DEPTH // 2

kernel_name = "hybrid_chunked_sb_rglru_trunk"


def rmsnorm(x, g):
    xf = x.astype(jnp.float32)
    y = xf * lax.rsqrt(jnp.mean(xf * xf, axis=-1, keepdims=True) + RMS_EPS)
    return (y * g.astype(jnp.float32)).astype(x.dtype)


def chunked_relpos_attention(q, k, v, rel_bias):
    b, s, h, dh = q.shape
    nc = s // CHUNK
    qc = q.reshape(b, nc, CHUNK, h, dh)

    def gather_band(t):
        tc = t.reshape(b, nc, CHUNK, h, dh)
        tp = jnp.pad(tc, ((0, 0), (N_LEFT_CHUNKS, 0), (0, 0), (0, 0), (0, 0)))
        return jnp.concatenate([tp[:, j:j + nc] for j in range(N_LEFT_CHUNKS + 1)], axis=2)

    kb, vb = gather_band(k), gather_band(v)
    scores = jnp.einsum('bcqhd,bckhd->bhcqk', qc, kb).astype(jnp.float32) * (dh ** -0.5)
    qpos = N_LEFT_CHUNKS * CHUNK + jnp.arange(CHUNK)
    kpos = jnp.arange(BAND)
    rel = jnp.clip(qpos[:, None] - kpos[None, :], -REL_CLIP, REL_CLIP) + REL_CLIP
    bias = rel_bias[:, rel].astype(jnp.float32)
    key_chunk = jnp.arange(nc)[:, None] - N_LEFT_CHUNKS + (kpos // CHUNK)[None, :]
    valid = key_chunk >= 0
    scores = scores + bias[None, :, None, :, :]
    scores = jnp.where(valid[None, None, :, None, :], scores, -jnp.inf)
    p = jax.nn.softmax(scores, axis=-1).astype(v.dtype)
    o = jnp.einsum('bhcqk,bckhd->bcqhd', p, vb)
    return o.reshape(b, s, h * dh)


def stick_breaking_attention(q, k, v):
    b, s, h, dh = q.shape
    nb = s // Q_BLOCK
    scale = dh ** -0.5
    kpos = jnp.arange(s)
    q_blocks = q.reshape(b, nb, Q_BLOCK, h, dh).transpose(1, 0, 2, 3, 4)

    def one_block(args):
        q_blk, blk = args
        z = jnp.einsum('bqhd,bkhd->bhqk', q_blk, k).astype(jnp.float32) * scale
        qpos = blk * Q_BLOCK + jnp.arange(Q_BLOCK)
        causal = kpos[None, :] < qpos[:, None]
        log_beta = jax.nn.log_sigmoid(z)
        log_1m_beta = jnp.where(causal, jax.nn.log_sigmoid(-z), 0.0)
        after = lax.cumsum(log_1m_beta, axis=3, reverse=True) - log_1m_beta
        w = jnp.where(causal, jnp.exp(log_beta + after), 0.0).astype(v.dtype)
        return jnp.einsum('bhqk,bkhd->bqhd', w, v)

    o = lax.map(one_block, (q_blocks, jnp.arange(nb)))
    return o.transpose(1, 0, 2, 3, 4).reshape(b, s, h * dh)


def attention_mixer(h, w_in, rel_bias, w_out):
    b, s, _ = h.shape
    proj = h @ w_in
    part_a, part_b = proj[..., :3 * A_W], proj[..., 3 * A_W:]
    qa, ka, va = [t.reshape(b, s, N_HEADS_A, HEAD_DIM) for t in jnp.split(part_a, 3, axis=-1)]
    qs, ks, vs = [t.reshape(b, s, N_HEADS_B, HEAD_DIM) for t in jnp.split(part_b, 3, axis=-1)]
    out_a = chunked_relpos_attention(qa, ka, va, rel_bias)
    out_b = stick_breaking_attention(qs, ks, vs)
    return jnp.concatenate([out_a, out_b], axis=-1) @ w_out


def recurrent_mixer(h, w_in, conv_w, conv_b, w_a, b_a, w_i, b_i, lam, w_out):
    b, s, _ = h.shape
    proj = h @ w_in
    gate, xr = jnp.split(proj, 2, axis=-1)
    gate = jax.nn.gelu(gate, approximate=True)
    xc = lax.conv_general_dilated(
        xr, conv_w, window_strides=(1,), padding=[(CONV_WIDTH - 1, 0)],
        dimension_numbers=('NWC', 'WIO', 'NWC'), feature_group_count=LRU_WIDTH) + conv_b
    xg = xc.reshape(b, s, LRU_BLOCKS, LRU_BLOCK_W)
    r = jax.nn.sigmoid(jnp.einsum('bsni,nij->bsnj', xg, w_a) + b_a).reshape(b, s, LRU_WIDTH)
    i = jax.nn.sigmoid(jnp.einsum('bsni,nij->bsnj', xg, w_i) + b_i).reshape(b, s, LRU_WIDTH)
    log_a = -LRU_C * r.astype(jnp.float32) * jax.nn.softplus(-lam.astype(jnp.float32))
    a = jnp.exp(log_a)
    mult = jnp.sqrt(-jnp.expm1(2.0 * log_a))
    u = mult * (i * xc).astype(jnp.float32)

    def combine(left, right):
        a1, b1 = left
        a2, b2 = right
        return a1 * a2, a2 * b1 + b2

    _, hs = lax.associative_scan(combine, (a, u), axis=1)
    return (hs.astype(h.dtype) * gate) @ w_out


def swiglu(h, w_gate, w_up, w_down):
    return (jax.nn.silu(h @ w_gate) * (h @ w_up)) @ w_down


def _fwd_setup_inputs(seed: int = 0) -> dict:
    key = jax.random.key(seed)
    ks = jax.random.split(key, 24)
    f32 = jnp.float32

    def nrm(k, shape, fan_in):
        return jax.random.normal(k, shape, f32) * (fan_in ** -0.5)

    def gain(k, shape):
        return 1.0 + 0.05 * jax.random.normal(k, shape, f32)

    u = jax.random.uniform(ks[12], (N_REC_LAYERS, LRU_WIDTH), f32, 0.9, 0.999)
    base = u ** (1.0 / LRU_C)
    lam = jnp.log(base) - jnp.log1p(-base)
    return {
        'x': jax.random.normal(ks[0], (BATCH, SEQ, D_MODEL), f32),
        'attn_w_in': nrm(ks[1], (N_ATTN_LAYERS, D_MODEL, 3 * MIX_W), D_MODEL),
        'attn_rel_bias': 0.2 * jax.random.normal(ks[2], (N_ATTN_LAYERS, N_HEADS_A, 2 * REL_CLIP + 1), f32),
        'attn_w_out': nrm(ks[3], (N_ATTN_LAYERS, MIX_W, D_MODEL), MIX_W),
        'rg_w_in': nrm(ks[4], (N_REC_LAYERS, D_MODEL, 2 * LRU_WIDTH), D_MODEL),
        'rg_conv_w': nrm(ks[5], (N_REC_LAYERS, CONV_WIDTH, 1, LRU_WIDTH), CONV_WIDTH),
        'rg_conv_b': 0.01 * jax.random.normal(ks[6], (N_REC_LAYERS, LRU_WIDTH), f32),
        'rg_w_a': nrm(ks[7], (N_REC_LAYERS, LRU_BLOCKS, LRU_BLOCK_W, LRU_BLOCK_W), LRU_BLOCK_W),
        'rg_b_a': 0.01 * jax.random.normal(ks[8], (N_REC_LAYERS, LRU_BLOCKS, LRU_BLOCK_W), f32),
        'rg_w_i': nrm(ks[9], (N_REC_LAYERS, LRU_BLOCKS, LRU_BLOCK_W, LRU_BLOCK_W), LRU_BLOCK_W),
        'rg_b_i': 0.01 * jax.random.normal(ks[10], (N_REC_LAYERS, LRU_BLOCKS, LRU_BLOCK_W), f32),
        'rg_lambda': lam,
        'rg_w_out': nrm(ks[11], (N_REC_LAYERS, LRU_WIDTH, D_MODEL), LRU_WIDTH),
        'norm_mix_pre': gain(ks[13], (DEPTH, D_MODEL)),
        'norm_mix_post': gain(ks[14], (DEPTH, D_MODEL)),
        'norm_ffn_pre': gain(ks[15], (DEPTH, D_MODEL)),
        'norm_ffn_post': gain(ks[16], (DEPTH, D_MODEL)),
        'ffn_w_gate': nrm(ks[17], (DEPTH, D_MODEL, D_FF), D_MODEL),
        'ffn_w_up': nrm(ks[18], (DEPTH, D_MODEL, D_FF), D_MODEL),
        'ffn_w_down': nrm(ks[19], (DEPTH, D_FF, D_MODEL), D_FF),
    }


def _fwd_reference(x, attn_w_in, attn_rel_bias, attn_w_out, rg_w_in, rg_conv_w, rg_conv_b,
              rg_w_a, rg_b_a, rg_w_i, rg_b_i, rg_lambda, rg_w_out,
              norm_mix_pre, norm_mix_post, norm_ffn_pre, norm_ffn_post,
              ffn_w_gate, ffn_w_up, ffn_w_down):
    for layer in range(DEPTH):
        j = layer // 2
        h = rmsnorm(x, norm_mix_pre[layer])
        if layer % 2 == 0:
            m = attention_mixer(h, attn_w_in[j], attn_rel_bias[j], attn_w_out[j])
        else:
            m = recurrent_mixer(h, rg_w_in[j], rg_conv_w[j], rg_conv_b[j], rg_w_a[j], rg_b_a[j],
                                rg_w_i[j], rg_b_i[j], rg_lambda[j], rg_w_out[j])
        x = x + rmsnorm(m, norm_mix_post[layer])
        h = rmsnorm(x, norm_ffn_pre[layer])
        f = swiglu(h, ffn_w_gate[layer], ffn_w_up[layer], ffn_w_down[layer])
        x = x + rmsnorm(f, norm_ffn_post[layer])
    return x


import jax as _jax
import jax.numpy as _jnp

TWIN_FORMAT = 'train_step'
FWD_PARAMS = ['x', 'attn_w_in', 'attn_rel_bias', 'attn_w_out', 'rg_w_in', 'rg_conv_w', 'rg_conv_b', 'rg_w_a', 'rg_b_a', 'rg_w_i', 'rg_b_i', 'rg_lambda', 'rg_w_out', 'norm_mix_pre', 'norm_mix_post', 'norm_ffn_pre', 'norm_ffn_post', 'ffn_w_gate', 'ffn_w_up', 'ffn_w_down']
TWIN_WEIGHTS = ['attn_w_in', 'attn_rel_bias', 'attn_w_out', 'rg_w_in', 'rg_conv_w', 'rg_conv_b', 'rg_w_a', 'rg_b_a', 'rg_w_i', 'rg_b_i', 'rg_lambda', 'rg_w_out', 'norm_mix_pre', 'norm_mix_post', 'norm_ffn_pre', 'norm_ffn_post', 'ffn_w_gate', 'ffn_w_up', 'ffn_w_down']
TWIN_DIFF_INPUT = 'x'
TWIN_INPUTS = ['x', 'attn_w_in', 'attn_rel_bias', 'attn_w_out', 'rg_w_in', 'rg_conv_w', 'rg_conv_b', 'rg_w_a', 'rg_b_a', 'rg_w_i', 'rg_b_i', 'rg_lambda', 'rg_w_out', 'norm_mix_pre', 'norm_mix_post', 'norm_ffn_pre', 'norm_ffn_post', 'ffn_w_gate', 'ffn_w_up', 'ffn_w_down', 'loss_target', 'm_attn_w_in', 'm_attn_rel_bias', 'm_attn_w_out', 'm_rg_w_in', 'm_rg_conv_w', 'm_rg_conv_b', 'm_rg_w_a', 'm_rg_b_a', 'm_rg_w_i', 'm_rg_b_i', 'm_rg_lambda', 'm_rg_w_out', 'm_norm_mix_pre', 'm_norm_mix_post', 'm_norm_ffn_pre', 'm_norm_ffn_post', 'm_ffn_w_gate', 'm_ffn_w_up', 'm_ffn_w_down', 'v_attn_w_in', 'v_attn_rel_bias', 'v_attn_w_out', 'v_rg_w_in', 'v_rg_conv_w', 'v_rg_conv_b', 'v_rg_w_a', 'v_rg_b_a', 'v_rg_w_i', 'v_rg_b_i', 'v_rg_lambda', 'v_rg_w_out', 'v_norm_mix_pre', 'v_norm_mix_post', 'v_norm_ffn_pre', 'v_norm_ffn_post', 'v_ffn_w_gate', 'v_ffn_w_up', 'v_ffn_w_down']
TWIN_OUTPUTS = ['loss', 'grad_x', 'grad_attn_w_in', 'grad_attn_rel_bias', 'grad_attn_w_out', 'grad_rg_w_in', 'grad_rg_conv_w', 'grad_rg_conv_b', 'grad_rg_w_a', 'grad_rg_b_a', 'grad_rg_w_i', 'grad_rg_b_i', 'grad_rg_lambda', 'grad_rg_w_out', 'grad_norm_mix_pre', 'grad_norm_mix_post', 'grad_norm_ffn_pre', 'grad_norm_ffn_post', 'grad_ffn_w_gate', 'grad_ffn_w_up', 'grad_ffn_w_down', 'delta_attn_w_in', 'delta_attn_rel_bias', 'delta_attn_w_out', 'delta_rg_w_in', 'delta_rg_conv_w', 'delta_rg_conv_b', 'delta_rg_w_a', 'delta_rg_b_a', 'delta_rg_w_i', 'delta_rg_b_i', 'delta_rg_lambda', 'delta_rg_w_out', 'delta_norm_mix_pre', 'delta_norm_mix_post', 'delta_norm_ffn_pre', 'delta_norm_ffn_post', 'delta_ffn_w_gate', 'delta_ffn_w_up', 'delta_ffn_w_down', 'new_m_attn_w_in', 'new_m_attn_rel_bias', 'new_m_attn_w_out', 'new_m_rg_w_in', 'new_m_rg_conv_w', 'new_m_rg_conv_b', 'new_m_rg_w_a', 'new_m_rg_b_a', 'new_m_rg_w_i', 'new_m_rg_b_i', 'new_m_rg_lambda', 'new_m_rg_w_out', 'new_m_norm_mix_pre', 'new_m_norm_mix_post', 'new_m_norm_ffn_pre', 'new_m_norm_ffn_post', 'new_m_ffn_w_gate', 'new_m_ffn_w_up', 'new_m_ffn_w_down', 'new_v_attn_w_in', 'new_v_attn_rel_bias', 'new_v_attn_w_out', 'new_v_rg_w_in', 'new_v_rg_conv_w', 'new_v_rg_conv_b', 'new_v_rg_w_a', 'new_v_rg_b_a', 'new_v_rg_w_i', 'new_v_rg_b_i', 'new_v_rg_lambda', 'new_v_rg_w_out', 'new_v_norm_mix_pre', 'new_v_norm_mix_post', 'new_v_norm_ffn_pre', 'new_v_norm_ffn_post', 'new_v_ffn_w_gate', 'new_v_ffn_w_up', 'new_v_ffn_w_down']
TWIN_LEAF_KINDS = {'loss': 'loss', 'grad_x': 'grad_x', 'grad_attn_w_in': 'grad_w', 'grad_attn_rel_bias': 'grad_w', 'grad_attn_w_out': 'grad_w', 'grad_rg_w_in': 'grad_w', 'grad_rg_conv_w': 'grad_w', 'grad_rg_conv_b': 'grad_w', 'grad_rg_w_a': 'grad_w', 'grad_rg_b_a': 'grad_w', 'grad_rg_w_i': 'grad_w', 'grad_rg_b_i': 'grad_w', 'grad_rg_lambda': 'grad_w', 'grad_rg_w_out': 'grad_w', 'grad_norm_mix_pre': 'grad_w', 'grad_norm_mix_post': 'grad_w', 'grad_norm_ffn_pre': 'grad_w', 'grad_norm_ffn_post': 'grad_w', 'grad_ffn_w_gate': 'grad_w', 'grad_ffn_w_up': 'grad_w', 'grad_ffn_w_down': 'grad_w', 'delta_attn_w_in': 'delta_w', 'delta_attn_rel_bias': 'delta_w', 'delta_attn_w_out': 'delta_w', 'delta_rg_w_in': 'delta_w', 'delta_rg_conv_w': 'delta_w', 'delta_rg_conv_b': 'delta_w', 'delta_rg_w_a': 'delta_w', 'delta_rg_b_a': 'delta_w', 'delta_rg_w_i': 'delta_w', 'delta_rg_b_i': 'delta_w', 'delta_rg_lambda': 'delta_w', 'delta_rg_w_out': 'delta_w', 'delta_norm_mix_pre': 'delta_w', 'delta_norm_mix_post': 'delta_w', 'delta_norm_ffn_pre': 'delta_w', 'delta_norm_ffn_post': 'delta_w', 'delta_ffn_w_gate': 'delta_w', 'delta_ffn_w_up': 'delta_w', 'delta_ffn_w_down': 'delta_w', 'new_m_attn_w_in': 'new_m', 'new_m_attn_rel_bias': 'new_m', 'new_m_attn_w_out': 'new_m', 'new_m_rg_w_in': 'new_m', 'new_m_rg_conv_w': 'new_m', 'new_m_rg_conv_b': 'new_m', 'new_m_rg_w_a': 'new_m', 'new_m_rg_b_a': 'new_m', 'new_m_rg_w_i': 'new_m', 'new_m_rg_b_i': 'new_m', 'new_m_rg_lambda': 'new_m', 'new_m_rg_w_out': 'new_m', 'new_m_norm_mix_pre': 'new_m', 'new_m_norm_mix_post': 'new_m', 'new_m_norm_ffn_pre': 'new_m', 'new_m_norm_ffn_post': 'new_m', 'new_m_ffn_w_gate': 'new_m', 'new_m_ffn_w_up': 'new_m', 'new_m_ffn_w_down': 'new_m', 'new_v_attn_w_in': 'new_v', 'new_v_attn_rel_bias': 'new_v', 'new_v_attn_w_out': 'new_v', 'new_v_rg_w_in': 'new_v', 'new_v_rg_conv_w': 'new_v', 'new_v_rg_conv_b': 'new_v', 'new_v_rg_w_a': 'new_v', 'new_v_rg_b_a': 'new_v', 'new_v_rg_w_i': 'new_v', 'new_v_rg_b_i': 'new_v', 'new_v_rg_lambda': 'new_v', 'new_v_rg_w_out': 'new_v', 'new_v_norm_mix_pre': 'new_v', 'new_v_norm_mix_post': 'new_v', 'new_v_norm_ffn_pre': 'new_v', 'new_v_norm_ffn_post': 'new_v', 'new_v_ffn_w_gate': 'new_v', 'new_v_ffn_w_up': 'new_v', 'new_v_ffn_w_down': 'new_v'}


def _forward(args):
    return _fwd_reference(*[args[k] for k in FWD_PARAMS])


def _output_shape():
    def fwd():
        inp = _fwd_setup_inputs(0)
        return _fwd_reference(*[inp[k] for k in FWD_PARAMS])
    out = _jax.eval_shape(fwd)
    return out.shape, out.dtype

N_MICROBATCH = 1
ADAM_LR = 0.001
ADAM_B1 = 0.9
ADAM_B2 = 0.999
ADAM_EPS = 1e-08
ADAM_WD = 0.01
ADAM_STEP = 10
PER_EXAMPLE_BATCH_AXIS = {'x': 0, 'loss_target': 0}
SHARED_INPUTS = []
_WEIGHT_DTYPES = {'attn_w_in': _jnp.float32, 'attn_rel_bias': _jnp.float32, 'attn_w_out': _jnp.float32, 'rg_w_in': _jnp.float32, 'rg_conv_w': _jnp.float32, 'rg_conv_b': _jnp.float32, 'rg_w_a': _jnp.float32, 'rg_b_a': _jnp.float32, 'rg_w_i': _jnp.float32, 'rg_b_i': _jnp.float32, 'rg_lambda': _jnp.float32, 'rg_w_out': _jnp.float32, 'norm_mix_pre': _jnp.float32, 'norm_mix_post': _jnp.float32, 'norm_ffn_pre': _jnp.float32, 'norm_ffn_post': _jnp.float32, 'ffn_w_gate': _jnp.float32, 'ffn_w_up': _jnp.float32, 'ffn_w_down': _jnp.float32}
MOMENT_SCALE = {'attn_w_in': 1.929075e+00, 'attn_rel_bias': 2.418117e-01, 'attn_w_out': 3.343906e+00, 'rg_w_in': 2.166916e+00, 'rg_conv_w': 3.270767e+00, 'rg_conv_b': 6.668431e+01, 'rg_w_a': 1.593057e+00, 'rg_b_a': 8.156682e-01, 'rg_w_i': 3.016647e+00, 'rg_b_i': 1.207886e+00, 'rg_lambda': 1.502390e+00, 'rg_w_out': 3.610379e+00, 'norm_mix_pre': 3.219365e+00, 'norm_mix_post': 6.304021e+01, 'norm_ffn_pre': 2.690103e+00, 'norm_ffn_post': 6.384104e+01, 'ffn_w_gate': 9.639275e-01, 'ffn_w_up': 1.203199e+00, 'ffn_w_down': 2.015963e+00}


def _to_microbatches(a, axis):
    t = _jnp.moveaxis(a, axis, 0)
    t = t.reshape((N_MICROBATCH, t.shape[0] // N_MICROBATCH) + t.shape[1:])
    return _jnp.moveaxis(t, 1, axis + 1)


def setup_inputs(seed: int = 0) -> dict:
    inp = _fwd_setup_inputs(seed)
    key = _jax.random.fold_in(_jax.random.key(seed), 7919)
    shape, _ = _output_shape()
    out = dict(inp)
    out["loss_target"] = _jax.random.normal(_jax.random.fold_in(key, 0), shape, _jnp.float32)
    for i, name in enumerate(TWIN_WEIGHTS):
        w = inp[name].astype(_jnp.float32)
        if MOMENT_SCALE is None:
            s = _jnp.sqrt(_jnp.mean(_jnp.square(w)) + 1e-30)
        else:
            s = MOMENT_SCALE[name]
        km, kv = _jax.random.split(_jax.random.fold_in(key, i + 1))
        out[name] = w
        out["m_" + name] = s * _jax.random.normal(km, w.shape, _jnp.float32)
        out["v_" + name] = (s * s) * _jax.random.uniform(kv, w.shape, _jnp.float32, 0.5, 1.5)
    if N_MICROBATCH > 1:
        for name, axis in PER_EXAMPLE_BATCH_AXIS.items():
            out[name] = _to_microbatches(out[name], axis)
    return {'x': out['x'], 'attn_w_in': out['attn_w_in'], 'attn_rel_bias': out['attn_rel_bias'], 'attn_w_out': out['attn_w_out'], 'rg_w_in': out['rg_w_in'], 'rg_conv_w': out['rg_conv_w'], 'rg_conv_b': out['rg_conv_b'], 'rg_w_a': out['rg_w_a'], 'rg_b_a': out['rg_b_a'], 'rg_w_i': out['rg_w_i'], 'rg_b_i': out['rg_b_i'], 'rg_lambda': out['rg_lambda'], 'rg_w_out': out['rg_w_out'], 'norm_mix_pre': out['norm_mix_pre'], 'norm_mix_post': out['norm_mix_post'], 'norm_ffn_pre': out['norm_ffn_pre'], 'norm_ffn_post': out['norm_ffn_post'], 'ffn_w_gate': out['ffn_w_gate'], 'ffn_w_up': out['ffn_w_up'], 'ffn_w_down': out['ffn_w_down'], 'loss_target': out['loss_target'], 'm_attn_w_in': out['m_attn_w_in'], 'm_attn_rel_bias': out['m_attn_rel_bias'], 'm_attn_w_out': out['m_attn_w_out'], 'm_rg_w_in': out['m_rg_w_in'], 'm_rg_conv_w': out['m_rg_conv_w'], 'm_rg_conv_b': out['m_rg_conv_b'], 'm_rg_w_a': out['m_rg_w_a'], 'm_rg_b_a': out['m_rg_b_a'], 'm_rg_w_i': out['m_rg_w_i'], 'm_rg_b_i': out['m_rg_b_i'], 'm_rg_lambda': out['m_rg_lambda'], 'm_rg_w_out': out['m_rg_w_out'], 'm_norm_mix_pre': out['m_norm_mix_pre'], 'm_norm_mix_post': out['m_norm_mix_post'], 'm_norm_ffn_pre': out['m_norm_ffn_pre'], 'm_norm_ffn_post': out['m_norm_ffn_post'], 'm_ffn_w_gate': out['m_ffn_w_gate'], 'm_ffn_w_up': out['m_ffn_w_up'], 'm_ffn_w_down': out['m_ffn_w_down'], 'v_attn_w_in': out['v_attn_w_in'], 'v_attn_rel_bias': out['v_attn_rel_bias'], 'v_attn_w_out': out['v_attn_w_out'], 'v_rg_w_in': out['v_rg_w_in'], 'v_rg_conv_w': out['v_rg_conv_w'], 'v_rg_conv_b': out['v_rg_conv_b'], 'v_rg_w_a': out['v_rg_w_a'], 'v_rg_b_a': out['v_rg_b_a'], 'v_rg_w_i': out['v_rg_w_i'], 'v_rg_b_i': out['v_rg_b_i'], 'v_rg_lambda': out['v_rg_lambda'], 'v_rg_w_out': out['v_rg_w_out'], 'v_norm_mix_pre': out['v_norm_mix_pre'], 'v_norm_mix_post': out['v_norm_mix_post'], 'v_norm_ffn_pre': out['v_norm_ffn_pre'], 'v_norm_ffn_post': out['v_norm_ffn_post'], 'v_ffn_w_gate': out['v_ffn_w_gate'], 'v_ffn_w_up': out['v_ffn_w_up'], 'v_ffn_w_down': out['v_ffn_w_down']}


def _loss(weights, diff, rest, loss_target):
    with _jax.named_scope("forward"):
        args = {**rest, TWIN_DIFF_INPUT: diff, **{k: w.astype(_WEIGHT_DTYPES[k]) for k, w in weights.items()}}
        y = _forward(args)
    with _jax.named_scope("loss_head"):
        err = _jnp.square(y.astype(_jnp.float32) - loss_target)
        return 0.5 * _jnp.sum(_jnp.mean(err, axis=-1)) if err.ndim else 0.5 * err


def _adamw(w, g, m, v):
    m = ADAM_B1 * m + (1.0 - ADAM_B1) * g
    v = ADAM_B2 * v + (1.0 - ADAM_B2) * _jnp.square(g)
    m_hat = m / (1.0 - ADAM_B1 ** ADAM_STEP)
    v_hat = v / (1.0 - ADAM_B2 ** ADAM_STEP)
    delta = -ADAM_LR * (m_hat / (_jnp.sqrt(v_hat) + ADAM_EPS) + ADAM_WD * w)
    return delta, m, v


def reference(x, attn_w_in, attn_rel_bias, attn_w_out, rg_w_in, rg_conv_w, rg_conv_b, rg_w_a, rg_b_a, rg_w_i, rg_b_i, rg_lambda, rg_w_out, norm_mix_pre, norm_mix_post, norm_ffn_pre, norm_ffn_post, ffn_w_gate, ffn_w_up, ffn_w_down, loss_target, m_attn_w_in, m_attn_rel_bias, m_attn_w_out, m_rg_w_in, m_rg_conv_w, m_rg_conv_b, m_rg_w_a, m_rg_b_a, m_rg_w_i, m_rg_b_i, m_rg_lambda, m_rg_w_out, m_norm_mix_pre, m_norm_mix_post, m_norm_ffn_pre, m_norm_ffn_post, m_ffn_w_gate, m_ffn_w_up, m_ffn_w_down, v_attn_w_in, v_attn_rel_bias, v_attn_w_out, v_rg_w_in, v_rg_conv_w, v_rg_conv_b, v_rg_w_a, v_rg_b_a, v_rg_w_i, v_rg_b_i, v_rg_lambda, v_rg_w_out, v_norm_mix_pre, v_norm_mix_post, v_norm_ffn_pre, v_norm_ffn_post, v_ffn_w_gate, v_ffn_w_up, v_ffn_w_down):
    given = dict(x=x, attn_w_in=attn_w_in, attn_rel_bias=attn_rel_bias, attn_w_out=attn_w_out, rg_w_in=rg_w_in, rg_conv_w=rg_conv_w, rg_conv_b=rg_conv_b, rg_w_a=rg_w_a, rg_b_a=rg_b_a, rg_w_i=rg_w_i, rg_b_i=rg_b_i, rg_lambda=rg_lambda, rg_w_out=rg_w_out, norm_mix_pre=norm_mix_pre, norm_mix_post=norm_mix_post, norm_ffn_pre=norm_ffn_pre, norm_ffn_post=norm_ffn_post, ffn_w_gate=ffn_w_gate, ffn_w_up=ffn_w_up, ffn_w_down=ffn_w_down, loss_target=loss_target, m_attn_w_in=m_attn_w_in, m_attn_rel_bias=m_attn_rel_bias, m_attn_w_out=m_attn_w_out, m_rg_w_in=m_rg_w_in, m_rg_conv_w=m_rg_conv_w, m_rg_conv_b=m_rg_conv_b, m_rg_w_a=m_rg_w_a, m_rg_b_a=m_rg_b_a, m_rg_w_i=m_rg_w_i, m_rg_b_i=m_rg_b_i, m_rg_lambda=m_rg_lambda, m_rg_w_out=m_rg_w_out, m_norm_mix_pre=m_norm_mix_pre, m_norm_mix_post=m_norm_mix_post, m_norm_ffn_pre=m_norm_ffn_pre, m_norm_ffn_post=m_norm_ffn_post, m_ffn_w_gate=m_ffn_w_gate, m_ffn_w_up=m_ffn_w_up, m_ffn_w_down=m_ffn_w_down, v_attn_w_in=v_attn_w_in, v_attn_rel_bias=v_attn_rel_bias, v_attn_w_out=v_attn_w_out, v_rg_w_in=v_rg_w_in, v_rg_conv_w=v_rg_conv_w, v_rg_conv_b=v_rg_conv_b, v_rg_w_a=v_rg_w_a, v_rg_b_a=v_rg_b_a, v_rg_w_i=v_rg_w_i, v_rg_b_i=v_rg_b_i, v_rg_lambda=v_rg_lambda, v_rg_w_out=v_rg_w_out, v_norm_mix_pre=v_norm_mix_pre, v_norm_mix_post=v_norm_mix_post, v_norm_ffn_pre=v_norm_ffn_pre, v_norm_ffn_post=v_norm_ffn_post, v_ffn_w_gate=v_ffn_w_gate, v_ffn_w_up=v_ffn_w_up, v_ffn_w_down=v_ffn_w_down)
    weights = {n: given[n] for n in TWIN_WEIGHTS}
    shared = {n: given[n] for n in SHARED_INPUTS}
    per_example = {n: given[n] for n in ['x']}
    grad_fn = _jax.value_and_grad(_loss, argnums=(0, 1))

    def one_microbatch(ex, loss_target):
        ex = dict(ex)
        diff = ex.pop(TWIN_DIFF_INPUT)
        return grad_fn(weights, diff, {**shared, **ex}, loss_target)

    if N_MICROBATCH == 1:
        loss, (grad_w, grad_x) = one_microbatch(per_example, given["loss_target"])
    else:
        def body(carry, xs):
            loss_sum, grad_sum = carry
            l_k, (gw_k, gx_k) = one_microbatch(xs[0], xs[1])
            with _jax.named_scope("update"):
                return (loss_sum + l_k, _jax.tree.map(_jnp.add, grad_sum, gw_k)), gx_k

        init = (_jnp.zeros((), _jnp.float32), _jax.tree.map(_jnp.zeros_like, weights))
        (loss, grad_w), grad_x = _jax.lax.scan(body, init, (per_example, given["loss_target"]))
    with _jax.named_scope("update"):
        delta_w, new_m, new_v = {}, {}, {}
        for n in TWIN_WEIGHTS:
            delta_w[n], new_m[n], new_v[n] = _adamw(weights[n], grad_w[n], given["m_" + n], given["v_" + n])
    return (loss, grad_x, *[grad_w[n] for n in TWIN_WEIGHTS], *[delta_w[n] for n in TWIN_WEIGHTS],
            *[new_m[n] for n in TWIN_WEIGHTS], *[new_v[n] for n in TWIN_WEIGHTS])
```

```python
import functools
import math

import jax
import jax.numpy as jnp
from jax import lax
from jax.experimental import pallas as pl
from jax.experimental.pallas import tpu as pltpu

F32 = jnp.float32
BF16 = jnp.bfloat16

N_DEV = 8
DEPTH = 4
CHUNK = 64
N_LEFT = 8
BAND = (N_LEFT + 1) * CHUNK
PAD_KEYS = N_LEFT * CHUNK
HEAD_DIM = 64
N_HEADS = 8
REL_CLIP = 256
LRU_BLOCKS = 4
LRU_C = 8.0
RMS_EPS = 1e-6
QK_SCALE = HEAD_DIM ** -0.5

ADAM_LR = 0.001
ADAM_B1 = 0.9
ADAM_B2 = 0.999
ADAM_EPS = 1e-08
ADAM_WD = 0.01
ADAM_STEP = 10

LANES = 1024
V7X_VMEM_LIMIT = 56 * 1024 * 1024

MESH = pl.DeviceIdType.MESH
ANY = pl.BlockSpec(memory_space=pl.ANY)

SHARDED = (
    ("attn_w_in", 2), ("attn_w_out", 1), ("rg_w_in", 2), ("rg_conv_w", 3), ("rg_conv_b", 1),
    ("rg_w_a", 2), ("rg_b_a", 2), ("rg_w_i", 2), ("rg_b_i", 2), ("rg_lambda", 1), ("rg_w_out", 1),
    ("ffn_w_gate", 2), ("ffn_w_up", 2), ("ffn_w_down", 1),
)
REPLICATED = ("attn_rel_bias", "norm_mix_pre", "norm_mix_post", "norm_ffn_pre", "norm_ffn_post")
WEIGHTS = ("attn_w_in", "attn_rel_bias", "attn_w_out", "rg_w_in", "rg_conv_w", "rg_conv_b", "rg_w_a", "rg_b_a",
           "rg_w_i", "rg_b_i", "rg_lambda", "rg_w_out", "norm_mix_pre", "norm_mix_post", "norm_ffn_pre",
           "norm_ffn_post", "ffn_w_gate", "ffn_w_up", "ffn_w_down")


def _params(*dims):
    return pltpu.CompilerParams(dimension_semantics=dims or None, vmem_limit_bytes=V7X_VMEM_LIMIT)


def _sds(shape, dtype):
    return jax.ShapeDtypeStruct(tuple(shape), dtype)


def _row_tile(n, pref):
    t = min(n, pref)
    assert n % t == 0, (n, pref)
    return t


NN = (((1,), (0,)), ((), ()))
NT = (((1,), (1,)), ((), ()))
TN = (((0,), (0,)), ((), ()))


def _gmm(name, a, b, *, grid, a_blk, a_idx, b_blk, b_idx, o_blk, o_idx, out_shape, out_dtype, dn, acc_shape):
    nk = grid[-1]
    kax = len(grid) - 1

    def body(a_ref, b_ref, o_ref, acc_ref):
        part = lax.dot_general(a_ref[...], b_ref[...], dn, preferred_element_type=F32)
        if nk == 1:
            o_ref[...] = part.astype(o_ref.dtype)
            return
        k = pl.program_id(kax)

        @pl.when(k == 0)
        def _():
            acc_ref[...] = part

        @pl.when(k > 0)
        def _():
            acc_ref[...] += part

        @pl.when(k == nk - 1)
        def _():
            o_ref[...] = acc_ref[...].astype(o_ref.dtype)

    return pl.pallas_call(
        body, grid=grid,
        in_specs=[pl.BlockSpec(a_blk, a_idx), pl.BlockSpec(b_blk, b_idx)],
        out_specs=pl.BlockSpec(o_blk, o_idx),
        out_shape=_sds(out_shape, out_dtype),
        scratch_shapes=[pltpu.VMEM(acc_shape, F32)],
        compiler_params=_params(*(["parallel"] * kax + ["arbitrary"])),
        name=name,
    )(a, b)


def mm_nn(name, a, b, out_dtype, tm=1024, tn=512, tk=1024):
    (m, k), (_, n) = a.shape, b.shape
    tm, tn, tk = _row_tile(m, tm), _row_tile(n, tn), _row_tile(k, tk)
    return _gmm(name, a, b, grid=(m // tm, n // tn, k // tk),
                a_blk=(tm, tk), a_idx=lambda i, j, kk: (i, kk), b_blk=(tk, tn), b_idx=lambda i, j, kk: (kk, j),
                o_blk=(tm, tn), o_idx=lambda i, j, kk: (i, j), out_shape=(m, n), out_dtype=out_dtype, dn=NN,
                acc_shape=(tm, tn))


def mm_nt(name, a, b, out_dtype, tm=1024, tn=512, tk=1024):
    (m, k), (n, _) = a.shape, b.shape
    tm, tn, tk = _row_tile(m, tm), _row_tile(n, tn), _row_tile(k, tk)
    return _gmm(name, a, b, grid=(m // tm, n // tn, k // tk),
                a_blk=(tm, tk), a_idx=lambda i, j, kk: (i, kk), b_blk=(tn, tk), b_idx=lambda i, j, kk: (j, kk),
                o_blk=(tm, tn), o_idx=lambda i, j, kk: (i, j), out_shape=(m, n), out_dtype=out_dtype, dn=NT,
                acc_shape=(tm, tn))


def mm_tn(name, a, b, out_dtype, tm=512, tn=512, tk=1024):
    (k, m), (_, n) = a.shape, b.shape
    tm, tn, tk = _row_tile(m, tm), _row_tile(n, tn), _row_tile(k, tk)
    return _gmm(name, a, b, grid=(m // tm, n // tn, k // tk),
                a_blk=(tk, tm), a_idx=lambda i, j, kk: (kk, i), b_blk=(tk, tn), b_idx=lambda i, j, kk: (kk, j),
                o_blk=(tm, tn), o_idx=lambda i, j, kk: (i, j), out_shape=(m, n), out_dtype=out_dtype, dn=TN,
                acc_shape=(tm, tn))


def rmsnorm_fwd(name, x, g):
    t, d = x.shape
    tr = _row_tile(t, 512)

    def body(x_ref, g_ref, o_ref):
        xv = x_ref[...]
        r = lax.rsqrt(jnp.mean(xv * xv, axis=-1, keepdims=True) + RMS_EPS)
        o_ref[...] = (xv * r * g_ref[...]).astype(o_ref.dtype)

    return pl.pallas_call(
        body, grid=(t // tr,),
        in_specs=[pl.BlockSpec((tr, d), lambda i: (i, 0)), pl.BlockSpec((1, d), lambda i: (0, 0))],
        out_specs=pl.BlockSpec((tr, d), lambda i: (i, 0)),
        out_shape=_sds((t, d), BF16), compiler_params=_params("parallel"), name=name)(x, g)


def resid_norm_fwd(name, x, m, g):
    t, d = x.shape
    tr = _row_tile(t, 512)

    def body(x_ref, m_ref, g_ref, o_ref):
        mv = m_ref[...]
        r = lax.rsqrt(jnp.mean(mv * mv, axis=-1, keepdims=True) + RMS_EPS)
        o_ref[...] = x_ref[...] + mv * r * g_ref[...]

    return pl.pallas_call(
        body, grid=(t // tr,),
        in_specs=[pl.BlockSpec((tr, d), lambda i: (i, 0)), pl.BlockSpec((tr, d), lambda i: (i, 0)),
                  pl.BlockSpec((1, d), lambda i: (0, 0))],
        out_specs=pl.BlockSpec((tr, d), lambda i: (i, 0)),
        out_shape=_sds((t, d), F32), compiler_params=_params("parallel"), name=name)(x, m, g)


def norm_bwd(name, dy, x, g, resid, out_dtype):
    t, d = x.shape
    tr = _row_tile(t, 512)
    has_res = resid is not None

    def body(*refs):
        if has_res:
            dy_ref, x_ref, g_ref, r_ref, dx_ref, dg_ref = refs
        else:
            dy_ref, x_ref, g_ref, dx_ref, dg_ref = refs
        i = pl.program_id(0)
        xv = x_ref[...]
        dyv = dy_ref[...].astype(F32)
        r = lax.rsqrt(jnp.mean(xv * xv, axis=-1, keepdims=True) + RMS_EPS)
        xh = xv * r
        dxh = dyv * g_ref[...]
        dx = r * (dxh - xh * jnp.mean(dxh * xh, axis=-1, keepdims=True))
        if has_res:
            dx = dx + r_ref[...]
        dx_ref[...] = dx.astype(dx_ref.dtype)
        part = jnp.sum(dyv * xh, axis=0, keepdims=True)

        @pl.when(i == 0)
        def _():
            dg_ref[...] = part

        @pl.when(i > 0)
        def _():
            dg_ref[...] += part

    row = pl.BlockSpec((tr, d), lambda i: (i, 0))
    vec = pl.BlockSpec((1, d), lambda i: (0, 0))
    ins = [dy, x, g] + ([resid] if has_res else [])
    return pl.pallas_call(
        body, grid=(t // tr,),
        in_specs=[row, row, vec] + ([row] if has_res else []),
        out_specs=[row, vec],
        out_shape=[_sds((t, d), out_dtype), _sds((1, d), F32)],
        compiler_params=_params("arbitrary"), name=name)(*ins)


def loss_grad(name, y, target):
    t, d = y.shape
    tr = _row_tile(t, 512)

    def body(y_ref, t_ref, dy_ref, s_ref):
        i = pl.program_id(0)
        err = y_ref[...] - t_ref[...]
        dy_ref[...] = err * (1.0 / d)
        part = jnp.sum(err * err, axis=0, keepdims=True)

        @pl.when(i == 0)
        def _():
            s_ref[...] = part

        @pl.when(i > 0)
        def _():
            s_ref[...] += part

    row = pl.BlockSpec((tr, d), lambda i: (i, 0))
    vec = pl.BlockSpec((1, d), lambda i: (0, 0))
    return pl.pallas_call(
        body, grid=(t // tr,), in_specs=[row, row], out_specs=[row, vec],
        out_shape=[_sds((t, d), F32), _sds((1, d), F32)],
        compiler_params=_params("arbitrary"), name=name)(y, target)


A_TQ = 512


def _a_scores(qc, kwin, bias, c):
    s = lax.dot_general(qc, kwin, NT, preferred_element_type=F32) + bias
    col = lax.broadcasted_iota(jnp.int32, s.shape, 1)
    valid = col >= (N_LEFT - c) * CHUNK
    return s, valid


def attn_a_fwd(name, q, kp, vp, bias):
    h, t, dh = q.shape
    tq = _row_tile(t, A_TQ)
    ncs = tq // CHUNK

    def body(q_ref, k_ref, v_ref, b_ref, o_ref, l_ref):
        i = pl.program_id(1)
        bias_h = b_ref[...]

        def chunk(cc, carry):
            c = i * ncs + cc
            r0 = pl.multiple_of(cc * CHUNK, CHUNK)
            k0 = pl.multiple_of(c * CHUNK, CHUNK)
            qc = q_ref[pl.ds(r0, CHUNK), :]
            kwin = k_ref[pl.ds(k0, BAND), :]
            vwin = v_ref[pl.ds(k0, BAND), :]
            s, valid = _a_scores(qc, kwin, bias_h, c)
            s = jnp.where(valid, s, -1e30)
            mx = jnp.max(s, axis=-1, keepdims=True)
            p = jnp.exp(s - mx)
            den = jnp.sum(p, axis=-1, keepdims=True)
            p = p * (1.0 / den)
            o_ref[pl.ds(r0, CHUNK), :] = jnp.dot(p.astype(BF16), vwin, preferred_element_type=F32)
            l_ref[pl.ds(r0, CHUNK), :] = mx + jnp.log(den)
            return carry

        lax.fori_loop(0, ncs, chunk, 0)

    return pl.pallas_call(
        body, grid=(h, t // tq),
        in_specs=[pl.BlockSpec((None, tq, dh), lambda hh, i: (hh, i, 0)),
                  pl.BlockSpec((None, t + PAD_KEYS, dh), lambda hh, i: (hh, 0, 0)),
                  pl.BlockSpec((None, t + PAD_KEYS, dh), lambda hh, i: (hh, 0, 0)),
                  pl.BlockSpec((None, CHUNK, BAND), lambda hh, i: (hh, 0, 0))],
        out_specs=[pl.BlockSpec((None, tq, dh), lambda hh, i: (hh, i, 0)),
                   pl.BlockSpec((None, tq, 1), lambda hh, i: (hh, i, 0))],
        out_shape=[_sds((h, t, dh), F32), _sds((h, t, 1), F32)],
        compiler_params=_params("parallel", "parallel"), name=name)(q, kp, vp, bias)


def attn_a_bwd(name, q, kp, vp, bias, o, lse, do):
    h, t, dh = q.shape
    tq = _row_tile(t, A_TQ)
    ncs = tq // CHUNK

    def body(q_ref, k_ref, v_ref, b_ref, o_ref, l_ref, do_ref, dq_ref, dk_ref, dv_ref, db_ref):
        i = pl.program_id(1)

        @pl.when(i == 0)
        def _():
            dk_ref[...] = jnp.zeros_like(dk_ref)
            dv_ref[...] = jnp.zeros_like(dv_ref)
            db_ref[...] = jnp.zeros_like(db_ref)

        bias_h = b_ref[...]

        def chunk(cc, carry):
            c = i * ncs + cc
            r0 = pl.multiple_of(cc * CHUNK, CHUNK)
            k0 = pl.multiple_of(c * CHUNK, CHUNK)
            qc = q_ref[pl.ds(r0, CHUNK), :]
            doc = do_ref[pl.ds(r0, CHUNK), :]
            kwin = k_ref[pl.ds(k0, BAND), :]
            vwin = v_ref[pl.ds(k0, BAND), :]
            s, valid = _a_scores(qc, kwin, bias_h, c)
            p = jnp.where(valid, jnp.exp(s - l_ref[pl.ds(r0, CHUNK), :]), 0.0)
            dp = lax.dot_general(doc, vwin, NT, preferred_element_type=F32)
            delta = jnp.sum(doc.astype(F32) * o_ref[pl.ds(r0, CHUNK), :], axis=-1, keepdims=True)
            ds = p * (dp - delta)
            db_ref[...] += ds
            dsb = ds.astype(BF16)
            dq_ref[pl.ds(r0, CHUNK), :] = jnp.dot(dsb, kwin, preferred_element_type=F32) * QK_SCALE
            dk_ref[pl.ds(k0, BAND), :] += lax.dot_general(dsb, qc, TN, preferred_element_type=F32)
            dv_ref[pl.ds(k0, BAND), :] += lax.dot_general(p.astype(BF16), doc, TN, preferred_element_type=F32)
            return carry

        lax.fori_loop(0, ncs, chunk, 0)

    tile = lambda w: pl.BlockSpec((None, tq, w), lambda hh, i: (hh, i, 0))
    whole = pl.BlockSpec((None, t + PAD_KEYS, dh), lambda hh, i: (hh, 0, 0))
    bspec = pl.BlockSpec((None, CHUNK, BAND), lambda hh, i: (hh, 0, 0))
    return pl.pallas_call(
        body, grid=(h, t // tq),
        in_specs=[tile(dh), whole, whole, bspec, tile(dh), tile(1), tile(dh)],
        out_specs=[tile(dh), whole, whole, bspec],
        out_shape=[_sds((h, t, dh), F32), _sds((h, t + PAD_KEYS, dh), F32), _sds((h, t + PAD_KEYS, dh), F32),
                   _sds((h, CHUNK, BAND), F32)],
        compiler_params=_params("parallel", "arbitrary"), name=name)(q, kp, vp, bias, o, lse, do)


SB_TQ = 256
SB_TK = 128


def _tri(strict):
    j = lax.broadcasted_iota(jnp.int32, (SB_TK, SB_TK), 0)
    s = lax.broadcasted_iota(jnp.int32, (SB_TK, SB_TK), 1)
    return jnp.where((j > s) if strict else (j >= s), 1.0, 0.0).astype(BF16)


def _suffix_sum(x, tri):
    hi = x.astype(BF16)
    lo = (x - hi.astype(F32)).astype(BF16)
    return jnp.dot(hi, tri, preferred_element_type=F32) + jnp.dot(lo, tri, preferred_element_type=F32)


def _sb_block(qv, ks, row, kb, cm, tri_strict):
    z = lax.dot_general(qv, ks, NT, preferred_element_type=F32)
    col = kb * SB_TK + lax.broadcasted_iota(jnp.int32, z.shape, 1)
    causal = col < row
    lb = jnp.minimum(z, 0.0) - jnp.log(1.0 + jnp.exp(-jnp.abs(z)))
    m = jnp.where(causal, lb - z, 0.0)
    after = _suffix_sum(m, tri_strict) + cm
    w = jnp.where(causal, jnp.exp(lb + after), 0.0)
    return causal, lb, m, w


def sb_fwd(name, q, k, v):
    h, t, dh = q.shape
    tq = _row_tile(t, SB_TQ)
    per = tq // SB_TK

    def body(q_ref, k_ref, v_ref, o_ref):
        i = pl.program_id(1)
        qv = q_ref[...]
        row = i * tq + lax.broadcasted_iota(jnp.int32, (tq, 1), 0)
        tri = _tri(True)
        nkb = (i + 1) * per

        def step(n, carry):
            acc, cm = carry
            kb = nkb - 1 - n
            k0 = pl.multiple_of(kb * SB_TK, SB_TK)
            ks = k_ref[pl.ds(k0, SB_TK), :]
            vs = v_ref[pl.ds(k0, SB_TK), :]
            _, _, m, w = _sb_block(qv, ks, row, kb, cm, tri)
            acc = acc + jnp.dot(w.astype(BF16), vs, preferred_element_type=F32)
            return acc, cm + jnp.sum(m, axis=-1, keepdims=True)

        acc, _ = lax.fori_loop(0, nkb, step, (jnp.zeros((tq, dh), F32), jnp.zeros((tq, 1), F32)))
        o_ref[...] = acc

    tile = pl.BlockSpec((None, tq, dh), lambda hh, i: (hh, i, 0))
    whole = pl.BlockSpec((None, t, dh), lambda hh, i: (hh, 0, 0))
    return pl.pallas_call(
        body, grid=(h, t // tq), in_specs=[tile, whole, whole], out_specs=tile,
        out_shape=_sds((h, t, dh), F32), compiler_params=_params("parallel", "parallel"), name=name)(q, k, v)


def sb_bwd(name, q, k, v, o, do):
    h, t, dh = q.shape
    tq = _row_tile(t, SB_TQ)
    per = tq // SB_TK

    def body(q_ref, k_ref, v_ref, o_ref, do_ref, dq_ref, dk_ref, dv_ref):
        i = pl.program_id(1)

        @pl.when(i == 0)
        def _():
            dk_ref[...] = jnp.zeros_like(dk_ref)
            dv_ref[...] = jnp.zeros_like(dv_ref)

        qv = q_ref[...]
        dov = do_ref[...]
        delta = jnp.sum(dov.astype(F32) * o_ref[...], axis=-1, keepdims=True)
        row = i * tq + lax.broadcasted_iota(jnp.int32, (tq, 1), 0)
        tri_s, tri_i = _tri(True), _tri(False)
        nkb = (i + 1) * per

        def step(n, carry):
            dq, cm, ce = carry
            kb = nkb - 1 - n
            k0 = pl.multiple_of(kb * SB_TK, SB_TK)
            ks = k_ref[pl.ds(k0, SB_TK), :]
            vs = v_ref[pl.ds(k0, SB_TK), :]
            causal, lb, m, w = _sb_block(qv, ks, row, kb, cm, tri_s)
            dw = lax.dot_general(dov, vs, NT, preferred_element_type=F32)
            wb = w.astype(BF16)
            e = wb.astype(F32) * dw
            left = delta - (_suffix_sum(e, tri_i) + ce)
            sig = jnp.exp(lb)
            dz = jnp.where(causal, e * (1.0 - sig) - left * sig, 0.0).astype(BF16)
            dq = dq + jnp.dot(dz, ks, preferred_element_type=F32)
            dk_ref[pl.ds(k0, SB_TK), :] += lax.dot_general(dz, qv, TN, preferred_element_type=F32)
            dv_ref[pl.ds(k0, SB_TK), :] += lax.dot_general(wb, dov, TN, preferred_element_type=F32)
            return dq, cm + jnp.sum(m, axis=-1, keepdims=True), ce + jnp.sum(e, axis=-1, keepdims=True)

        zero = jnp.zeros((tq, 1), F32)
        dq, _, _ = lax.fori_loop(0, nkb, step, (jnp.zeros((tq, dh), F32), zero, zero))
        dq_ref[...] = dq * QK_SCALE

    tile = pl.BlockSpec((None, tq, dh), lambda hh, i: (hh, i, 0))
    whole = pl.BlockSpec((None, t, dh), lambda hh, i: (hh, 0, 0))
    return pl.pallas_call(
        body, grid=(h, t // tq), in_specs=[tile, whole, whole, tile, tile], out_specs=[tile, whole, whole],
        out_shape=[_sds((h, t, dh), F32)] * 3,
        compiler_params=_params("parallel", "arbitrary"), name=name)(q, k, v, o, do)


def _sigmoid(x):
    return 1.0 / (1.0 + jnp.exp(-x))


def ffn_up(name, h, wg, wu):
    t, d = h.shape
    nb, _, f8 = wg.shape
    tm = _row_tile(t, 1024)

    def body(h_ref, wg_ref, wu_ref, g_ref, u_ref, a_ref):
        hv = h_ref[...]
        g = jnp.dot(hv, wg_ref[...], preferred_element_type=F32)
        u = jnp.dot(hv, wu_ref[...], preferred_element_type=F32)
        g_ref[...] = g.astype(BF16)
        u_ref[...] = u.astype(BF16)
        a_ref[...] = (g * _sigmoid(g) * u).astype(BF16)

    wspec = pl.BlockSpec((None, d, f8), lambda i, k: (k, 0, 0))
    ospec = pl.BlockSpec((None, tm, f8), lambda i, k: (k, i, 0))
    return pl.pallas_call(
        body, grid=(t // tm, nb), in_specs=[pl.BlockSpec((tm, d), lambda i, k: (i, 0)), wspec, wspec],
        out_specs=[ospec] * 3, out_shape=[_sds((nb, t, f8), BF16)] * 3,
        compiler_params=_params("parallel", "parallel"), name=name)(h, wg, wu)


def ffn_down(name, a, wd):
    nb, t, f8 = a.shape
    d = wd.shape[2]
    tm = _row_tile(t, 1024)
    return _gmm(name, a, wd, grid=(t // tm, nb),
                a_blk=(None, tm, f8), a_idx=lambda i, k: (k, i, 0), b_blk=(None, f8, d), b_idx=lambda i, k: (k, 0, 0),
                o_blk=(tm, d), o_idx=lambda i, k: (i, 0), out_shape=(t, d), out_dtype=F32, dn=NN, acc_shape=(tm, d))


def ffn_bwd_act(name, dm, wd, g, u):
    t, d = dm.shape
    nb, f8, _ = wd.shape
    tm = _row_tile(t, 1024)

    def body(dm_ref, wd_ref, g_ref, u_ref, dg_ref, du_ref):
        da = lax.dot_general(dm_ref[...], wd_ref[...], NT, preferred_element_type=F32)
        gv = g_ref[...].astype(F32)
        uv = u_ref[...].astype(F32)
        sg = _sigmoid(gv)
        dg_ref[...] = (da * uv * sg * (1.0 + gv * (1.0 - sg))).astype(BF16)
        du_ref[...] = (da * gv * sg).astype(BF16)

    bspec = pl.BlockSpec((None, tm, f8), lambda i, k: (k, i, 0))
    return pl.pallas_call(
        body, grid=(t // tm, nb),
        in_specs=[pl.BlockSpec((tm, d), lambda i, k: (i, 0)), pl.BlockSpec((None, f8, d), lambda i, k: (k, 0, 0)),
                  bspec, bspec],
        out_specs=[bspec] * 2, out_shape=[_sds((nb, t, f8), BF16)] * 2,
        compiler_params=_params("parallel", "parallel"), name=name)(dm, wd, g, u)


def ffn_bwd_dh(name, dg, du, wg, wu):
    nb, t, f8 = dg.shape
    d = wg.shape[1]
    tm = _row_tile(t, 1024)

    def body(dg_ref, du_ref, wg_ref, wu_ref, o_ref):
        k = pl.program_id(1)
        part = (lax.dot_general(dg_ref[...], wg_ref[...], NT, preferred_element_type=F32)
                + lax.dot_general(du_ref[...], wu_ref[...], NT, preferred_element_type=F32))

        @pl.when(k == 0)
        def _():
            o_ref[...] = part

        @pl.when(k > 0)
        def _():
            o_ref[...] += part

    bspec = pl.BlockSpec((None, tm, f8), lambda i, k: (k, i, 0))
    wspec = pl.BlockSpec((None, d, f8), lambda i, k: (k, 0, 0))
    return pl.pallas_call(
        body, grid=(t // tm, nb), in_specs=[bspec, bspec, wspec, wspec],
        out_specs=pl.BlockSpec((tm, d), lambda i, k: (i, 0)), out_shape=_sds((t, d), F32),
        compiler_params=_params("parallel", "arbitrary"), name=name)(dg, du, wg, wu)


def ffn_dw_in(name, h, dact):
    t, d = h.shape
    nb, _, f8 = dact.shape
    tk = _row_tile(t, 1024)
    return _gmm(name, h, dact, grid=(nb, t // tk),
                a_blk=(tk, d), a_idx=lambda b, s: (s, 0), b_blk=(None, tk, f8), b_idx=lambda b, s: (b, s, 0),
                o_blk=(None, d, f8), o_idx=lambda b, s: (b, 0, 0), out_shape=(nb, d, f8), out_dtype=F32, dn=TN,
                acc_shape=(d, f8))


def ffn_dw_down(name, a, dm):
    nb, t, f8 = a.shape
    d = dm.shape[1]
    tk = _row_tile(t, 1024)
    return _gmm(name, a, dm, grid=(nb, t // tk),
                a_blk=(None, tk, f8), a_idx=lambda b, s: (b, s, 0), b_blk=(tk, d), b_idx=lambda b, s: (s, 0),
                o_blk=(None, f8, d), o_idx=lambda b, s: (b, 0, 0), out_shape=(nb, f8, d), out_dtype=F32, dn=TN,
                acc_shape=(f8, d))


GELU_C = math.sqrt(2.0 / math.pi)
GELU_A = 0.044715


def _gelu(x):
    return 0.5 * x * (1.0 + jnp.tanh(GELU_C * (x + GELU_A * x * x * x)))


def _gelu_grad(x):
    th = jnp.tanh(GELU_C * (x + GELU_A * x * x * x))
    return 0.5 * (1.0 + th) + 0.5 * x * (1.0 - th * th) * GELU_C * (1.0 + 3.0 * GELU_A * x * x)


def _neg_expm1(x):
    series = x * (1.0 + x * (0.5 + x * (1.0 / 6.0 + x * (1.0 / 24.0 + x * (1.0 / 120.0 + x * (1.0 / 720.0))))))
    return -jnp.where(x > -0.25, series, jnp.exp(x) - 1.0)


def fir4(name, xs, w, b):
    nj, t, c = xs.shape
    tr = _row_tile(t, 256)

    def body(xs_ref, w_ref, b_ref, o_ref):
        acc = b_ref[...] + w_ref[0:1, :] * xs_ref[0]
        for j in range(1, nj):
            acc = acc + w_ref[j:j + 1, :] * xs_ref[j]
        o_ref[...] = acc

    return pl.pallas_call(
        body, grid=(t // tr,),
        in_specs=[pl.BlockSpec((nj, tr, c), lambda i: (0, i, 0)), pl.BlockSpec((nj, c), lambda i: (0, 0)),
                  pl.BlockSpec((1, c), lambda i: (0, 0))],
        out_specs=pl.BlockSpec((tr, c), lambda i: (i, 0)), out_shape=_sds((t, c), F32),
        compiler_params=_params("parallel"), name=name)(xs, w, b)


def fir4_bwd(name, xs, dy):
    nj, t, c = xs.shape
    tr = _row_tile(t, 256)

    def body(xs_ref, dy_ref, dw_ref, db_ref):
        i = pl.program_id(0)

        @pl.when(i == 0)
        def _():
            dw_ref[...] = jnp.zeros_like(dw_ref)
            db_ref[...] = jnp.zeros_like(db_ref)

        dyv = dy_ref[...]
        db_ref[...] += jnp.sum(dyv, axis=0, keepdims=True)
        for j in range(nj):
            dw_ref[j:j + 1, :] += jnp.sum(dyv * xs_ref[j], axis=0, keepdims=True)

    return pl.pallas_call(
        body, grid=(t // tr,),
        in_specs=[pl.BlockSpec((nj, tr, c), lambda i: (0, i, 0)), pl.BlockSpec((tr, c), lambda i: (i, 0))],
        out_specs=[pl.BlockSpec((nj, c), lambda i: (0, 0)), pl.BlockSpec((1, c), lambda i: (0, 0))],
        out_shape=[_sds((nj, c), F32), _sds((1, c), F32)],
        compiler_params=_params("arbitrary"), name=name)(xs, dy)


def _rg_gate_values(xcv, wa_ref, wi_ref, ba_ref, bi_ref, lam_ref):
    xb = xcv.astype(BF16)
    r = _sigmoid(jnp.dot(xb, wa_ref[...], preferred_element_type=F32) + ba_ref[...])
    ig = _sigmoid(jnp.dot(xb, wi_ref[...], preferred_element_type=F32) + bi_ref[...])
    lam = lam_ref[...]
    sp = jnp.maximum(-lam, 0.0) + jnp.log(1.0 + jnp.exp(-jnp.abs(lam)))
    log_a = -LRU_C * r * sp
    a = jnp.exp(log_a)
    mult = jnp.sqrt(_neg_expm1(2.0 * log_a))
    return xb, r, ig, sp, a, mult


def rg_gates_fwd(name, xc, wa, wi, ba, bi, lam):
    t, c = xc.shape
    nb, cb, _ = wa.shape
    tm = _row_tile(t, 512)

    def body(xc_ref, wa_ref, wi_ref, ba_ref, bi_ref, lam_ref, a_ref, u_ref):
        xcv = xc_ref[...]
        _, _, ig, _, a, mult = _rg_gate_values(xcv, wa_ref, wi_ref, ba_ref, bi_ref, lam_ref)
        a_ref[...] = a
        u_ref[...] = mult * (ig * xcv)

    blk = pl.BlockSpec((tm, cb), lambda i, n: (i, n))
    wsp = pl.BlockSpec((None, cb, cb), lambda i, n: (n, 0, 0))
    vec = pl.BlockSpec((1, cb), lambda i, n: (0, n))
    return pl.pallas_call(
        body, grid=(t // tm, nb), in_specs=[blk, wsp, wsp, vec, vec, vec], out_specs=[blk, blk],
        out_shape=[_sds((t, c), F32)] * 2, compiler_params=_params("parallel", "parallel"), name=name,
    )(xc, wa, wi, ba, bi, lam)


def rg_gates_bwd(name, xc, gu, hprev, wa, wi, ba, bi, lam):
    t, c = xc.shape
    nb, cb, _ = wa.shape
    tm = _row_tile(t, 512)

    def body(xc_ref, gu_ref, hp_ref, wa_ref, wi_ref, ba_ref, bi_ref, lam_ref,
             dxc_ref, dwa_ref, dwi_ref, dba_ref, dbi_ref, dlam_ref):
        i = pl.program_id(1)

        @pl.when(i == 0)
        def _():
            for ref in (dwa_ref, dwi_ref, dba_ref, dbi_ref, dlam_ref):
                ref[...] = jnp.zeros_like(ref)

        xcv = xc_ref[...]
        xb, r, ig, sp, a, mult = _rg_gate_values(xcv, wa_ref, wi_ref, ba_ref, bi_ref, lam_ref)
        gv = gu_ref[...]
        d_ixc = gv * mult
        d_i = d_ixc * xcv
        d_mult = gv * ig * xcv
        d_a = gv * hp_ref[...] - d_mult * a / mult
        d_log_a = d_a * a
        d_r = d_log_a * (-LRU_C * sp)
        sig_neg_lam = 1.0 / (1.0 + jnp.exp(lam_ref[...]))
        dlam_ref[...] += jnp.sum(d_log_a * r, axis=0, keepdims=True) * (LRU_C * sig_neg_lam)
        dpa = d_r * r * (1.0 - r)
        dpi = d_i * ig * (1.0 - ig)
        dba_ref[...] += jnp.sum(dpa, axis=0, keepdims=True)
        dbi_ref[...] += jnp.sum(dpi, axis=0, keepdims=True)
        dpab, dpib = dpa.astype(BF16), dpi.astype(BF16)
        dxc_ref[...] = (d_ixc * ig + lax.dot_general(dpab, wa_ref[...], NT, preferred_element_type=F32)
                        + lax.dot_general(dpib, wi_ref[...], NT, preferred_element_type=F32))
        dwa_ref[...] += lax.dot_general(xb, dpab, TN, preferred_element_type=F32)
        dwi_ref[...] += lax.dot_general(xb, dpib, TN, preferred_element_type=F32)

    blk = pl.BlockSpec((tm, cb), lambda n, i: (i, n))
    wsp = pl.BlockSpec((None, cb, cb), lambda n, i: (n, 0, 0))
    vec = pl.BlockSpec((1, cb), lambda n, i: (0, n))
    return pl.pallas_call(
        body, grid=(nb, t // tm), in_specs=[blk, blk, blk, wsp, wsp, vec, vec, vec],
        out_specs=[blk, wsp, wsp, vec, vec, vec],
        out_shape=[_sds((t, c), F32), _sds((nb, cb, cb), F32), _sds((nb, cb, cb), F32),
                   _sds((1, c), F32), _sds((1, c), F32), _sds((1, c), F32)],
        compiler_params=_params("parallel", "arbitrary"), name=name)(xc, gu, hprev, wa, wi, ba, bi, lam)


SCAN_TS = 256
SCAN_TC = 512


def _tile_scan(a, b, reverse):
    ts = a.shape[0]
    row = lax.broadcasted_iota(jnp.int32, a.shape, 0)
    d = 1
    while d < ts:
        if reverse:
            inside = row < ts - d
            a_sh = jnp.where(inside, pltpu.roll(a, ts - d, 0), 1.0)
            b_sh = jnp.where(inside, pltpu.roll(b, ts - d, 0), 0.0)
        else:
            inside = row >= d
            a_sh = jnp.where(inside, pltpu.roll(a, d, 0), 1.0)
            b_sh = jnp.where(inside, pltpu.roll(b, d, 0), 0.0)
        b = b + a * b_sh
        a = a * a_sh
        d *= 2
    return a, b


def rg_scan_fwd(name, a, u, gate_pre):
    t, c = a.shape
    ts, tc = _row_tile(t, SCAN_TS), _row_tile(c, SCAN_TC)

    def body(a_ref, u_ref, g_ref, h_ref, z_ref, carry_ref):
        s = pl.program_id(1)

        @pl.when(s == 0)
        def _():
            carry_ref[...] = jnp.zeros_like(carry_ref)

        ac, bc = _tile_scan(a_ref[...], u_ref[...], False)
        h = bc + ac * carry_ref[0:1, :]
        h_ref[...] = h
        z_ref[...] = (h * _gelu(g_ref[...])).astype(BF16)
        carry_ref[0:1, :] = h[ts - 1:ts, :]

    blk = pl.BlockSpec((ts, tc), lambda j, s: (s, j))
    return pl.pallas_call(
        body, grid=(c // tc, t // ts), in_specs=[blk, blk, blk], out_specs=[blk, blk],
        out_shape=[_sds((t, c), F32), _sds((t, c), BF16)], scratch_shapes=[pltpu.VMEM((8, tc), F32)],
        compiler_params=_params("parallel", "arbitrary"), name=name)(a, u, gate_pre)


def rg_scan_bwd(name, a_next, hs, gate_pre, dz):
    t, c = hs.shape
    ts, tc = _row_tile(t, SCAN_TS), _row_tile(c, SCAN_TC)
    nt = t // ts

    def body(an_ref, h_ref, g_ref, dz_ref, gu_ref, dgate_ref, carry_ref):
        s = pl.program_id(1)

        @pl.when(s == 0)
        def _():
            carry_ref[...] = jnp.zeros_like(carry_ref)

        gate = g_ref[...]
        dzv = dz_ref[...]
        dgate_ref[...] = (dzv * h_ref[...] * _gelu_grad(gate)).astype(BF16)
        ac, bc = _tile_scan(an_ref[...], dzv * _gelu(gate), True)
        gu = bc + ac * carry_ref[0:1, :]
        gu_ref[...] = gu
        carry_ref[0:1, :] = gu[0:1, :]

    blk = pl.BlockSpec((ts, tc), lambda j, s: (nt - 1 - s, j))
    return pl.pallas_call(
        body, grid=(c // tc, nt), in_specs=[blk, blk, blk, blk], out_specs=[blk, blk],
        out_shape=[_sds((t, c), F32), _sds((t, c), BF16)], scratch_shapes=[pltpu.VMEM((8, tc), F32)],
        compiler_params=_params("parallel", "arbitrary"), name=name)(a_next, hs, gate_pre, dz)


def _shift_down(x, k):
    return jnp.pad(x, ((k, 0), (0, 0)))[:x.shape[0]] if k else x


def _shift_up(x, k):
    return jnp.pad(x, ((0, k), (0, 0)))[k:] if k else x


def _heads(x2d, n):
    t = x2d.shape[0]
    return x2d.reshape(t, n, N_HEADS, HEAD_DIM).transpose(1, 2, 0, 3)


def _unheads(x):
    n, h, t, d = x.shape
    return x.transpose(2, 0, 1, 3).reshape(t, n * h * d)


def _rel_index():
    qpos = N_LEFT * CHUNK + jnp.arange(CHUNK)
    kpos = jnp.arange(BAND)
    return jnp.clip(qpos[:, None] - kpos[None, :], -REL_CLIP, REL_CLIP) + REL_CLIP


def attn_layer_fwd(tag, x, w):
    h = rmsnorm_fwd(tag + "_norm", x, w["g_pre"])
    proj = mm_nn(tag + "_proj", h, w["w_in"], BF16)
    qkv = _heads(proj, 6)
    pad = lambda a: jnp.pad(a, ((0, 0), (PAD_KEYS, 0), (0, 0)))
    qa, kap, vap = qkv[0] * QK_SCALE, pad(qkv[1]), pad(qkv[2])
    qs, ks, vs = qkv[3] * QK_SCALE, qkv[4], qkv[5]
    bias = w["rel_bias"][:, _rel_index()]
    oa, lse = attn_a_fwd(tag + "_a", qa, kap, vap, bias)
    ob = sb_fwd(tag + "_sb", qs, ks, vs)
    o = _unheads(jnp.stack([oa, ob])).astype(BF16)
    m = mm_nn(tag + "_out", o, w["w_out"], F32)
    x1 = resid_norm_fwd(tag + "_res", x, m, w["g_post"])
    return x1, (x, h, qa, kap, vap, qs, ks, vs, bias, oa, lse, ob, o, m)


def attn_layer_bwd(tag, dx1, saved, w):
    x, h, qa, kap, vap, qs, ks, vs, bias, oa, lse, ob, o, m = saved
    dm, dg_post = norm_bwd(tag + "_dpost", dx1, m, w["g_post"], None, BF16)
    d_w_out = mm_tn(tag + "_dwout", o, dm, F32)
    do = _heads(mm_nt(tag + "_do", dm, w["w_out"], BF16), 2)
    dqa, dkap, dvap, dbias = attn_a_bwd(tag + "_da", qa, kap, vap, bias, oa, lse, do[0])
    dqs, dks, dvs = sb_bwd(tag + "_dsb", qs, ks, vs, ob, do[1])
    d_rel = jnp.zeros_like(w["rel_bias"]).at[:, _rel_index()].add(dbias)
    dproj = _unheads(jnp.stack([dqa, dkap[:, PAD_KEYS:], dvap[:, PAD_KEYS:], dqs, dks, dvs])).astype(BF16)
    d_w_in = mm_tn(tag + "_dwin", h, dproj, F32)
    dh = mm_nt(tag + "_dh", dproj, w["w_in"], F32)
    dx, dg_pre = norm_bwd(tag + "_dpre", dh, x, w["g_pre"], dx1, F32)
    return dx, dict(w_in=d_w_in, w_out=d_w_out, rel_bias=d_rel, g_pre=dg_pre, g_post=dg_post)


def rg_layer_fwd(tag, x, w):
    c = w["w_out"].shape[0]
    h = rmsnorm_fwd(tag + "_norm", x, w["g_pre"])
    proj = mm_nn(tag + "_proj", h, w["w_in"], F32)
    gate_pre, xr = proj[:, :c], proj[:, c:]
    xs = jnp.stack([_shift_down(xr, 3 - j) for j in range(4)])
    xc = fir4(tag + "_conv", xs, w["conv_w"], w["conv_b"])
    a, u = rg_gates_fwd(tag + "_gates", xc, w["w_a"], w["w_i"], w["b_a"], w["b_i"], w["lam"])
    hs, z = rg_scan_fwd(tag + "_scan", a, u, gate_pre)
    m = mm_nn(tag + "_out", z, w["w_out"], F32)
    x1 = resid_norm_fwd(tag + "_res", x, m, w["g_post"])
    return x1, (x, h, gate_pre, xs, xc, a, hs, z, m)


def rg_layer_bwd(tag, dx1, saved, w):
    x, h, gate_pre, xs, xc, a, hs, z, m = saved
    dm, dg_post = norm_bwd(tag + "_dpost", dx1, m, w["g_post"], None, BF16)
    d_w_out = mm_tn(tag + "_dwout", z, dm, F32)
    dz = mm_nt(tag + "_dz", dm, w["w_out"], F32)
    gu, dgate = rg_scan_bwd(tag + "_dscan", _shift_up(a, 1), hs, gate_pre, dz)
    dxc, d_w_a, d_w_i, d_b_a, d_b_i, d_lam = rg_gates_bwd(
        tag + "_dgates", xc, gu, _shift_down(hs, 1), w["w_a"], w["w_i"], w["b_a"], w["b_i"], w["lam"])
    d_conv_w, d_conv_b = fir4_bwd(tag + "_dconvw", xs, dxc)
    dxs = jnp.stack([_shift_up(dxc, 3 - j) for j in range(4)])
    dxr = fir4(tag + "_dconv", dxs, w["conv_w"], jnp.zeros_like(w["conv_b"]))
    dproj = jnp.concatenate([dgate, dxr.astype(BF16)], axis=1)
    d_w_in = mm_tn(tag + "_dwin", h, dproj, F32)
    dh = mm_nt(tag + "_dh", dproj, w["w_in"], F32)
    dx, dg_pre = norm_bwd(tag + "_dpre", dh, x, w["g_pre"], dx1, F32)
    return dx, dict(w_in=d_w_in, w_out=d_w_out, conv_w=d_conv_w, conv_b=d_conv_b, w_a=d_w_a, w_i=d_w_i,
                    b_a=d_b_a, b_i=d_b_i, lam=d_lam, g_pre=dg_pre, g_post=dg_post)


def ffn_layer_fwd(tag, x, w):
    h = rmsnorm_fwd(tag + "_norm", x, w["g_pre"])
    g, u, a = ffn_up(tag + "_up", h, w["w_gate"], w["w_up"])
    f = ffn_down(tag + "_down", a, w["w_down"])
    x1 = resid_norm_fwd(tag + "_res", x, f, w["g_post"])
    return x1, (x, h, g, u, a, f)


def ffn_layer_bwd(tag, dx1, saved, w):
    x, h, g, u, a, f = saved
    dm, dg_post = norm_bwd(tag + "_dpost", dx1, f, w["g_post"], None, BF16)
    d_w_down = ffn_dw_down(tag + "_dwdown", a, dm)
    dg, du = ffn_bwd_act(tag + "_dact", dm, w["w_down"], g, u)
    d_w_gate = ffn_dw_in(tag + "_dwgate", h, dg)
    d_w_up = ffn_dw_in(tag + "_dwup", h, du)
    dh = ffn_bwd_dh(tag + "_dh", dg, du, w["w_gate"], w["w_up"])
    dx, dg_pre = norm_bwd(tag + "_dpre", dh, x, w["g_pre"], dx1, F32)
    return dx, dict(w_gate=d_w_gate, w_up=d_w_up, w_down=d_w_down, g_pre=dg_pre, g_post=dg_post)


def _place():
    return lax.axis_index("x"), lax.axis_index("y"), lax.axis_index("c")


def all_gather(name, blk):
    r, c = blk.shape

    def body(x_ref, out_ref, send_sems, recv_sems, local_sem):
        x, y, cc = _place()
        me, sibling = (x, y, cc), (x, y, 1 - cc)
        chips = [(1 - x, y), (x, 1 - y), (1 - x, 1 - y)]

        def slot(px, py, pc):
            return out_ref.at[4 * px + 2 * py + pc]

        def copy(k, block, to, src=None):
            return pltpu.make_async_remote_copy(
                src_ref=slot(*block) if src is None else src, dst_ref=slot(*block),
                send_sem=send_sems.at[k], recv_sem=recv_sems.at[k], device_id=to, device_id_type=MESH)

        mine = pltpu.make_async_copy(x_ref, slot(*me), local_sem)
        mine.start()
        first = [copy(0, me, sibling, src=x_ref)]
        first += [copy(1 + j, me, (*chip, cc), src=x_ref) for j, chip in enumerate(chips)]
        for cp in first:
            cp.start()
        passed = [copy(4 + j, (*chip, cc), sibling) for j, chip in enumerate(chips)]
        for j, chip in enumerate(chips):
            copy(1 + j, (*chip, cc), me).wait_recv()
            passed[j].start()
        copy(0, sibling, me).wait_recv()
        for j, chip in enumerate(chips):
            copy(4 + j, (*chip, 1 - cc), me).wait_recv()
        for cp in first + passed:
            cp.wait_send()
        mine.wait()

    return pl.pallas_call(
        body, out_shape=_sds((N_DEV, r, c), blk.dtype), in_specs=[ANY], out_specs=ANY,
        scratch_shapes=[pltpu.SemaphoreType.DMA((7,)), pltpu.SemaphoreType.DMA((7,)), pltpu.SemaphoreType.DMA],
        name=name)(blk)


def exchange_pair(name, g):
    nchip, _, r, c = g.shape

    def body(g_ref, land_ref, send_sems, recv_sems):
        x, y, cc = _place()
        copies = [pltpu.make_async_remote_copy(
            src_ref=g_ref.at[j, 1 - cc], dst_ref=land_ref.at[j], send_sem=send_sems.at[j], recv_sem=recv_sems.at[j],
            device_id=(x, y, 1 - cc), device_id_type=MESH) for j in range(nchip)]
        for cp in copies:
            cp.start()
        for cp in copies:
            cp.wait()

    return pl.pallas_call(
        body, out_shape=_sds((nchip, r, c), g.dtype), in_specs=[ANY], out_specs=ANY,
        scratch_shapes=[pltpu.SemaphoreType.DMA((nchip,)), pltpu.SemaphoreType.DMA((nchip,))], name=name)(g)


def pair_sum(name, g, land, core):
    nchip, _, r, c = g.shape
    tr = _row_tile(r, 128)

    def body(core_ref, g_ref, l_ref, o_ref):
        o_ref[...] = g_ref[...] + l_ref[...]

    return pl.pallas_call(
        body,
        grid_spec=pltpu.PrefetchScalarGridSpec(
            num_scalar_prefetch=1, grid=(nchip, r // tr),
            in_specs=[pl.BlockSpec((None, None, tr, c), lambda j, i, core_ref: (j, core_ref[0], i, 0)),
                      pl.BlockSpec((None, tr, c), lambda j, i, core_ref: (j, i, 0))],
            out_specs=pl.BlockSpec((None, tr, c), lambda j, i, core_ref: (j, i, 0))),
        out_shape=_sds((nchip, r, c), g.dtype), compiler_params=_params("parallel", "parallel"), name=name,
    )(core, g, land)


def exchange_chips(name, p):
    nchip, r, c = p.shape

    def body(p_ref, land_ref, send_sems, recv_sems, local_sem):
        x, y, cc = _place()
        mine = 2 * x + y
        chips = [(1 - x, y), (x, 1 - y), (1 - x, 1 - y)]
        own = pltpu.make_async_copy(p_ref.at[mine], land_ref.at[mine], local_sem)
        own.start()
        sends = [pltpu.make_async_remote_copy(
            src_ref=p_ref.at[2 * px + py], dst_ref=land_ref.at[mine], send_sem=send_sems.at[k],
            recv_sem=recv_sems.at[k], device_id=(px, py, cc), device_id_type=MESH)
            for k, (px, py) in enumerate(chips)]
        for cp in sends:
            cp.start()
        for k, (px, py) in enumerate(chips):
            pltpu.make_async_remote_copy(
                src_ref=p_ref.at[mine], dst_ref=land_ref.at[2 * px + py], send_sem=send_sems.at[k],
                recv_sem=recv_sems.at[k], device_id=(px, py, cc), device_id_type=MESH).wait_recv()
        for cp in sends:
            cp.wait_send()
        own.wait()

    return pl.pallas_call(
        body, out_shape=_sds((nchip, r, c), p.dtype), in_specs=[ANY], out_specs=ANY,
        scratch_shapes=[pltpu.SemaphoreType.DMA((3,)), pltpu.SemaphoreType.DMA((3,)), pltpu.SemaphoreType.DMA],
        name=name)(p)


def adamw(name, parts, w, m, v):
    npart, r, c = parts.shape
    tr = _row_tile(r, 128)
    c1 = 1.0 / (1.0 - ADAM_B1 ** ADAM_STEP)
    c2 = 1.0 / (1.0 - ADAM_B2 ** ADAM_STEP)

    def body(p_ref, w_ref, m_ref, v_ref, g_ref, d_ref, nm_ref, nv_ref):
        g = p_ref[0]
        for j in range(1, npart):
            g = g + p_ref[j]
        nm = ADAM_B1 * m_ref[...] + (1.0 - ADAM_B1) * g
        nv = ADAM_B2 * v_ref[...] + (1.0 - ADAM_B2) * (g * g)
        g_ref[...] = g
        nm_ref[...] = nm
        nv_ref[...] = nv
        d_ref[...] = -ADAM_LR * ((nm * c1) / (jnp.sqrt(nv * c2) + ADAM_EPS) + ADAM_WD * w_ref[...])

    row = pl.BlockSpec((tr, c), lambda i: (i, 0))
    return pl.pallas_call(
        body, grid=(r // tr,), in_specs=[pl.BlockSpec((npart, tr, c), lambda i: (0, i, 0)), row, row, row],
        out_specs=[row] * 4, out_shape=[_sds((r, c), F32)] * 4, compiler_params=_params("parallel"), name=name,
    )(parts, w, m, v)


def _pack(arrays, dtype, row_multiple):
    flat = jnp.concatenate([a.astype(dtype).reshape(-1) for a in arrays])
    per = row_multiple * LANES
    total = -(-flat.shape[0] // per) * per
    return jnp.pad(flat, (0, total - flat.shape[0])).reshape(total // LANES, LANES)


def _pack_blocked(arrays, dtype, row_multiple):
    flat = jnp.concatenate([a.astype(dtype).reshape(N_DEV, -1) for a in arrays], axis=1)
    per = row_multiple * LANES
    total = -(-flat.shape[1] // per) * per
    return jnp.pad(flat, ((0, 0), (0, total - flat.shape[1]))).reshape(N_DEV, total // LANES, LANES)


def _unpack(buf, shapes, lead=()):
    flat = buf.reshape(lead + (-1,))
    out, off = [], 0
    for s in shapes:
        n = math.prod(s)
        out.append(flat[..., off:off + n].reshape(lead + tuple(s)))
        off += n
    return out


def _to_blocked(full, ax):
    s = full.shape
    return jnp.moveaxis(full.reshape(s[:ax] + (N_DEV, s[ax] // N_DEV) + s[ax + 1:]), ax, 0)


def _from_blocked(blk, ax):
    moved = jnp.moveaxis(blk, 0, ax)
    s = moved.shape
    return moved.reshape(s[:ax] + (s[ax] * s[ax + 1],) + s[ax + 2:])


SMALL = ("rg_conv_w", "rg_conv_b", "rg_b_a", "rg_b_i", "rg_lambda")
PACK_ROWS = 128


def kernel(x, attn_w_in, attn_rel_bias, attn_w_out, rg_w_in, rg_conv_w, rg_conv_b, rg_w_a, rg_b_a, rg_w_i, rg_b_i, rg_lambda, rg_w_out, norm_mix_pre, norm_mix_post, norm_ffn_pre, norm_ffn_post, ffn_w_gate, ffn_w_up, ffn_w_down, loss_target, m_attn_w_in, m_attn_rel_bias, m_attn_w_out, m_rg_w_in, m_rg_conv_w, m_rg_conv_b, m_rg_w_a, m_rg_b_a, m_rg_w_i, m_rg_b_i, m_rg_lambda, m_rg_w_out, m_norm_mix_pre, m_norm_mix_post, m_norm_ffn_pre, m_norm_ffn_post, m_ffn_w_gate, m_ffn_w_up, m_ffn_w_down, v_attn_w_in, v_attn_rel_bias, v_attn_w_out, v_rg_w_in, v_rg_conv_w, v_rg_conv_b, v_rg_w_a, v_rg_b_a, v_rg_w_i, v_rg_b_i, v_rg_lambda, v_rg_w_out, v_norm_mix_pre, v_norm_mix_post, v_norm_ffn_pre, v_norm_ffn_post, v_ffn_w_gate, v_ffn_w_up, v_ffn_w_down):
    w_loc = dict(attn_w_in=attn_w_in, attn_rel_bias=attn_rel_bias, attn_w_out=attn_w_out, rg_w_in=rg_w_in,
                 rg_conv_w=rg_conv_w, rg_conv_b=rg_conv_b, rg_w_a=rg_w_a, rg_b_a=rg_b_a, rg_w_i=rg_w_i, rg_b_i=rg_b_i,
                 rg_lambda=rg_lambda, rg_w_out=rg_w_out, norm_mix_pre=norm_mix_pre, norm_mix_post=norm_mix_post,
                 norm_ffn_pre=norm_ffn_pre, norm_ffn_post=norm_ffn_post, ffn_w_gate=ffn_w_gate, ffn_w_up=ffn_w_up,
                 ffn_w_down=ffn_w_down)
    m_loc = dict(attn_w_in=m_attn_w_in, attn_rel_bias=m_attn_rel_bias, attn_w_out=m_attn_w_out, rg_w_in=m_rg_w_in,
                 rg_conv_w=m_rg_conv_w, rg_conv_b=m_rg_conv_b, rg_w_a=m_rg_w_a, rg_b_a=m_rg_b_a, rg_w_i=m_rg_w_i,
                 rg_b_i=m_rg_b_i, rg_lambda=m_rg_lambda, rg_w_out=m_rg_w_out, norm_mix_pre=m_norm_mix_pre,
                 norm_mix_post=m_norm_mix_post, norm_ffn_pre=m_norm_ffn_pre, norm_ffn_post=m_norm_ffn_post,
                 ffn_w_gate=m_ffn_w_gate, ffn_w_up=m_ffn_w_up, ffn_w_down=m_ffn_w_down)
    v_loc = dict(attn_w_in=v_attn_w_in, attn_rel_bias=v_attn_rel_bias, attn_w_out=v_attn_w_out, rg_w_in=v_rg_w_in,
                 rg_conv_w=v_rg_conv_w, rg_conv_b=v_rg_conv_b, rg_w_a=v_rg_w_a, rg_b_a=v_rg_b_a, rg_w_i=v_rg_w_i,
                 rg_b_i=v_rg_b_i, rg_lambda=v_rg_lambda, rg_w_out=v_rg_w_out, norm_mix_pre=v_norm_mix_pre,
                 norm_mix_post=v_norm_mix_post, norm_ffn_pre=v_norm_ffn_pre, norm_ffn_post=v_norm_ffn_post,
                 ffn_w_gate=v_ffn_w_gate, ffn_w_up=v_ffn_w_up, ffn_w_down=v_ffn_w_down)
    sharded = [n for n, _ in SHARDED]
    axis_of = dict(SHARDED)
    shard_shapes = [w_loc[n].shape for n in sharded]
    xt, target = x[0], loss_target[0]
    d_model = xt.shape[1]

    big = _unpack(all_gather("gather_weights", _pack([w_loc[n] for n in sharded], BF16, PACK_ROWS)),
                  shard_shapes, (N_DEV,))
    small = _unpack(all_gather("gather_small", _pack([w_loc[n] for n in SMALL], F32, 8)),
                    [w_loc[n].shape for n in SMALL], (N_DEV,))
    blocked = dict(zip(sharded, big))
    blocked.update(zip(SMALL, small))
    full = {n: _from_blocked(blocked[n], axis_of[n]) for n in sharded if not n.startswith("ffn_")}
    row = lambda a: a.reshape(1, -1).astype(F32)

    def layer_weights(layer):
        j = layer // 2
        norms = dict(g_pre=row(norm_mix_pre[layer]), g_post=row(norm_mix_post[layer]))
        if layer % 2 == 0:
            mix = dict(w_in=full["attn_w_in"][j], w_out=full["attn_w_out"][j], rel_bias=attn_rel_bias[j], **norms)
        else:
            mix = dict(w_in=full["rg_w_in"][j], w_out=full["rg_w_out"][j], conv_w=full["rg_conv_w"][j][:, 0, :],
                       conv_b=row(full["rg_conv_b"][j]), w_a=full["rg_w_a"][j], w_i=full["rg_w_i"][j],
                       b_a=row(full["rg_b_a"][j]), b_i=row(full["rg_b_i"][j]), lam=row(full["rg_lambda"][j]), **norms)
        ffn = dict(w_gate=blocked["ffn_w_gate"][:, layer], w_up=blocked["ffn_w_up"][:, layer],
                   w_down=blocked["ffn_w_down"][:, layer], g_pre=row(norm_ffn_pre[layer]),
                   g_post=row(norm_ffn_post[layer]))
        return mix, ffn

    act, tape = xt, []
    for layer in range(DEPTH):
        mix_w, ffn_w = layer_weights(layer)
        mixer_fwd = attn_layer_fwd if layer % 2 == 0 else rg_layer_fwd
        act, saved_mix = mixer_fwd(f"l{layer}_mix", act, mix_w)
        act, saved_ffn = ffn_layer_fwd(f"l{layer}_ffn", act, ffn_w)
        tape.append((mix_w, ffn_w, saved_mix, saved_ffn))
    dact, sq = loss_grad("loss", act, target)
    loss = lax.psum(0.5 * jnp.sum(sq) / d_model, ("x", "y", "c"))

    grads = {}
    for layer in reversed(range(DEPTH)):
        mix_w, ffn_w, saved_mix, saved_ffn = tape[layer]
        dact, grads[("ffn", layer)] = ffn_layer_bwd(f"l{layer}_ffn", dact, saved_ffn, ffn_w)
        mixer_bwd = attn_layer_bwd if layer % 2 == 0 else rg_layer_bwd
        dact, grads[("mix", layer)] = mixer_bwd(f"l{layer}_mix", dact, saved_mix, mix_w)
    attn_g = [grads[("mix", l)] for l in range(0, DEPTH, 2)]
    rg_g = [grads[("mix", l)] for l in range(1, DEPTH, 2)]
    ffn_g = [grads[("ffn", l)] for l in range(DEPTH)]
    stack = lambda gs, key: jnp.stack([g[key] for g in gs])
    contrib = dict(
        attn_w_in=stack(attn_g, "w_in"), attn_w_out=stack(attn_g, "w_out"), attn_rel_bias=stack(attn_g, "rel_bias"),
        rg_w_in=stack(rg_g, "w_in"), rg_conv_w=stack(rg_g, "conv_w")[:, :, None, :],
        rg_conv_b=stack(rg_g, "conv_b")[:, 0], rg_w_a=stack(rg_g, "w_a"), rg_b_a=stack(rg_g, "b_a").reshape(rg_b_a.shape[0], LRU_BLOCKS, -1),
        rg_w_i=stack(rg_g, "w_i"), rg_b_i=stack(rg_g, "b_i").reshape(rg_b_i.shape[0], LRU_BLOCKS, -1),
        rg_lambda=stack(rg_g, "lam")[:, 0], rg_w_out=stack(rg_g, "w_out"),
        norm_mix_pre=jnp.concatenate([grads[("mix", l)]["g_pre"] for l in range(DEPTH)]),
        norm_mix_post=jnp.concatenate([grads[("mix", l)]["g_post"] for l in range(DEPTH)]),
        norm_ffn_pre=jnp.concatenate([g["g_pre"] for g in ffn_g]),
        norm_ffn_post=jnp.concatenate([g["g_post"] for g in ffn_g]),
    )
    blocked_g = {n: _to_blocked(contrib[n], axis_of[n]) for n in sharded if not n.startswith("ffn_")}
    for n, key in (("ffn_w_gate", "w_gate"), ("ffn_w_up", "w_up"), ("ffn_w_down", "w_down")):
        blocked_g[n] = jnp.stack([g[key] for g in ffn_g], axis=1)

    gbuf = _pack_blocked([blocked_g[n] for n in sharded], F32, PACK_ROWS)
    rows = gbuf.shape[1]
    g4 = gbuf.reshape(4, 2, rows, LANES)
    core = lax.axis_index("c").astype(jnp.int32).reshape(1)
    from_sibling = exchange_pair("rs_pair", g4)
    pair = pair_sum("rs_pair_sum", g4, from_sibling, core)
    by_chip = exchange_chips("rs_chips", pair)
    outs_s = adamw("adamw_sharded", by_chip, *[_pack([d[n] for n in sharded], F32, PACK_ROWS)
                                                 for d in (w_loc, m_loc, v_loc)])
    rep_shapes = [w_loc[n].shape for n in REPLICATED]
    rep_parts = all_gather("gather_rep_grads", _pack([contrib[n] for n in REPLICATED], F32, 8))
    outs_r = adamw("adamw_replicated", rep_parts, *[_pack([d[n] for n in REPLICATED], F32, 8)
                                                    for d in (w_loc, m_loc, v_loc)])

    result = {}
    for kind, buf_s, buf_r in zip(("grad", "delta", "new_m", "new_v"), outs_s, outs_r):
        for n, a in zip(sharded, _unpack(buf_s, shard_shapes)):
            result[(kind, n)] = a
        for n, a in zip(REPLICATED, _unpack(buf_r, rep_shapes)):
            result[(kind, n)] = a
    return (loss, dact[None], *[result[(kind, n)] for kind in ("grad", "delta", "new_m", "new_v") for n in WEIGHTS])
```

```python
import functools
import math

import jax
import jax.numpy as jnp
from jax import lax
from jax.experimental import pallas as pl
from jax.experimental.pallas import tpu as pltpu

F32 = jnp.float32
BF16 = jnp.bfloat16

N_DEV = 8
DEPTH = 4
CHUNK = 64
N_LEFT = 8
BAND = (N_LEFT + 1) * CHUNK
PAD_KEYS = N_LEFT * CHUNK
HEAD_DIM = 64
N_HEADS = 8
REL_CLIP = 256
LRU_BLOCKS = 4
LRU_C = 8.0
RMS_EPS = 1e-6
QK_SCALE = HEAD_DIM ** -0.5

ADAM_LR = 0.001
ADAM_B1 = 0.9
ADAM_B2 = 0.999
ADAM_EPS = 1e-08
ADAM_WD = 0.01
ADAM_STEP = 10

LANES = 1024
V7X_VMEM_LIMIT = 56 * 1024 * 1024

MESH = pl.DeviceIdType.MESH
ANY = pl.BlockSpec(memory_space=pl.ANY)

SHARDED = (
    ("attn_w_in", 2), ("attn_w_out", 1), ("rg_w_in", 2), ("rg_conv_w", 3), ("rg_conv_b", 1),
    ("rg_w_a", 2), ("rg_b_a", 2), ("rg_w_i", 2), ("rg_b_i", 2), ("rg_lambda", 1), ("rg_w_out", 1),
    ("ffn_w_gate", 2), ("ffn_w_up", 2), ("ffn_w_down", 1),
)
REPLICATED = ("attn_rel_bias", "norm_mix_pre", "norm_mix_post", "norm_ffn_pre", "norm_ffn_post")
WEIGHTS = ("attn_w_in", "attn_rel_bias", "attn_w_out", "rg_w_in", "rg_conv_w", "rg_conv_b", "rg_w_a", "rg_b_a",
           "rg_w_i", "rg_b_i", "rg_lambda", "rg_w_out", "norm_mix_pre", "norm_mix_post", "norm_ffn_pre",
           "norm_ffn_post", "ffn_w_gate", "ffn_w_up", "ffn_w_down")


def _params(*dims):
    return pltpu.CompilerParams(dimension_semantics=dims or None, vmem_limit_bytes=V7X_VMEM_LIMIT)


def _sds(shape, dtype):
    return jax.ShapeDtypeStruct(tuple(shape), dtype)


def _row_tile(n, pref):
    t = min(n, pref)
    assert n % t == 0, (n, pref)
    return t


NN = (((1,), (0,)), ((), ()))
NT = (((1,), (1,)), ((), ()))
TN = (((0,), (0,)), ((), ()))


def _gmm(name, a, b, *, grid, a_blk, a_idx, b_blk, b_idx, o_blk, o_idx, out_shape, out_dtype, dn, acc_shape):
    nk = grid[-1]
    kax = len(grid) - 1

    def body(a_ref, b_ref, o_ref, acc_ref):
        part = lax.dot_general(a_ref[...], b_ref[...], dn, preferred_element_type=F32)
        if nk == 1:
            o_ref[...] = part.astype(o_ref.dtype)
            return
        k = pl.program_id(kax)

        @pl.when(k == 0)
        def _():
            acc_ref[...] = part

        @pl.when(k > 0)
        def _():
            acc_ref[...] += part

        @pl.when(k == nk - 1)
        def _():
            o_ref[...] = acc_ref[...].astype(o_ref.dtype)

    return pl.pallas_call(
        body, grid=grid,
        in_specs=[pl.BlockSpec(a_blk, a_idx), pl.BlockSpec(b_blk, b_idx)],
        out_specs=pl.BlockSpec(o_blk, o_idx),
        out_shape=_sds(out_shape, out_dtype),
        scratch_shapes=[pltpu.VMEM(acc_shape, F32)],
        compiler_params=_params(*(["parallel"] * kax + ["arbitrary"])),
        name=name,
    )(a, b)


def mm_nn(name, a, b, out_dtype, tm=1024, tn=512, tk=1024):
    (m, k), (_, n) = a.shape, b.shape
    tm, tn, tk = _row_tile(m, tm), _row_tile(n, tn), _row_tile(k, tk)
    return _gmm(name, a, b, grid=(m // tm, n // tn, k // tk),
                a_blk=(tm, tk), a_idx=lambda i, j, kk: (i, kk), b_blk=(tk, tn), b_idx=lambda i, j, kk: (kk, j),
                o_blk=(tm, tn), o_idx=lambda i, j, kk: (i, j), out_shape=(m, n), out_dtype=out_dtype, dn=NN,
                acc_shape=(tm, tn))


def mm_nt(name, a, b, out_dtype, tm=1024, tn=512, tk=1024):
    (m, k), (n, _) = a.shape, b.shape
    tm, tn, tk = _row_tile(m, tm), _row_tile(n, tn), _row_tile(k, tk)
    return _gmm(name, a, b, grid=(m // tm, n // tn, k // tk),
                a_blk=(tm, tk), a_idx=lambda i, j, kk: (i, kk), b_blk=(tn, tk), b_idx=lambda i, j, kk: (j, kk),
                o_blk=(tm, tn), o_idx=lambda i, j, kk: (i, j), out_shape=(m, n), out_dtype=out_dtype, dn=NT,
                acc_shape=(tm, tn))


def mm_tn(name, a, b, out_dtype, tm=512, tn=512, tk=1024):
    (k, m), (_, n) = a.shape, b.shape
    tm, tn, tk = _row_tile(m, tm), _row_tile(n, tn), _row_tile(k, tk)
    return _gmm(name, a, b, grid=(m // tm, n // tn, k // tk),
                a_blk=(tk, tm), a_idx=lambda i, j, kk: (kk, i), b_blk=(tk, tn), b_idx=lambda i, j, kk: (kk, j),
                o_blk=(tm, tn), o_idx=lambda i, j, kk: (i, j), out_shape=(m, n), out_dtype=out_dtype, dn=TN,
                acc_shape=(tm, tn))


def rmsnorm_fwd(name, x, g):
    t, d = x.shape
    tr = _row_tile(t, 512)

    def body(x_ref, g_ref, o_ref):
        xv = x_ref[...]
        r = lax.rsqrt(jnp.mean(xv * xv, axis=-1, keepdims=True) + RMS_EPS)
        o_ref[...] = (xv * r * g_ref[...]).astype(o_ref.dtype)

    return pl.pallas_call(
        body, grid=(t // tr,),
        in_specs=[pl.BlockSpec((tr, d), lambda i: (i, 0)), pl.BlockSpec((1, d), lambda i: (0, 0))],
        out_specs=pl.BlockSpec((tr, d), lambda i: (i, 0)),
        out_shape=_sds((t, d), BF16), compiler_params=_params("parallel"), name=name)(x, g)


def resid_norm_fwd(name, x, m, g):
    t, d = x.shape
    tr = _row_tile(t, 512)

    def body(x_ref, m_ref, g_ref, o_ref):
        mv = m_ref[...]
        r = lax.rsqrt(jnp.mean(mv * mv, axis=-1, keepdims=True) + RMS_EPS)
        o_ref[...] = x_ref[...] + mv * r * g_ref[...]

    return pl.pallas_call(
        body, grid=(t // tr,),
        in_specs=[pl.BlockSpec((tr, d), lambda i: (i, 0)), pl.BlockSpec((tr, d), lambda i: (i, 0)),
                  pl.BlockSpec((1, d), lambda i: (0, 0))],
        out_specs=pl.BlockSpec((tr, d), lambda i: (i, 0)),
        out_shape=_sds((t, d), F32), compiler_params=_params("parallel"), name=name)(x, m, g)


def norm_bwd(name, dy, x, g, resid, out_dtype):
    t, d = x.shape
    tr = _row_tile(t, 512)
    has_res = resid is not None

    def body(*refs):
        if has_res:
            dy_ref, x_ref, g_ref, r_ref, dx_ref, dg_ref = refs
        else:
            dy_ref, x_ref, g_ref, dx_ref, dg_ref = refs
        i = pl.program_id(0)
        xv = x_ref[...]
        dyv = dy_ref[...].astype(F32)
        r = lax.rsqrt(jnp.mean(xv * xv, axis=-1, keepdims=True) + RMS_EPS)
        xh = xv * r
        dxh = dyv * g_ref[...]
        dx = r * (dxh - xh * jnp.mean(dxh * xh, axis=-1, keepdims=True))
        if has_res:
            dx = dx + r_ref[...]
        dx_ref[...] = dx.astype(dx_ref.dtype)
        part = jnp.sum(dyv * xh, axis=0, keepdims=True)

        @pl.when(i == 0)
        def _():
            dg_ref[...] = part

        @pl.when(i > 0)
        def _():
            dg_ref[...] += part

    row = pl.BlockSpec((tr, d), lambda i: (i, 0))
    vec = pl.BlockSpec((1, d), lambda i: (0, 0))
    ins = [dy, x, g] + ([resid] if has_res else [])
    return pl.pallas_call(
        body, grid=(t // tr,),
        in_specs=[row, row, vec] + ([row] if has_res else []),
        out_specs=[row, vec],
        out_shape=[_sds((t, d), out_dtype), _sds((1, d), F32)],
        compiler_params=_params("arbitrary"), name=name)(*ins)


def loss_grad(name, y, target):
    t, d = y.shape
    tr = _row_tile(t, 512)

    def body(y_ref, t_ref, dy_ref, s_ref):
        i = pl.program_id(0)
        err = y_ref[...] - t_ref[...]
        dy_ref[...] = err * (1.0 / d)
        part = jnp.sum(err * err, axis=0, keepdims=True)

        @pl.when(i == 0)
        def _():
            s_ref[...] = part

        @pl.when(i > 0)
        def _():
            s_ref[...] += part

    row = pl.BlockSpec((tr, d), lambda i: (i, 0))
    vec = pl.BlockSpec((1, d), lambda i: (0, 0))
    return pl.pallas_call(
        body, grid=(t // tr,), in_specs=[row, row], out_specs=[row, vec],
        out_shape=[_sds((t, d), F32), _sds((1, d), F32)],
        compiler_params=_params("arbitrary"), name=name)(y, target)


A_TQ = 512


def _a_scores(qc, kwin, bias, c):
    s = lax.dot_general(qc, kwin, NT, preferred_element_type=F32) + bias
    col = lax.broadcasted_iota(jnp.int32, s.shape, 1)
    valid = col >= (N_LEFT - c) * CHUNK
    return s, valid


def attn_a_fwd(name, q, kp, vp, bias):
    h, t, dh = q.shape
    tq = _row_tile(t, A_TQ)
    ncs = tq // CHUNK

    def body(q_ref, k_ref, v_ref, b_ref, o_ref, l_ref):
        i = pl.program_id(1)
        bias_h = b_ref[...]

        def chunk(cc, carry):
            c = i * ncs + cc
            r0 = pl.multiple_of(cc * CHUNK, CHUNK)
            k0 = pl.multiple_of(c * CHUNK, CHUNK)
            qc = q_ref[pl.ds(r0, CHUNK), :]
            kwin = k_ref[pl.ds(k0, BAND), :]
            vwin = v_ref[pl.ds(k0, BAND), :]
            s, valid = _a_scores(qc, kwin, bias_h, c)
            s = jnp.where(valid, s, -1e30)
            mx = jnp.max(s, axis=-1, keepdims=True)
            p = jnp.exp(s - mx)
            den = jnp.sum(p, axis=-1, keepdims=True)
            p = p * (1.0 / den)
            o_ref[pl.ds(r0, CHUNK), :] = jnp.dot(p.astype(BF16), vwin, preferred_element_type=F32)
            l_ref[pl.ds(r0, CHUNK), :] = mx + jnp.log(den)
            return carry

        lax.fori_loop(0, ncs, chunk, 0)

    return pl.pallas_call(
        body, grid=(h, t // tq),
        in_specs=[pl.BlockSpec((None, tq, dh), lambda hh, i: (hh, i, 0)),
                  pl.BlockSpec((None, t + PAD_KEYS, dh), lambda hh, i: (hh, 0, 0)),
                  pl.BlockSpec((None, t + PAD_KEYS, dh), lambda hh, i: (hh, 0, 0)),
                  pl.BlockSpec((None, CHUNK, BAND), lambda hh, i: (hh, 0, 0))],
        out_specs=[pl.BlockSpec((None, tq, dh), lambda hh, i: (hh, i, 0)),
                   pl.BlockSpec((None, tq, 1), lambda hh, i: (hh, i, 0))],
        out_shape=[_sds((h, t, dh), F32), _sds((h, t, 1), F32)],
        compiler_params=_params("parallel", "parallel"), name=name)(q, kp, vp, bias)


def attn_a_bwd(name, q, kp, vp, bias, o, lse, do):
    h, t, dh = q.shape
    tq = _row_tile(t, A_TQ)
    ncs = tq // CHUNK

    def body(q_ref, k_ref, v_ref, b_ref, o_ref, l_ref, do_ref, dq_ref, dk_ref, dv_ref, db_ref):
        i = pl.program_id(1)

        @pl.when(i == 0)
        def _():
            dk_ref[...] = jnp.zeros_like(dk_ref)
            dv_ref[...] = jnp.zeros_like(dv_ref)
            db_ref[...] = jnp.zeros_like(db_ref)

        bias_h = b_ref[...]

        def chunk(cc, carry):
            c = i * ncs + cc
            r0 = pl.multiple_of(cc * CHUNK, CHUNK)
            k0 = pl.multiple_of(c * CHUNK, CHUNK)
            qc = q_ref[pl.ds(r0, CHUNK), :]
            doc = do_ref[pl.ds(r0, CHUNK), :]
            kwin = k_ref[pl.ds(k0, BAND), :]
            vwin = v_ref[pl.ds(k0, BAND), :]
            s, valid = _a_scores(qc, kwin, bias_h, c)
            p = jnp.where(valid, jnp.exp(s - l_ref[pl.ds(r0, CHUNK), :]), 0.0)
            dp = lax.dot_general(doc, vwin, NT, preferred_element_type=F32)
            delta = jnp.sum(doc.astype(F32) * o_ref[pl.ds(r0, CHUNK), :], axis=-1, keepdims=True)
            ds = p * (dp - delta)
            db_ref[...] += ds
            dsb = ds.astype(BF16)
            dq_ref[pl.ds(r0, CHUNK), :] = jnp.dot(dsb, kwin, preferred_element_type=F32) * QK_SCALE
            dk_ref[pl.ds(k0, BAND), :] += lax.dot_general(dsb, qc, TN, preferred_element_type=F32)
            dv_ref[pl.ds(k0, BAND), :] += lax.dot_general(p.astype(BF16), doc, TN, preferred_element_type=F32)
            return carry

        lax.fori_loop(0, ncs, chunk, 0)

    tile = lambda w: pl.BlockSpec((None, tq, w), lambda hh, i: (hh, i, 0))
    whole = pl.BlockSpec((None, t + PAD_KEYS, dh), lambda hh, i: (hh, 0, 0))
    bspec = pl.BlockSpec((None, CHUNK, BAND), lambda hh, i: (hh, 0, 0))
    return pl.pallas_call(
        body, grid=(h, t // tq),
        in_specs=[tile(dh), whole, whole, bspec, tile(dh), tile(1), tile(dh)],
        out_specs=[tile(dh), whole, whole, bspec],
        out_shape=[_sds((h, t, dh), F32), _sds((h, t + PAD_KEYS, dh), F32), _sds((h, t + PAD_KEYS, dh), F32),
                   _sds((h, CHUNK, BAND), F32)],
        compiler_params=_params("parallel", "arbitrary"), name=name)(q, kp, vp, bias, o, lse, do)


SB_TQ = 1024
SB_TK = 256


def _tri(n, strict):
    j = lax.broadcasted_iota(jnp.int32, (n, n), 0)
    s = lax.broadcasted_iota(jnp.int32, (n, n), 1)
    return jnp.where((j > s) if strict else (j >= s), 1.0, 0.0).astype(BF16)


def _suffix_sum(x, tri, exact):
    hi = x.astype(BF16)
    out = jnp.dot(hi, tri, preferred_element_type=F32)
    if exact:
        lo = (x - hi.astype(F32)).astype(BF16)
        out = out + jnp.dot(lo, tri, preferred_element_type=F32)
    return out


def _sb_block(qv, ks, cm, tri_strict, off):
    z = lax.dot_general(qv, ks, NT, preferred_element_type=F32)
    lb = jnp.minimum(z, 0.0) - jnp.log(1.0 + jnp.exp(-jnp.abs(z)))
    m = lb - z
    causal = None
    if off is not None:
        tq, tk = z.shape
        causal = (lax.broadcasted_iota(jnp.int32, (tq, tk), 1) + off * tk) < lax.broadcasted_iota(jnp.int32, (tq, tk), 0)
        m = jnp.where(causal, m, 0.0)
    after = _suffix_sum(m, tri_strict, False) + cm
    w = jnp.exp(lb + after)
    if off is not None:
        w = jnp.where(causal, w, 0.0)
    return causal, lb, m, w


def sb_fwd(name, q, k, v):
    h, t, dh = q.shape
    tq = _row_tile(t, SB_TQ)
    tk = min(SB_TK, tq)
    per = tq // tk

    def body(q_ref, k_ref, v_ref, o_ref):
        i = pl.program_id(1)
        qv = q_ref[...]
        tri = _tri(tk, True)

        def block(kb, acc, cm, off):
            k0 = pl.multiple_of(kb * tk, tk)
            _, _, m, w = _sb_block(qv, k_ref[pl.ds(k0, tk), :], cm, tri, off)
            acc = acc + jnp.dot(w.astype(BF16), v_ref[pl.ds(k0, tk), :], preferred_element_type=F32)
            return acc, cm + jnp.sum(m, axis=-1, keepdims=True)

        carry = (jnp.zeros((tq, dh), F32), jnp.zeros((tq, 1), F32))
        for off in reversed(range(per)):
            carry = block(i * per + off, *carry, off)
        acc, _ = lax.fori_loop(0, i * per, lambda n, c: block(i * per - 1 - n, *c, None), carry)
        o_ref[...] = acc

    tile = pl.BlockSpec((None, tq, dh), lambda hh, i: (hh, i, 0))
    whole = pl.BlockSpec((None, t, dh), lambda hh, i: (hh, 0, 0))
    return pl.pallas_call(
        body, grid=(h, t // tq), in_specs=[tile, whole, whole], out_specs=tile,
        out_shape=_sds((h, t, dh), F32), compiler_params=_params("parallel", "parallel"), name=name)(q, k, v)


def sb_bwd(name, q, k, v, o, do):
    h, t, dh = q.shape
    tq = _row_tile(t, SB_TQ)
    tk = min(SB_TK, tq)
    per = tq // tk

    def body(q_ref, k_ref, v_ref, o_ref, do_ref, dq_ref, dk_ref, dv_ref):
        i = pl.program_id(1)

        @pl.when(i == 0)
        def _():
            dk_ref[...] = jnp.zeros_like(dk_ref)
            dv_ref[...] = jnp.zeros_like(dv_ref)

        qv = q_ref[...]
        dov = do_ref[...]
        delta = jnp.sum(dov.astype(F32) * o_ref[...], axis=-1, keepdims=True)
        tri_s, tri_i = _tri(tk, True), _tri(tk, False)

        def block(kb, dq, cm, ce, off):
            k0 = pl.multiple_of(kb * tk, tk)
            ks = k_ref[pl.ds(k0, tk), :]
            vs = v_ref[pl.ds(k0, tk), :]
            dw = lax.dot_general(dov, vs, NT, preferred_element_type=F32)
            causal, lb, m, w = _sb_block(qv, ks, cm, tri_s, off)
            wb = w.astype(BF16)
            e = wb.astype(F32) * dw
            left = delta - (_suffix_sum(e, tri_i, True) + ce)
            sig = jnp.exp(lb)
            dz = e * (1.0 - sig) - left * sig
            if off is not None:
                dz = jnp.where(causal, dz, 0.0)
            dz = dz.astype(BF16)
            dq = dq + jnp.dot(dz, ks, preferred_element_type=F32)
            dk_ref[pl.ds(k0, tk), :] += lax.dot_general(dz, qv, TN, preferred_element_type=F32)
            dv_ref[pl.ds(k0, tk), :] += lax.dot_general(wb, dov, TN, preferred_element_type=F32)
            return dq, cm + jnp.sum(m, axis=-1, keepdims=True), ce + jnp.sum(e, axis=-1, keepdims=True)

        zero = jnp.zeros((tq, 1), F32)
        carry = (jnp.zeros((tq, dh), F32), zero, zero)
        for off in reversed(range(per)):
            carry = block(i * per + off, *carry, off)
        dq, _, _ = lax.fori_loop(0, i * per, lambda n, c: block(i * per - 1 - n, *c, None), carry)
        dq_ref[...] = dq * QK_SCALE

    tile = pl.BlockSpec((None, tq, dh), lambda hh, i: (hh, i, 0))
    whole = pl.BlockSpec((None, t, dh), lambda hh, i: (hh, 0, 0))
    return pl.pallas_call(
        body, grid=(h, t // tq), in_specs=[tile, whole, whole, tile, tile], out_specs=[tile, whole, whole],
        out_shape=[_sds((h, t, dh), F32)] * 3,
        compiler_params=_params("parallel", "arbitrary"), name=name)(q, k, v, o, do)


def _sigmoid(x):
    return 1.0 / (1.0 + jnp.exp(-x))


def ffn_up(name, h, wg, wu):
    t, d = h.shape
    nb, _, f8 = wg.shape
    tm = _row_tile(t, 1024)

    def body(h_ref, wg_ref, wu_ref, g_ref, u_ref, a_ref):
        hv = h_ref[...]
        g = jnp.dot(hv, wg_ref[...], preferred_element_type=F32)
        u = jnp.dot(hv, wu_ref[...], preferred_element_type=F32)
        g_ref[...] = g.astype(BF16)
        u_ref[...] = u.astype(BF16)
        a_ref[...] = (g * _sigmoid(g) * u).astype(BF16)

    wspec = pl.BlockSpec((None, d, f8), lambda i, k: (k, 0, 0))
    ospec = pl.BlockSpec((None, tm, f8), lambda i, k: (k, i, 0))
    return pl.pallas_call(
        body, grid=(t // tm, nb), in_specs=[pl.BlockSpec((tm, d), lambda i, k: (i, 0)), wspec, wspec],
        out_specs=[ospec] * 3, out_shape=[_sds((nb, t, f8), BF16)] * 3,
        compiler_params=_params("parallel", "parallel"), name=name)(h, wg, wu)


def ffn_down(name, a, wd):
    nb, t, f8 = a.shape
    d = wd.shape[2]
    tm = _row_tile(t, 1024)
    return _gmm(name, a, wd, grid=(t // tm, nb),
                a_blk=(None, tm, f8), a_idx=lambda i, k: (k, i, 0), b_blk=(None, f8, d), b_idx=lambda i, k: (k, 0, 0),
                o_blk=(tm, d), o_idx=lambda i, k: (i, 0), out_shape=(t, d), out_dtype=F32, dn=NN, acc_shape=(tm, d))


def ffn_bwd_act(name, dm, wd, g, u):
    t, d = dm.shape
    nb, f8, _ = wd.shape
    tm = _row_tile(t, 1024)

    def body(dm_ref, wd_ref, g_ref, u_ref, dg_ref, du_ref):
        da = lax.dot_general(dm_ref[...], wd_ref[...], NT, preferred_element_type=F32)
        gv = g_ref[...].astype(F32)
        uv = u_ref[...].astype(F32)
        sg = _sigmoid(gv)
        dg_ref[...] = (da * uv * sg * (1.0 + gv * (1.0 - sg))).astype(BF16)
        du_ref[...] = (da * gv * sg).astype(BF16)

    bspec = pl.BlockSpec((None, tm, f8), lambda i, k: (k, i, 0))
    return pl.pallas_call(
        body, grid=(t // tm, nb),
        in_specs=[pl.BlockSpec((tm, d), lambda i, k: (i, 0)), pl.BlockSpec((None, f8, d), lambda i, k: (k, 0, 0)),
                  bspec, bspec],
        out_specs=[bspec] * 2, out_shape=[_sds((nb, t, f8), BF16)] * 2,
        compiler_params=_params("parallel", "parallel"), name=name)(dm, wd, g, u)


def ffn_bwd_dh(name, dg, du, wg, wu):
    nb, t, f8 = dg.shape
    d = wg.shape[1]
    tm = _row_tile(t, 1024)

    def body(dg_ref, du_ref, wg_ref, wu_ref, o_ref):
        k = pl.program_id(1)
        part = (lax.dot_general(dg_ref[...], wg_ref[...], NT, preferred_element_type=F32)
                + lax.dot_general(du_ref[...], wu_ref[...], NT, preferred_element_type=F32))

        @pl.when(k == 0)
        def _():
            o_ref[...] = part

        @pl.when(k > 0)
        def _():
            o_ref[...] += part

    bspec = pl.BlockSpec((None, tm, f8), lambda i, k: (k, i, 0))
    wspec = pl.BlockSpec((None, d, f8), lambda i, k: (k, 0, 0))
    return pl.pallas_call(
        body, grid=(t // tm, nb), in_specs=[bspec, bspec, wspec, wspec],
        out_specs=pl.BlockSpec((tm, d), lambda i, k: (i, 0)), out_shape=_sds((t, d), F32),
        compiler_params=_params("parallel", "arbitrary"), name=name)(dg, du, wg, wu)


def ffn_dw_in(name, h, dact):
    t, d = h.shape
    nb, _, f8 = dact.shape
    tk = _row_tile(t, 1024)
    return _gmm(name, h, dact, grid=(nb, t // tk),
                a_blk=(tk, d), a_idx=lambda b, s: (s, 0), b_blk=(None, tk, f8), b_idx=lambda b, s: (b, s, 0),
                o_blk=(None, d, f8), o_idx=lambda b, s: (b, 0, 0), out_shape=(nb, d, f8), out_dtype=F32, dn=TN,
                acc_shape=(d, f8))


def ffn_dw_down(name, a, dm):
    nb, t, f8 = a.shape
    d = dm.shape[1]
    tk = _row_tile(t, 1024)
    return _gmm(name, a, dm, grid=(nb, t // tk),
                a_blk=(None, tk, f8), a_idx=lambda b, s: (b, s, 0), b_blk=(tk, d), b_idx=lambda b, s: (s, 0),
                o_blk=(None, f8, d), o_idx=lambda b, s: (b, 0, 0), out_shape=(nb, f8, d), out_dtype=F32, dn=TN,
                acc_shape=(f8, d))


GELU_C = math.sqrt(2.0 / math.pi)
GELU_A = 0.044715


def _gelu(x):
    return 0.5 * x * (1.0 + jnp.tanh(GELU_C * (x + GELU_A * x * x * x)))


def _gelu_grad(x):
    th = jnp.tanh(GELU_C * (x + GELU_A * x * x * x))
    return 0.5 * (1.0 + th) + 0.5 * x * (1.0 - th * th) * GELU_C * (1.0 + 3.0 * GELU_A * x * x)


def _neg_expm1(x):
    series = x * (1.0 + x * (0.5 + x * (1.0 / 6.0 + x * (1.0 / 24.0 + x * (1.0 / 120.0 + x * (1.0 / 720.0))))))
    return -jnp.where(x > -0.25, series, jnp.exp(x) - 1.0)


def fir4(name, xs, w, b):
    nj, t, c = xs.shape
    tr = _row_tile(t, 256)

    def body(xs_ref, w_ref, b_ref, o_ref):
        acc = b_ref[...] + w_ref[0:1, :] * xs_ref[0]
        for j in range(1, nj):
            acc = acc + w_ref[j:j + 1, :] * xs_ref[j]
        o_ref[...] = acc

    return pl.pallas_call(
        body, grid=(t // tr,),
        in_specs=[pl.BlockSpec((nj, tr, c), lambda i: (0, i, 0)), pl.BlockSpec((nj, c), lambda i: (0, 0)),
                  pl.BlockSpec((1, c), lambda i: (0, 0))],
        out_specs=pl.BlockSpec((tr, c), lambda i: (i, 0)), out_shape=_sds((t, c), F32),
        compiler_params=_params("parallel"), name=name)(xs, w, b)


def fir4_bwd(name, xs, dy):
    nj, t, c = xs.shape
    tr = _row_tile(t, 256)

    def body(xs_ref, dy_ref, dw_ref, db_ref):
        i = pl.program_id(0)

        @pl.when(i == 0)
        def _():
            dw_ref[...] = jnp.zeros_like(dw_ref)
            db_ref[...] = jnp.zeros_like(db_ref)

        dyv = dy_ref[...]
        db_ref[...] += jnp.sum(dyv, axis=0, keepdims=True)
        for j in range(nj):
            dw_ref[j:j + 1, :] += jnp.sum(dyv * xs_ref[j], axis=0, keepdims=True)

    return pl.pallas_call(
        body, grid=(t // tr,),
        in_specs=[pl.BlockSpec((nj, tr, c), lambda i: (0, i, 0)), pl.BlockSpec((tr, c), lambda i: (i, 0))],
        out_specs=[pl.BlockSpec((nj, c), lambda i: (0, 0)), pl.BlockSpec((1, c), lambda i: (0, 0))],
        out_shape=[_sds((nj, c), F32), _sds((1, c), F32)],
        compiler_params=_params("arbitrary"), name=name)(xs, dy)


def _rg_gate_values(xcv, wa_ref, wi_ref, ba_ref, bi_ref, lam_ref):
    xb = xcv.astype(BF16)
    r = _sigmoid(jnp.dot(xb, wa_ref[...], preferred_element_type=F32) + ba_ref[...])
    ig = _sigmoid(jnp.dot(xb, wi_ref[...], preferred_element_type=F32) + bi_ref[...])
    lam = lam_ref[...]
    sp = jnp.maximum(-lam, 0.0) + jnp.log(1.0 + jnp.exp(-jnp.abs(lam)))
    log_a = -LRU_C * r * sp
    a = jnp.exp(log_a)
    mult = jnp.sqrt(_neg_expm1(2.0 * log_a))
    return xb, r, ig, sp, a, mult


def rg_gates_fwd(name, xc, wa, wi, ba, bi, lam):
    t, c = xc.shape
    nb, cb, _ = wa.shape
    tm = _row_tile(t, 512)

    def body(xc_ref, wa_ref, wi_ref, ba_ref, bi_ref, lam_ref, a_ref, u_ref):
        xcv = xc_ref[...]
        _, _, ig, _, a, mult = _rg_gate_values(xcv, wa_ref, wi_ref, ba_ref, bi_ref, lam_ref)
        a_ref[...] = a
        u_ref[...] = mult * (ig * xcv)

    blk = pl.BlockSpec((tm, cb), lambda i, n: (i, n))
    wsp = pl.BlockSpec((None, cb, cb), lambda i, n: (n, 0, 0))
    vec = pl.BlockSpec((1, cb), lambda i, n: (0, n))
    return pl.pallas_call(
        body, grid=(t // tm, nb), in_specs=[blk, wsp, wsp, vec, vec, vec], out_specs=[blk, blk],
        out_shape=[_sds((t, c), F32)] * 2, compiler_params=_params("parallel", "parallel"), name=name,
    )(xc, wa, wi, ba, bi, lam)


def rg_gates_bwd(name, xc, gu, hprev, wa, wi, ba, bi, lam):
    t, c = xc.shape
    nb, cb, _ = wa.shape
    tm = _row_tile(t, 512)

    def body(xc_ref, gu_ref, hp_ref, wa_ref, wi_ref, ba_ref, bi_ref, lam_ref,
             dxc_ref, dwa_ref, dwi_ref, dba_ref, dbi_ref, dlam_ref):
        i = pl.program_id(1)

        @pl.when(i == 0)
        def _():
            for ref in (dwa_ref, dwi_ref, dba_ref, dbi_ref, dlam_ref):
                ref[...] = jnp.zeros_like(ref)

        xcv = xc_ref[...]
        xb, r, ig, sp, a, mult = _rg_gate_values(xcv, wa_ref, wi_ref, ba_ref, bi_ref, lam_ref)
        gv = gu_ref[...]
        d_ixc = gv * mult
        d_i = d_ixc * xcv
        d_mult = gv * ig * xcv
        d_a = gv * hp_ref[...] - d_mult * a / mult
        d_log_a = d_a * a
        d_r = d_log_a * (-LRU_C * sp)
        sig_neg_lam = 1.0 / (1.0 + jnp.exp(lam_ref[...]))
        dlam_ref[...] += jnp.sum(d_log_a * r, axis=0, keepdims=True) * (LRU_C * sig_neg_lam)
        dpa = d_r * r * (1.0 - r)
        dpi = d_i * ig * (1.0 - ig)
        dba_ref[...] += jnp.sum(dpa, axis=0, keepdims=True)
        dbi_ref[...] += jnp.sum(dpi, axis=0, keepdims=True)
        dpab, dpib = dpa.astype(BF16), dpi.astype(BF16)
        dxc_ref[...] = (d_ixc * ig + lax.dot_general(dpab, wa_ref[...], NT, preferred_element_type=F32)
                        + lax.dot_general(dpib, wi_ref[...], NT, preferred_element_type=F32))
        dwa_ref[...] += lax.dot_general(xb, dpab, TN, preferred_element_type=F32)
        dwi_ref[...] += lax.dot_general(xb, dpib, TN, preferred_element_type=F32)

    blk = pl.BlockSpec((tm, cb), lambda n, i: (i, n))
    wsp = pl.BlockSpec((None, cb, cb), lambda n, i: (n, 0, 0))
    vec = pl.BlockSpec((1, cb), lambda n, i: (0, n))
    return pl.pallas_call(
        body, grid=(nb, t // tm), in_specs=[blk, blk, blk, wsp, wsp, vec, vec, vec],
        out_specs=[blk, wsp, wsp, vec, vec, vec],
        out_shape=[_sds((t, c), F32), _sds((nb, cb, cb), F32), _sds((nb, cb, cb), F32),
                   _sds((1, c), F32), _sds((1, c), F32), _sds((1, c), F32)],
        compiler_params=_params("parallel", "arbitrary"), name=name)(xc, gu, hprev, wa, wi, ba, bi, lam)


SCAN_TS = 256
SCAN_TC = 512


def _tile_scan(a, b, reverse):
    ts = a.shape[0]
    row = lax.broadcasted_iota(jnp.int32, a.shape, 0)
    d = 1
    while d < ts:
        if reverse:
            inside = row < ts - d
            a_sh = jnp.where(inside, pltpu.roll(a, ts - d, 0), 1.0)
            b_sh = jnp.where(inside, pltpu.roll(b, ts - d, 0), 0.0)
        else:
            inside = row >= d
            a_sh = jnp.where(inside, pltpu.roll(a, d, 0), 1.0)
            b_sh = jnp.where(inside, pltpu.roll(b, d, 0), 0.0)
        b = b + a * b_sh
        a = a * a_sh
        d *= 2
    return a, b


def rg_scan_fwd(name, a, u, gate_pre):
    t, c = a.shape
    ts, tc = _row_tile(t, SCAN_TS), _row_tile(c, SCAN_TC)

    def body(a_ref, u_ref, g_ref, h_ref, z_ref, carry_ref):
        s = pl.program_id(1)

        @pl.when(s == 0)
        def _():
            carry_ref[...] = jnp.zeros_like(carry_ref)

        ac, bc = _tile_scan(a_ref[...], u_ref[...], False)
        h = bc + ac * carry_ref[0:1, :]
        h_ref[...] = h
        z_ref[...] = (h * _gelu(g_ref[...])).astype(BF16)
        carry_ref[0:1, :] = h[ts - 1:ts, :]

    blk = pl.BlockSpec((ts, tc), lambda j, s: (s, j))
    return pl.pallas_call(
        body, grid=(c // tc, t // ts), in_specs=[blk, blk, blk], out_specs=[blk, blk],
        out_shape=[_sds((t, c), F32), _sds((t, c), BF16)], scratch_shapes=[pltpu.VMEM((8, tc), F32)],
        compiler_params=_params("parallel", "arbitrary"), name=name)(a, u, gate_pre)


def rg_scan_bwd(name, a_next, hs, gate_pre, dz):
    t, c = hs.shape
    ts, tc = _row_tile(t, SCAN_TS), _row_tile(c, SCAN_TC)
    nt = t // ts

    def body(an_ref, h_ref, g_ref, dz_ref, gu_ref, dgate_ref, carry_ref):
        s = pl.program_id(1)

        @pl.when(s == 0)
        def _():
            carry_ref[...] = jnp.zeros_like(carry_ref)

        gate = g_ref[...]
        dzv = dz_ref[...]
        dgate_ref[...] = (dzv * h_ref[...] * _gelu_grad(gate)).astype(BF16)
        ac, bc = _tile_scan(an_ref[...], dzv * _gelu(gate), True)
        gu = bc + ac * carry_ref[0:1, :]
        gu_ref[...] = gu
        carry_ref[0:1, :] = gu[0:1, :]

    blk = pl.BlockSpec((ts, tc), lambda j, s: (nt - 1 - s, j))
    return pl.pallas_call(
        body, grid=(c // tc, nt), in_specs=[blk, blk, blk, blk], out_specs=[blk, blk],
        out_shape=[_sds((t, c), F32), _sds((t, c), BF16)], scratch_shapes=[pltpu.VMEM((8, tc), F32)],
        compiler_params=_params("parallel", "arbitrary"), name=name)(a_next, hs, gate_pre, dz)


def _shift_down(x, k):
    return jnp.pad(x, ((k, 0), (0, 0)))[:x.shape[0]] if k else x


def _shift_up(x, k):
    return jnp.pad(x, ((0, k), (0, 0)))[k:] if k else x


def _heads(x2d, n):
    t = x2d.shape[0]
    return x2d.reshape(t, n, N_HEADS, HEAD_DIM).transpose(1, 2, 0, 3)


def _unheads(x):
    n, h, t, d = x.shape
    return x.transpose(2, 0, 1, 3).reshape(t, n * h * d)


TOEP_W = 640
TOEP_FLAT = 320
TABLE_LOW = 193


def rel_bias_matrix(name, table):
    h = table.shape[0]
    diag = jnp.concatenate([jnp.repeat(table[:, 2 * REL_CLIP:], TOEP_FLAT, axis=1),
                            jnp.flip(table[:, TABLE_LOW:2 * REL_CLIP], axis=1),
                            jnp.zeros((h, 1), table.dtype)], axis=1)[:, None, :]

    def body(v_ref, o_ref):
        rows = jnp.broadcast_to(v_ref[...], (CHUNK, TOEP_W))
        o_ref[...] = pltpu.roll(rows, TOEP_W - (CHUNK - 1), 1, stride=1, stride_axis=0)

    out = pl.pallas_call(
        body, grid=(h,), in_specs=[pl.BlockSpec((None, 1, TOEP_W), lambda hh: (hh, 0, 0))],
        out_specs=pl.BlockSpec((None, CHUNK, TOEP_W), lambda hh: (hh, 0, 0)),
        out_shape=_sds((h, CHUNK, TOEP_W), F32), compiler_params=_params("parallel"), name=name)(diag)
    return out[:, :, :BAND]


def rel_bias_grad(name, dbias):
    h = dbias.shape[0]
    flipped = jnp.pad(jnp.flip(dbias, axis=1), ((0, 0), (0, 0), (0, TOEP_W - BAND)))

    def body(x_ref, o_ref):
        skew = pltpu.roll(x_ref[...], 0, 1, stride=1, stride_axis=0)
        col = jnp.sum(skew, axis=0, keepdims=True)
        lane = lax.broadcasted_iota(jnp.int32, col.shape, 1)
        flat = jnp.sum(jnp.where(lane < TOEP_FLAT, col, 0.0), axis=1, keepdims=True)
        o_ref[...] = jnp.where(lane == TOEP_W - 1, flat, col)

    out = pl.pallas_call(
        body, grid=(h,), in_specs=[pl.BlockSpec((None, CHUNK, TOEP_W), lambda hh: (hh, 0, 0))],
        out_specs=pl.BlockSpec((None, 1, TOEP_W), lambda hh: (hh, 0, 0)),
        out_shape=_sds((h, 1, TOEP_W), F32), compiler_params=_params("parallel"), name=name)(flipped)[:, 0, :]
    return jnp.concatenate([jnp.zeros((h, TABLE_LOW), F32), jnp.flip(out[:, TOEP_FLAT:TOEP_W - 1], axis=1),
                            out[:, TOEP_W - 1:]], axis=1)


def attn_layer_fwd(tag, x, w):
    h = rmsnorm_fwd(tag + "_norm", x, w["g_pre"])
    proj = mm_nn(tag + "_proj", h, w["w_in"], BF16)
    qkv = _heads(proj, 6)
    pad = lambda a: jnp.pad(a, ((0, 0), (PAD_KEYS, 0), (0, 0)))
    qa, kap, vap = qkv[0] * QK_SCALE, pad(qkv[1]), pad(qkv[2])
    qs, ks, vs = qkv[3] * QK_SCALE, qkv[4], qkv[5]
    bias = rel_bias_matrix(tag + "_bias", w["rel_bias"])
    oa, lse = attn_a_fwd(tag + "_a", qa, kap, vap, bias)
    ob = sb_fwd(tag + "_sb", qs, ks, vs)
    o = _unheads(jnp.stack([oa, ob])).astype(BF16)
    m = mm_nn(tag + "_out", o, w["w_out"], F32)
    x1 = resid_norm_fwd(tag + "_res", x, m, w["g_post"])
    return x1, (x, h, qa, kap, vap, qs, ks, vs, bias, oa, lse, ob, o, m)


def attn_layer_bwd(tag, dx1, saved, w):
    x, h, qa, kap, vap, qs, ks, vs, bias, oa, lse, ob, o, m = saved
    dm, dg_post = norm_bwd(tag + "_dpost", dx1, m, w["g_post"], None, BF16)
    d_w_out = mm_tn(tag + "_dwout", o, dm, F32)
    do = _heads(mm_nt(tag + "_do", dm, w["w_out"], BF16), 2)
    dqa, dkap, dvap, dbias = attn_a_bwd(tag + "_da", qa, kap, vap, bias, oa, lse, do[0])
    dqs, dks, dvs = sb_bwd(tag + "_dsb", qs, ks, vs, ob, do[1])
    d_rel = rel_bias_grad(tag + "_dbias", dbias)
    dproj = _unheads(jnp.stack([dqa, dkap[:, PAD_KEYS:], dvap[:, PAD_KEYS:], dqs, dks, dvs])).astype(BF16)
    d_w_in = mm_tn(tag + "_dwin", h, dproj, F32)
    dh = mm_nt(tag + "_dh", dproj, w["w_in"], F32)
    dx, dg_pre = norm_bwd(tag + "_dpre", dh, x, w["g_pre"], dx1, F32)
    return dx, dict(w_in=d_w_in, w_out=d_w_out, rel_bias=d_rel, g_pre=dg_pre, g_post=dg_post)


def rg_layer_fwd(tag, x, w):
    c = w["w_out"].shape[0]
    h = rmsnorm_fwd(tag + "_norm", x, w["g_pre"])
    proj = mm_nn(tag + "_proj", h, w["w_in"], F32)
    gate_pre, xr = proj[:, :c], proj[:, c:]
    xs = jnp.stack([_shift_down(xr, 3 - j) for j in range(4)])
    xc = fir4(tag + "_conv", xs, w["conv_w"], w["conv_b"])
    a, u = rg_gates_fwd(tag + "_gates", xc, w["w_a"], w["w_i"], w["b_a"], w["b_i"], w["lam"])
    hs, z = rg_scan_fwd(tag + "_scan", a, u, gate_pre)
    m = mm_nn(tag + "_out", z, w["w_out"], F32)
    x1 = resid_norm_fwd(tag + "_res", x, m, w["g_post"])
    return x1, (x, h, gate_pre, xs, xc, a, hs, z, m)


def rg_layer_bwd(tag, dx1, saved, w):
    x, h, gate_pre, xs, xc, a, hs, z, m = saved
    dm, dg_post = norm_bwd(tag + "_dpost", dx1, m, w["g_post"], None, BF16)
    d_w_out = mm_tn(tag + "_dwout", z, dm, F32)
    dz = mm_nt(tag + "_dz", dm, w["w_out"], F32)
    gu, dgate = rg_scan_bwd(tag + "_dscan", _shift_up(a, 1), hs, gate_pre, dz)
    dxc, d_w_a, d_w_i, d_b_a, d_b_i, d_lam = rg_gates_bwd(
        tag + "_dgates", xc, gu, _shift_down(hs, 1), w["w_a"], w["w_i"], w["b_a"], w["b_i"], w["lam"])
    d_conv_w, d_conv_b = fir4_bwd(tag + "_dconvw", xs, dxc)
    dxs = jnp.stack([_shift_up(dxc, 3 - j) for j in range(4)])
    dxr = fir4(tag + "_dconv", dxs, w["conv_w"], jnp.zeros_like(w["conv_b"]))
    dproj = jnp.concatenate([dgate, dxr.astype(BF16)], axis=1)
    d_w_in = mm_tn(tag + "_dwin", h, dproj, F32)
    dh = mm_nt(tag + "_dh", dproj, w["w_in"], F32)
    dx, dg_pre = norm_bwd(tag + "_dpre", dh, x, w["g_pre"], dx1, F32)
    return dx, dict(w_in=d_w_in, w_out=d_w_out, conv_w=d_conv_w, conv_b=d_conv_b, w_a=d_w_a, w_i=d_w_i,
                    b_a=d_b_a, b_i=d_b_i, lam=d_lam, g_pre=dg_pre, g_post=dg_post)


def ffn_layer_fwd(tag, x, w):
    h = rmsnorm_fwd(tag + "_norm", x, w["g_pre"])
    g, u, a = ffn_up(tag + "_up", h, w["w_gate"], w["w_up"])
    f = ffn_down(tag + "_down", a, w["w_down"])
    x1 = resid_norm_fwd(tag + "_res", x, f, w["g_post"])
    return x1, (x, h, g, u, a, f)


def ffn_layer_bwd(tag, dx1, saved, w):
    x, h, g, u, a, f = saved
    dm, dg_post = norm_bwd(tag + "_dpost", dx1, f, w["g_post"], None, BF16)
    d_w_down = ffn_dw_down(tag + "_dwdown", a, dm)
    dg, du = ffn_bwd_act(tag + "_dact", dm, w["w_down"], g, u)
    d_w_gate = ffn_dw_in(tag + "_dwgate", h, dg)
    d_w_up = ffn_dw_in(tag + "_dwup", h, du)
    dh = ffn_bwd_dh(tag + "_dh", dg, du, w["w_gate"], w["w_up"])
    dx, dg_pre = norm_bwd(tag + "_dpre", dh, x, w["g_pre"], dx1, F32)
    return dx, dict(w_gate=d_w_gate, w_up=d_w_up, w_down=d_w_down, g_pre=dg_pre, g_post=dg_post)


def _place():
    return lax.axis_index("x"), lax.axis_index("y"), lax.axis_index("c")


def all_gather(name, blk):
    r, c = blk.shape

    def body(x_ref, out_ref, send_sems, recv_sems, local_sem):
        x, y, cc = _place()
        me, sibling = (x, y, cc), (x, y, 1 - cc)
        chips = [(1 - x, y), (x, 1 - y), (1 - x, 1 - y)]

        def slot(px, py, pc):
            return out_ref.at[4 * px + 2 * py + pc]

        def copy(k, block, to, src=None):
            return pltpu.make_async_remote_copy(
                src_ref=slot(*block) if src is None else src, dst_ref=slot(*block),
                send_sem=send_sems.at[k], recv_sem=recv_sems.at[k], device_id=to, device_id_type=MESH)

        mine = pltpu.make_async_copy(x_ref, slot(*me), local_sem)
        mine.start()
        first = [copy(0, me, sibling, src=x_ref)]
        first += [copy(1 + j, me, (*chip, cc), src=x_ref) for j, chip in enumerate(chips)]
        for cp in first:
            cp.start()
        passed = [copy(4 + j, (*chip, cc), sibling) for j, chip in enumerate(chips)]
        for j, chip in enumerate(chips):
            copy(1 + j, (*chip, cc), me).wait_recv()
            passed[j].start()
        copy(0, sibling, me).wait_recv()
        for j, chip in enumerate(chips):
            copy(4 + j, (*chip, 1 - cc), me).wait_recv()
        for cp in first + passed:
            cp.wait_send()
        mine.wait()

    return pl.pallas_call(
        body, out_shape=_sds((N_DEV, r, c), blk.dtype), in_specs=[ANY], out_specs=ANY,
        scratch_shapes=[pltpu.SemaphoreType.DMA((7,)), pltpu.SemaphoreType.DMA((7,)), pltpu.SemaphoreType.DMA],
        name=name)(blk)


def exchange_pair(name, g):
    nchip, _, r, c = g.shape

    def body(g_ref, land_ref, send_sems, recv_sems):
        x, y, cc = _place()
        copies = [pltpu.make_async_remote_copy(
            src_ref=g_ref.at[j, 1 - cc], dst_ref=land_ref.at[j], send_sem=send_sems.at[j], recv_sem=recv_sems.at[j],
            device_id=(x, y, 1 - cc), device_id_type=MESH) for j in range(nchip)]
        for cp in copies:
            cp.start()
        for cp in copies:
            cp.wait()

    return pl.pallas_call(
        body, out_shape=_sds((nchip, r, c), g.dtype), in_specs=[ANY], out_specs=ANY,
        scratch_shapes=[pltpu.SemaphoreType.DMA((nchip,)), pltpu.SemaphoreType.DMA((nchip,))], name=name)(g)


def pair_sum(name, g, land, core):
    nchip, _, r, c = g.shape
    tr = _row_tile(r, 128)

    def body(core_ref, g_ref, l_ref, o_ref):
        o_ref[...] = g_ref[...] + l_ref[...]

    return pl.pallas_call(
        body,
        grid_spec=pltpu.PrefetchScalarGridSpec(
            num_scalar_prefetch=1, grid=(nchip, r // tr),
            in_specs=[pl.BlockSpec((None, None, tr, c), lambda j, i, core_ref: (j, core_ref[0], i, 0)),
                      pl.BlockSpec((None, tr, c), lambda j, i, core_ref: (j, i, 0))],
            out_specs=pl.BlockSpec((None, tr, c), lambda j, i, core_ref: (j, i, 0))),
        out_shape=_sds((nchip, r, c), g.dtype), compiler_params=_params("parallel", "parallel"), name=name,
    )(core, g, land)


def exchange_chips(name, p):
    nchip, r, c = p.shape

    def body(p_ref, land_ref, send_sems, recv_sems, local_sem):
        x, y, cc = _place()
        mine = 2 * x + y
        chips = [(1 - x, y), (x, 1 - y), (1 - x, 1 - y)]
        own = pltpu.make_async_copy(p_ref.at[mine], land_ref.at[mine], local_sem)
        own.start()
        sends = [pltpu.make_async_remote_copy(
            src_ref=p_ref.at[2 * px + py], dst_ref=land_ref.at[mine], send_sem=send_sems.at[k],
            recv_sem=recv_sems.at[k], device_id=(px, py, cc), device_id_type=MESH)
            for k, (px, py) in enumerate(chips)]
        for cp in sends:
            cp.start()
        for k, (px, py) in enumerate(chips):
            pltpu.make_async_remote_copy(
                src_ref=p_ref.at[mine], dst_ref=land_ref.at[2 * px + py], send_sem=send_sems.at[k],
                recv_sem=recv_sems.at[k], device_id=(px, py, cc), device_id_type=MESH).wait_recv()
        for cp in sends:
            cp.wait_send()
        own.wait()

    return pl.pallas_call(
        body, out_shape=_sds((nchip, r, c), p.dtype), in_specs=[ANY], out_specs=ANY,
        scratch_shapes=[pltpu.SemaphoreType.DMA((3,)), pltpu.SemaphoreType.DMA((3,)), pltpu.SemaphoreType.DMA],
        name=name)(p)


def adamw(name, parts, w, m, v):
    npart, r, c = parts.shape
    tr = _row_tile(r, 128)
    c1 = 1.0 / (1.0 - ADAM_B1 ** ADAM_STEP)
    c2 = 1.0 / (1.0 - ADAM_B2 ** ADAM_STEP)

    def body(p_ref, w_ref, m_ref, v_ref, g_ref, d_ref, nm_ref, nv_ref):
        g = p_ref[0]
        for j in range(1, npart):
            g = g + p_ref[j]
        nm = ADAM_B1 * m_ref[...] + (1.0 - ADAM_B1) * g
        nv = ADAM_B2 * v_ref[...] + (1.0 - ADAM_B2) * (g * g)
        g_ref[...] = g
        nm_ref[...] = nm
        nv_ref[...] = nv
        d_ref[...] = -ADAM_LR * ((nm * c1) / (jnp.sqrt(nv * c2) + ADAM_EPS) + ADAM_WD * w_ref[...])

    row = pl.BlockSpec((tr, c), lambda i: (i, 0))
    return pl.pallas_call(
        body, grid=(r // tr,), in_specs=[pl.BlockSpec((npart, tr, c), lambda i: (0, i, 0)), row, row, row],
        out_specs=[row] * 4, out_shape=[_sds((r, c), F32)] * 4, compiler_params=_params("parallel"), name=name,
    )(parts, w, m, v)


def _pack(arrays, dtype, row_multiple):
    flat = jnp.concatenate([a.astype(dtype).reshape(-1) for a in arrays])
    per = row_multiple * LANES
    total = -(-flat.shape[0] // per) * per
    return jnp.pad(flat, (0, total - flat.shape[0])).reshape(total // LANES, LANES)


def _pack_blocked(arrays, dtype, row_multiple):
    flat = jnp.concatenate([a.astype(dtype).reshape(N_DEV, -1) for a in arrays], axis=1)
    per = row_multiple * LANES
    total = -(-flat.shape[1] // per) * per
    return jnp.pad(flat, ((0, 0), (0, total - flat.shape[1]))).reshape(N_DEV, total // LANES, LANES)


def _unpack(buf, shapes, lead=()):
    flat = buf.reshape(lead + (-1,))
    out, off = [], 0
    for s in shapes:
        n = math.prod(s)
        out.append(flat[..., off:off + n].reshape(lead + tuple(s)))
        off += n
    return out


def _to_blocked(full, ax):
    s = full.shape
    return jnp.moveaxis(full.reshape(s[:ax] + (N_DEV, s[ax] // N_DEV) + s[ax + 1:]), ax, 0)


def _from_blocked(blk, ax):
    moved = jnp.moveaxis(blk, 0, ax)
    s = moved.shape
    return moved.reshape(s[:ax] + (s[ax] * s[ax + 1],) + s[ax + 2:])


SMALL = ("rg_conv_w", "rg_conv_b", "rg_b_a", "rg_b_i", "rg_lambda")
PACK_ROWS = 128


def kernel(x, attn_w_in, attn_rel_bias, attn_w_out, rg_w_in, rg_conv_w, rg_conv_b, rg_w_a, rg_b_a, rg_w_i, rg_b_i, rg_lambda, rg_w_out, norm_mix_pre, norm_mix_post, norm_ffn_pre, norm_ffn_post, ffn_w_gate, ffn_w_up, ffn_w_down, loss_target, m_attn_w_in, m_attn_rel_bias, m_attn_w_out, m_rg_w_in, m_rg_conv_w, m_rg_conv_b, m_rg_w_a, m_rg_b_a, m_rg_w_i, m_rg_b_i, m_rg_lambda, m_rg_w_out, m_norm_mix_pre, m_norm_mix_post, m_norm_ffn_pre, m_norm_ffn_post, m_ffn_w_gate, m_ffn_w_up, m_ffn_w_down, v_attn_w_in, v_attn_rel_bias, v_attn_w_out, v_rg_w_in, v_rg_conv_w, v_rg_conv_b, v_rg_w_a, v_rg_b_a, v_rg_w_i, v_rg_b_i, v_rg_lambda, v_rg_w_out, v_norm_mix_pre, v_norm_mix_post, v_norm_ffn_pre, v_norm_ffn_post, v_ffn_w_gate, v_ffn_w_up, v_ffn_w_down):
    w_loc = dict(attn_w_in=attn_w_in, attn_rel_bias=attn_rel_bias, attn_w_out=attn_w_out, rg_w_in=rg_w_in,
                 rg_conv_w=rg_conv_w, rg_conv_b=rg_conv_b, rg_w_a=rg_w_a, rg_b_a=rg_b_a, rg_w_i=rg_w_i, rg_b_i=rg_b_i,
                 rg_lambda=rg_lambda, rg_w_out=rg_w_out, norm_mix_pre=norm_mix_pre, norm_mix_post=norm_mix_post,
                 norm_ffn_pre=norm_ffn_pre, norm_ffn_post=norm_ffn_post, ffn_w_gate=ffn_w_gate, ffn_w_up=ffn_w_up,
                 ffn_w_down=ffn_w_down)
    m_loc = dict(attn_w_in=m_attn_w_in, attn_rel_bias=m_attn_rel_bias, attn_w_out=m_attn_w_out, rg_w_in=m_rg_w_in,
                 rg_conv_w=m_rg_conv_w, rg_conv_b=m_rg_conv_b, rg_w_a=m_rg_w_a, rg_b_a=m_rg_b_a, rg_w_i=m_rg_w_i,
                 rg_b_i=m_rg_b_i, rg_lambda=m_rg_lambda, rg_w_out=m_rg_w_out, norm_mix_pre=m_norm_mix_pre,
                 norm_mix_post=m_norm_mix_post, norm_ffn_pre=m_norm_ffn_pre, norm_ffn_post=m_norm_ffn_post,
                 ffn_w_gate=m_ffn_w_gate, ffn_w_up=m_ffn_w_up, ffn_w_down=m_ffn_w_down)
    v_loc = dict(attn_w_in=v_attn_w_in, attn_rel_bias=v_attn_rel_bias, attn_w_out=v_attn_w_out, rg_w_in=v_rg_w_in,
                 rg_conv_w=v_rg_conv_w, rg_conv_b=v_rg_conv_b, rg_w_a=v_rg_w_a, rg_b_a=v_rg_b_a, rg_w_i=v_rg_w_i,
                 rg_b_i=v_rg_b_i, rg_lambda=v_rg_lambda, rg_w_out=v_rg_w_out, norm_mix_pre=v_norm_mix_pre,
                 norm_mix_post=v_norm_mix_post, norm_ffn_pre=v_norm_ffn_pre, norm_ffn_post=v_norm_ffn_post,
                 ffn_w_gate=v_ffn_w_gate, ffn_w_up=v_ffn_w_up, ffn_w_down=v_ffn_w_down)
    sharded = [n for n, _ in SHARDED]
    axis_of = dict(SHARDED)
    shard_shapes = [w_loc[n].shape for n in sharded]
    xt, target = x[0], loss_target[0]
    d_model = xt.shape[1]

    big = _unpack(all_gather("gather_weights", _pack([w_loc[n] for n in sharded], BF16, PACK_ROWS)),
                  shard_shapes, (N_DEV,))
    small = _unpack(all_gather("gather_small", _pack([w_loc[n] for n in SMALL], F32, 8)),
                    [w_loc[n].shape for n in SMALL], (N_DEV,))
    blocked = dict(zip(sharded, big))
    blocked.update(zip(SMALL, small))
    full = {n: _from_blocked(blocked[n], axis_of[n]) for n in sharded if not n.startswith("ffn_")}
    row = lambda a: a.reshape(1, -1).astype(F32)

    def layer_weights(layer):
        j = layer // 2
        norms = dict(g_pre=row(norm_mix_pre[layer]), g_post=row(norm_mix_post[layer]))
        if layer % 2 == 0:
            mix = dict(w_in=full["attn_w_in"][j], w_out=full["attn_w_out"][j], rel_bias=attn_rel_bias[j], **norms)
        else:
            mix = dict(w_in=full["rg_w_in"][j], w_out=full["rg_w_out"][j], conv_w=full["rg_conv_w"][j][:, 0, :],
                       conv_b=row(full["rg_conv_b"][j]), w_a=full["rg_w_a"][j], w_i=full["rg_w_i"][j],
                       b_a=row(full["rg_b_a"][j]), b_i=row(full["rg_b_i"][j]), lam=row(full["rg_lambda"][j]), **norms)
        ffn = dict(w_gate=blocked["ffn_w_gate"][:, layer], w_up=blocked["ffn_w_up"][:, layer],
                   w_down=blocked["ffn_w_down"][:, layer], g_pre=row(norm_ffn_pre[layer]),
                   g_post=row(norm_ffn_post[layer]))
        return mix, ffn

    act, tape = xt, []
    for layer in range(DEPTH):
        mix_w, ffn_w = layer_weights(layer)
        mixer_fwd = attn_layer_fwd if layer % 2 == 0 else rg_layer_fwd
        act, saved_mix = mixer_fwd(f"l{layer}_mix", act, mix_w)
        act, saved_ffn = ffn_layer_fwd(f"l{layer}_ffn", act, ffn_w)
        tape.append((mix_w, ffn_w, saved_mix, saved_ffn))
    dact, sq = loss_grad("loss", act, target)
    loss = lax.psum(0.5 * jnp.sum(sq) / d_model, ("x", "y", "c"))

    grads = {}
    for layer in reversed(range(DEPTH)):
        mix_w, ffn_w, saved_mix, saved_ffn = tape[layer]
        dact, grads[("ffn", layer)] = ffn_layer_bwd(f"l{layer}_ffn", dact, saved_ffn, ffn_w)
        mixer_bwd = attn_layer_bwd if layer % 2 == 0 else rg_layer_bwd
        dact, grads[("mix", layer)] = mixer_bwd(f"l{layer}_mix", dact, saved_mix, mix_w)
    attn_g = [grads[("mix", l)] for l in range(0, DEPTH, 2)]
    rg_g = [grads[("mix", l)] for l in range(1, DEPTH, 2)]
    ffn_g = [grads[("ffn", l)] for l in range(DEPTH)]
    stack = lambda gs, key: jnp.stack([g[key] for g in gs])
    contrib = dict(
        attn_w_in=stack(attn_g, "w_in"), attn_w_out=stack(attn_g, "w_out"), attn_rel_bias=stack(attn_g, "rel_bias"),
        rg_w_in=stack(rg_g, "w_in"), rg_conv_w=stack(rg_g, "conv_w")[:, :, None, :],
        rg_conv_b=stack(rg_g, "conv_b")[:, 0], rg_w_a=stack(rg_g, "w_a"), rg_b_a=stack(rg_g, "b_a").reshape(rg_b_a.shape[0], LRU_BLOCKS, -1),
        rg_w_i=stack(rg_g, "w_i"), rg_b_i=stack(rg_g, "b_i").reshape(rg_b_i.shape[0], LRU_BLOCKS, -1),
        rg_lambda=stack(rg_g, "lam")[:, 0], rg_w_out=stack(rg_g, "w_out"),
        norm_mix_pre=jnp.concatenate([grads[("mix", l)]["g_pre"] for l in range(DEPTH)]),
        norm_mix_post=jnp.concatenate([grads[("mix", l)]["g_post"] for l in range(DEPTH)]),
        norm_ffn_pre=jnp.concatenate([g["g_pre"] for g in ffn_g]),
        norm_ffn_post=jnp.concatenate([g["g_post"] for g in ffn_g]),
    )
    blocked_g = {n: _to_blocked(contrib[n], axis_of[n]) for n in sharded if not n.startswith("ffn_")}
    for n, key in (("ffn_w_gate", "w_gate"), ("ffn_w_up", "w_up"), ("ffn_w_down", "w_down")):
        blocked_g[n] = jnp.stack([g[key] for g in ffn_g], axis=1)

    gbuf = _pack_blocked([blocked_g[n] for n in sharded], F32, PACK_ROWS)
    rows = gbuf.shape[1]
    g4 = gbuf.reshape(4, 2, rows, LANES)
    core = lax.axis_index("c").astype(jnp.int32).reshape(1)
    from_sibling = exchange_pair("rs_pair", g4)
    pair = pair_sum("rs_pair_sum", g4, from_sibling, core)
    by_chip = exchange_chips("rs_chips", pair)
    outs_s = adamw("adamw_sharded", by_chip, *[_pack([d[n] for n in sharded], F32, PACK_ROWS)
                                                 for d in (w_loc, m_loc, v_loc)])
    rep_shapes = [w_loc[n].shape for n in REPLICATED]
    rep_parts = all_gather("gather_rep_grads", _pack([contrib[n] for n in REPLICATED], F32, 8))
    outs_r = adamw("adamw_replicated", rep_parts, *[_pack([d[n] for n in REPLICATED], F32, 8)
                                                    for d in (w_loc, m_loc, v_loc)])

    result = {}
    for kind, buf_s, buf_r in zip(("grad", "delta", "new_m", "new_v"), outs_s, outs_r):
        for n, a in zip(sharded, _unpack(buf_s, shard_shapes)):
            result[(kind, n)] = a
        for n, a in zip(REPLICATED, _unpack(buf_r, rep_shapes)):
            result[(kind, n)] = a
    return (loss, dact[None], *[result[(kind, n)] for kind in ("grad", "delta", "new_m", "new_v") for n in WEIGHTS])
```

```python
import functools
import math

import jax
import jax.numpy as jnp
from jax import lax
from jax.experimental import pallas as pl
from jax.experimental.pallas import tpu as pltpu

F32 = jnp.float32
BF16 = jnp.bfloat16

N_DEV = 8
DEPTH = 4
CHUNK = 64
N_LEFT = 8
BAND = (N_LEFT + 1) * CHUNK
PAD_KEYS = N_LEFT * CHUNK
HEAD_DIM = 64
N_HEADS = 8
REL_CLIP = 256
LRU_BLOCKS = 4
LRU_C = 8.0
RMS_EPS = 1e-6
QK_SCALE = HEAD_DIM ** -0.5

ADAM_LR = 0.001
ADAM_B1 = 0.9
ADAM_B2 = 0.999
ADAM_EPS = 1e-08
ADAM_WD = 0.01
ADAM_STEP = 10

LANES = 1024
V7X_VMEM_LIMIT = 56 * 1024 * 1024

MESH = pl.DeviceIdType.MESH
ANY = pl.BlockSpec(memory_space=pl.ANY)

SHARDED = (
    ("attn_w_in", 2), ("attn_w_out", 1), ("rg_w_in", 2), ("rg_conv_w", 3), ("rg_conv_b", 1),
    ("rg_w_a", 2), ("rg_b_a", 2), ("rg_w_i", 2), ("rg_b_i", 2), ("rg_lambda", 1), ("rg_w_out", 1),
    ("ffn_w_gate", 2), ("ffn_w_up", 2), ("ffn_w_down", 1),
)
REPLICATED = ("attn_rel_bias", "norm_mix_pre", "norm_mix_post", "norm_ffn_pre", "norm_ffn_post")
WEIGHTS = ("attn_w_in", "attn_rel_bias", "attn_w_out", "rg_w_in", "rg_conv_w", "rg_conv_b", "rg_w_a", "rg_b_a",
           "rg_w_i", "rg_b_i", "rg_lambda", "rg_w_out", "norm_mix_pre", "norm_mix_post", "norm_ffn_pre",
           "norm_ffn_post", "ffn_w_gate", "ffn_w_up", "ffn_w_down")


def _params(*dims):
    return pltpu.CompilerParams(dimension_semantics=dims or None, vmem_limit_bytes=V7X_VMEM_LIMIT)


def _sds(shape, dtype):
    return jax.ShapeDtypeStruct(tuple(shape), dtype)


def _row_tile(n, pref):
    t = min(n, pref)
    assert n % t == 0, (n, pref)
    return t


NN = (((1,), (0,)), ((), ()))
NT = (((1,), (1,)), ((), ()))
TN = (((0,), (0,)), ((), ()))


def _gmm(name, a, b, *, grid, a_blk, a_idx, b_blk, b_idx, o_blk, o_idx, out_shape, out_dtype, dn, acc_shape):
    nk = grid[-1]
    kax = len(grid) - 1

    def body(a_ref, b_ref, o_ref, acc_ref):
        part = lax.dot_general(a_ref[...], b_ref[...], dn, preferred_element_type=F32)
        if nk == 1:
            o_ref[...] = part.astype(o_ref.dtype)
            return
        k = pl.program_id(kax)

        @pl.when(k == 0)
        def _():
            acc_ref[...] = part

        @pl.when(k > 0)
        def _():
            acc_ref[...] += part

        @pl.when(k == nk - 1)
        def _():
            o_ref[...] = acc_ref[...].astype(o_ref.dtype)

    return pl.pallas_call(
        body, grid=grid,
        in_specs=[pl.BlockSpec(a_blk, a_idx), pl.BlockSpec(b_blk, b_idx)],
        out_specs=pl.BlockSpec(o_blk, o_idx),
        out_shape=_sds(out_shape, out_dtype),
        scratch_shapes=[pltpu.VMEM(acc_shape, F32)],
        compiler_params=_params(*(["parallel"] * kax + ["arbitrary"])),
        name=name,
    )(a, b)


def mm_nn(name, a, b, out_dtype, tm=1024, tn=512, tk=1024):
    (m, k), (_, n) = a.shape, b.shape
    tm, tn, tk = _row_tile(m, tm), _row_tile(n, tn), _row_tile(k, tk)
    return _gmm(name, a, b, grid=(m // tm, n // tn, k // tk),
                a_blk=(tm, tk), a_idx=lambda i, j, kk: (i, kk), b_blk=(tk, tn), b_idx=lambda i, j, kk: (kk, j),
                o_blk=(tm, tn), o_idx=lambda i, j, kk: (i, j), out_shape=(m, n), out_dtype=out_dtype, dn=NN,
                acc_shape=(tm, tn))


def mm_nt(name, a, b, out_dtype, tm=1024, tn=512, tk=1024):
    (m, k), (n, _) = a.shape, b.shape
    tm, tn, tk = _row_tile(m, tm), _row_tile(n, tn), _row_tile(k, tk)
    return _gmm(name, a, b, grid=(m // tm, n // tn, k // tk),
                a_blk=(tm, tk), a_idx=lambda i, j, kk: (i, kk), b_blk=(tn, tk), b_idx=lambda i, j, kk: (j, kk),
                o_blk=(tm, tn), o_idx=lambda i, j, kk: (i, j), out_shape=(m, n), out_dtype=out_dtype, dn=NT,
                acc_shape=(tm, tn))


def mm_tn(name, a, b, out_dtype, tm=512, tn=512, tk=1024):
    (k, m), (_, n) = a.shape, b.shape
    tm, tn, tk = _row_tile(m, tm), _row_tile(n, tn), _row_tile(k, tk)
    return _gmm(name, a, b, grid=(m // tm, n // tn, k // tk),
                a_blk=(tk, tm), a_idx=lambda i, j, kk: (kk, i), b_blk=(tk, tn), b_idx=lambda i, j, kk: (kk, j),
                o_blk=(tm, tn), o_idx=lambda i, j, kk: (i, j), out_shape=(m, n), out_dtype=out_dtype, dn=TN,
                acc_shape=(tm, tn))


def mm_nn_wblk(name, a, wb, layer, out_dtype, tm=1024, tk=1024):
    (m, k), (nb, _, _, n8) = a.shape, wb.shape
    tm, tk = _row_tile(m, tm), _row_tile(k, tk)
    return _gmm(name, a, wb, grid=(m // tm, nb, k // tk),
                a_blk=(tm, tk), a_idx=lambda i, j, kk: (i, kk),
                b_blk=(None, None, tk, n8), b_idx=lambda i, j, kk: (j, layer, kk, 0),
                o_blk=(tm, n8), o_idx=lambda i, j, kk: (i, j), out_shape=(m, nb * n8), out_dtype=out_dtype, dn=NN,
                acc_shape=(tm, n8))


def mm_nt_wblk(name, a, wb, layer, out_dtype, tm=1024, tn=512):
    m = a.shape[0]
    nb, _, k, n8 = wb.shape
    tm, tn = _row_tile(m, tm), _row_tile(k, tn)
    return _gmm(name, a, wb, grid=(m // tm, k // tn, nb),
                a_blk=(tm, n8), a_idx=lambda i, j, kk: (i, kk),
                b_blk=(None, None, tn, n8), b_idx=lambda i, j, kk: (kk, layer, j, 0),
                o_blk=(tm, tn), o_idx=lambda i, j, kk: (i, j), out_shape=(m, k), out_dtype=out_dtype, dn=NT,
                acc_shape=(tm, tn))


def mm_tn_oblk(name, a, b, n8, out_dtype, tk=1024):
    (t, k), nb = a.shape, b.shape[1] // n8
    tk = _row_tile(t, tk)
    return _gmm(name, a, b, grid=(nb, t // tk),
                a_blk=(tk, k), a_idx=lambda j, s: (s, 0), b_blk=(tk, n8), b_idx=lambda j, s: (s, j),
                o_blk=(None, k, n8), o_idx=lambda j, s: (j, 0, 0), out_shape=(nb, k, n8), out_dtype=out_dtype, dn=TN,
                acc_shape=(k, n8))


def rmsnorm_fwd(name, x, g):
    t, d = x.shape
    tr = _row_tile(t, 512)

    def body(x_ref, g_ref, o_ref):
        xv = x_ref[...]
        r = lax.rsqrt(jnp.mean(xv * xv, axis=-1, keepdims=True) + RMS_EPS)
        o_ref[...] = (xv * r * g_ref[...]).astype(o_ref.dtype)

    return pl.pallas_call(
        body, grid=(t // tr,),
        in_specs=[pl.BlockSpec((tr, d), lambda i: (i, 0)), pl.BlockSpec((1, d), lambda i: (0, 0))],
        out_specs=pl.BlockSpec((tr, d), lambda i: (i, 0)),
        out_shape=_sds((t, d), BF16), compiler_params=_params("parallel"), name=name)(x, g)


def resid_norm_fwd(name, x, m, g):
    t, d = x.shape
    tr = _row_tile(t, 512)

    def body(x_ref, m_ref, g_ref, o_ref):
        mv = m_ref[...]
        r = lax.rsqrt(jnp.mean(mv * mv, axis=-1, keepdims=True) + RMS_EPS)
        o_ref[...] = x_ref[...] + mv * r * g_ref[...]

    return pl.pallas_call(
        body, grid=(t // tr,),
        in_specs=[pl.BlockSpec((tr, d), lambda i: (i, 0)), pl.BlockSpec((tr, d), lambda i: (i, 0)),
                  pl.BlockSpec((1, d), lambda i: (0, 0))],
        out_specs=pl.BlockSpec((tr, d), lambda i: (i, 0)),
        out_shape=_sds((t, d), F32), compiler_params=_params("parallel"), name=name)(x, m, g)


def norm_bwd(name, dy, x, g, resid, out_dtype):
    t, d = x.shape
    tr = _row_tile(t, 512)
    has_res = resid is not None

    def body(*refs):
        if has_res:
            dy_ref, x_ref, g_ref, r_ref, dx_ref, dg_ref = refs
        else:
            dy_ref, x_ref, g_ref, dx_ref, dg_ref = refs
        i = pl.program_id(0)
        xv = x_ref[...]
        dyv = dy_ref[...].astype(F32)
        r = lax.rsqrt(jnp.mean(xv * xv, axis=-1, keepdims=True) + RMS_EPS)
        xh = xv * r
        dxh = dyv * g_ref[...]
        dx = r * (dxh - xh * jnp.mean(dxh * xh, axis=-1, keepdims=True))
        if has_res:
            dx = dx + r_ref[...]
        dx_ref[...] = dx.astype(dx_ref.dtype)
        part = jnp.sum(dyv * xh, axis=0, keepdims=True)

        @pl.when(i == 0)
        def _():
            dg_ref[...] = part

        @pl.when(i > 0)
        def _():
            dg_ref[...] += part

    row = pl.BlockSpec((tr, d), lambda i: (i, 0))
    vec = pl.BlockSpec((1, d), lambda i: (0, 0))
    ins = [dy, x, g] + ([resid] if has_res else [])
    return pl.pallas_call(
        body, grid=(t // tr,),
        in_specs=[row, row, vec] + ([row] if has_res else []),
        out_specs=[row, vec],
        out_shape=[_sds((t, d), out_dtype), _sds((1, d), F32)],
        compiler_params=_params("arbitrary"), name=name)(*ins)


def loss_grad(name, y, target):
    t, d = y.shape
    tr = _row_tile(t, 512)

    def body(y_ref, t_ref, dy_ref, s_ref):
        i = pl.program_id(0)
        err = y_ref[...] - t_ref[...]
        dy_ref[...] = err * (1.0 / d)
        part = jnp.sum(err * err, axis=0, keepdims=True)

        @pl.when(i == 0)
        def _():
            s_ref[...] = part

        @pl.when(i > 0)
        def _():
            s_ref[...] += part

    row = pl.BlockSpec((tr, d), lambda i: (i, 0))
    vec = pl.BlockSpec((1, d), lambda i: (0, 0))
    return pl.pallas_call(
        body, grid=(t // tr,), in_specs=[row, row], out_specs=[row, vec],
        out_shape=[_sds((t, d), F32), _sds((1, d), F32)],
        compiler_params=_params("arbitrary"), name=name)(y, target)


A_TQ = 512


def _a_scores(qc, kwin, bias, c):
    s = lax.dot_general(qc, kwin, NT, preferred_element_type=F32) + bias
    col = lax.broadcasted_iota(jnp.int32, s.shape, 1)
    valid = col >= (N_LEFT - c) * CHUNK
    return s, valid


def attn_a_fwd(name, q, kp, vp, bias):
    h, t, dh = q.shape
    tq = _row_tile(t, A_TQ)
    ncs = tq // CHUNK

    def body(q_ref, k_ref, v_ref, b_ref, o_ref, l_ref):
        i = pl.program_id(1)
        bias_h = b_ref[...]

        def chunk(cc, carry):
            c = i * ncs + cc
            r0 = pl.multiple_of(cc * CHUNK, CHUNK)
            k0 = pl.multiple_of(c * CHUNK, CHUNK)
            qc = q_ref[pl.ds(r0, CHUNK), :]
            kwin = k_ref[pl.ds(k0, BAND), :]
            vwin = v_ref[pl.ds(k0, BAND), :]
            s, valid = _a_scores(qc, kwin, bias_h, c)
            s = jnp.where(valid, s, -1e30)
            mx = jnp.max(s, axis=-1, keepdims=True)
            p = jnp.exp(s - mx)
            den = jnp.sum(p, axis=-1, keepdims=True)
            p = p * (1.0 / den)
            o_ref[pl.ds(r0, CHUNK), :] = jnp.dot(p.astype(BF16), vwin, preferred_element_type=F32)
            l_ref[pl.ds(r0, CHUNK), :] = mx + jnp.log(den)
            return carry

        lax.fori_loop(0, ncs, chunk, 0)

    return pl.pallas_call(
        body, grid=(h, t // tq),
        in_specs=[pl.BlockSpec((None, tq, dh), lambda hh, i: (hh, i, 0)),
                  pl.BlockSpec((None, t + PAD_KEYS, dh), lambda hh, i: (hh, 0, 0)),
                  pl.BlockSpec((None, t + PAD_KEYS, dh), lambda hh, i: (hh, 0, 0)),
                  pl.BlockSpec((None, CHUNK, BAND), lambda hh, i: (hh, 0, 0))],
        out_specs=[pl.BlockSpec((None, tq, dh), lambda hh, i: (hh, i, 0)),
                   pl.BlockSpec((None, tq, 1), lambda hh, i: (hh, i, 0))],
        out_shape=[_sds((h, t, dh), F32), _sds((h, t, 1), F32)],
        compiler_params=_params("parallel", "parallel"), name=name)(q, kp, vp, bias)


def attn_a_bwd(name, q, kp, vp, bias, o, lse, do):
    h, t, dh = q.shape
    tq = _row_tile(t, A_TQ)
    ncs = tq // CHUNK

    def body(q_ref, k_ref, v_ref, b_ref, o_ref, l_ref, do_ref, dq_ref, dk_ref, dv_ref, db_ref):
        i = pl.program_id(1)

        @pl.when(i == 0)
        def _():
            dk_ref[...] = jnp.zeros_like(dk_ref)
            dv_ref[...] = jnp.zeros_like(dv_ref)
            db_ref[...] = jnp.zeros_like(db_ref)

        bias_h = b_ref[...]

        def chunk(cc, carry):
            c = i * ncs + cc
            r0 = pl.multiple_of(cc * CHUNK, CHUNK)
            k0 = pl.multiple_of(c * CHUNK, CHUNK)
            qc = q_ref[pl.ds(r0, CHUNK), :]
            doc = do_ref[pl.ds(r0, CHUNK), :]
            kwin = k_ref[pl.ds(k0, BAND), :]
            vwin = v_ref[pl.ds(k0, BAND), :]
            s, valid = _a_scores(qc, kwin, bias_h, c)
            p = jnp.where(valid, jnp.exp(s - l_ref[pl.ds(r0, CHUNK), :]), 0.0)
            dp = lax.dot_general(doc, vwin, NT, preferred_element_type=F32)
            delta = jnp.sum(doc.astype(F32) * o_ref[pl.ds(r0, CHUNK), :], axis=-1, keepdims=True)
            ds = p * (dp - delta)
            db_ref[...] += ds
            dsb = ds.astype(BF16)
            dq_ref[pl.ds(r0, CHUNK), :] = jnp.dot(dsb, kwin, preferred_element_type=F32) * QK_SCALE
            dk_ref[pl.ds(k0, BAND), :] += lax.dot_general(dsb, qc, TN, preferred_element_type=F32)
            dv_ref[pl.ds(k0, BAND), :] += lax.dot_general(p.astype(BF16), doc, TN, preferred_element_type=F32)
            return carry

        lax.fori_loop(0, ncs, chunk, 0)

    tile = lambda w: pl.BlockSpec((None, tq, w), lambda hh, i: (hh, i, 0))
    whole = pl.BlockSpec((None, t + PAD_KEYS, dh), lambda hh, i: (hh, 0, 0))
    bspec = pl.BlockSpec((None, CHUNK, BAND), lambda hh, i: (hh, 0, 0))
    return pl.pallas_call(
        body, grid=(h, t // tq),
        in_specs=[tile(dh), whole, whole, bspec, tile(dh), tile(1), tile(dh)],
        out_specs=[tile(dh), whole, whole, bspec],
        out_shape=[_sds((h, t, dh), F32), _sds((h, t + PAD_KEYS, dh), F32), _sds((h, t + PAD_KEYS, dh), F32),
                   _sds((h, CHUNK, BAND), F32)],
        compiler_params=_params("parallel", "arbitrary"), name=name)(q, kp, vp, bias, o, lse, do)


SB_TQ = 1024
SB_TK = 256


def _tri(n, strict):
    j = lax.broadcasted_iota(jnp.int32, (n, n), 0)
    s = lax.broadcasted_iota(jnp.int32, (n, n), 1)
    return jnp.where((j > s) if strict else (j >= s), 1.0, 0.0).astype(BF16)


def _suffix_sum(x, tri, exact):
    hi = x.astype(BF16)
    out = jnp.dot(hi, tri, preferred_element_type=F32)
    if exact:
        lo = (x - hi.astype(F32)).astype(BF16)
        out = out + jnp.dot(lo, tri, preferred_element_type=F32)
    return out


def _sb_block(qv, ks, cm, tri_strict, off):
    z = lax.dot_general(qv, ks, NT, preferred_element_type=F32)
    lb = jnp.minimum(z, 0.0) - jnp.log(1.0 + jnp.exp(-jnp.abs(z)))
    m = lb - z
    causal = None
    if off is not None:
        tq, tk = z.shape
        causal = (lax.broadcasted_iota(jnp.int32, (tq, tk), 1) + off * tk) < lax.broadcasted_iota(jnp.int32, (tq, tk), 0)
        m = jnp.where(causal, m, 0.0)
    after = _suffix_sum(m, tri_strict, False) + cm
    w = jnp.exp(lb + after)
    if off is not None:
        w = jnp.where(causal, w, 0.0)
    return causal, lb, m, w


def sb_fwd(name, q, k, v):
    h, t, dh = q.shape
    tq = _row_tile(t, SB_TQ)
    tk = min(SB_TK, tq)
    per = tq // tk

    def body(q_ref, k_ref, v_ref, o_ref):
        i = pl.program_id(1)
        qv = q_ref[...]
        tri = _tri(tk, True)

        def block(kb, acc, cm, off):
            k0 = pl.multiple_of(kb * tk, tk)
            _, _, m, w = _sb_block(qv, k_ref[pl.ds(k0, tk), :], cm, tri, off)
            acc = acc + jnp.dot(w.astype(BF16), v_ref[pl.ds(k0, tk), :], preferred_element_type=F32)
            return acc, cm + jnp.sum(m, axis=-1, keepdims=True)

        carry = (jnp.zeros((tq, dh), F32), jnp.zeros((tq, 1), F32))
        for off in reversed(range(per)):
            carry = block(i * per + off, *carry, off)
        acc, _ = lax.fori_loop(0, i * per, lambda n, c: block(i * per - 1 - n, *c, None), carry)
        o_ref[...] = acc

    tile = pl.BlockSpec((None, tq, dh), lambda hh, i: (hh, i, 0))
    whole = pl.BlockSpec((None, t, dh), lambda hh, i: (hh, 0, 0))
    return pl.pallas_call(
        body, grid=(h, t // tq), in_specs=[tile, whole, whole], out_specs=tile,
        out_shape=_sds((h, t, dh), F32), compiler_params=_params("parallel", "parallel"), name=name)(q, k, v)


def sb_bwd(name, q, k, v, o, do):
    h, t, dh = q.shape
    tq = _row_tile(t, SB_TQ)
    tk = min(SB_TK, tq)
    per = tq // tk

    def body(q_ref, k_ref, v_ref, o_ref, do_ref, dq_ref, dk_ref, dv_ref):
        i = pl.program_id(1)

        @pl.when(i == 0)
        def _():
            dk_ref[...] = jnp.zeros_like(dk_ref)
            dv_ref[...] = jnp.zeros_like(dv_ref)

        qv = q_ref[...]
        dov = do_ref[...]
        delta = jnp.sum(dov.astype(F32) * o_ref[...], axis=-1, keepdims=True)
        tri_s, tri_i = _tri(tk, True), _tri(tk, False)

        def block(kb, dq, cm, ce, off):
            k0 = pl.multiple_of(kb * tk, tk)
            ks = k_ref[pl.ds(k0, tk), :]
            vs = v_ref[pl.ds(k0, tk), :]
            dw = lax.dot_general(dov, vs, NT, preferred_element_type=F32)
            causal, lb, m, w = _sb_block(qv, ks, cm, tri_s, off)
            wb = w.astype(BF16)
            e = wb.astype(F32) * dw
            left = delta - (_suffix_sum(e, tri_i, True) + ce)
            sig = jnp.exp(lb)
            dz = e * (1.0 - sig) - left * sig
            if off is not None:
                dz = jnp.where(causal, dz, 0.0)
            dz = dz.astype(BF16)
            dq = dq + jnp.dot(dz, ks, preferred_element_type=F32)
            dk_ref[pl.ds(k0, tk), :] += lax.dot_general(dz, qv, TN, preferred_element_type=F32)
            dv_ref[pl.ds(k0, tk), :] += lax.dot_general(wb, dov, TN, preferred_element_type=F32)
            return dq, cm + jnp.sum(m, axis=-1, keepdims=True), ce + jnp.sum(e, axis=-1, keepdims=True)

        zero = jnp.zeros((tq, 1), F32)
        carry = (jnp.zeros((tq, dh), F32), zero, zero)
        for off in reversed(range(per)):
            carry = block(i * per + off, *carry, off)
        dq, _, _ = lax.fori_loop(0, i * per, lambda n, c: block(i * per - 1 - n, *c, None), carry)
        dq_ref[...] = dq * QK_SCALE

    tile = pl.BlockSpec((None, tq, dh), lambda hh, i: (hh, i, 0))
    whole = pl.BlockSpec((None, t, dh), lambda hh, i: (hh, 0, 0))
    return pl.pallas_call(
        body, grid=(h, t // tq), in_specs=[tile, whole, whole, tile, tile], out_specs=[tile, whole, whole],
        out_shape=[_sds((h, t, dh), F32)] * 3,
        compiler_params=_params("parallel", "arbitrary"), name=name)(q, k, v, o, do)


def _sigmoid(x):
    return 1.0 / (1.0 + jnp.exp(-x))


def ffn_up(name, h, wg, wu, layer):
    t, d = h.shape
    nb, _, _, f8 = wg.shape
    tm = _row_tile(t, 1024)

    def body(h_ref, wg_ref, wu_ref, g_ref, u_ref, a_ref):
        hv = h_ref[...]
        g = jnp.dot(hv, wg_ref[...], preferred_element_type=F32)
        u = jnp.dot(hv, wu_ref[...], preferred_element_type=F32)
        g_ref[...] = g.astype(BF16)
        u_ref[...] = u.astype(BF16)
        a_ref[...] = (g * _sigmoid(g) * u).astype(BF16)

    wspec = pl.BlockSpec((None, None, d, f8), lambda i, k: (k, layer, 0, 0))
    ospec = pl.BlockSpec((None, tm, f8), lambda i, k: (k, i, 0))
    return pl.pallas_call(
        body, grid=(t // tm, nb), in_specs=[pl.BlockSpec((tm, d), lambda i, k: (i, 0)), wspec, wspec],
        out_specs=[ospec] * 3, out_shape=[_sds((nb, t, f8), BF16)] * 3,
        compiler_params=_params("parallel", "parallel"), name=name)(h, wg, wu)


def ffn_down(name, a, wd, layer):
    nb, t, f8 = a.shape
    d = wd.shape[3]
    tm = _row_tile(t, 1024)
    return _gmm(name, a, wd, grid=(t // tm, nb),
                a_blk=(None, tm, f8), a_idx=lambda i, k: (k, i, 0),
                b_blk=(None, None, f8, d), b_idx=lambda i, k: (k, layer, 0, 0),
                o_blk=(tm, d), o_idx=lambda i, k: (i, 0), out_shape=(t, d), out_dtype=F32, dn=NN, acc_shape=(tm, d))


def ffn_bwd_act(name, dm, wd, g, u, layer):
    t, d = dm.shape
    nb, _, f8, _ = wd.shape
    tm = _row_tile(t, 1024)

    def body(dm_ref, wd_ref, g_ref, u_ref, dg_ref, du_ref):
        da = lax.dot_general(dm_ref[...], wd_ref[...], NT, preferred_element_type=F32)
        gv = g_ref[...].astype(F32)
        uv = u_ref[...].astype(F32)
        sg = _sigmoid(gv)
        dg_ref[...] = (da * uv * sg * (1.0 + gv * (1.0 - sg))).astype(BF16)
        du_ref[...] = (da * gv * sg).astype(BF16)

    bspec = pl.BlockSpec((None, tm, f8), lambda i, k: (k, i, 0))
    return pl.pallas_call(
        body, grid=(t // tm, nb),
        in_specs=[pl.BlockSpec((tm, d), lambda i, k: (i, 0)),
                  pl.BlockSpec((None, None, f8, d), lambda i, k: (k, layer, 0, 0)), bspec, bspec],
        out_specs=[bspec] * 2, out_shape=[_sds((nb, t, f8), BF16)] * 2,
        compiler_params=_params("parallel", "parallel"), name=name)(dm, wd, g, u)


def ffn_bwd_dh(name, dg, du, wg, wu, layer):
    nb, t, f8 = dg.shape
    d = wg.shape[2]
    tm = _row_tile(t, 1024)

    def body(dg_ref, du_ref, wg_ref, wu_ref, o_ref):
        k = pl.program_id(1)
        part = (lax.dot_general(dg_ref[...], wg_ref[...], NT, preferred_element_type=F32)
                + lax.dot_general(du_ref[...], wu_ref[...], NT, preferred_element_type=F32))

        @pl.when(k == 0)
        def _():
            o_ref[...] = part

        @pl.when(k > 0)
        def _():
            o_ref[...] += part

    bspec = pl.BlockSpec((None, tm, f8), lambda i, k: (k, i, 0))
    wspec = pl.BlockSpec((None, None, d, f8), lambda i, k: (k, layer, 0, 0))
    return pl.pallas_call(
        body, grid=(t // tm, nb), in_specs=[bspec, bspec, wspec, wspec],
        out_specs=pl.BlockSpec((tm, d), lambda i, k: (i, 0)), out_shape=_sds((t, d), F32),
        compiler_params=_params("parallel", "arbitrary"), name=name)(dg, du, wg, wu)


def ffn_dw_in(name, h, dact):
    t, d = h.shape
    nb, _, f8 = dact.shape
    tk = _row_tile(t, 1024)
    return _gmm(name, h, dact, grid=(nb, t // tk),
                a_blk=(tk, d), a_idx=lambda b, s: (s, 0), b_blk=(None, tk, f8), b_idx=lambda b, s: (b, s, 0),
                o_blk=(None, d, f8), o_idx=lambda b, s: (b, 0, 0), out_shape=(nb, d, f8), out_dtype=F32, dn=TN,
                acc_shape=(d, f8))


def ffn_dw_down(name, a, dm):
    nb, t, f8 = a.shape
    d = dm.shape[1]
    tk = _row_tile(t, 1024)
    return _gmm(name, a, dm, grid=(nb, t // tk),
                a_blk=(None, tk, f8), a_idx=lambda b, s: (b, s, 0), b_blk=(tk, d), b_idx=lambda b, s: (s, 0),
                o_blk=(None, f8, d), o_idx=lambda b, s: (b, 0, 0), out_shape=(nb, f8, d), out_dtype=F32, dn=TN,
                acc_shape=(f8, d))


GELU_C = math.sqrt(2.0 / math.pi)
GELU_A = 0.044715


def _gelu(x):
    return 0.5 * x * (1.0 + jnp.tanh(GELU_C * (x + GELU_A * x * x * x)))


def _gelu_grad(x):
    th = jnp.tanh(GELU_C * (x + GELU_A * x * x * x))
    return 0.5 * (1.0 + th) + 0.5 * x * (1.0 - th * th) * GELU_C * (1.0 + 3.0 * GELU_A * x * x)


def _neg_expm1(x):
    series = x * (1.0 + x * (0.5 + x * (1.0 / 6.0 + x * (1.0 / 24.0 + x * (1.0 / 120.0 + x * (1.0 / 720.0))))))
    return -jnp.where(x > -0.25, series, jnp.exp(x) - 1.0)


def fir4(name, xs, w, b):
    nj, t, c = xs.shape
    tr = _row_tile(t, 256)

    def body(xs_ref, w_ref, b_ref, o_ref):
        acc = b_ref[...] + w_ref[0:1, :] * xs_ref[0]
        for j in range(1, nj):
            acc = acc + w_ref[j:j + 1, :] * xs_ref[j]
        o_ref[...] = acc

    return pl.pallas_call(
        body, grid=(t // tr,),
        in_specs=[pl.BlockSpec((nj, tr, c), lambda i: (0, i, 0)), pl.BlockSpec((nj, c), lambda i: (0, 0)),
                  pl.BlockSpec((1, c), lambda i: (0, 0))],
        out_specs=pl.BlockSpec((tr, c), lambda i: (i, 0)), out_shape=_sds((t, c), F32),
        compiler_params=_params("parallel"), name=name)(xs, w, b)


def fir4_bwd(name, xs, dy):
    nj, t, c = xs.shape
    tr = _row_tile(t, 256)

    def body(xs_ref, dy_ref, dw_ref, db_ref):
        i = pl.program_id(0)

        @pl.when(i == 0)
        def _():
            dw_ref[...] = jnp.zeros_like(dw_ref)
            db_ref[...] = jnp.zeros_like(db_ref)

        dyv = dy_ref[...]
        db_ref[...] += jnp.sum(dyv, axis=0, keepdims=True)
        for j in range(nj):
            dw_ref[j:j + 1, :] += jnp.sum(dyv * xs_ref[j], axis=0, keepdims=True)

    return pl.pallas_call(
        body, grid=(t // tr,),
        in_specs=[pl.BlockSpec((nj, tr, c), lambda i: (0, i, 0)), pl.BlockSpec((tr, c), lambda i: (i, 0))],
        out_specs=[pl.BlockSpec((nj, c), lambda i: (0, 0)), pl.BlockSpec((1, c), lambda i: (0, 0))],
        out_shape=[_sds((nj, c), F32), _sds((1, c), F32)],
        compiler_params=_params("arbitrary"), name=name)(xs, dy)


def _rg_gate_values(xcv, wa_ref, wi_ref, ba_ref, bi_ref, lam_ref):
    xb = xcv.astype(BF16)
    r = _sigmoid(jnp.dot(xb, wa_ref[...], preferred_element_type=F32) + ba_ref[...])
    ig = _sigmoid(jnp.dot(xb, wi_ref[...], preferred_element_type=F32) + bi_ref[...])
    lam = lam_ref[...]
    sp = jnp.maximum(-lam, 0.0) + jnp.log(1.0 + jnp.exp(-jnp.abs(lam)))
    log_a = -LRU_C * r * sp
    a = jnp.exp(log_a)
    mult = jnp.sqrt(_neg_expm1(2.0 * log_a))
    return xb, r, ig, sp, a, mult


def rg_gates_fwd(name, xc, wa, wi, ba, bi, lam):
    t, c = xc.shape
    nb, cb, _ = wa.shape
    tm = _row_tile(t, 512)

    def body(xc_ref, wa_ref, wi_ref, ba_ref, bi_ref, lam_ref, a_ref, u_ref):
        xcv = xc_ref[...]
        _, _, ig, _, a, mult = _rg_gate_values(xcv, wa_ref, wi_ref, ba_ref, bi_ref, lam_ref)
        a_ref[...] = a
        u_ref[...] = mult * (ig * xcv)

    blk = pl.BlockSpec((tm, cb), lambda i, n: (i, n))
    wsp = pl.BlockSpec((None, cb, cb), lambda i, n: (n, 0, 0))
    vec = pl.BlockSpec((1, cb), lambda i, n: (0, n))
    return pl.pallas_call(
        body, grid=(t // tm, nb), in_specs=[blk, wsp, wsp, vec, vec, vec], out_specs=[blk, blk],
        out_shape=[_sds((t, c), F32)] * 2, compiler_params=_params("parallel", "parallel"), name=name,
    )(xc, wa, wi, ba, bi, lam)


def rg_gates_bwd(name, xc, gu, hprev, wa, wi, ba, bi, lam):
    t, c = xc.shape
    nb, cb, _ = wa.shape
    tm = _row_tile(t, 512)

    def body(xc_ref, gu_ref, hp_ref, wa_ref, wi_ref, ba_ref, bi_ref, lam_ref,
             dxc_ref, dwa_ref, dwi_ref, dba_ref, dbi_ref, dlam_ref):
        i = pl.program_id(1)

        @pl.when(i == 0)
        def _():
            for ref in (dwa_ref, dwi_ref, dba_ref, dbi_ref, dlam_ref):
                ref[...] = jnp.zeros_like(ref)

        xcv = xc_ref[...]
        xb, r, ig, sp, a, mult = _rg_gate_values(xcv, wa_ref, wi_ref, ba_ref, bi_ref, lam_ref)
        gv = gu_ref[...]
        d_ixc = gv * mult
        d_i = d_ixc * xcv
        d_mult = gv * ig * xcv
        d_a = gv * hp_ref[...] - d_mult * a / mult
        d_log_a = d_a * a
        d_r = d_log_a * (-LRU_C * sp)
        sig_neg_lam = 1.0 / (1.0 + jnp.exp(lam_ref[...]))
        dlam_ref[...] += jnp.sum(d_log_a * r, axis=0, keepdims=True) * (LRU_C * sig_neg_lam)
        dpa = d_r * r * (1.0 - r)
        dpi = d_i * ig * (1.0 - ig)
        dba_ref[...] += jnp.sum(dpa, axis=0, keepdims=True)
        dbi_ref[...] += jnp.sum(dpi, axis=0, keepdims=True)
        dpab, dpib = dpa.astype(BF16), dpi.astype(BF16)
        dxc_ref[...] = (d_ixc * ig + lax.dot_general(dpab, wa_ref[...], NT, preferred_element_type=F32)
                        + lax.dot_general(dpib, wi_ref[...], NT, preferred_element_type=F32))
        dwa_ref[...] += lax.dot_general(xb, dpab, TN, preferred_element_type=F32)
        dwi_ref[...] += lax.dot_general(xb, dpib, TN, preferred_element_type=F32)

    blk = pl.BlockSpec((tm, cb), lambda n, i: (i, n))
    wsp = pl.BlockSpec((None, cb, cb), lambda n, i: (n, 0, 0))
    vec = pl.BlockSpec((1, cb), lambda n, i: (0, n))
    return pl.pallas_call(
        body, grid=(nb, t // tm), in_specs=[blk, blk, blk, wsp, wsp, vec, vec, vec],
        out_specs=[blk, wsp, wsp, vec, vec, vec],
        out_shape=[_sds((t, c), F32), _sds((nb, cb, cb), F32), _sds((nb, cb, cb), F32),
                   _sds((1, c), F32), _sds((1, c), F32), _sds((1, c), F32)],
        compiler_params=_params("parallel", "arbitrary"), name=name)(xc, gu, hprev, wa, wi, ba, bi, lam)


SCAN_TS = 256
SCAN_TC = 512


def _tile_scan(a, b, reverse):
    ts = a.shape[0]
    row = lax.broadcasted_iota(jnp.int32, a.shape, 0)
    d = 1
    while d < ts:
        if reverse:
            inside = row < ts - d
            a_sh = jnp.where(inside, pltpu.roll(a, ts - d, 0), 1.0)
            b_sh = jnp.where(inside, pltpu.roll(b, ts - d, 0), 0.0)
        else:
            inside = row >= d
            a_sh = jnp.where(inside, pltpu.roll(a, d, 0), 1.0)
            b_sh = jnp.where(inside, pltpu.roll(b, d, 0), 0.0)
        b = b + a * b_sh
        a = a * a_sh
        d *= 2
    return a, b


def rg_scan_fwd(name, a, u, gate_pre):
    t, c = a.shape
    ts, tc = _row_tile(t, SCAN_TS), _row_tile(c, SCAN_TC)

    def body(a_ref, u_ref, g_ref, h_ref, z_ref, carry_ref):
        s = pl.program_id(1)

        @pl.when(s == 0)
        def _():
            carry_ref[...] = jnp.zeros_like(carry_ref)

        ac, bc = _tile_scan(a_ref[...], u_ref[...], False)
        h = bc + ac * carry_ref[0:1, :]
        h_ref[...] = h
        z_ref[...] = (h * _gelu(g_ref[...])).astype(BF16)
        carry_ref[0:1, :] = h[ts - 1:ts, :]

    blk = pl.BlockSpec((ts, tc), lambda j, s: (s, j))
    return pl.pallas_call(
        body, grid=(c // tc, t // ts), in_specs=[blk, blk, blk], out_specs=[blk, blk],
        out_shape=[_sds((t, c), F32), _sds((t, c), BF16)], scratch_shapes=[pltpu.VMEM((8, tc), F32)],
        compiler_params=_params("parallel", "arbitrary"), name=name)(a, u, gate_pre)


def rg_scan_bwd(name, a_next, hs, gate_pre, dz):
    t, c = hs.shape
    ts, tc = _row_tile(t, SCAN_TS), _row_tile(c, SCAN_TC)
    nt = t // ts

    def body(an_ref, h_ref, g_ref, dz_ref, gu_ref, dgate_ref, carry_ref):
        s = pl.program_id(1)

        @pl.when(s == 0)
        def _():
            carry_ref[...] = jnp.zeros_like(carry_ref)

        gate = g_ref[...]
        dzv = dz_ref[...]
        dgate_ref[...] = (dzv * h_ref[...] * _gelu_grad(gate)).astype(BF16)
        ac, bc = _tile_scan(an_ref[...], dzv * _gelu(gate), True)
        gu = bc + ac * carry_ref[0:1, :]
        gu_ref[...] = gu
        carry_ref[0:1, :] = gu[0:1, :]

    blk = pl.BlockSpec((ts, tc), lambda j, s: (nt - 1 - s, j))
    return pl.pallas_call(
        body, grid=(c // tc, nt), in_specs=[blk, blk, blk, blk], out_specs=[blk, blk],
        out_shape=[_sds((t, c), F32), _sds((t, c), BF16)], scratch_shapes=[pltpu.VMEM((8, tc), F32)],
        compiler_params=_params("parallel", "arbitrary"), name=name)(a_next, hs, gate_pre, dz)


def _shift_down(x, k):
    return jnp.pad(x, ((k, 0), (0, 0)))[:x.shape[0]] if k else x


def _shift_up(x, k):
    return jnp.pad(x, ((0, k), (0, 0)))[k:] if k else x


def _heads(x2d, n):
    t = x2d.shape[0]
    return x2d.reshape(t, n, N_HEADS, HEAD_DIM).transpose(1, 2, 0, 3)


def _unheads(x):
    n, h, t, d = x.shape
    return x.transpose(2, 0, 1, 3).reshape(t, n * h * d)


TOEP_W = 640
TOEP_FLAT = 320
TABLE_LOW = 193


def rel_bias_matrix(name, table):
    h = table.shape[0]
    diag = jnp.concatenate([jnp.repeat(table[:, 2 * REL_CLIP:], TOEP_FLAT, axis=1),
                            jnp.flip(table[:, TABLE_LOW:2 * REL_CLIP], axis=1),
                            jnp.zeros((h, 1), table.dtype)], axis=1)[:, None, :]

    def body(v_ref, o_ref):
        rows = jnp.broadcast_to(v_ref[...], (CHUNK, TOEP_W))
        o_ref[...] = pltpu.roll(rows, TOEP_W - (CHUNK - 1), 1, stride=1, stride_axis=0)

    out = pl.pallas_call(
        body, grid=(h,), in_specs=[pl.BlockSpec((None, 1, TOEP_W), lambda hh: (hh, 0, 0))],
        out_specs=pl.BlockSpec((None, CHUNK, TOEP_W), lambda hh: (hh, 0, 0)),
        out_shape=_sds((h, CHUNK, TOEP_W), F32), compiler_params=_params("parallel"), name=name)(diag)
    return out[:, :, :BAND]


def rel_bias_grad(name, dbias):
    h = dbias.shape[0]
    flipped = jnp.pad(jnp.flip(dbias, axis=1), ((0, 0), (0, 0), (0, TOEP_W - BAND)))

    def body(x_ref, o_ref):
        skew = pltpu.roll(x_ref[...], 0, 1, stride=1, stride_axis=0)
        col = jnp.sum(skew, axis=0, keepdims=True)
        lane = lax.broadcasted_iota(jnp.int32, col.shape, 1)
        flat = jnp.sum(jnp.where(lane < TOEP_FLAT, col, 0.0), axis=1, keepdims=True)
        o_ref[...] = jnp.where(lane == TOEP_W - 1, flat, col)

    out = pl.pallas_call(
        body, grid=(h,), in_specs=[pl.BlockSpec((None, CHUNK, TOEP_W), lambda hh: (hh, 0, 0))],
        out_specs=pl.BlockSpec((None, 1, TOEP_W), lambda hh: (hh, 0, 0)),
        out_shape=_sds((h, 1, TOEP_W), F32), compiler_params=_params("parallel"), name=name)(flipped)[:, 0, :]
    return jnp.concatenate([jnp.zeros((h, TABLE_LOW), F32), jnp.flip(out[:, TOEP_FLAT:TOEP_W - 1], axis=1),
                            out[:, TOEP_W - 1:]], axis=1)


def attn_layer_fwd(tag, x, w):
    h = rmsnorm_fwd(tag + "_norm", x, w["g_pre"])
    proj = mm_nn_wblk(tag + "_proj", h, w["w_in"], w["idx"], BF16)
    qkv = _heads(proj, 6)
    pad = lambda a: jnp.pad(a, ((0, 0), (PAD_KEYS, 0), (0, 0)))
    qa, kap, vap = qkv[0] * QK_SCALE, pad(qkv[1]), pad(qkv[2])
    qs, ks, vs = qkv[3] * QK_SCALE, qkv[4], qkv[5]
    bias = rel_bias_matrix(tag + "_bias", w["rel_bias"])
    oa, lse = attn_a_fwd(tag + "_a", qa, kap, vap, bias)
    ob = sb_fwd(tag + "_sb", qs, ks, vs)
    o = _unheads(jnp.stack([oa, ob])).astype(BF16)
    m = mm_nn(tag + "_out", o, w["w_out"], F32)
    x1 = resid_norm_fwd(tag + "_res", x, m, w["g_post"])
    return x1, (x, h, qa, kap, vap, qs, ks, vs, bias, oa, lse, ob, o, m)


def attn_layer_bwd(tag, dx1, saved, w):
    x, h, qa, kap, vap, qs, ks, vs, bias, oa, lse, ob, o, m = saved
    dm, dg_post = norm_bwd(tag + "_dpost", dx1, m, w["g_post"], None, BF16)
    d_w_out = mm_tn(tag + "_dwout", o, dm, F32)
    do = _heads(mm_nt(tag + "_do", dm, w["w_out"], BF16), 2)
    dqa, dkap, dvap, dbias = attn_a_bwd(tag + "_da", qa, kap, vap, bias, oa, lse, do[0])
    dqs, dks, dvs = sb_bwd(tag + "_dsb", qs, ks, vs, ob, do[1])
    d_rel = rel_bias_grad(tag + "_dbias", dbias)
    dproj = _unheads(jnp.stack([dqa, dkap[:, PAD_KEYS:], dvap[:, PAD_KEYS:], dqs, dks, dvs])).astype(BF16)
    d_w_in = mm_tn_oblk(tag + "_dwin", h, dproj, w["w_in"].shape[3], F32)
    dh = mm_nt_wblk(tag + "_dh", dproj, w["w_in"], w["idx"], F32)
    dx, dg_pre = norm_bwd(tag + "_dpre", dh, x, w["g_pre"], dx1, F32)
    return dx, dict(w_in=d_w_in, w_out=d_w_out, rel_bias=d_rel, g_pre=dg_pre, g_post=dg_post)


def rg_layer_fwd(tag, x, w):
    c = w["w_out"].shape[0]
    h = rmsnorm_fwd(tag + "_norm", x, w["g_pre"])
    proj = mm_nn_wblk(tag + "_proj", h, w["w_in"], w["idx"], F32)
    gate_pre, xr = proj[:, :c], proj[:, c:]
    xs = jnp.stack([_shift_down(xr, 3 - j) for j in range(4)])
    xc = fir4(tag + "_conv", xs, w["conv_w"], w["conv_b"])
    a, u = rg_gates_fwd(tag + "_gates", xc, w["w_a"], w["w_i"], w["b_a"], w["b_i"], w["lam"])
    hs, z = rg_scan_fwd(tag + "_scan", a, u, gate_pre)
    m = mm_nn(tag + "_out", z, w["w_out"], F32)
    x1 = resid_norm_fwd(tag + "_res", x, m, w["g_post"])
    return x1, (x, h, gate_pre, xs, xc, a, hs, z, m)


def rg_layer_bwd(tag, dx1, saved, w):
    x, h, gate_pre, xs, xc, a, hs, z, m = saved
    dm, dg_post = norm_bwd(tag + "_dpost", dx1, m, w["g_post"], None, BF16)
    d_w_out = mm_tn(tag + "_dwout", z, dm, F32)
    dz = mm_nt(tag + "_dz", dm, w["w_out"], F32)
    gu, dgate = rg_scan_bwd(tag + "_dscan", _shift_up(a, 1), hs, gate_pre, dz)
    dxc, d_w_a, d_w_i, d_b_a, d_b_i, d_lam = rg_gates_bwd(
        tag + "_dgates", xc, gu, _shift_down(hs, 1), w["w_a"], w["w_i"], w["b_a"], w["b_i"], w["lam"])
    d_conv_w, d_conv_b = fir4_bwd(tag + "_dconvw", xs, dxc)
    dxs = jnp.stack([_shift_up(dxc, 3 - j) for j in range(4)])
    dxr = fir4(tag + "_dconv", dxs, w["conv_w"], jnp.zeros_like(w["conv_b"]))
    dproj = jnp.concatenate([dgate, dxr.astype(BF16)], axis=1)
    d_w_in = mm_tn_oblk(tag + "_dwin", h, dproj, w["w_in"].shape[3], F32)
    dh = mm_nt_wblk(tag + "_dh", dproj, w["w_in"], w["idx"], F32)
    dx, dg_pre = norm_bwd(tag + "_dpre", dh, x, w["g_pre"], dx1, F32)
    return dx, dict(w_in=d_w_in, w_out=d_w_out, conv_w=d_conv_w, conv_b=d_conv_b, w_a=d_w_a, w_i=d_w_i,
                    b_a=d_b_a, b_i=d_b_i, lam=d_lam, g_pre=dg_pre, g_post=dg_post)


def ffn_layer_fwd(tag, x, w):
    h = rmsnorm_fwd(tag + "_norm", x, w["g_pre"])
    g, u, a = ffn_up(tag + "_up", h, w["w_gate"], w["w_up"], w["idx"])
    f = ffn_down(tag + "_down", a, w["w_down"], w["idx"])
    x1 = resid_norm_fwd(tag + "_res", x, f, w["g_post"])
    return x1, (x, h, g, u, a, f)


def ffn_layer_bwd(tag, dx1, saved, w):
    x, h, g, u, a, f = saved
    dm, dg_post = norm_bwd(tag + "_dpost", dx1, f, w["g_post"], None, BF16)
    d_w_down = ffn_dw_down(tag + "_dwdown", a, dm)
    dg, du = ffn_bwd_act(tag + "_dact", dm, w["w_down"], g, u, w["idx"])
    d_w_gate = ffn_dw_in(tag + "_dwgate", h, dg)
    d_w_up = ffn_dw_in(tag + "_dwup", h, du)
    dh = ffn_bwd_dh(tag + "_dh", dg, du, w["w_gate"], w["w_up"], w["idx"])
    dx, dg_pre = norm_bwd(tag + "_dpre", dh, x, w["g_pre"], dx1, F32)
    return dx, dict(w_gate=d_w_gate, w_up=d_w_up, w_down=d_w_down, g_pre=dg_pre, g_post=dg_post)


def _place():
    return lax.axis_index("x"), lax.axis_index("y"), lax.axis_index("c")


def all_gather(name, blks):
    n = len(blks)

    def body(*refs):
        x_refs, out_refs = refs[:n], refs[n:2 * n]
        send_sems, recv_sems, local_sems = refs[2 * n:]
        x, y, cc = _place()
        me, sibling = (x, y, cc), (x, y, 1 - cc)
        chips = [(1 - x, y), (x, 1 - y), (1 - x, 1 - y)]

        def slot(a, px, py, pc):
            return out_refs[a].at[4 * px + 2 * py + pc]

        def copy(a, k, block, to, src=None):
            return pltpu.make_async_remote_copy(
                src_ref=slot(a, *block) if src is None else src, dst_ref=slot(a, *block),
                send_sem=send_sems.at[7 * a + k], recv_sem=recv_sems.at[7 * a + k], device_id=to, device_id_type=MESH)

        mine = [pltpu.make_async_copy(x_refs[a], slot(a, *me), local_sems.at[a]) for a in range(n)]
        first = []
        for a in range(n):
            mine[a].start()
            first.append(copy(a, 0, me, sibling, src=x_refs[a]))
            first += [copy(a, 1 + j, me, (*chip, cc), src=x_refs[a]) for j, chip in enumerate(chips)]
        for cp in first:
            cp.start()
        passed = []
        for j, chip in enumerate(chips):
            for a in range(n):
                copy(a, 1 + j, (*chip, cc), me).wait_recv()
                passed.append(copy(a, 4 + j, (*chip, cc), sibling))
                passed[-1].start()
        for a in range(n):
            copy(a, 0, sibling, me).wait_recv()
            for j, chip in enumerate(chips):
                copy(a, 4 + j, (*chip, 1 - cc), me).wait_recv()
        for cp in first + passed:
            cp.wait_send()
        for cp in mine:
            cp.wait()

    return pl.pallas_call(
        body, out_shape=[_sds((N_DEV,) + b.shape, b.dtype) for b in blks], in_specs=[ANY] * n, out_specs=[ANY] * n,
        scratch_shapes=[pltpu.SemaphoreType.DMA((7 * n,)), pltpu.SemaphoreType.DMA((7 * n,)),
                        pltpu.SemaphoreType.DMA((n,))],
        name=name)(*blks)


def exchange_pair(name, gs):
    n = len(gs)
    nchip = 4

    def body(*refs):
        g_refs, land_refs = refs[:n], refs[n:2 * n]
        send_sems, recv_sems = refs[2 * n:]
        x, y, cc = _place()
        copies = [pltpu.make_async_remote_copy(
            src_ref=g_refs[a].at[j, 1 - cc], dst_ref=land_refs[a].at[j], send_sem=send_sems.at[nchip * a + j],
            recv_sem=recv_sems.at[nchip * a + j], device_id=(x, y, 1 - cc), device_id_type=MESH)
            for a in range(n) for j in range(nchip)]
        for cp in copies:
            cp.start()
        for cp in copies:
            cp.wait()

    return pl.pallas_call(
        body, out_shape=[_sds((nchip,) + g.shape[2:], g.dtype) for g in gs], in_specs=[ANY] * n, out_specs=[ANY] * n,
        scratch_shapes=[pltpu.SemaphoreType.DMA((nchip * n,)), pltpu.SemaphoreType.DMA((nchip * n,))],
        name=name)(*gs)


def pair_sum(name, g, land, core):
    nchip, _, r, c = g.shape
    tr = _row_tile(r, 128)

    def body(core_ref, g_ref, l_ref, o_ref):
        o_ref[...] = g_ref[...] + l_ref[...]

    return pl.pallas_call(
        body,
        grid_spec=pltpu.PrefetchScalarGridSpec(
            num_scalar_prefetch=1, grid=(nchip, r // tr),
            in_specs=[pl.BlockSpec((None, None, tr, c), lambda j, i, core_ref: (j, core_ref[0], i, 0)),
                      pl.BlockSpec((None, tr, c), lambda j, i, core_ref: (j, i, 0))],
            out_specs=pl.BlockSpec((None, tr, c), lambda j, i, core_ref: (j, i, 0))),
        out_shape=_sds((nchip, r, c), g.dtype), compiler_params=_params("parallel", "parallel"), name=name,
    )(core, g, land)


def exchange_chips(name, ps):
    n = len(ps)

    def body(*refs):
        p_refs, land_refs = refs[:n], refs[n:2 * n]
        send_sems, recv_sems, local_sems = refs[2 * n:]
        x, y, cc = _place()
        mine = 2 * x + y
        chips = [(1 - x, y), (x, 1 - y), (1 - x, 1 - y)]
        own = [pltpu.make_async_copy(p_refs[a].at[mine], land_refs[a].at[mine], local_sems.at[a]) for a in range(n)]
        for cp in own:
            cp.start()
        sends = [pltpu.make_async_remote_copy(
            src_ref=p_refs[a].at[2 * px + py], dst_ref=land_refs[a].at[mine], send_sem=send_sems.at[3 * a + k],
            recv_sem=recv_sems.at[3 * a + k], device_id=(px, py, cc), device_id_type=MESH)
            for a in range(n) for k, (px, py) in enumerate(chips)]
        for cp in sends:
            cp.start()
        for a in range(n):
            for k, (px, py) in enumerate(chips):
                pltpu.make_async_remote_copy(
                    src_ref=p_refs[a].at[mine], dst_ref=land_refs[a].at[2 * px + py], send_sem=send_sems.at[3 * a + k],
                    recv_sem=recv_sems.at[3 * a + k], device_id=(px, py, cc), device_id_type=MESH).wait_recv()
        for cp in sends:
            cp.wait_send()
        for cp in own:
            cp.wait()

    return pl.pallas_call(
        body, out_shape=[_sds(p.shape, p.dtype) for p in ps], in_specs=[ANY] * n, out_specs=[ANY] * n,
        scratch_shapes=[pltpu.SemaphoreType.DMA((3 * n,)), pltpu.SemaphoreType.DMA((3 * n,)),
                        pltpu.SemaphoreType.DMA((n,))],
        name=name)(*ps)


def adamw(name, parts, w, m, v):
    npart, r, c = parts.shape
    tr = _row_tile(r, 128)
    c1 = 1.0 / (1.0 - ADAM_B1 ** ADAM_STEP)
    c2 = 1.0 / (1.0 - ADAM_B2 ** ADAM_STEP)

    def body(p_ref, w_ref, m_ref, v_ref, g_ref, d_ref, nm_ref, nv_ref):
        g = p_ref[0]
        for j in range(1, npart):
            g = g + p_ref[j]
        nm = ADAM_B1 * m_ref[...] + (1.0 - ADAM_B1) * g
        nv = ADAM_B2 * v_ref[...] + (1.0 - ADAM_B2) * (g * g)
        g_ref[...] = g
        nm_ref[...] = nm
        nv_ref[...] = nv
        d_ref[...] = -ADAM_LR * ((nm * c1) / (jnp.sqrt(nv * c2) + ADAM_EPS) + ADAM_WD * w_ref[...])

    row = pl.BlockSpec((tr, c), lambda i: (i, 0))
    return pl.pallas_call(
        body, grid=(r // tr,), in_specs=[pl.BlockSpec((npart, tr, c), lambda i: (0, i, 0)), row, row, row],
        out_specs=[row] * 4, out_shape=[_sds((r, c), F32)] * 4, compiler_params=_params("parallel"), name=name,
    )(parts, w, m, v)


def _pack(arrays, dtype, row_multiple):
    flat = jnp.concatenate([a.astype(dtype).reshape(-1) for a in arrays])
    per = row_multiple * LANES
    total = -(-flat.shape[0] // per) * per
    return jnp.pad(flat, (0, total - flat.shape[0])).reshape(total // LANES, LANES)


def _pack_blocked(arrays, dtype, row_multiple):
    flat = jnp.concatenate([a.astype(dtype).reshape(N_DEV, -1) for a in arrays], axis=1)
    per = row_multiple * LANES
    total = -(-flat.shape[1] // per) * per
    return jnp.pad(flat, ((0, 0), (0, total - flat.shape[1]))).reshape(N_DEV, total // LANES, LANES)


def _unpack(buf, shapes, lead=()):
    flat = buf.reshape(lead + (-1,))
    out, off = [], 0
    for s in shapes:
        n = math.prod(s)
        out.append(flat[..., off:off + n].reshape(lead + tuple(s)))
        off += n
    return out


def _to_blocked(full, ax):
    s = full.shape
    return jnp.moveaxis(full.reshape(s[:ax] + (N_DEV, s[ax] // N_DEV) + s[ax + 1:]), ax, 0)


def _from_blocked(blk, ax):
    moved = jnp.moveaxis(blk, 0, ax)
    s = moved.shape
    return moved.reshape(s[:ax] + (s[ax] * s[ax + 1],) + s[ax + 2:])


SMALL = ("rg_conv_w", "rg_conv_b", "rg_b_a", "rg_b_i", "rg_lambda")
BIG = ("attn_w_in", "attn_w_out", "rg_w_in", "rg_w_a", "rg_w_i", "rg_w_out", "ffn_w_gate", "ffn_w_up", "ffn_w_down")


def kernel(x, attn_w_in, attn_rel_bias, attn_w_out, rg_w_in, rg_conv_w, rg_conv_b, rg_w_a, rg_b_a, rg_w_i, rg_b_i, rg_lambda, rg_w_out, norm_mix_pre, norm_mix_post, norm_ffn_pre, norm_ffn_post, ffn_w_gate, ffn_w_up, ffn_w_down, loss_target, m_attn_w_in, m_attn_rel_bias, m_attn_w_out, m_rg_w_in, m_rg_conv_w, m_rg_conv_b, m_rg_w_a, m_rg_b_a, m_rg_w_i, m_rg_b_i, m_rg_lambda, m_rg_w_out, m_norm_mix_pre, m_norm_mix_post, m_norm_ffn_pre, m_norm_ffn_post, m_ffn_w_gate, m_ffn_w_up, m_ffn_w_down, v_attn_w_in, v_attn_rel_bias, v_attn_w_out, v_rg_w_in, v_rg_conv_w, v_rg_conv_b, v_rg_w_a, v_rg_b_a, v_rg_w_i, v_rg_b_i, v_rg_lambda, v_rg_w_out, v_norm_mix_pre, v_norm_mix_post, v_norm_ffn_pre, v_norm_ffn_post, v_ffn_w_gate, v_ffn_w_up, v_ffn_w_down):
    w_loc = dict(attn_w_in=attn_w_in, attn_rel_bias=attn_rel_bias, attn_w_out=attn_w_out, rg_w_in=rg_w_in,
                 rg_conv_w=rg_conv_w, rg_conv_b=rg_conv_b, rg_w_a=rg_w_a, rg_b_a=rg_b_a, rg_w_i=rg_w_i, rg_b_i=rg_b_i,
                 rg_lambda=rg_lambda, rg_w_out=rg_w_out, norm_mix_pre=norm_mix_pre, norm_mix_post=norm_mix_post,
                 norm_ffn_pre=norm_ffn_pre, norm_ffn_post=norm_ffn_post, ffn_w_gate=ffn_w_gate, ffn_w_up=ffn_w_up,
                 ffn_w_down=ffn_w_down)
    m_loc = dict(attn_w_in=m_attn_w_in, attn_rel_bias=m_attn_rel_bias, attn_w_out=m_attn_w_out, rg_w_in=m_rg_w_in,
                 rg_conv_w=m_rg_conv_w, rg_conv_b=m_rg_conv_b, rg_w_a=m_rg_w_a, rg_b_a=m_rg_b_a, rg_w_i=m_rg_w_i,
                 rg_b_i=m_rg_b_i, rg_lambda=m_rg_lambda, rg_w_out=m_rg_w_out, norm_mix_pre=m_norm_mix_pre,
                 norm_mix_post=m_norm_mix_post, norm_ffn_pre=m_norm_ffn_pre, norm_ffn_post=m_norm_ffn_post,
                 ffn_w_gate=m_ffn_w_gate, ffn_w_up=m_ffn_w_up, ffn_w_down=m_ffn_w_down)
    v_loc = dict(attn_w_in=v_attn_w_in, attn_rel_bias=v_attn_rel_bias, attn_w_out=v_attn_w_out, rg_w_in=v_rg_w_in,
                 rg_conv_w=v_rg_conv_w, rg_conv_b=v_rg_conv_b, rg_w_a=v_rg_w_a, rg_b_a=v_rg_b_a, rg_w_i=v_rg_w_i,
                 rg_b_i=v_rg_b_i, rg_lambda=v_rg_lambda, rg_w_out=v_rg_w_out, norm_mix_pre=v_norm_mix_pre,
                 norm_mix_post=v_norm_mix_post, norm_ffn_pre=v_norm_ffn_pre, norm_ffn_post=v_norm_ffn_post,
                 ffn_w_gate=v_ffn_w_gate, ffn_w_up=v_ffn_w_up, ffn_w_down=v_ffn_w_down)
    axis_of = dict(SHARDED)
    xt, target = x[0], loss_target[0]
    d_model = xt.shape[1]
    rows2d = lambda a: a.reshape(-1, a.shape[-1])
    small_shapes = [w_loc[n].shape for n in SMALL]

    gathered = all_gather("gather_weights", [rows2d(w_loc[n]).astype(BF16) for n in BIG]
                          + [_pack([w_loc[n] for n in SMALL], F32, 8)])
    blocked = {n: g.reshape((N_DEV,) + w_loc[n].shape) for n, g in zip(BIG, gathered)}
    blocked.update(zip(SMALL, _unpack(gathered[-1], small_shapes, (N_DEV,))))
    full = {n: _from_blocked(blocked[n], axis_of[n]) for n in SMALL}
    row = lambda a: a.reshape(1, -1).astype(F32)
    square = lambda rows8: rows8.reshape(-1, rows8.shape[-1])
    gates = lambda g: jnp.swapaxes(g, 0, 1).reshape(LRU_BLOCKS, -1, g.shape[-1])

    def layer_weights(layer):
        j = layer // 2
        norms = dict(g_pre=row(norm_mix_pre[layer]), g_post=row(norm_mix_post[layer]), idx=j)
        if layer % 2 == 0:
            mix = dict(w_in=blocked["attn_w_in"], w_out=square(blocked["attn_w_out"][:, j]),
                       rel_bias=attn_rel_bias[j], **norms)
        else:
            mix = dict(w_in=blocked["rg_w_in"], w_out=square(blocked["rg_w_out"][:, j]),
                       conv_w=full["rg_conv_w"][j][:, 0, :], conv_b=row(full["rg_conv_b"][j]),
                       w_a=gates(blocked["rg_w_a"][:, j]), w_i=gates(blocked["rg_w_i"][:, j]),
                       b_a=row(full["rg_b_a"][j]), b_i=row(full["rg_b_i"][j]), lam=row(full["rg_lambda"][j]), **norms)
        ffn = dict(w_gate=blocked["ffn_w_gate"], w_up=blocked["ffn_w_up"], w_down=blocked["ffn_w_down"], idx=layer,
                   g_pre=row(norm_ffn_pre[layer]), g_post=row(norm_ffn_post[layer]))
        return mix, ffn

    act, tape = xt, []
    for layer in range(DEPTH):
        mix_w, ffn_w = layer_weights(layer)
        mixer_fwd = attn_layer_fwd if layer % 2 == 0 else rg_layer_fwd
        act, saved_mix = mixer_fwd(f"l{layer}_mix", act, mix_w)
        act, saved_ffn = ffn_layer_fwd(f"l{layer}_ffn", act, ffn_w)
        tape.append((mix_w, ffn_w, saved_mix, saved_ffn))
    dact, sq = loss_grad("loss", act, target)
    loss = lax.psum(0.5 * jnp.sum(sq) / d_model, ("x", "y", "c"))

    grads = {}
    for layer in reversed(range(DEPTH)):
        mix_w, ffn_w, saved_mix, saved_ffn = tape[layer]
        dact, grads[("ffn", layer)] = ffn_layer_bwd(f"l{layer}_ffn", dact, saved_ffn, ffn_w)
        mixer_bwd = attn_layer_bwd if layer % 2 == 0 else rg_layer_bwd
        dact, grads[("mix", layer)] = mixer_bwd(f"l{layer}_mix", dact, saved_mix, mix_w)
    attn_g = [grads[("mix", l)] for l in range(0, DEPTH, 2)]
    rg_g = [grads[("mix", l)] for l in range(1, DEPTH, 2)]
    ffn_g = [grads[("ffn", l)] for l in range(DEPTH)]
    stack = lambda gs, key: jnp.stack([g[key] for g in gs])
    by_owner = lambda gs, key, f: jnp.stack([f(g[key]) for g in gs], axis=1)
    rows8 = lambda a: a.reshape(N_DEV, -1, a.shape[-1])
    ungates = lambda a: jnp.swapaxes(a.reshape(LRU_BLOCKS, N_DEV, -1, a.shape[-1]), 0, 1)
    same = lambda a: a
    blocked_g = dict(
        attn_w_in=by_owner(attn_g, "w_in", same), attn_w_out=by_owner(attn_g, "w_out", rows8),
        rg_w_in=by_owner(rg_g, "w_in", same), rg_w_out=by_owner(rg_g, "w_out", rows8),
        rg_w_a=by_owner(rg_g, "w_a", ungates), rg_w_i=by_owner(rg_g, "w_i", ungates),
        ffn_w_gate=by_owner(ffn_g, "w_gate", same), ffn_w_up=by_owner(ffn_g, "w_up", same),
        ffn_w_down=by_owner(ffn_g, "w_down", same))
    contrib = dict(
        attn_rel_bias=stack(attn_g, "rel_bias"), rg_conv_w=stack(rg_g, "conv_w")[:, :, None, :],
        rg_conv_b=stack(rg_g, "conv_b")[:, 0], rg_b_a=stack(rg_g, "b_a").reshape(rg_b_a.shape[0], LRU_BLOCKS, -1),
        rg_b_i=stack(rg_g, "b_i").reshape(rg_b_i.shape[0], LRU_BLOCKS, -1), rg_lambda=stack(rg_g, "lam")[:, 0],
        norm_mix_pre=jnp.concatenate([grads[("mix", l)]["g_pre"] for l in range(DEPTH)]),
        norm_mix_post=jnp.concatenate([grads[("mix", l)]["g_post"] for l in range(DEPTH)]),
        norm_ffn_pre=jnp.concatenate([g["g_pre"] for g in ffn_g]),
        norm_ffn_post=jnp.concatenate([g["g_post"] for g in ffn_g]),
    )
    small_g = _pack_blocked([_to_blocked(contrib[n], axis_of[n]) for n in SMALL], F32, 8)

    slabs = [blocked_g[n].reshape(4, 2, -1, blocked_g[n].shape[-1]) for n in BIG] + [small_g.reshape(4, 2, -1, LANES)]
    core = lax.axis_index("c").astype(jnp.int32).reshape(1)
    from_sibling = exchange_pair("rs_pair", slabs)
    pairs = [pair_sum(f"rs_pair_sum_{i}", g, l, core) for i, (g, l) in enumerate(zip(slabs, from_sibling))]
    by_chip = exchange_chips("rs_chips", pairs)
    result = {}
    kinds = ("grad", "delta", "new_m", "new_v")
    for n, parts in zip(BIG, by_chip):
        outs = adamw("adamw_" + n, parts, *[rows2d(d[n]) for d in (w_loc, m_loc, v_loc)])
        for kind, a in zip(kinds, outs):
            result[(kind, n)] = a.reshape(w_loc[n].shape)
    outs = adamw("adamw_small", by_chip[-1], *[_pack([d[n] for n in SMALL], F32, 8) for d in (w_loc, m_loc, v_loc)])
    for kind, buf in zip(kinds, outs):
        result.update({(kind, n): a for n, a in zip(SMALL, _unpack(buf, small_shapes))})
    rep_shapes = [w_loc[n].shape for n in REPLICATED]
    rep_parts, = all_gather("gather_rep_grads", [_pack([contrib[n] for n in REPLICATED], F32, 8)])
    outs = adamw("adamw_replicated", rep_parts, *[_pack([d[n] for n in REPLICATED], F32, 8)
                                                  for d in (w_loc, m_loc, v_loc)])
    for kind, buf in zip(kinds, outs):
        result.update({(kind, n): a for n, a in zip(REPLICATED, _unpack(buf, rep_shapes))})
    return (loss, dact[None], *[result[(kind, n)] for kind in kinds for n in WEIGHTS])
```

```python
import functools
import math

import jax
import jax.numpy as jnp
from jax import lax
from jax.experimental import pallas as pl
from jax.experimental.pallas import tpu as pltpu

F32 = jnp.float32
BF16 = jnp.bfloat16

N_DEV = 8
DEPTH = 4
CHUNK = 64
N_LEFT = 8
BAND = (N_LEFT + 1) * CHUNK
PAD_KEYS = N_LEFT * CHUNK
HEAD_DIM = 64
N_HEADS = 8
REL_CLIP = 256
LRU_BLOCKS = 4
LRU_C = 8.0
RMS_EPS = 1e-6
QK_SCALE = HEAD_DIM ** -0.5

ADAM_LR = 0.001
ADAM_B1 = 0.9
ADAM_B2 = 0.999
ADAM_EPS = 1e-08
ADAM_WD = 0.01
ADAM_STEP = 10

LANES = 1024
V7X_VMEM_LIMIT = 56 * 1024 * 1024

MESH = pl.DeviceIdType.MESH
ANY = pl.BlockSpec(memory_space=pl.ANY)

SHARDED = (
    ("attn_w_in", 2), ("attn_w_out", 1), ("rg_w_in", 2), ("rg_conv_w", 3), ("rg_conv_b", 1),
    ("rg_w_a", 2), ("rg_b_a", 2), ("rg_w_i", 2), ("rg_b_i", 2), ("rg_lambda", 1), ("rg_w_out", 1),
    ("ffn_w_gate", 2), ("ffn_w_up", 2), ("ffn_w_down", 1),
)
REPLICATED = ("attn_rel_bias", "norm_mix_pre", "norm_mix_post", "norm_ffn_pre", "norm_ffn_post")
WEIGHTS = ("attn_w_in", "attn_rel_bias", "attn_w_out", "rg_w_in", "rg_conv_w", "rg_conv_b", "rg_w_a", "rg_b_a",
           "rg_w_i", "rg_b_i", "rg_lambda", "rg_w_out", "norm_mix_pre", "norm_mix_post", "norm_ffn_pre",
           "norm_ffn_post", "ffn_w_gate", "ffn_w_up", "ffn_w_down")


def _params(*dims):
    return pltpu.CompilerParams(dimension_semantics=dims or None, vmem_limit_bytes=V7X_VMEM_LIMIT)


def _sds(shape, dtype):
    return jax.ShapeDtypeStruct(tuple(shape), dtype)


def _row_tile(n, pref):
    t = min(n, pref)
    assert n % t == 0, (n, pref)
    return t


NN = (((1,), (0,)), ((), ()))
NT = (((1,), (1,)), ((), ()))
TN = (((0,), (0,)), ((), ()))


def _gmm(name, a, b, *, grid, a_blk, a_idx, b_blk, b_idx, o_blk, o_idx, out_shape, out_dtype, dn, acc_shape):
    nk = grid[-1]
    kax = len(grid) - 1

    def body(a_ref, b_ref, o_ref, acc_ref):
        part = lax.dot_general(a_ref[...], b_ref[...], dn, preferred_element_type=F32)
        if nk == 1:
            o_ref[...] = part.astype(o_ref.dtype)
            return
        k = pl.program_id(kax)

        @pl.when(k == 0)
        def _():
            acc_ref[...] = part

        @pl.when(k > 0)
        def _():
            acc_ref[...] += part

        @pl.when(k == nk - 1)
        def _():
            o_ref[...] = acc_ref[...].astype(o_ref.dtype)

    return pl.pallas_call(
        body, grid=grid,
        in_specs=[pl.BlockSpec(a_blk, a_idx), pl.BlockSpec(b_blk, b_idx)],
        out_specs=pl.BlockSpec(o_blk, o_idx),
        out_shape=_sds(out_shape, out_dtype),
        scratch_shapes=[pltpu.VMEM(acc_shape, F32)],
        compiler_params=_params(*(["parallel"] * kax + ["arbitrary"])),
        name=name,
    )(a, b)


def mm_nn(name, a, b, out_dtype, tm=1024, tn=512, tk=1024):
    (m, k), (_, n) = a.shape, b.shape
    tm, tn, tk = _row_tile(m, tm), _row_tile(n, tn), _row_tile(k, tk)
    return _gmm(name, a, b, grid=(m // tm, n // tn, k // tk),
                a_blk=(tm, tk), a_idx=lambda i, j, kk: (i, kk), b_blk=(tk, tn), b_idx=lambda i, j, kk: (kk, j),
                o_blk=(tm, tn), o_idx=lambda i, j, kk: (i, j), out_shape=(m, n), out_dtype=out_dtype, dn=NN,
                acc_shape=(tm, tn))


def mm_nt(name, a, b, out_dtype, tm=1024, tn=512, tk=1024):
    (m, k), (n, _) = a.shape, b.shape
    tm, tn, tk = _row_tile(m, tm), _row_tile(n, tn), _row_tile(k, tk)
    return _gmm(name, a, b, grid=(m // tm, n // tn, k // tk),
                a_blk=(tm, tk), a_idx=lambda i, j, kk: (i, kk), b_blk=(tn, tk), b_idx=lambda i, j, kk: (j, kk),
                o_blk=(tm, tn), o_idx=lambda i, j, kk: (i, j), out_shape=(m, n), out_dtype=out_dtype, dn=NT,
                acc_shape=(tm, tn))


def mm_tn(name, a, b, out_dtype, tm=512, tn=512, tk=1024):
    (k, m), (_, n) = a.shape, b.shape
    tm, tn, tk = _row_tile(m, tm), _row_tile(n, tn), _row_tile(k, tk)
    return _gmm(name, a, b, grid=(m // tm, n // tn, k // tk),
                a_blk=(tk, tm), a_idx=lambda i, j, kk: (kk, i), b_blk=(tk, tn), b_idx=lambda i, j, kk: (kk, j),
                o_blk=(tm, tn), o_idx=lambda i, j, kk: (i, j), out_shape=(m, n), out_dtype=out_dtype, dn=TN,
                acc_shape=(tm, tn))


def mm_nn_wblk(name, a, wb, layer, out_dtype, tm=1024, tk=1024):
    (m, k), (nb, _, _, n8) = a.shape, wb.shape
    tm, tk = _row_tile(m, tm), _row_tile(k, tk)
    return _gmm(name, a, wb, grid=(m // tm, nb, k // tk),
                a_blk=(tm, tk), a_idx=lambda i, j, kk: (i, kk),
                b_blk=(None, None, tk, n8), b_idx=lambda i, j, kk: (j, layer, kk, 0),
                o_blk=(tm, n8), o_idx=lambda i, j, kk: (i, j), out_shape=(m, nb * n8), out_dtype=out_dtype, dn=NN,
                acc_shape=(tm, n8))


def mm_nt_wblk(name, a, wb, layer, out_dtype, tm=1024, tn=512):
    m = a.shape[0]
    nb, _, k, n8 = wb.shape
    tm, tn = _row_tile(m, tm), _row_tile(k, tn)
    return _gmm(name, a, wb, grid=(m // tm, k // tn, nb),
                a_blk=(tm, n8), a_idx=lambda i, j, kk: (i, kk),
                b_blk=(None, None, tn, n8), b_idx=lambda i, j, kk: (kk, layer, j, 0),
                o_blk=(tm, tn), o_idx=lambda i, j, kk: (i, j), out_shape=(m, k), out_dtype=out_dtype, dn=NT,
                acc_shape=(tm, tn))


def mm_tn_oblk(name, a, b, n8, out_dtype, tk=1024):
    (t, k), nb = a.shape, b.shape[1] // n8
    tk = _row_tile(t, tk)
    return _gmm(name, a, b, grid=(nb, t // tk),
                a_blk=(tk, k), a_idx=lambda j, s: (s, 0), b_blk=(tk, n8), b_idx=lambda j, s: (s, j),
                o_blk=(None, k, n8), o_idx=lambda j, s: (j, 0, 0), out_shape=(nb, k, n8), out_dtype=out_dtype, dn=TN,
                acc_shape=(k, n8))


def rmsnorm_fwd(name, x, g):
    t, d = x.shape
    tr = _row_tile(t, 512)

    def body(x_ref, g_ref, o_ref):
        xv = x_ref[...]
        r = lax.rsqrt(jnp.mean(xv * xv, axis=-1, keepdims=True) + RMS_EPS)
        o_ref[...] = (xv * r * g_ref[...]).astype(o_ref.dtype)

    return pl.pallas_call(
        body, grid=(t // tr,),
        in_specs=[pl.BlockSpec((tr, d), lambda i: (i, 0)), pl.BlockSpec((1, d), lambda i: (0, 0))],
        out_specs=pl.BlockSpec((tr, d), lambda i: (i, 0)),
        out_shape=_sds((t, d), BF16), compiler_params=_params("parallel"), name=name)(x, g)


def resid_norm_fwd(name, x, m, g):
    t, d = x.shape
    tr = _row_tile(t, 512)

    def body(x_ref, m_ref, g_ref, o_ref):
        mv = m_ref[...]
        r = lax.rsqrt(jnp.mean(mv * mv, axis=-1, keepdims=True) + RMS_EPS)
        o_ref[...] = x_ref[...] + mv * r * g_ref[...]

    return pl.pallas_call(
        body, grid=(t // tr,),
        in_specs=[pl.BlockSpec((tr, d), lambda i: (i, 0)), pl.BlockSpec((tr, d), lambda i: (i, 0)),
                  pl.BlockSpec((1, d), lambda i: (0, 0))],
        out_specs=pl.BlockSpec((tr, d), lambda i: (i, 0)),
        out_shape=_sds((t, d), F32), compiler_params=_params("parallel"), name=name)(x, m, g)


def norm_bwd(name, dy, x, g, resid, out_dtype):
    t, d = x.shape
    tr = _row_tile(t, 512)
    has_res = resid is not None

    def body(*refs):
        if has_res:
            dy_ref, x_ref, g_ref, r_ref, dx_ref, dg_ref = refs
        else:
            dy_ref, x_ref, g_ref, dx_ref, dg_ref = refs
        i = pl.program_id(0)
        xv = x_ref[...]
        dyv = dy_ref[...].astype(F32)
        r = lax.rsqrt(jnp.mean(xv * xv, axis=-1, keepdims=True) + RMS_EPS)
        xh = xv * r
        dxh = dyv * g_ref[...]
        dx = r * (dxh - xh * jnp.mean(dxh * xh, axis=-1, keepdims=True))
        if has_res:
            dx = dx + r_ref[...]
        dx_ref[...] = dx.astype(dx_ref.dtype)
        part = jnp.sum(dyv * xh, axis=0, keepdims=True)

        @pl.when(i == 0)
        def _():
            dg_ref[...] = part

        @pl.when(i > 0)
        def _():
            dg_ref[...] += part

    row = pl.BlockSpec((tr, d), lambda i: (i, 0))
    vec = pl.BlockSpec((1, d), lambda i: (0, 0))
    ins = [dy, x, g] + ([resid] if has_res else [])
    return pl.pallas_call(
        body, grid=(t // tr,),
        in_specs=[row, row, vec] + ([row] if has_res else []),
        out_specs=[row, vec],
        out_shape=[_sds((t, d), out_dtype), _sds((1, d), F32)],
        compiler_params=_params("arbitrary"), name=name)(*ins)


def loss_grad(name, y, target):
    t, d = y.shape
    tr = _row_tile(t, 512)

    def body(y_ref, t_ref, dy_ref, s_ref):
        i = pl.program_id(0)
        err = y_ref[...] - t_ref[...]
        dy_ref[...] = err * (1.0 / d)
        part = jnp.sum(err * err, axis=0, keepdims=True)

        @pl.when(i == 0)
        def _():
            s_ref[...] = part

        @pl.when(i > 0)
        def _():
            s_ref[...] += part

    row = pl.BlockSpec((tr, d), lambda i: (i, 0))
    vec = pl.BlockSpec((1, d), lambda i: (0, 0))
    return pl.pallas_call(
        body, grid=(t // tr,), in_specs=[row, row], out_specs=[row, vec],
        out_shape=[_sds((t, d), F32), _sds((1, d), F32)],
        compiler_params=_params("arbitrary"), name=name)(y, target)


PAIR = 2 * HEAD_DIM
N_PAIRS = N_HEADS // 2
A_TQ = 512
A_UNROLL = 2


def _halves(x):
    lane = lax.broadcasted_iota(jnp.int32, x.shape, x.ndim - 1)
    zero = jnp.zeros_like(x)
    return jnp.where(lane < HEAD_DIM, x, zero), jnp.where(lane >= HEAD_DIM, x, zero)


def _merge(a, b):
    lane = lax.broadcasted_iota(jnp.int32, a.shape, a.ndim - 1)
    return jnp.where(lane < HEAD_DIM, a, b)


def _a_valid(c):
    col = lax.broadcasted_iota(jnp.int32, (CHUNK, BAND), 1)
    return col >= (N_LEFT - c) * CHUNK


def attn_a_fwd(name, proj, kp, vp, bias, q_blk):
    t = proj.shape[0]
    tq = _row_tile(t, A_TQ)
    ncs = tq // CHUNK
    un = math.gcd(A_UNROLL, ncs)

    def body(q_ref, k_ref, v_ref, b_ref, o_ref, l_ref):
        i = pl.program_id(1)

        def group(gg, carry):
            cs = [i * ncs + gg * un + u for u in range(un)]
            r0s = [pl.multiple_of((gg * un + u) * CHUNK, CHUNK) for u in range(un)]
            k0s = [pl.multiple_of(c * CHUNK, CHUNK) for c in cs]
            ss = []
            for u in range(un):
                qh = _halves(q_ref[pl.ds(r0s[u], CHUNK), :] * QK_SCALE)
                kwin = k_ref[pl.ds(k0s[u], BAND), :]
                valid = _a_valid(cs[u])
                for hh in range(2):
                    s = lax.dot_general(qh[hh], kwin, NT, preferred_element_type=F32) + b_ref[hh]
                    ss.append(jnp.where(valid, s, -1e30))
            ps, lses = [], []
            for s in ss:
                mx = jnp.max(s, axis=-1, keepdims=True)
                p = jnp.exp(s - mx)
                den = jnp.sum(p, axis=-1, keepdims=True)
                ps.append((p * (1.0 / den)).astype(BF16))
                lses.append(mx + jnp.log(den))
            for u in range(un):
                vwin = v_ref[pl.ds(k0s[u], BAND), :]
                o0 = jnp.dot(ps[2 * u], vwin, preferred_element_type=F32)
                o1 = jnp.dot(ps[2 * u + 1], vwin, preferred_element_type=F32)
                o_ref[pl.ds(r0s[u], CHUNK), :] = _merge(o0, o1)
                l_ref[pl.ds(r0s[u], CHUNK), :] = jnp.concatenate([lses[2 * u], lses[2 * u + 1]], axis=1)
            return carry

        lax.fori_loop(0, ncs // un, group, 0)

    return pl.pallas_call(
        body, grid=(N_PAIRS, t // tq),
        in_specs=[pl.BlockSpec((tq, PAIR), lambda p, i: (i, q_blk + p)),
                  pl.BlockSpec((t + PAD_KEYS, PAIR), lambda p, i: (0, p)),
                  pl.BlockSpec((t + PAD_KEYS, PAIR), lambda p, i: (0, p)),
                  pl.BlockSpec((2, CHUNK, BAND), lambda p, i: (p, 0, 0))],
        out_specs=[pl.BlockSpec((tq, PAIR), lambda p, i: (i, p)),
                   pl.BlockSpec((None, tq, 2), lambda p, i: (p, i, 0))],
        out_shape=[_sds((t, N_PAIRS * PAIR), F32), _sds((N_PAIRS, t, 2), F32)],
        compiler_params=_params("parallel", "parallel"), name=name)(proj, kp, vp, bias)


def attn_a_bwd(name, proj, kp, vp, bias, o, lse, do, q_blk, do_blk):
    t = proj.shape[0]
    tq = _row_tile(t, A_TQ)
    ncs = tq // CHUNK
    un = math.gcd(A_UNROLL, ncs)

    def body(q_ref, k_ref, v_ref, b_ref, o_ref, l_ref, do_ref, dq_ref, dk_ref, dv_ref, db_ref):
        i = pl.program_id(1)

        @pl.when(i == 0)
        def _():
            dk_ref[...] = jnp.zeros_like(dk_ref)
            dv_ref[...] = jnp.zeros_like(dv_ref)
            db_ref[...] = jnp.zeros_like(db_ref)

        def group(gg, carry):
            cs = [i * ncs + gg * un + u for u in range(un)]
            r0s = [pl.multiple_of((gg * un + u) * CHUNK, CHUNK) for u in range(un)]
            k0s = [pl.multiple_of(c * CHUNK, CHUNK) for c in cs]
            qhs, dohs, ps, dps, deltas = [], [], [], [], []
            for u in range(un):
                rows = pl.ds(r0s[u], CHUNK)
                qh = _halves(q_ref[rows, :] * QK_SCALE)
                doh = _halves(do_ref[rows, :])
                kwin = k_ref[pl.ds(k0s[u], BAND), :]
                vwin = v_ref[pl.ds(k0s[u], BAND), :]
                valid = _a_valid(cs[u])
                dl = _halves(do_ref[rows, :].astype(F32) * o_ref[rows, :])
                for hh in range(2):
                    s = lax.dot_general(qh[hh], kwin, NT, preferred_element_type=F32) + b_ref[hh]
                    ps.append(jnp.where(valid, jnp.exp(s - l_ref[rows, hh:hh + 1]), 0.0))
                    dps.append(lax.dot_general(doh[hh], vwin, NT, preferred_element_type=F32))
                    deltas.append(jnp.sum(dl[hh], axis=-1, keepdims=True))
                qhs.append(qh)
                dohs.append(doh)
            dss = [p * (dp - dl) for p, dp, dl in zip(ps, dps, deltas)]
            for hh in range(2):
                tot = dss[hh]
                for u in range(1, un):
                    tot = tot + dss[2 * u + hh]
                db_ref[hh] += tot
            for u in range(un):
                kwin = k_ref[pl.ds(k0s[u], BAND), :]
                ds0, ds1 = dss[2 * u].astype(BF16), dss[2 * u + 1].astype(BF16)
                dq_ref[pl.ds(r0s[u], CHUNK), :] = _merge(jnp.dot(ds0, kwin, preferred_element_type=F32),
                                                         jnp.dot(ds1, kwin, preferred_element_type=F32)) * QK_SCALE
                dk_ref[pl.ds(k0s[u], BAND), :] += (lax.dot_general(ds0, qhs[u][0], TN, preferred_element_type=F32)
                                                   + lax.dot_general(ds1, qhs[u][1], TN, preferred_element_type=F32))
                dv_ref[pl.ds(k0s[u], BAND), :] += (
                    lax.dot_general(ps[2 * u].astype(BF16), dohs[u][0], TN, preferred_element_type=F32)
                    + lax.dot_general(ps[2 * u + 1].astype(BF16), dohs[u][1], TN, preferred_element_type=F32))
            return carry

        lax.fori_loop(0, ncs // un, group, 0)

    tile = lambda blk: pl.BlockSpec((tq, PAIR), lambda p, i: (i, blk + p))
    whole = pl.BlockSpec((t + PAD_KEYS, PAIR), lambda p, i: (0, p))
    bspec = pl.BlockSpec((2, CHUNK, BAND), lambda p, i: (p, 0, 0))
    return pl.pallas_call(
        body, grid=(N_PAIRS, t // tq),
        in_specs=[tile(q_blk), whole, whole, bspec, tile(0), pl.BlockSpec((None, tq, 2), lambda p, i: (p, i, 0)),
                  tile(do_blk)],
        out_specs=[tile(0), whole, whole, bspec],
        out_shape=[_sds((t, N_PAIRS * PAIR), F32), _sds((t + PAD_KEYS, N_PAIRS * PAIR), F32),
                   _sds((t + PAD_KEYS, N_PAIRS * PAIR), F32), _sds((2 * N_PAIRS, CHUNK, BAND), F32)],
        compiler_params=_params("parallel", "arbitrary"), name=name)(proj, kp, vp, bias, o, lse, do)


SB_TQ = 256
SB_TK = 256
SB_DEAD = -125.0


def _tri(n, strict):
    j = lax.broadcasted_iota(jnp.int32, (n, n), 0)
    s = lax.broadcasted_iota(jnp.int32, (n, n), 1)
    return jnp.where((j > s) if strict else (j >= s), 1.0, 0.0).astype(BF16)


def _suffix_sum(x, tri, exact):
    hi = x.astype(BF16)
    out = jnp.dot(hi, tri, preferred_element_type=F32)
    if exact:
        lo = (x - hi.astype(F32)).astype(BF16)
        out = out + jnp.dot(lo, tri, preferred_element_type=F32)
    return out


def _sb_scores(qh, ks, causal):
    z = lax.dot_general(qh, ks, NT, preferred_element_type=F32)
    lb = jnp.minimum(z, 0.0) - jnp.log(1.0 + jnp.exp(-jnp.abs(z)))
    m = lb - z
    if causal is not None:
        m = jnp.where(causal, m, 0.0)
    return lb, m


def _causal(tq, tk, off):
    return (lax.broadcasted_iota(jnp.int32, (tq, tk), 1) + off * tk) < lax.broadcasted_iota(jnp.int32, (tq, tk), 0)


def sb_fwd(name, proj, q_blk, k_blk, v_blk):
    t = proj.shape[0]
    tq = _row_tile(t, SB_TQ)
    tk = min(SB_TK, tq)
    per = tq // tk

    def body(q_ref, k_ref, v_ref, o_ref):
        i = pl.program_id(1)
        tri = _tri(tk, True)
        qh = _halves(q_ref[...] * QK_SCALE)

        def blocks(kb, carry, off):
            k0 = pl.multiple_of(kb * tk, tk)
            ks, vs = k_ref[pl.ds(k0, tk), :], v_ref[pl.ds(k0, tk), :]
            causal = None if off is None else _causal(tq, tk, off)
            lbm = [_sb_scores(qh[hh], ks, causal) for hh in range(2)]
            afters = [_suffix_sum(lbm[hh][1], tri, False) for hh in range(2)]
            out = []
            for hh in range(2):
                acc, cm = carry[2 * hh], carry[2 * hh + 1]
                w = jnp.exp(lbm[hh][0] + afters[hh] + cm)
                if causal is not None:
                    w = jnp.where(causal, w, 0.0)
                out += [acc + jnp.dot(w.astype(BF16), vs, preferred_element_type=F32),
                        cm + jnp.sum(lbm[hh][1], axis=-1, keepdims=True)]
            return tuple(out)

        def alive(carry):
            return jnp.maximum(jnp.max(carry[1]), jnp.max(carry[3])) > SB_DEAD

        carry = (jnp.zeros((tq, PAIR), F32), jnp.zeros((tq, 1), F32)) * 2
        for off in reversed(range(per)):
            carry = blocks(i * per + off, carry, off)

        def step(c):
            new = blocks(i * per - 1 - c[0], c[2:], None)
            return (c[0] + 1, alive(new)) + new

        out = lax.while_loop(lambda c: jnp.logical_and(c[0] < i * per, c[1]), step,
                             (jnp.int32(0), alive(carry)) + carry)
        o_ref[...] = _merge(out[2], out[4])

    return pl.pallas_call(
        body, grid=(N_PAIRS, t // tq),
        in_specs=[pl.BlockSpec((tq, PAIR), lambda p, i: (i, q_blk + p)),
                  pl.BlockSpec((t, PAIR), lambda p, i: (0, k_blk + p)),
                  pl.BlockSpec((t, PAIR), lambda p, i: (0, v_blk + p))],
        out_specs=pl.BlockSpec((tq, PAIR), lambda p, i: (i, p)),
        out_shape=_sds((t, N_PAIRS * PAIR), F32), compiler_params=_params("parallel", "parallel"), name=name,
    )(proj, proj, proj)


def sb_bwd(name, proj, o, do, q_blk, k_blk, v_blk, do_blk):
    t = proj.shape[0]
    tq = _row_tile(t, SB_TQ)
    tk = min(SB_TK, tq)
    per = tq // tk

    def body(q_ref, k_ref, v_ref, o_ref, do_ref, dq_ref, dk_ref, dv_ref):
        i = pl.program_id(1)

        @pl.when(i == 0)
        def _():
            dk_ref[...] = jnp.zeros_like(dk_ref)
            dv_ref[...] = jnp.zeros_like(dv_ref)

        tri_s, tri_i = _tri(tk, True), _tri(tk, False)
        qh = _halves(q_ref[...] * QK_SCALE)
        doh = _halves(do_ref[...])
        deltas = [jnp.sum(x, axis=-1, keepdims=True) for x in _halves(do_ref[...].astype(F32) * o_ref[...])]

        def blocks(kb, carry, off):
            k0 = pl.multiple_of(kb * tk, tk)
            ks, vs = k_ref[pl.ds(k0, tk), :], v_ref[pl.ds(k0, tk), :]
            causal = None if off is None else _causal(tq, tk, off)
            lbm = [_sb_scores(qh[hh], ks, causal) for hh in range(2)]
            dws = [lax.dot_general(doh[hh], vs, NT, preferred_element_type=F32) for hh in range(2)]
            afters = [_suffix_sum(lbm[hh][1], tri_s, False) for hh in range(2)]
            wbs, es = [], []
            for hh in range(2):
                w = jnp.exp(lbm[hh][0] + afters[hh] + carry[3 * hh + 1])
                if causal is not None:
                    w = jnp.where(causal, w, 0.0)
                wbs.append(w.astype(BF16))
                es.append(wbs[hh].astype(F32) * dws[hh])
            sfx = [_suffix_sum(es[hh], tri_i, True) for hh in range(2)]
            dzs = []
            for hh in range(2):
                left = deltas[hh] - (sfx[hh] + carry[3 * hh + 2])
                sig = jnp.exp(lbm[hh][0])
                dz = es[hh] * (1.0 - sig) - left * sig
                if causal is not None:
                    dz = jnp.where(causal, dz, 0.0)
                dzs.append(dz.astype(BF16))
            dk_ref[pl.ds(k0, tk), :] += (lax.dot_general(dzs[0], qh[0], TN, preferred_element_type=F32)
                                         + lax.dot_general(dzs[1], qh[1], TN, preferred_element_type=F32))
            dv_ref[pl.ds(k0, tk), :] += (lax.dot_general(wbs[0], doh[0], TN, preferred_element_type=F32)
                                         + lax.dot_general(wbs[1], doh[1], TN, preferred_element_type=F32))
            out = []
            for hh in range(2):
                out += [carry[3 * hh] + jnp.dot(dzs[hh], ks, preferred_element_type=F32),
                        carry[3 * hh + 1] + jnp.sum(lbm[hh][1], axis=-1, keepdims=True),
                        carry[3 * hh + 2] + jnp.sum(es[hh], axis=-1, keepdims=True)]
            return tuple(out)

        def alive(carry):
            return jnp.maximum(jnp.max(carry[1]), jnp.max(carry[4])) > SB_DEAD

        zero = jnp.zeros((tq, 1), F32)
        carry = (jnp.zeros((tq, PAIR), F32), zero, zero) * 2
        for off in reversed(range(per)):
            carry = blocks(i * per + off, carry, off)

        def step(c):
            new = blocks(i * per - 1 - c[0], c[2:], None)
            return (c[0] + 1, alive(new)) + new

        out = lax.while_loop(lambda c: jnp.logical_and(c[0] < i * per, c[1]), step,
                             (jnp.int32(0), alive(carry)) + carry)
        dq_ref[...] = _merge(out[2], out[5]) * QK_SCALE

    tile = lambda blk: pl.BlockSpec((tq, PAIR), lambda p, i: (i, blk + p))
    whole = lambda blk: pl.BlockSpec((t, PAIR), lambda p, i: (0, blk + p))
    return pl.pallas_call(
        body, grid=(N_PAIRS, t // tq),
        in_specs=[tile(q_blk), whole(k_blk), whole(v_blk), tile(0), tile(do_blk)],
        out_specs=[tile(0), whole(0), whole(0)],
        out_shape=[_sds((t, N_PAIRS * PAIR), F32)] * 3,
        compiler_params=_params("parallel", "arbitrary"), name=name)(proj, proj, proj, o, do)


def _sigmoid(x):
    return 1.0 / (1.0 + jnp.exp(-x))


def ffn_up(name, h, wg, wu, layer):
    t, d = h.shape
    nb, _, _, f8 = wg.shape
    tm = _row_tile(t, 1024)

    def body(h_ref, wg_ref, wu_ref, g_ref, u_ref, a_ref):
        hv = h_ref[...]
        g = jnp.dot(hv, wg_ref[...], preferred_element_type=F32)
        u = jnp.dot(hv, wu_ref[...], preferred_element_type=F32)
        g_ref[...] = g.astype(BF16)
        u_ref[...] = u.astype(BF16)
        a_ref[...] = (g * _sigmoid(g) * u).astype(BF16)

    wspec = pl.BlockSpec((None, None, d, f8), lambda i, k: (k, layer, 0, 0))
    ospec = pl.BlockSpec((None, tm, f8), lambda i, k: (k, i, 0))
    return pl.pallas_call(
        body, grid=(t // tm, nb), in_specs=[pl.BlockSpec((tm, d), lambda i, k: (i, 0)), wspec, wspec],
        out_specs=[ospec] * 3, out_shape=[_sds((nb, t, f8), BF16)] * 3,
        compiler_params=_params("parallel", "parallel"), name=name)(h, wg, wu)


def ffn_down(name, a, wd, layer):
    nb, t, f8 = a.shape
    d = wd.shape[3]
    tm = _row_tile(t, 1024)
    return _gmm(name, a, wd, grid=(t // tm, nb),
                a_blk=(None, tm, f8), a_idx=lambda i, k: (k, i, 0),
                b_blk=(None, None, f8, d), b_idx=lambda i, k: (k, layer, 0, 0),
                o_blk=(tm, d), o_idx=lambda i, k: (i, 0), out_shape=(t, d), out_dtype=F32, dn=NN, acc_shape=(tm, d))


def ffn_bwd_act(name, dm, wd, g, u, layer):
    t, d = dm.shape
    nb, _, f8, _ = wd.shape
    tm = _row_tile(t, 1024)

    def body(dm_ref, wd_ref, g_ref, u_ref, dg_ref, du_ref):
        da = lax.dot_general(dm_ref[...], wd_ref[...], NT, preferred_element_type=F32)
        gv = g_ref[...].astype(F32)
        uv = u_ref[...].astype(F32)
        sg = _sigmoid(gv)
        dg_ref[...] = (da * uv * sg * (1.0 + gv * (1.0 - sg))).astype(BF16)
        du_ref[...] = (da * gv * sg).astype(BF16)

    bspec = pl.BlockSpec((None, tm, f8), lambda i, k: (k, i, 0))
    return pl.pallas_call(
        body, grid=(t // tm, nb),
        in_specs=[pl.BlockSpec((tm, d), lambda i, k: (i, 0)),
                  pl.BlockSpec((None, None, f8, d), lambda i, k: (k, layer, 0, 0)), bspec, bspec],
        out_specs=[bspec] * 2, out_shape=[_sds((nb, t, f8), BF16)] * 2,
        compiler_params=_params("parallel", "parallel"), name=name)(dm, wd, g, u)


def ffn_bwd_dh(name, dg, du, wg, wu, layer):
    nb, t, f8 = dg.shape
    d = wg.shape[2]
    tm = _row_tile(t, 1024)

    def body(dg_ref, du_ref, wg_ref, wu_ref, o_ref):
        k = pl.program_id(1)
        part = (lax.dot_general(dg_ref[...], wg_ref[...], NT, preferred_element_type=F32)
                + lax.dot_general(du_ref[...], wu_ref[...], NT, preferred_element_type=F32))

        @pl.when(k == 0)
        def _():
            o_ref[...] = part

        @pl.when(k > 0)
        def _():
            o_ref[...] += part

    bspec = pl.BlockSpec((None, tm, f8), lambda i, k: (k, i, 0))
    wspec = pl.BlockSpec((None, None, d, f8), lambda i, k: (k, layer, 0, 0))
    return pl.pallas_call(
        body, grid=(t // tm, nb), in_specs=[bspec, bspec, wspec, wspec],
        out_specs=pl.BlockSpec((tm, d), lambda i, k: (i, 0)), out_shape=_sds((t, d), F32),
        compiler_params=_params("parallel", "arbitrary"), name=name)(dg, du, wg, wu)


def ffn_dw_in(name, h, dact):
    t, d = h.shape
    nb, _, f8 = dact.shape
    tk = _row_tile(t, 1024)
    return _gmm(name, h, dact, grid=(nb, t // tk),
                a_blk=(tk, d), a_idx=lambda b, s: (s, 0), b_blk=(None, tk, f8), b_idx=lambda b, s: (b, s, 0),
                o_blk=(None, d, f8), o_idx=lambda b, s: (b, 0, 0), out_shape=(nb, d, f8), out_dtype=F32, dn=TN,
                acc_shape=(d, f8))


def ffn_dw_down(name, a, dm):
    nb, t, f8 = a.shape
    d = dm.shape[1]
    tk = _row_tile(t, 1024)
    return _gmm(name, a, dm, grid=(nb, t // tk),
                a_blk=(None, tk, f8), a_idx=lambda b, s: (b, s, 0), b_blk=(tk, d), b_idx=lambda b, s: (s, 0),
                o_blk=(None, f8, d), o_idx=lambda b, s: (b, 0, 0), out_shape=(nb, f8, d), out_dtype=F32, dn=TN,
                acc_shape=(f8, d))


GELU_C = math.sqrt(2.0 / math.pi)
GELU_A = 0.044715


def _gelu(x):
    return 0.5 * x * (1.0 + jnp.tanh(GELU_C * (x + GELU_A * x * x * x)))


def _gelu_grad(x):
    th = jnp.tanh(GELU_C * (x + GELU_A * x * x * x))
    return 0.5 * (1.0 + th) + 0.5 * x * (1.0 - th * th) * GELU_C * (1.0 + 3.0 * GELU_A * x * x)


def _neg_expm1(x):
    series = x * (1.0 + x * (0.5 + x * (1.0 / 6.0 + x * (1.0 / 24.0 + x * (1.0 / 120.0 + x * (1.0 / 720.0))))))
    return -jnp.where(x > -0.25, series, jnp.exp(x) - 1.0)


def fir4(name, xs, w, b):
    nj, t, c = xs.shape
    tr = _row_tile(t, 256)

    def body(xs_ref, w_ref, b_ref, o_ref):
        acc = b_ref[...] + w_ref[0:1, :] * xs_ref[0]
        for j in range(1, nj):
            acc = acc + w_ref[j:j + 1, :] * xs_ref[j]
        o_ref[...] = acc

    return pl.pallas_call(
        body, grid=(t // tr,),
        in_specs=[pl.BlockSpec((nj, tr, c), lambda i: (0, i, 0)), pl.BlockSpec((nj, c), lambda i: (0, 0)),
                  pl.BlockSpec((1, c), lambda i: (0, 0))],
        out_specs=pl.BlockSpec((tr, c), lambda i: (i, 0)), out_shape=_sds((t, c), F32),
        compiler_params=_params("parallel"), name=name)(xs, w, b)


def fir4_bwd(name, xs, dy):
    nj, t, c = xs.shape
    tr = _row_tile(t, 256)

    def body(xs_ref, dy_ref, dw_ref, db_ref):
        i = pl.program_id(0)

        @pl.when(i == 0)
        def _():
            dw_ref[...] = jnp.zeros_like(dw_ref)
            db_ref[...] = jnp.zeros_like(db_ref)

        dyv = dy_ref[...]
        db_ref[...] += jnp.sum(dyv, axis=0, keepdims=True)
        for j in range(nj):
            dw_ref[j:j + 1, :] += jnp.sum(dyv * xs_ref[j], axis=0, keepdims=True)

    return pl.pallas_call(
        body, grid=(t // tr,),
        in_specs=[pl.BlockSpec((nj, tr, c), lambda i: (0, i, 0)), pl.BlockSpec((tr, c), lambda i: (i, 0))],
        out_specs=[pl.BlockSpec((nj, c), lambda i: (0, 0)), pl.BlockSpec((1, c), lambda i: (0, 0))],
        out_shape=[_sds((nj, c), F32), _sds((1, c), F32)],
        compiler_params=_params("arbitrary"), name=name)(xs, dy)


def _rg_gate_values(xcv, wa_ref, wi_ref, ba_ref, bi_ref, lam_ref):
    xb = xcv.astype(BF16)
    r = _sigmoid(jnp.dot(xb, wa_ref[...], preferred_element_type=F32) + ba_ref[...])
    ig = _sigmoid(jnp.dot(xb, wi_ref[...], preferred_element_type=F32) + bi_ref[...])
    lam = lam_ref[...]
    sp = jnp.maximum(-lam, 0.0) + jnp.log(1.0 + jnp.exp(-jnp.abs(lam)))
    log_a = -LRU_C * r * sp
    a = jnp.exp(log_a)
    mult = jnp.sqrt(_neg_expm1(2.0 * log_a))
    return xb, r, ig, sp, a, mult


def rg_gates_fwd(name, xc, wa, wi, ba, bi, lam):
    t, c = xc.shape
    nb, cb, _ = wa.shape
    tm = _row_tile(t, 512)

    def body(xc_ref, wa_ref, wi_ref, ba_ref, bi_ref, lam_ref, a_ref, u_ref):
        xcv = xc_ref[...]
        _, _, ig, _, a, mult = _rg_gate_values(xcv, wa_ref, wi_ref, ba_ref, bi_ref, lam_ref)
        a_ref[...] = a
        u_ref[...] = mult * (ig * xcv)

    blk = pl.BlockSpec((tm, cb), lambda i, n: (i, n))
    wsp = pl.BlockSpec((None, cb, cb), lambda i, n: (n, 0, 0))
    vec = pl.BlockSpec((1, cb), lambda i, n: (0, n))
    return pl.pallas_call(
        body, grid=(t // tm, nb), in_specs=[blk, wsp, wsp, vec, vec, vec], out_specs=[blk, blk],
        out_shape=[_sds((t, c), F32)] * 2, compiler_params=_params("parallel", "parallel"), name=name,
    )(xc, wa, wi, ba, bi, lam)


def rg_gates_bwd(name, xc, gu, hprev, wa, wi, ba, bi, lam):
    t, c = xc.shape
    nb, cb, _ = wa.shape
    tm = _row_tile(t, 512)

    def body(xc_ref, gu_ref, hp_ref, wa_ref, wi_ref, ba_ref, bi_ref, lam_ref,
             dxc_ref, dwa_ref, dwi_ref, dba_ref, dbi_ref, dlam_ref):
        i = pl.program_id(1)

        @pl.when(i == 0)
        def _():
            for ref in (dwa_ref, dwi_ref, dba_ref, dbi_ref, dlam_ref):
                ref[...] = jnp.zeros_like(ref)

        xcv = xc_ref[...]
        xb, r, ig, sp, a, mult = _rg_gate_values(xcv, wa_ref, wi_ref, ba_ref, bi_ref, lam_ref)
        gv = gu_ref[...]
        d_ixc = gv * mult
        d_i = d_ixc * xcv
        d_mult = gv * ig * xcv
        d_a = gv * hp_ref[...] - d_mult * a / mult
        d_log_a = d_a * a
        d_r = d_log_a * (-LRU_C * sp)
        sig_neg_lam = 1.0 / (1.0 + jnp.exp(lam_ref[...]))
        dlam_ref[...] += jnp.sum(d_log_a * r, axis=0, keepdims=True) * (LRU_C * sig_neg_lam)
        dpa = d_r * r * (1.0 - r)
        dpi = d_i * ig * (1.0 - ig)
        dba_ref[...] += jnp.sum(dpa, axis=0, keepdims=True)
        dbi_ref[...] += jnp.sum(dpi, axis=0, keepdims=True)
        dpab, dpib = dpa.astype(BF16), dpi.astype(BF16)
        dxc_ref[...] = (d_ixc * ig + lax.dot_general(dpab, wa_ref[...], NT, preferred_element_type=F32)
                        + lax.dot_general(dpib, wi_ref[...], NT, preferred_element_type=F32))
        dwa_ref[...] += lax.dot_general(xb, dpab, TN, preferred_element_type=F32)
        dwi_ref[...] += lax.dot_general(xb, dpib, TN, preferred_element_type=F32)

    blk = pl.BlockSpec((tm, cb), lambda n, i: (i, n))
    wsp = pl.BlockSpec((None, cb, cb), lambda n, i: (n, 0, 0))
    vec = pl.BlockSpec((1, cb), lambda n, i: (0, n))
    return pl.pallas_call(
        body, grid=(nb, t // tm), in_specs=[blk, blk, blk, wsp, wsp, vec, vec, vec],
        out_specs=[blk, wsp, wsp, vec, vec, vec],
        out_shape=[_sds((t, c), F32), _sds((nb, cb, cb), F32), _sds((nb, cb, cb), F32),
                   _sds((1, c), F32), _sds((1, c), F32), _sds((1, c), F32)],
        compiler_params=_params("parallel", "arbitrary"), name=name)(xc, gu, hprev, wa, wi, ba, bi, lam)


SCAN_TS = 256
SCAN_TC = 512


def _tile_scan(a, b, reverse):
    ts = a.shape[0]
    row = lax.broadcasted_iota(jnp.int32, a.shape, 0)
    d = 1
    while d < ts:
        if reverse:
            inside = row < ts - d
            a_sh = jnp.where(inside, pltpu.roll(a, ts - d, 0), 1.0)
            b_sh = jnp.where(inside, pltpu.roll(b, ts - d, 0), 0.0)
        else:
            inside = row >= d
            a_sh = jnp.where(inside, pltpu.roll(a, d, 0), 1.0)
            b_sh = jnp.where(inside, pltpu.roll(b, d, 0), 0.0)
        b = b + a * b_sh
        a = a * a_sh
        d *= 2
    return a, b


def rg_scan_fwd(name, a, u, gate_pre):
    t, c = a.shape
    ts, tc = _row_tile(t, SCAN_TS), _row_tile(c, SCAN_TC)

    def body(a_ref, u_ref, g_ref, h_ref, z_ref, carry_ref):
        s = pl.program_id(1)

        @pl.when(s == 0)
        def _():
            carry_ref[...] = jnp.zeros_like(carry_ref)

        ac, bc = _tile_scan(a_ref[...], u_ref[...], False)
        h = bc + ac * carry_ref[0:1, :]
        h_ref[...] = h
        z_ref[...] = (h * _gelu(g_ref[...])).astype(BF16)
        carry_ref[0:1, :] = h[ts - 1:ts, :]

    blk = pl.BlockSpec((ts, tc), lambda j, s: (s, j))
    return pl.pallas_call(
        body, grid=(c // tc, t // ts), in_specs=[blk, blk, blk], out_specs=[blk, blk],
        out_shape=[_sds((t, c), F32), _sds((t, c), BF16)], scratch_shapes=[pltpu.VMEM((8, tc), F32)],
        compiler_params=_params("parallel", "arbitrary"), name=name)(a, u, gate_pre)


def rg_scan_bwd(name, a_next, hs, gate_pre, dz):
    t, c = hs.shape
    ts, tc = _row_tile(t, SCAN_TS), _row_tile(c, SCAN_TC)
    nt = t // ts

    def body(an_ref, h_ref, g_ref, dz_ref, gu_ref, dgate_ref, carry_ref):
        s = pl.program_id(1)

        @pl.when(s == 0)
        def _():
            carry_ref[...] = jnp.zeros_like(carry_ref)

        gate = g_ref[...]
        dzv = dz_ref[...]
        dgate_ref[...] = (dzv * h_ref[...] * _gelu_grad(gate)).astype(BF16)
        ac, bc = _tile_scan(an_ref[...], dzv * _gelu(gate), True)
        gu = bc + ac * carry_ref[0:1, :]
        gu_ref[...] = gu
        carry_ref[0:1, :] = gu[0:1, :]

    blk = pl.BlockSpec((ts, tc), lambda j, s: (nt - 1 - s, j))
    return pl.pallas_call(
        body, grid=(c // tc, nt), in_specs=[blk, blk, blk, blk], out_specs=[blk, blk],
        out_shape=[_sds((t, c), F32), _sds((t, c), BF16)], scratch_shapes=[pltpu.VMEM((8, tc), F32)],
        compiler_params=_params("parallel", "arbitrary"), name=name)(a_next, hs, gate_pre, dz)


def _shift_down(x, k):
    return jnp.pad(x, ((k, 0), (0, 0)))[:x.shape[0]] if k else x


def _shift_up(x, k):
    return jnp.pad(x, ((0, k), (0, 0)))[k:] if k else x


QA_BLK, KA_BLK, VA_BLK, QS_BLK, KS_BLK, VS_BLK = (g * N_PAIRS for g in range(6))


TOEP_W = 640
TOEP_FLAT = 320
TABLE_LOW = 193


def rel_bias_matrix(name, table):
    h = table.shape[0]
    diag = jnp.concatenate([jnp.repeat(table[:, 2 * REL_CLIP:], TOEP_FLAT, axis=1),
                            jnp.flip(table[:, TABLE_LOW:2 * REL_CLIP], axis=1),
                            jnp.zeros((h, 1), table.dtype)], axis=1)[:, None, :]

    def body(v_ref, o_ref):
        rows = jnp.broadcast_to(v_ref[...], (CHUNK, TOEP_W))
        o_ref[...] = pltpu.roll(rows, TOEP_W - (CHUNK - 1), 1, stride=1, stride_axis=0)

    out = pl.pallas_call(
        body, grid=(h,), in_specs=[pl.BlockSpec((None, 1, TOEP_W), lambda hh: (hh, 0, 0))],
        out_specs=pl.BlockSpec((None, CHUNK, TOEP_W), lambda hh: (hh, 0, 0)),
        out_shape=_sds((h, CHUNK, TOEP_W), F32), compiler_params=_params("parallel"), name=name)(diag)
    return out[:, :, :BAND]


def rel_bias_grad(name, dbias):
    h = dbias.shape[0]
    flipped = jnp.pad(jnp.flip(dbias, axis=1), ((0, 0), (0, 0), (0, TOEP_W - BAND)))

    def body(x_ref, o_ref):
        skew = pltpu.roll(x_ref[...], 0, 1, stride=1, stride_axis=0)
        col = jnp.sum(skew, axis=0, keepdims=True)
        lane = lax.broadcasted_iota(jnp.int32, col.shape, 1)
        flat = jnp.sum(jnp.where(lane < TOEP_FLAT, col, 0.0), axis=1, keepdims=True)
        o_ref[...] = jnp.where(lane == TOEP_W - 1, flat, col)

    out = pl.pallas_call(
        body, grid=(h,), in_specs=[pl.BlockSpec((None, CHUNK, TOEP_W), lambda hh: (hh, 0, 0))],
        out_specs=pl.BlockSpec((None, 1, TOEP_W), lambda hh: (hh, 0, 0)),
        out_shape=_sds((h, 1, TOEP_W), F32), compiler_params=_params("parallel"), name=name)(flipped)[:, 0, :]
    return jnp.concatenate([jnp.zeros((h, TABLE_LOW), F32), jnp.flip(out[:, TOEP_FLAT:TOEP_W - 1], axis=1),
                            out[:, TOEP_W - 1:]], axis=1)


def attn_layer_fwd(tag, x, w):
    h = rmsnorm_fwd(tag + "_norm", x, w["g_pre"])
    proj = mm_nn_wblk(tag + "_proj", h, w["w_in"], w["idx"], BF16)
    width = N_PAIRS * PAIR
    pad = lambda a: jnp.pad(a, ((PAD_KEYS, 0), (0, 0)))
    kap, vap = pad(proj[:, width:2 * width]), pad(proj[:, 2 * width:3 * width])
    bias = rel_bias_matrix(tag + "_bias", w["rel_bias"])
    oa, lse = attn_a_fwd(tag + "_a", proj, kap, vap, bias, QA_BLK)
    ob = sb_fwd(tag + "_sb", proj, QS_BLK, KS_BLK, VS_BLK)
    o = jnp.concatenate([oa, ob], axis=1).astype(BF16)
    m = mm_nn(tag + "_out", o, w["w_out"], F32)
    x1 = resid_norm_fwd(tag + "_res", x, m, w["g_post"])
    return x1, (x, h, proj, kap, vap, bias, oa, lse, ob, o, m)


def attn_layer_bwd(tag, dx1, saved, w):
    x, h, proj, kap, vap, bias, oa, lse, ob, o, m = saved
    dm, dg_post = norm_bwd(tag + "_dpost", dx1, m, w["g_post"], None, BF16)
    d_w_out = mm_tn(tag + "_dwout", o, dm, F32)
    do = mm_nt(tag + "_do", dm, w["w_out"], BF16)
    dqa, dkap, dvap, dbias = attn_a_bwd(tag + "_da", proj, kap, vap, bias, oa, lse, do, QA_BLK, 0)
    dqs, dks, dvs = sb_bwd(tag + "_dsb", proj, ob, do, QS_BLK, KS_BLK, VS_BLK, N_PAIRS)
    d_rel = rel_bias_grad(tag + "_dbias", dbias)
    dproj = jnp.concatenate([dqa, dkap[PAD_KEYS:], dvap[PAD_KEYS:], dqs, dks, dvs], axis=1).astype(BF16)
    d_w_in = mm_tn_oblk(tag + "_dwin", h, dproj, w["w_in"].shape[3], F32)
    dh = mm_nt_wblk(tag + "_dh", dproj, w["w_in"], w["idx"], F32)
    dx, dg_pre = norm_bwd(tag + "_dpre", dh, x, w["g_pre"], dx1, F32)
    return dx, dict(w_in=d_w_in, w_out=d_w_out, rel_bias=d_rel, g_pre=dg_pre, g_post=dg_post)


def rg_layer_fwd(tag, x, w):
    c = w["w_out"].shape[0]
    h = rmsnorm_fwd(tag + "_norm", x, w["g_pre"])
    proj = mm_nn_wblk(tag + "_proj", h, w["w_in"], w["idx"], F32)
    gate_pre, xr = proj[:, :c], proj[:, c:]
    xs = jnp.stack([_shift_down(xr, 3 - j) for j in range(4)])
    xc = fir4(tag + "_conv", xs, w["conv_w"], w["conv_b"])
    a, u = rg_gates_fwd(tag + "_gates", xc, w["w_a"], w["w_i"], w["b_a"], w["b_i"], w["lam"])
    hs, z = rg_scan_fwd(tag + "_scan", a, u, gate_pre)
    m = mm_nn(tag + "_out", z, w["w_out"], F32)
    x1 = resid_norm_fwd(tag + "_res", x, m, w["g_post"])
    return x1, (x, h, gate_pre, xs, xc, a, hs, z, m)


def rg_layer_bwd(tag, dx1, saved, w):
    x, h, gate_pre, xs, xc, a, hs, z, m = saved
    dm, dg_post = norm_bwd(tag + "_dpost", dx1, m, w["g_post"], None, BF16)
    d_w_out = mm_tn(tag + "_dwout", z, dm, F32)
    dz = mm_nt(tag + "_dz", dm, w["w_out"], F32)
    gu, dgate = rg_scan_bwd(tag + "_dscan", _shift_up(a, 1), hs, gate_pre, dz)
    dxc, d_w_a, d_w_i, d_b_a, d_b_i, d_lam = rg_gates_bwd(
        tag + "_dgates", xc, gu, _shift_down(hs, 1), w["w_a"], w["w_i"], w["b_a"], w["b_i"], w["lam"])
    d_conv_w, d_conv_b = fir4_bwd(tag + "_dconvw", xs, dxc)
    dxs = jnp.stack([_shift_up(dxc, 3 - j) for j in range(4)])
    dxr = fir4(tag + "_dconv", dxs, w["conv_w"], jnp.zeros_like(w["conv_b"]))
    dproj = jnp.concatenate([dgate, dxr.astype(BF16)], axis=1)
    d_w_in = mm_tn_oblk(tag + "_dwin", h, dproj, w["w_in"].shape[3], F32)
    dh = mm_nt_wblk(tag + "_dh", dproj, w["w_in"], w["idx"], F32)
    dx, dg_pre = norm_bwd(tag + "_dpre", dh, x, w["g_pre"], dx1, F32)
    return dx, dict(w_in=d_w_in, w_out=d_w_out, conv_w=d_conv_w, conv_b=d_conv_b, w_a=d_w_a, w_i=d_w_i,
                    b_a=d_b_a, b_i=d_b_i, lam=d_lam, g_pre=dg_pre, g_post=dg_post)


def ffn_layer_fwd(tag, x, w):
    h = rmsnorm_fwd(tag + "_norm", x, w["g_pre"])
    g, u, a = ffn_up(tag + "_up", h, w["w_gate"], w["w_up"], w["idx"])
    f = ffn_down(tag + "_down", a, w["w_down"], w["idx"])
    x1 = resid_norm_fwd(tag + "_res", x, f, w["g_post"])
    return x1, (x, h, g, u, a, f)


def ffn_layer_bwd(tag, dx1, saved, w):
    x, h, g, u, a, f = saved
    dm, dg_post = norm_bwd(tag + "_dpost", dx1, f, w["g_post"], None, BF16)
    d_w_down = ffn_dw_down(tag + "_dwdown", a, dm)
    dg, du = ffn_bwd_act(tag + "_dact", dm, w["w_down"], g, u, w["idx"])
    d_w_gate = ffn_dw_in(tag + "_dwgate", h, dg)
    d_w_up = ffn_dw_in(tag + "_dwup", h, du)
    dh = ffn_bwd_dh(tag + "_dh", dg, du, w["w_gate"], w["w_up"], w["idx"])
    dx, dg_pre = norm_bwd(tag + "_dpre", dh, x, w["g_pre"], dx1, F32)
    return dx, dict(w_gate=d_w_gate, w_up=d_w_up, w_down=d_w_down, g_pre=dg_pre, g_post=dg_post)


def _place():
    return lax.axis_index("x"), lax.axis_index("y"), lax.axis_index("c")


def all_gather(name, blks):
    n = len(blks)

    def body(*refs):
        x_refs, out_refs = refs[:n], refs[n:2 * n]
        send_sems, recv_sems, local_sems = refs[2 * n:]
        x, y, cc = _place()
        me, sibling = (x, y, cc), (x, y, 1 - cc)
        chips = [(1 - x, y), (x, 1 - y), (1 - x, 1 - y)]

        def slot(a, px, py, pc):
            return out_refs[a].at[4 * px + 2 * py + pc]

        def copy(a, k, block, to, src=None):
            return pltpu.make_async_remote_copy(
                src_ref=slot(a, *block) if src is None else src, dst_ref=slot(a, *block),
                send_sem=send_sems.at[7 * a + k], recv_sem=recv_sems.at[7 * a + k], device_id=to, device_id_type=MESH)

        mine = [pltpu.make_async_copy(x_refs[a], slot(a, *me), local_sems.at[a]) for a in range(n)]
        first = []
        for a in range(n):
            mine[a].start()
            first.append(copy(a, 0, me, sibling, src=x_refs[a]))
            first += [copy(a, 1 + j, me, (*chip, cc), src=x_refs[a]) for j, chip in enumerate(chips)]
        for cp in first:
            cp.start()
        passed = []
        for j, chip in enumerate(chips):
            for a in range(n):
                copy(a, 1 + j, (*chip, cc), me).wait_recv()
                passed.append(copy(a, 4 + j, (*chip, cc), sibling))
                passed[-1].start()
        for a in range(n):
            copy(a, 0, sibling, me).wait_recv()
            for j, chip in enumerate(chips):
                copy(a, 4 + j, (*chip, 1 - cc), me).wait_recv()
        for cp in first + passed:
            cp.wait_send()
        for cp in mine:
            cp.wait()

    return pl.pallas_call(
        body, out_shape=[_sds((N_DEV,) + b.shape, b.dtype) for b in blks], in_specs=[ANY] * n, out_specs=[ANY] * n,
        scratch_shapes=[pltpu.SemaphoreType.DMA((7 * n,)), pltpu.SemaphoreType.DMA((7 * n,)),
                        pltpu.SemaphoreType.DMA((n,))],
        name=name)(*blks)


def exchange_pair(name, gs):
    n = len(gs)
    nchip = 4

    def body(*refs):
        g_refs, land_refs = refs[:n], refs[n:2 * n]
        send_sems, recv_sems = refs[2 * n:]
        x, y, cc = _place()
        copies = [pltpu.make_async_remote_copy(
            src_ref=g_refs[a].at[j, 1 - cc], dst_ref=land_refs[a].at[j], send_sem=send_sems.at[nchip * a + j],
            recv_sem=recv_sems.at[nchip * a + j], device_id=(x, y, 1 - cc), device_id_type=MESH)
            for a in range(n) for j in range(nchip)]
        for cp in copies:
            cp.start()
        for cp in copies:
            cp.wait()

    return pl.pallas_call(
        body, out_shape=[_sds((nchip,) + g.shape[2:], g.dtype) for g in gs], in_specs=[ANY] * n, out_specs=[ANY] * n,
        scratch_shapes=[pltpu.SemaphoreType.DMA((nchip * n,)), pltpu.SemaphoreType.DMA((nchip * n,))],
        name=name)(*gs)


def pair_sum(name, g, land, core):
    nchip, _, r, c = g.shape
    tr = _row_tile(r, 128)

    def body(core_ref, g_ref, l_ref, o_ref):
        o_ref[...] = g_ref[...] + l_ref[...]

    return pl.pallas_call(
        body,
        grid_spec=pltpu.PrefetchScalarGridSpec(
            num_scalar_prefetch=1, grid=(nchip, r // tr),
            in_specs=[pl.BlockSpec((None, None, tr, c), lambda j, i, core_ref: (j, core_ref[0], i, 0)),
                      pl.BlockSpec((None, tr, c), lambda j, i, core_ref: (j, i, 0))],
            out_specs=pl.BlockSpec((None, tr, c), lambda j, i, core_ref: (j, i, 0))),
        out_shape=_sds((nchip, r, c), g.dtype), compiler_params=_params("parallel", "parallel"), name=name,
    )(core, g, land)


def exchange_chips(name, ps):
    n = len(ps)

    def body(*refs):
        p_refs, land_refs = refs[:n], refs[n:2 * n]
        send_sems, recv_sems, local_sems = refs[2 * n:]
        x, y, cc = _place()
        mine = 2 * x + y
        chips = [(1 - x, y), (x, 1 - y), (1 - x, 1 - y)]
        own = [pltpu.make_async_copy(p_refs[a].at[mine], land_refs[a].at[mine], local_sems.at[a]) for a in range(n)]
        for cp in own:
            cp.start()
        sends = [pltpu.make_async_remote_copy(
            src_ref=p_refs[a].at[2 * px + py], dst_ref=land_refs[a].at[mine], send_sem=send_sems.at[3 * a + k],
            recv_sem=recv_sems.at[3 * a + k], device_id=(px, py, cc), device_id_type=MESH)
            for a in range(n) for k, (px, py) in enumerate(chips)]
        for cp in sends:
            cp.start()
        for a in range(n):
            for k, (px, py) in enumerate(chips):
                pltpu.make_async_remote_copy(
                    src_ref=p_refs[a].at[mine], dst_ref=land_refs[a].at[2 * px + py], send_sem=send_sems.at[3 * a + k],
                    recv_sem=recv_sems.at[3 * a + k], device_id=(px, py, cc), device_id_type=MESH).wait_recv()
        for cp in sends:
            cp.wait_send()
        for cp in own:
            cp.wait()

    return pl.pallas_call(
        body, out_shape=[_sds(p.shape, p.dtype) for p in ps], in_specs=[ANY] * n, out_specs=[ANY] * n,
        scratch_shapes=[pltpu.SemaphoreType.DMA((3 * n,)), pltpu.SemaphoreType.DMA((3 * n,)),
                        pltpu.SemaphoreType.DMA((n,))],
        name=name)(*ps)


def adamw(name, parts, w, m, v):
    npart, r, c = parts.shape
    tr = _row_tile(r, 128)
    c1 = 1.0 / (1.0 - ADAM_B1 ** ADAM_STEP)
    c2 = 1.0 / (1.0 - ADAM_B2 ** ADAM_STEP)

    def body(p_ref, w_ref, m_ref, v_ref, g_ref, d_ref, nm_ref, nv_ref):
        g = p_ref[0]
        for j in range(1, npart):
            g = g + p_ref[j]
        nm = ADAM_B1 * m_ref[...] + (1.0 - ADAM_B1) * g
        nv = ADAM_B2 * v_ref[...] + (1.0 - ADAM_B2) * (g * g)
        g_ref[...] = g
        nm_ref[...] = nm
        nv_ref[...] = nv
        d_ref[...] = -ADAM_LR * ((nm * c1) / (jnp.sqrt(nv * c2) + ADAM_EPS) + ADAM_WD * w_ref[...])

    row = pl.BlockSpec((tr, c), lambda i: (i, 0))
    return pl.pallas_call(
        body, grid=(r // tr,), in_specs=[pl.BlockSpec((npart, tr, c), lambda i: (0, i, 0)), row, row, row],
        out_specs=[row] * 4, out_shape=[_sds((r, c), F32)] * 4, compiler_params=_params("parallel"), name=name,
    )(parts, w, m, v)


def _pack(arrays, dtype, row_multiple):
    flat = jnp.concatenate([a.astype(dtype).reshape(-1) for a in arrays])
    per = row_multiple * LANES
    total = -(-flat.shape[0] // per) * per
    return jnp.pad(flat, (0, total - flat.shape[0])).reshape(total // LANES, LANES)


def _pack_blocked(arrays, dtype, row_multiple):
    flat = jnp.concatenate([a.astype(dtype).reshape(N_DEV, -1) for a in arrays], axis=1)
    per = row_multiple * LANES
    total = -(-flat.shape[1] // per) * per
    return jnp.pad(flat, ((0, 0), (0, total - flat.shape[1]))).reshape(N_DEV, total // LANES, LANES)


def _unpack(buf, shapes, lead=()):
    flat = buf.reshape(lead + (-1,))
    out, off = [], 0
    for s in shapes:
        n = math.prod(s)
        out.append(flat[..., off:off + n].reshape(lead + tuple(s)))
        off += n
    return out


def _to_blocked(full, ax):
    s = full.shape
    return jnp.moveaxis(full.reshape(s[:ax] + (N_DEV, s[ax] // N_DEV) + s[ax + 1:]), ax, 0)


def _from_blocked(blk, ax):
    moved = jnp.moveaxis(blk, 0, ax)
    s = moved.shape
    return moved.reshape(s[:ax] + (s[ax] * s[ax + 1],) + s[ax + 2:])


SMALL = ("rg_conv_w", "rg_conv_b", "rg_b_a", "rg_b_i", "rg_lambda")
BIG = ("attn_w_in", "attn_w_out", "rg_w_in", "rg_w_a", "rg_w_i", "rg_w_out", "ffn_w_gate", "ffn_w_up", "ffn_w_down")


def kernel(x, attn_w_in, attn_rel_bias, attn_w_out, rg_w_in, rg_conv_w, rg_conv_b, rg_w_a, rg_b_a, rg_w_i, rg_b_i, rg_lambda, rg_w_out, norm_mix_pre, norm_mix_post, norm_ffn_pre, norm_ffn_post, ffn_w_gate, ffn_w_up, ffn_w_down, loss_target, m_attn_w_in, m_attn_rel_bias, m_attn_w_out, m_rg_w_in, m_rg_conv_w, m_rg_conv_b, m_rg_w_a, m_rg_b_a, m_rg_w_i, m_rg_b_i, m_rg_lambda, m_rg_w_out, m_norm_mix_pre, m_norm_mix_post, m_norm_ffn_pre, m_norm_ffn_post, m_ffn_w_gate, m_ffn_w_up, m_ffn_w_down, v_attn_w_in, v_attn_rel_bias, v_attn_w_out, v_rg_w_in, v_rg_conv_w, v_rg_conv_b, v_rg_w_a, v_rg_b_a, v_rg_w_i, v_rg_b_i, v_rg_lambda, v_rg_w_out, v_norm_mix_pre, v_norm_mix_post, v_norm_ffn_pre, v_norm_ffn_post, v_ffn_w_gate, v_ffn_w_up, v_ffn_w_down):
    w_loc = dict(attn_w_in=attn_w_in, attn_rel_bias=attn_rel_bias, attn_w_out=attn_w_out, rg_w_in=rg_w_in,
                 rg_conv_w=rg_conv_w, rg_conv_b=rg_conv_b, rg_w_a=rg_w_a, rg_b_a=rg_b_a, rg_w_i=rg_w_i, rg_b_i=rg_b_i,
                 rg_lambda=rg_lambda, rg_w_out=rg_w_out, norm_mix_pre=norm_mix_pre, norm_mix_post=norm_mix_post,
                 norm_ffn_pre=norm_ffn_pre, norm_ffn_post=norm_ffn_post, ffn_w_gate=ffn_w_gate, ffn_w_up=ffn_w_up,
                 ffn_w_down=ffn_w_down)
    m_loc = dict(attn_w_in=m_attn_w_in, attn_rel_bias=m_attn_rel_bias, attn_w_out=m_attn_w_out, rg_w_in=m_rg_w_in,
                 rg_conv_w=m_rg_conv_w, rg_conv_b=m_rg_conv_b, rg_w_a=m_rg_w_a, rg_b_a=m_rg_b_a, rg_w_i=m_rg_w_i,
                 rg_b_i=m_rg_b_i, rg_lambda=m_rg_lambda, rg_w_out=m_rg_w_out, norm_mix_pre=m_norm_mix_pre,
                 norm_mix_post=m_norm_mix_post, norm_ffn_pre=m_norm_ffn_pre, norm_ffn_post=m_norm_ffn_post,
                 ffn_w_gate=m_ffn_w_gate, ffn_w_up=m_ffn_w_up, ffn_w_down=m_ffn_w_down)
    v_loc = dict(attn_w_in=v_attn_w_in, attn_rel_bias=v_attn_rel_bias, attn_w_out=v_attn_w_out, rg_w_in=v_rg_w_in,
                 rg_conv_w=v_rg_conv_w, rg_conv_b=v_rg_conv_b, rg_w_a=v_rg_w_a, rg_b_a=v_rg_b_a, rg_w_i=v_rg_w_i,
                 rg_b_i=v_rg_b_i, rg_lambda=v_rg_lambda, rg_w_out=v_rg_w_out, norm_mix_pre=v_norm_mix_pre,
                 norm_mix_post=v_norm_mix_post, norm_ffn_pre=v_norm_ffn_pre, norm_ffn_post=v_norm_ffn_post,
                 ffn_w_gate=v_ffn_w_gate, ffn_w_up=v_ffn_w_up, ffn_w_down=v_ffn_w_down)
    axis_of = dict(SHARDED)
    xt, target = x[0], loss_target[0]
    d_model = xt.shape[1]
    rows2d = lambda a: a.reshape(-1, a.shape[-1])
    small_shapes = [w_loc[n].shape for n in SMALL]

    gathered = all_gather("gather_weights", [rows2d(w_loc[n]).astype(BF16) for n in BIG]
                          + [_pack([w_loc[n] for n in SMALL], F32, 8)])
    blocked = {n: g.reshape((N_DEV,) + w_loc[n].shape) for n, g in zip(BIG, gathered)}
    blocked.update(zip(SMALL, _unpack(gathered[-1], small_shapes, (N_DEV,))))
    full = {n: _from_blocked(blocked[n], axis_of[n]) for n in SMALL}
    row = lambda a: a.reshape(1, -1).astype(F32)
    square = lambda rows8: rows8.reshape(-1, rows8.shape[-1])
    gates = lambda g: jnp.swapaxes(g, 0, 1).reshape(LRU_BLOCKS, -1, g.shape[-1])

    def layer_weights(layer):
        j = layer // 2
        norms = dict(g_pre=row(norm_mix_pre[layer]), g_post=row(norm_mix_post[layer]), idx=j)
        if layer % 2 == 0:
            mix = dict(w_in=blocked["attn_w_in"], w_out=square(blocked["attn_w_out"][:, j]),
                       rel_bias=attn_rel_bias[j], **norms)
        else:
            mix = dict(w_in=blocked["rg_w_in"], w_out=square(blocked["rg_w_out"][:, j]),
                       conv_w=full["rg_conv_w"][j][:, 0, :], conv_b=row(full["rg_conv_b"][j]),
                       w_a=gates(blocked["rg_w_a"][:, j]), w_i=gates(blocked["rg_w_i"][:, j]),
                       b_a=row(full["rg_b_a"][j]), b_i=row(full["rg_b_i"][j]), lam=row(full["rg_lambda"][j]), **norms)
        ffn = dict(w_gate=blocked["ffn_w_gate"], w_up=blocked["ffn_w_up"], w_down=blocked["ffn_w_down"], idx=layer,
                   g_pre=row(norm_ffn_pre[layer]), g_post=row(norm_ffn_post[layer]))
        return mix, ffn

    act, tape = xt, []
    for layer in range(DEPTH):
        mix_w, ffn_w = layer_weights(layer)
        mixer_fwd = attn_layer_fwd if layer % 2 == 0 else rg_layer_fwd
        act, saved_mix = mixer_fwd(f"l{layer}_mix", act, mix_w)
        act, saved_ffn = ffn_layer_fwd(f"l{layer}_ffn", act, ffn_w)
        tape.append((mix_w, ffn_w, saved_mix, saved_ffn))
    dact, sq = loss_grad("loss", act, target)
    loss = lax.psum(0.5 * jnp.sum(sq) / d_model, ("x", "y", "c"))

    grads = {}
    for layer in reversed(range(DEPTH)):
        mix_w, ffn_w, saved_mix, saved_ffn = tape[layer]
        dact, grads[("ffn", layer)] = ffn_layer_bwd(f"l{layer}_ffn", dact, saved_ffn, ffn_w)
        mixer_bwd = attn_layer_bwd if layer % 2 == 0 else rg_layer_bwd
        dact, grads[("mix", layer)] = mixer_bwd(f"l{layer}_mix", dact, saved_mix, mix_w)
    attn_g = [grads[("mix", l)] for l in range(0, DEPTH, 2)]
    rg_g = [grads[("mix", l)] for l in range(1, DEPTH, 2)]
    ffn_g = [grads[("ffn", l)] for l in range(DEPTH)]
    stack = lambda gs, key: jnp.stack([g[key] for g in gs])
    by_owner = lambda gs, key, f: jnp.stack([f(g[key]) for g in gs], axis=1)
    rows8 = lambda a: a.reshape(N_DEV, -1, a.shape[-1])
    ungates = lambda a: jnp.swapaxes(a.reshape(LRU_BLOCKS, N_DEV, -1, a.shape[-1]), 0, 1)
    same = lambda a: a
    blocked_g = dict(
        attn_w_in=by_owner(attn_g, "w_in", same), attn_w_out=by_owner(attn_g, "w_out", rows8),
        rg_w_in=by_owner(rg_g, "w_in", same), rg_w_out=by_owner(rg_g, "w_out", rows8),
        rg_w_a=by_owner(rg_g, "w_a", ungates), rg_w_i=by_owner(rg_g, "w_i", ungates),
        ffn_w_gate=by_owner(ffn_g, "w_gate", same), ffn_w_up=by_owner(ffn_g, "w_up", same),
        ffn_w_down=by_owner(ffn_g, "w_down", same))
    contrib = dict(
        attn_rel_bias=stack(attn_g, "rel_bias"), rg_conv_w=stack(rg_g, "conv_w")[:, :, None, :],
        rg_conv_b=stack(rg_g, "conv_b")[:, 0], rg_b_a=stack(rg_g, "b_a").reshape(rg_b_a.shape[0], LRU_BLOCKS, -1),
        rg_b_i=stack(rg_g, "b_i").reshape(rg_b_i.shape[0], LRU_BLOCKS, -1), rg_lambda=stack(rg_g, "lam")[:, 0],
        norm_mix_pre=jnp.concatenate([grads[("mix", l)]["g_pre"] for l in range(DEPTH)]),
        norm_mix_post=jnp.concatenate([grads[("mix", l)]["g_post"] for l in range(DEPTH)]),
        norm_ffn_pre=jnp.concatenate([g["g_pre"] for g in ffn_g]),
        norm_ffn_post=jnp.concatenate([g["g_post"] for g in ffn_g]),
    )
    small_g = _pack_blocked([_to_blocked(contrib[n], axis_of[n]) for n in SMALL], F32, 8)

    slabs = [blocked_g[n].reshape(4, 2, -1, blocked_g[n].shape[-1]) for n in BIG] + [small_g.reshape(4, 2, -1, LANES)]
    core = lax.axis_index("c").astype(jnp.int32).reshape(1)
    from_sibling = exchange_pair("rs_pair", slabs)
    pairs = [pair_sum(f"rs_pair_sum_{i}", g, l, core) for i, (g, l) in enumerate(zip(slabs, from_sibling))]
    by_chip = exchange_chips("rs_chips", pairs)
    result = {}
    kinds = ("grad", "delta", "new_m", "new_v")
    for n, parts in zip(BIG, by_chip):
        outs = adamw("adamw_" + n, parts, *[rows2d(d[n]) for d in (w_loc, m_loc, v_loc)])
        for kind, a in zip(kinds, outs):
            result[(kind, n)] = a.reshape(w_loc[n].shape)
    outs = adamw("adamw_small", by_chip[-1], *[_pack([d[n] for n in SMALL], F32, 8) for d in (w_loc, m_loc, v_loc)])
    for kind, buf in zip(kinds, outs):
        result.update({(kind, n): a for n, a in zip(SMALL, _unpack(buf, small_shapes))})
    rep_shapes = [w_loc[n].shape for n in REPLICATED]
    rep_parts, = all_gather("gather_rep_grads", [_pack([contrib[n] for n in REPLICATED], F32, 8)])
    outs = adamw("adamw_replicated", rep_parts, *[_pack([d[n] for n in REPLICATED], F32, 8)
                                                  for d in (w_loc, m_loc, v_loc)])
    for kind, buf in zip(kinds, outs):
        result.update({(kind, n): a for n, a in zip(REPLICATED, _unpack(buf, rep_shapes))})
    return (loss, dact[None], *[result[(kind, n)] for kind in kinds for n in WEIGHTS])
```

```python
import functools
import math

import jax
import jax.numpy as jnp
from jax import lax
from jax.experimental import pallas as pl
from jax.experimental.pallas import tpu as pltpu

F32 = jnp.float32
BF16 = jnp.bfloat16

N_DEV = 8
DEPTH = 4
CHUNK = 64
N_LEFT = 8
BAND = (N_LEFT + 1) * CHUNK
PAD_KEYS = N_LEFT * CHUNK
HEAD_DIM = 64
N_HEADS = 8
REL_CLIP = 256
LRU_BLOCKS = 4
LRU_C = 8.0
RMS_EPS = 1e-6
QK_SCALE = HEAD_DIM ** -0.5

ADAM_LR = 0.001
ADAM_B1 = 0.9
ADAM_B2 = 0.999
ADAM_EPS = 1e-08
ADAM_WD = 0.01
ADAM_STEP = 10

LANES = 1024
V7X_VMEM_LIMIT = 56 * 1024 * 1024

MESH = pl.DeviceIdType.MESH
ANY = pl.BlockSpec(memory_space=pl.ANY)

SHARDED = (
    ("attn_w_in", 2), ("attn_w_out", 1), ("rg_w_in", 2), ("rg_conv_w", 3), ("rg_conv_b", 1),
    ("rg_w_a", 2), ("rg_b_a", 2), ("rg_w_i", 2), ("rg_b_i", 2), ("rg_lambda", 1), ("rg_w_out", 1),
    ("ffn_w_gate", 2), ("ffn_w_up", 2), ("ffn_w_down", 1),
)
REPLICATED = ("attn_rel_bias", "norm_mix_pre", "norm_mix_post", "norm_ffn_pre", "norm_ffn_post")
WEIGHTS = ("attn_w_in", "attn_rel_bias", "attn_w_out", "rg_w_in", "rg_conv_w", "rg_conv_b", "rg_w_a", "rg_b_a",
           "rg_w_i", "rg_b_i", "rg_lambda", "rg_w_out", "norm_mix_pre", "norm_mix_post", "norm_ffn_pre",
           "norm_ffn_post", "ffn_w_gate", "ffn_w_up", "ffn_w_down")


def _params(*dims):
    return pltpu.CompilerParams(dimension_semantics=dims or None, vmem_limit_bytes=V7X_VMEM_LIMIT)


def _sds(shape, dtype):
    return jax.ShapeDtypeStruct(tuple(shape), dtype)


def _row_tile(n, pref):
    t = min(n, pref)
    assert n % t == 0, (n, pref)
    return t


NN = (((1,), (0,)), ((), ()))
NT = (((1,), (1,)), ((), ()))
TN = (((0,), (0,)), ((), ()))


def _gmm(name, a, b, *, grid, a_blk, a_idx, b_blk, b_idx, o_blk, o_idx, out_shape, out_dtype, dn, acc_shape):
    nk = grid[-1]
    kax = len(grid) - 1

    def body(a_ref, b_ref, o_ref, acc_ref):
        part = lax.dot_general(a_ref[...], b_ref[...], dn, preferred_element_type=F32)
        if nk == 1:
            o_ref[...] = part.astype(o_ref.dtype)
            return
        k = pl.program_id(kax)

        @pl.when(k == 0)
        def _():
            acc_ref[...] = part

        @pl.when(k > 0)
        def _():
            acc_ref[...] += part

        @pl.when(k == nk - 1)
        def _():
            o_ref[...] = acc_ref[...].astype(o_ref.dtype)

    return pl.pallas_call(
        body, grid=grid,
        in_specs=[pl.BlockSpec(a_blk, a_idx), pl.BlockSpec(b_blk, b_idx)],
        out_specs=pl.BlockSpec(o_blk, o_idx),
        out_shape=_sds(out_shape, out_dtype),
        scratch_shapes=[pltpu.VMEM(acc_shape, F32)],
        compiler_params=_params(*(["parallel"] * kax + ["arbitrary"])),
        name=name,
    )(a, b)


def mm_nn(name, a, b, out_dtype, tm=1024, tn=512, tk=1024):
    (m, k), (_, n) = a.shape, b.shape
    tm, tn, tk = _row_tile(m, tm), _row_tile(n, tn), _row_tile(k, tk)
    return _gmm(name, a, b, grid=(m // tm, n // tn, k // tk),
                a_blk=(tm, tk), a_idx=lambda i, j, kk: (i, kk), b_blk=(tk, tn), b_idx=lambda i, j, kk: (kk, j),
                o_blk=(tm, tn), o_idx=lambda i, j, kk: (i, j), out_shape=(m, n), out_dtype=out_dtype, dn=NN,
                acc_shape=(tm, tn))


def mm_nt(name, a, b, out_dtype, tm=1024, tn=512, tk=1024):
    (m, k), (n, _) = a.shape, b.shape
    tm, tn, tk = _row_tile(m, tm), _row_tile(n, tn), _row_tile(k, tk)
    return _gmm(name, a, b, grid=(m // tm, n // tn, k // tk),
                a_blk=(tm, tk), a_idx=lambda i, j, kk: (i, kk), b_blk=(tn, tk), b_idx=lambda i, j, kk: (j, kk),
                o_blk=(tm, tn), o_idx=lambda i, j, kk: (i, j), out_shape=(m, n), out_dtype=out_dtype, dn=NT,
                acc_shape=(tm, tn))


def mm_tn(name, a, b, out_dtype, tm=512, tn=512, tk=1024):
    (k, m), (_, n) = a.shape, b.shape
    tm, tn, tk = _row_tile(m, tm), _row_tile(n, tn), _row_tile(k, tk)
    return _gmm(name, a, b, grid=(m // tm, n // tn, k // tk),
                a_blk=(tk, tm), a_idx=lambda i, j, kk: (kk, i), b_blk=(tk, tn), b_idx=lambda i, j, kk: (kk, j),
                o_blk=(tm, tn), o_idx=lambda i, j, kk: (i, j), out_shape=(m, n), out_dtype=out_dtype, dn=TN,
                acc_shape=(tm, tn))


def mm_nn_wblk(name, a, wb, layer, out_dtype, tm=1024, tk=1024):
    (m, k), (nb, _, _, n8) = a.shape, wb.shape
    tm, tk = _row_tile(m, tm), _row_tile(k, tk)
    return _gmm(name, a, wb, grid=(m // tm, nb, k // tk),
                a_blk=(tm, tk), a_idx=lambda i, j, kk: (i, kk),
                b_blk=(None, None, tk, n8), b_idx=lambda i, j, kk: (j, layer, kk, 0),
                o_blk=(tm, n8), o_idx=lambda i, j, kk: (i, j), out_shape=(m, nb * n8), out_dtype=out_dtype, dn=NN,
                acc_shape=(tm, n8))


def mm_nt_wblk(name, a, wb, layer, out_dtype, tm=1024, tn=512):
    m = a.shape[0]
    nb, _, k, n8 = wb.shape
    tm, tn = _row_tile(m, tm), _row_tile(k, tn)

    def body(a_ref, b_ref, o_ref):
        acc = lax.dot_general(a_ref[:, 0:n8], b_ref[0], NT, preferred_element_type=F32)
        for j in range(1, nb):
            acc = acc + lax.dot_general(a_ref[:, j * n8:(j + 1) * n8], b_ref[j], NT, preferred_element_type=F32)
        o_ref[...] = acc.astype(o_ref.dtype)

    return pl.pallas_call(
        body, grid=(m // tm, k // tn),
        in_specs=[pl.BlockSpec((tm, nb * n8), lambda i, j: (i, 0)),
                  pl.BlockSpec((nb, None, tn, n8), lambda i, j: (0, layer, j, 0))],
        out_specs=pl.BlockSpec((tm, tn), lambda i, j: (i, j)), out_shape=_sds((m, k), out_dtype),
        compiler_params=_params("parallel", "parallel"), name=name)(a, wb)


def mm_tn_oblk(name, a, b, n8, out_dtype, tk=2048):
    (t, k), nb = a.shape, b.shape[1] // n8
    tk = _row_tile(t, tk)
    return _gmm(name, a, b, grid=(nb, t // tk),
                a_blk=(tk, k), a_idx=lambda j, s: (s, 0), b_blk=(tk, n8), b_idx=lambda j, s: (s, j),
                o_blk=(None, k, n8), o_idx=lambda j, s: (j, 0, 0), out_shape=(nb, k, n8), out_dtype=out_dtype, dn=TN,
                acc_shape=(k, n8))


def rmsnorm_fwd(name, x, g):
    t, d = x.shape
    tr = _row_tile(t, 512)

    def body(x_ref, g_ref, o_ref):
        xv = x_ref[...]
        r = lax.rsqrt(jnp.mean(xv * xv, axis=-1, keepdims=True) + RMS_EPS)
        o_ref[...] = (xv * r * g_ref[...]).astype(o_ref.dtype)

    return pl.pallas_call(
        body, grid=(t // tr,),
        in_specs=[pl.BlockSpec((tr, d), lambda i: (i, 0)), pl.BlockSpec((1, d), lambda i: (0, 0))],
        out_specs=pl.BlockSpec((tr, d), lambda i: (i, 0)),
        out_shape=_sds((t, d), BF16), compiler_params=_params("parallel"), name=name)(x, g)


def resid_norm_fwd(name, x, m, g):
    t, d = x.shape
    tr = _row_tile(t, 512)

    def body(x_ref, m_ref, g_ref, o_ref):
        mv = m_ref[...]
        r = lax.rsqrt(jnp.mean(mv * mv, axis=-1, keepdims=True) + RMS_EPS)
        o_ref[...] = x_ref[...] + mv * r * g_ref[...]

    return pl.pallas_call(
        body, grid=(t // tr,),
        in_specs=[pl.BlockSpec((tr, d), lambda i: (i, 0)), pl.BlockSpec((tr, d), lambda i: (i, 0)),
                  pl.BlockSpec((1, d), lambda i: (0, 0))],
        out_specs=pl.BlockSpec((tr, d), lambda i: (i, 0)),
        out_shape=_sds((t, d), F32), compiler_params=_params("parallel"), name=name)(x, m, g)


def norm_bwd(name, dy, x, g, resid, out_dtype):
    t, d = x.shape
    tr = _row_tile(t, 512)
    has_res = resid is not None

    def body(*refs):
        if has_res:
            dy_ref, x_ref, g_ref, r_ref, dx_ref, dg_ref = refs
        else:
            dy_ref, x_ref, g_ref, dx_ref, dg_ref = refs
        i = pl.program_id(0)
        xv = x_ref[...]
        dyv = dy_ref[...].astype(F32)
        r = lax.rsqrt(jnp.mean(xv * xv, axis=-1, keepdims=True) + RMS_EPS)
        xh = xv * r
        dxh = dyv * g_ref[...]
        dx = r * (dxh - xh * jnp.mean(dxh * xh, axis=-1, keepdims=True))
        if has_res:
            dx = dx + r_ref[...]
        dx_ref[...] = dx.astype(dx_ref.dtype)
        part = jnp.sum(dyv * xh, axis=0, keepdims=True)

        @pl.when(i == 0)
        def _():
            dg_ref[...] = part

        @pl.when(i > 0)
        def _():
            dg_ref[...] += part

    row = pl.BlockSpec((tr, d), lambda i: (i, 0))
    vec = pl.BlockSpec((1, d), lambda i: (0, 0))
    ins = [dy, x, g] + ([resid] if has_res else [])
    return pl.pallas_call(
        body, grid=(t // tr,),
        in_specs=[row, row, vec] + ([row] if has_res else []),
        out_specs=[row, vec],
        out_shape=[_sds((t, d), out_dtype), _sds((1, d), F32)],
        compiler_params=_params("arbitrary"), name=name)(*ins)


def loss_grad(name, y, target):
    t, d = y.shape
    tr = _row_tile(t, 512)

    def body(y_ref, t_ref, dy_ref, s_ref):
        i = pl.program_id(0)
        err = y_ref[...] - t_ref[...]
        dy_ref[...] = err * (1.0 / d)
        part = jnp.sum(err * err, axis=0, keepdims=True)

        @pl.when(i == 0)
        def _():
            s_ref[...] = part

        @pl.when(i > 0)
        def _():
            s_ref[...] += part

    row = pl.BlockSpec((tr, d), lambda i: (i, 0))
    vec = pl.BlockSpec((1, d), lambda i: (0, 0))
    return pl.pallas_call(
        body, grid=(t // tr,), in_specs=[row, row], out_specs=[row, vec],
        out_shape=[_sds((t, d), F32), _sds((1, d), F32)],
        compiler_params=_params("arbitrary"), name=name)(y, target)


PAIR = 2 * HEAD_DIM
N_PAIRS = N_HEADS // 2
A_TQ = 512
A_UNROLL = 4


def _halves(x):
    lane = lax.broadcasted_iota(jnp.int32, x.shape, x.ndim - 1)
    zero = jnp.zeros_like(x)
    return jnp.where(lane < HEAD_DIM, x, zero), jnp.where(lane >= HEAD_DIM, x, zero)


def _merge(a, b):
    lane = lax.broadcasted_iota(jnp.int32, a.shape, a.ndim - 1)
    return jnp.where(lane < HEAD_DIM, a, b)


def _a_valid(c):
    col = lax.broadcasted_iota(jnp.int32, (CHUNK, BAND), 1)
    return col >= (N_LEFT - c) * CHUNK


def attn_a_fwd(name, proj, kp, vp, bias, q_blk):
    t = proj.shape[0]
    tq = _row_tile(t, A_TQ)
    ncs = tq // CHUNK
    un = math.gcd(A_UNROLL, ncs)

    def body(q_ref, k_ref, v_ref, b_ref, o_ref, l_ref):
        i = pl.program_id(1)

        def group(gg, carry):
            cs = [i * ncs + gg * un + u for u in range(un)]
            r0s = [pl.multiple_of((gg * un + u) * CHUNK, CHUNK) for u in range(un)]
            k0s = [pl.multiple_of(c * CHUNK, CHUNK) for c in cs]
            ss = []
            for u in range(un):
                qh = _halves(q_ref[pl.ds(r0s[u], CHUNK), :] * QK_SCALE)
                kwin = k_ref[pl.ds(k0s[u], BAND), :]
                valid = _a_valid(cs[u])
                for hh in range(2):
                    s = lax.dot_general(qh[hh], kwin, NT, preferred_element_type=F32) + b_ref[hh]
                    ss.append(jnp.where(valid, s, -1e30))
            ps, lses = [], []
            for s in ss:
                mx = jnp.max(s, axis=-1, keepdims=True)
                p = jnp.exp(s - mx)
                den = jnp.sum(p, axis=-1, keepdims=True)
                ps.append((p * (1.0 / den)).astype(BF16))
                lses.append(mx + jnp.log(den))
            for u in range(un):
                vwin = v_ref[pl.ds(k0s[u], BAND), :]
                o0 = jnp.dot(ps[2 * u], vwin, preferred_element_type=F32)
                o1 = jnp.dot(ps[2 * u + 1], vwin, preferred_element_type=F32)
                o_ref[pl.ds(r0s[u], CHUNK), :] = _merge(o0, o1)
                l_ref[pl.ds(r0s[u], CHUNK), :] = jnp.concatenate([lses[2 * u], lses[2 * u + 1]], axis=1)
            return carry

        lax.fori_loop(0, ncs // un, group, 0)

    return pl.pallas_call(
        body, grid=(N_PAIRS, t // tq),
        in_specs=[pl.BlockSpec((tq, PAIR), lambda p, i: (i, q_blk + p)),
                  pl.BlockSpec((t + PAD_KEYS, PAIR), lambda p, i: (0, p)),
                  pl.BlockSpec((t + PAD_KEYS, PAIR), lambda p, i: (0, p)),
                  pl.BlockSpec((2, CHUNK, BAND), lambda p, i: (p, 0, 0))],
        out_specs=[pl.BlockSpec((tq, PAIR), lambda p, i: (i, p)),
                   pl.BlockSpec((None, tq, 2), lambda p, i: (p, i, 0))],
        out_shape=[_sds((t, N_PAIRS * PAIR), F32), _sds((N_PAIRS, t, 2), F32)],
        compiler_params=_params("parallel", "parallel"), name=name)(proj, kp, vp, bias)


def attn_a_bwd(name, proj, kp, vp, bias, o, lse, do, q_blk, do_blk):
    t = proj.shape[0]
    tq = _row_tile(t, A_TQ)
    ncs = tq // CHUNK
    un = math.gcd(A_UNROLL, ncs)

    def body(q_ref, k_ref, v_ref, b_ref, o_ref, l_ref, do_ref, dq_ref, dk_ref, dv_ref, db_ref):
        i = pl.program_id(1)

        @pl.when(i == 0)
        def _():
            dk_ref[...] = jnp.zeros_like(dk_ref)
            dv_ref[...] = jnp.zeros_like(dv_ref)
            db_ref[...] = jnp.zeros_like(db_ref)

        def group(gg, carry):
            cs = [i * ncs + gg * un + u for u in range(un)]
            r0s = [pl.multiple_of((gg * un + u) * CHUNK, CHUNK) for u in range(un)]
            k0s = [pl.multiple_of(c * CHUNK, CHUNK) for c in cs]
            qhs, dohs, ps, dps, deltas = [], [], [], [], []
            for u in range(un):
                rows = pl.ds(r0s[u], CHUNK)
                qh = _halves(q_ref[rows, :] * QK_SCALE)
                doh = _halves(do_ref[rows, :])
                kwin = k_ref[pl.ds(k0s[u], BAND), :]
                vwin = v_ref[pl.ds(k0s[u], BAND), :]
                valid = _a_valid(cs[u])
                dl = _halves(do_ref[rows, :].astype(F32) * o_ref[rows, :])
                for hh in range(2):
                    s = lax.dot_general(qh[hh], kwin, NT, preferred_element_type=F32) + b_ref[hh]
                    ps.append(jnp.where(valid, jnp.exp(s - l_ref[rows, hh:hh + 1]), 0.0))
                    dps.append(lax.dot_general(doh[hh], vwin, NT, preferred_element_type=F32))
                    deltas.append(jnp.sum(dl[hh], axis=-1, keepdims=True))
                qhs.append(qh)
                dohs.append(doh)
            dss = [p * (dp - dl) for p, dp, dl in zip(ps, dps, deltas)]
            for hh in range(2):
                tot = dss[hh]
                for u in range(1, un):
                    tot = tot + dss[2 * u + hh]
                db_ref[hh] += tot
            for u in range(un):
                kwin = k_ref[pl.ds(k0s[u], BAND), :]
                ds0, ds1 = dss[2 * u].astype(BF16), dss[2 * u + 1].astype(BF16)
                dq_ref[pl.ds(r0s[u], CHUNK), :] = _merge(jnp.dot(ds0, kwin, preferred_element_type=F32),
                                                         jnp.dot(ds1, kwin, preferred_element_type=F32)) * QK_SCALE
                dk_ref[pl.ds(k0s[u], BAND), :] += (lax.dot_general(ds0, qhs[u][0], TN, preferred_element_type=F32)
                                                   + lax.dot_general(ds1, qhs[u][1], TN, preferred_element_type=F32))
                dv_ref[pl.ds(k0s[u], BAND), :] += (
                    lax.dot_general(ps[2 * u].astype(BF16), dohs[u][0], TN, preferred_element_type=F32)
                    + lax.dot_general(ps[2 * u + 1].astype(BF16), dohs[u][1], TN, preferred_element_type=F32))
            return carry

        lax.fori_loop(0, ncs // un, group, 0)

    tile = lambda blk: pl.BlockSpec((tq, PAIR), lambda p, i: (i, blk + p))
    whole = pl.BlockSpec((t + PAD_KEYS, PAIR), lambda p, i: (0, p))
    bspec = pl.BlockSpec((2, CHUNK, BAND), lambda p, i: (p, 0, 0))
    return pl.pallas_call(
        body, grid=(N_PAIRS, t // tq),
        in_specs=[tile(q_blk), whole, whole, bspec, tile(0), pl.BlockSpec((None, tq, 2), lambda p, i: (p, i, 0)),
                  tile(do_blk)],
        out_specs=[tile(0), whole, whole, bspec],
        out_shape=[_sds((t, N_PAIRS * PAIR), F32), _sds((t + PAD_KEYS, N_PAIRS * PAIR), F32),
                   _sds((t + PAD_KEYS, N_PAIRS * PAIR), F32), _sds((2 * N_PAIRS, CHUNK, BAND), F32)],
        compiler_params=_params("parallel", "arbitrary"), name=name)(proj, kp, vp, bias, o, lse, do)


SB_TQ = 256
SB_TK = 256
SB_DEAD = -125.0


def _tri(n, strict):
    j = lax.broadcasted_iota(jnp.int32, (n, n), 0)
    s = lax.broadcasted_iota(jnp.int32, (n, n), 1)
    return jnp.where((j > s) if strict else (j >= s), 1.0, 0.0).astype(BF16)


def _suffix_sum(x, tri, exact):
    hi = x.astype(BF16)
    out = jnp.dot(hi, tri, preferred_element_type=F32)
    if exact:
        lo = (x - hi.astype(F32)).astype(BF16)
        out = out + jnp.dot(lo, tri, preferred_element_type=F32)
    return out


def _sb_scores(qh, ks, causal):
    z = lax.dot_general(qh, ks, NT, preferred_element_type=F32)
    lb = jnp.minimum(z, 0.0) - jnp.log(1.0 + jnp.exp(-jnp.abs(z)))
    m = lb - z
    if causal is not None:
        m = jnp.where(causal, m, 0.0)
    return lb, m


def _causal(tq, tk, off):
    return (lax.broadcasted_iota(jnp.int32, (tq, tk), 1) + off * tk) < lax.broadcasted_iota(jnp.int32, (tq, tk), 0)


def sb_fwd(name, proj, q_blk, k_blk, v_blk):
    t = proj.shape[0]
    tq = _row_tile(t, SB_TQ)
    tk = min(SB_TK, tq)
    per = tq // tk

    def body(q_ref, k_ref, v_ref, o_ref):
        i = pl.program_id(1)
        tri = _tri(tk, True)
        qh = _halves(q_ref[...] * QK_SCALE)

        def blocks(kb, carry, off):
            k0 = pl.multiple_of(kb * tk, tk)
            ks, vs = k_ref[pl.ds(k0, tk), :], v_ref[pl.ds(k0, tk), :]
            causal = None if off is None else _causal(tq, tk, off)
            lbm = [_sb_scores(qh[hh], ks, causal) for hh in range(2)]
            afters = [_suffix_sum(lbm[hh][1], tri, False) for hh in range(2)]
            out = []
            for hh in range(2):
                acc, cm = carry[2 * hh], carry[2 * hh + 1]
                w = jnp.exp(lbm[hh][0] + afters[hh] + cm)
                if causal is not None:
                    w = jnp.where(causal, w, 0.0)
                out += [acc + jnp.dot(w.astype(BF16), vs, preferred_element_type=F32),
                        cm + jnp.sum(lbm[hh][1], axis=-1, keepdims=True)]
            return tuple(out)

        def alive(carry):
            return jnp.maximum(jnp.max(carry[1]), jnp.max(carry[3])) > SB_DEAD

        carry = (jnp.zeros((tq, PAIR), F32), jnp.zeros((tq, 1), F32)) * 2
        for off in reversed(range(per)):
            carry = blocks(i * per + off, carry, off)

        def step(c):
            new = blocks(i * per - 1 - c[0], c[2:], None)
            return (c[0] + 1, alive(new)) + new

        out = lax.while_loop(lambda c: jnp.logical_and(c[0] < i * per, c[1]), step,
                             (jnp.int32(0), alive(carry)) + carry)
        o_ref[...] = _merge(out[2], out[4])

    return pl.pallas_call(
        body, grid=(N_PAIRS, t // tq),
        in_specs=[pl.BlockSpec((tq, PAIR), lambda p, i: (i, q_blk + p)),
                  pl.BlockSpec((t, PAIR), lambda p, i: (0, k_blk + p)),
                  pl.BlockSpec((t, PAIR), lambda p, i: (0, v_blk + p))],
        out_specs=pl.BlockSpec((tq, PAIR), lambda p, i: (i, p)),
        out_shape=_sds((t, N_PAIRS * PAIR), F32), compiler_params=_params("parallel", "parallel"), name=name,
    )(proj, proj, proj)


def sb_bwd(name, proj, o, do, q_blk, k_blk, v_blk, do_blk):
    t = proj.shape[0]
    tq = _row_tile(t, SB_TQ)
    tk = min(SB_TK, tq)
    per = tq // tk

    def body(q_ref, k_ref, v_ref, o_ref, do_ref, dq_ref, dk_ref, dv_ref):
        i = pl.program_id(1)

        @pl.when(i == 0)
        def _():
            dk_ref[...] = jnp.zeros_like(dk_ref)
            dv_ref[...] = jnp.zeros_like(dv_ref)

        tri_s, tri_i = _tri(tk, True), _tri(tk, False)
        qh = _halves(q_ref[...] * QK_SCALE)
        doh = _halves(do_ref[...])
        deltas = [jnp.sum(x, axis=-1, keepdims=True) for x in _halves(do_ref[...].astype(F32) * o_ref[...])]

        def blocks(kb, carry, off):
            k0 = pl.multiple_of(kb * tk, tk)
            ks, vs = k_ref[pl.ds(k0, tk), :], v_ref[pl.ds(k0, tk), :]
            causal = None if off is None else _causal(tq, tk, off)
            lbm = [_sb_scores(qh[hh], ks, causal) for hh in range(2)]
            dws = [lax.dot_general(doh[hh], vs, NT, preferred_element_type=F32) for hh in range(2)]
            afters = [_suffix_sum(lbm[hh][1], tri_s, False) for hh in range(2)]
            wbs, es = [], []
            for hh in range(2):
                w = jnp.exp(lbm[hh][0] + afters[hh] + carry[3 * hh + 1])
                if causal is not None:
                    w = jnp.where(causal, w, 0.0)
                wbs.append(w.astype(BF16))
                es.append(wbs[hh].astype(F32) * dws[hh])
            sfx = [_suffix_sum(es[hh], tri_i, True) for hh in range(2)]
            dzs = []
            for hh in range(2):
                left = deltas[hh] - (sfx[hh] + carry[3 * hh + 2])
                sig = jnp.exp(lbm[hh][0])
                dz = es[hh] * (1.0 - sig) - left * sig
                if causal is not None:
                    dz = jnp.where(causal, dz, 0.0)
                dzs.append(dz.astype(BF16))
            dk_ref[pl.ds(k0, tk), :] += (lax.dot_general(dzs[0], qh[0], TN, preferred_element_type=F32)
                                         + lax.dot_general(dzs[1], qh[1], TN, preferred_element_type=F32))
            dv_ref[pl.ds(k0, tk), :] += (lax.dot_general(wbs[0], doh[0], TN, preferred_element_type=F32)
                                         + lax.dot_general(wbs[1], doh[1], TN, preferred_element_type=F32))
            out = []
            for hh in range(2):
                out += [carry[3 * hh] + jnp.dot(dzs[hh], ks, preferred_element_type=F32),
                        carry[3 * hh + 1] + jnp.sum(lbm[hh][1], axis=-1, keepdims=True),
                        carry[3 * hh + 2] + jnp.sum(es[hh], axis=-1, keepdims=True)]
            return tuple(out)

        def alive(carry):
            return jnp.maximum(jnp.max(carry[1]), jnp.max(carry[4])) > SB_DEAD

        zero = jnp.zeros((tq, 1), F32)
        carry = (jnp.zeros((tq, PAIR), F32), zero, zero) * 2
        for off in reversed(range(per)):
            carry = blocks(i * per + off, carry, off)

        def step(c):
            new = blocks(i * per - 1 - c[0], c[2:], None)
            return (c[0] + 1, alive(new)) + new

        out = lax.while_loop(lambda c: jnp.logical_and(c[0] < i * per, c[1]), step,
                             (jnp.int32(0), alive(carry)) + carry)
        dq_ref[...] = _merge(out[2], out[5]) * QK_SCALE

    tile = lambda blk: pl.BlockSpec((tq, PAIR), lambda p, i: (i, blk + p))
    whole = lambda blk: pl.BlockSpec((t, PAIR), lambda p, i: (0, blk + p))
    return pl.pallas_call(
        body, grid=(N_PAIRS, t // tq),
        in_specs=[tile(q_blk), whole(k_blk), whole(v_blk), tile(0), tile(do_blk)],
        out_specs=[tile(0), whole(0), whole(0)],
        out_shape=[_sds((t, N_PAIRS * PAIR), F32)] * 3,
        compiler_params=_params("parallel", "arbitrary"), name=name)(proj, proj, proj, o, do)


def _sigmoid(x):
    return 1.0 / (1.0 + jnp.exp(-x))


def ffn_up(name, h, wg, wu, layer):
    t, d = h.shape
    nb, _, _, f8 = wg.shape
    tm = _row_tile(t, 1024)

    def body(h_ref, wg_ref, wu_ref, g_ref, u_ref, a_ref):
        hv = h_ref[...]
        g = jnp.dot(hv, wg_ref[...], preferred_element_type=F32)
        u = jnp.dot(hv, wu_ref[...], preferred_element_type=F32)
        g_ref[...] = g.astype(BF16)
        u_ref[...] = u.astype(BF16)
        a_ref[...] = (g * _sigmoid(g) * u).astype(BF16)

    wspec = pl.BlockSpec((None, None, d, f8), lambda i, k: (k, layer, 0, 0))
    ospec = pl.BlockSpec((None, tm, f8), lambda i, k: (k, i, 0))
    return pl.pallas_call(
        body, grid=(t // tm, nb), in_specs=[pl.BlockSpec((tm, d), lambda i, k: (i, 0)), wspec, wspec],
        out_specs=[ospec] * 3, out_shape=[_sds((nb, t, f8), BF16)] * 3,
        compiler_params=_params("parallel", "parallel"), name=name)(h, wg, wu)


def ffn_down(name, a, wd, layer):
    nb, t, f8 = a.shape
    d = wd.shape[3]
    tm = _row_tile(t, 1024)
    return _gmm(name, a, wd, grid=(t // tm, nb),
                a_blk=(None, tm, f8), a_idx=lambda i, k: (k, i, 0),
                b_blk=(None, None, f8, d), b_idx=lambda i, k: (k, layer, 0, 0),
                o_blk=(tm, d), o_idx=lambda i, k: (i, 0), out_shape=(t, d), out_dtype=F32, dn=NN, acc_shape=(tm, d))


def ffn_bwd_act(name, dm, wd, g, u, layer):
    t, d = dm.shape
    nb, _, f8, _ = wd.shape
    tm = _row_tile(t, 1024)

    def body(dm_ref, wd_ref, g_ref, u_ref, dg_ref, du_ref):
        da = lax.dot_general(dm_ref[...], wd_ref[...], NT, preferred_element_type=F32)
        gv = g_ref[...].astype(F32)
        uv = u_ref[...].astype(F32)
        sg = _sigmoid(gv)
        dg_ref[...] = (da * uv * sg * (1.0 + gv * (1.0 - sg))).astype(BF16)
        du_ref[...] = (da * gv * sg).astype(BF16)

    bspec = pl.BlockSpec((None, tm, f8), lambda i, k: (k, i, 0))
    return pl.pallas_call(
        body, grid=(t // tm, nb),
        in_specs=[pl.BlockSpec((tm, d), lambda i, k: (i, 0)),
                  pl.BlockSpec((None, None, f8, d), lambda i, k: (k, layer, 0, 0)), bspec, bspec],
        out_specs=[bspec] * 2, out_shape=[_sds((nb, t, f8), BF16)] * 2,
        compiler_params=_params("parallel", "parallel"), name=name)(dm, wd, g, u)


def ffn_bwd_dh(name, dg, du, wg, wu, layer):
    nb, t, f8 = dg.shape
    d = wg.shape[2]
    tm = _row_tile(t, 1024)

    def body(dg_ref, du_ref, wg_ref, wu_ref, o_ref):
        k = pl.program_id(1)
        part = (lax.dot_general(dg_ref[...], wg_ref[...], NT, preferred_element_type=F32)
                + lax.dot_general(du_ref[...], wu_ref[...], NT, preferred_element_type=F32))

        @pl.when(k == 0)
        def _():
            o_ref[...] = part

        @pl.when(k > 0)
        def _():
            o_ref[...] += part

    bspec = pl.BlockSpec((None, tm, f8), lambda i, k: (k, i, 0))
    wspec = pl.BlockSpec((None, None, d, f8), lambda i, k: (k, layer, 0, 0))
    return pl.pallas_call(
        body, grid=(t // tm, nb), in_specs=[bspec, bspec, wspec, wspec],
        out_specs=pl.BlockSpec((tm, d), lambda i, k: (i, 0)), out_shape=_sds((t, d), F32),
        compiler_params=_params("parallel", "arbitrary"), name=name)(dg, du, wg, wu)


def ffn_dw_in(name, h, dact):
    t, d = h.shape
    nb, _, f8 = dact.shape
    tk = _row_tile(t, 2048)
    return _gmm(name, h, dact, grid=(nb, t // tk),
                a_blk=(tk, d), a_idx=lambda b, s: (s, 0), b_blk=(None, tk, f8), b_idx=lambda b, s: (b, s, 0),
                o_blk=(None, d, f8), o_idx=lambda b, s: (b, 0, 0), out_shape=(nb, d, f8), out_dtype=F32, dn=TN,
                acc_shape=(d, f8))


def ffn_dw_down(name, a, dm):
    nb, t, f8 = a.shape
    d = dm.shape[1]
    tk = _row_tile(t, 2048)
    return _gmm(name, a, dm, grid=(nb, t // tk),
                a_blk=(None, tk, f8), a_idx=lambda b, s: (b, s, 0), b_blk=(tk, d), b_idx=lambda b, s: (s, 0),
                o_blk=(None, f8, d), o_idx=lambda b, s: (b, 0, 0), out_shape=(nb, f8, d), out_dtype=F32, dn=TN,
                acc_shape=(f8, d))


GELU_C = math.sqrt(2.0 / math.pi)
GELU_A = 0.044715


def _gelu(x):
    return 0.5 * x * (1.0 + jnp.tanh(GELU_C * (x + GELU_A * x * x * x)))


def _gelu_grad(x):
    th = jnp.tanh(GELU_C * (x + GELU_A * x * x * x))
    return 0.5 * (1.0 + th) + 0.5 * x * (1.0 - th * th) * GELU_C * (1.0 + 3.0 * GELU_A * x * x)


def _neg_expm1(x):
    series = x * (1.0 + x * (0.5 + x * (1.0 / 6.0 + x * (1.0 / 24.0 + x * (1.0 / 120.0 + x * (1.0 / 720.0))))))
    return -jnp.where(x > -0.25, series, jnp.exp(x) - 1.0)


CONV_TR = 256
CONV_TAPS = 4
HALO = 8


def _shifted(ext, k, tr, back):
    if back:
        return pltpu.roll(ext, k, 0)[HALO:, :] if k else ext[HALO:, :]
    return pltpu.roll(ext, tr + HALO - k, 0)[:tr, :] if k else ext[:tr, :]


def conv4_fwd(name, src, cb, w, b):
    t, c = src.shape[0], w.shape[1]
    tr = _row_tile(t, CONV_TR)
    hb = tr // HALO

    def body(x_ref, h_ref, w_ref, b_ref, o_ref):
        i = pl.program_id(0)
        ext = jnp.concatenate([jnp.where(i == 0, 0.0, h_ref[...]), x_ref[...]], axis=0)
        acc = b_ref[...]
        for k in range(CONV_TAPS):
            acc = acc + w_ref[CONV_TAPS - 1 - k:CONV_TAPS - k, :] * _shifted(ext, k, tr, True)
        o_ref[...] = acc

    return pl.pallas_call(
        body, grid=(t // tr,),
        in_specs=[pl.BlockSpec((tr, c), lambda i: (i, cb)),
                  pl.BlockSpec((HALO, c), lambda i: (jnp.maximum(i * hb - 1, 0), cb)),
                  pl.BlockSpec((CONV_TAPS, c), lambda i: (0, 0)), pl.BlockSpec((1, c), lambda i: (0, 0))],
        out_specs=pl.BlockSpec((tr, c), lambda i: (i, 0)), out_shape=_sds((t, c), F32),
        compiler_params=_params("parallel"), name=name)(src, src, w, b)


def conv4_bwd_x(name, dy, w):
    t, c = dy.shape
    tr = _row_tile(t, CONV_TR)
    hb = tr // HALO
    last = t // tr - 1

    def body(y_ref, h_ref, w_ref, o_ref):
        i = pl.program_id(0)
        ext = jnp.concatenate([y_ref[...], jnp.where(i == last, 0.0, h_ref[...])], axis=0)
        acc = w_ref[CONV_TAPS - 1:CONV_TAPS, :] * y_ref[...]
        for k in range(1, CONV_TAPS):
            acc = acc + w_ref[CONV_TAPS - 1 - k:CONV_TAPS - k, :] * _shifted(ext, k, tr, False)
        o_ref[...] = acc

    return pl.pallas_call(
        body, grid=(t // tr,),
        in_specs=[pl.BlockSpec((tr, c), lambda i: (i, 0)),
                  pl.BlockSpec((HALO, c), lambda i: (jnp.minimum((i + 1) * hb, t // HALO - 1), 0)),
                  pl.BlockSpec((CONV_TAPS, c), lambda i: (0, 0))],
        out_specs=pl.BlockSpec((tr, c), lambda i: (i, 0)), out_shape=_sds((t, c), F32),
        compiler_params=_params("parallel"), name=name)(dy, dy, w)


def conv4_bwd_w(name, src, cb, dy):
    t, c = dy.shape
    tr = _row_tile(t, CONV_TR)
    hb = tr // HALO

    def body(x_ref, h_ref, dy_ref, dw_ref, db_ref):
        i = pl.program_id(0)

        @pl.when(i == 0)
        def _():
            dw_ref[...] = jnp.zeros_like(dw_ref)
            db_ref[...] = jnp.zeros_like(db_ref)

        ext = jnp.concatenate([jnp.where(i == 0, 0.0, h_ref[...]), x_ref[...]], axis=0)
        dyv = dy_ref[...]
        db_ref[...] += jnp.sum(dyv, axis=0, keepdims=True)
        for k in range(CONV_TAPS):
            dw_ref[CONV_TAPS - 1 - k:CONV_TAPS - k, :] += jnp.sum(dyv * _shifted(ext, k, tr, True), axis=0,
                                                                   keepdims=True)

    return pl.pallas_call(
        body, grid=(t // tr,),
        in_specs=[pl.BlockSpec((tr, c), lambda i: (i, cb)),
                  pl.BlockSpec((HALO, c), lambda i: (jnp.maximum(i * hb - 1, 0), cb)),
                  pl.BlockSpec((tr, c), lambda i: (i, 0))],
        out_specs=[pl.BlockSpec((CONV_TAPS, c), lambda i: (0, 0)), pl.BlockSpec((1, c), lambda i: (0, 0))],
        out_shape=[_sds((CONV_TAPS, c), F32), _sds((1, c), F32)],
        compiler_params=_params("arbitrary"), name=name)(src, src, dy)


def _rg_gate_values(xcv, wa_ref, wi_ref, ba_ref, bi_ref, lam_ref):
    xb = xcv.astype(BF16)
    r = _sigmoid(jnp.dot(xb, wa_ref[...], preferred_element_type=F32) + ba_ref[...])
    ig = _sigmoid(jnp.dot(xb, wi_ref[...], preferred_element_type=F32) + bi_ref[...])
    lam = lam_ref[...]
    sp = jnp.maximum(-lam, 0.0) + jnp.log(1.0 + jnp.exp(-jnp.abs(lam)))
    log_a = -LRU_C * r * sp
    a = jnp.exp(log_a)
    mult = jnp.sqrt(_neg_expm1(2.0 * log_a))
    return xb, r, ig, sp, a, mult


def rg_gates_fwd(name, xc, wa, wi, ba, bi, lam):
    t, c = xc.shape
    nb, cb, _ = wa.shape
    tm = _row_tile(t, 512)

    def body(xc_ref, wa_ref, wi_ref, ba_ref, bi_ref, lam_ref, a_ref, u_ref):
        xcv = xc_ref[...]
        _, _, ig, _, a, mult = _rg_gate_values(xcv, wa_ref, wi_ref, ba_ref, bi_ref, lam_ref)
        a_ref[...] = a
        u_ref[...] = mult * (ig * xcv)

    blk = pl.BlockSpec((tm, cb), lambda i, n: (i, n))
    wsp = pl.BlockSpec((None, cb, cb), lambda i, n: (n, 0, 0))
    vec = pl.BlockSpec((1, cb), lambda i, n: (0, n))
    return pl.pallas_call(
        body, grid=(t // tm, nb), in_specs=[blk, wsp, wsp, vec, vec, vec], out_specs=[blk, blk],
        out_shape=[_sds((t, c), F32)] * 2, compiler_params=_params("parallel", "parallel"), name=name,
    )(xc, wa, wi, ba, bi, lam)


def rg_gates_bwd(name, xc, gu, hprev, wa, wi, ba, bi, lam):
    t, c = xc.shape
    nb, cb, _ = wa.shape
    tm = _row_tile(t, 512)

    def body(xc_ref, gu_ref, hp_ref, wa_ref, wi_ref, ba_ref, bi_ref, lam_ref,
             dxc_ref, dwa_ref, dwi_ref, dba_ref, dbi_ref, dlam_ref):
        i = pl.program_id(1)

        @pl.when(i == 0)
        def _():
            for ref in (dwa_ref, dwi_ref, dba_ref, dbi_ref, dlam_ref):
                ref[...] = jnp.zeros_like(ref)

        xcv = xc_ref[...]
        xb, r, ig, sp, a, mult = _rg_gate_values(xcv, wa_ref, wi_ref, ba_ref, bi_ref, lam_ref)
        gv = gu_ref[...]
        d_ixc = gv * mult
        d_i = d_ixc * xcv
        d_mult = gv * ig * xcv
        d_a = gv * hp_ref[...] - d_mult * a / mult
        d_log_a = d_a * a
        d_r = d_log_a * (-LRU_C * sp)
        sig_neg_lam = 1.0 / (1.0 + jnp.exp(lam_ref[...]))
        dlam_ref[...] += jnp.sum(d_log_a * r, axis=0, keepdims=True) * (LRU_C * sig_neg_lam)
        dpa = d_r * r * (1.0 - r)
        dpi = d_i * ig * (1.0 - ig)
        dba_ref[...] += jnp.sum(dpa, axis=0, keepdims=True)
        dbi_ref[...] += jnp.sum(dpi, axis=0, keepdims=True)
        dpab, dpib = dpa.astype(BF16), dpi.astype(BF16)
        dxc_ref[...] = (d_ixc * ig + lax.dot_general(dpab, wa_ref[...], NT, preferred_element_type=F32)
                        + lax.dot_general(dpib, wi_ref[...], NT, preferred_element_type=F32))
        dwa_ref[...] += lax.dot_general(xb, dpab, TN, preferred_element_type=F32)
        dwi_ref[...] += lax.dot_general(xb, dpib, TN, preferred_element_type=F32)

    blk = pl.BlockSpec((tm, cb), lambda n, i: (i, n))
    wsp = pl.BlockSpec((None, cb, cb), lambda n, i: (n, 0, 0))
    vec = pl.BlockSpec((1, cb), lambda n, i: (0, n))
    return pl.pallas_call(
        body, grid=(nb, t // tm), in_specs=[blk, blk, blk, wsp, wsp, vec, vec, vec],
        out_specs=[blk, wsp, wsp, vec, vec, vec],
        out_shape=[_sds((t, c), F32), _sds((nb, cb, cb), F32), _sds((nb, cb, cb), F32),
                   _sds((1, c), F32), _sds((1, c), F32), _sds((1, c), F32)],
        compiler_params=_params("parallel", "arbitrary"), name=name)(xc, gu, hprev, wa, wi, ba, bi, lam)


SCAN_TS = 256
SCAN_TC = 512


def _tile_scan(a, b, reverse):
    ts = a.shape[0]
    row = lax.broadcasted_iota(jnp.int32, a.shape, 0)
    d = 1
    while d < ts:
        if reverse:
            inside = row < ts - d
            a_sh = jnp.where(inside, pltpu.roll(a, ts - d, 0), 1.0)
            b_sh = jnp.where(inside, pltpu.roll(b, ts - d, 0), 0.0)
        else:
            inside = row >= d
            a_sh = jnp.where(inside, pltpu.roll(a, d, 0), 1.0)
            b_sh = jnp.where(inside, pltpu.roll(b, d, 0), 0.0)
        b = b + a * b_sh
        a = a * a_sh
        d *= 2
    return a, b


def rg_scan_fwd(name, a, u, gate_pre):
    t, c = a.shape
    ts, tc = _row_tile(t, SCAN_TS), _row_tile(c, SCAN_TC)

    def body(a_ref, u_ref, g_ref, h_ref, z_ref, carry_ref):
        s = pl.program_id(1)

        @pl.when(s == 0)
        def _():
            carry_ref[...] = jnp.zeros_like(carry_ref)

        ac, bc = _tile_scan(a_ref[...], u_ref[...], False)
        h = bc + ac * carry_ref[0:1, :]
        h_ref[...] = h
        z_ref[...] = (h * _gelu(g_ref[...])).astype(BF16)
        carry_ref[0:1, :] = h[ts - 1:ts, :]

    blk = pl.BlockSpec((ts, tc), lambda j, s: (s, j))
    return pl.pallas_call(
        body, grid=(c // tc, t // ts), in_specs=[blk, blk, blk], out_specs=[blk, blk],
        out_shape=[_sds((t, c), F32), _sds((t, c), BF16)], scratch_shapes=[pltpu.VMEM((8, tc), F32)],
        compiler_params=_params("parallel", "arbitrary"), name=name)(a, u, gate_pre)


def rg_scan_bwd(name, a_next, hs, gate_pre, dz):
    t, c = hs.shape
    ts, tc = _row_tile(t, SCAN_TS), _row_tile(c, SCAN_TC)
    nt = t // ts

    def body(an_ref, h_ref, g_ref, dz_ref, gu_ref, dgate_ref, carry_ref):
        s = pl.program_id(1)

        @pl.when(s == 0)
        def _():
            carry_ref[...] = jnp.zeros_like(carry_ref)

        gate = g_ref[...]
        dzv = dz_ref[...]
        dgate_ref[...] = (dzv * h_ref[...] * _gelu_grad(gate)).astype(BF16)
        ac, bc = _tile_scan(an_ref[...], dzv * _gelu(gate), True)
        gu = bc + ac * carry_ref[0:1, :]
        gu_ref[...] = gu
        carry_ref[0:1, :] = gu[0:1, :]

    blk = pl.BlockSpec((ts, tc), lambda j, s: (nt - 1 - s, j))
    return pl.pallas_call(
        body, grid=(c // tc, nt), in_specs=[blk, blk, blk, blk], out_specs=[blk, blk],
        out_shape=[_sds((t, c), F32), _sds((t, c), BF16)], scratch_shapes=[pltpu.VMEM((8, tc), F32)],
        compiler_params=_params("parallel", "arbitrary"), name=name)(a_next, hs, gate_pre, dz)


def _shift_down(x, k):
    return jnp.pad(x, ((k, 0), (0, 0)))[:x.shape[0]] if k else x


def _shift_up(x, k):
    return jnp.pad(x, ((0, k), (0, 0)))[k:] if k else x


QA_BLK, KA_BLK, VA_BLK, QS_BLK, KS_BLK, VS_BLK = (g * N_PAIRS for g in range(6))


TOEP_W = 640
TOEP_FLAT = 320
TABLE_LOW = 193


def rel_bias_matrix(name, table):
    h = table.shape[0]
    diag = jnp.concatenate([jnp.repeat(table[:, 2 * REL_CLIP:], TOEP_FLAT, axis=1),
                            jnp.flip(table[:, TABLE_LOW:2 * REL_CLIP], axis=1),
                            jnp.zeros((h, 1), table.dtype)], axis=1)[:, None, :]

    def body(v_ref, o_ref):
        rows = jnp.broadcast_to(v_ref[...], (CHUNK, TOEP_W))
        o_ref[...] = pltpu.roll(rows, TOEP_W - (CHUNK - 1), 1, stride=1, stride_axis=0)

    out = pl.pallas_call(
        body, grid=(h,), in_specs=[pl.BlockSpec((None, 1, TOEP_W), lambda hh: (hh, 0, 0))],
        out_specs=pl.BlockSpec((None, CHUNK, TOEP_W), lambda hh: (hh, 0, 0)),
        out_shape=_sds((h, CHUNK, TOEP_W), F32), compiler_params=_params("parallel"), name=name)(diag)
    return out[:, :, :BAND]


def rel_bias_grad(name, dbias):
    h = dbias.shape[0]
    flipped = jnp.pad(jnp.flip(dbias, axis=1), ((0, 0), (0, 0), (0, TOEP_W - BAND)))

    def body(x_ref, o_ref):
        skew = pltpu.roll(x_ref[...], 0, 1, stride=1, stride_axis=0)
        col = jnp.sum(skew, axis=0, keepdims=True)
        lane = lax.broadcasted_iota(jnp.int32, col.shape, 1)
        flat = jnp.sum(jnp.where(lane < TOEP_FLAT, col, 0.0), axis=1, keepdims=True)
        o_ref[...] = jnp.where(lane == TOEP_W - 1, flat, col)

    out = pl.pallas_call(
        body, grid=(h,), in_specs=[pl.BlockSpec((None, CHUNK, TOEP_W), lambda hh: (hh, 0, 0))],
        out_specs=pl.BlockSpec((None, 1, TOEP_W), lambda hh: (hh, 0, 0)),
        out_shape=_sds((h, 1, TOEP_W), F32), compiler_params=_params("parallel"), name=name)(flipped)[:, 0, :]
    return jnp.concatenate([jnp.zeros((h, TABLE_LOW), F32), jnp.flip(out[:, TOEP_FLAT:TOEP_W - 1], axis=1),
                            out[:, TOEP_W - 1:]], axis=1)


def attn_layer_fwd(tag, x, w):
    h = rmsnorm_fwd(tag + "_norm", x, w["g_pre"])
    proj = mm_nn_wblk(tag + "_proj", h, w["w_in"], w["idx"], BF16)
    width = N_PAIRS * PAIR
    pad = lambda a: jnp.pad(a, ((PAD_KEYS, 0), (0, 0)))
    kap, vap = pad(proj[:, width:2 * width]), pad(proj[:, 2 * width:3 * width])
    bias = rel_bias_matrix(tag + "_bias", w["rel_bias"])
    oa, lse = attn_a_fwd(tag + "_a", proj, kap, vap, bias, QA_BLK)
    ob = sb_fwd(tag + "_sb", proj, QS_BLK, KS_BLK, VS_BLK)
    o = jnp.concatenate([oa, ob], axis=1).astype(BF16)
    m = mm_nn(tag + "_out", o, w["w_out"], F32)
    x1 = resid_norm_fwd(tag + "_res", x, m, w["g_post"])
    return x1, (x, h, proj, kap, vap, bias, oa, lse, ob, o, m)


def attn_layer_bwd(tag, dx1, saved, w):
    x, h, proj, kap, vap, bias, oa, lse, ob, o, m = saved
    dm, dg_post = norm_bwd(tag + "_dpost", dx1, m, w["g_post"], None, BF16)
    d_w_out = mm_tn(tag + "_dwout", o, dm, F32)
    do = mm_nt(tag + "_do", dm, w["w_out"], BF16)
    dqa, dkap, dvap, dbias = attn_a_bwd(tag + "_da", proj, kap, vap, bias, oa, lse, do, QA_BLK, 0)
    dqs, dks, dvs = sb_bwd(tag + "_dsb", proj, ob, do, QS_BLK, KS_BLK, VS_BLK, N_PAIRS)
    d_rel = rel_bias_grad(tag + "_dbias", dbias)
    dproj = jnp.concatenate([dqa, dkap[PAD_KEYS:], dvap[PAD_KEYS:], dqs, dks, dvs], axis=1).astype(BF16)
    d_w_in = mm_tn_oblk(tag + "_dwin", h, dproj, w["w_in"].shape[3], F32)
    dh = mm_nt_wblk(tag + "_dh", dproj, w["w_in"], w["idx"], F32)
    dx, dg_pre = norm_bwd(tag + "_dpre", dh, x, w["g_pre"], dx1, F32)
    return dx, dict(w_in=d_w_in, w_out=d_w_out, rel_bias=d_rel, g_pre=dg_pre, g_post=dg_post)


def rg_layer_fwd(tag, x, w):
    h = rmsnorm_fwd(tag + "_norm", x, w["g_pre"])
    proj = mm_nn_wblk(tag + "_proj", h, w["w_in"], w["idx"], F32)
    xc = conv4_fwd(tag + "_conv", proj, 1, w["conv_w"], w["conv_b"])
    a, u = rg_gates_fwd(tag + "_gates", xc, w["w_a"], w["w_i"], w["b_a"], w["b_i"], w["lam"])
    hs, z = rg_scan_fwd(tag + "_scan", a, u, proj)
    m = mm_nn(tag + "_out", z, w["w_out"], F32)
    x1 = resid_norm_fwd(tag + "_res", x, m, w["g_post"])
    return x1, (x, h, proj, xc, a, hs, z, m)


def rg_layer_bwd(tag, dx1, saved, w):
    x, h, proj, xc, a, hs, z, m = saved
    dm, dg_post = norm_bwd(tag + "_dpost", dx1, m, w["g_post"], None, BF16)
    d_w_out = mm_tn(tag + "_dwout", z, dm, F32)
    dz = mm_nt(tag + "_dz", dm, w["w_out"], F32)
    gu, dgate = rg_scan_bwd(tag + "_dscan", _shift_up(a, 1), hs, proj, dz)
    dxc, d_w_a, d_w_i, d_b_a, d_b_i, d_lam = rg_gates_bwd(
        tag + "_dgates", xc, gu, _shift_down(hs, 1), w["w_a"], w["w_i"], w["b_a"], w["b_i"], w["lam"])
    d_conv_w, d_conv_b = conv4_bwd_w(tag + "_dconvw", proj, 1, dxc)
    dxr = conv4_bwd_x(tag + "_dconv", dxc, w["conv_w"])
    dproj = jnp.concatenate([dgate, dxr.astype(BF16)], axis=1)
    d_w_in = mm_tn_oblk(tag + "_dwin", h, dproj, w["w_in"].shape[3], F32)
    dh = mm_nt_wblk(tag + "_dh", dproj, w["w_in"], w["idx"], F32)
    dx, dg_pre = norm_bwd(tag + "_dpre", dh, x, w["g_pre"], dx1, F32)
    return dx, dict(w_in=d_w_in, w_out=d_w_out, conv_w=d_conv_w, conv_b=d_conv_b, w_a=d_w_a, w_i=d_w_i,
                    b_a=d_b_a, b_i=d_b_i, lam=d_lam, g_pre=dg_pre, g_post=dg_post)


def ffn_layer_fwd(tag, x, w):
    h = rmsnorm_fwd(tag + "_norm", x, w["g_pre"])
    g, u, a = ffn_up(tag + "_up", h, w["w_gate"], w["w_up"], w["idx"])
    f = ffn_down(tag + "_down", a, w["w_down"], w["idx"])
    x1 = resid_norm_fwd(tag + "_res", x, f, w["g_post"])
    return x1, (x, h, g, u, a, f)


def ffn_layer_bwd(tag, dx1, saved, w):
    x, h, g, u, a, f = saved
    dm, dg_post = norm_bwd(tag + "_dpost", dx1, f, w["g_post"], None, BF16)
    d_w_down = ffn_dw_down(tag + "_dwdown", a, dm)
    dg, du = ffn_bwd_act(tag + "_dact", dm, w["w_down"], g, u, w["idx"])
    d_w_gate = ffn_dw_in(tag + "_dwgate", h, dg)
    d_w_up = ffn_dw_in(tag + "_dwup", h, du)
    dh = ffn_bwd_dh(tag + "_dh", dg, du, w["w_gate"], w["w_up"], w["idx"])
    dx, dg_pre = norm_bwd(tag + "_dpre", dh, x, w["g_pre"], dx1, F32)
    return dx, dict(w_gate=d_w_gate, w_up=d_w_up, w_down=d_w_down, g_pre=dg_pre, g_post=dg_post)


def _place():
    return lax.axis_index("x"), lax.axis_index("y"), lax.axis_index("c")


def all_gather(name, blks):
    n = len(blks)

    def body(*refs):
        x_refs, out_refs = refs[:n], refs[n:2 * n]
        send_sems, recv_sems, local_sems = refs[2 * n:]
        x, y, cc = _place()
        me, sibling = (x, y, cc), (x, y, 1 - cc)
        chips = [(1 - x, y), (x, 1 - y), (1 - x, 1 - y)]

        def slot(a, px, py, pc):
            return out_refs[a].at[4 * px + 2 * py + pc]

        def copy(a, k, block, to, src=None):
            return pltpu.make_async_remote_copy(
                src_ref=slot(a, *block) if src is None else src, dst_ref=slot(a, *block),
                send_sem=send_sems.at[7 * a + k], recv_sem=recv_sems.at[7 * a + k], device_id=to, device_id_type=MESH)

        mine = [pltpu.make_async_copy(x_refs[a], slot(a, *me), local_sems.at[a]) for a in range(n)]
        first = []
        for a in range(n):
            mine[a].start()
            first.append(copy(a, 0, me, sibling, src=x_refs[a]))
            first += [copy(a, 1 + j, me, (*chip, cc), src=x_refs[a]) for j, chip in enumerate(chips)]
        for cp in first:
            cp.start()
        passed = []
        for j, chip in enumerate(chips):
            for a in range(n):
                copy(a, 1 + j, (*chip, cc), me).wait_recv()
                passed.append(copy(a, 4 + j, (*chip, cc), sibling))
                passed[-1].start()
        for a in range(n):
            copy(a, 0, sibling, me).wait_recv()
            for j, chip in enumerate(chips):
                copy(a, 4 + j, (*chip, 1 - cc), me).wait_recv()
        for cp in first + passed:
            cp.wait_send()
        for cp in mine:
            cp.wait()

    return pl.pallas_call(
        body, out_shape=[_sds((N_DEV,) + b.shape, b.dtype) for b in blks], in_specs=[ANY] * n, out_specs=[ANY] * n,
        scratch_shapes=[pltpu.SemaphoreType.DMA((7 * n,)), pltpu.SemaphoreType.DMA((7 * n,)),
                        pltpu.SemaphoreType.DMA((n,))],
        name=name)(*blks)


def exchange_pair(name, gs):
    n = len(gs)
    nchip = 4

    def body(*refs):
        g_refs, land_refs = refs[:n], refs[n:2 * n]
        send_sems, recv_sems = refs[2 * n:]
        x, y, cc = _place()
        copies = [pltpu.make_async_remote_copy(
            src_ref=g_refs[a].at[j, 1 - cc], dst_ref=land_refs[a].at[j], send_sem=send_sems.at[nchip * a + j],
            recv_sem=recv_sems.at[nchip * a + j], device_id=(x, y, 1 - cc), device_id_type=MESH)
            for a in range(n) for j in range(nchip)]
        for cp in copies:
            cp.start()
        for cp in copies:
            cp.wait()

    return pl.pallas_call(
        body, out_shape=[_sds((nchip,) + g.shape[2:], g.dtype) for g in gs], in_specs=[ANY] * n, out_specs=[ANY] * n,
        scratch_shapes=[pltpu.SemaphoreType.DMA((nchip * n,)), pltpu.SemaphoreType.DMA((nchip * n,))],
        name=name)(*gs)


def pair_sum(name, g, land, core, out_dtype):
    nchip, _, r, c = g.shape
    tr = _row_tile(r, 128)

    def body(core_ref, g_ref, l_ref, o_ref):
        o_ref[...] = (g_ref[...] + l_ref[...]).astype(o_ref.dtype)

    return pl.pallas_call(
        body,
        grid_spec=pltpu.PrefetchScalarGridSpec(
            num_scalar_prefetch=1, grid=(nchip, r // tr),
            in_specs=[pl.BlockSpec((None, None, tr, c), lambda j, i, core_ref: (j, core_ref[0], i, 0)),
                      pl.BlockSpec((None, tr, c), lambda j, i, core_ref: (j, i, 0))],
            out_specs=pl.BlockSpec((None, tr, c), lambda j, i, core_ref: (j, i, 0))),
        out_shape=_sds((nchip, r, c), out_dtype), compiler_params=_params("parallel", "parallel"), name=name,
    )(core, g, land)


def exchange_chips(name, ps):
    n = len(ps)

    def body(*refs):
        p_refs, land_refs = refs[:n], refs[n:2 * n]
        send_sems, recv_sems, local_sems = refs[2 * n:]
        x, y, cc = _place()
        mine = 2 * x + y
        chips = [(1 - x, y), (x, 1 - y), (1 - x, 1 - y)]
        own = [pltpu.make_async_copy(p_refs[a].at[mine], land_refs[a].at[mine], local_sems.at[a]) for a in range(n)]
        for cp in own:
            cp.start()
        sends = [pltpu.make_async_remote_copy(
            src_ref=p_refs[a].at[2 * px + py], dst_ref=land_refs[a].at[mine], send_sem=send_sems.at[3 * a + k],
            recv_sem=recv_sems.at[3 * a + k], device_id=(px, py, cc), device_id_type=MESH)
            for a in range(n) for k, (px, py) in enumerate(chips)]
        for cp in sends:
            cp.start()
        for a in range(n):
            for k, (px, py) in enumerate(chips):
                pltpu.make_async_remote_copy(
                    src_ref=p_refs[a].at[mine], dst_ref=land_refs[a].at[2 * px + py], send_sem=send_sems.at[3 * a + k],
                    recv_sem=recv_sems.at[3 * a + k], device_id=(px, py, cc), device_id_type=MESH).wait_recv()
        for cp in sends:
            cp.wait_send()
        for cp in own:
            cp.wait()

    return pl.pallas_call(
        body, out_shape=[_sds(p.shape, p.dtype) for p in ps], in_specs=[ANY] * n, out_specs=[ANY] * n,
        scratch_shapes=[pltpu.SemaphoreType.DMA((3 * n,)), pltpu.SemaphoreType.DMA((3 * n,)),
                        pltpu.SemaphoreType.DMA((n,))],
        name=name)(*ps)


def adamw(name, parts, w, m, v):
    npart, r, c = parts.shape
    tr = _row_tile(r, 128)
    c1 = 1.0 / (1.0 - ADAM_B1 ** ADAM_STEP)
    c2 = 1.0 / (1.0 - ADAM_B2 ** ADAM_STEP)

    def body(p_ref, w_ref, m_ref, v_ref, g_ref, d_ref, nm_ref, nv_ref):
        g = p_ref[0].astype(F32)
        for j in range(1, npart):
            g = g + p_ref[j].astype(F32)
        nm = ADAM_B1 * m_ref[...] + (1.0 - ADAM_B1) * g
        nv = ADAM_B2 * v_ref[...] + (1.0 - ADAM_B2) * (g * g)
        g_ref[...] = g
        nm_ref[...] = nm
        nv_ref[...] = nv
        d_ref[...] = -ADAM_LR * ((nm * c1) / (jnp.sqrt(nv * c2) + ADAM_EPS) + ADAM_WD * w_ref[...])

    row = pl.BlockSpec((tr, c), lambda i: (i, 0))
    return pl.pallas_call(
        body, grid=(r // tr,), in_specs=[pl.BlockSpec((npart, tr, c), lambda i: (0, i, 0)), row, row, row],
        out_specs=[row] * 4, out_shape=[_sds((r, c), F32)] * 4, compiler_params=_params("parallel"), name=name,
    )(parts, w, m, v)


def _pack(arrays, dtype, row_multiple):
    flat = jnp.concatenate([a.astype(dtype).reshape(-1) for a in arrays])
    per = row_multiple * LANES
    total = -(-flat.shape[0] // per) * per
    return jnp.pad(flat, (0, total - flat.shape[0])).reshape(total // LANES, LANES)


def _pack_blocked(arrays, dtype, row_multiple):
    flat = jnp.concatenate([a.astype(dtype).reshape(N_DEV, -1) for a in arrays], axis=1)
    per = row_multiple * LANES
    total = -(-flat.shape[1] // per) * per
    return jnp.pad(flat, ((0, 0), (0, total - flat.shape[1]))).reshape(N_DEV, total // LANES, LANES)


def _unpack(buf, shapes, lead=()):
    flat = buf.reshape(lead + (-1,))
    out, off = [], 0
    for s in shapes:
        n = math.prod(s)
        out.append(flat[..., off:off + n].reshape(lead + tuple(s)))
        off += n
    return out


def _to_blocked(full, ax):
    s = full.shape
    return jnp.moveaxis(full.reshape(s[:ax] + (N_DEV, s[ax] // N_DEV) + s[ax + 1:]), ax, 0)


def _from_blocked(blk, ax):
    moved = jnp.moveaxis(blk, 0, ax)
    s = moved.shape
    return moved.reshape(s[:ax] + (s[ax] * s[ax + 1],) + s[ax + 2:])


SMALL = ("rg_conv_w", "rg_conv_b", "rg_b_a", "rg_b_i", "rg_lambda")
BIG = ("attn_w_in", "attn_w_out", "rg_w_in", "rg_w_a", "rg_w_i", "rg_w_out", "ffn_w_gate", "ffn_w_up", "ffn_w_down")


def kernel(x, attn_w_in, attn_rel_bias, attn_w_out, rg_w_in, rg_conv_w, rg_conv_b, rg_w_a, rg_b_a, rg_w_i, rg_b_i, rg_lambda, rg_w_out, norm_mix_pre, norm_mix_post, norm_ffn_pre, norm_ffn_post, ffn_w_gate, ffn_w_up, ffn_w_down, loss_target, m_attn_w_in, m_attn_rel_bias, m_attn_w_out, m_rg_w_in, m_rg_conv_w, m_rg_conv_b, m_rg_w_a, m_rg_b_a, m_rg_w_i, m_rg_b_i, m_rg_lambda, m_rg_w_out, m_norm_mix_pre, m_norm_mix_post, m_norm_ffn_pre, m_norm_ffn_post, m_ffn_w_gate, m_ffn_w_up, m_ffn_w_down, v_attn_w_in, v_attn_rel_bias, v_attn_w_out, v_rg_w_in, v_rg_conv_w, v_rg_conv_b, v_rg_w_a, v_rg_b_a, v_rg_w_i, v_rg_b_i, v_rg_lambda, v_rg_w_out, v_norm_mix_pre, v_norm_mix_post, v_norm_ffn_pre, v_norm_ffn_post, v_ffn_w_gate, v_ffn_w_up, v_ffn_w_down):
    w_loc = dict(attn_w_in=attn_w_in, attn_rel_bias=attn_rel_bias, attn_w_out=attn_w_out, rg_w_in=rg_w_in,
                 rg_conv_w=rg_conv_w, rg_conv_b=rg_conv_b, rg_w_a=rg_w_a, rg_b_a=rg_b_a, rg_w_i=rg_w_i, rg_b_i=rg_b_i,
                 rg_lambda=rg_lambda, rg_w_out=rg_w_out, norm_mix_pre=norm_mix_pre, norm_mix_post=norm_mix_post,
                 norm_ffn_pre=norm_ffn_pre, norm_ffn_post=norm_ffn_post, ffn_w_gate=ffn_w_gate, ffn_w_up=ffn_w_up,
                 ffn_w_down=ffn_w_down)
    m_loc = dict(attn_w_in=m_attn_w_in, attn_rel_bias=m_attn_rel_bias, attn_w_out=m_attn_w_out, rg_w_in=m_rg_w_in,
                 rg_conv_w=m_rg_conv_w, rg_conv_b=m_rg_conv_b, rg_w_a=m_rg_w_a, rg_b_a=m_rg_b_a, rg_w_i=m_rg_w_i,
                 rg_b_i=m_rg_b_i, rg_lambda=m_rg_lambda, rg_w_out=m_rg_w_out, norm_mix_pre=m_norm_mix_pre,
                 norm_mix_post=m_norm_mix_post, norm_ffn_pre=m_norm_ffn_pre, norm_ffn_post=m_norm_ffn_post,
                 ffn_w_gate=m_ffn_w_gate, ffn_w_up=m_ffn_w_up, ffn_w_down=m_ffn_w_down)
    v_loc = dict(attn_w_in=v_attn_w_in, attn_rel_bias=v_attn_rel_bias, attn_w_out=v_attn_w_out, rg_w_in=v_rg_w_in,
                 rg_conv_w=v_rg_conv_w, rg_conv_b=v_rg_conv_b, rg_w_a=v_rg_w_a, rg_b_a=v_rg_b_a, rg_w_i=v_rg_w_i,
                 rg_b_i=v_rg_b_i, rg_lambda=v_rg_lambda, rg_w_out=v_rg_w_out, norm_mix_pre=v_norm_mix_pre,
                 norm_mix_post=v_norm_mix_post, norm_ffn_pre=v_norm_ffn_pre, norm_ffn_post=v_norm_ffn_post,
                 ffn_w_gate=v_ffn_w_gate, ffn_w_up=v_ffn_w_up, ffn_w_down=v_ffn_w_down)
    axis_of = dict(SHARDED)
    xt, target = x[0], loss_target[0]
    d_model = xt.shape[1]
    rows2d = lambda a: a.reshape(-1, a.shape[-1])
    small_shapes = [w_loc[n].shape for n in SMALL]

    gathered = all_gather("gather_weights", [rows2d(w_loc[n]).astype(BF16) for n in BIG]
                          + [_pack([w_loc[n] for n in SMALL], F32, 8)])
    blocked = {n: g.reshape((N_DEV,) + w_loc[n].shape) for n, g in zip(BIG, gathered)}
    blocked.update(zip(SMALL, _unpack(gathered[-1], small_shapes, (N_DEV,))))
    full = {n: _from_blocked(blocked[n], axis_of[n]) for n in SMALL}
    row = lambda a: a.reshape(1, -1).astype(F32)
    square = lambda rows8: rows8.reshape(-1, rows8.shape[-1])
    gates = lambda g: jnp.swapaxes(g, 0, 1).reshape(LRU_BLOCKS, -1, g.shape[-1])

    def layer_weights(layer):
        j = layer // 2
        norms = dict(g_pre=row(norm_mix_pre[layer]), g_post=row(norm_mix_post[layer]), idx=j)
        if layer % 2 == 0:
            mix = dict(w_in=blocked["attn_w_in"], w_out=square(blocked["attn_w_out"][:, j]),
                       rel_bias=attn_rel_bias[j], **norms)
        else:
            mix = dict(w_in=blocked["rg_w_in"], w_out=square(blocked["rg_w_out"][:, j]),
                       conv_w=full["rg_conv_w"][j][:, 0, :], conv_b=row(full["rg_conv_b"][j]),
                       w_a=gates(blocked["rg_w_a"][:, j]), w_i=gates(blocked["rg_w_i"][:, j]),
                       b_a=row(full["rg_b_a"][j]), b_i=row(full["rg_b_i"][j]), lam=row(full["rg_lambda"][j]), **norms)
        ffn = dict(w_gate=blocked["ffn_w_gate"], w_up=blocked["ffn_w_up"], w_down=blocked["ffn_w_down"], idx=layer,
                   g_pre=row(norm_ffn_pre[layer]), g_post=row(norm_ffn_post[layer]))
        return mix, ffn

    act, tape = xt, []
    for layer in range(DEPTH):
        mix_w, ffn_w = layer_weights(layer)
        mixer_fwd = attn_layer_fwd if layer % 2 == 0 else rg_layer_fwd
        act, saved_mix = mixer_fwd(f"l{layer}_mix", act, mix_w)
        act, saved_ffn = ffn_layer_fwd(f"l{layer}_ffn", act, ffn_w)
        tape.append((mix_w, ffn_w, saved_mix, saved_ffn))
    dact, sq = loss_grad("loss", act, target)
    loss = lax.psum(0.5 * jnp.sum(sq) / d_model, ("x", "y", "c"))

    grads = {}
    for layer in reversed(range(DEPTH)):
        mix_w, ffn_w, saved_mix, saved_ffn = tape[layer]
        dact, grads[("ffn", layer)] = ffn_layer_bwd(f"l{layer}_ffn", dact, saved_ffn, ffn_w)
        mixer_bwd = attn_layer_bwd if layer % 2 == 0 else rg_layer_bwd
        dact, grads[("mix", layer)] = mixer_bwd(f"l{layer}_mix", dact, saved_mix, mix_w)
    attn_g = [grads[("mix", l)] for l in range(0, DEPTH, 2)]
    rg_g = [grads[("mix", l)] for l in range(1, DEPTH, 2)]
    ffn_g = [grads[("ffn", l)] for l in range(DEPTH)]
    stack = lambda gs, key: jnp.stack([g[key] for g in gs])
    by_owner = lambda gs, key, f: jnp.stack([f(g[key]) for g in gs], axis=1)
    rows8 = lambda a: a.reshape(N_DEV, -1, a.shape[-1])
    ungates = lambda a: jnp.swapaxes(a.reshape(LRU_BLOCKS, N_DEV, -1, a.shape[-1]), 0, 1)
    same = lambda a: a
    blocked_g = dict(
        attn_w_in=by_owner(attn_g, "w_in", same), attn_w_out=by_owner(attn_g, "w_out", rows8),
        rg_w_in=by_owner(rg_g, "w_in", same), rg_w_out=by_owner(rg_g, "w_out", rows8),
        rg_w_a=by_owner(rg_g, "w_a", ungates), rg_w_i=by_owner(rg_g, "w_i", ungates),
        ffn_w_gate=by_owner(ffn_g, "w_gate", same), ffn_w_up=by_owner(ffn_g, "w_up", same),
        ffn_w_down=by_owner(ffn_g, "w_down", same))
    contrib = dict(
        attn_rel_bias=stack(attn_g, "rel_bias"), rg_conv_w=stack(rg_g, "conv_w")[:, :, None, :],
        rg_conv_b=stack(rg_g, "conv_b")[:, 0], rg_b_a=stack(rg_g, "b_a").reshape(rg_b_a.shape[0], LRU_BLOCKS, -1),
        rg_b_i=stack(rg_g, "b_i").reshape(rg_b_i.shape[0], LRU_BLOCKS, -1), rg_lambda=stack(rg_g, "lam")[:, 0],
        norm_mix_pre=jnp.concatenate([grads[("mix", l)]["g_pre"] for l in range(DEPTH)]),
        norm_mix_post=jnp.concatenate([grads[("mix", l)]["g_post"] for l in range(DEPTH)]),
        norm_ffn_pre=jnp.concatenate([g["g_pre"] for g in ffn_g]),
        norm_ffn_post=jnp.concatenate([g["g_post"] for g in ffn_g]),
    )
    small_g = _pack_blocked([_to_blocked(contrib[n], axis_of[n]) for n in SMALL], F32, 8)

    slabs = [blocked_g[n].reshape(4, 2, -1, blocked_g[n].shape[-1]) for n in BIG] + [small_g.reshape(4, 2, -1, LANES)]
    core = lax.axis_index("c").astype(jnp.int32).reshape(1)
    from_sibling = exchange_pair("rs_pair", slabs)
    pairs = [pair_sum(f"rs_pair_sum_{i}", g, l, core, BF16 if i < len(BIG) else F32)
             for i, (g, l) in enumerate(zip(slabs, from_sibling))]
    by_chip = exchange_chips("rs_chips", pairs)
    result = {}
    kinds = ("grad", "delta", "new_m", "new_v")
    for n, parts in zip(BIG, by_chip):
        outs = adamw("adamw_" + n, parts, *[rows2d(d[n]) for d in (w_loc, m_loc, v_loc)])
        for kind, a in zip(kinds, outs):
            result[(kind, n)] = a.reshape(w_loc[n].shape)
    outs = adamw("adamw_small", by_chip[-1], *[_pack([d[n] for n in SMALL], F32, 8) for d in (w_loc, m_loc, v_loc)])
    for kind, buf in zip(kinds, outs):
        result.update({(kind, n): a for n, a in zip(SMALL, _unpack(buf, small_shapes))})
    rep_shapes = [w_loc[n].shape for n in REPLICATED]
    rep_parts, = all_gather("gather_rep_grads", [_pack([contrib[n] for n in REPLICATED], F32, 8)])
    outs = adamw("adamw_replicated", rep_parts, *[_pack([d[n] for n in REPLICATED], F32, 8)
                                                  for d in (w_loc, m_loc, v_loc)])
    for kind, buf in zip(kinds, outs):
        result.update({(kind, n): a for n, a in zip(REPLICATED, _unpack(buf, rep_shapes))})
    return (loss, dact[None], *[result[(kind, n)] for kind in kinds for n in WEIGHTS])
```

```python
import functools
import math

import jax
import jax.numpy as jnp
from jax import lax
from jax.experimental import pallas as pl
from jax.experimental.pallas import tpu as pltpu

F32 = jnp.float32
BF16 = jnp.bfloat16

N_DEV = 8
DEPTH = 4
CHUNK = 64
N_LEFT = 8
BAND = (N_LEFT + 1) * CHUNK
PAD_KEYS = N_LEFT * CHUNK
HEAD_DIM = 64
N_HEADS = 8
REL_CLIP = 256
LRU_BLOCKS = 4
LRU_C = 8.0
RMS_EPS = 1e-6
QK_SCALE = HEAD_DIM ** -0.5

ADAM_LR = 0.001
ADAM_B1 = 0.9
ADAM_B2 = 0.999
ADAM_EPS = 1e-08
ADAM_WD = 0.01
ADAM_STEP = 10

LANES = 1024
V7X_VMEM_LIMIT = 56 * 1024 * 1024

MESH = pl.DeviceIdType.MESH
ANY = pl.BlockSpec(memory_space=pl.ANY)

SHARDED = (
    ("attn_w_in", 2), ("attn_w_out", 1), ("rg_w_in", 2), ("rg_conv_w", 3), ("rg_conv_b", 1),
    ("rg_w_a", 2), ("rg_b_a", 2), ("rg_w_i", 2), ("rg_b_i", 2), ("rg_lambda", 1), ("rg_w_out", 1),
    ("ffn_w_gate", 2), ("ffn_w_up", 2), ("ffn_w_down", 1),
)
REPLICATED = ("attn_rel_bias", "norm_mix_pre", "norm_mix_post", "norm_ffn_pre", "norm_ffn_post")
WEIGHTS = ("attn_w_in", "attn_rel_bias", "attn_w_out", "rg_w_in", "rg_conv_w", "rg_conv_b", "rg_w_a", "rg_b_a",
           "rg_w_i", "rg_b_i", "rg_lambda", "rg_w_out", "norm_mix_pre", "norm_mix_post", "norm_ffn_pre",
           "norm_ffn_post", "ffn_w_gate", "ffn_w_up", "ffn_w_down")


def _params(*dims):
    return pltpu.CompilerParams(dimension_semantics=dims or None, vmem_limit_bytes=V7X_VMEM_LIMIT)


def _sds(shape, dtype):
    return jax.ShapeDtypeStruct(tuple(shape), dtype)


def _row_tile(n, pref):
    t = min(n, pref)
    assert n % t == 0, (n, pref)
    return t


NN = (((1,), (0,)), ((), ()))
NT = (((1,), (1,)), ((), ()))
TN = (((0,), (0,)), ((), ()))


def _gmm(name, a, b, *, grid, a_blk, a_idx, b_blk, b_idx, o_blk, o_idx, out_shape, out_dtype, dn, acc_shape):
    nk = grid[-1]
    kax = len(grid) - 1

    def body(a_ref, b_ref, o_ref, acc_ref):
        part = lax.dot_general(a_ref[...], b_ref[...], dn, preferred_element_type=F32)
        if nk == 1:
            o_ref[...] = part.astype(o_ref.dtype)
            return
        k = pl.program_id(kax)

        @pl.when(k == 0)
        def _():
            acc_ref[...] = part

        @pl.when(k > 0)
        def _():
            acc_ref[...] += part

        @pl.when(k == nk - 1)
        def _():
            o_ref[...] = acc_ref[...].astype(o_ref.dtype)

    return pl.pallas_call(
        body, grid=grid,
        in_specs=[pl.BlockSpec(a_blk, a_idx), pl.BlockSpec(b_blk, b_idx)],
        out_specs=pl.BlockSpec(o_blk, o_idx),
        out_shape=_sds(out_shape, out_dtype),
        scratch_shapes=[pltpu.VMEM(acc_shape, F32)],
        compiler_params=_params(*(["parallel"] * kax + ["arbitrary"])),
        name=name,
    )(a, b)


def mm_nn(name, a, b, out_dtype, tm=1024, tn=512, tk=1024):
    (m, k), (_, n) = a.shape, b.shape
    tm, tn, tk = _row_tile(m, tm), _row_tile(n, tn), _row_tile(k, tk)
    return _gmm(name, a, b, grid=(m // tm, n // tn, k // tk),
                a_blk=(tm, tk), a_idx=lambda i, j, kk: (i, kk), b_blk=(tk, tn), b_idx=lambda i, j, kk: (kk, j),
                o_blk=(tm, tn), o_idx=lambda i, j, kk: (i, j), out_shape=(m, n), out_dtype=out_dtype, dn=NN,
                acc_shape=(tm, tn))


def mm_nt(name, a, b, out_dtype, tm=1024, tn=512, tk=1024):
    (m, k), (n, _) = a.shape, b.shape
    tm, tn, tk = _row_tile(m, tm), _row_tile(n, tn), _row_tile(k, tk)
    return _gmm(name, a, b, grid=(m // tm, n // tn, k // tk),
                a_blk=(tm, tk), a_idx=lambda i, j, kk: (i, kk), b_blk=(tn, tk), b_idx=lambda i, j, kk: (j, kk),
                o_blk=(tm, tn), o_idx=lambda i, j, kk: (i, j), out_shape=(m, n), out_dtype=out_dtype, dn=NT,
                acc_shape=(tm, tn))


def mm_tn(name, a, b, out_dtype, tm=512, tn=512, tk=1024):
    (k, m), (_, n) = a.shape, b.shape
    tm, tn, tk = _row_tile(m, tm), _row_tile(n, tn), _row_tile(k, tk)
    return _gmm(name, a, b, grid=(m // tm, n // tn, k // tk),
                a_blk=(tk, tm), a_idx=lambda i, j, kk: (kk, i), b_blk=(tk, tn), b_idx=lambda i, j, kk: (kk, j),
                o_blk=(tm, tn), o_idx=lambda i, j, kk: (i, j), out_shape=(m, n), out_dtype=out_dtype, dn=TN,
                acc_shape=(tm, tn))


def mm_nn_wblk(name, a, wb, layer, out_dtype, tm=1024, tk=1024):
    (m, k), (nb, _, _, n8) = a.shape, wb.shape
    tm, tk = _row_tile(m, tm), _row_tile(k, tk)
    return _gmm(name, a, wb, grid=(m // tm, nb, k // tk),
                a_blk=(tm, tk), a_idx=lambda i, j, kk: (i, kk),
                b_blk=(None, None, tk, n8), b_idx=lambda i, j, kk: (j, layer, kk, 0),
                o_blk=(tm, n8), o_idx=lambda i, j, kk: (i, j), out_shape=(m, nb * n8), out_dtype=out_dtype, dn=NN,
                acc_shape=(tm, n8))


def mm_nt_wblk(name, a, wb, layer, out_dtype, tm=1024, tn=512):
    m = a.shape[0]
    nb, _, k, n8 = wb.shape
    tm, tn = _row_tile(m, tm), _row_tile(k, tn)

    def body(a_ref, b_ref, o_ref):
        acc = lax.dot_general(a_ref[:, 0:n8], b_ref[0], NT, preferred_element_type=F32)
        for j in range(1, nb):
            acc = acc + lax.dot_general(a_ref[:, j * n8:(j + 1) * n8], b_ref[j], NT, preferred_element_type=F32)
        o_ref[...] = acc.astype(o_ref.dtype)

    return pl.pallas_call(
        body, grid=(m // tm, k // tn),
        in_specs=[pl.BlockSpec((tm, nb * n8), lambda i, j: (i, 0)),
                  pl.BlockSpec((nb, None, tn, n8), lambda i, j: (0, layer, j, 0))],
        out_specs=pl.BlockSpec((tm, tn), lambda i, j: (i, j)), out_shape=_sds((m, k), out_dtype),
        compiler_params=_params("parallel", "parallel"), name=name)(a, wb)


def mm_tn_oblk(name, a, b, n8, out_dtype, tk=2048):
    (t, k), nb = a.shape, b.shape[1] // n8
    tk = _row_tile(t, tk)
    return _gmm(name, a, b, grid=(nb, t // tk),
                a_blk=(tk, k), a_idx=lambda j, s: (s, 0), b_blk=(tk, n8), b_idx=lambda j, s: (s, j),
                o_blk=(None, k, n8), o_idx=lambda j, s: (j, 0, 0), out_shape=(nb, k, n8), out_dtype=out_dtype, dn=TN,
                acc_shape=(k, n8))


def rmsnorm_fwd(name, x, g):
    t, d = x.shape
    tr = _row_tile(t, 512)

    def body(x_ref, g_ref, o_ref):
        xv = x_ref[...]
        r = lax.rsqrt(jnp.mean(xv * xv, axis=-1, keepdims=True) + RMS_EPS)
        o_ref[...] = (xv * r * g_ref[...]).astype(o_ref.dtype)

    return pl.pallas_call(
        body, grid=(t // tr,),
        in_specs=[pl.BlockSpec((tr, d), lambda i: (i, 0)), pl.BlockSpec((1, d), lambda i: (0, 0))],
        out_specs=pl.BlockSpec((tr, d), lambda i: (i, 0)),
        out_shape=_sds((t, d), BF16), compiler_params=_params("parallel"), name=name)(x, g)


def resid_norm_fwd(name, x, m, g):
    t, d = x.shape
    tr = _row_tile(t, 512)

    def body(x_ref, m_ref, g_ref, o_ref):
        mv = m_ref[...]
        r = lax.rsqrt(jnp.mean(mv * mv, axis=-1, keepdims=True) + RMS_EPS)
        o_ref[...] = x_ref[...] + mv * r * g_ref[...]

    return pl.pallas_call(
        body, grid=(t // tr,),
        in_specs=[pl.BlockSpec((tr, d), lambda i: (i, 0)), pl.BlockSpec((tr, d), lambda i: (i, 0)),
                  pl.BlockSpec((1, d), lambda i: (0, 0))],
        out_specs=pl.BlockSpec((tr, d), lambda i: (i, 0)),
        out_shape=_sds((t, d), F32), compiler_params=_params("parallel"), name=name)(x, m, g)


def norm_bwd(name, dy, x, g, resid, out_dtype):
    t, d = x.shape
    tr = _row_tile(t, 512)
    has_res = resid is not None

    def body(*refs):
        if has_res:
            dy_ref, x_ref, g_ref, r_ref, dx_ref, dg_ref = refs
        else:
            dy_ref, x_ref, g_ref, dx_ref, dg_ref = refs
        i = pl.program_id(0)
        xv = x_ref[...]
        dyv = dy_ref[...].astype(F32)
        r = lax.rsqrt(jnp.mean(xv * xv, axis=-1, keepdims=True) + RMS_EPS)
        xh = xv * r
        dxh = dyv * g_ref[...]
        dx = r * (dxh - xh * jnp.mean(dxh * xh, axis=-1, keepdims=True))
        if has_res:
            dx = dx + r_ref[...]
        dx_ref[...] = dx.astype(dx_ref.dtype)
        part = jnp.sum(dyv * xh, axis=0, keepdims=True)

        @pl.when(i == 0)
        def _():
            dg_ref[...] = part

        @pl.when(i > 0)
        def _():
            dg_ref[...] += part

    row = pl.BlockSpec((tr, d), lambda i: (i, 0))
    vec = pl.BlockSpec((1, d), lambda i: (0, 0))
    ins = [dy, x, g] + ([resid] if has_res else [])
    return pl.pallas_call(
        body, grid=(t // tr,),
        in_specs=[row, row, vec] + ([row] if has_res else []),
        out_specs=[row, vec],
        out_shape=[_sds((t, d), out_dtype), _sds((1, d), F32)],
        compiler_params=_params("arbitrary"), name=name)(*ins)


def loss_grad(name, y, target):
    t, d = y.shape
    tr = _row_tile(t, 512)

    def body(y_ref, t_ref, dy_ref, s_ref):
        i = pl.program_id(0)
        err = y_ref[...] - t_ref[...]
        dy_ref[...] = err * (1.0 / d)
        part = jnp.sum(err * err, axis=0, keepdims=True)

        @pl.when(i == 0)
        def _():
            s_ref[...] = part

        @pl.when(i > 0)
        def _():
            s_ref[...] += part

    row = pl.BlockSpec((tr, d), lambda i: (i, 0))
    vec = pl.BlockSpec((1, d), lambda i: (0, 0))
    return pl.pallas_call(
        body, grid=(t // tr,), in_specs=[row, row], out_specs=[row, vec],
        out_shape=[_sds((t, d), F32), _sds((1, d), F32)],
        compiler_params=_params("arbitrary"), name=name)(y, target)


PAIR = 2 * HEAD_DIM
N_PAIRS = N_HEADS // 2
A_TQ = 512
A_UNROLL = 4


def _halves(x):
    lane = lax.broadcasted_iota(jnp.int32, x.shape, x.ndim - 1)
    zero = jnp.zeros_like(x)
    return jnp.where(lane < HEAD_DIM, x, zero), jnp.where(lane >= HEAD_DIM, x, zero)


def _merge(a, b):
    lane = lax.broadcasted_iota(jnp.int32, a.shape, a.ndim - 1)
    return jnp.where(lane < HEAD_DIM, a, b)


def _a_valid(c):
    col = lax.broadcasted_iota(jnp.int32, (CHUNK, BAND), 1)
    return col >= (N_LEFT - c) * CHUNK


def attn_a_fwd(name, proj, kp, vp, bias, q_blk):
    t = proj.shape[0]
    tq = _row_tile(t, A_TQ)
    ncs = tq // CHUNK
    un = math.gcd(A_UNROLL, ncs)

    def body(q_ref, k_ref, v_ref, b_ref, o_ref, l_ref):
        i = pl.program_id(1)

        def group(gg, carry):
            cs = [i * ncs + gg * un + u for u in range(un)]
            r0s = [pl.multiple_of((gg * un + u) * CHUNK, CHUNK) for u in range(un)]
            k0s = [pl.multiple_of(c * CHUNK, CHUNK) for c in cs]
            ss = []
            for u in range(un):
                qh = _halves(q_ref[pl.ds(r0s[u], CHUNK), :] * QK_SCALE)
                kwin = k_ref[pl.ds(k0s[u], BAND), :]
                valid = _a_valid(cs[u])
                for hh in range(2):
                    s = lax.dot_general(qh[hh], kwin, NT, preferred_element_type=F32) + b_ref[hh]
                    ss.append(jnp.where(valid, s, -1e30))
            ps, lses = [], []
            for s in ss:
                mx = jnp.max(s, axis=-1, keepdims=True)
                p = jnp.exp(s - mx)
                den = jnp.sum(p, axis=-1, keepdims=True)
                ps.append((p * (1.0 / den)).astype(BF16))
                lses.append(mx + jnp.log(den))
            for u in range(un):
                vwin = v_ref[pl.ds(k0s[u], BAND), :]
                o0 = jnp.dot(ps[2 * u], vwin, preferred_element_type=F32)
                o1 = jnp.dot(ps[2 * u + 1], vwin, preferred_element_type=F32)
                o_ref[pl.ds(r0s[u], CHUNK), :] = _merge(o0, o1)
                l_ref[pl.ds(r0s[u], CHUNK), :] = jnp.concatenate([lses[2 * u], lses[2 * u + 1]], axis=1)
            return carry

        lax.fori_loop(0, ncs // un, group, 0)

    return pl.pallas_call(
        body, grid=(N_PAIRS, t // tq),
        in_specs=[pl.BlockSpec((tq, PAIR), lambda p, i: (i, q_blk + p)),
                  pl.BlockSpec((t + PAD_KEYS, PAIR), lambda p, i: (0, p)),
                  pl.BlockSpec((t + PAD_KEYS, PAIR), lambda p, i: (0, p)),
                  pl.BlockSpec((2, CHUNK, BAND), lambda p, i: (p, 0, 0))],
        out_specs=[pl.BlockSpec((tq, PAIR), lambda p, i: (i, p)),
                   pl.BlockSpec((None, tq, 2), lambda p, i: (p, i, 0))],
        out_shape=[_sds((t, N_PAIRS * PAIR), F32), _sds((N_PAIRS, t, 2), F32)],
        compiler_params=_params("parallel", "parallel"), name=name)(proj, kp, vp, bias)


def attn_a_bwd(name, proj, kp, vp, bias, o, lse, do, q_blk, do_blk):
    t = proj.shape[0]
    tq = _row_tile(t, A_TQ)
    ncs = tq // CHUNK
    un = math.gcd(A_UNROLL, ncs)

    def body(q_ref, k_ref, v_ref, b_ref, o_ref, l_ref, do_ref, dq_ref, dk_ref, dv_ref, db_ref):
        i = pl.program_id(1)

        @pl.when(i == 0)
        def _():
            dk_ref[...] = jnp.zeros_like(dk_ref)
            dv_ref[...] = jnp.zeros_like(dv_ref)
            db_ref[...] = jnp.zeros_like(db_ref)

        def group(gg, carry):
            cs = [i * ncs + gg * un + u for u in range(un)]
            r0s = [pl.multiple_of((gg * un + u) * CHUNK, CHUNK) for u in range(un)]
            k0s = [pl.multiple_of(c * CHUNK, CHUNK) for c in cs]
            qhs, dohs, ps, dps, deltas = [], [], [], [], []
            for u in range(un):
                rows = pl.ds(r0s[u], CHUNK)
                qh = _halves(q_ref[rows, :] * QK_SCALE)
                doh = _halves(do_ref[rows, :])
                kwin = k_ref[pl.ds(k0s[u], BAND), :]
                vwin = v_ref[pl.ds(k0s[u], BAND), :]
                valid = _a_valid(cs[u])
                dl = _halves(do_ref[rows, :].astype(F32) * o_ref[rows, :])
                for hh in range(2):
                    s = lax.dot_general(qh[hh], kwin, NT, preferred_element_type=F32) + b_ref[hh]
                    ps.append(jnp.where(valid, jnp.exp(s - l_ref[rows, hh:hh + 1]), 0.0))
                    dps.append(lax.dot_general(doh[hh], vwin, NT, preferred_element_type=F32))
                    deltas.append(jnp.sum(dl[hh], axis=-1, keepdims=True))
                qhs.append(qh)
                dohs.append(doh)
            dss = [p * (dp - dl) for p, dp, dl in zip(ps, dps, deltas)]
            for hh in range(2):
                tot = dss[hh]
                for u in range(1, un):
                    tot = tot + dss[2 * u + hh]
                db_ref[hh] += tot
            for u in range(un):
                kwin = k_ref[pl.ds(k0s[u], BAND), :]
                ds0, ds1 = dss[2 * u].astype(BF16), dss[2 * u + 1].astype(BF16)
                dq_ref[pl.ds(r0s[u], CHUNK), :] = _merge(jnp.dot(ds0, kwin, preferred_element_type=F32),
                                                         jnp.dot(ds1, kwin, preferred_element_type=F32)) * QK_SCALE
                dk_ref[pl.ds(k0s[u], BAND), :] += (lax.dot_general(ds0, qhs[u][0], TN, preferred_element_type=F32)
                                                   + lax.dot_general(ds1, qhs[u][1], TN, preferred_element_type=F32))
                dv_ref[pl.ds(k0s[u], BAND), :] += (
                    lax.dot_general(ps[2 * u].astype(BF16), dohs[u][0], TN, preferred_element_type=F32)
                    + lax.dot_general(ps[2 * u + 1].astype(BF16), dohs[u][1], TN, preferred_element_type=F32))
            return carry

        lax.fori_loop(0, ncs // un, group, 0)

    tile = lambda blk: pl.BlockSpec((tq, PAIR), lambda p, i: (i, blk + p))
    whole = pl.BlockSpec((t + PAD_KEYS, PAIR), lambda p, i: (0, p))
    bspec = pl.BlockSpec((2, CHUNK, BAND), lambda p, i: (p, 0, 0))
    return pl.pallas_call(
        body, grid=(N_PAIRS, t // tq),
        in_specs=[tile(q_blk), whole, whole, bspec, tile(0), pl.BlockSpec((None, tq, 2), lambda p, i: (p, i, 0)),
                  tile(do_blk)],
        out_specs=[tile(0), whole, whole, bspec],
        out_shape=[_sds((t, N_PAIRS * PAIR), F32), _sds((t + PAD_KEYS, N_PAIRS * PAIR), F32),
                   _sds((t + PAD_KEYS, N_PAIRS * PAIR), F32), _sds((2 * N_PAIRS, CHUNK, BAND), F32)],
        compiler_params=_params("parallel", "arbitrary"), name=name)(proj, kp, vp, bias, o, lse, do)


SB_TQ = 256
SB_TK = 256
SB_DEAD = -125.0


def _tri(n, strict):
    j = lax.broadcasted_iota(jnp.int32, (n, n), 0)
    s = lax.broadcasted_iota(jnp.int32, (n, n), 1)
    return jnp.where((j > s) if strict else (j >= s), 1.0, 0.0).astype(BF16)


def _suffix_sum(x, tri, exact):
    hi = x.astype(BF16)
    out = jnp.dot(hi, tri, preferred_element_type=F32)
    if exact:
        lo = (x - hi.astype(F32)).astype(BF16)
        out = out + jnp.dot(lo, tri, preferred_element_type=F32)
    return out


def _sb_scores(qh, ks, causal):
    z = lax.dot_general(qh, ks, NT, preferred_element_type=F32)
    lb = jnp.minimum(z, 0.0) - jnp.log(1.0 + jnp.exp(-jnp.abs(z)))
    m = lb - z
    if causal is not None:
        m = jnp.where(causal, m, 0.0)
    return lb, m


def _causal(tq, tk, off):
    return (lax.broadcasted_iota(jnp.int32, (tq, tk), 1) + off * tk) < lax.broadcasted_iota(jnp.int32, (tq, tk), 0)


def sb_fwd(name, proj, q_blk, k_blk, v_blk):
    t = proj.shape[0]
    tq = _row_tile(t, SB_TQ)
    tk = min(SB_TK, tq)
    per = tq // tk

    def body(q_ref, k_ref, v_ref, o_ref):
        i = pl.program_id(1)
        tri = _tri(tk, True)
        qh = _halves(q_ref[...] * QK_SCALE)

        def blocks(kb, carry, off):
            k0 = pl.multiple_of(kb * tk, tk)
            ks, vs = k_ref[pl.ds(k0, tk), :], v_ref[pl.ds(k0, tk), :]
            causal = None if off is None else _causal(tq, tk, off)
            lbm = [_sb_scores(qh[hh], ks, causal) for hh in range(2)]
            afters = [_suffix_sum(lbm[hh][1], tri, False) for hh in range(2)]
            out = []
            for hh in range(2):
                acc, cm = carry[2 * hh], carry[2 * hh + 1]
                w = jnp.exp(lbm[hh][0] + afters[hh] + cm)
                if causal is not None:
                    w = jnp.where(causal, w, 0.0)
                out += [acc + jnp.dot(w.astype(BF16), vs, preferred_element_type=F32),
                        cm + jnp.sum(lbm[hh][1], axis=-1, keepdims=True)]
            return tuple(out)

        def alive(carry):
            return jnp.maximum(jnp.max(carry[1]), jnp.max(carry[3])) > SB_DEAD

        carry = (jnp.zeros((tq, PAIR), F32), jnp.zeros((tq, 1), F32)) * 2
        for off in reversed(range(per)):
            carry = blocks(i * per + off, carry, off)

        def step(c):
            new = blocks(i * per - 1 - c[0], c[2:], None)
            return (c[0] + 1, alive(new)) + new

        out = lax.while_loop(lambda c: jnp.logical_and(c[0] < i * per, c[1]), step,
                             (jnp.int32(0), alive(carry)) + carry)
        o_ref[...] = _merge(out[2], out[4])

    return pl.pallas_call(
        body, grid=(N_PAIRS, t // tq),
        in_specs=[pl.BlockSpec((tq, PAIR), lambda p, i: (i, q_blk + p)),
                  pl.BlockSpec((t, PAIR), lambda p, i: (0, k_blk + p)),
                  pl.BlockSpec((t, PAIR), lambda p, i: (0, v_blk + p))],
        out_specs=pl.BlockSpec((tq, PAIR), lambda p, i: (i, p)),
        out_shape=_sds((t, N_PAIRS * PAIR), F32), compiler_params=_params("parallel", "parallel"), name=name,
    )(proj, proj, proj)


def sb_bwd(name, proj, o, do, q_blk, k_blk, v_blk, do_blk):
    t = proj.shape[0]
    tq = _row_tile(t, SB_TQ)
    tk = min(SB_TK, tq)
    per = tq // tk

    def body(q_ref, k_ref, v_ref, o_ref, do_ref, dq_ref, dk_ref, dv_ref):
        i = pl.program_id(1)

        @pl.when(i == 0)
        def _():
            dk_ref[...] = jnp.zeros_like(dk_ref)
            dv_ref[...] = jnp.zeros_like(dv_ref)

        tri_s, tri_i = _tri(tk, True), _tri(tk, False)
        qh = _halves(q_ref[...] * QK_SCALE)
        doh = _halves(do_ref[...])
        deltas = [jnp.sum(x, axis=-1, keepdims=True) for x in _halves(do_ref[...].astype(F32) * o_ref[...])]

        def blocks(kb, carry, off):
            k0 = pl.multiple_of(kb * tk, tk)
            ks, vs = k_ref[pl.ds(k0, tk), :], v_ref[pl.ds(k0, tk), :]
            causal = None if off is None else _causal(tq, tk, off)
            lbm = [_sb_scores(qh[hh], ks, causal) for hh in range(2)]
            dws = [lax.dot_general(doh[hh], vs, NT, preferred_element_type=F32) for hh in range(2)]
            afters = [_suffix_sum(lbm[hh][1], tri_s, False) for hh in range(2)]
            wbs, es = [], []
            for hh in range(2):
                w = jnp.exp(lbm[hh][0] + afters[hh] + carry[3 * hh + 1])
                if causal is not None:
                    w = jnp.where(causal, w, 0.0)
                wbs.append(w.astype(BF16))
                es.append(wbs[hh].astype(F32) * dws[hh])
            sfx = [_suffix_sum(es[hh], tri_i, True) for hh in range(2)]
            dzs = []
            for hh in range(2):
                left = deltas[hh] - (sfx[hh] + carry[3 * hh + 2])
                sig = jnp.exp(lbm[hh][0])
                dz = es[hh] * (1.0 - sig) - left * sig
                if causal is not None:
                    dz = jnp.where(causal, dz, 0.0)
                dzs.append(dz.astype(BF16))
            dk_ref[pl.ds(k0, tk), :] += (lax.dot_general(dzs[0], qh[0], TN, preferred_element_type=F32)
                                         + lax.dot_general(dzs[1], qh[1], TN, preferred_element_type=F32))
            dv_ref[pl.ds(k0, tk), :] += (lax.dot_general(wbs[0], doh[0], TN, preferred_element_type=F32)
                                         + lax.dot_general(wbs[1], doh[1], TN, preferred_element_type=F32))
            out = []
            for hh in range(2):
                out += [carry[3 * hh] + jnp.dot(dzs[hh], ks, preferred_element_type=F32),
                        carry[3 * hh + 1] + jnp.sum(lbm[hh][1], axis=-1, keepdims=True),
                        carry[3 * hh + 2] + jnp.sum(es[hh], axis=-1, keepdims=True)]
            return tuple(out)

        def alive(carry):
            return jnp.maximum(jnp.max(carry[1]), jnp.max(carry[4])) > SB_DEAD

        zero = jnp.zeros((tq, 1), F32)
        carry = (jnp.zeros((tq, PAIR), F32), zero, zero) * 2
        for off in reversed(range(per)):
            carry = blocks(i * per + off, carry, off)

        def step(c):
            new = blocks(i * per - 1 - c[0], c[2:], None)
            return (c[0] + 1, alive(new)) + new

        out = lax.while_loop(lambda c: jnp.logical_and(c[0] < i * per, c[1]), step,
                             (jnp.int32(0), alive(carry)) + carry)
        dq_ref[...] = _merge(out[2], out[5]) * QK_SCALE

    tile = lambda blk: pl.BlockSpec((tq, PAIR), lambda p, i: (i, blk + p))
    whole = lambda blk: pl.BlockSpec((t, PAIR), lambda p, i: (0, blk + p))
    return pl.pallas_call(
        body, grid=(N_PAIRS, t // tq),
        in_specs=[tile(q_blk), whole(k_blk), whole(v_blk), tile(0), tile(do_blk)],
        out_specs=[tile(0), whole(0), whole(0)],
        out_shape=[_sds((t, N_PAIRS * PAIR), F32)] * 3,
        compiler_params=_params("parallel", "arbitrary"), name=name)(proj, proj, proj, o, do)


def _sigmoid(x):
    return 1.0 / (1.0 + jnp.exp(-x))


def gu_gap(f8):
    return -(-f8 // 128) * 128


def merge_gu(gate, up):
    f8 = gate.shape[-1]
    pad = jnp.zeros(gate.shape[:-1] + (gu_gap(f8) - f8,), gate.dtype)
    return jnp.concatenate([gate, pad, up], axis=-1)


def split_gu(gu, f8):
    return gu[..., :f8], gu[..., gu_gap(f8):]


def ffn_up(name, h, wgu, f8, layer):
    t, d = h.shape
    nb, _, _, fw = wgu.shape
    gap = gu_gap(f8)
    tm = _row_tile(t, 1024)

    def body(h_ref, w_ref, gu_ref, a_ref):
        r = jnp.dot(h_ref[...], w_ref[...], preferred_element_type=F32)
        gu_ref[...] = r.astype(BF16)
        g, u = r[:, :f8], r[:, gap:]
        a_ref[...] = (g * _sigmoid(g) * u).astype(BF16)

    return pl.pallas_call(
        body, grid=(t // tm, nb),
        in_specs=[pl.BlockSpec((tm, d), lambda i, k: (i, 0)),
                  pl.BlockSpec((None, None, d, fw), lambda i, k: (k, layer, 0, 0))],
        out_specs=[pl.BlockSpec((None, tm, fw), lambda i, k: (k, i, 0)),
                   pl.BlockSpec((None, tm, f8), lambda i, k: (k, i, 0))],
        out_shape=[_sds((nb, t, fw), BF16), _sds((nb, t, f8), BF16)],
        compiler_params=_params("parallel", "parallel"), name=name)(h, wgu)


def _mm_all_blocks(name, a, w, layer, dn, tm):
    nb, t, f = a.shape
    wshape = w.shape[2:]
    d = wshape[1] if dn == NN else wshape[0]
    tm = _row_tile(t, tm)

    def body(a_ref, w_ref, o_ref):
        acc = lax.dot_general(a_ref[0], w_ref[0], dn, preferred_element_type=F32)
        for k in range(1, nb):
            acc = acc + lax.dot_general(a_ref[k], w_ref[k], dn, preferred_element_type=F32)
        o_ref[...] = acc

    return pl.pallas_call(
        body, grid=(t // tm,),
        in_specs=[pl.BlockSpec((nb, tm, f), lambda i: (0, i, 0)),
                  pl.BlockSpec((nb, None) + wshape, lambda i: (0, layer, 0, 0))],
        out_specs=pl.BlockSpec((tm, d), lambda i: (i, 0)), out_shape=_sds((t, d), F32),
        compiler_params=_params("parallel"), name=name)(a, w)


def ffn_down(name, a, wd, layer):
    return _mm_all_blocks(name, a, wd, layer, NN, 512)


def ffn_bwd_act(name, dm, wd, gu, f8, layer):
    t, d = dm.shape
    nb, _, fw = gu.shape
    gap = gu_gap(f8)
    tm = _row_tile(t, 1024)

    def body(dm_ref, wd_ref, gu_ref, o_ref):
        da = lax.dot_general(dm_ref[...], wd_ref[...], NT, preferred_element_type=F32)
        gv = gu_ref[:, :f8].astype(F32)
        uv = gu_ref[:, gap:].astype(F32)
        sg = _sigmoid(gv)
        o_ref[...] = jnp.zeros_like(o_ref)
        o_ref[:, :f8] = (da * uv * sg * (1.0 + gv * (1.0 - sg))).astype(BF16)
        o_ref[:, gap:] = (da * gv * sg).astype(BF16)

    bspec = pl.BlockSpec((None, tm, fw), lambda i, k: (k, i, 0))
    return pl.pallas_call(
        body, grid=(t // tm, nb),
        in_specs=[pl.BlockSpec((tm, d), lambda i, k: (i, 0)),
                  pl.BlockSpec((None, None, f8, d), lambda i, k: (k, layer, 0, 0)), bspec],
        out_specs=bspec, out_shape=_sds((nb, t, fw), BF16),
        compiler_params=_params("parallel", "parallel"), name=name)(dm, wd, gu)


def ffn_bwd_dh(name, dgu, wgu, layer):
    return _mm_all_blocks(name, dgu, wgu, layer, NT, 512)


def ffn_dw_in(name, h, dact):
    t, d = h.shape
    nb, _, f8 = dact.shape
    tk = _row_tile(t, 2048)
    return _gmm(name, h, dact, grid=(nb, t // tk),
                a_blk=(tk, d), a_idx=lambda b, s: (s, 0), b_blk=(None, tk, f8), b_idx=lambda b, s: (b, s, 0),
                o_blk=(None, d, f8), o_idx=lambda b, s: (b, 0, 0), out_shape=(nb, d, f8), out_dtype=F32, dn=TN,
                acc_shape=(d, f8))


def ffn_dw_down(name, a, dm):
    nb, t, f8 = a.shape
    d = dm.shape[1]
    tk = _row_tile(t, 2048)
    return _gmm(name, a, dm, grid=(nb, t // tk),
                a_blk=(None, tk, f8), a_idx=lambda b, s: (b, s, 0), b_blk=(tk, d), b_idx=lambda b, s: (s, 0),
                o_blk=(None, f8, d), o_idx=lambda b, s: (b, 0, 0), out_shape=(nb, f8, d), out_dtype=F32, dn=TN,
                acc_shape=(f8, d))


GELU_C = math.sqrt(2.0 / math.pi)
GELU_A = 0.044715


def _gelu(x):
    return 0.5 * x * (1.0 + jnp.tanh(GELU_C * (x + GELU_A * x * x * x)))


def _gelu_grad(x):
    th = jnp.tanh(GELU_C * (x + GELU_A * x * x * x))
    return 0.5 * (1.0 + th) + 0.5 * x * (1.0 - th * th) * GELU_C * (1.0 + 3.0 * GELU_A * x * x)


def _neg_expm1(x):
    series = x * (1.0 + x * (0.5 + x * (1.0 / 6.0 + x * (1.0 / 24.0 + x * (1.0 / 120.0 + x * (1.0 / 720.0))))))
    return -jnp.where(x > -0.25, series, jnp.exp(x) - 1.0)


CONV_TR = 256
CONV_TAPS = 4
HALO = 8


def _shifted(ext, k, tr, back):
    if back:
        return pltpu.roll(ext, k, 0)[HALO:, :] if k else ext[HALO:, :]
    return pltpu.roll(ext, tr + HALO - k, 0)[:tr, :] if k else ext[:tr, :]


def conv4_fwd(name, src, cb, w, b):
    t, c = src.shape[0], w.shape[1]
    tr = _row_tile(t, CONV_TR)
    hb = tr // HALO

    def body(x_ref, h_ref, w_ref, b_ref, o_ref):
        i = pl.program_id(0)
        ext = jnp.concatenate([jnp.where(i == 0, 0.0, h_ref[...]), x_ref[...]], axis=0)
        acc = b_ref[...]
        for k in range(CONV_TAPS):
            acc = acc + w_ref[CONV_TAPS - 1 - k:CONV_TAPS - k, :] * _shifted(ext, k, tr, True)
        o_ref[...] = acc

    return pl.pallas_call(
        body, grid=(t // tr,),
        in_specs=[pl.BlockSpec((tr, c), lambda i: (i, cb)),
                  pl.BlockSpec((HALO, c), lambda i: (jnp.maximum(i * hb - 1, 0), cb)),
                  pl.BlockSpec((CONV_TAPS, c), lambda i: (0, 0)), pl.BlockSpec((1, c), lambda i: (0, 0))],
        out_specs=pl.BlockSpec((tr, c), lambda i: (i, 0)), out_shape=_sds((t, c), F32),
        compiler_params=_params("parallel"), name=name)(src, src, w, b)


def conv4_bwd_x(name, dy, w):
    t, c = dy.shape
    tr = _row_tile(t, CONV_TR)
    hb = tr // HALO
    last = t // tr - 1

    def body(y_ref, h_ref, w_ref, o_ref):
        i = pl.program_id(0)
        ext = jnp.concatenate([y_ref[...], jnp.where(i == last, 0.0, h_ref[...])], axis=0)
        acc = w_ref[CONV_TAPS - 1:CONV_TAPS, :] * y_ref[...]
        for k in range(1, CONV_TAPS):
            acc = acc + w_ref[CONV_TAPS - 1 - k:CONV_TAPS - k, :] * _shifted(ext, k, tr, False)
        o_ref[...] = acc

    return pl.pallas_call(
        body, grid=(t // tr,),
        in_specs=[pl.BlockSpec((tr, c), lambda i: (i, 0)),
                  pl.BlockSpec((HALO, c), lambda i: (jnp.minimum((i + 1) * hb, t // HALO - 1), 0)),
                  pl.BlockSpec((CONV_TAPS, c), lambda i: (0, 0))],
        out_specs=pl.BlockSpec((tr, c), lambda i: (i, 0)), out_shape=_sds((t, c), F32),
        compiler_params=_params("parallel"), name=name)(dy, dy, w)


def conv4_bwd_w(name, src, cb, dy):
    t, c = dy.shape
    tr = _row_tile(t, CONV_TR)
    hb = tr // HALO

    def body(x_ref, h_ref, dy_ref, dw_ref, db_ref):
        i = pl.program_id(0)

        @pl.when(i == 0)
        def _():
            dw_ref[...] = jnp.zeros_like(dw_ref)
            db_ref[...] = jnp.zeros_like(db_ref)

        ext = jnp.concatenate([jnp.where(i == 0, 0.0, h_ref[...]), x_ref[...]], axis=0)
        dyv = dy_ref[...]
        db_ref[...] += jnp.sum(dyv, axis=0, keepdims=True)
        for k in range(CONV_TAPS):
            dw_ref[CONV_TAPS - 1 - k:CONV_TAPS - k, :] += jnp.sum(dyv * _shifted(ext, k, tr, True), axis=0,
                                                                   keepdims=True)

    return pl.pallas_call(
        body, grid=(t // tr,),
        in_specs=[pl.BlockSpec((tr, c), lambda i: (i, cb)),
                  pl.BlockSpec((HALO, c), lambda i: (jnp.maximum(i * hb - 1, 0), cb)),
                  pl.BlockSpec((tr, c), lambda i: (i, 0))],
        out_specs=[pl.BlockSpec((CONV_TAPS, c), lambda i: (0, 0)), pl.BlockSpec((1, c), lambda i: (0, 0))],
        out_shape=[_sds((CONV_TAPS, c), F32), _sds((1, c), F32)],
        compiler_params=_params("arbitrary"), name=name)(src, src, dy)


def _rg_gate_values(xcv, wa_ref, wi_ref, ba_ref, bi_ref, lam_ref):
    xb = xcv.astype(BF16)
    r = _sigmoid(jnp.dot(xb, wa_ref[...], preferred_element_type=F32) + ba_ref[...])
    ig = _sigmoid(jnp.dot(xb, wi_ref[...], preferred_element_type=F32) + bi_ref[...])
    lam = lam_ref[...]
    sp = jnp.maximum(-lam, 0.0) + jnp.log(1.0 + jnp.exp(-jnp.abs(lam)))
    log_a = -LRU_C * r * sp
    a = jnp.exp(log_a)
    mult = jnp.sqrt(_neg_expm1(2.0 * log_a))
    return xb, r, ig, sp, a, mult


def rg_gates_fwd(name, xc, wa, wi, ba, bi, lam):
    t, c = xc.shape
    nb, cb, _ = wa.shape
    tm = _row_tile(t, 512)

    def body(xc_ref, wa_ref, wi_ref, ba_ref, bi_ref, lam_ref, a_ref, u_ref):
        xcv = xc_ref[...]
        _, _, ig, _, a, mult = _rg_gate_values(xcv, wa_ref, wi_ref, ba_ref, bi_ref, lam_ref)
        a_ref[...] = a
        u_ref[...] = mult * (ig * xcv)

    blk = pl.BlockSpec((tm, cb), lambda i, n: (i, n))
    wsp = pl.BlockSpec((None, cb, cb), lambda i, n: (n, 0, 0))
    vec = pl.BlockSpec((1, cb), lambda i, n: (0, n))
    return pl.pallas_call(
        body, grid=(t // tm, nb), in_specs=[blk, wsp, wsp, vec, vec, vec], out_specs=[blk, blk],
        out_shape=[_sds((t, c), F32)] * 2, compiler_params=_params("parallel", "parallel"), name=name,
    )(xc, wa, wi, ba, bi, lam)


def rg_gates_bwd(name, xc, gu, hprev, wa, wi, ba, bi, lam):
    t, c = xc.shape
    nb, cb, _ = wa.shape
    tm = _row_tile(t, 512)

    def body(xc_ref, gu_ref, hp_ref, wa_ref, wi_ref, ba_ref, bi_ref, lam_ref,
             dxc_ref, dwa_ref, dwi_ref, dba_ref, dbi_ref, dlam_ref):
        i = pl.program_id(1)

        @pl.when(i == 0)
        def _():
            for ref in (dwa_ref, dwi_ref, dba_ref, dbi_ref, dlam_ref):
                ref[...] = jnp.zeros_like(ref)

        xcv = xc_ref[...]
        xb, r, ig, sp, a, mult = _rg_gate_values(xcv, wa_ref, wi_ref, ba_ref, bi_ref, lam_ref)
        gv = gu_ref[...]
        d_ixc = gv * mult
        d_i = d_ixc * xcv
        d_mult = gv * ig * xcv
        d_a = gv * hp_ref[...] - d_mult * a / mult
        d_log_a = d_a * a
        d_r = d_log_a * (-LRU_C * sp)
        sig_neg_lam = 1.0 / (1.0 + jnp.exp(lam_ref[...]))
        dlam_ref[...] += jnp.sum(d_log_a * r, axis=0, keepdims=True) * (LRU_C * sig_neg_lam)
        dpa = d_r * r * (1.0 - r)
        dpi = d_i * ig * (1.0 - ig)
        dba_ref[...] += jnp.sum(dpa, axis=0, keepdims=True)
        dbi_ref[...] += jnp.sum(dpi, axis=0, keepdims=True)
        dpab, dpib = dpa.astype(BF16), dpi.astype(BF16)
        dxc_ref[...] = (d_ixc * ig + lax.dot_general(dpab, wa_ref[...], NT, preferred_element_type=F32)
                        + lax.dot_general(dpib, wi_ref[...], NT, preferred_element_type=F32))
        dwa_ref[...] += lax.dot_general(xb, dpab, TN, preferred_element_type=F32)
        dwi_ref[...] += lax.dot_general(xb, dpib, TN, preferred_element_type=F32)

    blk = pl.BlockSpec((tm, cb), lambda n, i: (i, n))
    wsp = pl.BlockSpec((None, cb, cb), lambda n, i: (n, 0, 0))
    vec = pl.BlockSpec((1, cb), lambda n, i: (0, n))
    return pl.pallas_call(
        body, grid=(nb, t // tm), in_specs=[blk, blk, blk, wsp, wsp, vec, vec, vec],
        out_specs=[blk, wsp, wsp, vec, vec, vec],
        out_shape=[_sds((t, c), F32), _sds((nb, cb, cb), F32), _sds((nb, cb, cb), F32),
                   _sds((1, c), F32), _sds((1, c), F32), _sds((1, c), F32)],
        compiler_params=_params("parallel", "arbitrary"), name=name)(xc, gu, hprev, wa, wi, ba, bi, lam)


SCAN_TS = 256
SCAN_TC = 512


def _tile_scan(a, b, reverse):
    ts = a.shape[0]
    row = lax.broadcasted_iota(jnp.int32, a.shape, 0)
    d = 1
    while d < ts:
        if reverse:
            inside = row < ts - d
            a_sh = jnp.where(inside, pltpu.roll(a, ts - d, 0), 1.0)
            b_sh = jnp.where(inside, pltpu.roll(b, ts - d, 0), 0.0)
        else:
            inside = row >= d
            a_sh = jnp.where(inside, pltpu.roll(a, d, 0), 1.0)
            b_sh = jnp.where(inside, pltpu.roll(b, d, 0), 0.0)
        b = b + a * b_sh
        a = a * a_sh
        d *= 2
    return a, b


def rg_scan_fwd(name, a, u, gate_pre):
    t, c = a.shape
    ts, tc = _row_tile(t, SCAN_TS), _row_tile(c, SCAN_TC)

    def body(a_ref, u_ref, g_ref, h_ref, z_ref, carry_ref):
        s = pl.program_id(1)

        @pl.when(s == 0)
        def _():
            carry_ref[...] = jnp.zeros_like(carry_ref)

        ac, bc = _tile_scan(a_ref[...], u_ref[...], False)
        h = bc + ac * carry_ref[0:1, :]
        h_ref[...] = h
        z_ref[...] = (h * _gelu(g_ref[...])).astype(BF16)
        carry_ref[0:1, :] = h[ts - 1:ts, :]

    blk = pl.BlockSpec((ts, tc), lambda j, s: (s, j))
    return pl.pallas_call(
        body, grid=(c // tc, t // ts), in_specs=[blk, blk, blk], out_specs=[blk, blk],
        out_shape=[_sds((t, c), F32), _sds((t, c), BF16)], scratch_shapes=[pltpu.VMEM((8, tc), F32)],
        compiler_params=_params("parallel", "arbitrary"), name=name)(a, u, gate_pre)


def rg_scan_bwd(name, a_next, hs, gate_pre, dz):
    t, c = hs.shape
    ts, tc = _row_tile(t, SCAN_TS), _row_tile(c, SCAN_TC)
    nt = t // ts

    def body(an_ref, h_ref, g_ref, dz_ref, gu_ref, dgate_ref, carry_ref):
        s = pl.program_id(1)

        @pl.when(s == 0)
        def _():
            carry_ref[...] = jnp.zeros_like(carry_ref)

        gate = g_ref[...]
        dzv = dz_ref[...]
        dgate_ref[...] = (dzv * h_ref[...] * _gelu_grad(gate)).astype(BF16)
        ac, bc = _tile_scan(an_ref[...], dzv * _gelu(gate), True)
        gu = bc + ac * carry_ref[0:1, :]
        gu_ref[...] = gu
        carry_ref[0:1, :] = gu[0:1, :]

    blk = pl.BlockSpec((ts, tc), lambda j, s: (nt - 1 - s, j))
    return pl.pallas_call(
        body, grid=(c // tc, nt), in_specs=[blk, blk, blk, blk], out_specs=[blk, blk],
        out_shape=[_sds((t, c), F32), _sds((t, c), BF16)], scratch_shapes=[pltpu.VMEM((8, tc), F32)],
        compiler_params=_params("parallel", "arbitrary"), name=name)(a_next, hs, gate_pre, dz)


def _shift_down(x, k):
    return jnp.pad(x, ((k, 0), (0, 0)))[:x.shape[0]] if k else x


def _shift_up(x, k):
    return jnp.pad(x, ((0, k), (0, 0)))[k:] if k else x


QA_BLK, KA_BLK, VA_BLK, QS_BLK, KS_BLK, VS_BLK = (g * N_PAIRS for g in range(6))


TOEP_W = 640
TOEP_FLAT = 320
TABLE_LOW = 193


def rel_bias_matrix(name, table):
    h = table.shape[0]
    diag = jnp.concatenate([jnp.repeat(table[:, 2 * REL_CLIP:], TOEP_FLAT, axis=1),
                            jnp.flip(table[:, TABLE_LOW:2 * REL_CLIP], axis=1),
                            jnp.zeros((h, 1), table.dtype)], axis=1)[:, None, :]

    def body(v_ref, o_ref):
        rows = jnp.broadcast_to(v_ref[...], (CHUNK, TOEP_W))
        o_ref[...] = pltpu.roll(rows, TOEP_W - (CHUNK - 1), 1, stride=1, stride_axis=0)

    out = pl.pallas_call(
        body, grid=(h,), in_specs=[pl.BlockSpec((None, 1, TOEP_W), lambda hh: (hh, 0, 0))],
        out_specs=pl.BlockSpec((None, CHUNK, TOEP_W), lambda hh: (hh, 0, 0)),
        out_shape=_sds((h, CHUNK, TOEP_W), F32), compiler_params=_params("parallel"), name=name)(diag)
    return out[:, :, :BAND]


def rel_bias_grad(name, dbias):
    h = dbias.shape[0]
    flipped = jnp.pad(jnp.flip(dbias, axis=1), ((0, 0), (0, 0), (0, TOEP_W - BAND)))

    def body(x_ref, o_ref):
        skew = pltpu.roll(x_ref[...], 0, 1, stride=1, stride_axis=0)
        col = jnp.sum(skew, axis=0, keepdims=True)
        lane = lax.broadcasted_iota(jnp.int32, col.shape, 1)
        flat = jnp.sum(jnp.where(lane < TOEP_FLAT, col, 0.0), axis=1, keepdims=True)
        o_ref[...] = jnp.where(lane == TOEP_W - 1, flat, col)

    out = pl.pallas_call(
        body, grid=(h,), in_specs=[pl.BlockSpec((None, CHUNK, TOEP_W), lambda hh: (hh, 0, 0))],
        out_specs=pl.BlockSpec((None, 1, TOEP_W), lambda hh: (hh, 0, 0)),
        out_shape=_sds((h, 1, TOEP_W), F32), compiler_params=_params("parallel"), name=name)(flipped)[:, 0, :]
    return jnp.concatenate([jnp.zeros((h, TABLE_LOW), F32), jnp.flip(out[:, TOEP_FLAT:TOEP_W - 1], axis=1),
                            out[:, TOEP_W - 1:]], axis=1)


def attn_layer_fwd(tag, x, w):
    h = rmsnorm_fwd(tag + "_norm", x, w["g_pre"])
    proj = mm_nn_wblk(tag + "_proj", h, w["w_in"], w["idx"], BF16)
    width = N_PAIRS * PAIR
    pad = lambda a: jnp.pad(a, ((PAD_KEYS, 0), (0, 0)))
    kap, vap = pad(proj[:, width:2 * width]), pad(proj[:, 2 * width:3 * width])
    bias = rel_bias_matrix(tag + "_bias", w["rel_bias"])
    oa, lse = attn_a_fwd(tag + "_a", proj, kap, vap, bias, QA_BLK)
    ob = sb_fwd(tag + "_sb", proj, QS_BLK, KS_BLK, VS_BLK)
    o = jnp.concatenate([oa, ob], axis=1).astype(BF16)
    m = mm_nn(tag + "_out", o, w["w_out"], F32)
    x1 = resid_norm_fwd(tag + "_res", x, m, w["g_post"])
    return x1, (x, h, proj, kap, vap, bias, oa, lse, ob, o, m)


def attn_layer_bwd(tag, dx1, saved, w):
    x, h, proj, kap, vap, bias, oa, lse, ob, o, m = saved
    dm, dg_post = norm_bwd(tag + "_dpost", dx1, m, w["g_post"], None, BF16)
    d_w_out = mm_tn(tag + "_dwout", o, dm, F32)
    do = mm_nt(tag + "_do", dm, w["w_out"], BF16)
    dqa, dkap, dvap, dbias = attn_a_bwd(tag + "_da", proj, kap, vap, bias, oa, lse, do, QA_BLK, 0)
    dqs, dks, dvs = sb_bwd(tag + "_dsb", proj, ob, do, QS_BLK, KS_BLK, VS_BLK, N_PAIRS)
    d_rel = rel_bias_grad(tag + "_dbias", dbias)
    dproj = jnp.concatenate([dqa, dkap[PAD_KEYS:], dvap[PAD_KEYS:], dqs, dks, dvs], axis=1).astype(BF16)
    d_w_in = mm_tn_oblk(tag + "_dwin", h, dproj, w["w_in"].shape[3], F32)
    dh = mm_nt_wblk(tag + "_dh", dproj, w["w_in"], w["idx"], F32)
    dx, dg_pre = norm_bwd(tag + "_dpre", dh, x, w["g_pre"], dx1, F32)
    return dx, dict(w_in=d_w_in, w_out=d_w_out, rel_bias=d_rel, g_pre=dg_pre, g_post=dg_post)


def rg_layer_fwd(tag, x, w):
    h = rmsnorm_fwd(tag + "_norm", x, w["g_pre"])
    proj = mm_nn_wblk(tag + "_proj", h, w["w_in"], w["idx"], F32)
    xc = conv4_fwd(tag + "_conv", proj, 1, w["conv_w"], w["conv_b"])
    a, u = rg_gates_fwd(tag + "_gates", xc, w["w_a"], w["w_i"], w["b_a"], w["b_i"], w["lam"])
    hs, z = rg_scan_fwd(tag + "_scan", a, u, proj)
    m = mm_nn(tag + "_out", z, w["w_out"], F32)
    x1 = resid_norm_fwd(tag + "_res", x, m, w["g_post"])
    return x1, (x, h, proj, xc, a, hs, z, m)


def rg_layer_bwd(tag, dx1, saved, w):
    x, h, proj, xc, a, hs, z, m = saved
    dm, dg_post = norm_bwd(tag + "_dpost", dx1, m, w["g_post"], None, BF16)
    d_w_out = mm_tn(tag + "_dwout", z, dm, F32)
    dz = mm_nt(tag + "_dz", dm, w["w_out"], F32)
    gu, dgate = rg_scan_bwd(tag + "_dscan", _shift_up(a, 1), hs, proj, dz)
    dxc, d_w_a, d_w_i, d_b_a, d_b_i, d_lam = rg_gates_bwd(
        tag + "_dgates", xc, gu, _shift_down(hs, 1), w["w_a"], w["w_i"], w["b_a"], w["b_i"], w["lam"])
    d_conv_w, d_conv_b = conv4_bwd_w(tag + "_dconvw", proj, 1, dxc)
    dxr = conv4_bwd_x(tag + "_dconv", dxc, w["conv_w"])
    dproj = jnp.concatenate([dgate, dxr.astype(BF16)], axis=1)
    d_w_in = mm_tn_oblk(tag + "_dwin", h, dproj, w["w_in"].shape[3], F32)
    dh = mm_nt_wblk(tag + "_dh", dproj, w["w_in"], w["idx"], F32)
    dx, dg_pre = norm_bwd(tag + "_dpre", dh, x, w["g_pre"], dx1, F32)
    return dx, dict(w_in=d_w_in, w_out=d_w_out, conv_w=d_conv_w, conv_b=d_conv_b, w_a=d_w_a, w_i=d_w_i,
                    b_a=d_b_a, b_i=d_b_i, lam=d_lam, g_pre=dg_pre, g_post=dg_post)


def ffn_layer_fwd(tag, x, w):
    h = rmsnorm_fwd(tag + "_norm", x, w["g_pre"])
    f8 = w["w_down"].shape[2]
    gu, a = ffn_up(tag + "_up", h, w["w_gu"], f8, w["idx"])
    f = ffn_down(tag + "_down", a, w["w_down"], w["idx"])
    x1 = resid_norm_fwd(tag + "_res", x, f, w["g_post"])
    return x1, (x, h, gu, a, f)


def ffn_layer_bwd(tag, dx1, saved, w):
    x, h, gu, a, f = saved
    f8 = w["w_down"].shape[2]
    dm, dg_post = norm_bwd(tag + "_dpost", dx1, f, w["g_post"], None, BF16)
    d_w_down = ffn_dw_down(tag + "_dwdown", a, dm)
    dgu = ffn_bwd_act(tag + "_dact", dm, w["w_down"], gu, f8, w["idx"])
    d_w_gu = ffn_dw_in(tag + "_dwgu", h, dgu)
    dh = ffn_bwd_dh(tag + "_dh", dgu, w["w_gu"], w["idx"])
    dx, dg_pre = norm_bwd(tag + "_dpre", dh, x, w["g_pre"], dx1, F32)
    return dx, dict(w_gu=d_w_gu, w_down=d_w_down, g_pre=dg_pre, g_post=dg_post)


def _place():
    return lax.axis_index("x"), lax.axis_index("y"), lax.axis_index("c")


def all_gather(name, blks):
    n = len(blks)

    def body(*refs):
        x_refs, out_refs = refs[:n], refs[n:2 * n]
        send_sems, recv_sems, local_sems = refs[2 * n:]
        x, y, cc = _place()
        me, sibling = (x, y, cc), (x, y, 1 - cc)
        chips = [(1 - x, y), (x, 1 - y), (1 - x, 1 - y)]

        def slot(a, px, py, pc):
            return out_refs[a].at[4 * px + 2 * py + pc]

        def copy(a, k, block, to, src=None):
            return pltpu.make_async_remote_copy(
                src_ref=slot(a, *block) if src is None else src, dst_ref=slot(a, *block),
                send_sem=send_sems.at[7 * a + k], recv_sem=recv_sems.at[7 * a + k], device_id=to, device_id_type=MESH)

        mine = [pltpu.make_async_copy(x_refs[a], slot(a, *me), local_sems.at[a]) for a in range(n)]
        first = []
        for a in range(n):
            mine[a].start()
            first.append(copy(a, 0, me, sibling, src=x_refs[a]))
            first += [copy(a, 1 + j, me, (*chip, cc), src=x_refs[a]) for j, chip in enumerate(chips)]
        for cp in first:
            cp.start()
        passed = []
        for j, chip in enumerate(chips):
            for a in range(n):
                copy(a, 1 + j, (*chip, cc), me).wait_recv()
                passed.append(copy(a, 4 + j, (*chip, cc), sibling))
                passed[-1].start()
        for a in range(n):
            copy(a, 0, sibling, me).wait_recv()
            for j, chip in enumerate(chips):
                copy(a, 4 + j, (*chip, 1 - cc), me).wait_recv()
        for cp in first + passed:
            cp.wait_send()
        for cp in mine:
            cp.wait()

    return pl.pallas_call(
        body, out_shape=[_sds((N_DEV,) + b.shape, b.dtype) for b in blks], in_specs=[ANY] * n, out_specs=[ANY] * n,
        scratch_shapes=[pltpu.SemaphoreType.DMA((7 * n,)), pltpu.SemaphoreType.DMA((7 * n,)),
                        pltpu.SemaphoreType.DMA((n,))],
        name=name)(*blks)


def exchange_pair(name, gs):
    n = len(gs)
    nchip = 4

    def body(*refs):
        g_refs, land_refs = refs[:n], refs[n:2 * n]
        send_sems, recv_sems = refs[2 * n:]
        x, y, cc = _place()
        copies = [pltpu.make_async_remote_copy(
            src_ref=g_refs[a].at[j, 1 - cc], dst_ref=land_refs[a].at[j], send_sem=send_sems.at[nchip * a + j],
            recv_sem=recv_sems.at[nchip * a + j], device_id=(x, y, 1 - cc), device_id_type=MESH)
            for a in range(n) for j in range(nchip)]
        for cp in copies:
            cp.start()
        for cp in copies:
            cp.wait()

    return pl.pallas_call(
        body, out_shape=[_sds((nchip,) + g.shape[2:], g.dtype) for g in gs], in_specs=[ANY] * n, out_specs=[ANY] * n,
        scratch_shapes=[pltpu.SemaphoreType.DMA((nchip * n,)), pltpu.SemaphoreType.DMA((nchip * n,))],
        name=name)(*gs)


def pair_sum(name, g, land, core, out_dtype):
    nchip, _, r, c = g.shape
    tr = _row_tile(r, 128)

    def body(core_ref, g_ref, l_ref, o_ref):
        o_ref[...] = (g_ref[...] + l_ref[...]).astype(o_ref.dtype)

    return pl.pallas_call(
        body,
        grid_spec=pltpu.PrefetchScalarGridSpec(
            num_scalar_prefetch=1, grid=(nchip, r // tr),
            in_specs=[pl.BlockSpec((None, None, tr, c), lambda j, i, core_ref: (j, core_ref[0], i, 0)),
                      pl.BlockSpec((None, tr, c), lambda j, i, core_ref: (j, i, 0))],
            out_specs=pl.BlockSpec((None, tr, c), lambda j, i, core_ref: (j, i, 0))),
        out_shape=_sds((nchip, r, c), out_dtype), compiler_params=_params("parallel", "parallel"), name=name,
    )(core, g, land)


def exchange_chips(name, ps):
    n = len(ps)

    def body(*refs):
        p_refs, land_refs = refs[:n], refs[n:2 * n]
        send_sems, recv_sems, local_sems = refs[2 * n:]
        x, y, cc = _place()
        mine = 2 * x + y
        chips = [(1 - x, y), (x, 1 - y), (1 - x, 1 - y)]
        own = [pltpu.make_async_copy(p_refs[a].at[mine], land_refs[a].at[mine], local_sems.at[a]) for a in range(n)]
        for cp in own:
            cp.start()
        sends = [pltpu.make_async_remote_copy(
            src_ref=p_refs[a].at[2 * px + py], dst_ref=land_refs[a].at[mine], send_sem=send_sems.at[3 * a + k],
            recv_sem=recv_sems.at[3 * a + k], device_id=(px, py, cc), device_id_type=MESH)
            for a in range(n) for k, (px, py) in enumerate(chips)]
        for cp in sends:
            cp.start()
        for a in range(n):
            for k, (px, py) in enumerate(chips):
                pltpu.make_async_remote_copy(
                    src_ref=p_refs[a].at[mine], dst_ref=land_refs[a].at[2 * px + py], send_sem=send_sems.at[3 * a + k],
                    recv_sem=recv_sems.at[3 * a + k], device_id=(px, py, cc), device_id_type=MESH).wait_recv()
        for cp in sends:
            cp.wait_send()
        for cp in own:
            cp.wait()

    return pl.pallas_call(
        body, out_shape=[_sds(p.shape, p.dtype) for p in ps], in_specs=[ANY] * n, out_specs=[ANY] * n,
        scratch_shapes=[pltpu.SemaphoreType.DMA((3 * n,)), pltpu.SemaphoreType.DMA((3 * n,)),
                        pltpu.SemaphoreType.DMA((n,))],
        name=name)(*ps)


def adamw(name, parts, w, m, v):
    npart, r, c = parts.shape
    tr = _row_tile(r, 128)
    c1 = 1.0 / (1.0 - ADAM_B1 ** ADAM_STEP)
    c2 = 1.0 / (1.0 - ADAM_B2 ** ADAM_STEP)

    def body(p_ref, w_ref, m_ref, v_ref, g_ref, d_ref, nm_ref, nv_ref):
        g = p_ref[0].astype(F32)
        for j in range(1, npart):
            g = g + p_ref[j].astype(F32)
        nm = ADAM_B1 * m_ref[...] + (1.0 - ADAM_B1) * g
        nv = ADAM_B2 * v_ref[...] + (1.0 - ADAM_B2) * (g * g)
        g_ref[...] = g
        nm_ref[...] = nm
        nv_ref[...] = nv
        d_ref[...] = -ADAM_LR * ((nm * c1) / (jnp.sqrt(nv * c2) + ADAM_EPS) + ADAM_WD * w_ref[...])

    row = pl.BlockSpec((tr, c), lambda i: (i, 0))
    return pl.pallas_call(
        body, grid=(r // tr,), in_specs=[pl.BlockSpec((npart, tr, c), lambda i: (0, i, 0)), row, row, row],
        out_specs=[row] * 4, out_shape=[_sds((r, c), F32)] * 4, compiler_params=_params("parallel"), name=name,
    )(parts, w, m, v)


def _pack(arrays, dtype, row_multiple):
    flat = jnp.concatenate([a.astype(dtype).reshape(-1) for a in arrays])
    per = row_multiple * LANES
    total = -(-flat.shape[0] // per) * per
    return jnp.pad(flat, (0, total - flat.shape[0])).reshape(total // LANES, LANES)


def _pack_blocked(arrays, dtype, row_multiple):
    flat = jnp.concatenate([a.astype(dtype).reshape(N_DEV, -1) for a in arrays], axis=1)
    per = row_multiple * LANES
    total = -(-flat.shape[1] // per) * per
    return jnp.pad(flat, ((0, 0), (0, total - flat.shape[1]))).reshape(N_DEV, total // LANES, LANES)


def _unpack(buf, shapes, lead=()):
    flat = buf.reshape(lead + (-1,))
    out, off = [], 0
    for s in shapes:
        n = math.prod(s)
        out.append(flat[..., off:off + n].reshape(lead + tuple(s)))
        off += n
    return out


def _to_blocked(full, ax):
    s = full.shape
    return jnp.moveaxis(full.reshape(s[:ax] + (N_DEV, s[ax] // N_DEV) + s[ax + 1:]), ax, 0)


def _from_blocked(blk, ax):
    moved = jnp.moveaxis(blk, 0, ax)
    s = moved.shape
    return moved.reshape(s[:ax] + (s[ax] * s[ax + 1],) + s[ax + 2:])


SMALL = ("rg_conv_w", "rg_conv_b", "rg_b_a", "rg_b_i", "rg_lambda")
GU = "ffn_w_gu"
BIG = ("attn_w_in", "attn_w_out", "rg_w_in", "rg_w_a", "rg_w_i", "rg_w_out", GU, "ffn_w_down")


def kernel(x, attn_w_in, attn_rel_bias, attn_w_out, rg_w_in, rg_conv_w, rg_conv_b, rg_w_a, rg_b_a, rg_w_i, rg_b_i, rg_lambda, rg_w_out, norm_mix_pre, norm_mix_post, norm_ffn_pre, norm_ffn_post, ffn_w_gate, ffn_w_up, ffn_w_down, loss_target, m_attn_w_in, m_attn_rel_bias, m_attn_w_out, m_rg_w_in, m_rg_conv_w, m_rg_conv_b, m_rg_w_a, m_rg_b_a, m_rg_w_i, m_rg_b_i, m_rg_lambda, m_rg_w_out, m_norm_mix_pre, m_norm_mix_post, m_norm_ffn_pre, m_norm_ffn_post, m_ffn_w_gate, m_ffn_w_up, m_ffn_w_down, v_attn_w_in, v_attn_rel_bias, v_attn_w_out, v_rg_w_in, v_rg_conv_w, v_rg_conv_b, v_rg_w_a, v_rg_b_a, v_rg_w_i, v_rg_b_i, v_rg_lambda, v_rg_w_out, v_norm_mix_pre, v_norm_mix_post, v_norm_ffn_pre, v_norm_ffn_post, v_ffn_w_gate, v_ffn_w_up, v_ffn_w_down):
    w_loc = dict(attn_w_in=attn_w_in, attn_rel_bias=attn_rel_bias, attn_w_out=attn_w_out, rg_w_in=rg_w_in,
                 rg_conv_w=rg_conv_w, rg_conv_b=rg_conv_b, rg_w_a=rg_w_a, rg_b_a=rg_b_a, rg_w_i=rg_w_i, rg_b_i=rg_b_i,
                 rg_lambda=rg_lambda, rg_w_out=rg_w_out, norm_mix_pre=norm_mix_pre, norm_mix_post=norm_mix_post,
                 norm_ffn_pre=norm_ffn_pre, norm_ffn_post=norm_ffn_post, ffn_w_gate=ffn_w_gate, ffn_w_up=ffn_w_up,
                 ffn_w_down=ffn_w_down)
    m_loc = dict(attn_w_in=m_attn_w_in, attn_rel_bias=m_attn_rel_bias, attn_w_out=m_attn_w_out, rg_w_in=m_rg_w_in,
                 rg_conv_w=m_rg_conv_w, rg_conv_b=m_rg_conv_b, rg_w_a=m_rg_w_a, rg_b_a=m_rg_b_a, rg_w_i=m_rg_w_i,
                 rg_b_i=m_rg_b_i, rg_lambda=m_rg_lambda, rg_w_out=m_rg_w_out, norm_mix_pre=m_norm_mix_pre,
                 norm_mix_post=m_norm_mix_post, norm_ffn_pre=m_norm_ffn_pre, norm_ffn_post=m_norm_ffn_post,
                 ffn_w_gate=m_ffn_w_gate, ffn_w_up=m_ffn_w_up, ffn_w_down=m_ffn_w_down)
    v_loc = dict(attn_w_in=v_attn_w_in, attn_rel_bias=v_attn_rel_bias, attn_w_out=v_attn_w_out, rg_w_in=v_rg_w_in,
                 rg_conv_w=v_rg_conv_w, rg_conv_b=v_rg_conv_b, rg_w_a=v_rg_w_a, rg_b_a=v_rg_b_a, rg_w_i=v_rg_w_i,
                 rg_b_i=v_rg_b_i, rg_lambda=v_rg_lambda, rg_w_out=v_rg_w_out, norm_mix_pre=v_norm_mix_pre,
                 norm_mix_post=v_norm_mix_post, norm_ffn_pre=v_norm_ffn_pre, norm_ffn_post=v_norm_ffn_post,
                 ffn_w_gate=v_ffn_w_gate, ffn_w_up=v_ffn_w_up, ffn_w_down=v_ffn_w_down)
    axis_of = dict(SHARDED)
    xt, target = x[0], loss_target[0]
    d_model = xt.shape[1]
    rows2d = lambda a: a.reshape(-1, a.shape[-1])
    small_shapes = [w_loc[n].shape for n in SMALL]
    f8 = ffn_w_gate.shape[-1]
    for d in (w_loc, m_loc, v_loc):
        d[GU] = merge_gu(d["ffn_w_gate"], d["ffn_w_up"])

    gathered = all_gather("gather_weights", [rows2d(w_loc[n]).astype(BF16) for n in BIG]
                          + [_pack([w_loc[n] for n in SMALL], F32, 8)])
    blocked = {n: g.reshape((N_DEV,) + w_loc[n].shape) for n, g in zip(BIG, gathered)}
    blocked.update(zip(SMALL, _unpack(gathered[-1], small_shapes, (N_DEV,))))
    full = {n: _from_blocked(blocked[n], axis_of[n]) for n in SMALL}
    row = lambda a: a.reshape(1, -1).astype(F32)
    square = lambda rows8: rows8.reshape(-1, rows8.shape[-1])
    gates = lambda g: jnp.swapaxes(g, 0, 1).reshape(LRU_BLOCKS, -1, g.shape[-1])

    def layer_weights(layer):
        j = layer // 2
        norms = dict(g_pre=row(norm_mix_pre[layer]), g_post=row(norm_mix_post[layer]), idx=j)
        if layer % 2 == 0:
            mix = dict(w_in=blocked["attn_w_in"], w_out=square(blocked["attn_w_out"][:, j]),
                       rel_bias=attn_rel_bias[j], **norms)
        else:
            mix = dict(w_in=blocked["rg_w_in"], w_out=square(blocked["rg_w_out"][:, j]),
                       conv_w=full["rg_conv_w"][j][:, 0, :], conv_b=row(full["rg_conv_b"][j]),
                       w_a=gates(blocked["rg_w_a"][:, j]), w_i=gates(blocked["rg_w_i"][:, j]),
                       b_a=row(full["rg_b_a"][j]), b_i=row(full["rg_b_i"][j]), lam=row(full["rg_lambda"][j]), **norms)
        ffn = dict(w_gu=blocked[GU], w_down=blocked["ffn_w_down"], idx=layer,
                   g_pre=row(norm_ffn_pre[layer]), g_post=row(norm_ffn_post[layer]))
        return mix, ffn

    act, tape = xt, []
    for layer in range(DEPTH):
        mix_w, ffn_w = layer_weights(layer)
        mixer_fwd = attn_layer_fwd if layer % 2 == 0 else rg_layer_fwd
        act, saved_mix = mixer_fwd(f"l{layer}_mix", act, mix_w)
        act, saved_ffn = ffn_layer_fwd(f"l{layer}_ffn", act, ffn_w)
        tape.append((mix_w, ffn_w, saved_mix, saved_ffn))
    dact, sq = loss_grad("loss", act, target)
    loss = lax.psum(0.5 * jnp.sum(sq) / d_model, ("x", "y", "c"))

    grads = {}
    for layer in reversed(range(DEPTH)):
        mix_w, ffn_w, saved_mix, saved_ffn = tape[layer]
        dact, grads[("ffn", layer)] = ffn_layer_bwd(f"l{layer}_ffn", dact, saved_ffn, ffn_w)
        mixer_bwd = attn_layer_bwd if layer % 2 == 0 else rg_layer_bwd
        dact, grads[("mix", layer)] = mixer_bwd(f"l{layer}_mix", dact, saved_mix, mix_w)
    attn_g = [grads[("mix", l)] for l in range(0, DEPTH, 2)]
    rg_g = [grads[("mix", l)] for l in range(1, DEPTH, 2)]
    ffn_g = [grads[("ffn", l)] for l in range(DEPTH)]
    stack = lambda gs, key: jnp.stack([g[key] for g in gs])
    by_owner = lambda gs, key, f: jnp.stack([f(g[key]) for g in gs], axis=1)
    rows8 = lambda a: a.reshape(N_DEV, -1, a.shape[-1])
    ungates = lambda a: jnp.swapaxes(a.reshape(LRU_BLOCKS, N_DEV, -1, a.shape[-1]), 0, 1)
    same = lambda a: a
    blocked_g = dict(
        attn_w_in=by_owner(attn_g, "w_in", same), attn_w_out=by_owner(attn_g, "w_out", rows8),
        rg_w_in=by_owner(rg_g, "w_in", same), rg_w_out=by_owner(rg_g, "w_out", rows8),
        rg_w_a=by_owner(rg_g, "w_a", ungates), rg_w_i=by_owner(rg_g, "w_i", ungates),
        **{GU: by_owner(ffn_g, "w_gu", same)},
        ffn_w_down=by_owner(ffn_g, "w_down", same))
    contrib = dict(
        attn_rel_bias=stack(attn_g, "rel_bias"), rg_conv_w=stack(rg_g, "conv_w")[:, :, None, :],
        rg_conv_b=stack(rg_g, "conv_b")[:, 0], rg_b_a=stack(rg_g, "b_a").reshape(rg_b_a.shape[0], LRU_BLOCKS, -1),
        rg_b_i=stack(rg_g, "b_i").reshape(rg_b_i.shape[0], LRU_BLOCKS, -1), rg_lambda=stack(rg_g, "lam")[:, 0],
        norm_mix_pre=jnp.concatenate([grads[("mix", l)]["g_pre"] for l in range(DEPTH)]),
        norm_mix_post=jnp.concatenate([grads[("mix", l)]["g_post"] for l in range(DEPTH)]),
        norm_ffn_pre=jnp.concatenate([g["g_pre"] for g in ffn_g]),
        norm_ffn_post=jnp.concatenate([g["g_post"] for g in ffn_g]),
    )
    small_g = _pack_blocked([_to_blocked(contrib[n], axis_of[n]) for n in SMALL], F32, 8)

    slabs = [blocked_g[n].reshape(4, 2, -1, blocked_g[n].shape[-1]) for n in BIG] + [small_g.reshape(4, 2, -1, LANES)]
    core = lax.axis_index("c").astype(jnp.int32).reshape(1)
    from_sibling = exchange_pair("rs_pair", slabs)
    pairs = [pair_sum(f"rs_pair_sum_{i}", g, l, core, BF16 if i < len(BIG) else F32)
             for i, (g, l) in enumerate(zip(slabs, from_sibling))]
    by_chip = exchange_chips("rs_chips", pairs)
    result = {}
    kinds = ("grad", "delta", "new_m", "new_v")
    for n, parts in zip(BIG, by_chip):
        outs = adamw("adamw_" + n, parts, *[rows2d(d[n]) for d in (w_loc, m_loc, v_loc)])
        for kind, a in zip(kinds, outs):
            result[(kind, n)] = a.reshape(w_loc[n].shape)
    for kind in kinds:
        result[(kind, "ffn_w_gate")], result[(kind, "ffn_w_up")] = split_gu(result.pop((kind, GU)), f8)
    outs = adamw("adamw_small", by_chip[-1], *[_pack([d[n] for n in SMALL], F32, 8) for d in (w_loc, m_loc, v_loc)])
    for kind, buf in zip(kinds, outs):
        result.update({(kind, n): a for n, a in zip(SMALL, _unpack(buf, small_shapes))})
    rep_shapes = [w_loc[n].shape for n in REPLICATED]
    rep_parts, = all_gather("gather_rep_grads", [_pack([contrib[n] for n in REPLICATED], F32, 8)])
    outs = adamw("adamw_replicated", rep_parts, *[_pack([d[n] for n in REPLICATED], F32, 8)
                                                  for d in (w_loc, m_loc, v_loc)])
    for kind, buf in zip(kinds, outs):
        result.update({(kind, n): a for n, a in zip(REPLICATED, _unpack(buf, rep_shapes))})
    return (loss, dact[None], *[result[(kind, n)] for kind in kinds for n in WEIGHTS])
```

```python
import functools
import math

import jax
import jax.numpy as jnp
from jax import lax
from jax.experimental import pallas as pl
from jax.experimental.pallas import tpu as pltpu

F32 = jnp.float32
BF16 = jnp.bfloat16

N_DEV = 8
DEPTH = 4
CHUNK = 64
N_LEFT = 8
BAND = (N_LEFT + 1) * CHUNK
PAD_KEYS = N_LEFT * CHUNK
HEAD_DIM = 64
N_HEADS = 8
REL_CLIP = 256
LRU_BLOCKS = 4
LRU_C = 8.0
RMS_EPS = 1e-6
QK_SCALE = HEAD_DIM ** -0.5

ADAM_LR = 0.001
ADAM_B1 = 0.9
ADAM_B2 = 0.999
ADAM_EPS = 1e-08
ADAM_WD = 0.01
ADAM_STEP = 10

LANES = 1024
V7X_VMEM_LIMIT = 56 * 1024 * 1024

MESH = pl.DeviceIdType.MESH
ANY = pl.BlockSpec(memory_space=pl.ANY)

SHARDED = (
    ("attn_w_in", 2), ("attn_w_out", 1), ("rg_w_in", 2), ("rg_conv_w", 3), ("rg_conv_b", 1),
    ("rg_w_a", 2), ("rg_b_a", 2), ("rg_w_i", 2), ("rg_b_i", 2), ("rg_lambda", 1), ("rg_w_out", 1),
    ("ffn_w_gate", 2), ("ffn_w_up", 2), ("ffn_w_down", 1),
)
REPLICATED = ("attn_rel_bias", "norm_mix_pre", "norm_mix_post", "norm_ffn_pre", "norm_ffn_post")
WEIGHTS = ("attn_w_in", "attn_rel_bias", "attn_w_out", "rg_w_in", "rg_conv_w", "rg_conv_b", "rg_w_a", "rg_b_a",
           "rg_w_i", "rg_b_i", "rg_lambda", "rg_w_out", "norm_mix_pre", "norm_mix_post", "norm_ffn_pre",
           "norm_ffn_post", "ffn_w_gate", "ffn_w_up", "ffn_w_down")


def _params(*dims):
    return pltpu.CompilerParams(dimension_semantics=dims or None, vmem_limit_bytes=V7X_VMEM_LIMIT)


def _sds(shape, dtype):
    return jax.ShapeDtypeStruct(tuple(shape), dtype)


def _row_tile(n, pref):
    t = min(n, pref)
    assert n % t == 0, (n, pref)
    return t


def _divisor_tile(n, limit, multiple):
    if n <= limit:
        return n
    best = max(t for t in range(multiple, limit + 1, multiple) if n % t == 0)
    return best


NN = (((1,), (0,)), ((), ()))
NT = (((1,), (1,)), ((), ()))
TN = (((0,), (0,)), ((), ()))


def _gmm(name, a, b, *, grid, a_blk, a_idx, b_blk, b_idx, o_blk, o_idx, out_shape, out_dtype, dn, acc_shape):
    nk = grid[-1]
    kax = len(grid) - 1

    def body(a_ref, b_ref, o_ref, acc_ref):
        part = lax.dot_general(a_ref[...], b_ref[...], dn, preferred_element_type=F32)
        if nk == 1:
            o_ref[...] = part.astype(o_ref.dtype)
            return
        k = pl.program_id(kax)

        @pl.when(k == 0)
        def _():
            acc_ref[...] = part

        @pl.when(k > 0)
        def _():
            acc_ref[...] += part

        @pl.when(k == nk - 1)
        def _():
            o_ref[...] = acc_ref[...].astype(o_ref.dtype)

    return pl.pallas_call(
        body, grid=grid,
        in_specs=[pl.BlockSpec(a_blk, a_idx), pl.BlockSpec(b_blk, b_idx)],
        out_specs=pl.BlockSpec(o_blk, o_idx),
        out_shape=_sds(out_shape, out_dtype),
        scratch_shapes=[pltpu.VMEM(acc_shape, F32)],
        compiler_params=_params(*(["parallel"] * kax + ["arbitrary"])),
        name=name,
    )(a, b)


def mm_nn(name, a, b, out_dtype, tm=1024, tn=512, tk=1024):
    (m, k), (_, n) = a.shape, b.shape
    tm, tn, tk = _row_tile(m, tm), _row_tile(n, tn), _row_tile(k, tk)
    return _gmm(name, a, b, grid=(m // tm, n // tn, k // tk),
                a_blk=(tm, tk), a_idx=lambda i, j, kk: (i, kk), b_blk=(tk, tn), b_idx=lambda i, j, kk: (kk, j),
                o_blk=(tm, tn), o_idx=lambda i, j, kk: (i, j), out_shape=(m, n), out_dtype=out_dtype, dn=NN,
                acc_shape=(tm, tn))


def mm_nt(name, a, b, out_dtype, tm=1024, tn=512, tk=1024):
    (m, k), (n, _) = a.shape, b.shape
    tm, tn, tk = _row_tile(m, tm), _row_tile(n, tn), _row_tile(k, tk)
    return _gmm(name, a, b, grid=(m // tm, n // tn, k // tk),
                a_blk=(tm, tk), a_idx=lambda i, j, kk: (i, kk), b_blk=(tn, tk), b_idx=lambda i, j, kk: (j, kk),
                o_blk=(tm, tn), o_idx=lambda i, j, kk: (i, j), out_shape=(m, n), out_dtype=out_dtype, dn=NT,
                acc_shape=(tm, tn))


def mm_tn(name, a, b, out_dtype, tm=512, tn=512, tk=1024):
    (k, m), (_, n) = a.shape, b.shape
    tm, tn, tk = _row_tile(m, tm), _row_tile(n, tn), _row_tile(k, tk)
    return _gmm(name, a, b, grid=(m // tm, n // tn, k // tk),
                a_blk=(tk, tm), a_idx=lambda i, j, kk: (kk, i), b_blk=(tk, tn), b_idx=lambda i, j, kk: (kk, j),
                o_blk=(tm, tn), o_idx=lambda i, j, kk: (i, j), out_shape=(m, n), out_dtype=out_dtype, dn=TN,
                acc_shape=(tm, tn))


def mm_nn_wblk(name, a, wb, layer, out_dtype, tm=1024, tk=1024):
    (m, k), (nb, _, _, n8) = a.shape, wb.shape
    tm, tk = _row_tile(m, tm), _row_tile(k, tk)
    return _gmm(name, a, wb, grid=(m // tm, nb, k // tk),
                a_blk=(tm, tk), a_idx=lambda i, j, kk: (i, kk),
                b_blk=(None, None, tk, n8), b_idx=lambda i, j, kk: (j, layer, kk, 0),
                o_blk=(tm, n8), o_idx=lambda i, j, kk: (i, j), out_shape=(m, nb * n8), out_dtype=out_dtype, dn=NN,
                acc_shape=(tm, n8))


def mm_nt_wblk(name, a, wb, layer, out_dtype, tm=1024, tn=512):
    m = a.shape[0]
    nb, _, k, n8 = wb.shape
    tm, tn = _row_tile(m, tm), _row_tile(k, tn)

    def body(a_ref, b_ref, o_ref):
        acc = lax.dot_general(a_ref[:, 0:n8], b_ref[0], NT, preferred_element_type=F32)
        for j in range(1, nb):
            acc = acc + lax.dot_general(a_ref[:, j * n8:(j + 1) * n8], b_ref[j], NT, preferred_element_type=F32)
        o_ref[...] = acc.astype(o_ref.dtype)

    return pl.pallas_call(
        body, grid=(m // tm, k // tn),
        in_specs=[pl.BlockSpec((tm, nb * n8), lambda i, j: (i, 0)),
                  pl.BlockSpec((nb, None, tn, n8), lambda i, j: (0, layer, j, 0))],
        out_specs=pl.BlockSpec((tm, tn), lambda i, j: (i, j)), out_shape=_sds((m, k), out_dtype),
        compiler_params=_params("parallel", "parallel"), name=name)(a, wb)


def mm_tn_oblk(name, a, b, n8, out_dtype, tk=2048):
    (t, k), nb = a.shape, b.shape[1] // n8
    tk = _row_tile(t, tk)
    return _gmm(name, a, b, grid=(nb, t // tk),
                a_blk=(tk, k), a_idx=lambda j, s: (s, 0), b_blk=(tk, n8), b_idx=lambda j, s: (s, j),
                o_blk=(None, k, n8), o_idx=lambda j, s: (j, 0, 0), out_shape=(nb, k, n8), out_dtype=out_dtype, dn=TN,
                acc_shape=(k, n8))


def rmsnorm_fwd(name, x, g):
    t, d = x.shape
    tr = _row_tile(t, 512)

    def body(x_ref, g_ref, o_ref):
        xv = x_ref[...]
        r = lax.rsqrt(jnp.mean(xv * xv, axis=-1, keepdims=True) + RMS_EPS)
        o_ref[...] = (xv * r * g_ref[...]).astype(o_ref.dtype)

    return pl.pallas_call(
        body, grid=(t // tr,),
        in_specs=[pl.BlockSpec((tr, d), lambda i: (i, 0)), pl.BlockSpec((1, d), lambda i: (0, 0))],
        out_specs=pl.BlockSpec((tr, d), lambda i: (i, 0)),
        out_shape=_sds((t, d), BF16), compiler_params=_params("parallel"), name=name)(x, g)


def resid_norm_fwd(name, x, m, g, g_next):
    t, d = x.shape
    tr = _row_tile(t, 512)
    chained = g_next is not None

    def body(*refs):
        x_ref, m_ref, g_ref = refs[:3]
        mv = m_ref[...]
        r = lax.rsqrt(jnp.mean(mv * mv, axis=-1, keepdims=True) + RMS_EPS)
        x1 = x_ref[...] + mv * r * g_ref[...]
        if chained:
            gn_ref, o_ref, h_ref = refs[3:]
            r1 = lax.rsqrt(jnp.mean(x1 * x1, axis=-1, keepdims=True) + RMS_EPS)
            h_ref[...] = (x1 * r1 * gn_ref[...]).astype(BF16)
        else:
            o_ref, = refs[3:]
        o_ref[...] = x1

    row = pl.BlockSpec((tr, d), lambda i: (i, 0))
    vec = pl.BlockSpec((1, d), lambda i: (0, 0))
    out = pl.pallas_call(
        body, grid=(t // tr,),
        in_specs=[row, row, vec] + ([vec] if chained else []),
        out_specs=[row, row] if chained else [row],
        out_shape=[_sds((t, d), F32)] + ([_sds((t, d), BF16)] if chained else []),
        compiler_params=_params("parallel"), name=name)(*([x, m, g] + ([g_next] if chained else [])))
    return (out[0], out[1]) if chained else (out[0], None)


def norm_bwd(name, dy, x, g, resid, out_dtype):
    t, d = x.shape
    tr = _row_tile(t, 512)
    has_res = resid is not None

    def body(*refs):
        if has_res:
            dy_ref, x_ref, g_ref, r_ref, dx_ref, dg_ref = refs
        else:
            dy_ref, x_ref, g_ref, dx_ref, dg_ref = refs
        i = pl.program_id(0)
        xv = x_ref[...]
        dyv = dy_ref[...].astype(F32)
        r = lax.rsqrt(jnp.mean(xv * xv, axis=-1, keepdims=True) + RMS_EPS)
        xh = xv * r
        dxh = dyv * g_ref[...]
        dx = r * (dxh - xh * jnp.mean(dxh * xh, axis=-1, keepdims=True))
        if has_res:
            dx = dx + r_ref[...]
        dx_ref[...] = dx.astype(dx_ref.dtype)
        part = jnp.sum(dyv * xh, axis=0, keepdims=True)

        @pl.when(i == 0)
        def _():
            dg_ref[...] = part

        @pl.when(i > 0)
        def _():
            dg_ref[...] += part

    row = pl.BlockSpec((tr, d), lambda i: (i, 0))
    vec = pl.BlockSpec((1, d), lambda i: (0, 0))
    ins = [dy, x, g] + ([resid] if has_res else [])
    return pl.pallas_call(
        body, grid=(t // tr,),
        in_specs=[row, row, vec] + ([row] if has_res else []),
        out_specs=[row, vec],
        out_shape=[_sds((t, d), out_dtype), _sds((1, d), F32)],
        compiler_params=_params("arbitrary"), name=name)(*ins)


def loss_grad(name, y, target):
    t, d = y.shape
    tr = _row_tile(t, 512)

    def body(y_ref, t_ref, dy_ref, s_ref):
        i = pl.program_id(0)
        err = y_ref[...] - t_ref[...]
        dy_ref[...] = err * (1.0 / d)
        part = jnp.sum(err * err, axis=0, keepdims=True)

        @pl.when(i == 0)
        def _():
            s_ref[...] = part

        @pl.when(i > 0)
        def _():
            s_ref[...] += part

    row = pl.BlockSpec((tr, d), lambda i: (i, 0))
    vec = pl.BlockSpec((1, d), lambda i: (0, 0))
    return pl.pallas_call(
        body, grid=(t // tr,), in_specs=[row, row], out_specs=[row, vec],
        out_shape=[_sds((t, d), F32), _sds((1, d), F32)],
        compiler_params=_params("arbitrary"), name=name)(y, target)


PAIR = 2 * HEAD_DIM
N_PAIRS = N_HEADS // 2
A_TQ = 512
A_UNROLL = 4


def _halves(x):
    lane = lax.broadcasted_iota(jnp.int32, x.shape, x.ndim - 1)
    zero = jnp.zeros_like(x)
    return jnp.where(lane < HEAD_DIM, x, zero), jnp.where(lane >= HEAD_DIM, x, zero)


def _merge(a, b):
    lane = lax.broadcasted_iota(jnp.int32, a.shape, a.ndim - 1)
    return jnp.where(lane < HEAD_DIM, a, b)


def _a_valid(c):
    col = lax.broadcasted_iota(jnp.int32, (CHUNK, BAND), 1)
    return col >= (N_LEFT - c) * CHUNK


def attn_a_fwd(name, proj, kp, vp, bias, q_blk):
    t = proj.shape[0]
    tq = _row_tile(t, A_TQ)
    ncs = tq // CHUNK
    un = math.gcd(A_UNROLL, ncs)

    def body(q_ref, k_ref, v_ref, b_ref, o_ref, l_ref):
        i = pl.program_id(1)

        def group(gg, carry):
            cs = [i * ncs + gg * un + u for u in range(un)]
            r0s = [pl.multiple_of((gg * un + u) * CHUNK, CHUNK) for u in range(un)]
            k0s = [pl.multiple_of(c * CHUNK, CHUNK) for c in cs]
            ss = []
            for u in range(un):
                qh = _halves(q_ref[pl.ds(r0s[u], CHUNK), :] * QK_SCALE)
                kwin = k_ref[pl.ds(k0s[u], BAND), :]
                valid = _a_valid(cs[u])
                for hh in range(2):
                    s = lax.dot_general(qh[hh], kwin, NT, preferred_element_type=F32) + b_ref[hh]
                    ss.append(jnp.where(valid, s, -1e30))
            ps, lses = [], []
            for s in ss:
                mx = jnp.max(s, axis=-1, keepdims=True)
                p = jnp.exp(s - mx)
                den = jnp.sum(p, axis=-1, keepdims=True)
                ps.append((p * (1.0 / den)).astype(BF16))
                lses.append(mx + jnp.log(den))
            for u in range(un):
                vwin = v_ref[pl.ds(k0s[u], BAND), :]
                o0 = jnp.dot(ps[2 * u], vwin, preferred_element_type=F32)
                o1 = jnp.dot(ps[2 * u + 1], vwin, preferred_element_type=F32)
                o_ref[pl.ds(r0s[u], CHUNK), :] = _merge(o0, o1)
                l_ref[pl.ds(r0s[u], CHUNK), :] = jnp.concatenate([lses[2 * u], lses[2 * u + 1]], axis=1)
            return carry

        lax.fori_loop(0, ncs // un, group, 0)

    return pl.pallas_call(
        body, grid=(N_PAIRS, t // tq),
        in_specs=[pl.BlockSpec((tq, PAIR), lambda p, i: (i, q_blk + p)),
                  pl.BlockSpec((t + PAD_KEYS, PAIR), lambda p, i: (0, p)),
                  pl.BlockSpec((t + PAD_KEYS, PAIR), lambda p, i: (0, p)),
                  pl.BlockSpec((2, CHUNK, BAND), lambda p, i: (p, 0, 0))],
        out_specs=[pl.BlockSpec((tq, PAIR), lambda p, i: (i, p)),
                   pl.BlockSpec((None, tq, 2), lambda p, i: (p, i, 0))],
        out_shape=[_sds((t, N_PAIRS * PAIR), F32), _sds((N_PAIRS, t, 2), F32)],
        compiler_params=_params("parallel", "parallel"), name=name)(proj, kp, vp, bias)


def attn_a_bwd(name, proj, kp, vp, bias, o, lse, do, q_blk, do_blk):
    t = proj.shape[0]
    tq = _row_tile(t, A_TQ)
    ncs = tq // CHUNK
    un = math.gcd(A_UNROLL, ncs)

    def body(q_ref, k_ref, v_ref, b_ref, o_ref, l_ref, do_ref, dq_ref, dk_ref, dv_ref, db_ref):
        i = pl.program_id(1)

        @pl.when(i == 0)
        def _():
            dk_ref[...] = jnp.zeros_like(dk_ref)
            dv_ref[...] = jnp.zeros_like(dv_ref)
            db_ref[...] = jnp.zeros_like(db_ref)

        def group(gg, carry):
            cs = [i * ncs + gg * un + u for u in range(un)]
            r0s = [pl.multiple_of((gg * un + u) * CHUNK, CHUNK) for u in range(un)]
            k0s = [pl.multiple_of(c * CHUNK, CHUNK) for c in cs]
            qhs, dohs, ps, dps, deltas = [], [], [], [], []
            for u in range(un):
                rows = pl.ds(r0s[u], CHUNK)
                qh = _halves(q_ref[rows, :] * QK_SCALE)
                doh = _halves(do_ref[rows, :])
                kwin = k_ref[pl.ds(k0s[u], BAND), :]
                vwin = v_ref[pl.ds(k0s[u], BAND), :]
                valid = _a_valid(cs[u])
                dl = _halves(do_ref[rows, :].astype(F32) * o_ref[rows, :])
                for hh in range(2):
                    s = lax.dot_general(qh[hh], kwin, NT, preferred_element_type=F32) + b_ref[hh]
                    ps.append(jnp.where(valid, jnp.exp(s - l_ref[rows, hh:hh + 1]), 0.0))
                    dps.append(lax.dot_general(doh[hh], vwin, NT, preferred_element_type=F32))
                    deltas.append(jnp.sum(dl[hh], axis=-1, keepdims=True))
                qhs.append(qh)
                dohs.append(doh)
            dss = [p * (dp - dl) for p, dp, dl in zip(ps, dps, deltas)]
            for hh in range(2):
                tot = dss[hh]
                for u in range(1, un):
                    tot = tot + dss[2 * u + hh]
                db_ref[hh] += tot
            for u in range(un):
                kwin = k_ref[pl.ds(k0s[u], BAND), :]
                ds0, ds1 = dss[2 * u].astype(BF16), dss[2 * u + 1].astype(BF16)
                dq_ref[pl.ds(r0s[u], CHUNK), :] = _merge(jnp.dot(ds0, kwin, preferred_element_type=F32),
                                                         jnp.dot(ds1, kwin, preferred_element_type=F32)) * QK_SCALE
                dk_ref[pl.ds(k0s[u], BAND), :] += (lax.dot_general(ds0, qhs[u][0], TN, preferred_element_type=F32)
                                                   + lax.dot_general(ds1, qhs[u][1], TN, preferred_element_type=F32))
                dv_ref[pl.ds(k0s[u], BAND), :] += (
                    lax.dot_general(ps[2 * u].astype(BF16), dohs[u][0], TN, preferred_element_type=F32)
                    + lax.dot_general(ps[2 * u + 1].astype(BF16), dohs[u][1], TN, preferred_element_type=F32))
            return carry

        lax.fori_loop(0, ncs // un, group, 0)

    tile = lambda blk: pl.BlockSpec((tq, PAIR), lambda p, i: (i, blk + p))
    whole = pl.BlockSpec((t + PAD_KEYS, PAIR), lambda p, i: (0, p))
    bspec = pl.BlockSpec((2, CHUNK, BAND), lambda p, i: (p, 0, 0))
    return pl.pallas_call(
        body, grid=(N_PAIRS, t // tq),
        in_specs=[tile(q_blk), whole, whole, bspec, tile(0), pl.BlockSpec((None, tq, 2), lambda p, i: (p, i, 0)),
                  tile(do_blk)],
        out_specs=[tile(0), whole, whole, bspec],
        out_shape=[_sds((t, N_PAIRS * PAIR), F32), _sds((t + PAD_KEYS, N_PAIRS * PAIR), F32),
                   _sds((t + PAD_KEYS, N_PAIRS * PAIR), F32), _sds((2 * N_PAIRS, CHUNK, BAND), F32)],
        compiler_params=_params("parallel", "arbitrary"), name=name)(proj, kp, vp, bias, o, lse, do)


SB_TQ = 256
SB_TK = 256
SB_DEAD = -125.0


def _tri(n, strict):
    j = lax.broadcasted_iota(jnp.int32, (n, n), 0)
    s = lax.broadcasted_iota(jnp.int32, (n, n), 1)
    return jnp.where((j > s) if strict else (j >= s), 1.0, 0.0).astype(BF16)


def _suffix_sum(x, tri, exact):
    hi = x.astype(BF16)
    out = jnp.dot(hi, tri, preferred_element_type=F32)
    if exact:
        lo = (x - hi.astype(F32)).astype(BF16)
        out = out + jnp.dot(lo, tri, preferred_element_type=F32)
    return out


def _sb_scores(qh, ks, causal):
    z = lax.dot_general(qh, ks, NT, preferred_element_type=F32)
    lb = jnp.minimum(z, 0.0) - jnp.log(1.0 + jnp.exp(-jnp.abs(z)))
    m = lb - z
    if causal is not None:
        m = jnp.where(causal, m, 0.0)
    return lb, m


def _causal(tq, tk, off):
    return (lax.broadcasted_iota(jnp.int32, (tq, tk), 1) + off * tk) < lax.broadcasted_iota(jnp.int32, (tq, tk), 0)


def sb_fwd(name, proj, q_blk, k_blk, v_blk):
    t = proj.shape[0]
    tq = _row_tile(t, SB_TQ)
    tk = min(SB_TK, tq)
    per = tq // tk

    def body(q_ref, k_ref, v_ref, o_ref):
        i = pl.program_id(1)
        tri = _tri(tk, True)
        qh = _halves(q_ref[...] * QK_SCALE)

        def blocks(kb, carry, off):
            k0 = pl.multiple_of(kb * tk, tk)
            ks, vs = k_ref[pl.ds(k0, tk), :], v_ref[pl.ds(k0, tk), :]
            causal = None if off is None else _causal(tq, tk, off)
            lbm = [_sb_scores(qh[hh], ks, causal) for hh in range(2)]
            afters = [_suffix_sum(lbm[hh][1], tri, False) for hh in range(2)]
            out = []
            for hh in range(2):
                acc, cm = carry[2 * hh], carry[2 * hh + 1]
                w = jnp.exp(lbm[hh][0] + afters[hh] + cm)
                if causal is not None:
                    w = jnp.where(causal, w, 0.0)
                out += [acc + jnp.dot(w.astype(BF16), vs, preferred_element_type=F32),
                        cm + jnp.sum(lbm[hh][1], axis=-1, keepdims=True)]
            return tuple(out)

        def alive(carry):
            return jnp.maximum(jnp.max(carry[1]), jnp.max(carry[3])) > SB_DEAD

        carry = (jnp.zeros((tq, PAIR), F32), jnp.zeros((tq, 1), F32)) * 2
        for off in reversed(range(per)):
            carry = blocks(i * per + off, carry, off)

        def step(c):
            new = blocks(i * per - 1 - c[0], c[2:], None)
            return (c[0] + 1, alive(new)) + new

        out = lax.while_loop(lambda c: jnp.logical_and(c[0] < i * per, c[1]), step,
                             (jnp.int32(0), alive(carry)) + carry)
        o_ref[...] = _merge(out[2], out[4])

    return pl.pallas_call(
        body, grid=(N_PAIRS, t // tq),
        in_specs=[pl.BlockSpec((tq, PAIR), lambda p, i: (i, q_blk + p)),
                  pl.BlockSpec((t, PAIR), lambda p, i: (0, k_blk + p)),
                  pl.BlockSpec((t, PAIR), lambda p, i: (0, v_blk + p))],
        out_specs=pl.BlockSpec((tq, PAIR), lambda p, i: (i, p)),
        out_shape=_sds((t, N_PAIRS * PAIR), F32), compiler_params=_params("parallel", "parallel"), name=name,
    )(proj, proj, proj)


def sb_bwd(name, proj, o, do, q_blk, k_blk, v_blk, do_blk):
    t = proj.shape[0]
    tq = _row_tile(t, SB_TQ)
    tk = min(SB_TK, tq)
    per = tq // tk

    def body(q_ref, k_ref, v_ref, o_ref, do_ref, dq_ref, dk_ref, dv_ref):
        i = pl.program_id(1)

        @pl.when(i == 0)
        def _():
            dk_ref[...] = jnp.zeros_like(dk_ref)
            dv_ref[...] = jnp.zeros_like(dv_ref)

        tri_s, tri_i = _tri(tk, True), _tri(tk, False)
        qh = _halves(q_ref[...] * QK_SCALE)
        doh = _halves(do_ref[...])
        deltas = [jnp.sum(x, axis=-1, keepdims=True) for x in _halves(do_ref[...].astype(F32) * o_ref[...])]

        def blocks(kb, carry, off):
            k0 = pl.multiple_of(kb * tk, tk)
            ks, vs = k_ref[pl.ds(k0, tk), :], v_ref[pl.ds(k0, tk), :]
            causal = None if off is None else _causal(tq, tk, off)
            lbm = [_sb_scores(qh[hh], ks, causal) for hh in range(2)]
            dws = [lax.dot_general(doh[hh], vs, NT, preferred_element_type=F32) for hh in range(2)]
            afters = [_suffix_sum(lbm[hh][1], tri_s, False) for hh in range(2)]
            wbs, es = [], []
            for hh in range(2):
                w = jnp.exp(lbm[hh][0] + afters[hh] + carry[3 * hh + 1])
                if causal is not None:
                    w = jnp.where(causal, w, 0.0)
                wbs.append(w.astype(BF16))
                es.append(wbs[hh].astype(F32) * dws[hh])
            sfx = [_suffix_sum(es[hh], tri_i, True) for hh in range(2)]
            dzs = []
            for hh in range(2):
                left = deltas[hh] - (sfx[hh] + carry[3 * hh + 2])
                sig = jnp.exp(lbm[hh][0])
                dz = es[hh] * (1.0 - sig) - left * sig
                if causal is not None:
                    dz = jnp.where(causal, dz, 0.0)
                dzs.append(dz.astype(BF16))
            dk_ref[pl.ds(k0, tk), :] += (lax.dot_general(dzs[0], qh[0], TN, preferred_element_type=F32)
                                         + lax.dot_general(dzs[1], qh[1], TN, preferred_element_type=F32))
            dv_ref[pl.ds(k0, tk), :] += (lax.dot_general(wbs[0], doh[0], TN, preferred_element_type=F32)
                                         + lax.dot_general(wbs[1], doh[1], TN, preferred_element_type=F32))
            out = []
            for hh in range(2):
                out += [carry[3 * hh] + jnp.dot(dzs[hh], ks, preferred_element_type=F32),
                        carry[3 * hh + 1] + jnp.sum(lbm[hh][1], axis=-1, keepdims=True),
                        carry[3 * hh + 2] + jnp.sum(es[hh], axis=-1, keepdims=True)]
            return tuple(out)

        def alive(carry):
            return jnp.maximum(jnp.max(carry[1]), jnp.max(carry[4])) > SB_DEAD

        zero = jnp.zeros((tq, 1), F32)
        carry = (jnp.zeros((tq, PAIR), F32), zero, zero) * 2
        for off in reversed(range(per)):
            carry = blocks(i * per + off, carry, off)

        def step(c):
            new = blocks(i * per - 1 - c[0], c[2:], None)
            return (c[0] + 1, alive(new)) + new

        out = lax.while_loop(lambda c: jnp.logical_and(c[0] < i * per, c[1]), step,
                             (jnp.int32(0), alive(carry)) + carry)
        dq_ref[...] = _merge(out[2], out[5]) * QK_SCALE

    tile = lambda blk: pl.BlockSpec((tq, PAIR), lambda p, i: (i, blk + p))
    whole = lambda blk: pl.BlockSpec((t, PAIR), lambda p, i: (0, blk + p))
    return pl.pallas_call(
        body, grid=(N_PAIRS, t // tq),
        in_specs=[tile(q_blk), whole(k_blk), whole(v_blk), tile(0), tile(do_blk)],
        out_specs=[tile(0), whole(0), whole(0)],
        out_shape=[_sds((t, N_PAIRS * PAIR), F32)] * 3,
        compiler_params=_params("parallel", "arbitrary"), name=name)(proj, proj, proj, o, do)


def _sigmoid(x):
    return 1.0 / (1.0 + jnp.exp(-x))


def gu_gap(f8):
    return -(-f8 // 128) * 128


def merge_gu(gate, up):
    f8 = gate.shape[-1]
    pad = jnp.zeros(gate.shape[:-1] + (gu_gap(f8) - f8,), gate.dtype)
    return jnp.concatenate([gate, pad, up], axis=-1)


def split_gu(gu, f8):
    return gu[..., :f8], gu[..., gu_gap(f8):]


def ffn_up(name, h, wgu, f8, layer):
    t, d = h.shape
    nb, _, _, fw = wgu.shape
    gap = gu_gap(f8)
    tm = _row_tile(t, 1024)

    def body(h_ref, w_ref, gu_ref, a_ref):
        r = jnp.dot(h_ref[...], w_ref[...], preferred_element_type=F32)
        gu_ref[...] = r.astype(BF16)
        g, u = r[:, :f8], r[:, gap:]
        a_ref[...] = (g * _sigmoid(g) * u).astype(BF16)

    return pl.pallas_call(
        body, grid=(t // tm, nb),
        in_specs=[pl.BlockSpec((tm, d), lambda i, k: (i, 0)),
                  pl.BlockSpec((None, None, d, fw), lambda i, k: (k, layer, 0, 0))],
        out_specs=[pl.BlockSpec((None, tm, fw), lambda i, k: (k, i, 0)),
                   pl.BlockSpec((None, tm, f8), lambda i, k: (k, i, 0))],
        out_shape=[_sds((nb, t, fw), BF16), _sds((nb, t, f8), BF16)],
        compiler_params=_params("parallel", "parallel"), name=name)(h, wgu)


def _mm_all_blocks(name, a, w, layer, dn, tm):
    nb, t, f = a.shape
    wshape = w.shape[2:]
    d = wshape[1] if dn == NN else wshape[0]
    tm = _row_tile(t, tm)

    def body(a_ref, w_ref, o_ref):
        acc = lax.dot_general(a_ref[0], w_ref[0], dn, preferred_element_type=F32)
        for k in range(1, nb):
            acc = acc + lax.dot_general(a_ref[k], w_ref[k], dn, preferred_element_type=F32)
        o_ref[...] = acc

    return pl.pallas_call(
        body, grid=(t // tm,),
        in_specs=[pl.BlockSpec((nb, tm, f), lambda i: (0, i, 0)),
                  pl.BlockSpec((nb, None) + wshape, lambda i: (0, layer, 0, 0))],
        out_specs=pl.BlockSpec((tm, d), lambda i: (i, 0)), out_shape=_sds((t, d), F32),
        compiler_params=_params("parallel"), name=name)(a, w)


def ffn_down(name, a, wd, layer):
    return _mm_all_blocks(name, a, wd, layer, NN, 512)


def ffn_bwd_act(name, dm, wd, gu, f8, layer):
    t, d = dm.shape
    nb, _, fw = gu.shape
    gap = gu_gap(f8)
    tm = _row_tile(t, 1024)

    def body(dm_ref, wd_ref, gu_ref, o_ref):
        da = lax.dot_general(dm_ref[...], wd_ref[...], NT, preferred_element_type=F32)
        gv = gu_ref[:, :f8].astype(F32)
        uv = gu_ref[:, gap:].astype(F32)
        sg = _sigmoid(gv)
        o_ref[...] = jnp.zeros_like(o_ref)
        o_ref[:, :f8] = (da * uv * sg * (1.0 + gv * (1.0 - sg))).astype(BF16)
        o_ref[:, gap:] = (da * gv * sg).astype(BF16)

    bspec = pl.BlockSpec((None, tm, fw), lambda i, k: (k, i, 0))
    return pl.pallas_call(
        body, grid=(t // tm, nb),
        in_specs=[pl.BlockSpec((tm, d), lambda i, k: (i, 0)),
                  pl.BlockSpec((None, None, f8, d), lambda i, k: (k, layer, 0, 0)), bspec],
        out_specs=bspec, out_shape=_sds((nb, t, fw), BF16),
        compiler_params=_params("parallel", "parallel"), name=name)(dm, wd, gu)


def ffn_bwd_dh(name, dgu, wgu, layer):
    return _mm_all_blocks(name, dgu, wgu, layer, NT, 512)


def ffn_dw_in(name, h, dact):
    t, d = h.shape
    nb, _, f8 = dact.shape
    tk = _row_tile(t, 2048)
    return _gmm(name, h, dact, grid=(nb, t // tk),
                a_blk=(tk, d), a_idx=lambda b, s: (s, 0), b_blk=(None, tk, f8), b_idx=lambda b, s: (b, s, 0),
                o_blk=(None, d, f8), o_idx=lambda b, s: (b, 0, 0), out_shape=(nb, d, f8), out_dtype=F32, dn=TN,
                acc_shape=(d, f8))


def ffn_dw_down(name, a, dm):
    nb, t, f8 = a.shape
    d = dm.shape[1]
    tk = _row_tile(t, 2048)
    return _gmm(name, a, dm, grid=(nb, t // tk),
                a_blk=(None, tk, f8), a_idx=lambda b, s: (b, s, 0), b_blk=(tk, d), b_idx=lambda b, s: (s, 0),
                o_blk=(None, f8, d), o_idx=lambda b, s: (b, 0, 0), out_shape=(nb, f8, d), out_dtype=F32, dn=TN,
                acc_shape=(f8, d))


GELU_C = math.sqrt(2.0 / math.pi)
GELU_A = 0.044715


def _gelu(x):
    return 0.5 * x * (1.0 + jnp.tanh(GELU_C * (x + GELU_A * x * x * x)))


def _gelu_grad(x):
    th = jnp.tanh(GELU_C * (x + GELU_A * x * x * x))
    return 0.5 * (1.0 + th) + 0.5 * x * (1.0 - th * th) * GELU_C * (1.0 + 3.0 * GELU_A * x * x)


def _neg_expm1(x):
    series = x * (1.0 + x * (0.5 + x * (1.0 / 6.0 + x * (1.0 / 24.0 + x * (1.0 / 120.0 + x * (1.0 / 720.0))))))
    return -jnp.where(x > -0.25, series, jnp.exp(x) - 1.0)


CONV_TR = 256
CONV_TAPS = 4
HALO = 8


def _shifted(ext, k, tr, back):
    if back:
        return pltpu.roll(ext, k, 0)[HALO:, :] if k else ext[HALO:, :]
    return pltpu.roll(ext, tr + HALO - k, 0)[:tr, :] if k else ext[:tr, :]


def conv4_fwd(name, src, cb, w, b):
    t, c = src.shape[0], w.shape[1]
    tr = _row_tile(t, CONV_TR)
    hb = tr // HALO

    def body(x_ref, h_ref, w_ref, b_ref, o_ref):
        i = pl.program_id(0)
        ext = jnp.concatenate([jnp.where(i == 0, 0.0, h_ref[...]), x_ref[...]], axis=0)
        acc = b_ref[...]
        for k in range(CONV_TAPS):
            acc = acc + w_ref[CONV_TAPS - 1 - k:CONV_TAPS - k, :] * _shifted(ext, k, tr, True)
        o_ref[...] = acc

    return pl.pallas_call(
        body, grid=(t // tr,),
        in_specs=[pl.BlockSpec((tr, c), lambda i: (i, cb)),
                  pl.BlockSpec((HALO, c), lambda i: (jnp.maximum(i * hb - 1, 0), cb)),
                  pl.BlockSpec((CONV_TAPS, c), lambda i: (0, 0)), pl.BlockSpec((1, c), lambda i: (0, 0))],
        out_specs=pl.BlockSpec((tr, c), lambda i: (i, 0)), out_shape=_sds((t, c), F32),
        compiler_params=_params("parallel"), name=name)(src, src, w, b)


def conv4_bwd_x(name, dy, w):
    t, c = dy.shape
    tr = _row_tile(t, CONV_TR)
    hb = tr // HALO
    last = t // tr - 1

    def body(y_ref, h_ref, w_ref, o_ref):
        i = pl.program_id(0)
        ext = jnp.concatenate([y_ref[...], jnp.where(i == last, 0.0, h_ref[...])], axis=0)
        acc = w_ref[CONV_TAPS - 1:CONV_TAPS, :] * y_ref[...]
        for k in range(1, CONV_TAPS):
            acc = acc + w_ref[CONV_TAPS - 1 - k:CONV_TAPS - k, :] * _shifted(ext, k, tr, False)
        o_ref[...] = acc

    return pl.pallas_call(
        body, grid=(t // tr,),
        in_specs=[pl.BlockSpec((tr, c), lambda i: (i, 0)),
                  pl.BlockSpec((HALO, c), lambda i: (jnp.minimum((i + 1) * hb, t // HALO - 1), 0)),
                  pl.BlockSpec((CONV_TAPS, c), lambda i: (0, 0))],
        out_specs=pl.BlockSpec((tr, c), lambda i: (i, 0)), out_shape=_sds((t, c), F32),
        compiler_params=_params("parallel"), name=name)(dy, dy, w)


def conv4_bwd_w(name, src, cb, dy):
    t, c = dy.shape
    tr = _row_tile(t, CONV_TR)
    hb = tr // HALO

    def body(x_ref, h_ref, dy_ref, dw_ref, db_ref):
        i = pl.program_id(0)

        @pl.when(i == 0)
        def _():
            dw_ref[...] = jnp.zeros_like(dw_ref)
            db_ref[...] = jnp.zeros_like(db_ref)

        ext = jnp.concatenate([jnp.where(i == 0, 0.0, h_ref[...]), x_ref[...]], axis=0)
        dyv = dy_ref[...]
        db_ref[...] += jnp.sum(dyv, axis=0, keepdims=True)
        for k in range(CONV_TAPS):
            dw_ref[CONV_TAPS - 1 - k:CONV_TAPS - k, :] += jnp.sum(dyv * _shifted(ext, k, tr, True), axis=0,
                                                                   keepdims=True)

    return pl.pallas_call(
        body, grid=(t // tr,),
        in_specs=[pl.BlockSpec((tr, c), lambda i: (i, cb)),
                  pl.BlockSpec((HALO, c), lambda i: (jnp.maximum(i * hb - 1, 0), cb)),
                  pl.BlockSpec((tr, c), lambda i: (i, 0))],
        out_specs=[pl.BlockSpec((CONV_TAPS, c), lambda i: (0, 0)), pl.BlockSpec((1, c), lambda i: (0, 0))],
        out_shape=[_sds((CONV_TAPS, c), F32), _sds((1, c), F32)],
        compiler_params=_params("arbitrary"), name=name)(src, src, dy)


def _rg_gate_values(xcv, wa_ref, wi_ref, ba_ref, bi_ref, lam_ref):
    xb = xcv.astype(BF16)
    r = _sigmoid(jnp.dot(xb, wa_ref[...], preferred_element_type=F32) + ba_ref[...])
    ig = _sigmoid(jnp.dot(xb, wi_ref[...], preferred_element_type=F32) + bi_ref[...])
    lam = lam_ref[...]
    sp = jnp.maximum(-lam, 0.0) + jnp.log(1.0 + jnp.exp(-jnp.abs(lam)))
    log_a = -LRU_C * r * sp
    a = jnp.exp(log_a)
    mult = jnp.sqrt(_neg_expm1(2.0 * log_a))
    return xb, r, ig, sp, a, mult


def rg_gates_fwd(name, xc, wa, wi, ba, bi, lam):
    t, c = xc.shape
    nb, cb, _ = wa.shape
    tm = _row_tile(t, 512)

    def body(xc_ref, wa_ref, wi_ref, ba_ref, bi_ref, lam_ref, a_ref, u_ref):
        xcv = xc_ref[...]
        _, _, ig, _, a, mult = _rg_gate_values(xcv, wa_ref, wi_ref, ba_ref, bi_ref, lam_ref)
        a_ref[...] = a
        u_ref[...] = mult * (ig * xcv)

    blk = pl.BlockSpec((tm, cb), lambda i, n: (i, n))
    wsp = pl.BlockSpec((None, cb, cb), lambda i, n: (n, 0, 0))
    vec = pl.BlockSpec((1, cb), lambda i, n: (0, n))
    return pl.pallas_call(
        body, grid=(t // tm, nb), in_specs=[blk, wsp, wsp, vec, vec, vec], out_specs=[blk, blk],
        out_shape=[_sds((t, c), F32)] * 2, compiler_params=_params("parallel", "parallel"), name=name,
    )(xc, wa, wi, ba, bi, lam)


def rg_gates_bwd(name, xc, gu, hprev, wa, wi, ba, bi, lam):
    t, c = xc.shape
    nb, cb, _ = wa.shape
    tm = _row_tile(t, 512)

    def body(xc_ref, gu_ref, hp_ref, wa_ref, wi_ref, ba_ref, bi_ref, lam_ref,
             dxc_ref, dwa_ref, dwi_ref, dba_ref, dbi_ref, dlam_ref):
        i = pl.program_id(1)

        @pl.when(i == 0)
        def _():
            for ref in (dwa_ref, dwi_ref, dba_ref, dbi_ref, dlam_ref):
                ref[...] = jnp.zeros_like(ref)

        xcv = xc_ref[...]
        xb, r, ig, sp, a, mult = _rg_gate_values(xcv, wa_ref, wi_ref, ba_ref, bi_ref, lam_ref)
        gv = gu_ref[...]
        d_ixc = gv * mult
        d_i = d_ixc * xcv
        d_mult = gv * ig * xcv
        d_a = gv * hp_ref[...] - d_mult * a / mult
        d_log_a = d_a * a
        d_r = d_log_a * (-LRU_C * sp)
        sig_neg_lam = 1.0 / (1.0 + jnp.exp(lam_ref[...]))
        dlam_ref[...] += jnp.sum(d_log_a * r, axis=0, keepdims=True) * (LRU_C * sig_neg_lam)
        dpa = d_r * r * (1.0 - r)
        dpi = d_i * ig * (1.0 - ig)
        dba_ref[...] += jnp.sum(dpa, axis=0, keepdims=True)
        dbi_ref[...] += jnp.sum(dpi, axis=0, keepdims=True)
        dpab, dpib = dpa.astype(BF16), dpi.astype(BF16)
        dxc_ref[...] = (d_ixc * ig + lax.dot_general(dpab, wa_ref[...], NT, preferred_element_type=F32)
                        + lax.dot_general(dpib, wi_ref[...], NT, preferred_element_type=F32))
        dwa_ref[...] += lax.dot_general(xb, dpab, TN, preferred_element_type=F32)
        dwi_ref[...] += lax.dot_general(xb, dpib, TN, preferred_element_type=F32)

    blk = pl.BlockSpec((tm, cb), lambda n, i: (i, n))
    wsp = pl.BlockSpec((None, cb, cb), lambda n, i: (n, 0, 0))
    vec = pl.BlockSpec((1, cb), lambda n, i: (0, n))
    return pl.pallas_call(
        body, grid=(nb, t // tm), in_specs=[blk, blk, blk, wsp, wsp, vec, vec, vec],
        out_specs=[blk, wsp, wsp, vec, vec, vec],
        out_shape=[_sds((t, c), F32), _sds((nb, cb, cb), F32), _sds((nb, cb, cb), F32),
                   _sds((1, c), F32), _sds((1, c), F32), _sds((1, c), F32)],
        compiler_params=_params("parallel", "arbitrary"), name=name)(xc, gu, hprev, wa, wi, ba, bi, lam)


SCAN_TS = 256
SCAN_TC = 512


def _tile_scan(a, b, reverse):
    ts = a.shape[0]
    row = lax.broadcasted_iota(jnp.int32, a.shape, 0)
    d = 1
    while d < ts:
        if reverse:
            inside = row < ts - d
            a_sh = jnp.where(inside, pltpu.roll(a, ts - d, 0), 1.0)
            b_sh = jnp.where(inside, pltpu.roll(b, ts - d, 0), 0.0)
        else:
            inside = row >= d
            a_sh = jnp.where(inside, pltpu.roll(a, d, 0), 1.0)
            b_sh = jnp.where(inside, pltpu.roll(b, d, 0), 0.0)
        b = b + a * b_sh
        a = a * a_sh
        d *= 2
    return a, b


def rg_scan_fwd(name, a, u, gate_pre):
    t, c = a.shape
    ts, tc = _row_tile(t, SCAN_TS), _row_tile(c, SCAN_TC)

    def body(a_ref, u_ref, g_ref, h_ref, z_ref, carry_ref):
        s = pl.program_id(1)

        @pl.when(s == 0)
        def _():
            carry_ref[...] = jnp.zeros_like(carry_ref)

        ac, bc = _tile_scan(a_ref[...], u_ref[...], False)
        h = bc + ac * carry_ref[0:1, :]
        h_ref[...] = h
        z_ref[...] = (h * _gelu(g_ref[...])).astype(BF16)
        carry_ref[0:1, :] = h[ts - 1:ts, :]

    blk = pl.BlockSpec((ts, tc), lambda j, s: (s, j))
    return pl.pallas_call(
        body, grid=(c // tc, t // ts), in_specs=[blk, blk, blk], out_specs=[blk, blk],
        out_shape=[_sds((t, c), F32), _sds((t, c), BF16)], scratch_shapes=[pltpu.VMEM((8, tc), F32)],
        compiler_params=_params("parallel", "arbitrary"), name=name)(a, u, gate_pre)


def rg_scan_bwd(name, a_next, hs, gate_pre, dz):
    t, c = hs.shape
    ts, tc = _row_tile(t, SCAN_TS), _row_tile(c, SCAN_TC)
    nt = t // ts

    def body(an_ref, h_ref, g_ref, dz_ref, gu_ref, dgate_ref, carry_ref):
        s = pl.program_id(1)

        @pl.when(s == 0)
        def _():
            carry_ref[...] = jnp.zeros_like(carry_ref)

        gate = g_ref[...]
        dzv = dz_ref[...]
        dgate_ref[...] = (dzv * h_ref[...] * _gelu_grad(gate)).astype(BF16)
        ac, bc = _tile_scan(an_ref[...], dzv * _gelu(gate), True)
        gu = bc + ac * carry_ref[0:1, :]
        gu_ref[...] = gu
        carry_ref[0:1, :] = gu[0:1, :]

    blk = pl.BlockSpec((ts, tc), lambda j, s: (nt - 1 - s, j))
    return pl.pallas_call(
        body, grid=(c // tc, nt), in_specs=[blk, blk, blk, blk], out_specs=[blk, blk],
        out_shape=[_sds((t, c), F32), _sds((t, c), BF16)], scratch_shapes=[pltpu.VMEM((8, tc), F32)],
        compiler_params=_params("parallel", "arbitrary"), name=name)(a_next, hs, gate_pre, dz)


def _shift_down(x, k):
    return jnp.pad(x, ((k, 0), (0, 0)))[:x.shape[0]] if k else x


def _shift_up(x, k):
    return jnp.pad(x, ((0, k), (0, 0)))[k:] if k else x


QA_BLK, KA_BLK, VA_BLK, QS_BLK, KS_BLK, VS_BLK = (g * N_PAIRS for g in range(6))


TOEP_W = 640
TOEP_FLAT = 320
TABLE_LOW = 193


def rel_bias_matrix(name, table):
    h = table.shape[0]
    diag = jnp.concatenate([jnp.repeat(table[:, 2 * REL_CLIP:], TOEP_FLAT, axis=1),
                            jnp.flip(table[:, TABLE_LOW:2 * REL_CLIP], axis=1),
                            jnp.zeros((h, 1), table.dtype)], axis=1)[:, None, :]

    def body(v_ref, o_ref):
        rows = jnp.broadcast_to(v_ref[...], (CHUNK, TOEP_W))
        o_ref[...] = pltpu.roll(rows, TOEP_W - (CHUNK - 1), 1, stride=1, stride_axis=0)

    out = pl.pallas_call(
        body, grid=(h,), in_specs=[pl.BlockSpec((None, 1, TOEP_W), lambda hh: (hh, 0, 0))],
        out_specs=pl.BlockSpec((None, CHUNK, TOEP_W), lambda hh: (hh, 0, 0)),
        out_shape=_sds((h, CHUNK, TOEP_W), F32), compiler_params=_params("parallel"), name=name)(diag)
    return out[:, :, :BAND]


def rel_bias_grad(name, dbias):
    h = dbias.shape[0]
    flipped = jnp.pad(jnp.flip(dbias, axis=1), ((0, 0), (0, 0), (0, TOEP_W - BAND)))

    def body(x_ref, o_ref):
        skew = pltpu.roll(x_ref[...], 0, 1, stride=1, stride_axis=0)
        col = jnp.sum(skew, axis=0, keepdims=True)
        lane = lax.broadcasted_iota(jnp.int32, col.shape, 1)
        flat = jnp.sum(jnp.where(lane < TOEP_FLAT, col, 0.0), axis=1, keepdims=True)
        o_ref[...] = jnp.where(lane == TOEP_W - 1, flat, col)

    out = pl.pallas_call(
        body, grid=(h,), in_specs=[pl.BlockSpec((None, CHUNK, TOEP_W), lambda hh: (hh, 0, 0))],
        out_specs=pl.BlockSpec((None, 1, TOEP_W), lambda hh: (hh, 0, 0)),
        out_shape=_sds((h, 1, TOEP_W), F32), compiler_params=_params("parallel"), name=name)(flipped)[:, 0, :]
    return jnp.concatenate([jnp.zeros((h, TABLE_LOW), F32), jnp.flip(out[:, TOEP_FLAT:TOEP_W - 1], axis=1),
                            out[:, TOEP_W - 1:]], axis=1)


def attn_layer_fwd(tag, x, h, w, g_next):
    proj = mm_nn_wblk(tag + "_proj", h, w["w_in"], w["idx"], BF16)
    width = N_PAIRS * PAIR
    pad = lambda a: jnp.pad(a, ((PAD_KEYS, 0), (0, 0)))
    kap, vap = pad(proj[:, width:2 * width]), pad(proj[:, 2 * width:3 * width])
    bias = rel_bias_matrix(tag + "_bias", w["rel_bias"])
    oa, lse = attn_a_fwd(tag + "_a", proj, kap, vap, bias, QA_BLK)
    ob = sb_fwd(tag + "_sb", proj, QS_BLK, KS_BLK, VS_BLK)
    o = jnp.concatenate([oa, ob], axis=1).astype(BF16)
    m = mm_nn(tag + "_out", o, w["w_out"], F32)
    x1, h_next = resid_norm_fwd(tag + "_res", x, m, w["g_post"], g_next)
    return x1, h_next, (x, h, proj, kap, vap, bias, oa, lse, ob, o, m)


def attn_layer_bwd(tag, dx1, saved, w):
    x, h, proj, kap, vap, bias, oa, lse, ob, o, m = saved
    dm, dg_post = norm_bwd(tag + "_dpost", dx1, m, w["g_post"], None, BF16)
    d_w_out = mm_tn(tag + "_dwout", o, dm, F32)
    do = mm_nt(tag + "_do", dm, w["w_out"], BF16)
    dqa, dkap, dvap, dbias = attn_a_bwd(tag + "_da", proj, kap, vap, bias, oa, lse, do, QA_BLK, 0)
    dqs, dks, dvs = sb_bwd(tag + "_dsb", proj, ob, do, QS_BLK, KS_BLK, VS_BLK, N_PAIRS)
    d_rel = rel_bias_grad(tag + "_dbias", dbias)
    dproj = jnp.concatenate([dqa, dkap[PAD_KEYS:], dvap[PAD_KEYS:], dqs, dks, dvs], axis=1).astype(BF16)
    d_w_in = mm_tn_oblk(tag + "_dwin", h, dproj, w["w_in"].shape[3], F32)
    dh = mm_nt_wblk(tag + "_dh", dproj, w["w_in"], w["idx"], F32)
    dx, dg_pre = norm_bwd(tag + "_dpre", dh, x, w["g_pre"], dx1, F32)
    return dx, dict(w_in=d_w_in, w_out=d_w_out, rel_bias=d_rel, g_pre=dg_pre, g_post=dg_post)


def rg_layer_fwd(tag, x, h, w, g_next):
    proj = mm_nn_wblk(tag + "_proj", h, w["w_in"], w["idx"], F32)
    xc = conv4_fwd(tag + "_conv", proj, 1, w["conv_w"], w["conv_b"])
    a, u = rg_gates_fwd(tag + "_gates", xc, w["w_a"], w["w_i"], w["b_a"], w["b_i"], w["lam"])
    hs, z = rg_scan_fwd(tag + "_scan", a, u, proj)
    m = mm_nn(tag + "_out", z, w["w_out"], F32)
    x1, h_next = resid_norm_fwd(tag + "_res", x, m, w["g_post"], g_next)
    return x1, h_next, (x, h, proj, xc, a, hs, z, m)


def rg_layer_bwd(tag, dx1, saved, w):
    x, h, proj, xc, a, hs, z, m = saved
    dm, dg_post = norm_bwd(tag + "_dpost", dx1, m, w["g_post"], None, BF16)
    d_w_out = mm_tn(tag + "_dwout", z, dm, F32)
    dz = mm_nt(tag + "_dz", dm, w["w_out"], F32)
    gu, dgate = rg_scan_bwd(tag + "_dscan", _shift_up(a, 1), hs, proj, dz)
    dxc, d_w_a, d_w_i, d_b_a, d_b_i, d_lam = rg_gates_bwd(
        tag + "_dgates", xc, gu, _shift_down(hs, 1), w["w_a"], w["w_i"], w["b_a"], w["b_i"], w["lam"])
    d_conv_w, d_conv_b = conv4_bwd_w(tag + "_dconvw", proj, 1, dxc)
    dxr = conv4_bwd_x(tag + "_dconv", dxc, w["conv_w"])
    dproj = jnp.concatenate([dgate, dxr.astype(BF16)], axis=1)
    d_w_in = mm_tn_oblk(tag + "_dwin", h, dproj, w["w_in"].shape[3], F32)
    dh = mm_nt_wblk(tag + "_dh", dproj, w["w_in"], w["idx"], F32)
    dx, dg_pre = norm_bwd(tag + "_dpre", dh, x, w["g_pre"], dx1, F32)
    return dx, dict(w_in=d_w_in, w_out=d_w_out, conv_w=d_conv_w, conv_b=d_conv_b, w_a=d_w_a, w_i=d_w_i,
                    b_a=d_b_a, b_i=d_b_i, lam=d_lam, g_pre=dg_pre, g_post=dg_post)


def ffn_layer_fwd(tag, x, h, w, g_next):
    f8 = w["w_down"].shape[2]
    gu, a = ffn_up(tag + "_up", h, w["w_gu"], f8, w["idx"])
    f = ffn_down(tag + "_down", a, w["w_down"], w["idx"])
    x1, h_next = resid_norm_fwd(tag + "_res", x, f, w["g_post"], g_next)
    return x1, h_next, (x, h, gu, a, f)


def ffn_layer_bwd(tag, dx1, saved, w):
    x, h, gu, a, f = saved
    f8 = w["w_down"].shape[2]
    dm, dg_post = norm_bwd(tag + "_dpost", dx1, f, w["g_post"], None, BF16)
    d_w_down = ffn_dw_down(tag + "_dwdown", a, dm)
    dgu = ffn_bwd_act(tag + "_dact", dm, w["w_down"], gu, f8, w["idx"])
    d_w_gu = ffn_dw_in(tag + "_dwgu", h, dgu)
    dh = ffn_bwd_dh(tag + "_dh", dgu, w["w_gu"], w["idx"])
    dx, dg_pre = norm_bwd(tag + "_dpre", dh, x, w["g_pre"], dx1, F32)
    return dx, dict(w_gu=d_w_gu, w_down=d_w_down, g_pre=dg_pre, g_post=dg_post)


def _place():
    return lax.axis_index("x"), lax.axis_index("y"), lax.axis_index("c")


def all_gather(name, blks):
    n = len(blks)

    def body(*refs):
        x_refs, out_refs = refs[:n], refs[n:2 * n]
        send_sems, recv_sems, local_sems = refs[2 * n:]
        x, y, cc = _place()
        me, sibling = (x, y, cc), (x, y, 1 - cc)
        chips = [(1 - x, y), (x, 1 - y), (1 - x, 1 - y)]

        def slot(a, px, py, pc):
            return out_refs[a].at[4 * px + 2 * py + pc]

        def copy(a, k, block, to, src=None):
            return pltpu.make_async_remote_copy(
                src_ref=slot(a, *block) if src is None else src, dst_ref=slot(a, *block),
                send_sem=send_sems.at[7 * a + k], recv_sem=recv_sems.at[7 * a + k], device_id=to, device_id_type=MESH)

        mine = [pltpu.make_async_copy(x_refs[a], slot(a, *me), local_sems.at[a]) for a in range(n)]
        first = []
        for a in range(n):
            mine[a].start()
            first.append(copy(a, 0, me, sibling, src=x_refs[a]))
            first += [copy(a, 1 + j, me, (*chip, cc), src=x_refs[a]) for j, chip in enumerate(chips)]
        for cp in first:
            cp.start()
        passed = []
        for j, chip in enumerate(chips):
            for a in range(n):
                copy(a, 1 + j, (*chip, cc), me).wait_recv()
                passed.append(copy(a, 4 + j, (*chip, cc), sibling))
                passed[-1].start()
        for a in range(n):
            copy(a, 0, sibling, me).wait_recv()
            for j, chip in enumerate(chips):
                copy(a, 4 + j, (*chip, 1 - cc), me).wait_recv()
        for cp in first + passed:
            cp.wait_send()
        for cp in mine:
            cp.wait()

    return pl.pallas_call(
        body, out_shape=[_sds((N_DEV,) + b.shape, b.dtype) for b in blks], in_specs=[ANY] * n, out_specs=[ANY] * n,
        scratch_shapes=[pltpu.SemaphoreType.DMA((7 * n,)), pltpu.SemaphoreType.DMA((7 * n,)),
                        pltpu.SemaphoreType.DMA((n,))],
        name=name)(*blks)


def exchange_pair(name, gs):
    n = len(gs)
    nchip = 4

    def body(*refs):
        g_refs, land_refs = refs[:n], refs[n:2 * n]
        send_sems, recv_sems = refs[2 * n:]
        x, y, cc = _place()
        copies = [pltpu.make_async_remote_copy(
            src_ref=g_refs[a].at[j, 1 - cc], dst_ref=land_refs[a].at[j], send_sem=send_sems.at[nchip * a + j],
            recv_sem=recv_sems.at[nchip * a + j], device_id=(x, y, 1 - cc), device_id_type=MESH)
            for a in range(n) for j in range(nchip)]
        for cp in copies:
            cp.start()
        for cp in copies:
            cp.wait()

    return pl.pallas_call(
        body, out_shape=[_sds((nchip,) + g.shape[2:], g.dtype) for g in gs], in_specs=[ANY] * n, out_specs=[ANY] * n,
        scratch_shapes=[pltpu.SemaphoreType.DMA((nchip * n,)), pltpu.SemaphoreType.DMA((nchip * n,))],
        name=name)(*gs)


def pair_sum(name, g, land, core, out_dtype):
    nchip, _, r, c = g.shape
    tr = _divisor_tile(r, 1024, 16)

    def body(core_ref, g_ref, l_ref, o_ref):
        o_ref[...] = (g_ref[...] + l_ref[...]).astype(o_ref.dtype)

    return pl.pallas_call(
        body,
        grid_spec=pltpu.PrefetchScalarGridSpec(
            num_scalar_prefetch=1, grid=(nchip, r // tr),
            in_specs=[pl.BlockSpec((None, None, tr, c), lambda j, i, core_ref: (j, core_ref[0], i, 0)),
                      pl.BlockSpec((None, tr, c), lambda j, i, core_ref: (j, i, 0))],
            out_specs=pl.BlockSpec((None, tr, c), lambda j, i, core_ref: (j, i, 0))),
        out_shape=_sds((nchip, r, c), out_dtype), compiler_params=_params("parallel", "parallel"), name=name,
    )(core, g, land)


def exchange_chips(name, ps):
    n = len(ps)

    def body(*refs):
        p_refs, land_refs = refs[:n], refs[n:2 * n]
        send_sems, recv_sems, local_sems = refs[2 * n:]
        x, y, cc = _place()
        mine = 2 * x + y
        chips = [(1 - x, y), (x, 1 - y), (1 - x, 1 - y)]
        own = [pltpu.make_async_copy(p_refs[a].at[mine], land_refs[a].at[mine], local_sems.at[a]) for a in range(n)]
        for cp in own:
            cp.start()
        sends = [pltpu.make_async_remote_copy(
            src_ref=p_refs[a].at[2 * px + py], dst_ref=land_refs[a].at[mine], send_sem=send_sems.at[3 * a + k],
            recv_sem=recv_sems.at[3 * a + k], device_id=(px, py, cc), device_id_type=MESH)
            for a in range(n) for k, (px, py) in enumerate(chips)]
        for cp in sends:
            cp.start()
        for a in range(n):
            for k, (px, py) in enumerate(chips):
                pltpu.make_async_remote_copy(
                    src_ref=p_refs[a].at[mine], dst_ref=land_refs[a].at[2 * px + py], send_sem=send_sems.at[3 * a + k],
                    recv_sem=recv_sems.at[3 * a + k], device_id=(px, py, cc), device_id_type=MESH).wait_recv()
        for cp in sends:
            cp.wait_send()
        for cp in own:
            cp.wait()

    return pl.pallas_call(
        body, out_shape=[_sds(p.shape, p.dtype) for p in ps], in_specs=[ANY] * n, out_specs=[ANY] * n,
        scratch_shapes=[pltpu.SemaphoreType.DMA((3 * n,)), pltpu.SemaphoreType.DMA((3 * n,)),
                        pltpu.SemaphoreType.DMA((n,))],
        name=name)(*ps)


def adamw(name, parts, w, m, v):
    npart, r, c = parts.shape
    tr = _divisor_tile(r, 512, 16)
    c1 = 1.0 / (1.0 - ADAM_B1 ** ADAM_STEP)
    c2 = 1.0 / (1.0 - ADAM_B2 ** ADAM_STEP)

    def body(p_ref, w_ref, m_ref, v_ref, g_ref, d_ref, nm_ref, nv_ref):
        g = p_ref[0].astype(F32)
        for j in range(1, npart):
            g = g + p_ref[j].astype(F32)
        nm = ADAM_B1 * m_ref[...] + (1.0 - ADAM_B1) * g
        nv = ADAM_B2 * v_ref[...] + (1.0 - ADAM_B2) * (g * g)
        g_ref[...] = g
        nm_ref[...] = nm
        nv_ref[...] = nv
        d_ref[...] = -ADAM_LR * ((nm * c1) / (jnp.sqrt(nv * c2) + ADAM_EPS) + ADAM_WD * w_ref[...])

    row = pl.BlockSpec((tr, c), lambda i: (i, 0))
    return pl.pallas_call(
        body, grid=(r // tr,), in_specs=[pl.BlockSpec((npart, tr, c), lambda i: (0, i, 0)), row, row, row],
        out_specs=[row] * 4, out_shape=[_sds((r, c), F32)] * 4, compiler_params=_params("parallel"), name=name,
    )(parts, w, m, v)


def _pack(arrays, dtype, row_multiple):
    flat = jnp.concatenate([a.astype(dtype).reshape(-1) for a in arrays])
    per = row_multiple * LANES
    total = -(-flat.shape[0] // per) * per
    return jnp.pad(flat, (0, total - flat.shape[0])).reshape(total // LANES, LANES)


def _pack_blocked(arrays, dtype, row_multiple):
    flat = jnp.concatenate([a.astype(dtype).reshape(N_DEV, -1) for a in arrays], axis=1)
    per = row_multiple * LANES
    total = -(-flat.shape[1] // per) * per
    return jnp.pad(flat, ((0, 0), (0, total - flat.shape[1]))).reshape(N_DEV, total // LANES, LANES)


def _unpack(buf, shapes, lead=()):
    flat = buf.reshape(lead + (-1,))
    out, off = [], 0
    for s in shapes:
        n = math.prod(s)
        out.append(flat[..., off:off + n].reshape(lead + tuple(s)))
        off += n
    return out


def _to_blocked(full, ax):
    s = full.shape
    return jnp.moveaxis(full.reshape(s[:ax] + (N_DEV, s[ax] // N_DEV) + s[ax + 1:]), ax, 0)


def _from_blocked(blk, ax):
    moved = jnp.moveaxis(blk, 0, ax)
    s = moved.shape
    return moved.reshape(s[:ax] + (s[ax] * s[ax + 1],) + s[ax + 2:])


SMALL = ("rg_conv_w", "rg_conv_b", "rg_b_a", "rg_b_i", "rg_lambda")
GU = "ffn_w_gu"
BIG = ("attn_w_in", "attn_w_out", "rg_w_in", "rg_w_a", "rg_w_i", "rg_w_out", GU, "ffn_w_down")


def kernel(x, attn_w_in, attn_rel_bias, attn_w_out, rg_w_in, rg_conv_w, rg_conv_b, rg_w_a, rg_b_a, rg_w_i, rg_b_i, rg_lambda, rg_w_out, norm_mix_pre, norm_mix_post, norm_ffn_pre, norm_ffn_post, ffn_w_gate, ffn_w_up, ffn_w_down, loss_target, m_attn_w_in, m_attn_rel_bias, m_attn_w_out, m_rg_w_in, m_rg_conv_w, m_rg_conv_b, m_rg_w_a, m_rg_b_a, m_rg_w_i, m_rg_b_i, m_rg_lambda, m_rg_w_out, m_norm_mix_pre, m_norm_mix_post, m_norm_ffn_pre, m_norm_ffn_post, m_ffn_w_gate, m_ffn_w_up, m_ffn_w_down, v_attn_w_in, v_attn_rel_bias, v_attn_w_out, v_rg_w_in, v_rg_conv_w, v_rg_conv_b, v_rg_w_a, v_rg_b_a, v_rg_w_i, v_rg_b_i, v_rg_lambda, v_rg_w_out, v_norm_mix_pre, v_norm_mix_post, v_norm_ffn_pre, v_norm_ffn_post, v_ffn_w_gate, v_ffn_w_up, v_ffn_w_down):
    w_loc = dict(attn_w_in=attn_w_in, attn_rel_bias=attn_rel_bias, attn_w_out=attn_w_out, rg_w_in=rg_w_in,
                 rg_conv_w=rg_conv_w, rg_conv_b=rg_conv_b, rg_w_a=rg_w_a, rg_b_a=rg_b_a, rg_w_i=rg_w_i, rg_b_i=rg_b_i,
                 rg_lambda=rg_lambda, rg_w_out=rg_w_out, norm_mix_pre=norm_mix_pre, norm_mix_post=norm_mix_post,
                 norm_ffn_pre=norm_ffn_pre, norm_ffn_post=norm_ffn_post, ffn_w_gate=ffn_w_gate, ffn_w_up=ffn_w_up,
                 ffn_w_down=ffn_w_down)
    m_loc = dict(attn_w_in=m_attn_w_in, attn_rel_bias=m_attn_rel_bias, attn_w_out=m_attn_w_out, rg_w_in=m_rg_w_in,
                 rg_conv_w=m_rg_conv_w, rg_conv_b=m_rg_conv_b, rg_w_a=m_rg_w_a, rg_b_a=m_rg_b_a, rg_w_i=m_rg_w_i,
                 rg_b_i=m_rg_b_i, rg_lambda=m_rg_lambda, rg_w_out=m_rg_w_out, norm_mix_pre=m_norm_mix_pre,
                 norm_mix_post=m_norm_mix_post, norm_ffn_pre=m_norm_ffn_pre, norm_ffn_post=m_norm_ffn_post,
                 ffn_w_gate=m_ffn_w_gate, ffn_w_up=m_ffn_w_up, ffn_w_down=m_ffn_w_down)
    v_loc = dict(attn_w_in=v_attn_w_in, attn_rel_bias=v_attn_rel_bias, attn_w_out=v_attn_w_out, rg_w_in=v_rg_w_in,
                 rg_conv_w=v_rg_conv_w, rg_conv_b=v_rg_conv_b, rg_w_a=v_rg_w_a, rg_b_a=v_rg_b_a, rg_w_i=v_rg_w_i,
                 rg_b_i=v_rg_b_i, rg_lambda=v_rg_lambda, rg_w_out=v_rg_w_out, norm_mix_pre=v_norm_mix_pre,
                 norm_mix_post=v_norm_mix_post, norm_ffn_pre=v_norm_ffn_pre, norm_ffn_post=v_norm_ffn_post,
                 ffn_w_gate=v_ffn_w_gate, ffn_w_up=v_ffn_w_up, ffn_w_down=v_ffn_w_down)
    axis_of = dict(SHARDED)
    xt, target = x[0], loss_target[0]
    d_model = xt.shape[1]
    rows2d = lambda a: a.reshape(-1, a.shape[-1])
    small_shapes = [w_loc[n].shape for n in SMALL]
    f8 = ffn_w_gate.shape[-1]
    for d in (w_loc, m_loc, v_loc):
        d[GU] = merge_gu(d["ffn_w_gate"], d["ffn_w_up"])

    gathered = all_gather("gather_weights", [rows2d(w_loc[n]).astype(BF16) for n in BIG]
                          + [_pack([w_loc[n] for n in SMALL], F32, 8)])
    blocked = {n: g.reshape((N_DEV,) + w_loc[n].shape) for n, g in zip(BIG, gathered)}
    blocked.update(zip(SMALL, _unpack(gathered[-1], small_shapes, (N_DEV,))))
    full = {n: _from_blocked(blocked[n], axis_of[n]) for n in SMALL}
    row = lambda a: a.reshape(1, -1).astype(F32)
    square = lambda rows8: rows8.reshape(-1, rows8.shape[-1])
    gates = lambda g: jnp.swapaxes(g, 0, 1).reshape(LRU_BLOCKS, -1, g.shape[-1])

    def layer_weights(layer):
        j = layer // 2
        norms = dict(g_pre=row(norm_mix_pre[layer]), g_post=row(norm_mix_post[layer]), idx=j)
        if layer % 2 == 0:
            mix = dict(w_in=blocked["attn_w_in"], w_out=square(blocked["attn_w_out"][:, j]),
                       rel_bias=attn_rel_bias[j], **norms)
        else:
            mix = dict(w_in=blocked["rg_w_in"], w_out=square(blocked["rg_w_out"][:, j]),
                       conv_w=full["rg_conv_w"][j][:, 0, :], conv_b=row(full["rg_conv_b"][j]),
                       w_a=gates(blocked["rg_w_a"][:, j]), w_i=gates(blocked["rg_w_i"][:, j]),
                       b_a=row(full["rg_b_a"][j]), b_i=row(full["rg_b_i"][j]), lam=row(full["rg_lambda"][j]), **norms)
        ffn = dict(w_gu=blocked[GU], w_down=blocked["ffn_w_down"], idx=layer,
                   g_pre=row(norm_ffn_pre[layer]), g_post=row(norm_ffn_post[layer]))
        return mix, ffn

    weights = [layer_weights(layer) for layer in range(DEPTH)]
    act, tape = xt, []
    h = rmsnorm_fwd("l0_mix_norm", act, weights[0][0]["g_pre"])
    for layer in range(DEPTH):
        mix_w, ffn_w = weights[layer]
        mixer_fwd = attn_layer_fwd if layer % 2 == 0 else rg_layer_fwd
        act, h, saved_mix = mixer_fwd(f"l{layer}_mix", act, h, mix_w, ffn_w["g_pre"])
        g_next = weights[layer + 1][0]["g_pre"] if layer + 1 < DEPTH else None
        act, h, saved_ffn = ffn_layer_fwd(f"l{layer}_ffn", act, h, ffn_w, g_next)
        tape.append((mix_w, ffn_w, saved_mix, saved_ffn))
    dact, sq = loss_grad("loss", act, target)
    loss = lax.psum(0.5 * jnp.sum(sq) / d_model, ("x", "y", "c"))

    grads = {}
    for layer in reversed(range(DEPTH)):
        mix_w, ffn_w, saved_mix, saved_ffn = tape[layer]
        dact, grads[("ffn", layer)] = ffn_layer_bwd(f"l{layer}_ffn", dact, saved_ffn, ffn_w)
        mixer_bwd = attn_layer_bwd if layer % 2 == 0 else rg_layer_bwd
        dact, grads[("mix", layer)] = mixer_bwd(f"l{layer}_mix", dact, saved_mix, mix_w)
    attn_g = [grads[("mix", l)] for l in range(0, DEPTH, 2)]
    rg_g = [grads[("mix", l)] for l in range(1, DEPTH, 2)]
    ffn_g = [grads[("ffn", l)] for l in range(DEPTH)]
    stack = lambda gs, key: jnp.stack([g[key] for g in gs])
    by_owner = lambda gs, key, f: jnp.stack([f(g[key]) for g in gs], axis=1)
    rows8 = lambda a: a.reshape(N_DEV, -1, a.shape[-1])
    ungates = lambda a: jnp.swapaxes(a.reshape(LRU_BLOCKS, N_DEV, -1, a.shape[-1]), 0, 1)
    same = lambda a: a
    blocked_g = dict(
        attn_w_in=by_owner(attn_g, "w_in", same), attn_w_out=by_owner(attn_g, "w_out", rows8),
        rg_w_in=by_owner(rg_g, "w_in", same), rg_w_out=by_owner(rg_g, "w_out", rows8),
        rg_w_a=by_owner(rg_g, "w_a", ungates), rg_w_i=by_owner(rg_g, "w_i", ungates),
        **{GU: by_owner(ffn_g, "w_gu", same)},
        ffn_w_down=by_owner(ffn_g, "w_down", same))
    contrib = dict(
        attn_rel_bias=stack(attn_g, "rel_bias"), rg_conv_w=stack(rg_g, "conv_w")[:, :, None, :],
        rg_conv_b=stack(rg_g, "conv_b")[:, 0], rg_b_a=stack(rg_g, "b_a").reshape(rg_b_a.shape[0], LRU_BLOCKS, -1),
        rg_b_i=stack(rg_g, "b_i").reshape(rg_b_i.shape[0], LRU_BLOCKS, -1), rg_lambda=stack(rg_g, "lam")[:, 0],
        norm_mix_pre=jnp.concatenate([grads[("mix", l)]["g_pre"] for l in range(DEPTH)]),
        norm_mix_post=jnp.concatenate([grads[("mix", l)]["g_post"] for l in range(DEPTH)]),
        norm_ffn_pre=jnp.concatenate([g["g_pre"] for g in ffn_g]),
        norm_ffn_post=jnp.concatenate([g["g_post"] for g in ffn_g]),
    )
    small_g = _pack_blocked([_to_blocked(contrib[n], axis_of[n]) for n in SMALL], F32, 8)

    slabs = [blocked_g[n].reshape(4, 2, -1, blocked_g[n].shape[-1]) for n in BIG] + [small_g.reshape(4, 2, -1, LANES)]
    core = lax.axis_index("c").astype(jnp.int32).reshape(1)
    from_sibling = exchange_pair("rs_pair", slabs)
    pairs = [pair_sum(f"rs_pair_sum_{i}", g, l, core, BF16 if i < len(BIG) else F32)
             for i, (g, l) in enumerate(zip(slabs, from_sibling))]
    by_chip = exchange_chips("rs_chips", pairs)
    result = {}
    kinds = ("grad", "delta", "new_m", "new_v")
    for n, parts in zip(BIG, by_chip):
        outs = adamw("adamw_" + n, parts, *[rows2d(d[n]) for d in (w_loc, m_loc, v_loc)])
        for kind, a in zip(kinds, outs):
            result[(kind, n)] = a.reshape(w_loc[n].shape)
    for kind in kinds:
        result[(kind, "ffn_w_gate")], result[(kind, "ffn_w_up")] = split_gu(result.pop((kind, GU)), f8)
    outs = adamw("adamw_small", by_chip[-1], *[_pack([d[n] for n in SMALL], F32, 8) for d in (w_loc, m_loc, v_loc)])
    for kind, buf in zip(kinds, outs):
        result.update({(kind, n): a for n, a in zip(SMALL, _unpack(buf, small_shapes))})
    rep_shapes = [w_loc[n].shape for n in REPLICATED]
    rep_parts, = all_gather("gather_rep_grads", [_pack([contrib[n] for n in REPLICATED], F32, 8)])
    outs = adamw("adamw_replicated", rep_parts, *[_pack([d[n] for n in REPLICATED], F32, 8)
                                                  for d in (w_loc, m_loc, v_loc)])
    for kind, buf in zip(kinds, outs):
        result.update({(kind, n): a for n, a in zip(REPLICATED, _unpack(buf, rep_shapes))})
    return (loss, dact[None], *[result[(kind, n)] for kind in kinds for n in WEIGHTS])
```

```python
import functools
import math

import jax
import jax.numpy as jnp
from jax import lax
from jax.experimental import pallas as pl
from jax.experimental.pallas import tpu as pltpu

F32 = jnp.float32
BF16 = jnp.bfloat16

N_DEV = 8
DEPTH = 4
CHUNK = 64
N_LEFT = 8
BAND = (N_LEFT + 1) * CHUNK
PAD_KEYS = N_LEFT * CHUNK
HEAD_DIM = 64
N_HEADS = 8
REL_CLIP = 256
LRU_BLOCKS = 4
LRU_C = 8.0
RMS_EPS = 1e-6
QK_SCALE = HEAD_DIM ** -0.5

ADAM_LR = 0.001
ADAM_B1 = 0.9
ADAM_B2 = 0.999
ADAM_EPS = 1e-08
ADAM_WD = 0.01
ADAM_STEP = 10

LANES = 1024
V7X_VMEM_LIMIT = 56 * 1024 * 1024

MESH = pl.DeviceIdType.MESH
ANY = pl.BlockSpec(memory_space=pl.ANY)

SHARDED = (
    ("attn_w_in", 2), ("attn_w_out", 1), ("rg_w_in", 2), ("rg_conv_w", 3), ("rg_conv_b", 1),
    ("rg_w_a", 2), ("rg_b_a", 2), ("rg_w_i", 2), ("rg_b_i", 2), ("rg_lambda", 1), ("rg_w_out", 1),
    ("ffn_w_gate", 2), ("ffn_w_up", 2), ("ffn_w_down", 1),
)
REPLICATED = ("attn_rel_bias", "norm_mix_pre", "norm_mix_post", "norm_ffn_pre", "norm_ffn_post")
WEIGHTS = ("attn_w_in", "attn_rel_bias", "attn_w_out", "rg_w_in", "rg_conv_w", "rg_conv_b", "rg_w_a", "rg_b_a",
           "rg_w_i", "rg_b_i", "rg_lambda", "rg_w_out", "norm_mix_pre", "norm_mix_post", "norm_ffn_pre",
           "norm_ffn_post", "ffn_w_gate", "ffn_w_up", "ffn_w_down")


def _params(*dims):
    return pltpu.CompilerParams(dimension_semantics=dims or None, vmem_limit_bytes=V7X_VMEM_LIMIT)


def _sds(shape, dtype):
    return jax.ShapeDtypeStruct(tuple(shape), dtype)


def _row_tile(n, pref):
    t = min(n, pref)
    assert n % t == 0, (n, pref)
    return t


def _divisor_tile(n, limit, multiple):
    if n <= limit:
        return n
    best = max(t for t in range(multiple, limit + 1, multiple) if n % t == 0)
    return best


NN = (((1,), (0,)), ((), ()))
NT = (((1,), (1,)), ((), ()))
TN = (((0,), (0,)), ((), ()))


def _gmm(name, a, b, *, grid, a_blk, a_idx, b_blk, b_idx, o_blk, o_idx, out_shape, out_dtype, dn, acc_shape):
    nk = grid[-1]
    kax = len(grid) - 1

    def body(a_ref, b_ref, o_ref, acc_ref):
        part = lax.dot_general(a_ref[...], b_ref[...], dn, preferred_element_type=F32)
        if nk == 1:
            o_ref[...] = part.astype(o_ref.dtype)
            return
        k = pl.program_id(kax)

        @pl.when(k == 0)
        def _():
            acc_ref[...] = part

        @pl.when(k > 0)
        def _():
            acc_ref[...] += part

        @pl.when(k == nk - 1)
        def _():
            o_ref[...] = acc_ref[...].astype(o_ref.dtype)

    return pl.pallas_call(
        body, grid=grid,
        in_specs=[pl.BlockSpec(a_blk, a_idx), pl.BlockSpec(b_blk, b_idx)],
        out_specs=pl.BlockSpec(o_blk, o_idx),
        out_shape=_sds(out_shape, out_dtype),
        scratch_shapes=[pltpu.VMEM(acc_shape, F32)],
        compiler_params=_params(*(["parallel"] * kax + ["arbitrary"])),
        name=name,
    )(a, b)


def mm_nn(name, a, b, out_dtype, tm=1024, tn=512, tk=1024):
    (m, k), (_, n) = a.shape, b.shape
    tm, tn, tk = _row_tile(m, tm), _row_tile(n, tn), _row_tile(k, tk)
    return _gmm(name, a, b, grid=(m // tm, n // tn, k // tk),
                a_blk=(tm, tk), a_idx=lambda i, j, kk: (i, kk), b_blk=(tk, tn), b_idx=lambda i, j, kk: (kk, j),
                o_blk=(tm, tn), o_idx=lambda i, j, kk: (i, j), out_shape=(m, n), out_dtype=out_dtype, dn=NN,
                acc_shape=(tm, tn))


def mm_nt(name, a, b, out_dtype, tm=1024, tn=512, tk=1024):
    (m, k), (n, _) = a.shape, b.shape
    tm, tn, tk = _row_tile(m, tm), _row_tile(n, tn), _row_tile(k, tk)
    return _gmm(name, a, b, grid=(m // tm, n // tn, k // tk),
                a_blk=(tm, tk), a_idx=lambda i, j, kk: (i, kk), b_blk=(tn, tk), b_idx=lambda i, j, kk: (j, kk),
                o_blk=(tm, tn), o_idx=lambda i, j, kk: (i, j), out_shape=(m, n), out_dtype=out_dtype, dn=NT,
                acc_shape=(tm, tn))


def mm_tn(name, a, b, out_dtype, tm=512, tn=512, tk=1024):
    (k, m), (_, n) = a.shape, b.shape
    tm, tn, tk = _row_tile(m, tm), _row_tile(n, tn), _row_tile(k, tk)
    return _gmm(name, a, b, grid=(m // tm, n // tn, k // tk),
                a_blk=(tk, tm), a_idx=lambda i, j, kk: (kk, i), b_blk=(tk, tn), b_idx=lambda i, j, kk: (kk, j),
                o_blk=(tm, tn), o_idx=lambda i, j, kk: (i, j), out_shape=(m, n), out_dtype=out_dtype, dn=TN,
                acc_shape=(tm, tn))


def mm_nn_wblk(name, a, wb, layer, out_dtype, tm=1024, tk=1024):
    (m, k), (nb, _, _, n8) = a.shape, wb.shape
    tm, tk = _row_tile(m, tm), _row_tile(k, tk)
    return _gmm(name, a, wb, grid=(m // tm, nb, k // tk),
                a_blk=(tm, tk), a_idx=lambda i, j, kk: (i, kk),
                b_blk=(None, None, tk, n8), b_idx=lambda i, j, kk: (j, layer, kk, 0),
                o_blk=(tm, n8), o_idx=lambda i, j, kk: (i, j), out_shape=(m, nb * n8), out_dtype=out_dtype, dn=NN,
                acc_shape=(tm, n8))


def mm_nt_wblk(name, a, wb, layer, out_dtype, tm=1024, tn=512):
    m = a.shape[0]
    nb, _, k, n8 = wb.shape
    tm, tn = _row_tile(m, tm), _row_tile(k, tn)

    def body(a_ref, b_ref, o_ref):
        acc = lax.dot_general(a_ref[:, 0:n8], b_ref[0], NT, preferred_element_type=F32)
        for j in range(1, nb):
            acc = acc + lax.dot_general(a_ref[:, j * n8:(j + 1) * n8], b_ref[j], NT, preferred_element_type=F32)
        o_ref[...] = acc.astype(o_ref.dtype)

    return pl.pallas_call(
        body, grid=(m // tm, k // tn),
        in_specs=[pl.BlockSpec((tm, nb * n8), lambda i, j: (i, 0)),
                  pl.BlockSpec((nb, None, tn, n8), lambda i, j: (0, layer, j, 0))],
        out_specs=pl.BlockSpec((tm, tn), lambda i, j: (i, j)), out_shape=_sds((m, k), out_dtype),
        compiler_params=_params("parallel", "parallel"), name=name)(a, wb)


def mm_tn_oblk(name, a, b, n8, out_dtype, tk=2048):
    (t, k), nb = a.shape, b.shape[1] // n8
    tk = _row_tile(t, tk)
    return _gmm(name, a, b, grid=(nb, t // tk),
                a_blk=(tk, k), a_idx=lambda j, s: (s, 0), b_blk=(tk, n8), b_idx=lambda j, s: (s, j),
                o_blk=(None, k, n8), o_idx=lambda j, s: (j, 0, 0), out_shape=(nb, k, n8), out_dtype=out_dtype, dn=TN,
                acc_shape=(k, n8))


def rmsnorm_fwd(name, x, g):
    t, d = x.shape
    tr = _row_tile(t, 512)

    def body(x_ref, g_ref, o_ref):
        xv = x_ref[...]
        r = lax.rsqrt(jnp.mean(xv * xv, axis=-1, keepdims=True) + RMS_EPS)
        o_ref[...] = (xv * r * g_ref[...]).astype(o_ref.dtype)

    return pl.pallas_call(
        body, grid=(t // tr,),
        in_specs=[pl.BlockSpec((tr, d), lambda i: (i, 0)), pl.BlockSpec((1, d), lambda i: (0, 0))],
        out_specs=pl.BlockSpec((tr, d), lambda i: (i, 0)),
        out_shape=_sds((t, d), BF16), compiler_params=_params("parallel"), name=name)(x, g)


def resid_norm_fwd(name, x, m, g, g_next):
    t, d = x.shape
    tr = _row_tile(t, 512)
    chained = g_next is not None

    def body(*refs):
        x_ref, m_ref, g_ref = refs[:3]
        mv = m_ref[...]
        r = lax.rsqrt(jnp.mean(mv * mv, axis=-1, keepdims=True) + RMS_EPS)
        x1 = x_ref[...] + mv * r * g_ref[...]
        if chained:
            gn_ref, o_ref, h_ref = refs[3:]
            r1 = lax.rsqrt(jnp.mean(x1 * x1, axis=-1, keepdims=True) + RMS_EPS)
            h_ref[...] = (x1 * r1 * gn_ref[...]).astype(BF16)
        else:
            o_ref, = refs[3:]
        o_ref[...] = x1

    row = pl.BlockSpec((tr, d), lambda i: (i, 0))
    vec = pl.BlockSpec((1, d), lambda i: (0, 0))
    out = pl.pallas_call(
        body, grid=(t // tr,),
        in_specs=[row, row, vec] + ([vec] if chained else []),
        out_specs=[row, row] if chained else [row],
        out_shape=[_sds((t, d), F32)] + ([_sds((t, d), BF16)] if chained else []),
        compiler_params=_params("parallel"), name=name)(*([x, m, g] + ([g_next] if chained else [])))
    return (out[0], out[1]) if chained else (out[0], None)


def norm_bwd(name, dy, x, g, resid, out_dtype):
    t, d = x.shape
    tr = _row_tile(t, 512)
    has_res = resid is not None

    def body(*refs):
        if has_res:
            dy_ref, x_ref, g_ref, r_ref, dx_ref, dg_ref = refs
        else:
            dy_ref, x_ref, g_ref, dx_ref, dg_ref = refs
        i = pl.program_id(0)
        xv = x_ref[...]
        dyv = dy_ref[...].astype(F32)
        r = lax.rsqrt(jnp.mean(xv * xv, axis=-1, keepdims=True) + RMS_EPS)
        xh = xv * r
        dxh = dyv * g_ref[...]
        dx = r * (dxh - xh * jnp.mean(dxh * xh, axis=-1, keepdims=True))
        if has_res:
            dx = dx + r_ref[...]
        dx_ref[...] = dx.astype(dx_ref.dtype)
        part = jnp.sum(dyv * xh, axis=0, keepdims=True)

        @pl.when(i == 0)
        def _():
            dg_ref[...] = part

        @pl.when(i > 0)
        def _():
            dg_ref[...] += part

    row = pl.BlockSpec((tr, d), lambda i: (i, 0))
    vec = pl.BlockSpec((1, d), lambda i: (0, 0))
    ins = [dy, x, g] + ([resid] if has_res else [])
    return pl.pallas_call(
        body, grid=(t // tr,),
        in_specs=[row, row, vec] + ([row] if has_res else []),
        out_specs=[row, vec],
        out_shape=[_sds((t, d), out_dtype), _sds((1, d), F32)],
        compiler_params=_params("arbitrary"), name=name)(*ins)


def loss_grad(name, y, target):
    t, d = y.shape
    tr = _row_tile(t, 512)

    def body(y_ref, t_ref, dy_ref, s_ref):
        i = pl.program_id(0)
        err = y_ref[...] - t_ref[...]
        dy_ref[...] = err * (1.0 / d)
        part = jnp.sum(err * err, axis=0, keepdims=True)

        @pl.when(i == 0)
        def _():
            s_ref[...] = part

        @pl.when(i > 0)
        def _():
            s_ref[...] += part

    row = pl.BlockSpec((tr, d), lambda i: (i, 0))
    vec = pl.BlockSpec((1, d), lambda i: (0, 0))
    return pl.pallas_call(
        body, grid=(t // tr,), in_specs=[row, row], out_specs=[row, vec],
        out_shape=[_sds((t, d), F32), _sds((1, d), F32)],
        compiler_params=_params("arbitrary"), name=name)(y, target)


PAIR = 2 * HEAD_DIM
N_PAIRS = N_HEADS // 2
A_TQ = 512
A_UNROLL = 4


def _halves(x):
    lane = lax.broadcasted_iota(jnp.int32, x.shape, x.ndim - 1)
    zero = jnp.zeros_like(x)
    return jnp.where(lane < HEAD_DIM, x, zero), jnp.where(lane >= HEAD_DIM, x, zero)


def _merge(a, b):
    lane = lax.broadcasted_iota(jnp.int32, a.shape, a.ndim - 1)
    return jnp.where(lane < HEAD_DIM, a, b)


def _a_valid(c):
    col = lax.broadcasted_iota(jnp.int32, (CHUNK, BAND), 1)
    return col >= (N_LEFT - c) * CHUNK


def attn_a_fwd(name, proj, kp, vp, bias, q_blk):
    t = proj.shape[0]
    tq = _row_tile(t, A_TQ)
    ncs = tq // CHUNK
    un = math.gcd(A_UNROLL, ncs)

    def body(q_ref, k_ref, v_ref, b_ref, o_ref, l_ref):
        i = pl.program_id(1)

        def group(gg, carry):
            cs = [i * ncs + gg * un + u for u in range(un)]
            r0s = [pl.multiple_of((gg * un + u) * CHUNK, CHUNK) for u in range(un)]
            k0s = [pl.multiple_of(c * CHUNK, CHUNK) for c in cs]
            ss = []
            for u in range(un):
                qh = _halves(q_ref[pl.ds(r0s[u], CHUNK), :] * QK_SCALE)
                kwin = k_ref[pl.ds(k0s[u], BAND), :]
                valid = _a_valid(cs[u])
                for hh in range(2):
                    s = lax.dot_general(qh[hh], kwin, NT, preferred_element_type=F32) + b_ref[hh]
                    ss.append(jnp.where(valid, s, -1e30))
            ps, lses = [], []
            for s in ss:
                mx = jnp.max(s, axis=-1, keepdims=True)
                p = jnp.exp(s - mx)
                den = jnp.sum(p, axis=-1, keepdims=True)
                ps.append((p * (1.0 / den)).astype(BF16))
                lses.append(mx + jnp.log(den))
            for u in range(un):
                vwin = v_ref[pl.ds(k0s[u], BAND), :]
                o0 = jnp.dot(ps[2 * u], vwin, preferred_element_type=F32)
                o1 = jnp.dot(ps[2 * u + 1], vwin, preferred_element_type=F32)
                o_ref[pl.ds(r0s[u], CHUNK), :] = _merge(o0, o1)
                l_ref[pl.ds(r0s[u], CHUNK), :] = jnp.concatenate([lses[2 * u], lses[2 * u + 1]], axis=1)
            return carry

        lax.fori_loop(0, ncs // un, group, 0)

    return pl.pallas_call(
        body, grid=(N_PAIRS, t // tq),
        in_specs=[pl.BlockSpec((tq, PAIR), lambda p, i: (i, q_blk + p)),
                  pl.BlockSpec((t + PAD_KEYS, PAIR), lambda p, i: (0, p)),
                  pl.BlockSpec((t + PAD_KEYS, PAIR), lambda p, i: (0, p)),
                  pl.BlockSpec((2, CHUNK, BAND), lambda p, i: (p, 0, 0))],
        out_specs=[pl.BlockSpec((tq, PAIR), lambda p, i: (i, p)),
                   pl.BlockSpec((None, tq, 2), lambda p, i: (p, i, 0))],
        out_shape=[_sds((t, N_PAIRS * PAIR), F32), _sds((N_PAIRS, t, 2), F32)],
        compiler_params=_params("parallel", "parallel"), name=name)(proj, kp, vp, bias)


def attn_a_bwd(name, proj, kp, vp, bias, o, lse, do, q_blk, do_blk):
    t = proj.shape[0]
    tq = _row_tile(t, A_TQ)
    ncs = tq // CHUNK
    un = math.gcd(A_UNROLL, ncs)

    def body(q_ref, k_ref, v_ref, b_ref, o_ref, l_ref, do_ref, dq_ref, dk_ref, dv_ref, db_ref):
        i = pl.program_id(1)

        @pl.when(i == 0)
        def _():
            dk_ref[...] = jnp.zeros_like(dk_ref)
            dv_ref[...] = jnp.zeros_like(dv_ref)
            db_ref[...] = jnp.zeros_like(db_ref)

        def group(gg, carry):
            cs = [i * ncs + gg * un + u for u in range(un)]
            r0s = [pl.multiple_of((gg * un + u) * CHUNK, CHUNK) for u in range(un)]
            k0s = [pl.multiple_of(c * CHUNK, CHUNK) for c in cs]
            qhs, dohs, ps, dps, deltas = [], [], [], [], []
            for u in range(un):
                rows = pl.ds(r0s[u], CHUNK)
                qh = _halves(q_ref[rows, :] * QK_SCALE)
                doh = _halves(do_ref[rows, :])
                kwin = k_ref[pl.ds(k0s[u], BAND), :]
                vwin = v_ref[pl.ds(k0s[u], BAND), :]
                valid = _a_valid(cs[u])
                dl = _halves(do_ref[rows, :].astype(F32) * o_ref[rows, :])
                for hh in range(2):
                    s = lax.dot_general(qh[hh], kwin, NT, preferred_element_type=F32) + b_ref[hh]
                    ps.append(jnp.where(valid, jnp.exp(s - l_ref[rows, hh:hh + 1]), 0.0))
                    dps.append(lax.dot_general(doh[hh], vwin, NT, preferred_element_type=F32))
                    deltas.append(jnp.sum(dl[hh], axis=-1, keepdims=True))
                qhs.append(qh)
                dohs.append(doh)
            dss = [p * (dp - dl) for p, dp, dl in zip(ps, dps, deltas)]
            for hh in range(2):
                tot = dss[hh]
                for u in range(1, un):
                    tot = tot + dss[2 * u + hh]
                db_ref[hh] += tot
            for u in range(un):
                kwin = k_ref[pl.ds(k0s[u], BAND), :]
                ds0, ds1 = dss[2 * u].astype(BF16), dss[2 * u + 1].astype(BF16)
                dq_ref[pl.ds(r0s[u], CHUNK), :] = _merge(jnp.dot(ds0, kwin, preferred_element_type=F32),
                                                         jnp.dot(ds1, kwin, preferred_element_type=F32)) * QK_SCALE
                dk_ref[pl.ds(k0s[u], BAND), :] += (lax.dot_general(ds0, qhs[u][0], TN, preferred_element_type=F32)
                                                   + lax.dot_general(ds1, qhs[u][1], TN, preferred_element_type=F32))
                dv_ref[pl.ds(k0s[u], BAND), :] += (
                    lax.dot_general(ps[2 * u].astype(BF16), dohs[u][0], TN, preferred_element_type=F32)
                    + lax.dot_general(ps[2 * u + 1].astype(BF16), dohs[u][1], TN, preferred_element_type=F32))
            return carry

        lax.fori_loop(0, ncs // un, group, 0)

    tile = lambda blk: pl.BlockSpec((tq, PAIR), lambda p, i: (i, blk + p))
    whole = pl.BlockSpec((t + PAD_KEYS, PAIR), lambda p, i: (0, p))
    bspec = pl.BlockSpec((2, CHUNK, BAND), lambda p, i: (p, 0, 0))
    return pl.pallas_call(
        body, grid=(N_PAIRS, t // tq),
        in_specs=[tile(q_blk), whole, whole, bspec, tile(0), pl.BlockSpec((None, tq, 2), lambda p, i: (p, i, 0)),
                  tile(do_blk)],
        out_specs=[tile(0), whole, whole, bspec],
        out_shape=[_sds((t, N_PAIRS * PAIR), F32), _sds((t + PAD_KEYS, N_PAIRS * PAIR), F32),
                   _sds((t + PAD_KEYS, N_PAIRS * PAIR), F32), _sds((2 * N_PAIRS, CHUNK, BAND), F32)],
        compiler_params=_params("parallel", "arbitrary"), name=name)(proj, kp, vp, bias, o, lse, do)


SB_TQ = 256
SB_TK = 256
SB_DEAD = -125.0


def _tri(n, strict):
    j = lax.broadcasted_iota(jnp.int32, (n, n), 0)
    s = lax.broadcasted_iota(jnp.int32, (n, n), 1)
    return jnp.where((j > s) if strict else (j >= s), 1.0, 0.0).astype(BF16)


def _suffix_sum(x, tri, exact):
    hi = x.astype(BF16)
    out = jnp.dot(hi, tri, preferred_element_type=F32)
    if exact:
        lo = (x - hi.astype(F32)).astype(BF16)
        out = out + jnp.dot(lo, tri, preferred_element_type=F32)
    return out


def _sb_scores(qh, ks, causal):
    z = lax.dot_general(qh, ks, NT, preferred_element_type=F32)
    lb = jnp.minimum(z, 0.0) - jnp.log(1.0 + jnp.exp(-jnp.abs(z)))
    m = lb - z
    if causal is not None:
        m = jnp.where(causal, m, 0.0)
    return lb, m


def _causal(tq, tk, off):
    return (lax.broadcasted_iota(jnp.int32, (tq, tk), 1) + off * tk) < lax.broadcasted_iota(jnp.int32, (tq, tk), 0)


def sb_fwd(name, proj, q_blk, k_blk, v_blk):
    t = proj.shape[0]
    tq = _row_tile(t, SB_TQ)
    tk = min(SB_TK, tq)
    per = tq // tk

    def body(q_ref, k_ref, v_ref, o_ref):
        i = pl.program_id(1)
        tri = _tri(tk, True)
        qh = _halves(q_ref[...] * QK_SCALE)

        def blocks(kb, carry, off):
            k0 = pl.multiple_of(kb * tk, tk)
            ks, vs = k_ref[pl.ds(k0, tk), :], v_ref[pl.ds(k0, tk), :]
            causal = None if off is None else _causal(tq, tk, off)
            lbm = [_sb_scores(qh[hh], ks, causal) for hh in range(2)]
            afters = [_suffix_sum(lbm[hh][1], tri, False) for hh in range(2)]
            out = []
            for hh in range(2):
                acc, cm = carry[2 * hh], carry[2 * hh + 1]
                w = jnp.exp(lbm[hh][0] + afters[hh] + cm)
                if causal is not None:
                    w = jnp.where(causal, w, 0.0)
                out += [acc + jnp.dot(w.astype(BF16), vs, preferred_element_type=F32),
                        cm + jnp.sum(lbm[hh][1], axis=-1, keepdims=True)]
            return tuple(out)

        def alive(carry):
            return jnp.maximum(jnp.max(carry[1]), jnp.max(carry[3])) > SB_DEAD

        carry = (jnp.zeros((tq, PAIR), F32), jnp.zeros((tq, 1), F32)) * 2
        for off in reversed(range(per)):
            carry = blocks(i * per + off, carry, off)

        def step(c):
            new = blocks(i * per - 1 - c[0], c[2:], None)
            return (c[0] + 1, alive(new)) + new

        out = lax.while_loop(lambda c: jnp.logical_and(c[0] < i * per, c[1]), step,
                             (jnp.int32(0), alive(carry)) + carry)
        o_ref[...] = _merge(out[2], out[4])

    return pl.pallas_call(
        body, grid=(N_PAIRS, t // tq),
        in_specs=[pl.BlockSpec((tq, PAIR), lambda p, i: (i, q_blk + p)),
                  pl.BlockSpec((t, PAIR), lambda p, i: (0, k_blk + p)),
                  pl.BlockSpec((t, PAIR), lambda p, i: (0, v_blk + p))],
        out_specs=pl.BlockSpec((tq, PAIR), lambda p, i: (i, p)),
        out_shape=_sds((t, N_PAIRS * PAIR), F32), compiler_params=_params("parallel", "parallel"), name=name,
    )(proj, proj, proj)


def sb_bwd(name, proj, o, do, q_blk, k_blk, v_blk, do_blk):
    t = proj.shape[0]
    tq = _row_tile(t, SB_TQ)
    tk = min(SB_TK, tq)
    per = tq // tk

    def body(q_ref, k_ref, v_ref, o_ref, do_ref, dq_ref, dk_ref, dv_ref):
        i = pl.program_id(1)

        @pl.when(i == 0)
        def _():
            dk_ref[...] = jnp.zeros_like(dk_ref)
            dv_ref[...] = jnp.zeros_like(dv_ref)

        tri_s, tri_i = _tri(tk, True), _tri(tk, False)
        qh = _halves(q_ref[...] * QK_SCALE)
        doh = _halves(do_ref[...])
        deltas = [jnp.sum(x, axis=-1, keepdims=True) for x in _halves(do_ref[...].astype(F32) * o_ref[...])]

        def blocks(kb, carry, off):
            k0 = pl.multiple_of(kb * tk, tk)
            ks, vs = k_ref[pl.ds(k0, tk), :], v_ref[pl.ds(k0, tk), :]
            causal = None if off is None else _causal(tq, tk, off)
            lbm = [_sb_scores(qh[hh], ks, causal) for hh in range(2)]
            dws = [lax.dot_general(doh[hh], vs, NT, preferred_element_type=F32) for hh in range(2)]
            afters = [_suffix_sum(lbm[hh][1], tri_s, False) for hh in range(2)]
            wbs, es = [], []
            for hh in range(2):
                w = jnp.exp(lbm[hh][0] + afters[hh] + carry[3 * hh + 1])
                if causal is not None:
                    w = jnp.where(causal, w, 0.0)
                wbs.append(w.astype(BF16))
                es.append(wbs[hh].astype(F32) * dws[hh])
            sfx = [_suffix_sum(es[hh], tri_i, True) for hh in range(2)]
            dzs = []
            for hh in range(2):
                left = deltas[hh] - (sfx[hh] + carry[3 * hh + 2])
                sig = jnp.exp(lbm[hh][0])
                dz = es[hh] * (1.0 - sig) - left * sig
                if causal is not None:
                    dz = jnp.where(causal, dz, 0.0)
                dzs.append(dz.astype(BF16))
            dk_ref[pl.ds(k0, tk), :] += (lax.dot_general(dzs[0], qh[0], TN, preferred_element_type=F32)
                                         + lax.dot_general(dzs[1], qh[1], TN, preferred_element_type=F32))
            dv_ref[pl.ds(k0, tk), :] += (lax.dot_general(wbs[0], doh[0], TN, preferred_element_type=F32)
                                         + lax.dot_general(wbs[1], doh[1], TN, preferred_element_type=F32))
            out = []
            for hh in range(2):
                out += [carry[3 * hh] + jnp.dot(dzs[hh], ks, preferred_element_type=F32),
                        carry[3 * hh + 1] + jnp.sum(lbm[hh][1], axis=-1, keepdims=True),
                        carry[3 * hh + 2] + jnp.sum(es[hh], axis=-1, keepdims=True)]
            return tuple(out)

        def alive(carry):
            return jnp.maximum(jnp.max(carry[1]), jnp.max(carry[4])) > SB_DEAD

        zero = jnp.zeros((tq, 1), F32)
        carry = (jnp.zeros((tq, PAIR), F32), zero, zero) * 2
        for off in reversed(range(per)):
            carry = blocks(i * per + off, carry, off)

        def step(c):
            new = blocks(i * per - 1 - c[0], c[2:], None)
            return (c[0] + 1, alive(new)) + new

        out = lax.while_loop(lambda c: jnp.logical_and(c[0] < i * per, c[1]), step,
                             (jnp.int32(0), alive(carry)) + carry)
        dq_ref[...] = _merge(out[2], out[5]) * QK_SCALE

    tile = lambda blk: pl.BlockSpec((tq, PAIR), lambda p, i: (i, blk + p))
    whole = lambda blk: pl.BlockSpec((t, PAIR), lambda p, i: (0, blk + p))
    return pl.pallas_call(
        body, grid=(N_PAIRS, t // tq),
        in_specs=[tile(q_blk), whole(k_blk), whole(v_blk), tile(0), tile(do_blk)],
        out_specs=[tile(0), whole(0), whole(0)],
        out_shape=[_sds((t, N_PAIRS * PAIR), F32)] * 3,
        compiler_params=_params("parallel", "arbitrary"), name=name)(proj, proj, proj, o, do)


def _sigmoid(x):
    return 1.0 / (1.0 + jnp.exp(-x))


def gu_gap(f8):
    return -(-f8 // 128) * 128


def merge_gu(gate, up):
    f8 = gate.shape[-1]
    pad = jnp.zeros(gate.shape[:-1] + (gu_gap(f8) - f8,), gate.dtype)
    return jnp.concatenate([gate, pad, up], axis=-1)


def split_gu(gu, f8):
    return gu[..., :f8], gu[..., gu_gap(f8):]


def ffn_up(name, h, wgu, f8, layer):
    t, d = h.shape
    nb, _, _, fw = wgu.shape
    gap = gu_gap(f8)
    tm = _row_tile(t, 1024)

    per = 2

    def body(h_ref, w_ref, gu_ref, a_ref):
        hv = h_ref[...]
        rs = [jnp.dot(hv, w_ref[j], preferred_element_type=F32) for j in range(per)]
        for j in range(per):
            gu_ref[j] = rs[j].astype(BF16)
            g, u = rs[j][:, :f8], rs[j][:, gap:]
            a_ref[j] = (g * _sigmoid(g) * u).astype(BF16)

    return pl.pallas_call(
        body, grid=(t // tm, nb // per),
        in_specs=[pl.BlockSpec((tm, d), lambda i, k: (i, 0)),
                  pl.BlockSpec((per, None, d, fw), lambda i, k: (k, layer, 0, 0))],
        out_specs=[pl.BlockSpec((per, tm, fw), lambda i, k: (k, i, 0)),
                   pl.BlockSpec((per, tm, f8), lambda i, k: (k, i, 0))],
        out_shape=[_sds((nb, t, fw), BF16), _sds((nb, t, f8), BF16)],
        compiler_params=_params("parallel", "parallel"), name=name)(h, wgu)


def _mm_all_blocks(name, a, w, layer, dn, tm):
    nb, t, f = a.shape
    wshape = w.shape[2:]
    d = wshape[1] if dn == NN else wshape[0]
    tm = _row_tile(t, tm)

    def body(a_ref, w_ref, o_ref):
        acc = lax.dot_general(a_ref[0], w_ref[0], dn, preferred_element_type=F32)
        for k in range(1, nb):
            acc = acc + lax.dot_general(a_ref[k], w_ref[k], dn, preferred_element_type=F32)
        o_ref[...] = acc

    return pl.pallas_call(
        body, grid=(t // tm,),
        in_specs=[pl.BlockSpec((nb, tm, f), lambda i: (0, i, 0)),
                  pl.BlockSpec((nb, None) + wshape, lambda i: (0, layer, 0, 0))],
        out_specs=pl.BlockSpec((tm, d), lambda i: (i, 0)), out_shape=_sds((t, d), F32),
        compiler_params=_params("parallel"), name=name)(a, w)


def ffn_down(name, a, wd, layer):
    return _mm_all_blocks(name, a, wd, layer, NN, 512)


def ffn_bwd_act(name, dm, wd, gu, f8, layer):
    t, d = dm.shape
    nb, _, fw = gu.shape
    gap = gu_gap(f8)
    tm = _row_tile(t, 1024)

    per = 2

    def body(dm_ref, wd_ref, gu_ref, o_ref):
        dmv = dm_ref[...]
        das = [lax.dot_general(dmv, wd_ref[j], NT, preferred_element_type=F32) for j in range(per)]
        o_ref[...] = jnp.zeros_like(o_ref)
        for j in range(per):
            gv = gu_ref[j, :, :f8].astype(F32)
            uv = gu_ref[j, :, gap:].astype(F32)
            sg = _sigmoid(gv)
            o_ref[j, :, :f8] = (das[j] * uv * sg * (1.0 + gv * (1.0 - sg))).astype(BF16)
            o_ref[j, :, gap:] = (das[j] * gv * sg).astype(BF16)

    bspec = pl.BlockSpec((per, tm, fw), lambda i, k: (k, i, 0))
    return pl.pallas_call(
        body, grid=(t // tm, nb // per),
        in_specs=[pl.BlockSpec((tm, d), lambda i, k: (i, 0)),
                  pl.BlockSpec((per, None, f8, d), lambda i, k: (k, layer, 0, 0)), bspec],
        out_specs=bspec, out_shape=_sds((nb, t, fw), BF16),
        compiler_params=_params("parallel", "parallel"), name=name)(dm, wd, gu)


def ffn_bwd_dh(name, dgu, wgu, layer):
    return _mm_all_blocks(name, dgu, wgu, layer, NT, 512)


def ffn_dw_in(name, h, dact):
    t, d = h.shape
    nb, _, f8 = dact.shape
    tk = _row_tile(t, 2048)
    return _gmm(name, h, dact, grid=(nb, t // tk),
                a_blk=(tk, d), a_idx=lambda b, s: (s, 0), b_blk=(None, tk, f8), b_idx=lambda b, s: (b, s, 0),
                o_blk=(None, d, f8), o_idx=lambda b, s: (b, 0, 0), out_shape=(nb, d, f8), out_dtype=F32, dn=TN,
                acc_shape=(d, f8))


def ffn_dw_down(name, a, dm):
    nb, t, f8 = a.shape
    d = dm.shape[1]
    tk = _row_tile(t, 2048)
    return _gmm(name, a, dm, grid=(nb, t // tk),
                a_blk=(None, tk, f8), a_idx=lambda b, s: (b, s, 0), b_blk=(tk, d), b_idx=lambda b, s: (s, 0),
                o_blk=(None, f8, d), o_idx=lambda b, s: (b, 0, 0), out_shape=(nb, f8, d), out_dtype=F32, dn=TN,
                acc_shape=(f8, d))


GELU_C = math.sqrt(2.0 / math.pi)
GELU_A = 0.044715


def _gelu(x):
    return 0.5 * x * (1.0 + jnp.tanh(GELU_C * (x + GELU_A * x * x * x)))


def _gelu_grad(x):
    th = jnp.tanh(GELU_C * (x + GELU_A * x * x * x))
    return 0.5 * (1.0 + th) + 0.5 * x * (1.0 - th * th) * GELU_C * (1.0 + 3.0 * GELU_A * x * x)


def _neg_expm1(x):
    series = x * (1.0 + x * (0.5 + x * (1.0 / 6.0 + x * (1.0 / 24.0 + x * (1.0 / 120.0 + x * (1.0 / 720.0))))))
    return -jnp.where(x > -0.25, series, jnp.exp(x) - 1.0)


CONV_TR = 256
CONV_TAPS = 4
HALO = 8


def _shifted(ext, k, tr, back):
    if back:
        return pltpu.roll(ext, k, 0)[HALO:, :] if k else ext[HALO:, :]
    return pltpu.roll(ext, tr + HALO - k, 0)[:tr, :] if k else ext[:tr, :]


def conv4_fwd(name, src, cb, w, b):
    t, c = src.shape[0], w.shape[1]
    tr = _row_tile(t, CONV_TR)
    hb = tr // HALO

    def body(x_ref, h_ref, w_ref, b_ref, o_ref):
        i = pl.program_id(0)
        ext = jnp.concatenate([jnp.where(i == 0, 0.0, h_ref[...]), x_ref[...]], axis=0)
        acc = b_ref[...]
        for k in range(CONV_TAPS):
            acc = acc + w_ref[CONV_TAPS - 1 - k:CONV_TAPS - k, :] * _shifted(ext, k, tr, True)
        o_ref[...] = acc

    return pl.pallas_call(
        body, grid=(t // tr,),
        in_specs=[pl.BlockSpec((tr, c), lambda i: (i, cb)),
                  pl.BlockSpec((HALO, c), lambda i: (jnp.maximum(i * hb - 1, 0), cb)),
                  pl.BlockSpec((CONV_TAPS, c), lambda i: (0, 0)), pl.BlockSpec((1, c), lambda i: (0, 0))],
        out_specs=pl.BlockSpec((tr, c), lambda i: (i, 0)), out_shape=_sds((t, c), F32),
        compiler_params=_params("parallel"), name=name)(src, src, w, b)


def conv4_bwd_x(name, dy, w):
    t, c = dy.shape
    tr = _row_tile(t, CONV_TR)
    hb = tr // HALO
    last = t // tr - 1

    def body(y_ref, h_ref, w_ref, o_ref):
        i = pl.program_id(0)
        ext = jnp.concatenate([y_ref[...], jnp.where(i == last, 0.0, h_ref[...])], axis=0)
        acc = w_ref[CONV_TAPS - 1:CONV_TAPS, :] * y_ref[...]
        for k in range(1, CONV_TAPS):
            acc = acc + w_ref[CONV_TAPS - 1 - k:CONV_TAPS - k, :] * _shifted(ext, k, tr, False)
        o_ref[...] = acc

    return pl.pallas_call(
        body, grid=(t // tr,),
        in_specs=[pl.BlockSpec((tr, c), lambda i: (i, 0)),
                  pl.BlockSpec((HALO, c), lambda i: (jnp.minimum((i + 1) * hb, t // HALO - 1), 0)),
                  pl.BlockSpec((CONV_TAPS, c), lambda i: (0, 0))],
        out_specs=pl.BlockSpec((tr, c), lambda i: (i, 0)), out_shape=_sds((t, c), F32),
        compiler_params=_params("parallel"), name=name)(dy, dy, w)


def conv4_bwd_w(name, src, cb, dy):
    t, c = dy.shape
    tr = _row_tile(t, CONV_TR)
    hb = tr // HALO

    def body(x_ref, h_ref, dy_ref, dw_ref, db_ref):
        i = pl.program_id(0)

        @pl.when(i == 0)
        def _():
            dw_ref[...] = jnp.zeros_like(dw_ref)
            db_ref[...] = jnp.zeros_like(db_ref)

        ext = jnp.concatenate([jnp.where(i == 0, 0.0, h_ref[...]), x_ref[...]], axis=0)
        dyv = dy_ref[...]
        db_ref[...] += jnp.sum(dyv, axis=0, keepdims=True)
        for k in range(CONV_TAPS):
            dw_ref[CONV_TAPS - 1 - k:CONV_TAPS - k, :] += jnp.sum(dyv * _shifted(ext, k, tr, True), axis=0,
                                                                   keepdims=True)

    return pl.pallas_call(
        body, grid=(t // tr,),
        in_specs=[pl.BlockSpec((tr, c), lambda i: (i, cb)),
                  pl.BlockSpec((HALO, c), lambda i: (jnp.maximum(i * hb - 1, 0), cb)),
                  pl.BlockSpec((tr, c), lambda i: (i, 0))],
        out_specs=[pl.BlockSpec((CONV_TAPS, c), lambda i: (0, 0)), pl.BlockSpec((1, c), lambda i: (0, 0))],
        out_shape=[_sds((CONV_TAPS, c), F32), _sds((1, c), F32)],
        compiler_params=_params("arbitrary"), name=name)(src, src, dy)


def _rg_gate_values(xcv, wa_ref, wi_ref, ba_ref, bi_ref, lam_ref):
    xb = xcv.astype(BF16)
    r = _sigmoid(jnp.dot(xb, wa_ref[...], preferred_element_type=F32) + ba_ref[...])
    ig = _sigmoid(jnp.dot(xb, wi_ref[...], preferred_element_type=F32) + bi_ref[...])
    lam = lam_ref[...]
    sp = jnp.maximum(-lam, 0.0) + jnp.log(1.0 + jnp.exp(-jnp.abs(lam)))
    log_a = -LRU_C * r * sp
    a = jnp.exp(log_a)
    mult = jnp.sqrt(_neg_expm1(2.0 * log_a))
    return xb, r, ig, sp, a, mult


def rg_gates_fwd(name, xc, wa, wi, ba, bi, lam):
    t, c = xc.shape
    nb, cb, _ = wa.shape
    tm = _row_tile(t, 512)

    def body(xc_ref, wa_ref, wi_ref, ba_ref, bi_ref, lam_ref, a_ref, u_ref):
        xcv = xc_ref[...]
        _, _, ig, _, a, mult = _rg_gate_values(xcv, wa_ref, wi_ref, ba_ref, bi_ref, lam_ref)
        a_ref[...] = a
        u_ref[...] = mult * (ig * xcv)

    blk = pl.BlockSpec((tm, cb), lambda i, n: (i, n))
    wsp = pl.BlockSpec((None, cb, cb), lambda i, n: (n, 0, 0))
    vec = pl.BlockSpec((1, cb), lambda i, n: (0, n))
    return pl.pallas_call(
        body, grid=(t // tm, nb), in_specs=[blk, wsp, wsp, vec, vec, vec], out_specs=[blk, blk],
        out_shape=[_sds((t, c), F32)] * 2, compiler_params=_params("parallel", "parallel"), name=name,
    )(xc, wa, wi, ba, bi, lam)


def rg_gates_bwd(name, xc, gu, hprev, wa, wi, ba, bi, lam):
    t, c = xc.shape
    nb, cb, _ = wa.shape
    tm = _row_tile(t, 512)

    def body(xc_ref, gu_ref, hp_ref, wa_ref, wi_ref, ba_ref, bi_ref, lam_ref,
             dxc_ref, dwa_ref, dwi_ref, dba_ref, dbi_ref, dlam_ref):
        i = pl.program_id(1)

        @pl.when(i == 0)
        def _():
            for ref in (dwa_ref, dwi_ref, dba_ref, dbi_ref, dlam_ref):
                ref[...] = jnp.zeros_like(ref)

        xcv = xc_ref[...]
        xb, r, ig, sp, a, mult = _rg_gate_values(xcv, wa_ref, wi_ref, ba_ref, bi_ref, lam_ref)
        gv = gu_ref[...]
        d_ixc = gv * mult
        d_i = d_ixc * xcv
        d_mult = gv * ig * xcv
        d_a = gv * hp_ref[...] - d_mult * a / mult
        d_log_a = d_a * a
        d_r = d_log_a * (-LRU_C * sp)
        sig_neg_lam = 1.0 / (1.0 + jnp.exp(lam_ref[...]))
        dlam_ref[...] += jnp.sum(d_log_a * r, axis=0, keepdims=True) * (LRU_C * sig_neg_lam)
        dpa = d_r * r * (1.0 - r)
        dpi = d_i * ig * (1.0 - ig)
        dba_ref[...] += jnp.sum(dpa, axis=0, keepdims=True)
        dbi_ref[...] += jnp.sum(dpi, axis=0, keepdims=True)
        dpab, dpib = dpa.astype(BF16), dpi.astype(BF16)
        dxc_ref[...] = (d_ixc * ig + lax.dot_general(dpab, wa_ref[...], NT, preferred_element_type=F32)
                        + lax.dot_general(dpib, wi_ref[...], NT, preferred_element_type=F32))
        dwa_ref[...] += lax.dot_general(xb, dpab, TN, preferred_element_type=F32)
        dwi_ref[...] += lax.dot_general(xb, dpib, TN, preferred_element_type=F32)

    blk = pl.BlockSpec((tm, cb), lambda n, i: (i, n))
    wsp = pl.BlockSpec((None, cb, cb), lambda n, i: (n, 0, 0))
    vec = pl.BlockSpec((1, cb), lambda n, i: (0, n))
    return pl.pallas_call(
        body, grid=(nb, t // tm), in_specs=[blk, blk, blk, wsp, wsp, vec, vec, vec],
        out_specs=[blk, wsp, wsp, vec, vec, vec],
        out_shape=[_sds((t, c), F32), _sds((nb, cb, cb), F32), _sds((nb, cb, cb), F32),
                   _sds((1, c), F32), _sds((1, c), F32), _sds((1, c), F32)],
        compiler_params=_params("parallel", "arbitrary"), name=name)(xc, gu, hprev, wa, wi, ba, bi, lam)


SCAN_TS = 256
SCAN_TC = 512


def _tile_scan(a, b, reverse):
    ts = a.shape[0]
    row = lax.broadcasted_iota(jnp.int32, a.shape, 0)
    d = 1
    while d < ts:
        if reverse:
            inside = row < ts - d
            a_sh = jnp.where(inside, pltpu.roll(a, ts - d, 0), 1.0)
            b_sh = jnp.where(inside, pltpu.roll(b, ts - d, 0), 0.0)
        else:
            inside = row >= d
            a_sh = jnp.where(inside, pltpu.roll(a, d, 0), 1.0)
            b_sh = jnp.where(inside, pltpu.roll(b, d, 0), 0.0)
        b = b + a * b_sh
        a = a * a_sh
        d *= 2
    return a, b


def rg_scan_fwd(name, a, u, gate_pre):
    t, c = a.shape
    ts, tc = _row_tile(t, SCAN_TS), _row_tile(c, SCAN_TC)

    def body(a_ref, u_ref, g_ref, h_ref, z_ref, carry_ref):
        s = pl.program_id(1)

        @pl.when(s == 0)
        def _():
            carry_ref[...] = jnp.zeros_like(carry_ref)

        ac, bc = _tile_scan(a_ref[...], u_ref[...], False)
        h = bc + ac * carry_ref[0:1, :]
        h_ref[...] = h
        z_ref[...] = (h * _gelu(g_ref[...])).astype(BF16)
        carry_ref[0:1, :] = h[ts - 1:ts, :]

    blk = pl.BlockSpec((ts, tc), lambda j, s: (s, j))
    return pl.pallas_call(
        body, grid=(c // tc, t // ts), in_specs=[blk, blk, blk], out_specs=[blk, blk],
        out_shape=[_sds((t, c), F32), _sds((t, c), BF16)], scratch_shapes=[pltpu.VMEM((8, tc), F32)],
        compiler_params=_params("parallel", "arbitrary"), name=name)(a, u, gate_pre)


def rg_scan_bwd(name, a_next, hs, gate_pre, dz):
    t, c = hs.shape
    ts, tc = _row_tile(t, SCAN_TS), _row_tile(c, SCAN_TC)
    nt = t // ts

    def body(an_ref, h_ref, g_ref, dz_ref, gu_ref, dgate_ref, carry_ref):
        s = pl.program_id(1)

        @pl.when(s == 0)
        def _():
            carry_ref[...] = jnp.zeros_like(carry_ref)

        gate = g_ref[...]
        dzv = dz_ref[...]
        dgate_ref[...] = (dzv * h_ref[...] * _gelu_grad(gate)).astype(BF16)
        ac, bc = _tile_scan(an_ref[...], dzv * _gelu(gate), True)
        gu = bc + ac * carry_ref[0:1, :]
        gu_ref[...] = gu
        carry_ref[0:1, :] = gu[0:1, :]

    blk = pl.BlockSpec((ts, tc), lambda j, s: (nt - 1 - s, j))
    return pl.pallas_call(
        body, grid=(c // tc, nt), in_specs=[blk, blk, blk, blk], out_specs=[blk, blk],
        out_shape=[_sds((t, c), F32), _sds((t, c), BF16)], scratch_shapes=[pltpu.VMEM((8, tc), F32)],
        compiler_params=_params("parallel", "arbitrary"), name=name)(a_next, hs, gate_pre, dz)


def _shift_down(x, k):
    return jnp.pad(x, ((k, 0), (0, 0)))[:x.shape[0]] if k else x


def _shift_up(x, k):
    return jnp.pad(x, ((0, k), (0, 0)))[k:] if k else x


QA_BLK, KA_BLK, VA_BLK, QS_BLK, KS_BLK, VS_BLK = (g * N_PAIRS for g in range(6))


TOEP_W = 640
TOEP_FLAT = 320
TABLE_LOW = 193


def rel_bias_matrix(name, table):
    h = table.shape[0]
    diag = jnp.concatenate([jnp.repeat(table[:, 2 * REL_CLIP:], TOEP_FLAT, axis=1),
                            jnp.flip(table[:, TABLE_LOW:2 * REL_CLIP], axis=1),
                            jnp.zeros((h, 1), table.dtype)], axis=1)[:, None, :]

    def body(v_ref, o_ref):
        rows = jnp.broadcast_to(v_ref[...], (CHUNK, TOEP_W))
        o_ref[...] = pltpu.roll(rows, TOEP_W - (CHUNK - 1), 1, stride=1, stride_axis=0)

    out = pl.pallas_call(
        body, grid=(h,), in_specs=[pl.BlockSpec((None, 1, TOEP_W), lambda hh: (hh, 0, 0))],
        out_specs=pl.BlockSpec((None, CHUNK, TOEP_W), lambda hh: (hh, 0, 0)),
        out_shape=_sds((h, CHUNK, TOEP_W), F32), compiler_params=_params("parallel"), name=name)(diag)
    return out[:, :, :BAND]


def rel_bias_grad(name, dbias):
    h = dbias.shape[0]
    flipped = jnp.pad(jnp.flip(dbias, axis=1), ((0, 0), (0, 0), (0, TOEP_W - BAND)))

    def body(x_ref, o_ref):
        skew = pltpu.roll(x_ref[...], 0, 1, stride=1, stride_axis=0)
        col = jnp.sum(skew, axis=0, keepdims=True)
        lane = lax.broadcasted_iota(jnp.int32, col.shape, 1)
        flat = jnp.sum(jnp.where(lane < TOEP_FLAT, col, 0.0), axis=1, keepdims=True)
        o_ref[...] = jnp.where(lane == TOEP_W - 1, flat, col)

    out = pl.pallas_call(
        body, grid=(h,), in_specs=[pl.BlockSpec((None, CHUNK, TOEP_W), lambda hh: (hh, 0, 0))],
        out_specs=pl.BlockSpec((None, 1, TOEP_W), lambda hh: (hh, 0, 0)),
        out_shape=_sds((h, 1, TOEP_W), F32), compiler_params=_params("parallel"), name=name)(flipped)[:, 0, :]
    return jnp.concatenate([jnp.zeros((h, TABLE_LOW), F32), jnp.flip(out[:, TOEP_FLAT:TOEP_W - 1], axis=1),
                            out[:, TOEP_W - 1:]], axis=1)


def attn_layer_fwd(tag, x, h, w, g_next):
    proj = mm_nn_wblk(tag + "_proj", h, w["w_in"], w["idx"], BF16)
    width = N_PAIRS * PAIR
    pad = lambda a: jnp.pad(a, ((PAD_KEYS, 0), (0, 0)))
    kap, vap = pad(proj[:, width:2 * width]), pad(proj[:, 2 * width:3 * width])
    bias = rel_bias_matrix(tag + "_bias", w["rel_bias"])
    oa, lse = attn_a_fwd(tag + "_a", proj, kap, vap, bias, QA_BLK)
    ob = sb_fwd(tag + "_sb", proj, QS_BLK, KS_BLK, VS_BLK)
    o = jnp.concatenate([oa, ob], axis=1).astype(BF16)
    m = mm_nn(tag + "_out", o, w["w_out"], F32)
    x1, h_next = resid_norm_fwd(tag + "_res", x, m, w["g_post"], g_next)
    return x1, h_next, (x, h, proj, kap, vap, bias, oa, lse, ob, o, m)


def attn_layer_bwd(tag, dx1, saved, w):
    x, h, proj, kap, vap, bias, oa, lse, ob, o, m = saved
    dm, dg_post = norm_bwd(tag + "_dpost", dx1, m, w["g_post"], None, BF16)
    d_w_out = mm_tn(tag + "_dwout", o, dm, F32)
    do = mm_nt(tag + "_do", dm, w["w_out"], BF16)
    dqa, dkap, dvap, dbias = attn_a_bwd(tag + "_da", proj, kap, vap, bias, oa, lse, do, QA_BLK, 0)
    dqs, dks, dvs = sb_bwd(tag + "_dsb", proj, ob, do, QS_BLK, KS_BLK, VS_BLK, N_PAIRS)
    d_rel = rel_bias_grad(tag + "_dbias", dbias)
    dproj = jnp.concatenate([dqa, dkap[PAD_KEYS:], dvap[PAD_KEYS:], dqs, dks, dvs], axis=1).astype(BF16)
    d_w_in = mm_tn_oblk(tag + "_dwin", h, dproj, w["w_in"].shape[3], F32)
    dh = mm_nt_wblk(tag + "_dh", dproj, w["w_in"], w["idx"], F32)
    dx, dg_pre = norm_bwd(tag + "_dpre", dh, x, w["g_pre"], dx1, F32)
    return dx, dict(w_in=d_w_in, w_out=d_w_out, rel_bias=d_rel, g_pre=dg_pre, g_post=dg_post)


def rg_layer_fwd(tag, x, h, w, g_next):
    proj = mm_nn_wblk(tag + "_proj", h, w["w_in"], w["idx"], F32)
    xc = conv4_fwd(tag + "_conv", proj, 1, w["conv_w"], w["conv_b"])
    a, u = rg_gates_fwd(tag + "_gates", xc, w["w_a"], w["w_i"], w["b_a"], w["b_i"], w["lam"])
    hs, z = rg_scan_fwd(tag + "_scan", a, u, proj)
    m = mm_nn(tag + "_out", z, w["w_out"], F32)
    x1, h_next = resid_norm_fwd(tag + "_res", x, m, w["g_post"], g_next)
    return x1, h_next, (x, h, proj, xc, a, hs, z, m)


def rg_layer_bwd(tag, dx1, saved, w):
    x, h, proj, xc, a, hs, z, m = saved
    dm, dg_post = norm_bwd(tag + "_dpost", dx1, m, w["g_post"], None, BF16)
    d_w_out = mm_tn(tag + "_dwout", z, dm, F32)
    dz = mm_nt(tag + "_dz", dm, w["w_out"], F32)
    gu, dgate = rg_scan_bwd(tag + "_dscan", _shift_up(a, 1), hs, proj, dz)
    dxc, d_w_a, d_w_i, d_b_a, d_b_i, d_lam = rg_gates_bwd(
        tag + "_dgates", xc, gu, _shift_down(hs, 1), w["w_a"], w["w_i"], w["b_a"], w["b_i"], w["lam"])
    d_conv_w, d_conv_b = conv4_bwd_w(tag + "_dconvw", proj, 1, dxc)
    dxr = conv4_bwd_x(tag + "_dconv", dxc, w["conv_w"])
    dproj = jnp.concatenate([dgate, dxr.astype(BF16)], axis=1)
    d_w_in = mm_tn_oblk(tag + "_dwin", h, dproj, w["w_in"].shape[3], F32)
    dh = mm_nt_wblk(tag + "_dh", dproj, w["w_in"], w["idx"], F32)
    dx, dg_pre = norm_bwd(tag + "_dpre", dh, x, w["g_pre"], dx1, F32)
    return dx, dict(w_in=d_w_in, w_out=d_w_out, conv_w=d_conv_w, conv_b=d_conv_b, w_a=d_w_a, w_i=d_w_i,
                    b_a=d_b_a, b_i=d_b_i, lam=d_lam, g_pre=dg_pre, g_post=dg_post)


def ffn_layer_fwd(tag, x, h, w, g_next):
    f8 = w["w_down"].shape[2]
    gu, a = ffn_up(tag + "_up", h, w["w_gu"], f8, w["idx"])
    f = ffn_down(tag + "_down", a, w["w_down"], w["idx"])
    x1, h_next = resid_norm_fwd(tag + "_res", x, f, w["g_post"], g_next)
    return x1, h_next, (x, h, gu, a, f)


def ffn_layer_bwd(tag, dx1, saved, w):
    x, h, gu, a, f = saved
    f8 = w["w_down"].shape[2]
    dm, dg_post = norm_bwd(tag + "_dpost", dx1, f, w["g_post"], None, BF16)
    d_w_down = ffn_dw_down(tag + "_dwdown", a, dm)
    dgu = ffn_bwd_act(tag + "_dact", dm, w["w_down"], gu, f8, w["idx"])
    d_w_gu = ffn_dw_in(tag + "_dwgu", h, dgu)
    dh = ffn_bwd_dh(tag + "_dh", dgu, w["w_gu"], w["idx"])
    dx, dg_pre = norm_bwd(tag + "_dpre", dh, x, w["g_pre"], dx1, F32)
    return dx, dict(w_gu=d_w_gu, w_down=d_w_down, g_pre=dg_pre, g_post=dg_post)


def _place():
    return lax.axis_index("x"), lax.axis_index("y"), lax.axis_index("c")


def all_gather(name, blks):
    n = len(blks)

    def body(*refs):
        x_refs, out_refs = refs[:n], refs[n:2 * n]
        send_sems, recv_sems, local_sems = refs[2 * n:]
        x, y, cc = _place()
        me, sibling = (x, y, cc), (x, y, 1 - cc)
        chips = [(1 - x, y), (x, 1 - y), (1 - x, 1 - y)]
        south = cc == 0
        via = (jnp.where(south, 1 - x, x), jnp.where(south, y, 1 - y))
        onward = (jnp.where(south, x, 1 - x), jnp.where(south, 1 - y, y))
        k_via, k_onward = 1 + cc, 2 - cc

        def slot(a, px, py, pc):
            return out_refs[a].at[4 * px + 2 * py + pc]

        def copy(a, k, block, to, src=None):
            return pltpu.make_async_remote_copy(
                src_ref=slot(a, *block) if src is None else src, dst_ref=slot(a, *block),
                send_sem=send_sems.at[7 * a + k], recv_sem=recv_sems.at[7 * a + k], device_id=to, device_id_type=MESH)

        mine = [pltpu.make_async_copy(x_refs[a], slot(a, *me), local_sems.at[a]) for a in range(n)]
        sends = []
        for a in range(n):
            mine[a].start()
            sends.append(copy(a, 0, me, sibling, src=x_refs[a]))
            sends += [copy(a, 1 + j, me, (*chips[j], cc), src=x_refs[a]) for j in range(2)]
        for cp in sends:
            cp.start()
        for a in range(n):
            copy(a, k_via, (*via, cc), me).wait_recv()
            sends.append(copy(a, 3, (*via, cc), (*onward, cc)))
            sends.append(copy(a, 3 + k_via, (*via, cc), sibling))
            sends[-2].start()
            sends[-1].start()
        for a in range(n):
            copy(a, k_onward, (*onward, cc), me).wait_recv()
            sends.append(copy(a, 3 + k_onward, (*onward, cc), sibling))
            sends[-1].start()
        for a in range(n):
            copy(a, 3, (*chips[2], cc), me).wait_recv()
            sends.append(copy(a, 6, (*chips[2], cc), sibling))
            sends[-1].start()
        for a in range(n):
            copy(a, 0, sibling, me).wait_recv()
            for j, chip in enumerate(chips):
                copy(a, 4 + j, (*chip, 1 - cc), me).wait_recv()
        for cp in sends:
            cp.wait_send()
        for cp in mine:
            cp.wait()

    return pl.pallas_call(
        body, out_shape=[_sds((N_DEV,) + b.shape, b.dtype) for b in blks], in_specs=[ANY] * n, out_specs=[ANY] * n,
        scratch_shapes=[pltpu.SemaphoreType.DMA((7 * n,)), pltpu.SemaphoreType.DMA((7 * n,)),
                        pltpu.SemaphoreType.DMA((n,))],
        name=name)(*blks)


def exchange_pair(name, gs):
    n = len(gs)
    nchip = 4

    def body(*refs):
        g_refs, land_refs = refs[:n], refs[n:2 * n]
        send_sems, recv_sems = refs[2 * n:]
        x, y, cc = _place()
        copies = [pltpu.make_async_remote_copy(
            src_ref=g_refs[a].at[j, 1 - cc], dst_ref=land_refs[a].at[j], send_sem=send_sems.at[nchip * a + j],
            recv_sem=recv_sems.at[nchip * a + j], device_id=(x, y, 1 - cc), device_id_type=MESH)
            for a in range(n) for j in range(nchip)]
        for cp in copies:
            cp.start()
        for cp in copies:
            cp.wait()

    return pl.pallas_call(
        body, out_shape=[_sds((nchip,) + g.shape[2:], g.dtype) for g in gs], in_specs=[ANY] * n, out_specs=[ANY] * n,
        scratch_shapes=[pltpu.SemaphoreType.DMA((nchip * n,)), pltpu.SemaphoreType.DMA((nchip * n,))],
        name=name)(*gs)


def pair_sum(name, g, land, core, out_dtype):
    nchip, _, r, c = g.shape
    tr = _divisor_tile(r, 1024, 16)

    def body(core_ref, g_ref, l_ref, o_ref):
        o_ref[...] = (g_ref[...] + l_ref[...]).astype(o_ref.dtype)

    return pl.pallas_call(
        body,
        grid_spec=pltpu.PrefetchScalarGridSpec(
            num_scalar_prefetch=1, grid=(nchip, r // tr),
            in_specs=[pl.BlockSpec((None, None, tr, c), lambda j, i, core_ref: (j, core_ref[0], i, 0)),
                      pl.BlockSpec((None, tr, c), lambda j, i, core_ref: (j, i, 0))],
            out_specs=pl.BlockSpec((None, tr, c), lambda j, i, core_ref: (j, i, 0))),
        out_shape=_sds((nchip, r, c), out_dtype), compiler_params=_params("parallel", "parallel"), name=name,
    )(core, g, land)


def exchange_chips(name, ps):
    n = len(ps)

    def body(*refs):
        p_refs, land_refs = refs[:n], refs[n:2 * n]
        send_sems, recv_sems, local_sems = refs[2 * n:]
        x, y, cc = _place()
        mine = 2 * x + y
        chips = [(1 - x, y), (x, 1 - y), (1 - x, 1 - y)]
        own = [pltpu.make_async_copy(p_refs[a].at[mine], land_refs[a].at[mine], local_sems.at[a]) for a in range(n)]
        for cp in own:
            cp.start()
        sends = [pltpu.make_async_remote_copy(
            src_ref=p_refs[a].at[2 * px + py], dst_ref=land_refs[a].at[mine], send_sem=send_sems.at[3 * a + k],
            recv_sem=recv_sems.at[3 * a + k], device_id=(px, py, cc), device_id_type=MESH)
            for a in range(n) for k, (px, py) in enumerate(chips)]
        for cp in sends:
            cp.start()
        for a in range(n):
            for k, (px, py) in enumerate(chips):
                pltpu.make_async_remote_copy(
                    src_ref=p_refs[a].at[mine], dst_ref=land_refs[a].at[2 * px + py], send_sem=send_sems.at[3 * a + k],
                    recv_sem=recv_sems.at[3 * a + k], device_id=(px, py, cc), device_id_type=MESH).wait_recv()
        for cp in sends:
            cp.wait_send()
        for cp in own:
            cp.wait()

    return pl.pallas_call(
        body, out_shape=[_sds(p.shape, p.dtype) for p in ps], in_specs=[ANY] * n, out_specs=[ANY] * n,
        scratch_shapes=[pltpu.SemaphoreType.DMA((3 * n,)), pltpu.SemaphoreType.DMA((3 * n,)),
                        pltpu.SemaphoreType.DMA((n,))],
        name=name)(*ps)


def adamw(name, parts, w, m, v):
    npart, r, c = parts.shape
    tr = _divisor_tile(r, 512, 16)
    c1 = 1.0 / (1.0 - ADAM_B1 ** ADAM_STEP)
    c2 = 1.0 / (1.0 - ADAM_B2 ** ADAM_STEP)

    def body(p_ref, w_ref, m_ref, v_ref, g_ref, d_ref, nm_ref, nv_ref):
        g = p_ref[0].astype(F32)
        for j in range(1, npart):
            g = g + p_ref[j].astype(F32)
        nm = ADAM_B1 * m_ref[...] + (1.0 - ADAM_B1) * g
        nv = ADAM_B2 * v_ref[...] + (1.0 - ADAM_B2) * (g * g)
        g_ref[...] = g
        nm_ref[...] = nm
        nv_ref[...] = nv
        d_ref[...] = -ADAM_LR * ((nm * c1) / (jnp.sqrt(nv * c2) + ADAM_EPS) + ADAM_WD * w_ref[...])

    row = pl.BlockSpec((tr, c), lambda i: (i, 0))
    return pl.pallas_call(
        body, grid=(r // tr,), in_specs=[pl.BlockSpec((npart, tr, c), lambda i: (0, i, 0)), row, row, row],
        out_specs=[row] * 4, out_shape=[_sds((r, c), F32)] * 4, compiler_params=_params("parallel"), name=name,
    )(parts, w, m, v)


def _pack(arrays, dtype, row_multiple):
    flat = jnp.concatenate([a.astype(dtype).reshape(-1) for a in arrays])
    per = row_multiple * LANES
    total = -(-flat.shape[0] // per) * per
    return jnp.pad(flat, (0, total - flat.shape[0])).reshape(total // LANES, LANES)


def _pack_blocked(arrays, dtype, row_multiple):
    flat = jnp.concatenate([a.astype(dtype).reshape(N_DEV, -1) for a in arrays], axis=1)
    per = row_multiple * LANES
    total = -(-flat.shape[1] // per) * per
    return jnp.pad(flat, ((0, 0), (0, total - flat.shape[1]))).reshape(N_DEV, total // LANES, LANES)


def _unpack(buf, shapes, lead=()):
    flat = buf.reshape(lead + (-1,))
    out, off = [], 0
    for s in shapes:
        n = math.prod(s)
        out.append(flat[..., off:off + n].reshape(lead + tuple(s)))
        off += n
    return out


def _to_blocked(full, ax):
    s = full.shape
    return jnp.moveaxis(full.reshape(s[:ax] + (N_DEV, s[ax] // N_DEV) + s[ax + 1:]), ax, 0)


def _from_blocked(blk, ax):
    moved = jnp.moveaxis(blk, 0, ax)
    s = moved.shape
    return moved.reshape(s[:ax] + (s[ax] * s[ax + 1],) + s[ax + 2:])


SMALL = ("rg_conv_w", "rg_conv_b", "rg_b_a", "rg_b_i", "rg_lambda")
GU = "ffn_w_gu"
BIG = ("attn_w_in", "attn_w_out", "rg_w_in", "rg_w_a", "rg_w_i", "rg_w_out", GU, "ffn_w_down")


def kernel(x, attn_w_in, attn_rel_bias, attn_w_out, rg_w_in, rg_conv_w, rg_conv_b, rg_w_a, rg_b_a, rg_w_i, rg_b_i, rg_lambda, rg_w_out, norm_mix_pre, norm_mix_post, norm_ffn_pre, norm_ffn_post, ffn_w_gate, ffn_w_up, ffn_w_down, loss_target, m_attn_w_in, m_attn_rel_bias, m_attn_w_out, m_rg_w_in, m_rg_conv_w, m_rg_conv_b, m_rg_w_a, m_rg_b_a, m_rg_w_i, m_rg_b_i, m_rg_lambda, m_rg_w_out, m_norm_mix_pre, m_norm_mix_post, m_norm_ffn_pre, m_norm_ffn_post, m_ffn_w_gate, m_ffn_w_up, m_ffn_w_down, v_attn_w_in, v_attn_rel_bias, v_attn_w_out, v_rg_w_in, v_rg_conv_w, v_rg_conv_b, v_rg_w_a, v_rg_b_a, v_rg_w_i, v_rg_b_i, v_rg_lambda, v_rg_w_out, v_norm_mix_pre, v_norm_mix_post, v_norm_ffn_pre, v_norm_ffn_post, v_ffn_w_gate, v_ffn_w_up, v_ffn_w_down):
    w_loc = dict(attn_w_in=attn_w_in, attn_rel_bias=attn_rel_bias, attn_w_out=attn_w_out, rg_w_in=rg_w_in,
                 rg_conv_w=rg_conv_w, rg_conv_b=rg_conv_b, rg_w_a=rg_w_a, rg_b_a=rg_b_a, rg_w_i=rg_w_i, rg_b_i=rg_b_i,
                 rg_lambda=rg_lambda, rg_w_out=rg_w_out, norm_mix_pre=norm_mix_pre, norm_mix_post=norm_mix_post,
                 norm_ffn_pre=norm_ffn_pre, norm_ffn_post=norm_ffn_post, ffn_w_gate=ffn_w_gate, ffn_w_up=ffn_w_up,
                 ffn_w_down=ffn_w_down)
    m_loc = dict(attn_w_in=m_attn_w_in, attn_rel_bias=m_attn_rel_bias, attn_w_out=m_attn_w_out, rg_w_in=m_rg_w_in,
                 rg_conv_w=m_rg_conv_w, rg_conv_b=m_rg_conv_b, rg_w_a=m_rg_w_a, rg_b_a=m_rg_b_a, rg_w_i=m_rg_w_i,
                 rg_b_i=m_rg_b_i, rg_lambda=m_rg_lambda, rg_w_out=m_rg_w_out, norm_mix_pre=m_norm_mix_pre,
                 norm_mix_post=m_norm_mix_post, norm_ffn_pre=m_norm_ffn_pre, norm_ffn_post=m_norm_ffn_post,
                 ffn_w_gate=m_ffn_w_gate, ffn_w_up=m_ffn_w_up, ffn_w_down=m_ffn_w_down)
    v_loc = dict(attn_w_in=v_attn_w_in, attn_rel_bias=v_attn_rel_bias, attn_w_out=v_attn_w_out, rg_w_in=v_rg_w_in,
                 rg_conv_w=v_rg_conv_w, rg_conv_b=v_rg_conv_b, rg_w_a=v_rg_w_a, rg_b_a=v_rg_b_a, rg_w_i=v_rg_w_i,
                 rg_b_i=v_rg_b_i, rg_lambda=v_rg_lambda, rg_w_out=v_rg_w_out, norm_mix_pre=v_norm_mix_pre,
                 norm_mix_post=v_norm_mix_post, norm_ffn_pre=v_norm_ffn_pre, norm_ffn_post=v_norm_ffn_post,
                 ffn_w_gate=v_ffn_w_gate, ffn_w_up=v_ffn_w_up, ffn_w_down=v_ffn_w_down)
    axis_of = dict(SHARDED)
    xt, target = x[0], loss_target[0]
    d_model = xt.shape[1]
    rows2d = lambda a: a.reshape(-1, a.shape[-1])
    small_shapes = [w_loc[n].shape for n in SMALL]
    f8 = ffn_w_gate.shape[-1]
    for d in (w_loc, m_loc, v_loc):
        d[GU] = merge_gu(d["ffn_w_gate"], d["ffn_w_up"])

    gathered = all_gather("gather_weights", [rows2d(w_loc[n]).astype(BF16) for n in BIG]
                          + [_pack([w_loc[n] for n in SMALL], F32, 8)])
    blocked = {n: g.reshape((N_DEV,) + w_loc[n].shape) for n, g in zip(BIG, gathered)}
    blocked.update(zip(SMALL, _unpack(gathered[-1], small_shapes, (N_DEV,))))
    full = {n: _from_blocked(blocked[n], axis_of[n]) for n in SMALL}
    row = lambda a: a.reshape(1, -1).astype(F32)
    square = lambda rows8: rows8.reshape(-1, rows8.shape[-1])
    gates = lambda g: jnp.swapaxes(g, 0, 1).reshape(LRU_BLOCKS, -1, g.shape[-1])

    def layer_weights(layer):
        j = layer // 2
        norms = dict(g_pre=row(norm_mix_pre[layer]), g_post=row(norm_mix_post[layer]), idx=j)
        if layer % 2 == 0:
            mix = dict(w_in=blocked["attn_w_in"], w_out=square(blocked["attn_w_out"][:, j]),
                       rel_bias=attn_rel_bias[j], **norms)
        else:
            mix = dict(w_in=blocked["rg_w_in"], w_out=square(blocked["rg_w_out"][:, j]),
                       conv_w=full["rg_conv_w"][j][:, 0, :], conv_b=row(full["rg_conv_b"][j]),
                       w_a=gates(blocked["rg_w_a"][:, j]), w_i=gates(blocked["rg_w_i"][:, j]),
                       b_a=row(full["rg_b_a"][j]), b_i=row(full["rg_b_i"][j]), lam=row(full["rg_lambda"][j]), **norms)
        ffn = dict(w_gu=blocked[GU], w_down=blocked["ffn_w_down"], idx=layer,
                   g_pre=row(norm_ffn_pre[layer]), g_post=row(norm_ffn_post[layer]))
        return mix, ffn

    weights = [layer_weights(layer) for layer in range(DEPTH)]
    act, tape = xt, []
    h = rmsnorm_fwd("l0_mix_norm", act, weights[0][0]["g_pre"])
    for layer in range(DEPTH):
        mix_w, ffn_w = weights[layer]
        mixer_fwd = attn_layer_fwd if layer % 2 == 0 else rg_layer_fwd
        act, h, saved_mix = mixer_fwd(f"l{layer}_mix", act, h, mix_w, ffn_w["g_pre"])
        g_next = weights[layer + 1][0]["g_pre"] if layer + 1 < DEPTH else None
        act, h, saved_ffn = ffn_layer_fwd(f"l{layer}_ffn", act, h, ffn_w, g_next)
        tape.append((mix_w, ffn_w, saved_mix, saved_ffn))
    dact, sq = loss_grad("loss", act, target)
    loss = lax.psum(0.5 * jnp.sum(sq) / d_model, ("x", "y", "c"))

    grads = {}
    for layer in reversed(range(DEPTH)):
        mix_w, ffn_w, saved_mix, saved_ffn = tape[layer]
        dact, grads[("ffn", layer)] = ffn_layer_bwd(f"l{layer}_ffn", dact, saved_ffn, ffn_w)
        mixer_bwd = attn_layer_bwd if layer % 2 == 0 else rg_layer_bwd
        dact, grads[("mix", layer)] = mixer_bwd(f"l{layer}_mix", dact, saved_mix, mix_w)
    attn_g = [grads[("mix", l)] for l in range(0, DEPTH, 2)]
    rg_g = [grads[("mix", l)] for l in range(1, DEPTH, 2)]
    ffn_g = [grads[("ffn", l)] for l in range(DEPTH)]
    stack = lambda gs, key: jnp.stack([g[key] for g in gs])
    by_owner = lambda gs, key, f: jnp.stack([f(g[key]) for g in gs], axis=1)
    rows8 = lambda a: a.reshape(N_DEV, -1, a.shape[-1])
    ungates = lambda a: jnp.swapaxes(a.reshape(LRU_BLOCKS, N_DEV, -1, a.shape[-1]), 0, 1)
    same = lambda a: a
    blocked_g = dict(
        attn_w_in=by_owner(attn_g, "w_in", same), attn_w_out=by_owner(attn_g, "w_out", rows8),
        rg_w_in=by_owner(rg_g, "w_in", same), rg_w_out=by_owner(rg_g, "w_out", rows8),
        rg_w_a=by_owner(rg_g, "w_a", ungates), rg_w_i=by_owner(rg_g, "w_i", ungates),
        **{GU: by_owner(ffn_g, "w_gu", same)},
        ffn_w_down=by_owner(ffn_g, "w_down", same))
    contrib = dict(
        attn_rel_bias=stack(attn_g, "rel_bias"), rg_conv_w=stack(rg_g, "conv_w")[:, :, None, :],
        rg_conv_b=stack(rg_g, "conv_b")[:, 0], rg_b_a=stack(rg_g, "b_a").reshape(rg_b_a.shape[0], LRU_BLOCKS, -1),
        rg_b_i=stack(rg_g, "b_i").reshape(rg_b_i.shape[0], LRU_BLOCKS, -1), rg_lambda=stack(rg_g, "lam")[:, 0],
        norm_mix_pre=jnp.concatenate([grads[("mix", l)]["g_pre"] for l in range(DEPTH)]),
        norm_mix_post=jnp.concatenate([grads[("mix", l)]["g_post"] for l in range(DEPTH)]),
        norm_ffn_pre=jnp.concatenate([g["g_pre"] for g in ffn_g]),
        norm_ffn_post=jnp.concatenate([g["g_post"] for g in ffn_g]),
    )
    small_g = _pack_blocked([_to_blocked(contrib[n], axis_of[n]) for n in SMALL], F32, 8)

    slabs = [blocked_g[n].reshape(4, 2, -1, blocked_g[n].shape[-1]) for n in BIG] + [small_g.reshape(4, 2, -1, LANES)]
    core = lax.axis_index("c").astype(jnp.int32).reshape(1)
    from_sibling = exchange_pair("rs_pair", slabs)
    pairs = [pair_sum(f"rs_pair_sum_{i}", g, l, core, BF16 if i < len(BIG) else F32)
             for i, (g, l) in enumerate(zip(slabs, from_sibling))]
    by_chip = exchange_chips("rs_chips", pairs)
    result = {}
    kinds = ("grad", "delta", "new_m", "new_v")
    for n, parts in zip(BIG, by_chip):
        outs = adamw("adamw_" + n, parts, *[rows2d(d[n]) for d in (w_loc, m_loc, v_loc)])
        for kind, a in zip(kinds, outs):
            result[(kind, n)] = a.reshape(w_loc[n].shape)
    for kind in kinds:
        result[(kind, "ffn_w_gate")], result[(kind, "ffn_w_up")] = split_gu(result.pop((kind, GU)), f8)
    outs = adamw("adamw_small", by_chip[-1], *[_pack([d[n] for n in SMALL], F32, 8) for d in (w_loc, m_loc, v_loc)])
    for kind, buf in zip(kinds, outs):
        result.update({(kind, n): a for n, a in zip(SMALL, _unpack(buf, small_shapes))})
    rep_shapes = [w_loc[n].shape for n in REPLICATED]
    rep_parts, = all_gather("gather_rep_grads", [_pack([contrib[n] for n in REPLICATED], F32, 8)])
    outs = adamw("adamw_replicated", rep_parts, *[_pack([d[n] for n in REPLICATED], F32, 8)
                                                  for d in (w_loc, m_loc, v_loc)])
    for kind, buf in zip(kinds, outs):
        result.update({(kind, n): a for n, a in zip(REPLICATED, _unpack(buf, rep_shapes))})
    return (loss, dact[None], *[result[(kind, n)] for kind in kinds for n in WEIGHTS])
```

```python
import functools
import math

import jax
import jax.numpy as jnp
from jax import lax
from jax.experimental import pallas as pl
from jax.experimental.pallas import tpu as pltpu

F32 = jnp.float32
BF16 = jnp.bfloat16

N_DEV = 8
DEPTH = 4
CHUNK = 64
N_LEFT = 8
BAND = (N_LEFT + 1) * CHUNK
PAD_KEYS = N_LEFT * CHUNK
HEAD_DIM = 64
N_HEADS = 8
REL_CLIP = 256
LRU_BLOCKS = 4
LRU_C = 8.0
RMS_EPS = 1e-6
QK_SCALE = HEAD_DIM ** -0.5

ADAM_LR = 0.001
ADAM_B1 = 0.9
ADAM_B2 = 0.999
ADAM_EPS = 1e-08
ADAM_WD = 0.01
ADAM_STEP = 10

LANES = 1024
V7X_VMEM_LIMIT = 56 * 1024 * 1024

MESH = pl.DeviceIdType.MESH
ANY = pl.BlockSpec(memory_space=pl.ANY)

SHARDED = (
    ("attn_w_in", 2), ("attn_w_out", 1), ("rg_w_in", 2), ("rg_conv_w", 3), ("rg_conv_b", 1),
    ("rg_w_a", 2), ("rg_b_a", 2), ("rg_w_i", 2), ("rg_b_i", 2), ("rg_lambda", 1), ("rg_w_out", 1),
    ("ffn_w_gate", 2), ("ffn_w_up", 2), ("ffn_w_down", 1),
)
REPLICATED = ("attn_rel_bias", "norm_mix_pre", "norm_mix_post", "norm_ffn_pre", "norm_ffn_post")
WEIGHTS = ("attn_w_in", "attn_rel_bias", "attn_w_out", "rg_w_in", "rg_conv_w", "rg_conv_b", "rg_w_a", "rg_b_a",
           "rg_w_i", "rg_b_i", "rg_lambda", "rg_w_out", "norm_mix_pre", "norm_mix_post", "norm_ffn_pre",
           "norm_ffn_post", "ffn_w_gate", "ffn_w_up", "ffn_w_down")


def _params(*dims):
    return pltpu.CompilerParams(dimension_semantics=dims or None, vmem_limit_bytes=V7X_VMEM_LIMIT)


def _sds(shape, dtype):
    return jax.ShapeDtypeStruct(tuple(shape), dtype)


def _row_tile(n, pref):
    t = min(n, pref)
    assert n % t == 0, (n, pref)
    return t


def _divisor_tile(n, limit, multiple):
    if n <= limit:
        return n
    best = max(t for t in range(multiple, limit + 1, multiple) if n % t == 0)
    return best


NN = (((1,), (0,)), ((), ()))
NT = (((1,), (1,)), ((), ()))
TN = (((0,), (0,)), ((), ()))


def _gmm(name, a, b, *, grid, a_blk, a_idx, b_blk, b_idx, o_blk, o_idx, out_shape, out_dtype, dn, acc_shape):
    nk = grid[-1]
    kax = len(grid) - 1

    def body(a_ref, b_ref, o_ref, acc_ref):
        part = lax.dot_general(a_ref[...], b_ref[...], dn, preferred_element_type=F32)
        if nk == 1:
            o_ref[...] = part.astype(o_ref.dtype)
            return
        k = pl.program_id(kax)

        @pl.when(k == 0)
        def _():
            acc_ref[...] = part

        @pl.when(k > 0)
        def _():
            acc_ref[...] += part

        @pl.when(k == nk - 1)
        def _():
            o_ref[...] = acc_ref[...].astype(o_ref.dtype)

    return pl.pallas_call(
        body, grid=grid,
        in_specs=[pl.BlockSpec(a_blk, a_idx), pl.BlockSpec(b_blk, b_idx)],
        out_specs=pl.BlockSpec(o_blk, o_idx),
        out_shape=_sds(out_shape, out_dtype),
        scratch_shapes=[pltpu.VMEM(acc_shape, F32)],
        compiler_params=_params(*(["parallel"] * kax + ["arbitrary"])),
        name=name,
    )(a, b)


def mm_nn(name, a, b, out_dtype, tm=1024, tn=512, tk=1024):
    (m, k), (_, n) = a.shape, b.shape
    tm, tn, tk = _row_tile(m, tm), _row_tile(n, tn), _row_tile(k, tk)
    return _gmm(name, a, b, grid=(m // tm, n // tn, k // tk),
                a_blk=(tm, tk), a_idx=lambda i, j, kk: (i, kk), b_blk=(tk, tn), b_idx=lambda i, j, kk: (kk, j),
                o_blk=(tm, tn), o_idx=lambda i, j, kk: (i, j), out_shape=(m, n), out_dtype=out_dtype, dn=NN,
                acc_shape=(tm, tn))


def mm_nt(name, a, b, out_dtype, tm=1024, tn=512, tk=1024):
    (m, k), (n, _) = a.shape, b.shape
    tm, tn, tk = _row_tile(m, tm), _row_tile(n, tn), _row_tile(k, tk)
    return _gmm(name, a, b, grid=(m // tm, n // tn, k // tk),
                a_blk=(tm, tk), a_idx=lambda i, j, kk: (i, kk), b_blk=(tn, tk), b_idx=lambda i, j, kk: (j, kk),
                o_blk=(tm, tn), o_idx=lambda i, j, kk: (i, j), out_shape=(m, n), out_dtype=out_dtype, dn=NT,
                acc_shape=(tm, tn))


def mm_tn(name, a, b, out_dtype, tm=512, tn=512, tk=1024):
    (k, m), (_, n) = a.shape, b.shape
    tm, tn, tk = _row_tile(m, tm), _row_tile(n, tn), _row_tile(k, tk)
    return _gmm(name, a, b, grid=(m // tm, n // tn, k // tk),
                a_blk=(tk, tm), a_idx=lambda i, j, kk: (kk, i), b_blk=(tk, tn), b_idx=lambda i, j, kk: (kk, j),
                o_blk=(tm, tn), o_idx=lambda i, j, kk: (i, j), out_shape=(m, n), out_dtype=out_dtype, dn=TN,
                acc_shape=(tm, tn))


def mm_nn_wblk(name, a, wb, layer, out_dtype, tm=1024, tk=1024):
    (m, k), (nb, _, _, n8) = a.shape, wb.shape
    tm, tk = _row_tile(m, tm), _row_tile(k, tk)
    return _gmm(name, a, wb, grid=(m // tm, nb, k // tk),
                a_blk=(tm, tk), a_idx=lambda i, j, kk: (i, kk),
                b_blk=(None, None, tk, n8), b_idx=lambda i, j, kk: (j, layer, kk, 0),
                o_blk=(tm, n8), o_idx=lambda i, j, kk: (i, j), out_shape=(m, nb * n8), out_dtype=out_dtype, dn=NN,
                acc_shape=(tm, n8))


def mm_nt_wblk(name, a, wb, layer, out_dtype, tm=1024, tn=512):
    m = a.shape[0]
    nb, _, k, n8 = wb.shape
    tm, tn = _row_tile(m, tm), _row_tile(k, tn)

    def body(a_ref, b_ref, o_ref):
        acc = lax.dot_general(a_ref[:, 0:n8], b_ref[0], NT, preferred_element_type=F32)
        for j in range(1, nb):
            acc = acc + lax.dot_general(a_ref[:, j * n8:(j + 1) * n8], b_ref[j], NT, preferred_element_type=F32)
        o_ref[...] = acc.astype(o_ref.dtype)

    return pl.pallas_call(
        body, grid=(m // tm, k // tn),
        in_specs=[pl.BlockSpec((tm, nb * n8), lambda i, j: (i, 0)),
                  pl.BlockSpec((nb, None, tn, n8), lambda i, j: (0, layer, j, 0))],
        out_specs=pl.BlockSpec((tm, tn), lambda i, j: (i, j)), out_shape=_sds((m, k), out_dtype),
        compiler_params=_params("parallel", "parallel"), name=name)(a, wb)


def mm_tn_oblk(name, a, b, n8, out_dtype, tk=2048):
    (t, k), nb = a.shape, b.shape[1] // n8
    tk = _row_tile(t, tk)
    return _gmm(name, a, b, grid=(nb, t // tk),
                a_blk=(tk, k), a_idx=lambda j, s: (s, 0), b_blk=(tk, n8), b_idx=lambda j, s: (s, j),
                o_blk=(None, k, n8), o_idx=lambda j, s: (j, 0, 0), out_shape=(nb, k, n8), out_dtype=out_dtype, dn=TN,
                acc_shape=(k, n8))


def rmsnorm_fwd(name, x, g):
    t, d = x.shape
    tr = _row_tile(t, 512)

    def body(x_ref, g_ref, o_ref):
        xv = x_ref[...]
        r = lax.rsqrt(jnp.mean(xv * xv, axis=-1, keepdims=True) + RMS_EPS)
        o_ref[...] = (xv * r * g_ref[...]).astype(o_ref.dtype)

    return pl.pallas_call(
        body, grid=(t // tr,),
        in_specs=[pl.BlockSpec((tr, d), lambda i: (i, 0)), pl.BlockSpec((1, d), lambda i: (0, 0))],
        out_specs=pl.BlockSpec((tr, d), lambda i: (i, 0)),
        out_shape=_sds((t, d), BF16), compiler_params=_params("parallel"), name=name)(x, g)


def resid_norm_fwd(name, x, m, g, g_next):
    t, d = x.shape
    tr = _row_tile(t, 512)
    chained = g_next is not None

    def body(*refs):
        x_ref, m_ref, g_ref = refs[:3]
        mv = m_ref[...]
        r = lax.rsqrt(jnp.mean(mv * mv, axis=-1, keepdims=True) + RMS_EPS)
        x1 = x_ref[...] + mv * r * g_ref[...]
        if chained:
            gn_ref, o_ref, h_ref = refs[3:]
            r1 = lax.rsqrt(jnp.mean(x1 * x1, axis=-1, keepdims=True) + RMS_EPS)
            h_ref[...] = (x1 * r1 * gn_ref[...]).astype(BF16)
        else:
            o_ref, = refs[3:]
        o_ref[...] = x1

    row = pl.BlockSpec((tr, d), lambda i: (i, 0))
    vec = pl.BlockSpec((1, d), lambda i: (0, 0))
    out = pl.pallas_call(
        body, grid=(t // tr,),
        in_specs=[row, row, vec] + ([vec] if chained else []),
        out_specs=[row, row] if chained else [row],
        out_shape=[_sds((t, d), F32)] + ([_sds((t, d), BF16)] if chained else []),
        compiler_params=_params("parallel"), name=name)(*([x, m, g] + ([g_next] if chained else [])))
    return (out[0], out[1]) if chained else (out[0], None)


def norm_bwd(name, dy, x, g, resid, out_dtype):
    t, d = x.shape
    tr = _row_tile(t, 512)
    has_res = resid is not None

    def body(*refs):
        if has_res:
            dy_ref, x_ref, g_ref, r_ref, dx_ref, dg_ref = refs
        else:
            dy_ref, x_ref, g_ref, dx_ref, dg_ref = refs
        i = pl.program_id(0)
        xv = x_ref[...]
        dyv = dy_ref[...].astype(F32)
        r = lax.rsqrt(jnp.mean(xv * xv, axis=-1, keepdims=True) + RMS_EPS)
        xh = xv * r
        dxh = dyv * g_ref[...]
        dx = r * (dxh - xh * jnp.mean(dxh * xh, axis=-1, keepdims=True))
        if has_res:
            dx = dx + r_ref[...]
        dx_ref[...] = dx.astype(dx_ref.dtype)
        part = jnp.sum(dyv * xh, axis=0, keepdims=True)

        @pl.when(i == 0)
        def _():
            dg_ref[...] = part

        @pl.when(i > 0)
        def _():
            dg_ref[...] += part

    row = pl.BlockSpec((tr, d), lambda i: (i, 0))
    vec = pl.BlockSpec((1, d), lambda i: (0, 0))
    ins = [dy, x, g] + ([resid] if has_res else [])
    return pl.pallas_call(
        body, grid=(t // tr,),
        in_specs=[row, row, vec] + ([row] if has_res else []),
        out_specs=[row, vec],
        out_shape=[_sds((t, d), out_dtype), _sds((1, d), F32)],
        compiler_params=_params("arbitrary"), name=name)(*ins)


def norm_bwd_chain(name, dh, x, g, resid, m_prev, g_prev):
    t, d = x.shape
    tr = _row_tile(t, 512)

    def body(dh_ref, x_ref, g_ref, r_ref, m_ref, gp_ref, dx_ref, dg_ref, dm_ref, dgp_ref):
        i = pl.program_id(0)
        xv = x_ref[...]
        dhv = dh_ref[...]
        r = lax.rsqrt(jnp.mean(xv * xv, axis=-1, keepdims=True) + RMS_EPS)
        xh = xv * r
        dxh = dhv * g_ref[...]
        dx = r * (dxh - xh * jnp.mean(dxh * xh, axis=-1, keepdims=True)) + r_ref[...]
        dx_ref[...] = dx
        mv = m_ref[...]
        rm = lax.rsqrt(jnp.mean(mv * mv, axis=-1, keepdims=True) + RMS_EPS)
        mh = mv * rm
        dmh = dx * gp_ref[...]
        dm_ref[...] = (rm * (dmh - mh * jnp.mean(dmh * mh, axis=-1, keepdims=True))).astype(BF16)
        part = jnp.sum(dhv * xh, axis=0, keepdims=True)
        part_prev = jnp.sum(dx * mh, axis=0, keepdims=True)

        @pl.when(i == 0)
        def _():
            dg_ref[...] = part
            dgp_ref[...] = part_prev

        @pl.when(i > 0)
        def _():
            dg_ref[...] += part
            dgp_ref[...] += part_prev

    row = pl.BlockSpec((tr, d), lambda i: (i, 0))
    vec = pl.BlockSpec((1, d), lambda i: (0, 0))
    return pl.pallas_call(
        body, grid=(t // tr,), in_specs=[row, row, vec, row, row, vec], out_specs=[row, vec, row, vec],
        out_shape=[_sds((t, d), F32), _sds((1, d), F32), _sds((t, d), BF16), _sds((1, d), F32)],
        compiler_params=_params("arbitrary"), name=name)(dh, x, g, resid, m_prev, g_prev)


def close_bwd(tag, dh, x, g_pre, dx1, prev):
    if prev is None:
        dx, dg_pre = norm_bwd(tag + "_dpre", dh, x, g_pre, dx1, F32)
        return dx, dg_pre, None, None
    return norm_bwd_chain(tag + "_dpre", dh, x, g_pre, dx1, *prev)


def loss_grad(name, y, target):
    t, d = y.shape
    tr = _row_tile(t, 512)

    def body(y_ref, t_ref, dy_ref, s_ref):
        i = pl.program_id(0)
        err = y_ref[...] - t_ref[...]
        dy_ref[...] = err * (1.0 / d)
        part = jnp.sum(err * err, axis=0, keepdims=True)

        @pl.when(i == 0)
        def _():
            s_ref[...] = part

        @pl.when(i > 0)
        def _():
            s_ref[...] += part

    row = pl.BlockSpec((tr, d), lambda i: (i, 0))
    vec = pl.BlockSpec((1, d), lambda i: (0, 0))
    return pl.pallas_call(
        body, grid=(t // tr,), in_specs=[row, row], out_specs=[row, vec],
        out_shape=[_sds((t, d), F32), _sds((1, d), F32)],
        compiler_params=_params("arbitrary"), name=name)(y, target)


PAIR = 2 * HEAD_DIM
N_PAIRS = N_HEADS // 2
A_TQ = 512
A_UNROLL = 4


def _halves(x):
    lane = lax.broadcasted_iota(jnp.int32, x.shape, x.ndim - 1)
    zero = jnp.zeros_like(x)
    return jnp.where(lane < HEAD_DIM, x, zero), jnp.where(lane >= HEAD_DIM, x, zero)


def _merge(a, b):
    lane = lax.broadcasted_iota(jnp.int32, a.shape, a.ndim - 1)
    return jnp.where(lane < HEAD_DIM, a, b)


def _a_valid(c):
    col = lax.broadcasted_iota(jnp.int32, (CHUNK, BAND), 1)
    return col >= (N_LEFT - c) * CHUNK


def attn_a_fwd(name, proj, kp, vp, bias, q_blk):
    t = proj.shape[0]
    tq = _row_tile(t, A_TQ)
    ncs = tq // CHUNK
    un = math.gcd(A_UNROLL, ncs)

    def body(q_ref, k_ref, v_ref, b_ref, o_ref, l_ref):
        i = pl.program_id(1)

        def group(gg, carry):
            cs = [i * ncs + gg * un + u for u in range(un)]
            r0s = [pl.multiple_of((gg * un + u) * CHUNK, CHUNK) for u in range(un)]
            k0s = [pl.multiple_of(c * CHUNK, CHUNK) for c in cs]
            ss = []
            for u in range(un):
                qh = _halves(q_ref[pl.ds(r0s[u], CHUNK), :] * QK_SCALE)
                kwin = k_ref[pl.ds(k0s[u], BAND), :]
                valid = _a_valid(cs[u])
                for hh in range(2):
                    s = lax.dot_general(qh[hh], kwin, NT, preferred_element_type=F32) + b_ref[hh]
                    ss.append(jnp.where(valid, s, -1e30))
            ps, lses = [], []
            for s in ss:
                mx = jnp.max(s, axis=-1, keepdims=True)
                p = jnp.exp(s - mx)
                den = jnp.sum(p, axis=-1, keepdims=True)
                ps.append((p * (1.0 / den)).astype(BF16))
                lses.append(mx + jnp.log(den))
            for u in range(un):
                vwin = v_ref[pl.ds(k0s[u], BAND), :]
                o0 = jnp.dot(ps[2 * u], vwin, preferred_element_type=F32)
                o1 = jnp.dot(ps[2 * u + 1], vwin, preferred_element_type=F32)
                o_ref[pl.ds(r0s[u], CHUNK), :] = _merge(o0, o1)
                l_ref[pl.ds(r0s[u], CHUNK), :] = jnp.concatenate([lses[2 * u], lses[2 * u + 1]], axis=1)
            return carry

        lax.fori_loop(0, ncs // un, group, 0)

    return pl.pallas_call(
        body, grid=(N_PAIRS, t // tq),
        in_specs=[pl.BlockSpec((tq, PAIR), lambda p, i: (i, q_blk + p)),
                  pl.BlockSpec((t + PAD_KEYS, PAIR), lambda p, i: (0, p)),
                  pl.BlockSpec((t + PAD_KEYS, PAIR), lambda p, i: (0, p)),
                  pl.BlockSpec((2, CHUNK, BAND), lambda p, i: (p, 0, 0))],
        out_specs=[pl.BlockSpec((tq, PAIR), lambda p, i: (i, p)),
                   pl.BlockSpec((None, tq, 2), lambda p, i: (p, i, 0))],
        out_shape=[_sds((t, N_PAIRS * PAIR), F32), _sds((N_PAIRS, t, 2), F32)],
        compiler_params=_params("parallel", "parallel"), name=name)(proj, kp, vp, bias)


def attn_a_bwd(name, proj, kp, vp, bias, o, lse, do, q_blk, do_blk):
    t = proj.shape[0]
    tq = _row_tile(t, A_TQ)
    ncs = tq // CHUNK
    un = math.gcd(A_UNROLL, ncs)

    def body(q_ref, k_ref, v_ref, b_ref, o_ref, l_ref, do_ref, dq_ref, dk_ref, dv_ref, db_ref):
        i = pl.program_id(1)

        @pl.when(i == 0)
        def _():
            dk_ref[...] = jnp.zeros_like(dk_ref)
            dv_ref[...] = jnp.zeros_like(dv_ref)
            db_ref[...] = jnp.zeros_like(db_ref)

        def group(gg, carry):
            cs = [i * ncs + gg * un + u for u in range(un)]
            r0s = [pl.multiple_of((gg * un + u) * CHUNK, CHUNK) for u in range(un)]
            k0s = [pl.multiple_of(c * CHUNK, CHUNK) for c in cs]
            qhs, dohs, ps, dps, deltas = [], [], [], [], []
            for u in range(un):
                rows = pl.ds(r0s[u], CHUNK)
                qh = _halves(q_ref[rows, :] * QK_SCALE)
                doh = _halves(do_ref[rows, :])
                kwin = k_ref[pl.ds(k0s[u], BAND), :]
                vwin = v_ref[pl.ds(k0s[u], BAND), :]
                valid = _a_valid(cs[u])
                dl = _halves(do_ref[rows, :].astype(F32) * o_ref[rows, :])
                for hh in range(2):
                    s = lax.dot_general(qh[hh], kwin, NT, preferred_element_type=F32) + b_ref[hh]
                    ps.append(jnp.where(valid, jnp.exp(s - l_ref[rows, hh:hh + 1]), 0.0))
                    dps.append(lax.dot_general(doh[hh], vwin, NT, preferred_element_type=F32))
                    deltas.append(jnp.sum(dl[hh], axis=-1, keepdims=True))
                qhs.append(qh)
                dohs.append(doh)
            dss = [p * (dp - dl) for p, dp, dl in zip(ps, dps, deltas)]
            for hh in range(2):
                tot = dss[hh]
                for u in range(1, un):
                    tot = tot + dss[2 * u + hh]
                db_ref[hh] += tot
            for u in range(un):
                kwin = k_ref[pl.ds(k0s[u], BAND), :]
                ds0, ds1 = dss[2 * u].astype(BF16), dss[2 * u + 1].astype(BF16)
                dq_ref[pl.ds(r0s[u], CHUNK), :] = _merge(jnp.dot(ds0, kwin, preferred_element_type=F32),
                                                         jnp.dot(ds1, kwin, preferred_element_type=F32)) * QK_SCALE
                dk_ref[pl.ds(k0s[u], BAND), :] += (lax.dot_general(ds0, qhs[u][0], TN, preferred_element_type=F32)
                                                   + lax.dot_general(ds1, qhs[u][1], TN, preferred_element_type=F32))
                dv_ref[pl.ds(k0s[u], BAND), :] += (
                    lax.dot_general(ps[2 * u].astype(BF16), dohs[u][0], TN, preferred_element_type=F32)
                    + lax.dot_general(ps[2 * u + 1].astype(BF16), dohs[u][1], TN, preferred_element_type=F32))
            return carry

        lax.fori_loop(0, ncs // un, group, 0)

    tile = lambda blk: pl.BlockSpec((tq, PAIR), lambda p, i: (i, blk + p))
    whole = pl.BlockSpec((t + PAD_KEYS, PAIR), lambda p, i: (0, p))
    bspec = pl.BlockSpec((2, CHUNK, BAND), lambda p, i: (p, 0, 0))
    return pl.pallas_call(
        body, grid=(N_PAIRS, t // tq),
        in_specs=[tile(q_blk), whole, whole, bspec, tile(0), pl.BlockSpec((None, tq, 2), lambda p, i: (p, i, 0)),
                  tile(do_blk)],
        out_specs=[tile(0), whole, whole, bspec],
        out_shape=[_sds((t, N_PAIRS * PAIR), F32), _sds((t + PAD_KEYS, N_PAIRS * PAIR), F32),
                   _sds((t + PAD_KEYS, N_PAIRS * PAIR), F32), _sds((2 * N_PAIRS, CHUNK, BAND), F32)],
        compiler_params=_params("parallel", "arbitrary"), name=name)(proj, kp, vp, bias, o, lse, do)


SB_TQ = 256
SB_TK = 256
SB_DEAD = -125.0


def _tri(n, strict):
    j = lax.broadcasted_iota(jnp.int32, (n, n), 0)
    s = lax.broadcasted_iota(jnp.int32, (n, n), 1)
    return jnp.where((j > s) if strict else (j >= s), 1.0, 0.0).astype(BF16)


def _suffix_sum(x, tri, exact):
    hi = x.astype(BF16)
    out = jnp.dot(hi, tri, preferred_element_type=F32)
    if exact:
        lo = (x - hi.astype(F32)).astype(BF16)
        out = out + jnp.dot(lo, tri, preferred_element_type=F32)
    return out


def _sb_scores(qh, ks, causal):
    z = lax.dot_general(qh, ks, NT, preferred_element_type=F32)
    lb = jnp.minimum(z, 0.0) - jnp.log(1.0 + jnp.exp(-jnp.abs(z)))
    m = lb - z
    if causal is not None:
        m = jnp.where(causal, m, 0.0)
    return lb, m


def _causal(tq, tk, off):
    return (lax.broadcasted_iota(jnp.int32, (tq, tk), 1) + off * tk) < lax.broadcasted_iota(jnp.int32, (tq, tk), 0)


def sb_fwd(name, proj, q_blk, k_blk, v_blk):
    t = proj.shape[0]
    tq = _row_tile(t, SB_TQ)
    tk = min(SB_TK, tq)
    per = tq // tk

    def body(q_ref, k_ref, v_ref, o_ref):
        i = pl.program_id(1)
        tri = _tri(tk, True)
        qh = _halves(q_ref[...] * QK_SCALE)

        def blocks(kb, carry, off):
            k0 = pl.multiple_of(kb * tk, tk)
            ks, vs = k_ref[pl.ds(k0, tk), :], v_ref[pl.ds(k0, tk), :]
            causal = None if off is None else _causal(tq, tk, off)
            lbm = [_sb_scores(qh[hh], ks, causal) for hh in range(2)]
            afters = [_suffix_sum(lbm[hh][1], tri, False) for hh in range(2)]
            out = []
            for hh in range(2):
                acc, cm = carry[2 * hh], carry[2 * hh + 1]
                w = jnp.exp(lbm[hh][0] + afters[hh] + cm)
                if causal is not None:
                    w = jnp.where(causal, w, 0.0)
                out += [acc + jnp.dot(w.astype(BF16), vs, preferred_element_type=F32),
                        cm + jnp.sum(lbm[hh][1], axis=-1, keepdims=True)]
            return tuple(out)

        def alive(carry):
            return jnp.maximum(jnp.max(carry[1]), jnp.max(carry[3])) > SB_DEAD

        carry = (jnp.zeros((tq, PAIR), F32), jnp.zeros((tq, 1), F32)) * 2
        for off in reversed(range(per)):
            carry = blocks(i * per + off, carry, off)

        def step(c):
            new = blocks(i * per - 1 - c[0], c[2:], None)
            return (c[0] + 1, alive(new)) + new

        out = lax.while_loop(lambda c: jnp.logical_and(c[0] < i * per, c[1]), step,
                             (jnp.int32(0), alive(carry)) + carry)
        o_ref[...] = _merge(out[2], out[4])

    return pl.pallas_call(
        body, grid=(N_PAIRS, t // tq),
        in_specs=[pl.BlockSpec((tq, PAIR), lambda p, i: (i, q_blk + p)),
                  pl.BlockSpec((t, PAIR), lambda p, i: (0, k_blk + p)),
                  pl.BlockSpec((t, PAIR), lambda p, i: (0, v_blk + p))],
        out_specs=pl.BlockSpec((tq, PAIR), lambda p, i: (i, p)),
        out_shape=_sds((t, N_PAIRS * PAIR), F32), compiler_params=_params("parallel", "parallel"), name=name,
    )(proj, proj, proj)


def sb_bwd(name, proj, o, do, q_blk, k_blk, v_blk, do_blk):
    t = proj.shape[0]
    tq = _row_tile(t, SB_TQ)
    tk = min(SB_TK, tq)
    per = tq // tk

    def body(q_ref, k_ref, v_ref, o_ref, do_ref, dq_ref, dk_ref, dv_ref):
        i = pl.program_id(1)

        @pl.when(i == 0)
        def _():
            dk_ref[...] = jnp.zeros_like(dk_ref)
            dv_ref[...] = jnp.zeros_like(dv_ref)

        tri_s, tri_i = _tri(tk, True), _tri(tk, False)
        qh = _halves(q_ref[...] * QK_SCALE)
        doh = _halves(do_ref[...])
        deltas = [jnp.sum(x, axis=-1, keepdims=True) for x in _halves(do_ref[...].astype(F32) * o_ref[...])]

        def blocks(kb, carry, off):
            k0 = pl.multiple_of(kb * tk, tk)
            ks, vs = k_ref[pl.ds(k0, tk), :], v_ref[pl.ds(k0, tk), :]
            causal = None if off is None else _causal(tq, tk, off)
            lbm = [_sb_scores(qh[hh], ks, causal) for hh in range(2)]
            dws = [lax.dot_general(doh[hh], vs, NT, preferred_element_type=F32) for hh in range(2)]
            afters = [_suffix_sum(lbm[hh][1], tri_s, False) for hh in range(2)]
            wbs, es = [], []
            for hh in range(2):
                w = jnp.exp(lbm[hh][0] + afters[hh] + carry[3 * hh + 1])
                if causal is not None:
                    w = jnp.where(causal, w, 0.0)
                wbs.append(w.astype(BF16))
                es.append(wbs[hh].astype(F32) * dws[hh])
            sfx = [_suffix_sum(es[hh], tri_i, True) for hh in range(2)]
            dzs = []
            for hh in range(2):
                left = deltas[hh] - (sfx[hh] + carry[3 * hh + 2])
                sig = jnp.exp(lbm[hh][0])
                dz = es[hh] * (1.0 - sig) - left * sig
                if causal is not None:
                    dz = jnp.where(causal, dz, 0.0)
                dzs.append(dz.astype(BF16))
            dk_ref[pl.ds(k0, tk), :] += (lax.dot_general(dzs[0], qh[0], TN, preferred_element_type=F32)
                                         + lax.dot_general(dzs[1], qh[1], TN, preferred_element_type=F32))
            dv_ref[pl.ds(k0, tk), :] += (lax.dot_general(wbs[0], doh[0], TN, preferred_element_type=F32)
                                         + lax.dot_general(wbs[1], doh[1], TN, preferred_element_type=F32))
            out = []
            for hh in range(2):
                out += [carry[3 * hh] + jnp.dot(dzs[hh], ks, preferred_element_type=F32),
                        carry[3 * hh + 1] + jnp.sum(lbm[hh][1], axis=-1, keepdims=True),
                        carry[3 * hh + 2] + jnp.sum(es[hh], axis=-1, keepdims=True)]
            return tuple(out)

        def alive(carry):
            return jnp.maximum(jnp.max(carry[1]), jnp.max(carry[4])) > SB_DEAD

        zero = jnp.zeros((tq, 1), F32)
        carry = (jnp.zeros((tq, PAIR), F32), zero, zero) * 2
        for off in reversed(range(per)):
            carry = blocks(i * per + off, carry, off)

        def step(c):
            new = blocks(i * per - 1 - c[0], c[2:], None)
            return (c[0] + 1, alive(new)) + new

        out = lax.while_loop(lambda c: jnp.logical_and(c[0] < i * per, c[1]), step,
                             (jnp.int32(0), alive(carry)) + carry)
        dq_ref[...] = _merge(out[2], out[5]) * QK_SCALE

    tile = lambda blk: pl.BlockSpec((tq, PAIR), lambda p, i: (i, blk + p))
    whole = lambda blk: pl.BlockSpec((t, PAIR), lambda p, i: (0, blk + p))
    return pl.pallas_call(
        body, grid=(N_PAIRS, t // tq),
        in_specs=[tile(q_blk), whole(k_blk), whole(v_blk), tile(0), tile(do_blk)],
        out_specs=[tile(0), whole(0), whole(0)],
        out_shape=[_sds((t, N_PAIRS * PAIR), F32)] * 3,
        compiler_params=_params("parallel", "arbitrary"), name=name)(proj, proj, proj, o, do)


def _sigmoid(x):
    return 1.0 / (1.0 + jnp.exp(-x))


def gu_gap(f8):
    return -(-f8 // 128) * 128


def merge_gu(gate, up):
    f8 = gate.shape[-1]
    pad = jnp.zeros(gate.shape[:-1] + (gu_gap(f8) - f8,), gate.dtype)
    return jnp.concatenate([gate, pad, up], axis=-1)


def split_gu(gu, f8):
    return gu[..., :f8], gu[..., gu_gap(f8):]


def ffn_up(name, h, wgu, f8, layer):
    t, d = h.shape
    nb, _, _, fw = wgu.shape
    gap = gu_gap(f8)
    tm = _row_tile(t, 1024)

    per = 2

    def body(h_ref, w_ref, gu_ref, a_ref):
        hv = h_ref[...]
        rs = [jnp.dot(hv, w_ref[j], preferred_element_type=F32) for j in range(per)]
        for j in range(per):
            gu_ref[j] = rs[j].astype(BF16)
            g, u = rs[j][:, :f8], rs[j][:, gap:]
            a_ref[j] = (g * _sigmoid(g) * u).astype(BF16)

    return pl.pallas_call(
        body, grid=(t // tm, nb // per),
        in_specs=[pl.BlockSpec((tm, d), lambda i, k: (i, 0)),
                  pl.BlockSpec((per, None, d, fw), lambda i, k: (k, layer, 0, 0))],
        out_specs=[pl.BlockSpec((per, tm, fw), lambda i, k: (k, i, 0)),
                   pl.BlockSpec((per, tm, f8), lambda i, k: (k, i, 0))],
        out_shape=[_sds((nb, t, fw), BF16), _sds((nb, t, f8), BF16)],
        compiler_params=_params("parallel", "parallel"), name=name)(h, wgu)


def _mm_all_blocks(name, a, w, layer, dn, tm):
    nb, t, f = a.shape
    wshape = w.shape[2:]
    d = wshape[1] if dn == NN else wshape[0]
    tm = _row_tile(t, tm)

    def body(a_ref, w_ref, o_ref):
        acc = lax.dot_general(a_ref[0], w_ref[0], dn, preferred_element_type=F32)
        for k in range(1, nb):
            acc = acc + lax.dot_general(a_ref[k], w_ref[k], dn, preferred_element_type=F32)
        o_ref[...] = acc

    return pl.pallas_call(
        body, grid=(t // tm,),
        in_specs=[pl.BlockSpec((nb, tm, f), lambda i: (0, i, 0)),
                  pl.BlockSpec((nb, None) + wshape, lambda i: (0, layer, 0, 0))],
        out_specs=pl.BlockSpec((tm, d), lambda i: (i, 0)), out_shape=_sds((t, d), F32),
        compiler_params=_params("parallel"), name=name)(a, w)


def ffn_down(name, a, wd, layer):
    return _mm_all_blocks(name, a, wd, layer, NN, 512)


def ffn_bwd_act(name, dm, wd, gu, f8, layer):
    t, d = dm.shape
    nb, _, fw = gu.shape
    gap = gu_gap(f8)
    tm = _row_tile(t, 1024)

    per = 2

    def body(dm_ref, wd_ref, gu_ref, o_ref):
        dmv = dm_ref[...]
        das = [lax.dot_general(dmv, wd_ref[j], NT, preferred_element_type=F32) for j in range(per)]
        o_ref[...] = jnp.zeros_like(o_ref)
        for j in range(per):
            gv = gu_ref[j, :, :f8].astype(F32)
            uv = gu_ref[j, :, gap:].astype(F32)
            sg = _sigmoid(gv)
            o_ref[j, :, :f8] = (das[j] * uv * sg * (1.0 + gv * (1.0 - sg))).astype(BF16)
            o_ref[j, :, gap:] = (das[j] * gv * sg).astype(BF16)

    bspec = pl.BlockSpec((per, tm, fw), lambda i, k: (k, i, 0))
    return pl.pallas_call(
        body, grid=(t // tm, nb // per),
        in_specs=[pl.BlockSpec((tm, d), lambda i, k: (i, 0)),
                  pl.BlockSpec((per, None, f8, d), lambda i, k: (k, layer, 0, 0)), bspec],
        out_specs=bspec, out_shape=_sds((nb, t, fw), BF16),
        compiler_params=_params("parallel", "parallel"), name=name)(dm, wd, gu)


def ffn_bwd_dh(name, dgu, wgu, layer):
    return _mm_all_blocks(name, dgu, wgu, layer, NT, 512)


def ffn_dw_in(name, h, dact):
    t, d = h.shape
    nb, _, f8 = dact.shape
    tk = _row_tile(t, 2048)
    return _gmm(name, h, dact, grid=(nb, t // tk),
                a_blk=(tk, d), a_idx=lambda b, s: (s, 0), b_blk=(None, tk, f8), b_idx=lambda b, s: (b, s, 0),
                o_blk=(None, d, f8), o_idx=lambda b, s: (b, 0, 0), out_shape=(nb, d, f8), out_dtype=F32, dn=TN,
                acc_shape=(d, f8))


def ffn_dw_down(name, a, dm):
    nb, t, f8 = a.shape
    d = dm.shape[1]
    tk = _row_tile(t, 2048)
    return _gmm(name, a, dm, grid=(nb, t // tk),
                a_blk=(None, tk, f8), a_idx=lambda b, s: (b, s, 0), b_blk=(tk, d), b_idx=lambda b, s: (s, 0),
                o_blk=(None, f8, d), o_idx=lambda b, s: (b, 0, 0), out_shape=(nb, f8, d), out_dtype=F32, dn=TN,
                acc_shape=(f8, d))


GELU_C = math.sqrt(2.0 / math.pi)
GELU_A = 0.044715


def _gelu(x):
    return 0.5 * x * (1.0 + jnp.tanh(GELU_C * (x + GELU_A * x * x * x)))


def _gelu_grad(x):
    th = jnp.tanh(GELU_C * (x + GELU_A * x * x * x))
    return 0.5 * (1.0 + th) + 0.5 * x * (1.0 - th * th) * GELU_C * (1.0 + 3.0 * GELU_A * x * x)


def _neg_expm1(x):
    series = x * (1.0 + x * (0.5 + x * (1.0 / 6.0 + x * (1.0 / 24.0 + x * (1.0 / 120.0 + x * (1.0 / 720.0))))))
    return -jnp.where(x > -0.25, series, jnp.exp(x) - 1.0)


CONV_TR = 256
CONV_TAPS = 4
HALO = 8


def _shifted(ext, k, tr, back):
    if back:
        return pltpu.roll(ext, k, 0)[HALO:, :] if k else ext[HALO:, :]
    return pltpu.roll(ext, tr + HALO - k, 0)[:tr, :] if k else ext[:tr, :]


def conv4_fwd(name, src, cb, w, b):
    t, c = src.shape[0], w.shape[1]
    tr = _row_tile(t, CONV_TR)
    hb = tr // HALO

    def body(x_ref, h_ref, w_ref, b_ref, o_ref):
        i = pl.program_id(0)
        ext = jnp.concatenate([jnp.where(i == 0, 0.0, h_ref[...]), x_ref[...]], axis=0)
        acc = b_ref[...]
        for k in range(CONV_TAPS):
            acc = acc + w_ref[CONV_TAPS - 1 - k:CONV_TAPS - k, :] * _shifted(ext, k, tr, True)
        o_ref[...] = acc

    return pl.pallas_call(
        body, grid=(t // tr,),
        in_specs=[pl.BlockSpec((tr, c), lambda i: (i, cb)),
                  pl.BlockSpec((HALO, c), lambda i: (jnp.maximum(i * hb - 1, 0), cb)),
                  pl.BlockSpec((CONV_TAPS, c), lambda i: (0, 0)), pl.BlockSpec((1, c), lambda i: (0, 0))],
        out_specs=pl.BlockSpec((tr, c), lambda i: (i, 0)), out_shape=_sds((t, c), F32),
        compiler_params=_params("parallel"), name=name)(src, src, w, b)


def conv4_bwd_x(name, dy, w):
    t, c = dy.shape
    tr = _row_tile(t, CONV_TR)
    hb = tr // HALO
    last = t // tr - 1

    def body(y_ref, h_ref, w_ref, o_ref):
        i = pl.program_id(0)
        ext = jnp.concatenate([y_ref[...], jnp.where(i == last, 0.0, h_ref[...])], axis=0)
        acc = w_ref[CONV_TAPS - 1:CONV_TAPS, :] * y_ref[...]
        for k in range(1, CONV_TAPS):
            acc = acc + w_ref[CONV_TAPS - 1 - k:CONV_TAPS - k, :] * _shifted(ext, k, tr, False)
        o_ref[...] = acc

    return pl.pallas_call(
        body, grid=(t // tr,),
        in_specs=[pl.BlockSpec((tr, c), lambda i: (i, 0)),
                  pl.BlockSpec((HALO, c), lambda i: (jnp.minimum((i + 1) * hb, t // HALO - 1), 0)),
                  pl.BlockSpec((CONV_TAPS, c), lambda i: (0, 0))],
        out_specs=pl.BlockSpec((tr, c), lambda i: (i, 0)), out_shape=_sds((t, c), F32),
        compiler_params=_params("parallel"), name=name)(dy, dy, w)


def conv4_bwd_w(name, src, cb, dy):
    t, c = dy.shape
    tr = _row_tile(t, CONV_TR)
    hb = tr // HALO

    def body(x_ref, h_ref, dy_ref, dw_ref, db_ref):
        i = pl.program_id(0)

        @pl.when(i == 0)
        def _():
            dw_ref[...] = jnp.zeros_like(dw_ref)
            db_ref[...] = jnp.zeros_like(db_ref)

        ext = jnp.concatenate([jnp.where(i == 0, 0.0, h_ref[...]), x_ref[...]], axis=0)
        dyv = dy_ref[...]
        db_ref[...] += jnp.sum(dyv, axis=0, keepdims=True)
        for k in range(CONV_TAPS):
            dw_ref[CONV_TAPS - 1 - k:CONV_TAPS - k, :] += jnp.sum(dyv * _shifted(ext, k, tr, True), axis=0,
                                                                   keepdims=True)

    return pl.pallas_call(
        body, grid=(t // tr,),
        in_specs=[pl.BlockSpec((tr, c), lambda i: (i, cb)),
                  pl.BlockSpec((HALO, c), lambda i: (jnp.maximum(i * hb - 1, 0), cb)),
                  pl.BlockSpec((tr, c), lambda i: (i, 0))],
        out_specs=[pl.BlockSpec((CONV_TAPS, c), lambda i: (0, 0)), pl.BlockSpec((1, c), lambda i: (0, 0))],
        out_shape=[_sds((CONV_TAPS, c), F32), _sds((1, c), F32)],
        compiler_params=_params("arbitrary"), name=name)(src, src, dy)


def _rg_gate_values(xcv, wa_ref, wi_ref, ba_ref, bi_ref, lam_ref):
    xb = xcv.astype(BF16)
    r = _sigmoid(jnp.dot(xb, wa_ref[...], preferred_element_type=F32) + ba_ref[...])
    ig = _sigmoid(jnp.dot(xb, wi_ref[...], preferred_element_type=F32) + bi_ref[...])
    lam = lam_ref[...]
    sp = jnp.maximum(-lam, 0.0) + jnp.log(1.0 + jnp.exp(-jnp.abs(lam)))
    log_a = -LRU_C * r * sp
    a = jnp.exp(log_a)
    mult = jnp.sqrt(_neg_expm1(2.0 * log_a))
    return xb, r, ig, sp, a, mult


def rg_gates_fwd(name, xc, wa, wi, ba, bi, lam):
    t, c = xc.shape
    nb, cb, _ = wa.shape
    tm = _row_tile(t, 512)

    def body(xc_ref, wa_ref, wi_ref, ba_ref, bi_ref, lam_ref, a_ref, u_ref):
        xcv = xc_ref[...]
        _, _, ig, _, a, mult = _rg_gate_values(xcv, wa_ref, wi_ref, ba_ref, bi_ref, lam_ref)
        a_ref[...] = a
        u_ref[...] = mult * (ig * xcv)

    blk = pl.BlockSpec((tm, cb), lambda i, n: (i, n))
    wsp = pl.BlockSpec((None, cb, cb), lambda i, n: (n, 0, 0))
    vec = pl.BlockSpec((1, cb), lambda i, n: (0, n))
    return pl.pallas_call(
        body, grid=(t // tm, nb), in_specs=[blk, wsp, wsp, vec, vec, vec], out_specs=[blk, blk],
        out_shape=[_sds((t, c), F32)] * 2, compiler_params=_params("parallel", "parallel"), name=name,
    )(xc, wa, wi, ba, bi, lam)


def rg_gates_bwd(name, xc, gu, hprev, wa, wi, ba, bi, lam):
    t, c = xc.shape
    nb, cb, _ = wa.shape
    tm = _row_tile(t, 512)

    def body(xc_ref, gu_ref, hp_ref, wa_ref, wi_ref, ba_ref, bi_ref, lam_ref,
             dxc_ref, dwa_ref, dwi_ref, dba_ref, dbi_ref, dlam_ref):
        i = pl.program_id(1)

        @pl.when(i == 0)
        def _():
            for ref in (dwa_ref, dwi_ref, dba_ref, dbi_ref, dlam_ref):
                ref[...] = jnp.zeros_like(ref)

        xcv = xc_ref[...]
        xb, r, ig, sp, a, mult = _rg_gate_values(xcv, wa_ref, wi_ref, ba_ref, bi_ref, lam_ref)
        gv = gu_ref[...]
        d_ixc = gv * mult
        d_i = d_ixc * xcv
        d_mult = gv * ig * xcv
        d_a = gv * hp_ref[...] - d_mult * a / mult
        d_log_a = d_a * a
        d_r = d_log_a * (-LRU_C * sp)
        sig_neg_lam = 1.0 / (1.0 + jnp.exp(lam_ref[...]))
        dlam_ref[...] += jnp.sum(d_log_a * r, axis=0, keepdims=True) * (LRU_C * sig_neg_lam)
        dpa = d_r * r * (1.0 - r)
        dpi = d_i * ig * (1.0 - ig)
        dba_ref[...] += jnp.sum(dpa, axis=0, keepdims=True)
        dbi_ref[...] += jnp.sum(dpi, axis=0, keepdims=True)
        dpab, dpib = dpa.astype(BF16), dpi.astype(BF16)
        dxc_ref[...] = (d_ixc * ig + lax.dot_general(dpab, wa_ref[...], NT, preferred_element_type=F32)
                        + lax.dot_general(dpib, wi_ref[...], NT, preferred_element_type=F32))
        dwa_ref[...] += lax.dot_general(xb, dpab, TN, preferred_element_type=F32)
        dwi_ref[...] += lax.dot_general(xb, dpib, TN, preferred_element_type=F32)

    blk = pl.BlockSpec((tm, cb), lambda n, i: (i, n))
    wsp = pl.BlockSpec((None, cb, cb), lambda n, i: (n, 0, 0))
    vec = pl.BlockSpec((1, cb), lambda n, i: (0, n))
    return pl.pallas_call(
        body, grid=(nb, t // tm), in_specs=[blk, blk, blk, wsp, wsp, vec, vec, vec],
        out_specs=[blk, wsp, wsp, vec, vec, vec],
        out_shape=[_sds((t, c), F32), _sds((nb, cb, cb), F32), _sds((nb, cb, cb), F32),
                   _sds((1, c), F32), _sds((1, c), F32), _sds((1, c), F32)],
        compiler_params=_params("parallel", "arbitrary"), name=name)(xc, gu, hprev, wa, wi, ba, bi, lam)


SCAN_TS = 256
SCAN_TC = 512


def _tile_scan(a, b, reverse):
    ts = a.shape[0]
    row = lax.broadcasted_iota(jnp.int32, a.shape, 0)
    d = 1
    while d < ts:
        if reverse:
            inside = row < ts - d
            a_sh = jnp.where(inside, pltpu.roll(a, ts - d, 0), 1.0)
            b_sh = jnp.where(inside, pltpu.roll(b, ts - d, 0), 0.0)
        else:
            inside = row >= d
            a_sh = jnp.where(inside, pltpu.roll(a, d, 0), 1.0)
            b_sh = jnp.where(inside, pltpu.roll(b, d, 0), 0.0)
        b = b + a * b_sh
        a = a * a_sh
        d *= 2
    return a, b


def rg_scan_fwd(name, a, u, gate_pre):
    t, c = a.shape
    ts, tc = _row_tile(t, SCAN_TS), _row_tile(c, SCAN_TC)

    def body(a_ref, u_ref, g_ref, h_ref, z_ref, carry_ref):
        s = pl.program_id(1)

        @pl.when(s == 0)
        def _():
            carry_ref[...] = jnp.zeros_like(carry_ref)

        ac, bc = _tile_scan(a_ref[...], u_ref[...], False)
        h = bc + ac * carry_ref[0:1, :]
        h_ref[...] = h
        z_ref[...] = (h * _gelu(g_ref[...])).astype(BF16)
        carry_ref[0:1, :] = h[ts - 1:ts, :]

    blk = pl.BlockSpec((ts, tc), lambda j, s: (s, j))
    return pl.pallas_call(
        body, grid=(c // tc, t // ts), in_specs=[blk, blk, blk], out_specs=[blk, blk],
        out_shape=[_sds((t, c), F32), _sds((t, c), BF16)], scratch_shapes=[pltpu.VMEM((8, tc), F32)],
        compiler_params=_params("parallel", "arbitrary"), name=name)(a, u, gate_pre)


def rg_scan_bwd(name, a_next, hs, gate_pre, dz):
    t, c = hs.shape
    ts, tc = _row_tile(t, SCAN_TS), _row_tile(c, SCAN_TC)
    nt = t // ts

    def body(an_ref, h_ref, g_ref, dz_ref, gu_ref, dgate_ref, carry_ref):
        s = pl.program_id(1)

        @pl.when(s == 0)
        def _():
            carry_ref[...] = jnp.zeros_like(carry_ref)

        gate = g_ref[...]
        dzv = dz_ref[...]
        dgate_ref[...] = (dzv * h_ref[...] * _gelu_grad(gate)).astype(BF16)
        ac, bc = _tile_scan(an_ref[...], dzv * _gelu(gate), True)
        gu = bc + ac * carry_ref[0:1, :]
        gu_ref[...] = gu
        carry_ref[0:1, :] = gu[0:1, :]

    blk = pl.BlockSpec((ts, tc), lambda j, s: (nt - 1 - s, j))
    return pl.pallas_call(
        body, grid=(c // tc, nt), in_specs=[blk, blk, blk, blk], out_specs=[blk, blk],
        out_shape=[_sds((t, c), F32), _sds((t, c), BF16)], scratch_shapes=[pltpu.VMEM((8, tc), F32)],
        compiler_params=_params("parallel", "arbitrary"), name=name)(a_next, hs, gate_pre, dz)


def _shift_down(x, k):
    return jnp.pad(x, ((k, 0), (0, 0)))[:x.shape[0]] if k else x


def _shift_up(x, k):
    return jnp.pad(x, ((0, k), (0, 0)))[k:] if k else x


QA_BLK, KA_BLK, VA_BLK, QS_BLK, KS_BLK, VS_BLK = (g * N_PAIRS for g in range(6))


TOEP_W = 640
TOEP_FLAT = 320
TABLE_LOW = 193


def rel_bias_matrix(name, table):
    h = table.shape[0]
    diag = jnp.concatenate([jnp.repeat(table[:, 2 * REL_CLIP:], TOEP_FLAT, axis=1),
                            jnp.flip(table[:, TABLE_LOW:2 * REL_CLIP], axis=1),
                            jnp.zeros((h, 1), table.dtype)], axis=1)[:, None, :]

    def body(v_ref, o_ref):
        rows = jnp.broadcast_to(v_ref[...], (CHUNK, TOEP_W))
        o_ref[...] = pltpu.roll(rows, TOEP_W - (CHUNK - 1), 1, stride=1, stride_axis=0)

    out = pl.pallas_call(
        body, grid=(h,), in_specs=[pl.BlockSpec((None, 1, TOEP_W), lambda hh: (hh, 0, 0))],
        out_specs=pl.BlockSpec((None, CHUNK, TOEP_W), lambda hh: (hh, 0, 0)),
        out_shape=_sds((h, CHUNK, TOEP_W), F32), compiler_params=_params("parallel"), name=name)(diag)
    return out[:, :, :BAND]


def rel_bias_grad(name, dbias):
    h = dbias.shape[0]
    flipped = jnp.pad(jnp.flip(dbias, axis=1), ((0, 0), (0, 0), (0, TOEP_W - BAND)))

    def body(x_ref, o_ref):
        skew = pltpu.roll(x_ref[...], 0, 1, stride=1, stride_axis=0)
        col = jnp.sum(skew, axis=0, keepdims=True)
        lane = lax.broadcasted_iota(jnp.int32, col.shape, 1)
        flat = jnp.sum(jnp.where(lane < TOEP_FLAT, col, 0.0), axis=1, keepdims=True)
        o_ref[...] = jnp.where(lane == TOEP_W - 1, flat, col)

    out = pl.pallas_call(
        body, grid=(h,), in_specs=[pl.BlockSpec((None, CHUNK, TOEP_W), lambda hh: (hh, 0, 0))],
        out_specs=pl.BlockSpec((None, 1, TOEP_W), lambda hh: (hh, 0, 0)),
        out_shape=_sds((h, 1, TOEP_W), F32), compiler_params=_params("parallel"), name=name)(flipped)[:, 0, :]
    return jnp.concatenate([jnp.zeros((h, TABLE_LOW), F32), jnp.flip(out[:, TOEP_FLAT:TOEP_W - 1], axis=1),
                            out[:, TOEP_W - 1:]], axis=1)


def attn_layer_fwd(tag, x, h, w, g_next):
    proj = mm_nn_wblk(tag + "_proj", h, w["w_in"], w["idx"], BF16)
    width = N_PAIRS * PAIR
    pad = lambda a: jnp.pad(a, ((PAD_KEYS, 0), (0, 0)))
    kap, vap = pad(proj[:, width:2 * width]), pad(proj[:, 2 * width:3 * width])
    bias = rel_bias_matrix(tag + "_bias", w["rel_bias"])
    oa, lse = attn_a_fwd(tag + "_a", proj, kap, vap, bias, QA_BLK)
    ob = sb_fwd(tag + "_sb", proj, QS_BLK, KS_BLK, VS_BLK)
    o = jnp.concatenate([oa, ob], axis=1).astype(BF16)
    m = mm_nn(tag + "_out", o, w["w_out"], F32)
    x1, h_next = resid_norm_fwd(tag + "_res", x, m, w["g_post"], g_next)
    return x1, h_next, (x, h, proj, kap, vap, bias, oa, lse, ob, o, m)


def attn_layer_bwd(tag, dm, dx1, saved, w, prev):
    x, h, proj, kap, vap, bias, oa, lse, ob, o, m = saved
    d_w_out = mm_tn(tag + "_dwout", o, dm, F32)
    do = mm_nt(tag + "_do", dm, w["w_out"], BF16)
    dqa, dkap, dvap, dbias = attn_a_bwd(tag + "_da", proj, kap, vap, bias, oa, lse, do, QA_BLK, 0)
    dqs, dks, dvs = sb_bwd(tag + "_dsb", proj, ob, do, QS_BLK, KS_BLK, VS_BLK, N_PAIRS)
    d_rel = rel_bias_grad(tag + "_dbias", dbias)
    dproj = jnp.concatenate([dqa, dkap[PAD_KEYS:], dvap[PAD_KEYS:], dqs, dks, dvs], axis=1).astype(BF16)
    d_w_in = mm_tn_oblk(tag + "_dwin", h, dproj, w["w_in"].shape[3], F32)
    dh = mm_nt_wblk(tag + "_dh", dproj, w["w_in"], w["idx"], F32)
    dx, dg_pre, dm_prev, dg_post_prev = close_bwd(tag, dh, x, w["g_pre"], dx1, prev)
    return dx, dm_prev, dg_post_prev, dict(w_in=d_w_in, w_out=d_w_out, rel_bias=d_rel, g_pre=dg_pre)


def rg_layer_fwd(tag, x, h, w, g_next):
    proj = mm_nn_wblk(tag + "_proj", h, w["w_in"], w["idx"], F32)
    xc = conv4_fwd(tag + "_conv", proj, 1, w["conv_w"], w["conv_b"])
    a, u = rg_gates_fwd(tag + "_gates", xc, w["w_a"], w["w_i"], w["b_a"], w["b_i"], w["lam"])
    hs, z = rg_scan_fwd(tag + "_scan", a, u, proj)
    m = mm_nn(tag + "_out", z, w["w_out"], F32)
    x1, h_next = resid_norm_fwd(tag + "_res", x, m, w["g_post"], g_next)
    return x1, h_next, (x, h, proj, xc, a, hs, z, m)


def rg_layer_bwd(tag, dm, dx1, saved, w, prev):
    x, h, proj, xc, a, hs, z, m = saved
    d_w_out = mm_tn(tag + "_dwout", z, dm, F32)
    dz = mm_nt(tag + "_dz", dm, w["w_out"], F32)
    gu, dgate = rg_scan_bwd(tag + "_dscan", _shift_up(a, 1), hs, proj, dz)
    dxc, d_w_a, d_w_i, d_b_a, d_b_i, d_lam = rg_gates_bwd(
        tag + "_dgates", xc, gu, _shift_down(hs, 1), w["w_a"], w["w_i"], w["b_a"], w["b_i"], w["lam"])
    d_conv_w, d_conv_b = conv4_bwd_w(tag + "_dconvw", proj, 1, dxc)
    dxr = conv4_bwd_x(tag + "_dconv", dxc, w["conv_w"])
    dproj = jnp.concatenate([dgate, dxr.astype(BF16)], axis=1)
    d_w_in = mm_tn_oblk(tag + "_dwin", h, dproj, w["w_in"].shape[3], F32)
    dh = mm_nt_wblk(tag + "_dh", dproj, w["w_in"], w["idx"], F32)
    dx, dg_pre, dm_prev, dg_post_prev = close_bwd(tag, dh, x, w["g_pre"], dx1, prev)
    return dx, dm_prev, dg_post_prev, dict(w_in=d_w_in, w_out=d_w_out, conv_w=d_conv_w, conv_b=d_conv_b, w_a=d_w_a,
                                           w_i=d_w_i, b_a=d_b_a, b_i=d_b_i, lam=d_lam, g_pre=dg_pre)


def ffn_layer_fwd(tag, x, h, w, g_next):
    f8 = w["w_down"].shape[2]
    gu, a = ffn_up(tag + "_up", h, w["w_gu"], f8, w["idx"])
    f = ffn_down(tag + "_down", a, w["w_down"], w["idx"])
    x1, h_next = resid_norm_fwd(tag + "_res", x, f, w["g_post"], g_next)
    return x1, h_next, (x, h, gu, a, f)


def ffn_layer_bwd(tag, dm, dx1, saved, w, prev):
    x, h, gu, a, f = saved
    f8 = w["w_down"].shape[2]
    d_w_down = ffn_dw_down(tag + "_dwdown", a, dm)
    dgu = ffn_bwd_act(tag + "_dact", dm, w["w_down"], gu, f8, w["idx"])
    d_w_gu = ffn_dw_in(tag + "_dwgu", h, dgu)
    dh = ffn_bwd_dh(tag + "_dh", dgu, w["w_gu"], w["idx"])
    dx, dg_pre, dm_prev, dg_post_prev = close_bwd(tag, dh, x, w["g_pre"], dx1, prev)
    return dx, dm_prev, dg_post_prev, dict(w_gu=d_w_gu, w_down=d_w_down, g_pre=dg_pre)


def _place():
    return lax.axis_index("x"), lax.axis_index("y"), lax.axis_index("c")


def all_gather(name, blks):
    n = len(blks)

    def body(*refs):
        x_refs, out_refs = refs[:n], refs[n:2 * n]
        send_sems, recv_sems, local_sems = refs[2 * n:]
        x, y, cc = _place()
        me, sibling = (x, y, cc), (x, y, 1 - cc)
        chips = [(1 - x, y), (x, 1 - y), (1 - x, 1 - y)]
        south = cc == 0
        via = (jnp.where(south, 1 - x, x), jnp.where(south, y, 1 - y))
        onward = (jnp.where(south, x, 1 - x), jnp.where(south, 1 - y, y))
        k_via, k_onward = 1 + cc, 2 - cc

        def slot(a, px, py, pc):
            return out_refs[a].at[4 * px + 2 * py + pc]

        def copy(a, k, block, to, src=None):
            return pltpu.make_async_remote_copy(
                src_ref=slot(a, *block) if src is None else src, dst_ref=slot(a, *block),
                send_sem=send_sems.at[7 * a + k], recv_sem=recv_sems.at[7 * a + k], device_id=to, device_id_type=MESH)

        mine = [pltpu.make_async_copy(x_refs[a], slot(a, *me), local_sems.at[a]) for a in range(n)]
        sends = []
        for a in range(n):
            mine[a].start()
            sends.append(copy(a, 0, me, sibling, src=x_refs[a]))
            sends += [copy(a, 1 + j, me, (*chips[j], cc), src=x_refs[a]) for j in range(2)]
        for cp in sends:
            cp.start()
        for a in range(n):
            copy(a, k_via, (*via, cc), me).wait_recv()
            sends.append(copy(a, 3, (*via, cc), (*onward, cc)))
            sends.append(copy(a, 3 + k_via, (*via, cc), sibling))
            sends[-2].start()
            sends[-1].start()
        for a in range(n):
            copy(a, k_onward, (*onward, cc), me).wait_recv()
            sends.append(copy(a, 3 + k_onward, (*onward, cc), sibling))
            sends[-1].start()
        for a in range(n):
            copy(a, 3, (*chips[2], cc), me).wait_recv()
            sends.append(copy(a, 6, (*chips[2], cc), sibling))
            sends[-1].start()
        for a in range(n):
            copy(a, 0, sibling, me).wait_recv()
            for j, chip in enumerate(chips):
                copy(a, 4 + j, (*chip, 1 - cc), me).wait_recv()
        for cp in sends:
            cp.wait_send()
        for cp in mine:
            cp.wait()

    return pl.pallas_call(
        body, out_shape=[_sds((N_DEV,) + b.shape, b.dtype) for b in blks], in_specs=[ANY] * n, out_specs=[ANY] * n,
        scratch_shapes=[pltpu.SemaphoreType.DMA((7 * n,)), pltpu.SemaphoreType.DMA((7 * n,)),
                        pltpu.SemaphoreType.DMA((n,))],
        name=name)(*blks)


def exchange_pair(name, gs):
    n = len(gs)
    nchip = 4

    def body(*refs):
        g_refs, land_refs = refs[:n], refs[n:2 * n]
        send_sems, recv_sems = refs[2 * n:]
        x, y, cc = _place()
        copies = [pltpu.make_async_remote_copy(
            src_ref=g_refs[a].at[j, 1 - cc], dst_ref=land_refs[a].at[j], send_sem=send_sems.at[nchip * a + j],
            recv_sem=recv_sems.at[nchip * a + j], device_id=(x, y, 1 - cc), device_id_type=MESH)
            for a in range(n) for j in range(nchip)]
        for cp in copies:
            cp.start()
        for cp in copies:
            cp.wait()

    return pl.pallas_call(
        body, out_shape=[_sds((nchip,) + g.shape[2:], g.dtype) for g in gs], in_specs=[ANY] * n, out_specs=[ANY] * n,
        scratch_shapes=[pltpu.SemaphoreType.DMA((nchip * n,)), pltpu.SemaphoreType.DMA((nchip * n,))],
        name=name)(*gs)


def pair_sum(name, g, land, core, out_dtype):
    nchip, _, r, c = g.shape
    tr = _divisor_tile(r, 1024, 16)

    def body(core_ref, g_ref, l_ref, o_ref):
        o_ref[...] = (g_ref[...] + l_ref[...]).astype(o_ref.dtype)

    return pl.pallas_call(
        body,
        grid_spec=pltpu.PrefetchScalarGridSpec(
            num_scalar_prefetch=1, grid=(nchip, r // tr),
            in_specs=[pl.BlockSpec((None, None, tr, c), lambda j, i, core_ref: (j, core_ref[0], i, 0)),
                      pl.BlockSpec((None, tr, c), lambda j, i, core_ref: (j, i, 0))],
            out_specs=pl.BlockSpec((None, tr, c), lambda j, i, core_ref: (j, i, 0))),
        out_shape=_sds((nchip, r, c), out_dtype), compiler_params=_params("parallel", "parallel"), name=name,
    )(core, g, land)


def exchange_chips(name, ps):
    n = len(ps)

    def body(*refs):
        p_refs, land_refs = refs[:n], refs[n:2 * n]
        send_sems, recv_sems, local_sems = refs[2 * n:]
        x, y, cc = _place()
        mine = 2 * x + y
        chips = [(1 - x, y), (x, 1 - y), (1 - x, 1 - y)]
        own = [pltpu.make_async_copy(p_refs[a].at[mine], land_refs[a].at[mine], local_sems.at[a]) for a in range(n)]
        for cp in own:
            cp.start()
        sends = [pltpu.make_async_remote_copy(
            src_ref=p_refs[a].at[2 * px + py], dst_ref=land_refs[a].at[mine], send_sem=send_sems.at[3 * a + k],
            recv_sem=recv_sems.at[3 * a + k], device_id=(px, py, cc), device_id_type=MESH)
            for a in range(n) for k, (px, py) in enumerate(chips)]
        for cp in sends:
            cp.start()
        for a in range(n):
            for k, (px, py) in enumerate(chips):
                pltpu.make_async_remote_copy(
                    src_ref=p_refs[a].at[mine], dst_ref=land_refs[a].at[2 * px + py], send_sem=send_sems.at[3 * a + k],
                    recv_sem=recv_sems.at[3 * a + k], device_id=(px, py, cc), device_id_type=MESH).wait_recv()
        for cp in sends:
            cp.wait_send()
        for cp in own:
            cp.wait()

    return pl.pallas_call(
        body, out_shape=[_sds(p.shape, p.dtype) for p in ps], in_specs=[ANY] * n, out_specs=[ANY] * n,
        scratch_shapes=[pltpu.SemaphoreType.DMA((3 * n,)), pltpu.SemaphoreType.DMA((3 * n,)),
                        pltpu.SemaphoreType.DMA((n,))],
        name=name)(*ps)


def adamw(name, parts, w, m, v):
    npart, r, c = parts.shape
    tr = _divisor_tile(r, 512, 16)
    c1 = 1.0 / (1.0 - ADAM_B1 ** ADAM_STEP)
    c2 = 1.0 / (1.0 - ADAM_B2 ** ADAM_STEP)

    def body(p_ref, w_ref, m_ref, v_ref, g_ref, d_ref, nm_ref, nv_ref):
        g = p_ref[0].astype(F32)
        for j in range(1, npart):
            g = g + p_ref[j].astype(F32)
        nm = ADAM_B1 * m_ref[...] + (1.0 - ADAM_B1) * g
        nv = ADAM_B2 * v_ref[...] + (1.0 - ADAM_B2) * (g * g)
        g_ref[...] = g
        nm_ref[...] = nm
        nv_ref[...] = nv
        d_ref[...] = -ADAM_LR * ((nm * c1) / (jnp.sqrt(nv * c2) + ADAM_EPS) + ADAM_WD * w_ref[...])

    row = pl.BlockSpec((tr, c), lambda i: (i, 0))
    return pl.pallas_call(
        body, grid=(r // tr,), in_specs=[pl.BlockSpec((npart, tr, c), lambda i: (0, i, 0)), row, row, row],
        out_specs=[row] * 4, out_shape=[_sds((r, c), F32)] * 4, compiler_params=_params("parallel"), name=name,
    )(parts, w, m, v)


def _pack(arrays, dtype, row_multiple):
    flat = jnp.concatenate([a.astype(dtype).reshape(-1) for a in arrays])
    per = row_multiple * LANES
    total = -(-flat.shape[0] // per) * per
    return jnp.pad(flat, (0, total - flat.shape[0])).reshape(total // LANES, LANES)


def _pack_blocked(arrays, dtype, row_multiple):
    flat = jnp.concatenate([a.astype(dtype).reshape(N_DEV, -1) for a in arrays], axis=1)
    per = row_multiple * LANES
    total = -(-flat.shape[1] // per) * per
    return jnp.pad(flat, ((0, 0), (0, total - flat.shape[1]))).reshape(N_DEV, total // LANES, LANES)


def _unpack(buf, shapes, lead=()):
    flat = buf.reshape(lead + (-1,))
    out, off = [], 0
    for s in shapes:
        n = math.prod(s)
        out.append(flat[..., off:off + n].reshape(lead + tuple(s)))
        off += n
    return out


def _to_blocked(full, ax):
    s = full.shape
    return jnp.moveaxis(full.reshape(s[:ax] + (N_DEV, s[ax] // N_DEV) + s[ax + 1:]), ax, 0)


def _from_blocked(blk, ax):
    moved = jnp.moveaxis(blk, 0, ax)
    s = moved.shape
    return moved.reshape(s[:ax] + (s[ax] * s[ax + 1],) + s[ax + 2:])


SMALL = ("rg_conv_w", "rg_conv_b", "rg_b_a", "rg_b_i", "rg_lambda")
GU = "ffn_w_gu"
BIG = ("attn_w_in", "attn_w_out", "rg_w_in", "rg_w_a", "rg_w_i", "rg_w_out", GU, "ffn_w_down")


def kernel(x, attn_w_in, attn_rel_bias, attn_w_out, rg_w_in, rg_conv_w, rg_conv_b, rg_w_a, rg_b_a, rg_w_i, rg_b_i, rg_lambda, rg_w_out, norm_mix_pre, norm_mix_post, norm_ffn_pre, norm_ffn_post, ffn_w_gate, ffn_w_up, ffn_w_down, loss_target, m_attn_w_in, m_attn_rel_bias, m_attn_w_out, m_rg_w_in, m_rg_conv_w, m_rg_conv_b, m_rg_w_a, m_rg_b_a, m_rg_w_i, m_rg_b_i, m_rg_lambda, m_rg_w_out, m_norm_mix_pre, m_norm_mix_post, m_norm_ffn_pre, m_norm_ffn_post, m_ffn_w_gate, m_ffn_w_up, m_ffn_w_down, v_attn_w_in, v_attn_rel_bias, v_attn_w_out, v_rg_w_in, v_rg_conv_w, v_rg_conv_b, v_rg_w_a, v_rg_b_a, v_rg_w_i, v_rg_b_i, v_rg_lambda, v_rg_w_out, v_norm_mix_pre, v_norm_mix_post, v_norm_ffn_pre, v_norm_ffn_post, v_ffn_w_gate, v_ffn_w_up, v_ffn_w_down):
    w_loc = dict(attn_w_in=attn_w_in, attn_rel_bias=attn_rel_bias, attn_w_out=attn_w_out, rg_w_in=rg_w_in,
                 rg_conv_w=rg_conv_w, rg_conv_b=rg_conv_b, rg_w_a=rg_w_a, rg_b_a=rg_b_a, rg_w_i=rg_w_i, rg_b_i=rg_b_i,
                 rg_lambda=rg_lambda, rg_w_out=rg_w_out, norm_mix_pre=norm_mix_pre, norm_mix_post=norm_mix_post,
                 norm_ffn_pre=norm_ffn_pre, norm_ffn_post=norm_ffn_post, ffn_w_gate=ffn_w_gate, ffn_w_up=ffn_w_up,
                 ffn_w_down=ffn_w_down)
    m_loc = dict(attn_w_in=m_attn_w_in, attn_rel_bias=m_attn_rel_bias, attn_w_out=m_attn_w_out, rg_w_in=m_rg_w_in,
                 rg_conv_w=m_rg_conv_w, rg_conv_b=m_rg_conv_b, rg_w_a=m_rg_w_a, rg_b_a=m_rg_b_a, rg_w_i=m_rg_w_i,
                 rg_b_i=m_rg_b_i, rg_lambda=m_rg_lambda, rg_w_out=m_rg_w_out, norm_mix_pre=m_norm_mix_pre,
                 norm_mix_post=m_norm_mix_post, norm_ffn_pre=m_norm_ffn_pre, norm_ffn_post=m_norm_ffn_post,
                 ffn_w_gate=m_ffn_w_gate, ffn_w_up=m_ffn_w_up, ffn_w_down=m_ffn_w_down)
    v_loc = dict(attn_w_in=v_attn_w_in, attn_rel_bias=v_attn_rel_bias, attn_w_out=v_attn_w_out, rg_w_in=v_rg_w_in,
                 rg_conv_w=v_rg_conv_w, rg_conv_b=v_rg_conv_b, rg_w_a=v_rg_w_a, rg_b_a=v_rg_b_a, rg_w_i=v_rg_w_i,
                 rg_b_i=v_rg_b_i, rg_lambda=v_rg_lambda, rg_w_out=v_rg_w_out, norm_mix_pre=v_norm_mix_pre,
                 norm_mix_post=v_norm_mix_post, norm_ffn_pre=v_norm_ffn_pre, norm_ffn_post=v_norm_ffn_post,
                 ffn_w_gate=v_ffn_w_gate, ffn_w_up=v_ffn_w_up, ffn_w_down=v_ffn_w_down)
    axis_of = dict(SHARDED)
    xt, target = x[0], loss_target[0]
    d_model = xt.shape[1]
    rows2d = lambda a: a.reshape(-1, a.shape[-1])
    small_shapes = [w_loc[n].shape for n in SMALL]
    f8 = ffn_w_gate.shape[-1]
    for d in (w_loc, m_loc, v_loc):
        d[GU] = merge_gu(d["ffn_w_gate"], d["ffn_w_up"])

    gathered = all_gather("gather_weights", [rows2d(w_loc[n]).astype(BF16) for n in BIG]
                          + [_pack([w_loc[n] for n in SMALL], F32, 8)])
    blocked = {n: g.reshape((N_DEV,) + w_loc[n].shape) for n, g in zip(BIG, gathered)}
    blocked.update(zip(SMALL, _unpack(gathered[-1], small_shapes, (N_DEV,))))
    full = {n: _from_blocked(blocked[n], axis_of[n]) for n in SMALL}
    row = lambda a: a.reshape(1, -1).astype(F32)
    square = lambda rows8: rows8.reshape(-1, rows8.shape[-1])
    gates = lambda g: jnp.swapaxes(g, 0, 1).reshape(LRU_BLOCKS, -1, g.shape[-1])

    def layer_weights(layer):
        j = layer // 2
        norms = dict(g_pre=row(norm_mix_pre[layer]), g_post=row(norm_mix_post[layer]), idx=j)
        if layer % 2 == 0:
            mix = dict(w_in=blocked["attn_w_in"], w_out=square(blocked["attn_w_out"][:, j]),
                       rel_bias=attn_rel_bias[j], **norms)
        else:
            mix = dict(w_in=blocked["rg_w_in"], w_out=square(blocked["rg_w_out"][:, j]),
                       conv_w=full["rg_conv_w"][j][:, 0, :], conv_b=row(full["rg_conv_b"][j]),
                       w_a=gates(blocked["rg_w_a"][:, j]), w_i=gates(blocked["rg_w_i"][:, j]),
                       b_a=row(full["rg_b_a"][j]), b_i=row(full["rg_b_i"][j]), lam=row(full["rg_lambda"][j]), **norms)
        ffn = dict(w_gu=blocked[GU], w_down=blocked["ffn_w_down"], idx=layer,
                   g_pre=row(norm_ffn_pre[layer]), g_post=row(norm_ffn_post[layer]))
        return mix, ffn

    weights = [layer_weights(layer) for layer in range(DEPTH)]
    act, tape = xt, []
    h = rmsnorm_fwd("l0_mix_norm", act, weights[0][0]["g_pre"])
    for layer in range(DEPTH):
        mix_w, ffn_w = weights[layer]
        mixer_fwd = attn_layer_fwd if layer % 2 == 0 else rg_layer_fwd
        act, h, saved_mix = mixer_fwd(f"l{layer}_mix", act, h, mix_w, ffn_w["g_pre"])
        g_next = weights[layer + 1][0]["g_pre"] if layer + 1 < DEPTH else None
        act, h, saved_ffn = ffn_layer_fwd(f"l{layer}_ffn", act, h, ffn_w, g_next)
        tape.append((mix_w, ffn_w, saved_mix, saved_ffn))
    dact, sq = loss_grad("loss", act, target)
    loss = lax.psum(0.5 * jnp.sum(sq) / d_model, ("x", "y", "c"))

    grads = {}
    last = tape[DEPTH - 1]
    dm, grads[("ffn_post", DEPTH - 1)] = norm_bwd(f"l{DEPTH - 1}_ffn_dpost", dact, last[3][-1], last[1]["g_post"],
                                                 None, BF16)
    for layer in reversed(range(DEPTH)):
        mix_w, ffn_w, saved_mix, saved_ffn = tape[layer]
        dact, dm, grads[("mix_post", layer)], grads[("ffn", layer)] = ffn_layer_bwd(
            f"l{layer}_ffn", dm, dact, saved_ffn, ffn_w, (saved_mix[-1], mix_w["g_post"]))
        mixer_bwd = attn_layer_bwd if layer % 2 == 0 else rg_layer_bwd
        prev = (tape[layer - 1][3][-1], tape[layer - 1][1]["g_post"]) if layer else None
        dact, dm, grads[("ffn_post", layer - 1)], grads[("mix", layer)] = mixer_bwd(
            f"l{layer}_mix", dm, dact, saved_mix, mix_w, prev)
    attn_g = [grads[("mix", l)] for l in range(0, DEPTH, 2)]
    rg_g = [grads[("mix", l)] for l in range(1, DEPTH, 2)]
    ffn_g = [grads[("ffn", l)] for l in range(DEPTH)]
    stack = lambda gs, key: jnp.stack([g[key] for g in gs])
    by_owner = lambda gs, key, f: jnp.stack([f(g[key]) for g in gs], axis=1)
    rows8 = lambda a: a.reshape(N_DEV, -1, a.shape[-1])
    ungates = lambda a: jnp.swapaxes(a.reshape(LRU_BLOCKS, N_DEV, -1, a.shape[-1]), 0, 1)
    same = lambda a: a
    blocked_g = dict(
        attn_w_in=by_owner(attn_g, "w_in", same), attn_w_out=by_owner(attn_g, "w_out", rows8),
        rg_w_in=by_owner(rg_g, "w_in", same), rg_w_out=by_owner(rg_g, "w_out", rows8),
        rg_w_a=by_owner(rg_g, "w_a", ungates), rg_w_i=by_owner(rg_g, "w_i", ungates),
        **{GU: by_owner(ffn_g, "w_gu", same)},
        ffn_w_down=by_owner(ffn_g, "w_down", same))
    contrib = dict(
        attn_rel_bias=stack(attn_g, "rel_bias"), rg_conv_w=stack(rg_g, "conv_w")[:, :, None, :],
        rg_conv_b=stack(rg_g, "conv_b")[:, 0], rg_b_a=stack(rg_g, "b_a").reshape(rg_b_a.shape[0], LRU_BLOCKS, -1),
        rg_b_i=stack(rg_g, "b_i").reshape(rg_b_i.shape[0], LRU_BLOCKS, -1), rg_lambda=stack(rg_g, "lam")[:, 0],
        norm_mix_pre=jnp.concatenate([grads[("mix", l)]["g_pre"] for l in range(DEPTH)]),
        norm_mix_post=jnp.concatenate([grads[("mix_post", l)] for l in range(DEPTH)]),
        norm_ffn_pre=jnp.concatenate([g["g_pre"] for g in ffn_g]),
        norm_ffn_post=jnp.concatenate([grads[("ffn_post", l)] for l in range(DEPTH)]),
    )
    small_g = _pack_blocked([_to_blocked(contrib[n], axis_of[n]) for n in SMALL], F32, 8)

    slabs = [blocked_g[n].reshape(4, 2, -1, blocked_g[n].shape[-1]) for n in BIG] + [small_g.reshape(4, 2, -1, LANES)]
    core = lax.axis_index("c").astype(jnp.int32).reshape(1)
    from_sibling = exchange_pair("rs_pair", slabs)
    pairs = [pair_sum(f"rs_pair_sum_{i}", g, l, core, BF16 if i < len(BIG) else F32)
             for i, (g, l) in enumerate(zip(slabs, from_sibling))]
    by_chip = exchange_chips("rs_chips", pairs)
    result = {}
    kinds = ("grad", "delta", "new_m", "new_v")
    for n, parts in zip(BIG, by_chip):
        outs = adamw("adamw_" + n, parts, *[rows2d(d[n]) for d in (w_loc, m_loc, v_loc)])
        for kind, a in zip(kinds, outs):
            result[(kind, n)] = a.reshape(w_loc[n].shape)
    for kind in kinds:
        result[(kind, "ffn_w_gate")], result[(kind, "ffn_w_up")] = split_gu(result.pop((kind, GU)), f8)
    outs = adamw("adamw_small", by_chip[-1], *[_pack([d[n] for n in SMALL], F32, 8) for d in (w_loc, m_loc, v_loc)])
    for kind, buf in zip(kinds, outs):
        result.update({(kind, n): a for n, a in zip(SMALL, _unpack(buf, small_shapes))})
    rep_shapes = [w_loc[n].shape for n in REPLICATED]
    rep_parts, = all_gather("gather_rep_grads", [_pack([contrib[n] for n in REPLICATED], F32, 8)])
    outs = adamw("adamw_replicated", rep_parts, *[_pack([d[n] for n in REPLICATED], F32, 8)
                                                  for d in (w_loc, m_loc, v_loc)])
    for kind, buf in zip(kinds, outs):
        result.update({(kind, n): a for n, a in zip(REPLICATED, _unpack(buf, rep_shapes))})
    return (loss, dact[None], *[result[(kind, n)] for kind in kinds for n in WEIGHTS])
```

```python
import functools
import math

import jax
import jax.numpy as jnp
from jax import lax
from jax.experimental import pallas as pl
from jax.experimental.pallas import tpu as pltpu

F32 = jnp.float32
BF16 = jnp.bfloat16

N_DEV = 8
DEPTH = 4
CHUNK = 64
N_LEFT = 8
BAND = (N_LEFT + 1) * CHUNK
PAD_KEYS = N_LEFT * CHUNK
HEAD_DIM = 64
N_HEADS = 8
REL_CLIP = 256
LRU_BLOCKS = 4
LRU_C = 8.0
RMS_EPS = 1e-6
QK_SCALE = HEAD_DIM ** -0.5

ADAM_LR = 0.001
ADAM_B1 = 0.9
ADAM_B2 = 0.999
ADAM_EPS = 1e-08
ADAM_WD = 0.01
ADAM_STEP = 10

LANES = 1024
V7X_VMEM_LIMIT = 56 * 1024 * 1024

MESH = pl.DeviceIdType.MESH
ANY = pl.BlockSpec(memory_space=pl.ANY)

SHARDED = (
    ("attn_w_in", 2), ("attn_w_out", 1), ("rg_w_in", 2), ("rg_conv_w", 3), ("rg_conv_b", 1),
    ("rg_w_a", 2), ("rg_b_a", 2), ("rg_w_i", 2), ("rg_b_i", 2), ("rg_lambda", 1), ("rg_w_out", 1),
    ("ffn_w_gate", 2), ("ffn_w_up", 2), ("ffn_w_down", 1),
)
REPLICATED = ("attn_rel_bias", "norm_mix_pre", "norm_mix_post", "norm_ffn_pre", "norm_ffn_post")
WEIGHTS = ("attn_w_in", "attn_rel_bias", "attn_w_out", "rg_w_in", "rg_conv_w", "rg_conv_b", "rg_w_a", "rg_b_a",
           "rg_w_i", "rg_b_i", "rg_lambda", "rg_w_out", "norm_mix_pre", "norm_mix_post", "norm_ffn_pre",
           "norm_ffn_post", "ffn_w_gate", "ffn_w_up", "ffn_w_down")


def _params(*dims):
    return pltpu.CompilerParams(dimension_semantics=dims or None, vmem_limit_bytes=V7X_VMEM_LIMIT)


def _sds(shape, dtype):
    return jax.ShapeDtypeStruct(tuple(shape), dtype)


def _row_tile(n, pref):
    t = min(n, pref)
    assert n % t == 0, (n, pref)
    return t


def _divisor_tile(n, limit, multiple):
    if n <= limit:
        return n
    best = max(t for t in range(multiple, limit + 1, multiple) if n % t == 0)
    return best


NN = (((1,), (0,)), ((), ()))
NT = (((1,), (1,)), ((), ()))
TN = (((0,), (0,)), ((), ()))


def _gmm(name, a, b, *, grid, a_blk, a_idx, b_blk, b_idx, o_blk, o_idx, out_shape, out_dtype, dn, acc_shape):
    nk = grid[-1]
    kax = len(grid) - 1

    def body(a_ref, b_ref, o_ref, acc_ref):
        part = lax.dot_general(a_ref[...], b_ref[...], dn, preferred_element_type=F32)
        if nk == 1:
            o_ref[...] = part.astype(o_ref.dtype)
            return
        k = pl.program_id(kax)

        @pl.when(k == 0)
        def _():
            acc_ref[...] = part

        @pl.when(k > 0)
        def _():
            acc_ref[...] += part

        @pl.when(k == nk - 1)
        def _():
            o_ref[...] = acc_ref[...].astype(o_ref.dtype)

    return pl.pallas_call(
        body, grid=grid,
        in_specs=[pl.BlockSpec(a_blk, a_idx), pl.BlockSpec(b_blk, b_idx)],
        out_specs=pl.BlockSpec(o_blk, o_idx),
        out_shape=_sds(out_shape, out_dtype),
        scratch_shapes=[pltpu.VMEM(acc_shape, F32)],
        compiler_params=_params(*(["parallel"] * kax + ["arbitrary"])),
        name=name,
    )(a, b)


def mm_nn(name, a, b, out_dtype, tm=1024, tn=512, tk=1024):
    (m, k), (_, n) = a.shape, b.shape
    tm, tn, tk = _row_tile(m, tm), _row_tile(n, tn), _row_tile(k, tk)
    return _gmm(name, a, b, grid=(m // tm, n // tn, k // tk),
                a_blk=(tm, tk), a_idx=lambda i, j, kk: (i, kk), b_blk=(tk, tn), b_idx=lambda i, j, kk: (kk, j),
                o_blk=(tm, tn), o_idx=lambda i, j, kk: (i, j), out_shape=(m, n), out_dtype=out_dtype, dn=NN,
                acc_shape=(tm, tn))


def mm_nt(name, a, b, out_dtype, tm=1024, tn=512, tk=1024):
    (m, k), (n, _) = a.shape, b.shape
    tm, tn, tk = _row_tile(m, tm), _row_tile(n, tn), _row_tile(k, tk)
    return _gmm(name, a, b, grid=(m // tm, n // tn, k // tk),
                a_blk=(tm, tk), a_idx=lambda i, j, kk: (i, kk), b_blk=(tn, tk), b_idx=lambda i, j, kk: (j, kk),
                o_blk=(tm, tn), o_idx=lambda i, j, kk: (i, j), out_shape=(m, n), out_dtype=out_dtype, dn=NT,
                acc_shape=(tm, tn))


def mm_tn(name, a, b, out_dtype, tm=512, tn=512, tk=1024):
    (k, m), (_, n) = a.shape, b.shape
    tm, tn, tk = _row_tile(m, tm), _row_tile(n, tn), _row_tile(k, tk)
    return _gmm(name, a, b, grid=(m // tm, n // tn, k // tk),
                a_blk=(tk, tm), a_idx=lambda i, j, kk: (kk, i), b_blk=(tk, tn), b_idx=lambda i, j, kk: (kk, j),
                o_blk=(tm, tn), o_idx=lambda i, j, kk: (i, j), out_shape=(m, n), out_dtype=out_dtype, dn=TN,
                acc_shape=(tm, tn))


def mm_nn_wblk(name, a, wb, layer, out_dtype, tm=1024, tk=1024):
    (m, k), (nb, _, _, n8) = a.shape, wb.shape
    tm, tk = _row_tile(m, tm), _row_tile(k, tk)
    return _gmm(name, a, wb, grid=(m // tm, nb, k // tk),
                a_blk=(tm, tk), a_idx=lambda i, j, kk: (i, kk),
                b_blk=(None, None, tk, n8), b_idx=lambda i, j, kk: (j, layer, kk, 0),
                o_blk=(tm, n8), o_idx=lambda i, j, kk: (i, j), out_shape=(m, nb * n8), out_dtype=out_dtype, dn=NN,
                acc_shape=(tm, n8))


def mm_nt_wblk(name, a, wb, layer, out_dtype, tm=1024, tn=512):
    m = a.shape[0]
    nb, _, k, n8 = wb.shape
    tm, tn = _row_tile(m, tm), _row_tile(k, tn)

    def body(a_ref, b_ref, o_ref):
        acc = lax.dot_general(a_ref[:, 0:n8], b_ref[0], NT, preferred_element_type=F32)
        for j in range(1, nb):
            acc = acc + lax.dot_general(a_ref[:, j * n8:(j + 1) * n8], b_ref[j], NT, preferred_element_type=F32)
        o_ref[...] = acc.astype(o_ref.dtype)

    return pl.pallas_call(
        body, grid=(m // tm, k // tn),
        in_specs=[pl.BlockSpec((tm, nb * n8), lambda i, j: (i, 0)),
                  pl.BlockSpec((nb, None, tn, n8), lambda i, j: (0, layer, j, 0))],
        out_specs=pl.BlockSpec((tm, tn), lambda i, j: (i, j)), out_shape=_sds((m, k), out_dtype),
        compiler_params=_params("parallel", "parallel"), name=name)(a, wb)


def mm_tn_oblk(name, a, b, n8, out_dtype, tk=2048):
    (t, k), nb = a.shape, b.shape[1] // n8
    tk = _row_tile(t, tk)
    return _gmm(name, a, b, grid=(nb, t // tk),
                a_blk=(tk, k), a_idx=lambda j, s: (s, 0), b_blk=(tk, n8), b_idx=lambda j, s: (s, j),
                o_blk=(None, k, n8), o_idx=lambda j, s: (j, 0, 0), out_shape=(nb, k, n8), out_dtype=out_dtype, dn=TN,
                acc_shape=(k, n8))


def rmsnorm_fwd(name, x, g):
    t, d = x.shape
    tr = _row_tile(t, 512)

    def body(x_ref, g_ref, o_ref):
        xv = x_ref[...]
        r = lax.rsqrt(jnp.mean(xv * xv, axis=-1, keepdims=True) + RMS_EPS)
        o_ref[...] = (xv * r * g_ref[...]).astype(o_ref.dtype)

    return pl.pallas_call(
        body, grid=(t // tr,),
        in_specs=[pl.BlockSpec((tr, d), lambda i: (i, 0)), pl.BlockSpec((1, d), lambda i: (0, 0))],
        out_specs=pl.BlockSpec((tr, d), lambda i: (i, 0)),
        out_shape=_sds((t, d), BF16), compiler_params=_params("parallel"), name=name)(x, g)


def resid_norm_fwd(name, x, m, g, g_next):
    t, d = x.shape
    tr = _row_tile(t, 512)
    chained = g_next is not None

    def body(*refs):
        x_ref, m_ref, g_ref = refs[:3]
        mv = m_ref[...]
        r = lax.rsqrt(jnp.mean(mv * mv, axis=-1, keepdims=True) + RMS_EPS)
        x1 = x_ref[...] + mv * r * g_ref[...]
        if chained:
            gn_ref, o_ref, h_ref = refs[3:]
            r1 = lax.rsqrt(jnp.mean(x1 * x1, axis=-1, keepdims=True) + RMS_EPS)
            h_ref[...] = (x1 * r1 * gn_ref[...]).astype(BF16)
        else:
            o_ref, = refs[3:]
        o_ref[...] = x1

    row = pl.BlockSpec((tr, d), lambda i: (i, 0))
    vec = pl.BlockSpec((1, d), lambda i: (0, 0))
    out = pl.pallas_call(
        body, grid=(t // tr,),
        in_specs=[row, row, vec] + ([vec] if chained else []),
        out_specs=[row, row] if chained else [row],
        out_shape=[_sds((t, d), F32)] + ([_sds((t, d), BF16)] if chained else []),
        compiler_params=_params("parallel"), name=name)(*([x, m, g] + ([g_next] if chained else [])))
    return (out[0], out[1]) if chained else (out[0], None)


def norm_bwd(name, dy, x, g, resid, out_dtype):
    t, d = x.shape
    tr = _row_tile(t, 512)
    has_res = resid is not None

    def body(*refs):
        if has_res:
            dy_ref, x_ref, g_ref, r_ref, dx_ref, dg_ref = refs
        else:
            dy_ref, x_ref, g_ref, dx_ref, dg_ref = refs
        i = pl.program_id(0)
        xv = x_ref[...]
        dyv = dy_ref[...].astype(F32)
        r = lax.rsqrt(jnp.mean(xv * xv, axis=-1, keepdims=True) + RMS_EPS)
        xh = xv * r
        dxh = dyv * g_ref[...]
        dx = r * (dxh - xh * jnp.mean(dxh * xh, axis=-1, keepdims=True))
        if has_res:
            dx = dx + r_ref[...]
        dx_ref[...] = dx.astype(dx_ref.dtype)
        part = jnp.sum(dyv * xh, axis=0, keepdims=True)

        @pl.when(i == 0)
        def _():
            dg_ref[...] = part

        @pl.when(i > 0)
        def _():
            dg_ref[...] += part

    row = pl.BlockSpec((tr, d), lambda i: (i, 0))
    vec = pl.BlockSpec((1, d), lambda i: (0, 0))
    ins = [dy, x, g] + ([resid] if has_res else [])
    return pl.pallas_call(
        body, grid=(t // tr,),
        in_specs=[row, row, vec] + ([row] if has_res else []),
        out_specs=[row, vec],
        out_shape=[_sds((t, d), out_dtype), _sds((1, d), F32)],
        compiler_params=_params("arbitrary"), name=name)(*ins)


def norm_bwd_chain(name, dh, x, g, resid, m_prev, g_prev):
    t, d = x.shape
    tr = _row_tile(t, 512)

    def body(dh_ref, x_ref, g_ref, r_ref, m_ref, gp_ref, dx_ref, dg_ref, dm_ref, dgp_ref):
        i = pl.program_id(0)
        xv = x_ref[...]
        dhv = dh_ref[...]
        r = lax.rsqrt(jnp.mean(xv * xv, axis=-1, keepdims=True) + RMS_EPS)
        xh = xv * r
        dxh = dhv * g_ref[...]
        dx = r * (dxh - xh * jnp.mean(dxh * xh, axis=-1, keepdims=True)) + r_ref[...]
        dx_ref[...] = dx
        mv = m_ref[...]
        rm = lax.rsqrt(jnp.mean(mv * mv, axis=-1, keepdims=True) + RMS_EPS)
        mh = mv * rm
        dmh = dx * gp_ref[...]
        dm_ref[...] = (rm * (dmh - mh * jnp.mean(dmh * mh, axis=-1, keepdims=True))).astype(BF16)
        part = jnp.sum(dhv * xh, axis=0, keepdims=True)
        part_prev = jnp.sum(dx * mh, axis=0, keepdims=True)

        @pl.when(i == 0)
        def _():
            dg_ref[...] = part
            dgp_ref[...] = part_prev

        @pl.when(i > 0)
        def _():
            dg_ref[...] += part
            dgp_ref[...] += part_prev

    row = pl.BlockSpec((tr, d), lambda i: (i, 0))
    vec = pl.BlockSpec((1, d), lambda i: (0, 0))
    return pl.pallas_call(
        body, grid=(t // tr,), in_specs=[row, row, vec, row, row, vec], out_specs=[row, vec, row, vec],
        out_shape=[_sds((t, d), F32), _sds((1, d), F32), _sds((t, d), BF16), _sds((1, d), F32)],
        compiler_params=_params("arbitrary"), name=name)(dh, x, g, resid, m_prev, g_prev)


def close_bwd(tag, dh, x, g_pre, dx1, prev):
    if prev is None:
        dx, dg_pre = norm_bwd(tag + "_dpre", dh, x, g_pre, dx1, F32)
        return dx, dg_pre, None, None
    return norm_bwd_chain(tag + "_dpre", dh, x, g_pre, dx1, *prev)


def loss_grad(name, y, target):
    t, d = y.shape
    tr = _row_tile(t, 512)

    def body(y_ref, t_ref, dy_ref, s_ref):
        i = pl.program_id(0)
        err = y_ref[...] - t_ref[...]
        dy_ref[...] = err * (1.0 / d)
        part = jnp.sum(err * err, axis=0, keepdims=True)

        @pl.when(i == 0)
        def _():
            s_ref[...] = part

        @pl.when(i > 0)
        def _():
            s_ref[...] += part

    row = pl.BlockSpec((tr, d), lambda i: (i, 0))
    vec = pl.BlockSpec((1, d), lambda i: (0, 0))
    return pl.pallas_call(
        body, grid=(t // tr,), in_specs=[row, row], out_specs=[row, vec],
        out_shape=[_sds((t, d), F32), _sds((1, d), F32)],
        compiler_params=_params("arbitrary"), name=name)(y, target)


PAIR = 2 * HEAD_DIM
N_PAIRS = N_HEADS // 2
A_TQ = 512
A_UNROLL = 4


def _halves(x):
    lane = lax.broadcasted_iota(jnp.int32, x.shape, x.ndim - 1)
    zero = jnp.zeros_like(x)
    return jnp.where(lane < HEAD_DIM, x, zero), jnp.where(lane >= HEAD_DIM, x, zero)


def _merge(a, b):
    lane = lax.broadcasted_iota(jnp.int32, a.shape, a.ndim - 1)
    return jnp.where(lane < HEAD_DIM, a, b)


def _a_valid(c):
    col = lax.broadcasted_iota(jnp.int32, (CHUNK, BAND), 1)
    return col >= (N_LEFT - c) * CHUNK


def attn_a_fwd(name, proj, kp, vp, bias, q_blk):
    t = proj.shape[0]
    tq = _row_tile(t, A_TQ)
    ncs = tq // CHUNK
    un = math.gcd(A_UNROLL, ncs)

    def body(q_ref, k_ref, v_ref, b_ref, o_ref, l_ref):
        i = pl.program_id(1)

        def group(gg, carry):
            cs = [i * ncs + gg * un + u for u in range(un)]
            r0s = [pl.multiple_of((gg * un + u) * CHUNK, CHUNK) for u in range(un)]
            k0s = [pl.multiple_of(c * CHUNK, CHUNK) for c in cs]
            ss = []
            for u in range(un):
                qh = _halves(q_ref[pl.ds(r0s[u], CHUNK), :] * QK_SCALE)
                kwin = k_ref[pl.ds(k0s[u], BAND), :]
                valid = _a_valid(cs[u])
                for hh in range(2):
                    s = lax.dot_general(qh[hh], kwin, NT, preferred_element_type=F32) + b_ref[hh]
                    ss.append(jnp.where(valid, s, -1e30))
            ps, lses = [], []
            for s in ss:
                mx = jnp.max(s, axis=-1, keepdims=True)
                p = jnp.exp(s - mx)
                den = jnp.sum(p, axis=-1, keepdims=True)
                ps.append((p * (1.0 / den)).astype(BF16))
                lses.append(mx + jnp.log(den))
            for u in range(un):
                vwin = v_ref[pl.ds(k0s[u], BAND), :]
                o0 = jnp.dot(ps[2 * u], vwin, preferred_element_type=F32)
                o1 = jnp.dot(ps[2 * u + 1], vwin, preferred_element_type=F32)
                o_ref[pl.ds(r0s[u], CHUNK), :] = _merge(o0, o1)
                l_ref[pl.ds(r0s[u], CHUNK), :] = jnp.concatenate([lses[2 * u], lses[2 * u + 1]], axis=1)
            return carry

        lax.fori_loop(0, ncs // un, group, 0)

    return pl.pallas_call(
        body, grid=(N_PAIRS, t // tq),
        in_specs=[pl.BlockSpec((tq, PAIR), lambda p, i: (i, q_blk + p)),
                  pl.BlockSpec((t + PAD_KEYS, PAIR), lambda p, i: (0, p)),
                  pl.BlockSpec((t + PAD_KEYS, PAIR), lambda p, i: (0, p)),
                  pl.BlockSpec((2, CHUNK, BAND), lambda p, i: (p, 0, 0))],
        out_specs=[pl.BlockSpec((tq, PAIR), lambda p, i: (i, p)),
                   pl.BlockSpec((None, tq, 2), lambda p, i: (p, i, 0))],
        out_shape=[_sds((t, N_PAIRS * PAIR), F32), _sds((N_PAIRS, t, 2), F32)],
        compiler_params=_params("parallel", "parallel"), name=name)(proj, kp, vp, bias)


def attn_a_bwd(name, proj, kp, vp, bias, o, lse, do, q_blk, do_blk):
    t = proj.shape[0]
    tq = _row_tile(t, A_TQ)
    ncs = tq // CHUNK
    un = math.gcd(A_UNROLL, ncs)

    def body(q_ref, k_ref, v_ref, b_ref, o_ref, l_ref, do_ref, dq_ref, dk_ref, dv_ref, db_ref):
        i = pl.program_id(1)

        @pl.when(i == 0)
        def _():
            dk_ref[...] = jnp.zeros_like(dk_ref)
            dv_ref[...] = jnp.zeros_like(dv_ref)
            db_ref[...] = jnp.zeros_like(db_ref)

        def group(gg, carry):
            cs = [i * ncs + gg * un + u for u in range(un)]
            r0s = [pl.multiple_of((gg * un + u) * CHUNK, CHUNK) for u in range(un)]
            k0s = [pl.multiple_of(c * CHUNK, CHUNK) for c in cs]
            qhs, dohs, ps, dps, deltas = [], [], [], [], []
            for u in range(un):
                rows = pl.ds(r0s[u], CHUNK)
                qh = _halves(q_ref[rows, :] * QK_SCALE)
                doh = _halves(do_ref[rows, :])
                kwin = k_ref[pl.ds(k0s[u], BAND), :]
                vwin = v_ref[pl.ds(k0s[u], BAND), :]
                valid = _a_valid(cs[u])
                dl = _halves(do_ref[rows, :].astype(F32) * o_ref[rows, :])
                for hh in range(2):
                    s = lax.dot_general(qh[hh], kwin, NT, preferred_element_type=F32) + b_ref[hh]
                    ps.append(jnp.where(valid, jnp.exp(s - l_ref[rows, hh:hh + 1]), 0.0))
                    dps.append(lax.dot_general(doh[hh], vwin, NT, preferred_element_type=F32))
                    deltas.append(jnp.sum(dl[hh], axis=-1, keepdims=True))
                qhs.append(qh)
                dohs.append(doh)
            dss = [p * (dp - dl) for p, dp, dl in zip(ps, dps, deltas)]
            for hh in range(2):
                tot = dss[hh]
                for u in range(1, un):
                    tot = tot + dss[2 * u + hh]
                db_ref[hh] += tot
            for u in range(un):
                kwin = k_ref[pl.ds(k0s[u], BAND), :]
                ds0, ds1 = dss[2 * u].astype(BF16), dss[2 * u + 1].astype(BF16)
                dq_ref[pl.ds(r0s[u], CHUNK), :] = _merge(jnp.dot(ds0, kwin, preferred_element_type=F32),
                                                         jnp.dot(ds1, kwin, preferred_element_type=F32)) * QK_SCALE
                dk_ref[pl.ds(k0s[u], BAND), :] += (lax.dot_general(ds0, qhs[u][0], TN, preferred_element_type=F32)
                                                   + lax.dot_general(ds1, qhs[u][1], TN, preferred_element_type=F32))
                dv_ref[pl.ds(k0s[u], BAND), :] += (
                    lax.dot_general(ps[2 * u].astype(BF16), dohs[u][0], TN, preferred_element_type=F32)
                    + lax.dot_general(ps[2 * u + 1].astype(BF16), dohs[u][1], TN, preferred_element_type=F32))
            return carry

        lax.fori_loop(0, ncs // un, group, 0)

    tile = lambda blk: pl.BlockSpec((tq, PAIR), lambda p, i: (i, blk + p))
    whole = pl.BlockSpec((t + PAD_KEYS, PAIR), lambda p, i: (0, p))
    bspec = pl.BlockSpec((2, CHUNK, BAND), lambda p, i: (p, 0, 0))
    return pl.pallas_call(
        body, grid=(N_PAIRS, t // tq),
        in_specs=[tile(q_blk), whole, whole, bspec, tile(0), pl.BlockSpec((None, tq, 2), lambda p, i: (p, i, 0)),
                  tile(do_blk)],
        out_specs=[tile(0), whole, whole, bspec],
        out_shape=[_sds((t, N_PAIRS * PAIR), F32), _sds((t + PAD_KEYS, N_PAIRS * PAIR), F32),
                   _sds((t + PAD_KEYS, N_PAIRS * PAIR), F32), _sds((2 * N_PAIRS, CHUNK, BAND), F32)],
        compiler_params=_params("parallel", "arbitrary"), name=name)(proj, kp, vp, bias, o, lse, do)


SB_TQ = 256
SB_TK = 256
SB_DEAD = -125.0


def _tri(n, strict):
    j = lax.broadcasted_iota(jnp.int32, (n, n), 0)
    s = lax.broadcasted_iota(jnp.int32, (n, n), 1)
    return jnp.where((j > s) if strict else (j >= s), 1.0, 0.0).astype(BF16)


def _suffix_sum(x, tri, exact):
    hi = x.astype(BF16)
    out = jnp.dot(hi, tri, preferred_element_type=F32)
    if exact:
        lo = (x - hi.astype(F32)).astype(BF16)
        out = out + jnp.dot(lo, tri, preferred_element_type=F32)
    return out


def _sb_scores(qh, ks, causal):
    z = lax.dot_general(qh, ks, NT, preferred_element_type=F32)
    lb = jnp.minimum(z, 0.0) - jnp.log(1.0 + jnp.exp(-jnp.abs(z)))
    m = lb - z
    if causal is not None:
        m = jnp.where(causal, m, 0.0)
    return lb, m


def _causal(tq, tk, off):
    return (lax.broadcasted_iota(jnp.int32, (tq, tk), 1) + off * tk) < lax.broadcasted_iota(jnp.int32, (tq, tk), 0)


def sb_fwd(name, proj, q_blk, k_blk, v_blk):
    t = proj.shape[0]
    tq = _row_tile(t, SB_TQ)
    tk = min(SB_TK, tq)
    per = tq // tk

    def body(q_ref, k_ref, v_ref, o_ref):
        i = pl.program_id(1)
        tri = _tri(tk, True)
        qh = _halves(q_ref[...] * QK_SCALE)

        def blocks(kb, carry, off):
            k0 = pl.multiple_of(kb * tk, tk)
            ks, vs = k_ref[pl.ds(k0, tk), :], v_ref[pl.ds(k0, tk), :]
            causal = None if off is None else _causal(tq, tk, off)
            lbm = [_sb_scores(qh[hh], ks, causal) for hh in range(2)]
            afters = [_suffix_sum(lbm[hh][1], tri, False) for hh in range(2)]
            out = []
            for hh in range(2):
                acc, cm = carry[2 * hh], carry[2 * hh + 1]
                w = jnp.exp(lbm[hh][0] + afters[hh] + cm)
                if causal is not None:
                    w = jnp.where(causal, w, 0.0)
                out += [acc + jnp.dot(w.astype(BF16), vs, preferred_element_type=F32),
                        cm + jnp.sum(lbm[hh][1], axis=-1, keepdims=True)]
            return tuple(out)

        def alive(carry):
            return jnp.maximum(jnp.max(carry[1]), jnp.max(carry[3])) > SB_DEAD

        carry = (jnp.zeros((tq, PAIR), F32), jnp.zeros((tq, 1), F32)) * 2
        for off in reversed(range(per)):
            carry = blocks(i * per + off, carry, off)

        def step(c):
            new = blocks(i * per - 1 - c[0], c[2:], None)
            return (c[0] + 1, alive(new)) + new

        out = lax.while_loop(lambda c: jnp.logical_and(c[0] < i * per, c[1]), step,
                             (jnp.int32(0), alive(carry)) + carry)
        o_ref[...] = _merge(out[2], out[4])

    return pl.pallas_call(
        body, grid=(N_PAIRS, t // tq),
        in_specs=[pl.BlockSpec((tq, PAIR), lambda p, i: (i, q_blk + p)),
                  pl.BlockSpec((t, PAIR), lambda p, i: (0, k_blk + p)),
                  pl.BlockSpec((t, PAIR), lambda p, i: (0, v_blk + p))],
        out_specs=pl.BlockSpec((tq, PAIR), lambda p, i: (i, p)),
        out_shape=_sds((t, N_PAIRS * PAIR), F32), compiler_params=_params("parallel", "parallel"), name=name,
    )(proj, proj, proj)


def sb_bwd(name, proj, o, do, q_blk, k_blk, v_blk, do_blk):
    t = proj.shape[0]
    tq = _row_tile(t, SB_TQ)
    tk = min(SB_TK, tq)
    per = tq // tk

    def body(q_ref, k_ref, v_ref, o_ref, do_ref, dq_ref, dk_ref, dv_ref):
        i = pl.program_id(1)

        @pl.when(i == 0)
        def _():
            dk_ref[...] = jnp.zeros_like(dk_ref)
            dv_ref[...] = jnp.zeros_like(dv_ref)

        tri_s, tri_i = _tri(tk, True), _tri(tk, False)
        qh = _halves(q_ref[...] * QK_SCALE)
        doh = _halves(do_ref[...])
        deltas = [jnp.sum(x, axis=-1, keepdims=True) for x in _halves(do_ref[...].astype(F32) * o_ref[...])]

        def blocks(kb, carry, off):
            k0 = pl.multiple_of(kb * tk, tk)
            ks, vs = k_ref[pl.ds(k0, tk), :], v_ref[pl.ds(k0, tk), :]
            causal = None if off is None else _causal(tq, tk, off)
            lbm = [_sb_scores(qh[hh], ks, causal) for hh in range(2)]
            dws = [lax.dot_general(doh[hh], vs, NT, preferred_element_type=F32) for hh in range(2)]
            afters = [_suffix_sum(lbm[hh][1], tri_s, False) for hh in range(2)]
            wbs, es = [], []
            for hh in range(2):
                w = jnp.exp(lbm[hh][0] + afters[hh] + carry[3 * hh + 1])
                if causal is not None:
                    w = jnp.where(causal, w, 0.0)
                wbs.append(w.astype(BF16))
                es.append(wbs[hh].astype(F32) * dws[hh])
            sfx = [_suffix_sum(es[hh], tri_i, True) for hh in range(2)]
            dzs = []
            for hh in range(2):
                left = deltas[hh] - (sfx[hh] + carry[3 * hh + 2])
                sig = jnp.exp(lbm[hh][0])
                dz = es[hh] * (1.0 - sig) - left * sig
                if causal is not None:
                    dz = jnp.where(causal, dz, 0.0)
                dzs.append(dz.astype(BF16))
            dk_ref[pl.ds(k0, tk), :] += (lax.dot_general(dzs[0], qh[0], TN, preferred_element_type=F32)
                                         + lax.dot_general(dzs[1], qh[1], TN, preferred_element_type=F32))
            dv_ref[pl.ds(k0, tk), :] += (lax.dot_general(wbs[0], doh[0], TN, preferred_element_type=F32)
                                         + lax.dot_general(wbs[1], doh[1], TN, preferred_element_type=F32))
            out = []
            for hh in range(2):
                out += [carry[3 * hh] + jnp.dot(dzs[hh], ks, preferred_element_type=F32),
                        carry[3 * hh + 1] + jnp.sum(lbm[hh][1], axis=-1, keepdims=True),
                        carry[3 * hh + 2] + jnp.sum(es[hh], axis=-1, keepdims=True)]
            return tuple(out)

        def alive(carry):
            return jnp.maximum(jnp.max(carry[1]), jnp.max(carry[4])) > SB_DEAD

        zero = jnp.zeros((tq, 1), F32)
        carry = (jnp.zeros((tq, PAIR), F32), zero, zero) * 2
        for off in reversed(range(per)):
            carry = blocks(i * per + off, carry, off)

        def step(c):
            new = blocks(i * per - 1 - c[0], c[2:], None)
            return (c[0] + 1, alive(new)) + new

        out = lax.while_loop(lambda c: jnp.logical_and(c[0] < i * per, c[1]), step,
                             (jnp.int32(0), alive(carry)) + carry)
        dq_ref[...] = _merge(out[2], out[5]) * QK_SCALE

    tile = lambda blk: pl.BlockSpec((tq, PAIR), lambda p, i: (i, blk + p))
    whole = lambda blk: pl.BlockSpec((t, PAIR), lambda p, i: (0, blk + p))
    return pl.pallas_call(
        body, grid=(N_PAIRS, t // tq),
        in_specs=[tile(q_blk), whole(k_blk), whole(v_blk), tile(0), tile(do_blk)],
        out_specs=[tile(0), whole(0), whole(0)],
        out_shape=[_sds((t, N_PAIRS * PAIR), F32)] * 3,
        compiler_params=_params("parallel", "arbitrary"), name=name)(proj, proj, proj, o, do)


def _sigmoid(x):
    return 1.0 / (1.0 + jnp.exp(-x))


def gu_gap(f8):
    return -(-f8 // 128) * 128


def merge_gu(gate, up):
    f8 = gate.shape[-1]
    pad = jnp.zeros(gate.shape[:-1] + (gu_gap(f8) - f8,), gate.dtype)
    return jnp.concatenate([gate, pad, up], axis=-1)


def split_gu(gu, f8):
    return gu[..., :f8], gu[..., gu_gap(f8):]


def ffn_up(name, h, wgu, f8, layer):
    t, d = h.shape
    nb, _, _, fw = wgu.shape
    gap = gu_gap(f8)
    tm = _row_tile(t, 1024)

    per = 2

    def body(h_ref, w_ref, gu_ref, a_ref):
        hv = h_ref[...]
        rs = [jnp.dot(hv, w_ref[j], preferred_element_type=F32) for j in range(per)]
        for j in range(per):
            gu_ref[j] = rs[j].astype(BF16)
            g, u = rs[j][:, :f8], rs[j][:, gap:]
            a_ref[j] = (g * _sigmoid(g) * u).astype(BF16)

    return pl.pallas_call(
        body, grid=(t // tm, nb // per),
        in_specs=[pl.BlockSpec((tm, d), lambda i, k: (i, 0)),
                  pl.BlockSpec((per, None, d, fw), lambda i, k: (k, layer, 0, 0))],
        out_specs=[pl.BlockSpec((per, tm, fw), lambda i, k: (k, i, 0)),
                   pl.BlockSpec((per, tm, f8), lambda i, k: (k, i, 0))],
        out_shape=[_sds((nb, t, fw), BF16), _sds((nb, t, f8), BF16)],
        compiler_params=_params("parallel", "parallel"), name=name)(h, wgu)


def _mm_all_blocks(name, a, w, layer, dn, tm):
    nb, t, f = a.shape
    wshape = w.shape[2:]
    d = wshape[1] if dn == NN else wshape[0]
    tm = _row_tile(t, tm)

    def body(a_ref, w_ref, o_ref):
        acc = lax.dot_general(a_ref[0], w_ref[0], dn, preferred_element_type=F32)
        for k in range(1, nb):
            acc = acc + lax.dot_general(a_ref[k], w_ref[k], dn, preferred_element_type=F32)
        o_ref[...] = acc

    return pl.pallas_call(
        body, grid=(t // tm,),
        in_specs=[pl.BlockSpec((nb, tm, f), lambda i: (0, i, 0)),
                  pl.BlockSpec((nb, None) + wshape, lambda i: (0, layer, 0, 0))],
        out_specs=pl.BlockSpec((tm, d), lambda i: (i, 0)), out_shape=_sds((t, d), F32),
        compiler_params=_params("parallel"), name=name)(a, w)


def ffn_down(name, a, wd, layer):
    return _mm_all_blocks(name, a, wd, layer, NN, 512)


def ffn_bwd_act(name, dm, wd, gu, f8, layer):
    t, d = dm.shape
    nb, _, fw = gu.shape
    gap = gu_gap(f8)
    tm = _row_tile(t, 1024)

    per = 2

    def body(dm_ref, wd_ref, gu_ref, o_ref):
        dmv = dm_ref[...]
        das = [lax.dot_general(dmv, wd_ref[j], NT, preferred_element_type=F32) for j in range(per)]
        o_ref[...] = jnp.zeros_like(o_ref)
        for j in range(per):
            gv = gu_ref[j, :, :f8].astype(F32)
            uv = gu_ref[j, :, gap:].astype(F32)
            sg = _sigmoid(gv)
            o_ref[j, :, :f8] = (das[j] * uv * sg * (1.0 + gv * (1.0 - sg))).astype(BF16)
            o_ref[j, :, gap:] = (das[j] * gv * sg).astype(BF16)

    bspec = pl.BlockSpec((per, tm, fw), lambda i, k: (k, i, 0))
    return pl.pallas_call(
        body, grid=(t // tm, nb // per),
        in_specs=[pl.BlockSpec((tm, d), lambda i, k: (i, 0)),
                  pl.BlockSpec((per, None, f8, d), lambda i, k: (k, layer, 0, 0)), bspec],
        out_specs=bspec, out_shape=_sds((nb, t, fw), BF16),
        compiler_params=_params("parallel", "parallel"), name=name)(dm, wd, gu)


def ffn_bwd_dh(name, dgu, wgu, layer):
    return _mm_all_blocks(name, dgu, wgu, layer, NT, 512)


def ffn_dw_in(name, h, dact):
    t, d = h.shape
    nb, _, f8 = dact.shape
    tk = _row_tile(t, 2048)
    return _gmm(name, h, dact, grid=(nb, t // tk),
                a_blk=(tk, d), a_idx=lambda b, s: (s, 0), b_blk=(None, tk, f8), b_idx=lambda b, s: (b, s, 0),
                o_blk=(None, d, f8), o_idx=lambda b, s: (b, 0, 0), out_shape=(nb, d, f8), out_dtype=F32, dn=TN,
                acc_shape=(d, f8))


def ffn_dw_down(name, a, dm):
    nb, t, f8 = a.shape
    d = dm.shape[1]
    tk = _row_tile(t, 2048)
    return _gmm(name, a, dm, grid=(nb, t // tk),
                a_blk=(None, tk, f8), a_idx=lambda b, s: (b, s, 0), b_blk=(tk, d), b_idx=lambda b, s: (s, 0),
                o_blk=(None, f8, d), o_idx=lambda b, s: (b, 0, 0), out_shape=(nb, f8, d), out_dtype=F32, dn=TN,
                acc_shape=(f8, d))


GELU_C = math.sqrt(2.0 / math.pi)
GELU_A = 0.044715


def _gelu(x):
    return 0.5 * x * (1.0 + jnp.tanh(GELU_C * (x + GELU_A * x * x * x)))


def _gelu_grad(x):
    th = jnp.tanh(GELU_C * (x + GELU_A * x * x * x))
    return 0.5 * (1.0 + th) + 0.5 * x * (1.0 - th * th) * GELU_C * (1.0 + 3.0 * GELU_A * x * x)


def _neg_expm1(x):
    series = x * (1.0 + x * (0.5 + x * (1.0 / 6.0 + x * (1.0 / 24.0 + x * (1.0 / 120.0 + x * (1.0 / 720.0))))))
    return -jnp.where(x > -0.25, series, jnp.exp(x) - 1.0)


CONV_TR = 256
CONV_TAPS = 4
HALO = 8


def _shifted(ext, k, tr, back):
    if back:
        return pltpu.roll(ext, k, 0)[HALO:, :] if k else ext[HALO:, :]
    return pltpu.roll(ext, tr + HALO - k, 0)[:tr, :] if k else ext[:tr, :]


def conv4_fwd(name, src, cb, w, b):
    t, c = src.shape[0], w.shape[1]
    tr = _row_tile(t, CONV_TR)
    hb = tr // HALO

    def body(x_ref, h_ref, w_ref, b_ref, o_ref):
        i = pl.program_id(0)
        ext = jnp.concatenate([jnp.where(i == 0, 0.0, h_ref[...]), x_ref[...]], axis=0)
        acc = b_ref[...]
        for k in range(CONV_TAPS):
            acc = acc + w_ref[CONV_TAPS - 1 - k:CONV_TAPS - k, :] * _shifted(ext, k, tr, True)
        o_ref[...] = acc

    return pl.pallas_call(
        body, grid=(t // tr,),
        in_specs=[pl.BlockSpec((tr, c), lambda i: (i, cb)),
                  pl.BlockSpec((HALO, c), lambda i: (jnp.maximum(i * hb - 1, 0), cb)),
                  pl.BlockSpec((CONV_TAPS, c), lambda i: (0, 0)), pl.BlockSpec((1, c), lambda i: (0, 0))],
        out_specs=pl.BlockSpec((tr, c), lambda i: (i, 0)), out_shape=_sds((t, c), F32),
        compiler_params=_params("parallel"), name=name)(src, src, w, b)


def conv4_bwd_x(name, dy, w):
    t, c = dy.shape
    tr = _row_tile(t, CONV_TR)
    hb = tr // HALO
    last = t // tr - 1

    def body(y_ref, h_ref, w_ref, o_ref):
        i = pl.program_id(0)
        ext = jnp.concatenate([y_ref[...], jnp.where(i == last, 0.0, h_ref[...])], axis=0)
        acc = w_ref[CONV_TAPS - 1:CONV_TAPS, :] * y_ref[...]
        for k in range(1, CONV_TAPS):
            acc = acc + w_ref[CONV_TAPS - 1 - k:CONV_TAPS - k, :] * _shifted(ext, k, tr, False)
        o_ref[...] = acc

    return pl.pallas_call(
        body, grid=(t // tr,),
        in_specs=[pl.BlockSpec((tr, c), lambda i: (i, 0)),
                  pl.BlockSpec((HALO, c), lambda i: (jnp.minimum((i + 1) * hb, t // HALO - 1), 0)),
                  pl.BlockSpec((CONV_TAPS, c), lambda i: (0, 0))],
        out_specs=pl.BlockSpec((tr, c), lambda i: (i, 0)), out_shape=_sds((t, c), F32),
        compiler_params=_params("parallel"), name=name)(dy, dy, w)


def conv4_bwd_w(name, src, cb, dy):
    t, c = dy.shape
    tr = _row_tile(t, CONV_TR)
    hb = tr // HALO

    def body(x_ref, h_ref, dy_ref, dw_ref, db_ref):
        i = pl.program_id(0)

        @pl.when(i == 0)
        def _():
            dw_ref[...] = jnp.zeros_like(dw_ref)
            db_ref[...] = jnp.zeros_like(db_ref)

        ext = jnp.concatenate([jnp.where(i == 0, 0.0, h_ref[...]), x_ref[...]], axis=0)
        dyv = dy_ref[...]
        db_ref[...] += jnp.sum(dyv, axis=0, keepdims=True)
        for k in range(CONV_TAPS):
            dw_ref[CONV_TAPS - 1 - k:CONV_TAPS - k, :] += jnp.sum(dyv * _shifted(ext, k, tr, True), axis=0,
                                                                   keepdims=True)

    return pl.pallas_call(
        body, grid=(t // tr,),
        in_specs=[pl.BlockSpec((tr, c), lambda i: (i, cb)),
                  pl.BlockSpec((HALO, c), lambda i: (jnp.maximum(i * hb - 1, 0), cb)),
                  pl.BlockSpec((tr, c), lambda i: (i, 0))],
        out_specs=[pl.BlockSpec((CONV_TAPS, c), lambda i: (0, 0)), pl.BlockSpec((1, c), lambda i: (0, 0))],
        out_shape=[_sds((CONV_TAPS, c), F32), _sds((1, c), F32)],
        compiler_params=_params("arbitrary"), name=name)(src, src, dy)


def _rg_gate_values(xcv, wa_ref, wi_ref, ba_ref, bi_ref, lam_ref):
    xb = xcv.astype(BF16)
    r = _sigmoid(jnp.dot(xb, wa_ref[...], preferred_element_type=F32) + ba_ref[...])
    ig = _sigmoid(jnp.dot(xb, wi_ref[...], preferred_element_type=F32) + bi_ref[...])
    lam = lam_ref[...]
    sp = jnp.maximum(-lam, 0.0) + jnp.log(1.0 + jnp.exp(-jnp.abs(lam)))
    log_a = -LRU_C * r * sp
    a = jnp.exp(log_a)
    mult = jnp.sqrt(_neg_expm1(2.0 * log_a))
    return xb, r, ig, sp, a, mult


def rg_gates_fwd(name, xc, wa, wi, ba, bi, lam):
    t, c = xc.shape
    nb, cb, _ = wa.shape
    tm = _row_tile(t, 512)

    def body(xc_ref, wa_ref, wi_ref, ba_ref, bi_ref, lam_ref, a_ref, u_ref):
        xcv = xc_ref[...]
        _, _, ig, _, a, mult = _rg_gate_values(xcv, wa_ref, wi_ref, ba_ref, bi_ref, lam_ref)
        a_ref[...] = a
        u_ref[...] = mult * (ig * xcv)

    blk = pl.BlockSpec((tm, cb), lambda i, n: (i, n))
    wsp = pl.BlockSpec((None, cb, cb), lambda i, n: (n, 0, 0))
    vec = pl.BlockSpec((1, cb), lambda i, n: (0, n))
    return pl.pallas_call(
        body, grid=(t // tm, nb), in_specs=[blk, wsp, wsp, vec, vec, vec], out_specs=[blk, blk],
        out_shape=[_sds((t, c), F32)] * 2, compiler_params=_params("parallel", "parallel"), name=name,
    )(xc, wa, wi, ba, bi, lam)


def rg_gates_bwd(name, xc, gu, hs, wa, wi, ba, bi, lam):
    t, c = xc.shape
    nb, cb, _ = wa.shape
    tm = _row_tile(t, 512)
    hb = tm // HALO

    def body(xc_ref, gu_ref, h_ref, halo_ref, wa_ref, wi_ref, ba_ref, bi_ref, lam_ref,
             dxc_ref, dwa_ref, dwi_ref, dba_ref, dbi_ref, dlam_ref):
        i = pl.program_id(1)
        hprev = _shifted(jnp.concatenate([jnp.where(i == 0, 0.0, halo_ref[...]), h_ref[...]], axis=0), 1, tm, True)

        @pl.when(i == 0)
        def _():
            for ref in (dwa_ref, dwi_ref, dba_ref, dbi_ref, dlam_ref):
                ref[...] = jnp.zeros_like(ref)

        xcv = xc_ref[...]
        xb, r, ig, sp, a, mult = _rg_gate_values(xcv, wa_ref, wi_ref, ba_ref, bi_ref, lam_ref)
        gv = gu_ref[...]
        d_ixc = gv * mult
        d_i = d_ixc * xcv
        d_mult = gv * ig * xcv
        d_a = gv * hprev - d_mult * a / mult
        d_log_a = d_a * a
        d_r = d_log_a * (-LRU_C * sp)
        sig_neg_lam = 1.0 / (1.0 + jnp.exp(lam_ref[...]))
        dlam_ref[...] += jnp.sum(d_log_a * r, axis=0, keepdims=True) * (LRU_C * sig_neg_lam)
        dpa = d_r * r * (1.0 - r)
        dpi = d_i * ig * (1.0 - ig)
        dba_ref[...] += jnp.sum(dpa, axis=0, keepdims=True)
        dbi_ref[...] += jnp.sum(dpi, axis=0, keepdims=True)
        dpab, dpib = dpa.astype(BF16), dpi.astype(BF16)
        dxc_ref[...] = (d_ixc * ig + lax.dot_general(dpab, wa_ref[...], NT, preferred_element_type=F32)
                        + lax.dot_general(dpib, wi_ref[...], NT, preferred_element_type=F32))
        dwa_ref[...] += lax.dot_general(xb, dpab, TN, preferred_element_type=F32)
        dwi_ref[...] += lax.dot_general(xb, dpib, TN, preferred_element_type=F32)

    blk = pl.BlockSpec((tm, cb), lambda n, i: (i, n))
    wsp = pl.BlockSpec((None, cb, cb), lambda n, i: (n, 0, 0))
    vec = pl.BlockSpec((1, cb), lambda n, i: (0, n))
    halo = pl.BlockSpec((HALO, cb), lambda n, i: (jnp.maximum(i * hb - 1, 0), n))
    return pl.pallas_call(
        body, grid=(nb, t // tm), in_specs=[blk, blk, blk, halo, wsp, wsp, vec, vec, vec],
        out_specs=[blk, wsp, wsp, vec, vec, vec],
        out_shape=[_sds((t, c), F32), _sds((nb, cb, cb), F32), _sds((nb, cb, cb), F32),
                   _sds((1, c), F32), _sds((1, c), F32), _sds((1, c), F32)],
        compiler_params=_params("parallel", "arbitrary"), name=name)(xc, gu, hs, hs, wa, wi, ba, bi, lam)


SCAN_TS = 256
SCAN_TC = 512


def _tile_scan(a, b, reverse):
    ts = a.shape[0]
    row = lax.broadcasted_iota(jnp.int32, a.shape, 0)
    d = 1
    while d < ts:
        if reverse:
            inside = row < ts - d
            a_sh = jnp.where(inside, pltpu.roll(a, ts - d, 0), 1.0)
            b_sh = jnp.where(inside, pltpu.roll(b, ts - d, 0), 0.0)
        else:
            inside = row >= d
            a_sh = jnp.where(inside, pltpu.roll(a, d, 0), 1.0)
            b_sh = jnp.where(inside, pltpu.roll(b, d, 0), 0.0)
        b = b + a * b_sh
        a = a * a_sh
        d *= 2
    return a, b


def rg_scan_fwd(name, a, u, gate_pre):
    t, c = a.shape
    ts, tc = _row_tile(t, SCAN_TS), _row_tile(c, SCAN_TC)

    def body(a_ref, u_ref, g_ref, h_ref, z_ref, carry_ref):
        s = pl.program_id(1)

        @pl.when(s == 0)
        def _():
            carry_ref[...] = jnp.zeros_like(carry_ref)

        ac, bc = _tile_scan(a_ref[...], u_ref[...], False)
        h = bc + ac * carry_ref[0:1, :]
        h_ref[...] = h
        z_ref[...] = (h * _gelu(g_ref[...])).astype(BF16)
        carry_ref[0:1, :] = h[ts - 1:ts, :]

    blk = pl.BlockSpec((ts, tc), lambda j, s: (s, j))
    return pl.pallas_call(
        body, grid=(c // tc, t // ts), in_specs=[blk, blk, blk], out_specs=[blk, blk],
        out_shape=[_sds((t, c), F32), _sds((t, c), BF16)], scratch_shapes=[pltpu.VMEM((8, tc), F32)],
        compiler_params=_params("parallel", "arbitrary"), name=name)(a, u, gate_pre)


def rg_scan_bwd(name, a, hs, gate_pre, dz):
    t, c = hs.shape
    ts, tc = _row_tile(t, SCAN_TS), _row_tile(c, SCAN_TC)
    nt = t // ts
    hb = ts // HALO

    def body(a_ref, halo_ref, h_ref, g_ref, dz_ref, gu_ref, dgate_ref, carry_ref):
        s = pl.program_id(1)

        @pl.when(s == 0)
        def _():
            carry_ref[...] = jnp.zeros_like(carry_ref)

        a_next = _shifted(jnp.concatenate([a_ref[...], jnp.where(s == 0, 0.0, halo_ref[...])], axis=0), 1, ts, False)
        gate = g_ref[...]
        dzv = dz_ref[...]
        dgate_ref[...] = (dzv * h_ref[...] * _gelu_grad(gate)).astype(BF16)
        ac, bc = _tile_scan(a_next, dzv * _gelu(gate), True)
        gu = bc + ac * carry_ref[0:1, :]
        gu_ref[...] = gu
        carry_ref[0:1, :] = gu[0:1, :]

    blk = pl.BlockSpec((ts, tc), lambda j, s: (nt - 1 - s, j))
    halo = pl.BlockSpec((HALO, tc), lambda j, s: (jnp.minimum((nt - s) * hb, t // HALO - 1), j))
    return pl.pallas_call(
        body, grid=(c // tc, nt), in_specs=[blk, halo, blk, blk, blk], out_specs=[blk, blk],
        out_shape=[_sds((t, c), F32), _sds((t, c), BF16)], scratch_shapes=[pltpu.VMEM((8, tc), F32)],
        compiler_params=_params("parallel", "arbitrary"), name=name)(a, a, hs, gate_pre, dz)


QA_BLK, KA_BLK, VA_BLK, QS_BLK, KS_BLK, VS_BLK = (g * N_PAIRS for g in range(6))


TOEP_W = 640
TOEP_FLAT = 320
TABLE_LOW = 193


def rel_bias_matrix(name, table):
    h = table.shape[0]
    diag = jnp.concatenate([jnp.repeat(table[:, 2 * REL_CLIP:], TOEP_FLAT, axis=1),
                            jnp.flip(table[:, TABLE_LOW:2 * REL_CLIP], axis=1),
                            jnp.zeros((h, 1), table.dtype)], axis=1)[:, None, :]

    def body(v_ref, o_ref):
        rows = jnp.broadcast_to(v_ref[...], (CHUNK, TOEP_W))
        o_ref[...] = pltpu.roll(rows, TOEP_W - (CHUNK - 1), 1, stride=1, stride_axis=0)

    out = pl.pallas_call(
        body, grid=(h,), in_specs=[pl.BlockSpec((None, 1, TOEP_W), lambda hh: (hh, 0, 0))],
        out_specs=pl.BlockSpec((None, CHUNK, TOEP_W), lambda hh: (hh, 0, 0)),
        out_shape=_sds((h, CHUNK, TOEP_W), F32), compiler_params=_params("parallel"), name=name)(diag)
    return out[:, :, :BAND]


def rel_bias_grad(name, dbias):
    h = dbias.shape[0]
    flipped = jnp.pad(jnp.flip(dbias, axis=1), ((0, 0), (0, 0), (0, TOEP_W - BAND)))

    def body(x_ref, o_ref):
        skew = pltpu.roll(x_ref[...], 0, 1, stride=1, stride_axis=0)
        col = jnp.sum(skew, axis=0, keepdims=True)
        lane = lax.broadcasted_iota(jnp.int32, col.shape, 1)
        flat = jnp.sum(jnp.where(lane < TOEP_FLAT, col, 0.0), axis=1, keepdims=True)
        o_ref[...] = jnp.where(lane == TOEP_W - 1, flat, col)

    out = pl.pallas_call(
        body, grid=(h,), in_specs=[pl.BlockSpec((None, CHUNK, TOEP_W), lambda hh: (hh, 0, 0))],
        out_specs=pl.BlockSpec((None, 1, TOEP_W), lambda hh: (hh, 0, 0)),
        out_shape=_sds((h, 1, TOEP_W), F32), compiler_params=_params("parallel"), name=name)(flipped)[:, 0, :]
    return jnp.concatenate([jnp.zeros((h, TABLE_LOW), F32), jnp.flip(out[:, TOEP_FLAT:TOEP_W - 1], axis=1),
                            out[:, TOEP_W - 1:]], axis=1)


def attn_layer_fwd(tag, x, h, w, g_next):
    proj = mm_nn_wblk(tag + "_proj", h, w["w_in"], w["idx"], BF16)
    width = N_PAIRS * PAIR
    pad = lambda a: jnp.pad(a, ((PAD_KEYS, 0), (0, 0)))
    kap, vap = pad(proj[:, width:2 * width]), pad(proj[:, 2 * width:3 * width])
    bias = rel_bias_matrix(tag + "_bias", w["rel_bias"])
    oa, lse = attn_a_fwd(tag + "_a", proj, kap, vap, bias, QA_BLK)
    ob = sb_fwd(tag + "_sb", proj, QS_BLK, KS_BLK, VS_BLK)
    o = jnp.concatenate([oa, ob], axis=1).astype(BF16)
    m = mm_nn(tag + "_out", o, w["w_out"], F32)
    x1, h_next = resid_norm_fwd(tag + "_res", x, m, w["g_post"], g_next)
    return x1, h_next, (x, h, proj, kap, vap, bias, oa, lse, ob, o, m)


def attn_layer_bwd(tag, dm, dx1, saved, w, prev):
    x, h, proj, kap, vap, bias, oa, lse, ob, o, m = saved
    d_w_out = mm_tn(tag + "_dwout", o, dm, F32)
    do = mm_nt(tag + "_do", dm, w["w_out"], BF16)
    dqa, dkap, dvap, dbias = attn_a_bwd(tag + "_da", proj, kap, vap, bias, oa, lse, do, QA_BLK, 0)
    dqs, dks, dvs = sb_bwd(tag + "_dsb", proj, ob, do, QS_BLK, KS_BLK, VS_BLK, N_PAIRS)
    d_rel = rel_bias_grad(tag + "_dbias", dbias)
    dproj = jnp.concatenate([dqa, dkap[PAD_KEYS:], dvap[PAD_KEYS:], dqs, dks, dvs], axis=1).astype(BF16)
    d_w_in = mm_tn_oblk(tag + "_dwin", h, dproj, w["w_in"].shape[3], F32)
    dh = mm_nt_wblk(tag + "_dh", dproj, w["w_in"], w["idx"], F32)
    dx, dg_pre, dm_prev, dg_post_prev = close_bwd(tag, dh, x, w["g_pre"], dx1, prev)
    return dx, dm_prev, dg_post_prev, dict(w_in=d_w_in, w_out=d_w_out, rel_bias=d_rel, g_pre=dg_pre)


def rg_layer_fwd(tag, x, h, w, g_next):
    proj = mm_nn_wblk(tag + "_proj", h, w["w_in"], w["idx"], F32)
    xc = conv4_fwd(tag + "_conv", proj, 1, w["conv_w"], w["conv_b"])
    a, u = rg_gates_fwd(tag + "_gates", xc, w["w_a"], w["w_i"], w["b_a"], w["b_i"], w["lam"])
    hs, z = rg_scan_fwd(tag + "_scan", a, u, proj)
    m = mm_nn(tag + "_out", z, w["w_out"], F32)
    x1, h_next = resid_norm_fwd(tag + "_res", x, m, w["g_post"], g_next)
    return x1, h_next, (x, h, proj, xc, a, hs, z, m)


def rg_layer_bwd(tag, dm, dx1, saved, w, prev):
    x, h, proj, xc, a, hs, z, m = saved
    d_w_out = mm_tn(tag + "_dwout", z, dm, F32)
    dz = mm_nt(tag + "_dz", dm, w["w_out"], F32)
    gu, dgate = rg_scan_bwd(tag + "_dscan", a, hs, proj, dz)
    dxc, d_w_a, d_w_i, d_b_a, d_b_i, d_lam = rg_gates_bwd(
        tag + "_dgates", xc, gu, hs, w["w_a"], w["w_i"], w["b_a"], w["b_i"], w["lam"])
    d_conv_w, d_conv_b = conv4_bwd_w(tag + "_dconvw", proj, 1, dxc)
    dxr = conv4_bwd_x(tag + "_dconv", dxc, w["conv_w"])
    dproj = jnp.concatenate([dgate, dxr.astype(BF16)], axis=1)
    d_w_in = mm_tn_oblk(tag + "_dwin", h, dproj, w["w_in"].shape[3], F32)
    dh = mm_nt_wblk(tag + "_dh", dproj, w["w_in"], w["idx"], F32)
    dx, dg_pre, dm_prev, dg_post_prev = close_bwd(tag, dh, x, w["g_pre"], dx1, prev)
    return dx, dm_prev, dg_post_prev, dict(w_in=d_w_in, w_out=d_w_out, conv_w=d_conv_w, conv_b=d_conv_b, w_a=d_w_a,
                                           w_i=d_w_i, b_a=d_b_a, b_i=d_b_i, lam=d_lam, g_pre=dg_pre)


def ffn_layer_fwd(tag, x, h, w, g_next):
    f8 = w["w_down"].shape[2]
    gu, a = ffn_up(tag + "_up", h, w["w_gu"], f8, w["idx"])
    f = ffn_down(tag + "_down", a, w["w_down"], w["idx"])
    x1, h_next = resid_norm_fwd(tag + "_res", x, f, w["g_post"], g_next)
    return x1, h_next, (x, h, gu, a, f)


def ffn_layer_bwd(tag, dm, dx1, saved, w, prev):
    x, h, gu, a, f = saved
    f8 = w["w_down"].shape[2]
    d_w_down = ffn_dw_down(tag + "_dwdown", a, dm)
    dgu = ffn_bwd_act(tag + "_dact", dm, w["w_down"], gu, f8, w["idx"])
    d_w_gu = ffn_dw_in(tag + "_dwgu", h, dgu)
    dh = ffn_bwd_dh(tag + "_dh", dgu, w["w_gu"], w["idx"])
    dx, dg_pre, dm_prev, dg_post_prev = close_bwd(tag, dh, x, w["g_pre"], dx1, prev)
    return dx, dm_prev, dg_post_prev, dict(w_gu=d_w_gu, w_down=d_w_down, g_pre=dg_pre)


def _place():
    return lax.axis_index("x"), lax.axis_index("y"), lax.axis_index("c")


def all_gather(name, blks):
    n = len(blks)

    def body(*refs):
        x_refs, out_refs = refs[:n], refs[n:2 * n]
        send_sems, recv_sems, local_sems = refs[2 * n:]
        x, y, cc = _place()
        me, sibling = (x, y, cc), (x, y, 1 - cc)
        chips = [(1 - x, y), (x, 1 - y), (1 - x, 1 - y)]
        south = cc == 0
        via = (jnp.where(south, 1 - x, x), jnp.where(south, y, 1 - y))
        onward = (jnp.where(south, x, 1 - x), jnp.where(south, 1 - y, y))
        k_via, k_onward = 1 + cc, 2 - cc

        def slot(a, px, py, pc):
            return out_refs[a].at[4 * px + 2 * py + pc]

        def copy(a, k, block, to, src=None):
            return pltpu.make_async_remote_copy(
                src_ref=slot(a, *block) if src is None else src, dst_ref=slot(a, *block),
                send_sem=send_sems.at[7 * a + k], recv_sem=recv_sems.at[7 * a + k], device_id=to, device_id_type=MESH)

        mine = [pltpu.make_async_copy(x_refs[a], slot(a, *me), local_sems.at[a]) for a in range(n)]
        sends = []
        for a in range(n):
            mine[a].start()
            sends.append(copy(a, 0, me, sibling, src=x_refs[a]))
            sends += [copy(a, 1 + j, me, (*chips[j], cc), src=x_refs[a]) for j in range(2)]
        for cp in sends:
            cp.start()
        for a in range(n):
            copy(a, k_via, (*via, cc), me).wait_recv()
            sends.append(copy(a, 3, (*via, cc), (*onward, cc)))
            sends.append(copy(a, 3 + k_via, (*via, cc), sibling))
            sends[-2].start()
            sends[-1].start()
        for a in range(n):
            copy(a, k_onward, (*onward, cc), me).wait_recv()
            sends.append(copy(a, 3 + k_onward, (*onward, cc), sibling))
            sends[-1].start()
        for a in range(n):
            copy(a, 3, (*chips[2], cc), me).wait_recv()
            sends.append(copy(a, 6, (*chips[2], cc), sibling))
            sends[-1].start()
        for a in range(n):
            copy(a, 0, sibling, me).wait_recv()
            for j, chip in enumerate(chips):
                copy(a, 4 + j, (*chip, 1 - cc), me).wait_recv()
        for cp in sends:
            cp.wait_send()
        for cp in mine:
            cp.wait()

    return pl.pallas_call(
        body, out_shape=[_sds((N_DEV,) + b.shape, b.dtype) for b in blks], in_specs=[ANY] * n, out_specs=[ANY] * n,
        scratch_shapes=[pltpu.SemaphoreType.DMA((7 * n,)), pltpu.SemaphoreType.DMA((7 * n,)),
                        pltpu.SemaphoreType.DMA((n,))],
        name=name)(*blks)


def exchange_pair(name, gs):
    n = len(gs)
    nchip = 4

    def body(*refs):
        g_refs, land_refs = refs[:n], refs[n:2 * n]
        send_sems, recv_sems = refs[2 * n:]
        x, y, cc = _place()
        copies = [pltpu.make_async_remote_copy(
            src_ref=g_refs[a].at[j, 1 - cc], dst_ref=land_refs[a].at[j], send_sem=send_sems.at[nchip * a + j],
            recv_sem=recv_sems.at[nchip * a + j], device_id=(x, y, 1 - cc), device_id_type=MESH)
            for a in range(n) for j in range(nchip)]
        for cp in copies:
            cp.start()
        for cp in copies:
            cp.wait()

    return pl.pallas_call(
        body, out_shape=[_sds((nchip,) + g.shape[2:], g.dtype) for g in gs], in_specs=[ANY] * n, out_specs=[ANY] * n,
        scratch_shapes=[pltpu.SemaphoreType.DMA((nchip * n,)), pltpu.SemaphoreType.DMA((nchip * n,))],
        name=name)(*gs)


def pair_sum(name, g, land, core, out_dtype):
    nchip, _, r, c = g.shape
    tr = _divisor_tile(r, 1024, 16)

    def body(core_ref, g_ref, l_ref, o_ref):
        o_ref[...] = (g_ref[...] + l_ref[...]).astype(o_ref.dtype)

    return pl.pallas_call(
        body,
        grid_spec=pltpu.PrefetchScalarGridSpec(
            num_scalar_prefetch=1, grid=(nchip, r // tr),
            in_specs=[pl.BlockSpec((None, None, tr, c), lambda j, i, core_ref: (j, core_ref[0], i, 0)),
                      pl.BlockSpec((None, tr, c), lambda j, i, core_ref: (j, i, 0))],
            out_specs=pl.BlockSpec((None, tr, c), lambda j, i, core_ref: (j, i, 0))),
        out_shape=_sds((nchip, r, c), out_dtype), compiler_params=_params("parallel", "parallel"), name=name,
    )(core, g, land)


def exchange_chips(name, ps):
    n = len(ps)

    def body(*refs):
        p_refs, land_refs = refs[:n], refs[n:2 * n]
        send_sems, recv_sems, local_sems = refs[2 * n:]
        x, y, cc = _place()
        mine = 2 * x + y
        chips = [(1 - x, y), (x, 1 - y), (1 - x, 1 - y)]
        own = [pltpu.make_async_copy(p_refs[a].at[mine], land_refs[a].at[mine], local_sems.at[a]) for a in range(n)]
        for cp in own:
            cp.start()
        sends = [pltpu.make_async_remote_copy(
            src_ref=p_refs[a].at[2 * px + py], dst_ref=land_refs[a].at[mine], send_sem=send_sems.at[3 * a + k],
            recv_sem=recv_sems.at[3 * a + k], device_id=(px, py, cc), device_id_type=MESH)
            for a in range(n) for k, (px, py) in enumerate(chips)]
        for cp in sends:
            cp.start()
        for a in range(n):
            for k, (px, py) in enumerate(chips):
                pltpu.make_async_remote_copy(
                    src_ref=p_refs[a].at[mine], dst_ref=land_refs[a].at[2 * px + py], send_sem=send_sems.at[3 * a + k],
                    recv_sem=recv_sems.at[3 * a + k], device_id=(px, py, cc), device_id_type=MESH).wait_recv()
        for cp in sends:
            cp.wait_send()
        for cp in own:
            cp.wait()

    return pl.pallas_call(
        body, out_shape=[_sds(p.shape, p.dtype) for p in ps], in_specs=[ANY] * n, out_specs=[ANY] * n,
        scratch_shapes=[pltpu.SemaphoreType.DMA((3 * n,)), pltpu.SemaphoreType.DMA((3 * n,)),
                        pltpu.SemaphoreType.DMA((n,))],
        name=name)(*ps)


def adamw(name, parts, w, m, v):
    npart, r, c = parts.shape
    tr = _divisor_tile(r, 512, 16)
    c1 = 1.0 / (1.0 - ADAM_B1 ** ADAM_STEP)
    c2 = 1.0 / (1.0 - ADAM_B2 ** ADAM_STEP)

    def body(p_ref, w_ref, m_ref, v_ref, g_ref, d_ref, nm_ref, nv_ref):
        g = p_ref[0].astype(F32)
        for j in range(1, npart):
            g = g + p_ref[j].astype(F32)
        nm = ADAM_B1 * m_ref[...] + (1.0 - ADAM_B1) * g
        nv = ADAM_B2 * v_ref[...] + (1.0 - ADAM_B2) * (g * g)
        g_ref[...] = g
        nm_ref[...] = nm
        nv_ref[...] = nv
        d_ref[...] = -ADAM_LR * ((nm * c1) / (jnp.sqrt(nv * c2) + ADAM_EPS) + ADAM_WD * w_ref[...])

    row = pl.BlockSpec((tr, c), lambda i: (i, 0))
    return pl.pallas_call(
        body, grid=(r // tr,), in_specs=[pl.BlockSpec((npart, tr, c), lambda i: (0, i, 0)), row, row, row],
        out_specs=[row] * 4, out_shape=[_sds((r, c), F32)] * 4, compiler_params=_params("parallel"), name=name,
    )(parts, w, m, v)


def _pack(arrays, dtype, row_multiple):
    flat = jnp.concatenate([a.astype(dtype).reshape(-1) for a in arrays])
    per = row_multiple * LANES
    total = -(-flat.shape[0] // per) * per
    return jnp.pad(flat, (0, total - flat.shape[0])).reshape(total // LANES, LANES)


def _pack_blocked(arrays, dtype, row_multiple):
    flat = jnp.concatenate([a.astype(dtype).reshape(N_DEV, -1) for a in arrays], axis=1)
    per = row_multiple * LANES
    total = -(-flat.shape[1] // per) * per
    return jnp.pad(flat, ((0, 0), (0, total - flat.shape[1]))).reshape(N_DEV, total // LANES, LANES)


def _unpack(buf, shapes, lead=()):
    flat = buf.reshape(lead + (-1,))
    out, off = [], 0
    for s in shapes:
        n = math.prod(s)
        out.append(flat[..., off:off + n].reshape(lead + tuple(s)))
        off += n
    return out


def _to_blocked(full, ax):
    s = full.shape
    return jnp.moveaxis(full.reshape(s[:ax] + (N_DEV, s[ax] // N_DEV) + s[ax + 1:]), ax, 0)


def _from_blocked(blk, ax):
    moved = jnp.moveaxis(blk, 0, ax)
    s = moved.shape
    return moved.reshape(s[:ax] + (s[ax] * s[ax + 1],) + s[ax + 2:])


SMALL = ("rg_conv_w", "rg_conv_b", "rg_b_a", "rg_b_i", "rg_lambda")
GU = "ffn_w_gu"
BIG = ("attn_w_in", "attn_w_out", "rg_w_in", "rg_w_a", "rg_w_i", "rg_w_out", GU, "ffn_w_down")


def kernel(x, attn_w_in, attn_rel_bias, attn_w_out, rg_w_in, rg_conv_w, rg_conv_b, rg_w_a, rg_b_a, rg_w_i, rg_b_i, rg_lambda, rg_w_out, norm_mix_pre, norm_mix_post, norm_ffn_pre, norm_ffn_post, ffn_w_gate, ffn_w_up, ffn_w_down, loss_target, m_attn_w_in, m_attn_rel_bias, m_attn_w_out, m_rg_w_in, m_rg_conv_w, m_rg_conv_b, m_rg_w_a, m_rg_b_a, m_rg_w_i, m_rg_b_i, m_rg_lambda, m_rg_w_out, m_norm_mix_pre, m_norm_mix_post, m_norm_ffn_pre, m_norm_ffn_post, m_ffn_w_gate, m_ffn_w_up, m_ffn_w_down, v_attn_w_in, v_attn_rel_bias, v_attn_w_out, v_rg_w_in, v_rg_conv_w, v_rg_conv_b, v_rg_w_a, v_rg_b_a, v_rg_w_i, v_rg_b_i, v_rg_lambda, v_rg_w_out, v_norm_mix_pre, v_norm_mix_post, v_norm_ffn_pre, v_norm_ffn_post, v_ffn_w_gate, v_ffn_w_up, v_ffn_w_down):
    w_loc = dict(attn_w_in=attn_w_in, attn_rel_bias=attn_rel_bias, attn_w_out=attn_w_out, rg_w_in=rg_w_in,
                 rg_conv_w=rg_conv_w, rg_conv_b=rg_conv_b, rg_w_a=rg_w_a, rg_b_a=rg_b_a, rg_w_i=rg_w_i, rg_b_i=rg_b_i,
                 rg_lambda=rg_lambda, rg_w_out=rg_w_out, norm_mix_pre=norm_mix_pre, norm_mix_post=norm_mix_post,
                 norm_ffn_pre=norm_ffn_pre, norm_ffn_post=norm_ffn_post, ffn_w_gate=ffn_w_gate, ffn_w_up=ffn_w_up,
                 ffn_w_down=ffn_w_down)
    m_loc = dict(attn_w_in=m_attn_w_in, attn_rel_bias=m_attn_rel_bias, attn_w_out=m_attn_w_out, rg_w_in=m_rg_w_in,
                 rg_conv_w=m_rg_conv_w, rg_conv_b=m_rg_conv_b, rg_w_a=m_rg_w_a, rg_b_a=m_rg_b_a, rg_w_i=m_rg_w_i,
                 rg_b_i=m_rg_b_i, rg_lambda=m_rg_lambda, rg_w_out=m_rg_w_out, norm_mix_pre=m_norm_mix_pre,
                 norm_mix_post=m_norm_mix_post, norm_ffn_pre=m_norm_ffn_pre, norm_ffn_post=m_norm_ffn_post,
                 ffn_w_gate=m_ffn_w_gate, ffn_w_up=m_ffn_w_up, ffn_w_down=m_ffn_w_down)
    v_loc = dict(attn_w_in=v_attn_w_in, attn_rel_bias=v_attn_rel_bias, attn_w_out=v_attn_w_out, rg_w_in=v_rg_w_in,
                 rg_conv_w=v_rg_conv_w, rg_conv_b=v_rg_conv_b, rg_w_a=v_rg_w_a, rg_b_a=v_rg_b_a, rg_w_i=v_rg_w_i,
                 rg_b_i=v_rg_b_i, rg_lambda=v_rg_lambda, rg_w_out=v_rg_w_out, norm_mix_pre=v_norm_mix_pre,
                 norm_mix_post=v_norm_mix_post, norm_ffn_pre=v_norm_ffn_pre, norm_ffn_post=v_norm_ffn_post,
                 ffn_w_gate=v_ffn_w_gate, ffn_w_up=v_ffn_w_up, ffn_w_down=v_ffn_w_down)
    axis_of = dict(SHARDED)
    xt, target = x[0], loss_target[0]
    d_model = xt.shape[1]
    rows2d = lambda a: a.reshape(-1, a.shape[-1])
    small_shapes = [w_loc[n].shape for n in SMALL]
    f8 = ffn_w_gate.shape[-1]
    for d in (w_loc, m_loc, v_loc):
        d[GU] = merge_gu(d["ffn_w_gate"], d["ffn_w_up"])

    gathered = all_gather("gather_weights", [rows2d(w_loc[n]).astype(BF16) for n in BIG]
                          + [_pack([w_loc[n] for n in SMALL], F32, 8)])
    blocked = {n: g.reshape((N_DEV,) + w_loc[n].shape) for n, g in zip(BIG, gathered)}
    blocked.update(zip(SMALL, _unpack(gathered[-1], small_shapes, (N_DEV,))))
    full = {n: _from_blocked(blocked[n], axis_of[n]) for n in SMALL}
    row = lambda a: a.reshape(1, -1).astype(F32)
    square = lambda rows8: rows8.reshape(-1, rows8.shape[-1])
    gates = lambda g: jnp.swapaxes(g, 0, 1).reshape(LRU_BLOCKS, -1, g.shape[-1])

    def layer_weights(layer):
        j = layer // 2
        norms = dict(g_pre=row(norm_mix_pre[layer]), g_post=row(norm_mix_post[layer]), idx=j)
        if layer % 2 == 0:
            mix = dict(w_in=blocked["attn_w_in"], w_out=square(blocked["attn_w_out"][:, j]),
                       rel_bias=attn_rel_bias[j], **norms)
        else:
            mix = dict(w_in=blocked["rg_w_in"], w_out=square(blocked["rg_w_out"][:, j]),
                       conv_w=full["rg_conv_w"][j][:, 0, :], conv_b=row(full["rg_conv_b"][j]),
                       w_a=gates(blocked["rg_w_a"][:, j]), w_i=gates(blocked["rg_w_i"][:, j]),
                       b_a=row(full["rg_b_a"][j]), b_i=row(full["rg_b_i"][j]), lam=row(full["rg_lambda"][j]), **norms)
        ffn = dict(w_gu=blocked[GU], w_down=blocked["ffn_w_down"], idx=layer,
                   g_pre=row(norm_ffn_pre[layer]), g_post=row(norm_ffn_post[layer]))
        return mix, ffn

    weights = [layer_weights(layer) for layer in range(DEPTH)]
    act, tape = xt, []
    h = rmsnorm_fwd("l0_mix_norm", act, weights[0][0]["g_pre"])
    for layer in range(DEPTH):
        mix_w, ffn_w = weights[layer]
        mixer_fwd = attn_layer_fwd if layer % 2 == 0 else rg_layer_fwd
        act, h, saved_mix = mixer_fwd(f"l{layer}_mix", act, h, mix_w, ffn_w["g_pre"])
        g_next = weights[layer + 1][0]["g_pre"] if layer + 1 < DEPTH else None
        act, h, saved_ffn = ffn_layer_fwd(f"l{layer}_ffn", act, h, ffn_w, g_next)
        tape.append((mix_w, ffn_w, saved_mix, saved_ffn))
    dact, sq = loss_grad("loss", act, target)
    loss_part = (0.5 * jnp.sum(sq) / d_model).reshape(1)

    grads = {}
    last = tape[DEPTH - 1]
    dm, grads[("ffn_post", DEPTH - 1)] = norm_bwd(f"l{DEPTH - 1}_ffn_dpost", dact, last[3][-1], last[1]["g_post"],
                                                 None, BF16)
    for layer in reversed(range(DEPTH)):
        mix_w, ffn_w, saved_mix, saved_ffn = tape[layer]
        dact, dm, grads[("mix_post", layer)], grads[("ffn", layer)] = ffn_layer_bwd(
            f"l{layer}_ffn", dm, dact, saved_ffn, ffn_w, (saved_mix[-1], mix_w["g_post"]))
        mixer_bwd = attn_layer_bwd if layer % 2 == 0 else rg_layer_bwd
        prev = (tape[layer - 1][3][-1], tape[layer - 1][1]["g_post"]) if layer else None
        dact, dm, grads[("ffn_post", layer - 1)], grads[("mix", layer)] = mixer_bwd(
            f"l{layer}_mix", dm, dact, saved_mix, mix_w, prev)
    attn_g = [grads[("mix", l)] for l in range(0, DEPTH, 2)]
    rg_g = [grads[("mix", l)] for l in range(1, DEPTH, 2)]
    ffn_g = [grads[("ffn", l)] for l in range(DEPTH)]
    stack = lambda gs, key: jnp.stack([g[key] for g in gs])
    by_owner = lambda gs, key, f: jnp.stack([f(g[key]) for g in gs], axis=1)
    rows8 = lambda a: a.reshape(N_DEV, -1, a.shape[-1])
    ungates = lambda a: jnp.swapaxes(a.reshape(LRU_BLOCKS, N_DEV, -1, a.shape[-1]), 0, 1)
    same = lambda a: a
    blocked_g = dict(
        attn_w_in=by_owner(attn_g, "w_in", same), attn_w_out=by_owner(attn_g, "w_out", rows8),
        rg_w_in=by_owner(rg_g, "w_in", same), rg_w_out=by_owner(rg_g, "w_out", rows8),
        rg_w_a=by_owner(rg_g, "w_a", ungates), rg_w_i=by_owner(rg_g, "w_i", ungates),
        **{GU: by_owner(ffn_g, "w_gu", same)},
        ffn_w_down=by_owner(ffn_g, "w_down", same))
    contrib = dict(
        attn_rel_bias=stack(attn_g, "rel_bias"), rg_conv_w=stack(rg_g, "conv_w")[:, :, None, :],
        rg_conv_b=stack(rg_g, "conv_b")[:, 0], rg_b_a=stack(rg_g, "b_a").reshape(rg_b_a.shape[0], LRU_BLOCKS, -1),
        rg_b_i=stack(rg_g, "b_i").reshape(rg_b_i.shape[0], LRU_BLOCKS, -1), rg_lambda=stack(rg_g, "lam")[:, 0],
        norm_mix_pre=jnp.concatenate([grads[("mix", l)]["g_pre"] for l in range(DEPTH)]),
        norm_mix_post=jnp.concatenate([grads[("mix_post", l)] for l in range(DEPTH)]),
        norm_ffn_pre=jnp.concatenate([g["g_pre"] for g in ffn_g]),
        norm_ffn_post=jnp.concatenate([grads[("ffn_post", l)] for l in range(DEPTH)]),
    )
    small_g = _pack_blocked([_to_blocked(contrib[n], axis_of[n]) for n in SMALL], F32, 8)

    slabs = [blocked_g[n].reshape(4, 2, -1, blocked_g[n].shape[-1]) for n in BIG] + [small_g.reshape(4, 2, -1, LANES)]
    core = lax.axis_index("c").astype(jnp.int32).reshape(1)
    from_sibling = exchange_pair("rs_pair", slabs)
    pairs = [pair_sum(f"rs_pair_sum_{i}", g, l, core, BF16 if i < len(BIG) else F32)
             for i, (g, l) in enumerate(zip(slabs, from_sibling))]
    by_chip = exchange_chips("rs_chips", pairs)
    result = {}
    kinds = ("grad", "delta", "new_m", "new_v")
    for n, parts in zip(BIG, by_chip):
        outs = adamw("adamw_" + n, parts, *[rows2d(d[n]) for d in (w_loc, m_loc, v_loc)])
        for kind, a in zip(kinds, outs):
            result[(kind, n)] = a.reshape(w_loc[n].shape)
    for kind in kinds:
        result[(kind, "ffn_w_gate")], result[(kind, "ffn_w_up")] = split_gu(result.pop((kind, GU)), f8)
    outs = adamw("adamw_small", by_chip[-1], *[_pack([d[n] for n in SMALL], F32, 8) for d in (w_loc, m_loc, v_loc)])
    for kind, buf in zip(kinds, outs):
        result.update({(kind, n): a for n, a in zip(SMALL, _unpack(buf, small_shapes))})
    rep_shapes = [w_loc[n].shape for n in REPLICATED] + [(1,)]
    rep_parts, = all_gather("gather_rep_grads", [_pack([contrib[n] for n in REPLICATED] + [loss_part], F32, 8)])
    outs = adamw("adamw_replicated", rep_parts, *[_pack([d[n] for n in REPLICATED] + [jnp.zeros((1,), F32)], F32, 8)
                                                  for d in (w_loc, m_loc, v_loc)])
    for kind, buf in zip(kinds, outs):
        result.update({(kind, n): a for n, a in zip(REPLICATED + ("loss",), _unpack(buf, rep_shapes))})
    loss = result[("grad", "loss")][0]
    return (loss, dact[None], *[result[(kind, n)] for kind in kinds for n in WEIGHTS])
```

```python
import functools
import math

import jax
import jax.numpy as jnp
from jax import lax
from jax.experimental import pallas as pl
from jax.experimental.pallas import tpu as pltpu

F32 = jnp.float32
BF16 = jnp.bfloat16

N_DEV = 8
DEPTH = 4
CHUNK = 64
N_LEFT = 8
BAND = (N_LEFT + 1) * CHUNK
PAD_KEYS = N_LEFT * CHUNK
HEAD_DIM = 64
N_HEADS = 8
REL_CLIP = 256
LRU_BLOCKS = 4
LRU_C = 8.0
RMS_EPS = 1e-6
QK_SCALE = HEAD_DIM ** -0.5

ADAM_LR = 0.001
ADAM_B1 = 0.9
ADAM_B2 = 0.999
ADAM_EPS = 1e-08
ADAM_WD = 0.01
ADAM_STEP = 10

LANES = 1024
V7X_VMEM_LIMIT = 56 * 1024 * 1024

MESH = pl.DeviceIdType.MESH
ANY = pl.BlockSpec(memory_space=pl.ANY)

SHARDED = (
    ("attn_w_in", 2), ("attn_w_out", 1), ("rg_w_in", 2), ("rg_conv_w", 3), ("rg_conv_b", 1),
    ("rg_w_a", 2), ("rg_b_a", 2), ("rg_w_i", 2), ("rg_b_i", 2), ("rg_lambda", 1), ("rg_w_out", 1),
    ("ffn_w_gate", 2), ("ffn_w_up", 2), ("ffn_w_down", 1),
)
REPLICATED = ("attn_rel_bias", "norm_mix_pre", "norm_mix_post", "norm_ffn_pre", "norm_ffn_post")
WEIGHTS = ("attn_w_in", "attn_rel_bias", "attn_w_out", "rg_w_in", "rg_conv_w", "rg_conv_b", "rg_w_a", "rg_b_a",
           "rg_w_i", "rg_b_i", "rg_lambda", "rg_w_out", "norm_mix_pre", "norm_mix_post", "norm_ffn_pre",
           "norm_ffn_post", "ffn_w_gate", "ffn_w_up", "ffn_w_down")


def _params(*dims):
    return pltpu.CompilerParams(dimension_semantics=dims or None, vmem_limit_bytes=V7X_VMEM_LIMIT)


def _sds(shape, dtype):
    return jax.ShapeDtypeStruct(tuple(shape), dtype)


def _row_tile(n, pref):
    t = min(n, pref)
    assert n % t == 0, (n, pref)
    return t


def _divisor_tile(n, limit, multiple):
    if n <= limit:
        return n
    best = max(t for t in range(multiple, limit + 1, multiple) if n % t == 0)
    return best


NN = (((1,), (0,)), ((), ()))
NT = (((1,), (1,)), ((), ()))
TN = (((0,), (0,)), ((), ()))


def _gmm(name, a, b, *, grid, a_blk, a_idx, b_blk, b_idx, o_blk, o_idx, out_shape, out_dtype, dn, acc_shape):
    nk = grid[-1]
    kax = len(grid) - 1

    def body(a_ref, b_ref, o_ref, acc_ref):
        part = lax.dot_general(a_ref[...], b_ref[...], dn, preferred_element_type=F32)
        if nk == 1:
            o_ref[...] = part.astype(o_ref.dtype)
            return
        k = pl.program_id(kax)

        @pl.when(k == 0)
        def _():
            acc_ref[...] = part

        @pl.when(k > 0)
        def _():
            acc_ref[...] += part

        @pl.when(k == nk - 1)
        def _():
            o_ref[...] = acc_ref[...].astype(o_ref.dtype)

    return pl.pallas_call(
        body, grid=grid,
        in_specs=[pl.BlockSpec(a_blk, a_idx), pl.BlockSpec(b_blk, b_idx)],
        out_specs=pl.BlockSpec(o_blk, o_idx),
        out_shape=_sds(out_shape, out_dtype),
        scratch_shapes=[pltpu.VMEM(acc_shape, F32)],
        compiler_params=_params(*(["parallel"] * kax + ["arbitrary"])),
        name=name,
    )(a, b)


def mm_nn(name, a, b, out_dtype, tm=1024, tn=512, tk=1024):
    (m, k), (_, n) = a.shape, b.shape
    tm, tn, tk = _row_tile(m, tm), _row_tile(n, tn), _row_tile(k, tk)
    return _gmm(name, a, b, grid=(m // tm, n // tn, k // tk),
                a_blk=(tm, tk), a_idx=lambda i, j, kk: (i, kk), b_blk=(tk, tn), b_idx=lambda i, j, kk: (kk, j),
                o_blk=(tm, tn), o_idx=lambda i, j, kk: (i, j), out_shape=(m, n), out_dtype=out_dtype, dn=NN,
                acc_shape=(tm, tn))


def mm_nt(name, a, b, out_dtype, tm=1024, tn=512, tk=1024):
    (m, k), (n, _) = a.shape, b.shape
    tm, tn, tk = _row_tile(m, tm), _row_tile(n, tn), _row_tile(k, tk)
    return _gmm(name, a, b, grid=(m // tm, n // tn, k // tk),
                a_blk=(tm, tk), a_idx=lambda i, j, kk: (i, kk), b_blk=(tn, tk), b_idx=lambda i, j, kk: (j, kk),
                o_blk=(tm, tn), o_idx=lambda i, j, kk: (i, j), out_shape=(m, n), out_dtype=out_dtype, dn=NT,
                acc_shape=(tm, tn))


def mm_tn(name, a, b, out_dtype, tm=1024, tn=512, tk=2048):
    (k, m), (_, n) = a.shape, b.shape
    tm, tn, tk = _row_tile(m, tm), _row_tile(n, tn), _row_tile(k, tk)
    return _gmm(name, a, b, grid=(m // tm, n // tn, k // tk),
                a_blk=(tk, tm), a_idx=lambda i, j, kk: (kk, i), b_blk=(tk, tn), b_idx=lambda i, j, kk: (kk, j),
                o_blk=(tm, tn), o_idx=lambda i, j, kk: (i, j), out_shape=(m, n), out_dtype=out_dtype, dn=TN,
                acc_shape=(tm, tn))


def mm_nn_wblk(name, a, wb, layer, out_dtype, tm=1024, tk=1024):
    (m, k), (nb, _, _, n8) = a.shape, wb.shape
    tm, tk = _row_tile(m, tm), _row_tile(k, tk)
    return _gmm(name, a, wb, grid=(m // tm, nb, k // tk),
                a_blk=(tm, tk), a_idx=lambda i, j, kk: (i, kk),
                b_blk=(None, None, tk, n8), b_idx=lambda i, j, kk: (j, layer, kk, 0),
                o_blk=(tm, n8), o_idx=lambda i, j, kk: (i, j), out_shape=(m, nb * n8), out_dtype=out_dtype, dn=NN,
                acc_shape=(tm, n8))


def mm_nt_wblk(name, a, wb, layer, out_dtype, tm=1024, tn=512):
    m = a.shape[0]
    nb, _, k, n8 = wb.shape
    tm, tn = _row_tile(m, tm), _row_tile(k, tn)

    def body(a_ref, b_ref, o_ref):
        acc = lax.dot_general(a_ref[:, 0:n8], b_ref[0], NT, preferred_element_type=F32)
        for j in range(1, nb):
            acc = acc + lax.dot_general(a_ref[:, j * n8:(j + 1) * n8], b_ref[j], NT, preferred_element_type=F32)
        o_ref[...] = acc.astype(o_ref.dtype)

    return pl.pallas_call(
        body, grid=(m // tm, k // tn),
        in_specs=[pl.BlockSpec((tm, nb * n8), lambda i, j: (i, 0)),
                  pl.BlockSpec((nb, None, tn, n8), lambda i, j: (0, layer, j, 0))],
        out_specs=pl.BlockSpec((tm, tn), lambda i, j: (i, j)), out_shape=_sds((m, k), out_dtype),
        compiler_params=_params("parallel", "parallel"), name=name)(a, wb)


def mm_tn_oblk(name, a, b, n8, out_dtype, tk=2048):
    (t, k), nb = a.shape, b.shape[1] // n8
    tk = _row_tile(t, tk)
    return _gmm(name, a, b, grid=(nb, t // tk),
                a_blk=(tk, k), a_idx=lambda j, s: (s, 0), b_blk=(tk, n8), b_idx=lambda j, s: (s, j),
                o_blk=(None, k, n8), o_idx=lambda j, s: (j, 0, 0), out_shape=(nb, k, n8), out_dtype=out_dtype, dn=TN,
                acc_shape=(k, n8))


def rmsnorm_fwd(name, x, g):
    t, d = x.shape
    tr = _row_tile(t, 512)

    def body(x_ref, g_ref, o_ref):
        xv = x_ref[...]
        r = lax.rsqrt(jnp.mean(xv * xv, axis=-1, keepdims=True) + RMS_EPS)
        o_ref[...] = (xv * r * g_ref[...]).astype(o_ref.dtype)

    return pl.pallas_call(
        body, grid=(t // tr,),
        in_specs=[pl.BlockSpec((tr, d), lambda i: (i, 0)), pl.BlockSpec((1, d), lambda i: (0, 0))],
        out_specs=pl.BlockSpec((tr, d), lambda i: (i, 0)),
        out_shape=_sds((t, d), BF16), compiler_params=_params("parallel"), name=name)(x, g)


def resid_norm_fwd(name, x, m, g, g_next):
    t, d = x.shape
    tr = _row_tile(t, 512)
    chained = g_next is not None

    def body(*refs):
        x_ref, m_ref, g_ref = refs[:3]
        mv = m_ref[...]
        r = lax.rsqrt(jnp.mean(mv * mv, axis=-1, keepdims=True) + RMS_EPS)
        x1 = x_ref[...] + mv * r * g_ref[...]
        if chained:
            gn_ref, o_ref, h_ref = refs[3:]
            r1 = lax.rsqrt(jnp.mean(x1 * x1, axis=-1, keepdims=True) + RMS_EPS)
            h_ref[...] = (x1 * r1 * gn_ref[...]).astype(BF16)
        else:
            o_ref, = refs[3:]
        o_ref[...] = x1

    row = pl.BlockSpec((tr, d), lambda i: (i, 0))
    vec = pl.BlockSpec((1, d), lambda i: (0, 0))
    out = pl.pallas_call(
        body, grid=(t // tr,),
        in_specs=[row, row, vec] + ([vec] if chained else []),
        out_specs=[row, row] if chained else [row],
        out_shape=[_sds((t, d), F32)] + ([_sds((t, d), BF16)] if chained else []),
        compiler_params=_params("parallel"), name=name)(*([x, m, g] + ([g_next] if chained else [])))
    return (out[0], out[1]) if chained else (out[0], None)


def norm_bwd(name, dy, x, g, resid, out_dtype):
    t, d = x.shape
    tr = _row_tile(t, 512)
    has_res = resid is not None

    def body(*refs):
        if has_res:
            dy_ref, x_ref, g_ref, r_ref, dx_ref, dg_ref = refs
        else:
            dy_ref, x_ref, g_ref, dx_ref, dg_ref = refs
        i = pl.program_id(0)
        xv = x_ref[...]
        dyv = dy_ref[...].astype(F32)
        r = lax.rsqrt(jnp.mean(xv * xv, axis=-1, keepdims=True) + RMS_EPS)
        xh = xv * r
        dxh = dyv * g_ref[...]
        dx = r * (dxh - xh * jnp.mean(dxh * xh, axis=-1, keepdims=True))
        if has_res:
            dx = dx + r_ref[...]
        dx_ref[...] = dx.astype(dx_ref.dtype)
        part = jnp.sum(dyv * xh, axis=0, keepdims=True)

        @pl.when(i == 0)
        def _():
            dg_ref[...] = part

        @pl.when(i > 0)
        def _():
            dg_ref[...] += part

    row = pl.BlockSpec((tr, d), lambda i: (i, 0))
    vec = pl.BlockSpec((1, d), lambda i: (0, 0))
    ins = [dy, x, g] + ([resid] if has_res else [])
    return pl.pallas_call(
        body, grid=(t // tr,),
        in_specs=[row, row, vec] + ([row] if has_res else []),
        out_specs=[row, vec],
        out_shape=[_sds((t, d), out_dtype), _sds((1, d), F32)],
        compiler_params=_params("arbitrary"), name=name)(*ins)


def norm_bwd_chain(name, dh, x, g, resid, m_prev, g_prev):
    t, d = x.shape
    tr = _row_tile(t, 512)

    def body(dh_ref, x_ref, g_ref, r_ref, m_ref, gp_ref, dx_ref, dg_ref, dm_ref, dgp_ref):
        i = pl.program_id(0)
        xv = x_ref[...]
        dhv = dh_ref[...]
        r = lax.rsqrt(jnp.mean(xv * xv, axis=-1, keepdims=True) + RMS_EPS)
        xh = xv * r
        dxh = dhv * g_ref[...]
        dx = r * (dxh - xh * jnp.mean(dxh * xh, axis=-1, keepdims=True)) + r_ref[...]
        dx_ref[...] = dx
        mv = m_ref[...]
        rm = lax.rsqrt(jnp.mean(mv * mv, axis=-1, keepdims=True) + RMS_EPS)
        mh = mv * rm
        dmh = dx * gp_ref[...]
        dm_ref[...] = (rm * (dmh - mh * jnp.mean(dmh * mh, axis=-1, keepdims=True))).astype(BF16)
        part = jnp.sum(dhv * xh, axis=0, keepdims=True)
        part_prev = jnp.sum(dx * mh, axis=0, keepdims=True)

        @pl.when(i == 0)
        def _():
            dg_ref[...] = part
            dgp_ref[...] = part_prev

        @pl.when(i > 0)
        def _():
            dg_ref[...] += part
            dgp_ref[...] += part_prev

    row = pl.BlockSpec((tr, d), lambda i: (i, 0))
    vec = pl.BlockSpec((1, d), lambda i: (0, 0))
    return pl.pallas_call(
        body, grid=(t // tr,), in_specs=[row, row, vec, row, row, vec], out_specs=[row, vec, row, vec],
        out_shape=[_sds((t, d), F32), _sds((1, d), F32), _sds((t, d), BF16), _sds((1, d), F32)],
        compiler_params=_params("arbitrary"), name=name)(dh, x, g, resid, m_prev, g_prev)


def close_bwd(tag, dh, x, g_pre, dx1, prev):
    if prev is None:
        dx, dg_pre = norm_bwd(tag + "_dpre", dh, x, g_pre, dx1, F32)
        return dx, dg_pre, None, None
    return norm_bwd_chain(tag + "_dpre", dh, x, g_pre, dx1, *prev)


def loss_grad(name, y, target):
    t, d = y.shape
    tr = _row_tile(t, 512)

    def body(y_ref, t_ref, dy_ref, s_ref):
        i = pl.program_id(0)
        err = y_ref[...] - t_ref[...]
        dy_ref[...] = err * (1.0 / d)
        part = jnp.sum(err * err, axis=0, keepdims=True)

        @pl.when(i == 0)
        def _():
            s_ref[...] = part

        @pl.when(i > 0)
        def _():
            s_ref[...] += part

    row = pl.BlockSpec((tr, d), lambda i: (i, 0))
    vec = pl.BlockSpec((1, d), lambda i: (0, 0))
    return pl.pallas_call(
        body, grid=(t // tr,), in_specs=[row, row], out_specs=[row, vec],
        out_shape=[_sds((t, d), F32), _sds((1, d), F32)],
        compiler_params=_params("arbitrary"), name=name)(y, target)


PAIR = 2 * HEAD_DIM
N_PAIRS = N_HEADS // 2
A_TQ = 512
A_UNROLL_FWD = 8
A_UNROLL_BWD = 4


def _halves(x):
    lane = lax.broadcasted_iota(jnp.int32, x.shape, x.ndim - 1)
    zero = jnp.zeros_like(x)
    return jnp.where(lane < HEAD_DIM, x, zero), jnp.where(lane >= HEAD_DIM, x, zero)


def _merge(a, b):
    lane = lax.broadcasted_iota(jnp.int32, a.shape, a.ndim - 1)
    return jnp.where(lane < HEAD_DIM, a, b)


def _a_valid(c):
    col = lax.broadcasted_iota(jnp.int32, (CHUNK, BAND), 1)
    return col >= (N_LEFT - c) * CHUNK


def attn_a_fwd(name, proj, kp, vp, bias, q_blk):
    t = proj.shape[0]
    tq = _row_tile(t, A_TQ)
    ncs = tq // CHUNK
    un = math.gcd(A_UNROLL_FWD, ncs)

    def body(q_ref, k_ref, v_ref, b_ref, o_ref, l_ref):
        i = pl.program_id(1)

        def group(gg, carry):
            cs = [i * ncs + gg * un + u for u in range(un)]
            r0s = [pl.multiple_of((gg * un + u) * CHUNK, CHUNK) for u in range(un)]
            k0s = [pl.multiple_of(c * CHUNK, CHUNK) for c in cs]
            ss = []
            for u in range(un):
                qh = _halves(q_ref[pl.ds(r0s[u], CHUNK), :] * QK_SCALE)
                kwin = k_ref[pl.ds(k0s[u], BAND), :]
                valid = _a_valid(cs[u])
                for hh in range(2):
                    s = lax.dot_general(qh[hh], kwin, NT, preferred_element_type=F32) + b_ref[hh]
                    ss.append(jnp.where(valid, s, -1e30))
            ps, lses = [], []
            for s in ss:
                mx = jnp.max(s, axis=-1, keepdims=True)
                p = jnp.exp(s - mx)
                den = jnp.sum(p, axis=-1, keepdims=True)
                ps.append((p * (1.0 / den)).astype(BF16))
                lses.append(mx + jnp.log(den))
            for u in range(un):
                vwin = v_ref[pl.ds(k0s[u], BAND), :]
                o0 = jnp.dot(ps[2 * u], vwin, preferred_element_type=F32)
                o1 = jnp.dot(ps[2 * u + 1], vwin, preferred_element_type=F32)
                o_ref[pl.ds(r0s[u], CHUNK), :] = _merge(o0, o1)
                l_ref[pl.ds(r0s[u], CHUNK), :] = jnp.concatenate([lses[2 * u], lses[2 * u + 1]], axis=1)
            return carry

        lax.fori_loop(0, ncs // un, group, 0)

    return pl.pallas_call(
        body, grid=(N_PAIRS, t // tq),
        in_specs=[pl.BlockSpec((tq, PAIR), lambda p, i: (i, q_blk + p)),
                  pl.BlockSpec((t + PAD_KEYS, PAIR), lambda p, i: (0, p)),
                  pl.BlockSpec((t + PAD_KEYS, PAIR), lambda p, i: (0, p)),
                  pl.BlockSpec((2, CHUNK, BAND), lambda p, i: (p, 0, 0))],
        out_specs=[pl.BlockSpec((tq, PAIR), lambda p, i: (i, p)),
                   pl.BlockSpec((None, tq, 2), lambda p, i: (p, i, 0))],
        out_shape=[_sds((t, N_PAIRS * PAIR), F32), _sds((N_PAIRS, t, 2), F32)],
        compiler_params=_params("parallel", "parallel"), name=name)(proj, kp, vp, bias)


def attn_a_bwd(name, proj, kp, vp, bias, o, lse, do, q_blk, do_blk):
    t = proj.shape[0]
    tq = _row_tile(t, A_TQ)
    ncs = tq // CHUNK
    un = math.gcd(A_UNROLL_BWD, ncs)

    def body(q_ref, k_ref, v_ref, b_ref, o_ref, l_ref, do_ref, dq_ref, dk_ref, dv_ref, db_ref):
        i = pl.program_id(1)

        @pl.when(i == 0)
        def _():
            dk_ref[...] = jnp.zeros_like(dk_ref)
            dv_ref[...] = jnp.zeros_like(dv_ref)
            db_ref[...] = jnp.zeros_like(db_ref)

        def group(gg, carry):
            cs = [i * ncs + gg * un + u for u in range(un)]
            r0s = [pl.multiple_of((gg * un + u) * CHUNK, CHUNK) for u in range(un)]
            k0s = [pl.multiple_of(c * CHUNK, CHUNK) for c in cs]
            qhs, dohs, ps, dps, deltas = [], [], [], [], []
            for u in range(un):
                rows = pl.ds(r0s[u], CHUNK)
                qh = _halves(q_ref[rows, :] * QK_SCALE)
                doh = _halves(do_ref[rows, :])
                kwin = k_ref[pl.ds(k0s[u], BAND), :]
                vwin = v_ref[pl.ds(k0s[u], BAND), :]
                valid = _a_valid(cs[u])
                dl = _halves(do_ref[rows, :].astype(F32) * o_ref[rows, :])
                for hh in range(2):
                    s = lax.dot_general(qh[hh], kwin, NT, preferred_element_type=F32) + b_ref[hh]
                    ps.append(jnp.where(valid, jnp.exp(s - l_ref[rows, hh:hh + 1]), 0.0))
                    dps.append(lax.dot_general(doh[hh], vwin, NT, preferred_element_type=F32))
                    deltas.append(jnp.sum(dl[hh], axis=-1, keepdims=True))
                qhs.append(qh)
                dohs.append(doh)
            dss = [p * (dp - dl) for p, dp, dl in zip(ps, dps, deltas)]
            for hh in range(2):
                tot = dss[hh]
                for u in range(1, un):
                    tot = tot + dss[2 * u + hh]
                db_ref[hh] += tot
            for u in range(un):
                kwin = k_ref[pl.ds(k0s[u], BAND), :]
                ds0, ds1 = dss[2 * u].astype(BF16), dss[2 * u + 1].astype(BF16)
                dq_ref[pl.ds(r0s[u], CHUNK), :] = _merge(jnp.dot(ds0, kwin, preferred_element_type=F32),
                                                         jnp.dot(ds1, kwin, preferred_element_type=F32)) * QK_SCALE
                dk_ref[pl.ds(k0s[u], BAND), :] += (lax.dot_general(ds0, qhs[u][0], TN, preferred_element_type=F32)
                                                   + lax.dot_general(ds1, qhs[u][1], TN, preferred_element_type=F32))
                dv_ref[pl.ds(k0s[u], BAND), :] += (
                    lax.dot_general(ps[2 * u].astype(BF16), dohs[u][0], TN, preferred_element_type=F32)
                    + lax.dot_general(ps[2 * u + 1].astype(BF16), dohs[u][1], TN, preferred_element_type=F32))
            return carry

        lax.fori_loop(0, ncs // un, group, 0)

    tile = lambda blk: pl.BlockSpec((tq, PAIR), lambda p, i: (i, blk + p))
    whole = pl.BlockSpec((t + PAD_KEYS, PAIR), lambda p, i: (0, p))
    bspec = pl.BlockSpec((2, CHUNK, BAND), lambda p, i: (p, 0, 0))
    return pl.pallas_call(
        body, grid=(N_PAIRS, t // tq),
        in_specs=[tile(q_blk), whole, whole, bspec, tile(0), pl.BlockSpec((None, tq, 2), lambda p, i: (p, i, 0)),
                  tile(do_blk)],
        out_specs=[tile(0), whole, whole, bspec],
        out_shape=[_sds((t, N_PAIRS * PAIR), F32), _sds((t + PAD_KEYS, N_PAIRS * PAIR), F32),
                   _sds((t + PAD_KEYS, N_PAIRS * PAIR), F32), _sds((2 * N_PAIRS, CHUNK, BAND), F32)],
        compiler_params=_params("parallel", "arbitrary"), name=name)(proj, kp, vp, bias, o, lse, do)


SB_TQ = 256
SB_TK = 256
SB_DEAD = -125.0


def _tri(n, strict):
    j = lax.broadcasted_iota(jnp.int32, (n, n), 0)
    s = lax.broadcasted_iota(jnp.int32, (n, n), 1)
    return jnp.where((j > s) if strict else (j >= s), 1.0, 0.0).astype(BF16)


def _suffix_sum(x, tri, exact):
    hi = x.astype(BF16)
    out = jnp.dot(hi, tri, preferred_element_type=F32)
    if exact:
        lo = (x - hi.astype(F32)).astype(BF16)
        out = out + jnp.dot(lo, tri, preferred_element_type=F32)
    return out


def _sb_scores(qh, ks, causal):
    z = lax.dot_general(qh, ks, NT, preferred_element_type=F32)
    lb = jnp.minimum(z, 0.0) - jnp.log(1.0 + jnp.exp(-jnp.abs(z)))
    m = lb - z
    if causal is not None:
        m = jnp.where(causal, m, 0.0)
    return lb, m


def _causal(tq, tk, off):
    return (lax.broadcasted_iota(jnp.int32, (tq, tk), 1) + off * tk) < lax.broadcasted_iota(jnp.int32, (tq, tk), 0)


def sb_fwd(name, proj, q_blk, k_blk, v_blk):
    t = proj.shape[0]
    tq = _row_tile(t, SB_TQ)
    tk = min(SB_TK, tq)
    per = tq // tk

    def body(q_ref, k_ref, v_ref, o_ref):
        i = pl.program_id(1)
        tri = _tri(tk, True)
        qh = _halves(q_ref[...] * QK_SCALE)

        def blocks(kb, carry, off):
            k0 = pl.multiple_of(kb * tk, tk)
            ks, vs = k_ref[pl.ds(k0, tk), :], v_ref[pl.ds(k0, tk), :]
            causal = None if off is None else _causal(tq, tk, off)
            lbm = [_sb_scores(qh[hh], ks, causal) for hh in range(2)]
            afters = [_suffix_sum(lbm[hh][1], tri, False) for hh in range(2)]
            out = []
            for hh in range(2):
                acc, cm = carry[2 * hh], carry[2 * hh + 1]
                w = jnp.exp(lbm[hh][0] + afters[hh] + cm)
                if causal is not None:
                    w = jnp.where(causal, w, 0.0)
                out += [acc + jnp.dot(w.astype(BF16), vs, preferred_element_type=F32),
                        cm + jnp.sum(lbm[hh][1], axis=-1, keepdims=True)]
            return tuple(out)

        def alive(carry):
            return jnp.maximum(jnp.max(carry[1]), jnp.max(carry[3])) > SB_DEAD

        carry = (jnp.zeros((tq, PAIR), F32), jnp.zeros((tq, 1), F32)) * 2
        for off in reversed(range(per)):
            carry = blocks(i * per + off, carry, off)

        def step(c):
            new = blocks(i * per - 1 - c[0], c[2:], None)
            return (c[0] + 1, alive(new)) + new

        out = lax.while_loop(lambda c: jnp.logical_and(c[0] < i * per, c[1]), step,
                             (jnp.int32(0), alive(carry)) + carry)
        o_ref[...] = _merge(out[2], out[4])

    return pl.pallas_call(
        body, grid=(N_PAIRS, t // tq),
        in_specs=[pl.BlockSpec((tq, PAIR), lambda p, i: (i, q_blk + p)),
                  pl.BlockSpec((t, PAIR), lambda p, i: (0, k_blk + p)),
                  pl.BlockSpec((t, PAIR), lambda p, i: (0, v_blk + p))],
        out_specs=pl.BlockSpec((tq, PAIR), lambda p, i: (i, p)),
        out_shape=_sds((t, N_PAIRS * PAIR), F32), compiler_params=_params("parallel", "parallel"), name=name,
    )(proj, proj, proj)


def sb_bwd(name, proj, o, do, q_blk, k_blk, v_blk, do_blk):
    t = proj.shape[0]
    tq = _row_tile(t, SB_TQ)
    tk = min(SB_TK, tq)
    per = tq // tk

    def body(q_ref, k_ref, v_ref, o_ref, do_ref, dq_ref, dk_ref, dv_ref):
        i = pl.program_id(1)

        @pl.when(i == 0)
        def _():
            dk_ref[...] = jnp.zeros_like(dk_ref)
            dv_ref[...] = jnp.zeros_like(dv_ref)

        tri_s, tri_i = _tri(tk, True), _tri(tk, False)
        qh = _halves(q_ref[...] * QK_SCALE)
        doh = _halves(do_ref[...])
        deltas = [jnp.sum(x, axis=-1, keepdims=True) for x in _halves(do_ref[...].astype(F32) * o_ref[...])]

        def blocks(kb, carry, off):
            k0 = pl.multiple_of(kb * tk, tk)
            ks, vs = k_ref[pl.ds(k0, tk), :], v_ref[pl.ds(k0, tk), :]
            causal = None if off is None else _causal(tq, tk, off)
            lbm = [_sb_scores(qh[hh], ks, causal) for hh in range(2)]
            dws = [lax.dot_general(doh[hh], vs, NT, preferred_element_type=F32) for hh in range(2)]
            afters = [_suffix_sum(lbm[hh][1], tri_s, False) for hh in range(2)]
            wbs, es = [], []
            for hh in range(2):
                w = jnp.exp(lbm[hh][0] + afters[hh] + carry[3 * hh + 1])
                if causal is not None:
                    w = jnp.where(causal, w, 0.0)
                wbs.append(w.astype(BF16))
                es.append(wbs[hh].astype(F32) * dws[hh])
            sfx = [_suffix_sum(es[hh], tri_i, True) for hh in range(2)]
            dzs = []
            for hh in range(2):
                left = deltas[hh] - (sfx[hh] + carry[3 * hh + 2])
                sig = jnp.exp(lbm[hh][0])
                dz = es[hh] * (1.0 - sig) - left * sig
                if causal is not None:
                    dz = jnp.where(causal, dz, 0.0)
                dzs.append(dz.astype(BF16))
            dk_ref[pl.ds(k0, tk), :] += (lax.dot_general(dzs[0], qh[0], TN, preferred_element_type=F32)
                                         + lax.dot_general(dzs[1], qh[1], TN, preferred_element_type=F32))
            dv_ref[pl.ds(k0, tk), :] += (lax.dot_general(wbs[0], doh[0], TN, preferred_element_type=F32)
                                         + lax.dot_general(wbs[1], doh[1], TN, preferred_element_type=F32))
            out = []
            for hh in range(2):
                out += [carry[3 * hh] + jnp.dot(dzs[hh], ks, preferred_element_type=F32),
                        carry[3 * hh + 1] + jnp.sum(lbm[hh][1], axis=-1, keepdims=True),
                        carry[3 * hh + 2] + jnp.sum(es[hh], axis=-1, keepdims=True)]
            return tuple(out)

        def alive(carry):
            return jnp.maximum(jnp.max(carry[1]), jnp.max(carry[4])) > SB_DEAD

        zero = jnp.zeros((tq, 1), F32)
        carry = (jnp.zeros((tq, PAIR), F32), zero, zero) * 2
        for off in reversed(range(per)):
            carry = blocks(i * per + off, carry, off)

        def step(c):
            new = blocks(i * per - 1 - c[0], c[2:], None)
            return (c[0] + 1, alive(new)) + new

        out = lax.while_loop(lambda c: jnp.logical_and(c[0] < i * per, c[1]), step,
                             (jnp.int32(0), alive(carry)) + carry)
        dq_ref[...] = _merge(out[2], out[5]) * QK_SCALE

    tile = lambda blk: pl.BlockSpec((tq, PAIR), lambda p, i: (i, blk + p))
    whole = lambda blk: pl.BlockSpec((t, PAIR), lambda p, i: (0, blk + p))
    return pl.pallas_call(
        body, grid=(N_PAIRS, t // tq),
        in_specs=[tile(q_blk), whole(k_blk), whole(v_blk), tile(0), tile(do_blk)],
        out_specs=[tile(0), whole(0), whole(0)],
        out_shape=[_sds((t, N_PAIRS * PAIR), F32)] * 3,
        compiler_params=_params("parallel", "arbitrary"), name=name)(proj, proj, proj, o, do)


def _sigmoid(x):
    return 1.0 / (1.0 + jnp.exp(-x))


def gu_gap(f8):
    return -(-f8 // 128) * 128


def merge_gu(gate, up):
    f8 = gate.shape[-1]
    pad = jnp.zeros(gate.shape[:-1] + (gu_gap(f8) - f8,), gate.dtype)
    return jnp.concatenate([gate, pad, up], axis=-1)


def split_gu(gu, f8):
    return gu[..., :f8], gu[..., gu_gap(f8):]


def ffn_up(name, h, wgu, f8, layer):
    t, d = h.shape
    nb, _, _, fw = wgu.shape
    gap = gu_gap(f8)
    tm = _row_tile(t, 1024)

    per = 2

    def body(h_ref, w_ref, gu_ref, a_ref):
        hv = h_ref[...]
        rs = [jnp.dot(hv, w_ref[j], preferred_element_type=F32) for j in range(per)]
        for j in range(per):
            gu_ref[j] = rs[j].astype(BF16)
            g, u = rs[j][:, :f8], rs[j][:, gap:]
            a_ref[j] = (g * _sigmoid(g) * u).astype(BF16)

    return pl.pallas_call(
        body, grid=(t // tm, nb // per),
        in_specs=[pl.BlockSpec((tm, d), lambda i, k: (i, 0)),
                  pl.BlockSpec((per, None, d, fw), lambda i, k: (k, layer, 0, 0))],
        out_specs=[pl.BlockSpec((per, tm, fw), lambda i, k: (k, i, 0)),
                   pl.BlockSpec((per, tm, f8), lambda i, k: (k, i, 0))],
        out_shape=[_sds((nb, t, fw), BF16), _sds((nb, t, f8), BF16)],
        compiler_params=_params("parallel", "parallel"), name=name)(h, wgu)


def _mm_all_blocks(name, a, w, layer, dn, tm):
    nb, t, f = a.shape
    wshape = w.shape[2:]
    d = wshape[1] if dn == NN else wshape[0]
    tm = _row_tile(t, tm)

    def body(a_ref, w_ref, o_ref):
        acc = lax.dot_general(a_ref[0], w_ref[0], dn, preferred_element_type=F32)
        for k in range(1, nb):
            acc = acc + lax.dot_general(a_ref[k], w_ref[k], dn, preferred_element_type=F32)
        o_ref[...] = acc

    return pl.pallas_call(
        body, grid=(t // tm,),
        in_specs=[pl.BlockSpec((nb, tm, f), lambda i: (0, i, 0)),
                  pl.BlockSpec((nb, None) + wshape, lambda i: (0, layer, 0, 0))],
        out_specs=pl.BlockSpec((tm, d), lambda i: (i, 0)), out_shape=_sds((t, d), F32),
        compiler_params=_params("parallel"), name=name)(a, w)


def ffn_down(name, a, wd, layer):
    return _mm_all_blocks(name, a, wd, layer, NN, 512)


def ffn_bwd_act(name, dm, wd, gu, f8, layer):
    t, d = dm.shape
    nb, _, fw = gu.shape
    gap = gu_gap(f8)
    tm = _row_tile(t, 1024)

    per = 2

    def body(dm_ref, wd_ref, gu_ref, o_ref):
        dmv = dm_ref[...]
        das = [lax.dot_general(dmv, wd_ref[j], NT, preferred_element_type=F32) for j in range(per)]
        o_ref[...] = jnp.zeros_like(o_ref)
        for j in range(per):
            gv = gu_ref[j, :, :f8].astype(F32)
            uv = gu_ref[j, :, gap:].astype(F32)
            sg = _sigmoid(gv)
            o_ref[j, :, :f8] = (das[j] * uv * sg * (1.0 + gv * (1.0 - sg))).astype(BF16)
            o_ref[j, :, gap:] = (das[j] * gv * sg).astype(BF16)

    bspec = pl.BlockSpec((per, tm, fw), lambda i, k: (k, i, 0))
    return pl.pallas_call(
        body, grid=(t // tm, nb // per),
        in_specs=[pl.BlockSpec((tm, d), lambda i, k: (i, 0)),
                  pl.BlockSpec((per, None, f8, d), lambda i, k: (k, layer, 0, 0)), bspec],
        out_specs=bspec, out_shape=_sds((nb, t, fw), BF16),
        compiler_params=_params("parallel", "parallel"), name=name)(dm, wd, gu)


def ffn_bwd_dh(name, dgu, wgu, layer):
    return _mm_all_blocks(name, dgu, wgu, layer, NT, 512)


def ffn_dw_in(name, h, dact):
    t, d = h.shape
    nb, _, f8 = dact.shape
    tk = _row_tile(t, 2048)
    return _gmm(name, h, dact, grid=(nb, t // tk),
                a_blk=(tk, d), a_idx=lambda b, s: (s, 0), b_blk=(None, tk, f8), b_idx=lambda b, s: (b, s, 0),
                o_blk=(None, d, f8), o_idx=lambda b, s: (b, 0, 0), out_shape=(nb, d, f8), out_dtype=F32, dn=TN,
                acc_shape=(d, f8))


def ffn_dw_down(name, a, dm):
    nb, t, f8 = a.shape
    d = dm.shape[1]
    tk = _row_tile(t, 2048)
    return _gmm(name, a, dm, grid=(nb, t // tk),
                a_blk=(None, tk, f8), a_idx=lambda b, s: (b, s, 0), b_blk=(tk, d), b_idx=lambda b, s: (s, 0),
                o_blk=(None, f8, d), o_idx=lambda b, s: (b, 0, 0), out_shape=(nb, f8, d), out_dtype=F32, dn=TN,
                acc_shape=(f8, d))


GELU_C = math.sqrt(2.0 / math.pi)
GELU_A = 0.044715


def _gelu(x):
    return 0.5 * x * (1.0 + jnp.tanh(GELU_C * (x + GELU_A * x * x * x)))


def _gelu_grad(x):
    th = jnp.tanh(GELU_C * (x + GELU_A * x * x * x))
    return 0.5 * (1.0 + th) + 0.5 * x * (1.0 - th * th) * GELU_C * (1.0 + 3.0 * GELU_A * x * x)


def _neg_expm1(x):
    series = x * (1.0 + x * (0.5 + x * (1.0 / 6.0 + x * (1.0 / 24.0 + x * (1.0 / 120.0 + x * (1.0 / 720.0))))))
    return -jnp.where(x > -0.25, series, jnp.exp(x) - 1.0)


CONV_TR = 256
CONV_TAPS = 4
HALO = 8


def _shifted(ext, k, tr, back):
    if back:
        return pltpu.roll(ext, k, 0)[HALO:, :] if k else ext[HALO:, :]
    return pltpu.roll(ext, tr + HALO - k, 0)[:tr, :] if k else ext[:tr, :]


def conv4_fwd(name, src, cb, w, b):
    t, c = src.shape[0], w.shape[1]
    tr = _row_tile(t, CONV_TR)
    hb = tr // HALO

    def body(x_ref, h_ref, w_ref, b_ref, o_ref):
        i = pl.program_id(0)
        ext = jnp.concatenate([jnp.where(i == 0, 0.0, h_ref[...]), x_ref[...]], axis=0)
        acc = b_ref[...]
        for k in range(CONV_TAPS):
            acc = acc + w_ref[CONV_TAPS - 1 - k:CONV_TAPS - k, :] * _shifted(ext, k, tr, True)
        o_ref[...] = acc

    return pl.pallas_call(
        body, grid=(t // tr,),
        in_specs=[pl.BlockSpec((tr, c), lambda i: (i, cb)),
                  pl.BlockSpec((HALO, c), lambda i: (jnp.maximum(i * hb - 1, 0), cb)),
                  pl.BlockSpec((CONV_TAPS, c), lambda i: (0, 0)), pl.BlockSpec((1, c), lambda i: (0, 0))],
        out_specs=pl.BlockSpec((tr, c), lambda i: (i, 0)), out_shape=_sds((t, c), F32),
        compiler_params=_params("parallel"), name=name)(src, src, w, b)


def conv4_bwd_x(name, dy, w):
    t, c = dy.shape
    tr = _row_tile(t, CONV_TR)
    hb = tr // HALO
    last = t // tr - 1

    def body(y_ref, h_ref, w_ref, o_ref):
        i = pl.program_id(0)
        ext = jnp.concatenate([y_ref[...], jnp.where(i == last, 0.0, h_ref[...])], axis=0)
        acc = w_ref[CONV_TAPS - 1:CONV_TAPS, :] * y_ref[...]
        for k in range(1, CONV_TAPS):
            acc = acc + w_ref[CONV_TAPS - 1 - k:CONV_TAPS - k, :] * _shifted(ext, k, tr, False)
        o_ref[...] = acc

    return pl.pallas_call(
        body, grid=(t // tr,),
        in_specs=[pl.BlockSpec((tr, c), lambda i: (i, 0)),
                  pl.BlockSpec((HALO, c), lambda i: (jnp.minimum((i + 1) * hb, t // HALO - 1), 0)),
                  pl.BlockSpec((CONV_TAPS, c), lambda i: (0, 0))],
        out_specs=pl.BlockSpec((tr, c), lambda i: (i, 0)), out_shape=_sds((t, c), F32),
        compiler_params=_params("parallel"), name=name)(dy, dy, w)


def conv4_bwd_w(name, src, cb, dy):
    t, c = dy.shape
    tr = _row_tile(t, CONV_TR)
    hb = tr // HALO

    def body(x_ref, h_ref, dy_ref, dw_ref, db_ref):
        i = pl.program_id(0)

        @pl.when(i == 0)
        def _():
            dw_ref[...] = jnp.zeros_like(dw_ref)
            db_ref[...] = jnp.zeros_like(db_ref)

        ext = jnp.concatenate([jnp.where(i == 0, 0.0, h_ref[...]), x_ref[...]], axis=0)
        dyv = dy_ref[...]
        db_ref[...] += jnp.sum(dyv, axis=0, keepdims=True)
        for k in range(CONV_TAPS):
            dw_ref[CONV_TAPS - 1 - k:CONV_TAPS - k, :] += jnp.sum(dyv * _shifted(ext, k, tr, True), axis=0,
                                                                   keepdims=True)

    return pl.pallas_call(
        body, grid=(t // tr,),
        in_specs=[pl.BlockSpec((tr, c), lambda i: (i, cb)),
                  pl.BlockSpec((HALO, c), lambda i: (jnp.maximum(i * hb - 1, 0), cb)),
                  pl.BlockSpec((tr, c), lambda i: (i, 0))],
        out_specs=[pl.BlockSpec((CONV_TAPS, c), lambda i: (0, 0)), pl.BlockSpec((1, c), lambda i: (0, 0))],
        out_shape=[_sds((CONV_TAPS, c), F32), _sds((1, c), F32)],
        compiler_params=_params("arbitrary"), name=name)(src, src, dy)


def _rg_gate_values(xcv, wa_ref, wi_ref, ba_ref, bi_ref, lam_ref):
    xb = xcv.astype(BF16)
    r = _sigmoid(jnp.dot(xb, wa_ref[...], preferred_element_type=F32) + ba_ref[...])
    ig = _sigmoid(jnp.dot(xb, wi_ref[...], preferred_element_type=F32) + bi_ref[...])
    lam = lam_ref[...]
    sp = jnp.maximum(-lam, 0.0) + jnp.log(1.0 + jnp.exp(-jnp.abs(lam)))
    log_a = -LRU_C * r * sp
    a = jnp.exp(log_a)
    mult = jnp.sqrt(_neg_expm1(2.0 * log_a))
    return xb, r, ig, sp, a, mult


def rg_gates_fwd(name, xc, wa, wi, ba, bi, lam):
    t, c = xc.shape
    nb, cb, _ = wa.shape
    tm = _row_tile(t, 512)

    def body(xc_ref, wa_ref, wi_ref, ba_ref, bi_ref, lam_ref, a_ref, u_ref):
        xcv = xc_ref[...]
        _, _, ig, _, a, mult = _rg_gate_values(xcv, wa_ref, wi_ref, ba_ref, bi_ref, lam_ref)
        a_ref[...] = a
        u_ref[...] = mult * (ig * xcv)

    blk = pl.BlockSpec((tm, cb), lambda i, n: (i, n))
    wsp = pl.BlockSpec((None, cb, cb), lambda i, n: (n, 0, 0))
    vec = pl.BlockSpec((1, cb), lambda i, n: (0, n))
    return pl.pallas_call(
        body, grid=(t // tm, nb), in_specs=[blk, wsp, wsp, vec, vec, vec], out_specs=[blk, blk],
        out_shape=[_sds((t, c), F32)] * 2, compiler_params=_params("parallel", "parallel"), name=name,
    )(xc, wa, wi, ba, bi, lam)


def rg_gates_bwd(name, xc, gu, hs, wa, wi, ba, bi, lam):
    t, c = xc.shape
    nb, cb, _ = wa.shape
    tm = _row_tile(t, 512)
    hb = tm // HALO

    def body(xc_ref, gu_ref, h_ref, halo_ref, wa_ref, wi_ref, ba_ref, bi_ref, lam_ref,
             dxc_ref, dwa_ref, dwi_ref, dba_ref, dbi_ref, dlam_ref):
        i = pl.program_id(1)
        hprev = _shifted(jnp.concatenate([jnp.where(i == 0, 0.0, halo_ref[...]), h_ref[...]], axis=0), 1, tm, True)

        @pl.when(i == 0)
        def _():
            for ref in (dwa_ref, dwi_ref, dba_ref, dbi_ref, dlam_ref):
                ref[...] = jnp.zeros_like(ref)

        xcv = xc_ref[...]
        xb, r, ig, sp, a, mult = _rg_gate_values(xcv, wa_ref, wi_ref, ba_ref, bi_ref, lam_ref)
        gv = gu_ref[...]
        d_ixc = gv * mult
        d_i = d_ixc * xcv
        d_mult = gv * ig * xcv
        d_a = gv * hprev - d_mult * a / mult
        d_log_a = d_a * a
        d_r = d_log_a * (-LRU_C * sp)
        sig_neg_lam = 1.0 / (1.0 + jnp.exp(lam_ref[...]))
        dlam_ref[...] += jnp.sum(d_log_a * r, axis=0, keepdims=True) * (LRU_C * sig_neg_lam)
        dpa = d_r * r * (1.0 - r)
        dpi = d_i * ig * (1.0 - ig)
        dba_ref[...] += jnp.sum(dpa, axis=0, keepdims=True)
        dbi_ref[...] += jnp.sum(dpi, axis=0, keepdims=True)
        dpab, dpib = dpa.astype(BF16), dpi.astype(BF16)
        dxc_ref[...] = (d_ixc * ig + lax.dot_general(dpab, wa_ref[...], NT, preferred_element_type=F32)
                        + lax.dot_general(dpib, wi_ref[...], NT, preferred_element_type=F32))
        dwa_ref[...] += lax.dot_general(xb, dpab, TN, preferred_element_type=F32)
        dwi_ref[...] += lax.dot_general(xb, dpib, TN, preferred_element_type=F32)

    blk = pl.BlockSpec((tm, cb), lambda n, i: (i, n))
    wsp = pl.BlockSpec((None, cb, cb), lambda n, i: (n, 0, 0))
    vec = pl.BlockSpec((1, cb), lambda n, i: (0, n))
    halo = pl.BlockSpec((HALO, cb), lambda n, i: (jnp.maximum(i * hb - 1, 0), n))
    return pl.pallas_call(
        body, grid=(nb, t // tm), in_specs=[blk, blk, blk, halo, wsp, wsp, vec, vec, vec],
        out_specs=[blk, wsp, wsp, vec, vec, vec],
        out_shape=[_sds((t, c), F32), _sds((nb, cb, cb), F32), _sds((nb, cb, cb), F32),
                   _sds((1, c), F32), _sds((1, c), F32), _sds((1, c), F32)],
        compiler_params=_params("parallel", "arbitrary"), name=name)(xc, gu, hs, hs, wa, wi, ba, bi, lam)


SCAN_TS = 256
SCAN_TC = 512


def _tile_scan(a, b, reverse):
    ts = a.shape[0]
    row = lax.broadcasted_iota(jnp.int32, a.shape, 0)
    d = 1
    while d < ts:
        if reverse:
            inside = row < ts - d
            a_sh = jnp.where(inside, pltpu.roll(a, ts - d, 0), 1.0)
            b_sh = jnp.where(inside, pltpu.roll(b, ts - d, 0), 0.0)
        else:
            inside = row >= d
            a_sh = jnp.where(inside, pltpu.roll(a, d, 0), 1.0)
            b_sh = jnp.where(inside, pltpu.roll(b, d, 0), 0.0)
        b = b + a * b_sh
        a = a * a_sh
        d *= 2
    return a, b


def rg_scan_fwd(name, a, u, gate_pre):
    t, c = a.shape
    ts, tc = _row_tile(t, SCAN_TS), _row_tile(c, SCAN_TC)

    def body(a_ref, u_ref, g_ref, h_ref, z_ref, carry_ref):
        s = pl.program_id(1)

        @pl.when(s == 0)
        def _():
            carry_ref[...] = jnp.zeros_like(carry_ref)

        ac, bc = _tile_scan(a_ref[...], u_ref[...], False)
        h = bc + ac * carry_ref[0:1, :]
        h_ref[...] = h
        z_ref[...] = (h * _gelu(g_ref[...])).astype(BF16)
        carry_ref[0:1, :] = h[ts - 1:ts, :]

    blk = pl.BlockSpec((ts, tc), lambda j, s: (s, j))
    return pl.pallas_call(
        body, grid=(c // tc, t // ts), in_specs=[blk, blk, blk], out_specs=[blk, blk],
        out_shape=[_sds((t, c), F32), _sds((t, c), BF16)], scratch_shapes=[pltpu.VMEM((8, tc), F32)],
        compiler_params=_params("parallel", "arbitrary"), name=name)(a, u, gate_pre)


def rg_scan_bwd(name, a, hs, gate_pre, dz):
    t, c = hs.shape
    ts, tc = _row_tile(t, SCAN_TS), _row_tile(c, SCAN_TC)
    nt = t // ts
    hb = ts // HALO

    def body(a_ref, halo_ref, h_ref, g_ref, dz_ref, gu_ref, dgate_ref, carry_ref):
        s = pl.program_id(1)

        @pl.when(s == 0)
        def _():
            carry_ref[...] = jnp.zeros_like(carry_ref)

        a_next = _shifted(jnp.concatenate([a_ref[...], jnp.where(s == 0, 0.0, halo_ref[...])], axis=0), 1, ts, False)
        gate = g_ref[...]
        dzv = dz_ref[...]
        dgate_ref[...] = (dzv * h_ref[...] * _gelu_grad(gate)).astype(BF16)
        ac, bc = _tile_scan(a_next, dzv * _gelu(gate), True)
        gu = bc + ac * carry_ref[0:1, :]
        gu_ref[...] = gu
        carry_ref[0:1, :] = gu[0:1, :]

    blk = pl.BlockSpec((ts, tc), lambda j, s: (nt - 1 - s, j))
    halo = pl.BlockSpec((HALO, tc), lambda j, s: (jnp.minimum((nt - s) * hb, t // HALO - 1), j))
    return pl.pallas_call(
        body, grid=(c // tc, nt), in_specs=[blk, halo, blk, blk, blk], out_specs=[blk, blk],
        out_shape=[_sds((t, c), F32), _sds((t, c), BF16)], scratch_shapes=[pltpu.VMEM((8, tc), F32)],
        compiler_params=_params("parallel", "arbitrary"), name=name)(a, a, hs, gate_pre, dz)


QA_BLK, KA_BLK, VA_BLK, QS_BLK, KS_BLK, VS_BLK = (g * N_PAIRS for g in range(6))


TOEP_W = 640
TOEP_FLAT = 320
TABLE_LOW = 193


def rel_bias_matrix(name, table):
    h = table.shape[0]
    diag = jnp.concatenate([jnp.repeat(table[:, 2 * REL_CLIP:], TOEP_FLAT, axis=1),
                            jnp.flip(table[:, TABLE_LOW:2 * REL_CLIP], axis=1),
                            jnp.zeros((h, 1), table.dtype)], axis=1)[:, None, :]

    def body(v_ref, o_ref):
        rows = jnp.broadcast_to(v_ref[...], (CHUNK, TOEP_W))
        o_ref[...] = pltpu.roll(rows, TOEP_W - (CHUNK - 1), 1, stride=1, stride_axis=0)

    out = pl.pallas_call(
        body, grid=(h,), in_specs=[pl.BlockSpec((None, 1, TOEP_W), lambda hh: (hh, 0, 0))],
        out_specs=pl.BlockSpec((None, CHUNK, TOEP_W), lambda hh: (hh, 0, 0)),
        out_shape=_sds((h, CHUNK, TOEP_W), F32), compiler_params=_params("parallel"), name=name)(diag)
    return out[:, :, :BAND]


def rel_bias_grad(name, dbias):
    h = dbias.shape[0]
    flipped = jnp.pad(jnp.flip(dbias, axis=1), ((0, 0), (0, 0), (0, TOEP_W - BAND)))

    def body(x_ref, o_ref):
        skew = pltpu.roll(x_ref[...], 0, 1, stride=1, stride_axis=0)
        col = jnp.sum(skew, axis=0, keepdims=True)
        lane = lax.broadcasted_iota(jnp.int32, col.shape, 1)
        flat = jnp.sum(jnp.where(lane < TOEP_FLAT, col, 0.0), axis=1, keepdims=True)
        o_ref[...] = jnp.where(lane == TOEP_W - 1, flat, col)

    out = pl.pallas_call(
        body, grid=(h,), in_specs=[pl.BlockSpec((None, CHUNK, TOEP_W), lambda hh: (hh, 0, 0))],
        out_specs=pl.BlockSpec((None, 1, TOEP_W), lambda hh: (hh, 0, 0)),
        out_shape=_sds((h, 1, TOEP_W), F32), compiler_params=_params("parallel"), name=name)(flipped)[:, 0, :]
    return jnp.concatenate([jnp.zeros((h, TABLE_LOW), F32), jnp.flip(out[:, TOEP_FLAT:TOEP_W - 1], axis=1),
                            out[:, TOEP_W - 1:]], axis=1)


def attn_layer_fwd(tag, x, h, w, g_next):
    proj = mm_nn_wblk(tag + "_proj", h, w["w_in"], w["idx"], BF16)
    width = N_PAIRS * PAIR
    pad = lambda a: jnp.pad(a, ((PAD_KEYS, 0), (0, 0)))
    kap, vap = pad(proj[:, width:2 * width]), pad(proj[:, 2 * width:3 * width])
    bias = rel_bias_matrix(tag + "_bias", w["rel_bias"])
    oa, lse = attn_a_fwd(tag + "_a", proj, kap, vap, bias, QA_BLK)
    ob = sb_fwd(tag + "_sb", proj, QS_BLK, KS_BLK, VS_BLK)
    o = jnp.concatenate([oa, ob], axis=1).astype(BF16)
    m = mm_nn(tag + "_out", o, w["w_out"], F32)
    x1, h_next = resid_norm_fwd(tag + "_res", x, m, w["g_post"], g_next)
    return x1, h_next, (x, h, proj, kap, vap, bias, oa, lse, ob, o, m)


def attn_layer_bwd(tag, dm, dx1, saved, w, prev):
    x, h, proj, kap, vap, bias, oa, lse, ob, o, m = saved
    d_w_out = mm_tn(tag + "_dwout", o, dm, F32)
    do = mm_nt(tag + "_do", dm, w["w_out"], BF16)
    dqa, dkap, dvap, dbias = attn_a_bwd(tag + "_da", proj, kap, vap, bias, oa, lse, do, QA_BLK, 0)
    dqs, dks, dvs = sb_bwd(tag + "_dsb", proj, ob, do, QS_BLK, KS_BLK, VS_BLK, N_PAIRS)
    d_rel = rel_bias_grad(tag + "_dbias", dbias)
    dproj = jnp.concatenate([dqa, dkap[PAD_KEYS:], dvap[PAD_KEYS:], dqs, dks, dvs], axis=1).astype(BF16)
    d_w_in = mm_tn_oblk(tag + "_dwin", h, dproj, w["w_in"].shape[3], F32)
    dh = mm_nt_wblk(tag + "_dh", dproj, w["w_in"], w["idx"], F32)
    dx, dg_pre, dm_prev, dg_post_prev = close_bwd(tag, dh, x, w["g_pre"], dx1, prev)
    return dx, dm_prev, dg_post_prev, dict(w_in=d_w_in, w_out=d_w_out, rel_bias=d_rel, g_pre=dg_pre)


def rg_layer_fwd(tag, x, h, w, g_next):
    proj = mm_nn_wblk(tag + "_proj", h, w["w_in"], w["idx"], F32)
    xc = conv4_fwd(tag + "_conv", proj, 1, w["conv_w"], w["conv_b"])
    a, u = rg_gates_fwd(tag + "_gates", xc, w["w_a"], w["w_i"], w["b_a"], w["b_i"], w["lam"])
    hs, z = rg_scan_fwd(tag + "_scan", a, u, proj)
    m = mm_nn(tag + "_out", z, w["w_out"], F32)
    x1, h_next = resid_norm_fwd(tag + "_res", x, m, w["g_post"], g_next)
    return x1, h_next, (x, h, proj, xc, a, hs, z, m)


def rg_layer_bwd(tag, dm, dx1, saved, w, prev):
    x, h, proj, xc, a, hs, z, m = saved
    d_w_out = mm_tn(tag + "_dwout", z, dm, F32)
    dz = mm_nt(tag + "_dz", dm, w["w_out"], F32)
    gu, dgate = rg_scan_bwd(tag + "_dscan", a, hs, proj, dz)
    dxc, d_w_a, d_w_i, d_b_a, d_b_i, d_lam = rg_gates_bwd(
        tag + "_dgates", xc, gu, hs, w["w_a"], w["w_i"], w["b_a"], w["b_i"], w["lam"])
    d_conv_w, d_conv_b = conv4_bwd_w(tag + "_dconvw", proj, 1, dxc)
    dxr = conv4_bwd_x(tag + "_dconv", dxc, w["conv_w"])
    dproj = jnp.concatenate([dgate, dxr.astype(BF16)], axis=1)
    d_w_in = mm_tn_oblk(tag + "_dwin", h, dproj, w["w_in"].shape[3], F32)
    dh = mm_nt_wblk(tag + "_dh", dproj, w["w_in"], w["idx"], F32)
    dx, dg_pre, dm_prev, dg_post_prev = close_bwd(tag, dh, x, w["g_pre"], dx1, prev)
    return dx, dm_prev, dg_post_prev, dict(w_in=d_w_in, w_out=d_w_out, conv_w=d_conv_w, conv_b=d_conv_b, w_a=d_w_a,
                                           w_i=d_w_i, b_a=d_b_a, b_i=d_b_i, lam=d_lam, g_pre=dg_pre)


def ffn_layer_fwd(tag, x, h, w, g_next):
    f8 = w["w_down"].shape[2]
    gu, a = ffn_up(tag + "_up", h, w["w_gu"], f8, w["idx"])
    f = ffn_down(tag + "_down", a, w["w_down"], w["idx"])
    x1, h_next = resid_norm_fwd(tag + "_res", x, f, w["g_post"], g_next)
    return x1, h_next, (x, h, gu, a, f)


def ffn_layer_bwd(tag, dm, dx1, saved, w, prev):
    x, h, gu, a, f = saved
    f8 = w["w_down"].shape[2]
    d_w_down = ffn_dw_down(tag + "_dwdown", a, dm)
    dgu = ffn_bwd_act(tag + "_dact", dm, w["w_down"], gu, f8, w["idx"])
    d_w_gu = ffn_dw_in(tag + "_dwgu", h, dgu)
    dh = ffn_bwd_dh(tag + "_dh", dgu, w["w_gu"], w["idx"])
    dx, dg_pre, dm_prev, dg_post_prev = close_bwd(tag, dh, x, w["g_pre"], dx1, prev)
    return dx, dm_prev, dg_post_prev, dict(w_gu=d_w_gu, w_down=d_w_down, g_pre=dg_pre)


def _place():
    return lax.axis_index("x"), lax.axis_index("y"), lax.axis_index("c")


def all_gather(name, blks):
    n = len(blks)

    def body(*refs):
        x_refs, out_refs = refs[:n], refs[n:2 * n]
        send_sems, recv_sems, local_sems = refs[2 * n:]
        x, y, cc = _place()
        me, sibling = (x, y, cc), (x, y, 1 - cc)
        chips = [(1 - x, y), (x, 1 - y), (1 - x, 1 - y)]
        south = cc == 0
        via = (jnp.where(south, 1 - x, x), jnp.where(south, y, 1 - y))
        onward = (jnp.where(south, x, 1 - x), jnp.where(south, 1 - y, y))
        k_via, k_onward = 1 + cc, 2 - cc

        def slot(a, px, py, pc):
            return out_refs[a].at[4 * px + 2 * py + pc]

        def copy(a, k, block, to, src=None):
            return pltpu.make_async_remote_copy(
                src_ref=slot(a, *block) if src is None else src, dst_ref=slot(a, *block),
                send_sem=send_sems.at[7 * a + k], recv_sem=recv_sems.at[7 * a + k], device_id=to, device_id_type=MESH)

        mine = [pltpu.make_async_copy(x_refs[a], slot(a, *me), local_sems.at[a]) for a in range(n)]
        sends = []
        for a in range(n):
            mine[a].start()
            sends.append(copy(a, 0, me, sibling, src=x_refs[a]))
            sends += [copy(a, 1 + j, me, (*chips[j], cc), src=x_refs[a]) for j in range(2)]
        for cp in sends:
            cp.start()
        for a in range(n):
            copy(a, k_via, (*via, cc), me).wait_recv()
            sends.append(copy(a, 3, (*via, cc), (*onward, cc)))
            sends.append(copy(a, 3 + k_via, (*via, cc), sibling))
            sends[-2].start()
            sends[-1].start()
        for a in range(n):
            copy(a, k_onward, (*onward, cc), me).wait_recv()
            sends.append(copy(a, 3 + k_onward, (*onward, cc), sibling))
            sends[-1].start()
        for a in range(n):
            copy(a, 3, (*chips[2], cc), me).wait_recv()
            sends.append(copy(a, 6, (*chips[2], cc), sibling))
            sends[-1].start()
        for a in range(n):
            copy(a, 0, sibling, me).wait_recv()
            for j, chip in enumerate(chips):
                copy(a, 4 + j, (*chip, 1 - cc), me).wait_recv()
        for cp in sends:
            cp.wait_send()
        for cp in mine:
            cp.wait()

    return pl.pallas_call(
        body, out_shape=[_sds((N_DEV,) + b.shape, b.dtype) for b in blks], in_specs=[ANY] * n, out_specs=[ANY] * n,
        scratch_shapes=[pltpu.SemaphoreType.DMA((7 * n,)), pltpu.SemaphoreType.DMA((7 * n,)),
                        pltpu.SemaphoreType.DMA((n,))],
        name=name)(*blks)


def exchange_pair(name, gs):
    n = len(gs)
    nchip = 4

    def body(*refs):
        g_refs, land_refs = refs[:n], refs[n:2 * n]
        send_sems, recv_sems = refs[2 * n:]
        x, y, cc = _place()
        copies = [pltpu.make_async_remote_copy(
            src_ref=g_refs[a].at[j, 1 - cc], dst_ref=land_refs[a].at[j], send_sem=send_sems.at[nchip * a + j],
            recv_sem=recv_sems.at[nchip * a + j], device_id=(x, y, 1 - cc), device_id_type=MESH)
            for a in range(n) for j in range(nchip)]
        for cp in copies:
            cp.start()
        for cp in copies:
            cp.wait()

    return pl.pallas_call(
        body, out_shape=[_sds((nchip,) + g.shape[2:], g.dtype) for g in gs], in_specs=[ANY] * n, out_specs=[ANY] * n,
        scratch_shapes=[pltpu.SemaphoreType.DMA((nchip * n,)), pltpu.SemaphoreType.DMA((nchip * n,))],
        name=name)(*gs)


def pair_sum(name, g, land, core, out_dtype):
    nchip, _, r, c = g.shape
    tr = _divisor_tile(r, 1024, 16)

    def body(core_ref, g_ref, l_ref, o_ref):
        o_ref[...] = (g_ref[...] + l_ref[...]).astype(o_ref.dtype)

    return pl.pallas_call(
        body,
        grid_spec=pltpu.PrefetchScalarGridSpec(
            num_scalar_prefetch=1, grid=(nchip, r // tr),
            in_specs=[pl.BlockSpec((None, None, tr, c), lambda j, i, core_ref: (j, core_ref[0], i, 0)),
                      pl.BlockSpec((None, tr, c), lambda j, i, core_ref: (j, i, 0))],
            out_specs=pl.BlockSpec((None, tr, c), lambda j, i, core_ref: (j, i, 0))),
        out_shape=_sds((nchip, r, c), out_dtype), compiler_params=_params("parallel", "parallel"), name=name,
    )(core, g, land)


def exchange_chips(name, ps):
    n = len(ps)

    def body(*refs):
        p_refs, land_refs = refs[:n], refs[n:2 * n]
        send_sems, recv_sems, local_sems = refs[2 * n:]
        x, y, cc = _place()
        mine = 2 * x + y
        chips = [(1 - x, y), (x, 1 - y), (1 - x, 1 - y)]
        own = [pltpu.make_async_copy(p_refs[a].at[mine], land_refs[a].at[mine], local_sems.at[a]) for a in range(n)]
        for cp in own:
            cp.start()
        sends = [pltpu.make_async_remote_copy(
            src_ref=p_refs[a].at[2 * px + py], dst_ref=land_refs[a].at[mine], send_sem=send_sems.at[3 * a + k],
            recv_sem=recv_sems.at[3 * a + k], device_id=(px, py, cc), device_id_type=MESH)
            for a in range(n) for k, (px, py) in enumerate(chips)]
        for cp in sends:
            cp.start()
        for a in range(n):
            for k, (px, py) in enumerate(chips):
                pltpu.make_async_remote_copy(
                    src_ref=p_refs[a].at[mine], dst_ref=land_refs[a].at[2 * px + py], send_sem=send_sems.at[3 * a + k],
                    recv_sem=recv_sems.at[3 * a + k], device_id=(px, py, cc), device_id_type=MESH).wait_recv()
        for cp in sends:
            cp.wait_send()
        for cp in own:
            cp.wait()

    return pl.pallas_call(
        body, out_shape=[_sds(p.shape, p.dtype) for p in ps], in_specs=[ANY] * n, out_specs=[ANY] * n,
        scratch_shapes=[pltpu.SemaphoreType.DMA((3 * n,)), pltpu.SemaphoreType.DMA((3 * n,)),
                        pltpu.SemaphoreType.DMA((n,))],
        name=name)(*ps)


def adamw(name, parts, w, m, v):
    npart, r, c = parts.shape
    tr = _divisor_tile(r, 512, 16)
    c1 = 1.0 / (1.0 - ADAM_B1 ** ADAM_STEP)
    c2 = 1.0 / (1.0 - ADAM_B2 ** ADAM_STEP)

    def body(p_ref, w_ref, m_ref, v_ref, g_ref, d_ref, nm_ref, nv_ref):
        g = p_ref[0].astype(F32)
        for j in range(1, npart):
            g = g + p_ref[j].astype(F32)
        nm = ADAM_B1 * m_ref[...] + (1.0 - ADAM_B1) * g
        nv = ADAM_B2 * v_ref[...] + (1.0 - ADAM_B2) * (g * g)
        g_ref[...] = g
        nm_ref[...] = nm
        nv_ref[...] = nv
        d_ref[...] = -ADAM_LR * ((nm * c1) / (jnp.sqrt(nv * c2) + ADAM_EPS) + ADAM_WD * w_ref[...])

    row = pl.BlockSpec((tr, c), lambda i: (i, 0))
    return pl.pallas_call(
        body, grid=(r // tr,), in_specs=[pl.BlockSpec((npart, tr, c), lambda i: (0, i, 0)), row, row, row],
        out_specs=[row] * 4, out_shape=[_sds((r, c), F32)] * 4, compiler_params=_params("parallel"), name=name,
    )(parts, w, m, v)


def _pack(arrays, dtype, row_multiple):
    flat = jnp.concatenate([a.astype(dtype).reshape(-1) for a in arrays])
    per = row_multiple * LANES
    total = -(-flat.shape[0] // per) * per
    return jnp.pad(flat, (0, total - flat.shape[0])).reshape(total // LANES, LANES)


def _pack_blocked(arrays, dtype, row_multiple):
    flat = jnp.concatenate([a.astype(dtype).reshape(N_DEV, -1) for a in arrays], axis=1)
    per = row_multiple * LANES
    total = -(-flat.shape[1] // per) * per
    return jnp.pad(flat, ((0, 0), (0, total - flat.shape[1]))).reshape(N_DEV, total // LANES, LANES)


def _unpack(buf, shapes, lead=()):
    flat = buf.reshape(lead + (-1,))
    out, off = [], 0
    for s in shapes:
        n = math.prod(s)
        out.append(flat[..., off:off + n].reshape(lead + tuple(s)))
        off += n
    return out


def _to_blocked(full, ax):
    s = full.shape
    return jnp.moveaxis(full.reshape(s[:ax] + (N_DEV, s[ax] // N_DEV) + s[ax + 1:]), ax, 0)


def _from_blocked(blk, ax):
    moved = jnp.moveaxis(blk, 0, ax)
    s = moved.shape
    return moved.reshape(s[:ax] + (s[ax] * s[ax + 1],) + s[ax + 2:])


SMALL = ("rg_conv_w", "rg_conv_b", "rg_b_a", "rg_b_i", "rg_lambda")
GU = "ffn_w_gu"
BIG = ("attn_w_in", "attn_w_out", "rg_w_in", "rg_w_a", "rg_w_i", "rg_w_out", GU, "ffn_w_down")


def kernel(x, attn_w_in, attn_rel_bias, attn_w_out, rg_w_in, rg_conv_w, rg_conv_b, rg_w_a, rg_b_a, rg_w_i, rg_b_i, rg_lambda, rg_w_out, norm_mix_pre, norm_mix_post, norm_ffn_pre, norm_ffn_post, ffn_w_gate, ffn_w_up, ffn_w_down, loss_target, m_attn_w_in, m_attn_rel_bias, m_attn_w_out, m_rg_w_in, m_rg_conv_w, m_rg_conv_b, m_rg_w_a, m_rg_b_a, m_rg_w_i, m_rg_b_i, m_rg_lambda, m_rg_w_out, m_norm_mix_pre, m_norm_mix_post, m_norm_ffn_pre, m_norm_ffn_post, m_ffn_w_gate, m_ffn_w_up, m_ffn_w_down, v_attn_w_in, v_attn_rel_bias, v_attn_w_out, v_rg_w_in, v_rg_conv_w, v_rg_conv_b, v_rg_w_a, v_rg_b_a, v_rg_w_i, v_rg_b_i, v_rg_lambda, v_rg_w_out, v_norm_mix_pre, v_norm_mix_post, v_norm_ffn_pre, v_norm_ffn_post, v_ffn_w_gate, v_ffn_w_up, v_ffn_w_down):
    w_loc = dict(attn_w_in=attn_w_in, attn_rel_bias=attn_rel_bias, attn_w_out=attn_w_out, rg_w_in=rg_w_in,
                 rg_conv_w=rg_conv_w, rg_conv_b=rg_conv_b, rg_w_a=rg_w_a, rg_b_a=rg_b_a, rg_w_i=rg_w_i, rg_b_i=rg_b_i,
                 rg_lambda=rg_lambda, rg_w_out=rg_w_out, norm_mix_pre=norm_mix_pre, norm_mix_post=norm_mix_post,
                 norm_ffn_pre=norm_ffn_pre, norm_ffn_post=norm_ffn_post, ffn_w_gate=ffn_w_gate, ffn_w_up=ffn_w_up,
                 ffn_w_down=ffn_w_down)
    m_loc = dict(attn_w_in=m_attn_w_in, attn_rel_bias=m_attn_rel_bias, attn_w_out=m_attn_w_out, rg_w_in=m_rg_w_in,
                 rg_conv_w=m_rg_conv_w, rg_conv_b=m_rg_conv_b, rg_w_a=m_rg_w_a, rg_b_a=m_rg_b_a, rg_w_i=m_rg_w_i,
                 rg_b_i=m_rg_b_i, rg_lambda=m_rg_lambda, rg_w_out=m_rg_w_out, norm_mix_pre=m_norm_mix_pre,
                 norm_mix_post=m_norm_mix_post, norm_ffn_pre=m_norm_ffn_pre, norm_ffn_post=m_norm_ffn_post,
                 ffn_w_gate=m_ffn_w_gate, ffn_w_up=m_ffn_w_up, ffn_w_down=m_ffn_w_down)
    v_loc = dict(attn_w_in=v_attn_w_in, attn_rel_bias=v_attn_rel_bias, attn_w_out=v_attn_w_out, rg_w_in=v_rg_w_in,
                 rg_conv_w=v_rg_conv_w, rg_conv_b=v_rg_conv_b, rg_w_a=v_rg_w_a, rg_b_a=v_rg_b_a, rg_w_i=v_rg_w_i,
                 rg_b_i=v_rg_b_i, rg_lambda=v_rg_lambda, rg_w_out=v_rg_w_out, norm_mix_pre=v_norm_mix_pre,
                 norm_mix_post=v_norm_mix_post, norm_ffn_pre=v_norm_ffn_pre, norm_ffn_post=v_norm_ffn_post,
                 ffn_w_gate=v_ffn_w_gate, ffn_w_up=v_ffn_w_up, ffn_w_down=v_ffn_w_down)
    axis_of = dict(SHARDED)
    xt, target = x[0], loss_target[0]
    d_model = xt.shape[1]
    rows2d = lambda a: a.reshape(-1, a.shape[-1])
    small_shapes = [w_loc[n].shape for n in SMALL]
    f8 = ffn_w_gate.shape[-1]
    for d in (w_loc, m_loc, v_loc):
        d[GU] = merge_gu(d["ffn_w_gate"], d["ffn_w_up"])

    gathered = all_gather("gather_weights", [rows2d(w_loc[n]).astype(BF16) for n in BIG]
                          + [_pack([w_loc[n] for n in SMALL], F32, 8)])
    blocked = {n: g.reshape((N_DEV,) + w_loc[n].shape) for n, g in zip(BIG, gathered)}
    blocked.update(zip(SMALL, _unpack(gathered[-1], small_shapes, (N_DEV,))))
    full = {n: _from_blocked(blocked[n], axis_of[n]) for n in SMALL}
    row = lambda a: a.reshape(1, -1).astype(F32)
    square = lambda rows8: rows8.reshape(-1, rows8.shape[-1])
    gates = lambda g: jnp.swapaxes(g, 0, 1).reshape(LRU_BLOCKS, -1, g.shape[-1])

    def layer_weights(layer):
        j = layer // 2
        norms = dict(g_pre=row(norm_mix_pre[layer]), g_post=row(norm_mix_post[layer]), idx=j)
        if layer % 2 == 0:
            mix = dict(w_in=blocked["attn_w_in"], w_out=square(blocked["attn_w_out"][:, j]),
                       rel_bias=attn_rel_bias[j], **norms)
        else:
            mix = dict(w_in=blocked["rg_w_in"], w_out=square(blocked["rg_w_out"][:, j]),
                       conv_w=full["rg_conv_w"][j][:, 0, :], conv_b=row(full["rg_conv_b"][j]),
                       w_a=gates(blocked["rg_w_a"][:, j]), w_i=gates(blocked["rg_w_i"][:, j]),
                       b_a=row(full["rg_b_a"][j]), b_i=row(full["rg_b_i"][j]), lam=row(full["rg_lambda"][j]), **norms)
        ffn = dict(w_gu=blocked[GU], w_down=blocked["ffn_w_down"], idx=layer,
                   g_pre=row(norm_ffn_pre[layer]), g_post=row(norm_ffn_post[layer]))
        return mix, ffn

    weights = [layer_weights(layer) for layer in range(DEPTH)]
    act, tape = xt, []
    h = rmsnorm_fwd("l0_mix_norm", act, weights[0][0]["g_pre"])
    for layer in range(DEPTH):
        mix_w, ffn_w = weights[layer]
        mixer_fwd = attn_layer_fwd if layer % 2 == 0 else rg_layer_fwd
        act, h, saved_mix = mixer_fwd(f"l{layer}_mix", act, h, mix_w, ffn_w["g_pre"])
        g_next = weights[layer + 1][0]["g_pre"] if layer + 1 < DEPTH else None
        act, h, saved_ffn = ffn_layer_fwd(f"l{layer}_ffn", act, h, ffn_w, g_next)
        tape.append((mix_w, ffn_w, saved_mix, saved_ffn))
    dact, sq = loss_grad("loss", act, target)
    loss_part = (0.5 * jnp.sum(sq) / d_model).reshape(1)

    grads = {}
    last = tape[DEPTH - 1]
    dm, grads[("ffn_post", DEPTH - 1)] = norm_bwd(f"l{DEPTH - 1}_ffn_dpost", dact, last[3][-1], last[1]["g_post"],
                                                 None, BF16)
    for layer in reversed(range(DEPTH)):
        mix_w, ffn_w, saved_mix, saved_ffn = tape[layer]
        dact, dm, grads[("mix_post", layer)], grads[("ffn", layer)] = ffn_layer_bwd(
            f"l{layer}_ffn", dm, dact, saved_ffn, ffn_w, (saved_mix[-1], mix_w["g_post"]))
        mixer_bwd = attn_layer_bwd if layer % 2 == 0 else rg_layer_bwd
        prev = (tape[layer - 1][3][-1], tape[layer - 1][1]["g_post"]) if layer else None
        dact, dm, grads[("ffn_post", layer - 1)], grads[("mix", layer)] = mixer_bwd(
            f"l{layer}_mix", dm, dact, saved_mix, mix_w, prev)
    attn_g = [grads[("mix", l)] for l in range(0, DEPTH, 2)]
    rg_g = [grads[("mix", l)] for l in range(1, DEPTH, 2)]
    ffn_g = [grads[("ffn", l)] for l in range(DEPTH)]
    stack = lambda gs, key: jnp.stack([g[key] for g in gs])
    by_owner = lambda gs, key, f: jnp.stack([f(g[key]) for g in gs], axis=1)
    rows8 = lambda a: a.reshape(N_DEV, -1, a.shape[-1])
    ungates = lambda a: jnp.swapaxes(a.reshape(LRU_BLOCKS, N_DEV, -1, a.shape[-1]), 0, 1)
    same = lambda a: a
    blocked_g = dict(
        attn_w_in=by_owner(attn_g, "w_in", same), attn_w_out=by_owner(attn_g, "w_out", rows8),
        rg_w_in=by_owner(rg_g, "w_in", same), rg_w_out=by_owner(rg_g, "w_out", rows8),
        rg_w_a=by_owner(rg_g, "w_a", ungates), rg_w_i=by_owner(rg_g, "w_i", ungates),
        **{GU: by_owner(ffn_g, "w_gu", same)},
        ffn_w_down=by_owner(ffn_g, "w_down", same))
    contrib = dict(
        attn_rel_bias=stack(attn_g, "rel_bias"), rg_conv_w=stack(rg_g, "conv_w")[:, :, None, :],
        rg_conv_b=stack(rg_g, "conv_b")[:, 0], rg_b_a=stack(rg_g, "b_a").reshape(rg_b_a.shape[0], LRU_BLOCKS, -1),
        rg_b_i=stack(rg_g, "b_i").reshape(rg_b_i.shape[0], LRU_BLOCKS, -1), rg_lambda=stack(rg_g, "lam")[:, 0],
        norm_mix_pre=jnp.concatenate([grads[("mix", l)]["g_pre"] for l in range(DEPTH)]),
        norm_mix_post=jnp.concatenate([grads[("mix_post", l)] for l in range(DEPTH)]),
        norm_ffn_pre=jnp.concatenate([g["g_pre"] for g in ffn_g]),
        norm_ffn_post=jnp.concatenate([grads[("ffn_post", l)] for l in range(DEPTH)]),
    )
    small_g = _pack_blocked([_to_blocked(contrib[n], axis_of[n]) for n in SMALL], F32, 8)

    slabs = [blocked_g[n].reshape(4, 2, -1, blocked_g[n].shape[-1]) for n in BIG] + [small_g.reshape(4, 2, -1, LANES)]
    core = lax.axis_index("c").astype(jnp.int32).reshape(1)
    from_sibling = exchange_pair("rs_pair", slabs)
    pairs = [pair_sum(f"rs_pair_sum_{i}", g, l, core, BF16 if i < len(BIG) else F32)
             for i, (g, l) in enumerate(zip(slabs, from_sibling))]
    by_chip = exchange_chips("rs_chips", pairs)
    result = {}
    kinds = ("grad", "delta", "new_m", "new_v")
    for n, parts in zip(BIG, by_chip):
        outs = adamw("adamw_" + n, parts, *[rows2d(d[n]) for d in (w_loc, m_loc, v_loc)])
        for kind, a in zip(kinds, outs):
            result[(kind, n)] = a.reshape(w_loc[n].shape)
    for kind in kinds:
        result[(kind, "ffn_w_gate")], result[(kind, "ffn_w_up")] = split_gu(result.pop((kind, GU)), f8)
    outs = adamw("adamw_small", by_chip[-1], *[_pack([d[n] for n in SMALL], F32, 8) for d in (w_loc, m_loc, v_loc)])
    for kind, buf in zip(kinds, outs):
        result.update({(kind, n): a for n, a in zip(SMALL, _unpack(buf, small_shapes))})
    rep_shapes = [w_loc[n].shape for n in REPLICATED] + [(1,)]
    rep_parts, = all_gather("gather_rep_grads", [_pack([contrib[n] for n in REPLICATED] + [loss_part], F32, 8)])
    outs = adamw("adamw_replicated", rep_parts, *[_pack([d[n] for n in REPLICATED] + [jnp.zeros((1,), F32)], F32, 8)
                                                  for d in (w_loc, m_loc, v_loc)])
    for kind, buf in zip(kinds, outs):
        result.update({(kind, n): a for n, a in zip(REPLICATED + ("loss",), _unpack(buf, rep_shapes))})
    loss = result[("grad", "loss")][0]
    return (loss, dact[None], *[result[(kind, n)] for kind in kinds for n in WEIGHTS])
```

```python
import functools
import math

import jax
import jax.numpy as jnp
from jax import lax
from jax.experimental import pallas as pl
from jax.experimental.pallas import tpu as pltpu

F32 = jnp.float32
BF16 = jnp.bfloat16

N_DEV = 8
DEPTH = 4
CHUNK = 64
N_LEFT = 8
BAND = (N_LEFT + 1) * CHUNK
PAD_KEYS = N_LEFT * CHUNK
HEAD_DIM = 64
N_HEADS = 8
REL_CLIP = 256
LRU_BLOCKS = 4
LRU_C = 8.0
RMS_EPS = 1e-6
QK_SCALE = HEAD_DIM ** -0.5

ADAM_LR = 0.001
ADAM_B1 = 0.9
ADAM_B2 = 0.999
ADAM_EPS = 1e-08
ADAM_WD = 0.01
ADAM_STEP = 10

LANES = 1024
V7X_VMEM_LIMIT = 56 * 1024 * 1024

MESH = pl.DeviceIdType.MESH
ANY = pl.BlockSpec(memory_space=pl.ANY)

SHARDED = (
    ("attn_w_in", 2), ("attn_w_out", 1), ("rg_w_in", 2), ("rg_conv_w", 3), ("rg_conv_b", 1),
    ("rg_w_a", 2), ("rg_b_a", 2), ("rg_w_i", 2), ("rg_b_i", 2), ("rg_lambda", 1), ("rg_w_out", 1),
    ("ffn_w_gate", 2), ("ffn_w_up", 2), ("ffn_w_down", 1),
)
REPLICATED = ("attn_rel_bias", "norm_mix_pre", "norm_mix_post", "norm_ffn_pre", "norm_ffn_post")
WEIGHTS = ("attn_w_in", "attn_rel_bias", "attn_w_out", "rg_w_in", "rg_conv_w", "rg_conv_b", "rg_w_a", "rg_b_a",
           "rg_w_i", "rg_b_i", "rg_lambda", "rg_w_out", "norm_mix_pre", "norm_mix_post", "norm_ffn_pre",
           "norm_ffn_post", "ffn_w_gate", "ffn_w_up", "ffn_w_down")


def _params(*dims):
    return pltpu.CompilerParams(dimension_semantics=dims or None, vmem_limit_bytes=V7X_VMEM_LIMIT)


def _sds(shape, dtype):
    return jax.ShapeDtypeStruct(tuple(shape), dtype)


def _row_tile(n, pref):
    t = min(n, pref)
    assert n % t == 0, (n, pref)
    return t


def _divisor_tile(n, limit, multiple):
    if n <= limit:
        return n
    best = max(t for t in range(multiple, limit + 1, multiple) if n % t == 0)
    return best


NN = (((1,), (0,)), ((), ()))
NT = (((1,), (1,)), ((), ()))
TN = (((0,), (0,)), ((), ()))


def _gmm(name, a, b, *, grid, a_blk, a_idx, b_blk, b_idx, o_blk, o_idx, out_shape, out_dtype, dn, acc_shape):
    nk = grid[-1]
    kax = len(grid) - 1

    def body(a_ref, b_ref, o_ref, acc_ref):
        part = lax.dot_general(a_ref[...], b_ref[...], dn, preferred_element_type=F32)
        if nk == 1:
            o_ref[...] = part.astype(o_ref.dtype)
            return
        k = pl.program_id(kax)

        @pl.when(k == 0)
        def _():
            acc_ref[...] = part

        @pl.when(k > 0)
        def _():
            acc_ref[...] += part

        @pl.when(k == nk - 1)
        def _():
            o_ref[...] = acc_ref[...].astype(o_ref.dtype)

    return pl.pallas_call(
        body, grid=grid,
        in_specs=[pl.BlockSpec(a_blk, a_idx), pl.BlockSpec(b_blk, b_idx)],
        out_specs=pl.BlockSpec(o_blk, o_idx),
        out_shape=_sds(out_shape, out_dtype),
        scratch_shapes=[pltpu.VMEM(acc_shape, F32)],
        compiler_params=_params(*(["parallel"] * kax + ["arbitrary"])),
        name=name,
    )(a, b)


def mm_nn(name, a, b, out_dtype, tm=1024, tn=512, tk=1024):
    (m, k), (_, n) = a.shape, b.shape
    tm, tn, tk = _row_tile(m, tm), _row_tile(n, tn), _row_tile(k, tk)
    return _gmm(name, a, b, grid=(m // tm, n // tn, k // tk),
                a_blk=(tm, tk), a_idx=lambda i, j, kk: (i, kk), b_blk=(tk, tn), b_idx=lambda i, j, kk: (kk, j),
                o_blk=(tm, tn), o_idx=lambda i, j, kk: (i, j), out_shape=(m, n), out_dtype=out_dtype, dn=NN,
                acc_shape=(tm, tn))


def mm_nt(name, a, b, out_dtype, tm=1024, tn=512, tk=1024):
    (m, k), (n, _) = a.shape, b.shape
    tm, tn, tk = _row_tile(m, tm), _row_tile(n, tn), _row_tile(k, tk)
    return _gmm(name, a, b, grid=(m // tm, n // tn, k // tk),
                a_blk=(tm, tk), a_idx=lambda i, j, kk: (i, kk), b_blk=(tn, tk), b_idx=lambda i, j, kk: (j, kk),
                o_blk=(tm, tn), o_idx=lambda i, j, kk: (i, j), out_shape=(m, n), out_dtype=out_dtype, dn=NT,
                acc_shape=(tm, tn))


def mm_tn(name, a, b, out_dtype, tm=1024, tn=512, tk=2048):
    (k, m), (_, n) = a.shape, b.shape
    tm, tn, tk = _row_tile(m, tm), _row_tile(n, tn), _row_tile(k, tk)
    return _gmm(name, a, b, grid=(m // tm, n // tn, k // tk),
                a_blk=(tk, tm), a_idx=lambda i, j, kk: (kk, i), b_blk=(tk, tn), b_idx=lambda i, j, kk: (kk, j),
                o_blk=(tm, tn), o_idx=lambda i, j, kk: (i, j), out_shape=(m, n), out_dtype=out_dtype, dn=TN,
                acc_shape=(tm, tn))


MXU_WIDTH = 256


def _blocks_per_step(n8):
    return 1 if n8 % MXU_WIDTH == 0 else 2


def _side_by_side(w_ref, j0, per):
    return w_ref[j0] if per == 1 else jnp.concatenate([w_ref[j0 + j] for j in range(per)], axis=1)


def mm_nn_wblk(name, a, wb, layer, out_dtype, tm=1024):
    (m, k), (nb, _, _, n8) = a.shape, wb.shape
    tm = _row_tile(m, tm)
    per = _blocks_per_step(n8)

    def body(a_ref, w_ref, o_ref):
        o_ref[...] = jnp.dot(a_ref[...], _side_by_side(w_ref, 0, per), preferred_element_type=F32).astype(o_ref.dtype)

    return pl.pallas_call(
        body, grid=(m // tm, nb // per),
        in_specs=[pl.BlockSpec((tm, k), lambda i, j: (i, 0)),
                  pl.BlockSpec((per, None, k, n8), lambda i, j: (j, layer, 0, 0))],
        out_specs=pl.BlockSpec((tm, per * n8), lambda i, j: (i, j)), out_shape=_sds((m, nb * n8), out_dtype),
        compiler_params=_params("parallel", "parallel"), name=name)(a, wb)


def mm_nt_wblk(name, a, wb, layer, out_dtype, tm=1024, tn=512):
    m = a.shape[0]
    nb, _, k, n8 = wb.shape
    tm, tn = _row_tile(m, tm), _row_tile(k, tn)
    per = _blocks_per_step(n8)

    def body(a_ref, b_ref, o_ref):
        acc = None
        for j in range(0, nb, per):
            part = lax.dot_general(a_ref[:, j * n8:(j + per) * n8], _side_by_side(b_ref, j, per), NT,
                                   preferred_element_type=F32)
            acc = part if acc is None else acc + part
        o_ref[...] = acc.astype(o_ref.dtype)

    return pl.pallas_call(
        body, grid=(m // tm, k // tn),
        in_specs=[pl.BlockSpec((tm, nb * n8), lambda i, j: (i, 0)),
                  pl.BlockSpec((nb, None, tn, n8), lambda i, j: (0, layer, j, 0))],
        out_specs=pl.BlockSpec((tm, tn), lambda i, j: (i, j)), out_shape=_sds((m, k), out_dtype),
        compiler_params=_params("parallel", "parallel"), name=name)(a, wb)


def mm_tn_oblk(name, a, b, n8, out_dtype, tk=2048):
    (t, k), nb = a.shape, b.shape[1] // n8
    tk = _row_tile(t, tk)
    per = _blocks_per_step(n8)
    nk = t // tk

    def body(a_ref, b_ref, o_ref, acc_ref):
        s = pl.program_id(1)
        part = lax.dot_general(a_ref[...], b_ref[...], TN, preferred_element_type=F32)

        @pl.when(s == 0)
        def _():
            acc_ref[...] = part

        @pl.when(s > 0)
        def _():
            acc_ref[...] += part

        @pl.when(s == nk - 1)
        def _():
            for j in range(per):
                o_ref[j] = acc_ref[:, j * n8:(j + 1) * n8].astype(o_ref.dtype)

    return pl.pallas_call(
        body, grid=(nb // per, nk),
        in_specs=[pl.BlockSpec((tk, k), lambda j, s: (s, 0)), pl.BlockSpec((tk, per * n8), lambda j, s: (s, j))],
        out_specs=pl.BlockSpec((per, k, n8), lambda j, s: (j, 0, 0)), out_shape=_sds((nb, k, n8), out_dtype),
        scratch_shapes=[pltpu.VMEM((k, per * n8), F32)],
        compiler_params=_params("parallel", "arbitrary"), name=name)(a, b)


def rmsnorm_fwd(name, x, g):
    t, d = x.shape
    tr = _row_tile(t, 512)

    def body(x_ref, g_ref, o_ref):
        xv = x_ref[...]
        r = lax.rsqrt(jnp.mean(xv * xv, axis=-1, keepdims=True) + RMS_EPS)
        o_ref[...] = (xv * r * g_ref[...]).astype(o_ref.dtype)

    return pl.pallas_call(
        body, grid=(t // tr,),
        in_specs=[pl.BlockSpec((tr, d), lambda i: (i, 0)), pl.BlockSpec((1, d), lambda i: (0, 0))],
        out_specs=pl.BlockSpec((tr, d), lambda i: (i, 0)),
        out_shape=_sds((t, d), BF16), compiler_params=_params("parallel"), name=name)(x, g)


def resid_norm_fwd(name, x, m, g, g_next):
    t, d = x.shape
    tr = _row_tile(t, 512)
    chained = g_next is not None

    def body(*refs):
        x_ref, m_ref, g_ref = refs[:3]
        mv = m_ref[...]
        r = lax.rsqrt(jnp.mean(mv * mv, axis=-1, keepdims=True) + RMS_EPS)
        x1 = x_ref[...] + mv * r * g_ref[...]
        if chained:
            gn_ref, o_ref, h_ref = refs[3:]
            r1 = lax.rsqrt(jnp.mean(x1 * x1, axis=-1, keepdims=True) + RMS_EPS)
            h_ref[...] = (x1 * r1 * gn_ref[...]).astype(BF16)
        else:
            o_ref, = refs[3:]
        o_ref[...] = x1

    row = pl.BlockSpec((tr, d), lambda i: (i, 0))
    vec = pl.BlockSpec((1, d), lambda i: (0, 0))
    out = pl.pallas_call(
        body, grid=(t // tr,),
        in_specs=[row, row, vec] + ([vec] if chained else []),
        out_specs=[row, row] if chained else [row],
        out_shape=[_sds((t, d), F32)] + ([_sds((t, d), BF16)] if chained else []),
        compiler_params=_params("parallel"), name=name)(*([x, m, g] + ([g_next] if chained else [])))
    return (out[0], out[1]) if chained else (out[0], None)


def norm_bwd(name, dy, x, g, resid, out_dtype):
    t, d = x.shape
    tr = _row_tile(t, 512)
    has_res = resid is not None

    def body(*refs):
        if has_res:
            dy_ref, x_ref, g_ref, r_ref, dx_ref, dg_ref = refs
        else:
            dy_ref, x_ref, g_ref, dx_ref, dg_ref = refs
        i = pl.program_id(0)
        xv = x_ref[...]
        dyv = dy_ref[...].astype(F32)
        r = lax.rsqrt(jnp.mean(xv * xv, axis=-1, keepdims=True) + RMS_EPS)
        xh = xv * r
        dxh = dyv * g_ref[...]
        dx = r * (dxh - xh * jnp.mean(dxh * xh, axis=-1, keepdims=True))
        if has_res:
            dx = dx + r_ref[...]
        dx_ref[...] = dx.astype(dx_ref.dtype)
        part = jnp.sum(dyv * xh, axis=0, keepdims=True)

        @pl.when(i == 0)
        def _():
            dg_ref[...] = part

        @pl.when(i > 0)
        def _():
            dg_ref[...] += part

    row = pl.BlockSpec((tr, d), lambda i: (i, 0))
    vec = pl.BlockSpec((1, d), lambda i: (0, 0))
    ins = [dy, x, g] + ([resid] if has_res else [])
    return pl.pallas_call(
        body, grid=(t // tr,),
        in_specs=[row, row, vec] + ([row] if has_res else []),
        out_specs=[row, vec],
        out_shape=[_sds((t, d), out_dtype), _sds((1, d), F32)],
        compiler_params=_params("arbitrary"), name=name)(*ins)


def norm_bwd_chain(name, dh, x, g, resid, m_prev, g_prev):
    t, d = x.shape
    tr = _row_tile(t, 512)

    def body(dh_ref, x_ref, g_ref, r_ref, m_ref, gp_ref, dx_ref, dg_ref, dm_ref, dgp_ref):
        i = pl.program_id(0)
        xv = x_ref[...]
        dhv = dh_ref[...]
        r = lax.rsqrt(jnp.mean(xv * xv, axis=-1, keepdims=True) + RMS_EPS)
        xh = xv * r
        dxh = dhv * g_ref[...]
        dx = r * (dxh - xh * jnp.mean(dxh * xh, axis=-1, keepdims=True)) + r_ref[...]
        dx_ref[...] = dx
        mv = m_ref[...]
        rm = lax.rsqrt(jnp.mean(mv * mv, axis=-1, keepdims=True) + RMS_EPS)
        mh = mv * rm
        dmh = dx * gp_ref[...]
        dm_ref[...] = (rm * (dmh - mh * jnp.mean(dmh * mh, axis=-1, keepdims=True))).astype(BF16)
        part = jnp.sum(dhv * xh, axis=0, keepdims=True)
        part_prev = jnp.sum(dx * mh, axis=0, keepdims=True)

        @pl.when(i == 0)
        def _():
            dg_ref[...] = part
            dgp_ref[...] = part_prev

        @pl.when(i > 0)
        def _():
            dg_ref[...] += part
            dgp_ref[...] += part_prev

    row = pl.BlockSpec((tr, d), lambda i: (i, 0))
    vec = pl.BlockSpec((1, d), lambda i: (0, 0))
    return pl.pallas_call(
        body, grid=(t // tr,), in_specs=[row, row, vec, row, row, vec], out_specs=[row, vec, row, vec],
        out_shape=[_sds((t, d), F32), _sds((1, d), F32), _sds((t, d), BF16), _sds((1, d), F32)],
        compiler_params=_params("arbitrary"), name=name)(dh, x, g, resid, m_prev, g_prev)


def close_bwd(tag, dh, x, g_pre, dx1, prev):
    if prev is None:
        dx, dg_pre = norm_bwd(tag + "_dpre", dh, x, g_pre, dx1, F32)
        return dx, dg_pre, None, None
    return norm_bwd_chain(tag + "_dpre", dh, x, g_pre, dx1, *prev)


def loss_grad(name, y, target):
    t, d = y.shape
    tr = _row_tile(t, 512)

    def body(y_ref, t_ref, dy_ref, s_ref):
        i = pl.program_id(0)
        err = y_ref[...] - t_ref[...]
        dy_ref[...] = err * (1.0 / d)
        part = jnp.sum(err * err, axis=0, keepdims=True)

        @pl.when(i == 0)
        def _():
            s_ref[...] = part

        @pl.when(i > 0)
        def _():
            s_ref[...] += part

    row = pl.BlockSpec((tr, d), lambda i: (i, 0))
    vec = pl.BlockSpec((1, d), lambda i: (0, 0))
    return pl.pallas_call(
        body, grid=(t // tr,), in_specs=[row, row], out_specs=[row, vec],
        out_shape=[_sds((t, d), F32), _sds((1, d), F32)],
        compiler_params=_params("arbitrary"), name=name)(y, target)


PAIR = 2 * HEAD_DIM
N_PAIRS = N_HEADS // 2
A_TQ = 512
A_UNROLL_FWD = 8
A_UNROLL_BWD = 4


def _halves(x):
    lane = lax.broadcasted_iota(jnp.int32, x.shape, x.ndim - 1)
    zero = jnp.zeros_like(x)
    return jnp.where(lane < HEAD_DIM, x, zero), jnp.where(lane >= HEAD_DIM, x, zero)


def _merge(a, b):
    lane = lax.broadcasted_iota(jnp.int32, a.shape, a.ndim - 1)
    return jnp.where(lane < HEAD_DIM, a, b)


def _a_valid(c):
    col = lax.broadcasted_iota(jnp.int32, (CHUNK, BAND), 1)
    return col >= (N_LEFT - c) * CHUNK


def attn_a_fwd(name, proj, kp, vp, bias, q_blk):
    t = proj.shape[0]
    tq = _row_tile(t, A_TQ)
    ncs = tq // CHUNK
    un = math.gcd(A_UNROLL_FWD, ncs)

    def body(q_ref, k_ref, v_ref, b_ref, o_ref, l_ref):
        i = pl.program_id(1)

        def group(gg, carry):
            cs = [i * ncs + gg * un + u for u in range(un)]
            r0s = [pl.multiple_of((gg * un + u) * CHUNK, CHUNK) for u in range(un)]
            k0s = [pl.multiple_of(c * CHUNK, CHUNK) for c in cs]
            ss = []
            for u in range(un):
                qh = _halves(q_ref[pl.ds(r0s[u], CHUNK), :] * QK_SCALE)
                kwin = k_ref[pl.ds(k0s[u], BAND), :]
                valid = _a_valid(cs[u])
                for hh in range(2):
                    s = lax.dot_general(qh[hh], kwin, NT, preferred_element_type=F32) + b_ref[hh]
                    ss.append(jnp.where(valid, s, -1e30))
            ps, lses = [], []
            for s in ss:
                mx = jnp.max(s, axis=-1, keepdims=True)
                p = jnp.exp(s - mx)
                den = jnp.sum(p, axis=-1, keepdims=True)
                ps.append((p * (1.0 / den)).astype(BF16))
                lses.append(mx + jnp.log(den))
            for u in range(un):
                vwin = v_ref[pl.ds(k0s[u], BAND), :]
                o0 = jnp.dot(ps[2 * u], vwin, preferred_element_type=F32)
                o1 = jnp.dot(ps[2 * u + 1], vwin, preferred_element_type=F32)
                o_ref[pl.ds(r0s[u], CHUNK), :] = _merge(o0, o1)
                l_ref[pl.ds(r0s[u], CHUNK), :] = jnp.concatenate([lses[2 * u], lses[2 * u + 1]], axis=1)
            return carry

        lax.fori_loop(0, ncs // un, group, 0)

    return pl.pallas_call(
        body, grid=(N_PAIRS, t // tq),
        in_specs=[pl.BlockSpec((tq, PAIR), lambda p, i: (i, q_blk + p)),
                  pl.BlockSpec((t + PAD_KEYS, PAIR), lambda p, i: (0, p)),
                  pl.BlockSpec((t + PAD_KEYS, PAIR), lambda p, i: (0, p)),
                  pl.BlockSpec((2, CHUNK, BAND), lambda p, i: (p, 0, 0))],
        out_specs=[pl.BlockSpec((tq, PAIR), lambda p, i: (i, p)),
                   pl.BlockSpec((None, tq, 2), lambda p, i: (p, i, 0))],
        out_shape=[_sds((t, N_PAIRS * PAIR), F32), _sds((N_PAIRS, t, 2), F32)],
        compiler_params=_params("parallel", "parallel"), name=name)(proj, kp, vp, bias)


def attn_a_bwd(name, proj, kp, vp, bias, o, lse, do, q_blk, do_blk):
    t = proj.shape[0]
    tq = _row_tile(t, A_TQ)
    ncs = tq // CHUNK
    un = math.gcd(A_UNROLL_BWD, ncs)

    def body(q_ref, k_ref, v_ref, b_ref, o_ref, l_ref, do_ref, dq_ref, dk_ref, dv_ref, db_ref):
        i = pl.program_id(1)

        @pl.when(i == 0)
        def _():
            dk_ref[...] = jnp.zeros_like(dk_ref)
            dv_ref[...] = jnp.zeros_like(dv_ref)
            db_ref[...] = jnp.zeros_like(db_ref)

        def group(gg, carry):
            cs = [i * ncs + gg * un + u for u in range(un)]
            r0s = [pl.multiple_of((gg * un + u) * CHUNK, CHUNK) for u in range(un)]
            k0s = [pl.multiple_of(c * CHUNK, CHUNK) for c in cs]
            qhs, dohs, ps, dps, deltas = [], [], [], [], []
            for u in range(un):
                rows = pl.ds(r0s[u], CHUNK)
                qh = _halves(q_ref[rows, :] * QK_SCALE)
                doh = _halves(do_ref[rows, :])
                kwin = k_ref[pl.ds(k0s[u], BAND), :]
                vwin = v_ref[pl.ds(k0s[u], BAND), :]
                valid = _a_valid(cs[u])
                dl = _halves(do_ref[rows, :].astype(F32) * o_ref[rows, :])
                for hh in range(2):
                    s = lax.dot_general(qh[hh], kwin, NT, preferred_element_type=F32) + b_ref[hh]
                    ps.append(jnp.where(valid, jnp.exp(s - l_ref[rows, hh:hh + 1]), 0.0))
                    dps.append(lax.dot_general(doh[hh], vwin, NT, preferred_element_type=F32))
                    deltas.append(jnp.sum(dl[hh], axis=-1, keepdims=True))
                qhs.append(qh)
                dohs.append(doh)
            dss = [p * (dp - dl) for p, dp, dl in zip(ps, dps, deltas)]
            for hh in range(2):
                tot = dss[hh]
                for u in range(1, un):
                    tot = tot + dss[2 * u + hh]
                db_ref[hh] += tot
            for u in range(un):
                kwin = k_ref[pl.ds(k0s[u], BAND), :]
                ds0, ds1 = dss[2 * u].astype(BF16), dss[2 * u + 1].astype(BF16)
                dq_ref[pl.ds(r0s[u], CHUNK), :] = _merge(jnp.dot(ds0, kwin, preferred_element_type=F32),
                                                         jnp.dot(ds1, kwin, preferred_element_type=F32)) * QK_SCALE
                dk_ref[pl.ds(k0s[u], BAND), :] += (lax.dot_general(ds0, qhs[u][0], TN, preferred_element_type=F32)
                                                   + lax.dot_general(ds1, qhs[u][1], TN, preferred_element_type=F32))
                dv_ref[pl.ds(k0s[u], BAND), :] += (
                    lax.dot_general(ps[2 * u].astype(BF16), dohs[u][0], TN, preferred_element_type=F32)
                    + lax.dot_general(ps[2 * u + 1].astype(BF16), dohs[u][1], TN, preferred_element_type=F32))
            return carry

        lax.fori_loop(0, ncs // un, group, 0)

    tile = lambda blk: pl.BlockSpec((tq, PAIR), lambda p, i: (i, blk + p))
    whole = pl.BlockSpec((t + PAD_KEYS, PAIR), lambda p, i: (0, p))
    bspec = pl.BlockSpec((2, CHUNK, BAND), lambda p, i: (p, 0, 0))
    return pl.pallas_call(
        body, grid=(N_PAIRS, t // tq),
        in_specs=[tile(q_blk), whole, whole, bspec, tile(0), pl.BlockSpec((None, tq, 2), lambda p, i: (p, i, 0)),
                  tile(do_blk)],
        out_specs=[tile(0), whole, whole, bspec],
        out_shape=[_sds((t, N_PAIRS * PAIR), F32), _sds((t + PAD_KEYS, N_PAIRS * PAIR), F32),
                   _sds((t + PAD_KEYS, N_PAIRS * PAIR), F32), _sds((2 * N_PAIRS, CHUNK, BAND), F32)],
        compiler_params=_params("parallel", "arbitrary"), name=name)(proj, kp, vp, bias, o, lse, do)


SB_TQ = 256
SB_TK = 256
SB_DEAD = -125.0


def _tri(n, strict):
    j = lax.broadcasted_iota(jnp.int32, (n, n), 0)
    s = lax.broadcasted_iota(jnp.int32, (n, n), 1)
    return jnp.where((j > s) if strict else (j >= s), 1.0, 0.0).astype(BF16)


def _suffix_sum(x, tri, exact):
    hi = x.astype(BF16)
    out = jnp.dot(hi, tri, preferred_element_type=F32)
    if exact:
        lo = (x - hi.astype(F32)).astype(BF16)
        out = out + jnp.dot(lo, tri, preferred_element_type=F32)
    return out


def _sb_scores(qh, ks, causal):
    z = lax.dot_general(qh, ks, NT, preferred_element_type=F32)
    lb = jnp.minimum(z, 0.0) - jnp.log(1.0 + jnp.exp(-jnp.abs(z)))
    m = lb - z
    if causal is not None:
        m = jnp.where(causal, m, 0.0)
    return lb, m


def _causal(tq, tk, off):
    return (lax.broadcasted_iota(jnp.int32, (tq, tk), 1) + off * tk) < lax.broadcasted_iota(jnp.int32, (tq, tk), 0)


def sb_fwd(name, proj, q_blk, k_blk, v_blk):
    t = proj.shape[0]
    tq = _row_tile(t, SB_TQ)
    tk = min(SB_TK, tq)
    per = tq // tk

    def body(q_ref, k_ref, v_ref, o_ref):
        i = pl.program_id(1)
        tri = _tri(tk, True)
        qh = _halves(q_ref[...] * QK_SCALE)

        def blocks(kb, carry, off):
            k0 = pl.multiple_of(kb * tk, tk)
            ks, vs = k_ref[pl.ds(k0, tk), :], v_ref[pl.ds(k0, tk), :]
            causal = None if off is None else _causal(tq, tk, off)
            lbm = [_sb_scores(qh[hh], ks, causal) for hh in range(2)]
            afters = [_suffix_sum(lbm[hh][1], tri, False) for hh in range(2)]
            out = []
            for hh in range(2):
                acc, cm = carry[2 * hh], carry[2 * hh + 1]
                w = jnp.exp(lbm[hh][0] + afters[hh] + cm)
                if causal is not None:
                    w = jnp.where(causal, w, 0.0)
                out += [acc + jnp.dot(w.astype(BF16), vs, preferred_element_type=F32),
                        cm + jnp.sum(lbm[hh][1], axis=-1, keepdims=True)]
            return tuple(out)

        def alive(carry):
            return jnp.maximum(jnp.max(carry[1]), jnp.max(carry[3])) > SB_DEAD

        carry = (jnp.zeros((tq, PAIR), F32), jnp.zeros((tq, 1), F32)) * 2
        for off in reversed(range(per)):
            carry = blocks(i * per + off, carry, off)

        def step(c):
            new = blocks(i * per - 1 - c[0], c[2:], None)
            return (c[0] + 1, alive(new)) + new

        out = lax.while_loop(lambda c: jnp.logical_and(c[0] < i * per, c[1]), step,
                             (jnp.int32(0), alive(carry)) + carry)
        o_ref[...] = _merge(out[2], out[4])

    return pl.pallas_call(
        body, grid=(N_PAIRS, t // tq),
        in_specs=[pl.BlockSpec((tq, PAIR), lambda p, i: (i, q_blk + p)),
                  pl.BlockSpec((t, PAIR), lambda p, i: (0, k_blk + p)),
                  pl.BlockSpec((t, PAIR), lambda p, i: (0, v_blk + p))],
        out_specs=pl.BlockSpec((tq, PAIR), lambda p, i: (i, p)),
        out_shape=_sds((t, N_PAIRS * PAIR), F32), compiler_params=_params("parallel", "parallel"), name=name,
    )(proj, proj, proj)


def sb_bwd(name, proj, o, do, q_blk, k_blk, v_blk, do_blk):
    t = proj.shape[0]
    tq = _row_tile(t, SB_TQ)
    tk = min(SB_TK, tq)
    per = tq // tk

    def body(q_ref, k_ref, v_ref, o_ref, do_ref, dq_ref, dk_ref, dv_ref):
        i = pl.program_id(1)

        @pl.when(i == 0)
        def _():
            dk_ref[...] = jnp.zeros_like(dk_ref)
            dv_ref[...] = jnp.zeros_like(dv_ref)

        tri_s, tri_i = _tri(tk, True), _tri(tk, False)
        qh = _halves(q_ref[...] * QK_SCALE)
        doh = _halves(do_ref[...])
        deltas = [jnp.sum(x, axis=-1, keepdims=True) for x in _halves(do_ref[...].astype(F32) * o_ref[...])]

        def blocks(kb, carry, off):
            k0 = pl.multiple_of(kb * tk, tk)
            ks, vs = k_ref[pl.ds(k0, tk), :], v_ref[pl.ds(k0, tk), :]
            causal = None if off is None else _causal(tq, tk, off)
            lbm = [_sb_scores(qh[hh], ks, causal) for hh in range(2)]
            dws = [lax.dot_general(doh[hh], vs, NT, preferred_element_type=F32) for hh in range(2)]
            afters = [_suffix_sum(lbm[hh][1], tri_s, False) for hh in range(2)]
            wbs, es = [], []
            for hh in range(2):
                w = jnp.exp(lbm[hh][0] + afters[hh] + carry[3 * hh + 1])
                if causal is not None:
                    w = jnp.where(causal, w, 0.0)
                wbs.append(w.astype(BF16))
                es.append(wbs[hh].astype(F32) * dws[hh])
            sfx = [_suffix_sum(es[hh], tri_i, True) for hh in range(2)]
            dzs = []
            for hh in range(2):
                left = deltas[hh] - (sfx[hh] + carry[3 * hh + 2])
                sig = jnp.exp(lbm[hh][0])
                dz = es[hh] * (1.0 - sig) - left * sig
                if causal is not None:
                    dz = jnp.where(causal, dz, 0.0)
                dzs.append(dz.astype(BF16))
            dk_ref[pl.ds(k0, tk), :] += (lax.dot_general(dzs[0], qh[0], TN, preferred_element_type=F32)
                                         + lax.dot_general(dzs[1], qh[1], TN, preferred_element_type=F32))
            dv_ref[pl.ds(k0, tk), :] += (lax.dot_general(wbs[0], doh[0], TN, preferred_element_type=F32)
                                         + lax.dot_general(wbs[1], doh[1], TN, preferred_element_type=F32))
            out = []
            for hh in range(2):
                out += [carry[3 * hh] + jnp.dot(dzs[hh], ks, preferred_element_type=F32),
                        carry[3 * hh + 1] + jnp.sum(lbm[hh][1], axis=-1, keepdims=True),
                        carry[3 * hh + 2] + jnp.sum(es[hh], axis=-1, keepdims=True)]
            return tuple(out)

        def alive(carry):
            return jnp.maximum(jnp.max(carry[1]), jnp.max(carry[4])) > SB_DEAD

        zero = jnp.zeros((tq, 1), F32)
        carry = (jnp.zeros((tq, PAIR), F32), zero, zero) * 2
        for off in reversed(range(per)):
            carry = blocks(i * per + off, carry, off)

        def step(c):
            new = blocks(i * per - 1 - c[0], c[2:], None)
            return (c[0] + 1, alive(new)) + new

        out = lax.while_loop(lambda c: jnp.logical_and(c[0] < i * per, c[1]), step,
                             (jnp.int32(0), alive(carry)) + carry)
        dq_ref[...] = _merge(out[2], out[5]) * QK_SCALE

    tile = lambda blk: pl.BlockSpec((tq, PAIR), lambda p, i: (i, blk + p))
    whole = lambda blk: pl.BlockSpec((t, PAIR), lambda p, i: (0, blk + p))
    return pl.pallas_call(
        body, grid=(N_PAIRS, t // tq),
        in_specs=[tile(q_blk), whole(k_blk), whole(v_blk), tile(0), tile(do_blk)],
        out_specs=[tile(0), whole(0), whole(0)],
        out_shape=[_sds((t, N_PAIRS * PAIR), F32)] * 3,
        compiler_params=_params("parallel", "arbitrary"), name=name)(proj, proj, proj, o, do)


def _sigmoid(x):
    return 1.0 / (1.0 + jnp.exp(-x))


def gu_gap(f8):
    return -(-f8 // 128) * 128


def merge_gu(gate, up):
    f8 = gate.shape[-1]
    pad = jnp.zeros(gate.shape[:-1] + (gu_gap(f8) - f8,), gate.dtype)
    return jnp.concatenate([gate, pad, up], axis=-1)


def split_gu(gu, f8):
    return gu[..., :f8], gu[..., gu_gap(f8):]


def ffn_up(name, h, wgu, f8, layer):
    t, d = h.shape
    nb, _, _, fw = wgu.shape
    gap = gu_gap(f8)
    tm = _row_tile(t, 1024)

    per = 2

    def body(h_ref, w_ref, gu_ref, a_ref):
        hv = h_ref[...]
        rs = [jnp.dot(hv, w_ref[j], preferred_element_type=F32) for j in range(per)]
        for j in range(per):
            gu_ref[j] = rs[j].astype(BF16)
            g, u = rs[j][:, :f8], rs[j][:, gap:]
            a_ref[j] = (g * _sigmoid(g) * u).astype(BF16)

    return pl.pallas_call(
        body, grid=(t // tm, nb // per),
        in_specs=[pl.BlockSpec((tm, d), lambda i, k: (i, 0)),
                  pl.BlockSpec((per, None, d, fw), lambda i, k: (k, layer, 0, 0))],
        out_specs=[pl.BlockSpec((per, tm, fw), lambda i, k: (k, i, 0)),
                   pl.BlockSpec((per, tm, f8), lambda i, k: (k, i, 0))],
        out_shape=[_sds((nb, t, fw), BF16), _sds((nb, t, f8), BF16)],
        compiler_params=_params("parallel", "parallel"), name=name)(h, wgu)


def _mm_all_blocks(name, a, w, layer, dn, tm):
    nb, t, f = a.shape
    wshape = w.shape[2:]
    d = wshape[1] if dn == NN else wshape[0]
    tm = _row_tile(t, tm)

    def body(a_ref, w_ref, o_ref):
        acc = lax.dot_general(a_ref[0], w_ref[0], dn, preferred_element_type=F32)
        for k in range(1, nb):
            acc = acc + lax.dot_general(a_ref[k], w_ref[k], dn, preferred_element_type=F32)
        o_ref[...] = acc

    return pl.pallas_call(
        body, grid=(t // tm,),
        in_specs=[pl.BlockSpec((nb, tm, f), lambda i: (0, i, 0)),
                  pl.BlockSpec((nb, None) + wshape, lambda i: (0, layer, 0, 0))],
        out_specs=pl.BlockSpec((tm, d), lambda i: (i, 0)), out_shape=_sds((t, d), F32),
        compiler_params=_params("parallel"), name=name)(a, w)


def ffn_down(name, a, wd, layer):
    return _mm_all_blocks(name, a, wd, layer, NN, 512)


def ffn_bwd_act(name, dm, wd, gu, f8, layer):
    t, d = dm.shape
    nb, _, fw = gu.shape
    gap = gu_gap(f8)
    tm = _row_tile(t, 1024)

    per = 2

    def body(dm_ref, wd_ref, gu_ref, o_ref):
        dmv = dm_ref[...]
        das = [lax.dot_general(dmv, wd_ref[j], NT, preferred_element_type=F32) for j in range(per)]
        o_ref[...] = jnp.zeros_like(o_ref)
        for j in range(per):
            gv = gu_ref[j, :, :f8].astype(F32)
            uv = gu_ref[j, :, gap:].astype(F32)
            sg = _sigmoid(gv)
            o_ref[j, :, :f8] = (das[j] * uv * sg * (1.0 + gv * (1.0 - sg))).astype(BF16)
            o_ref[j, :, gap:] = (das[j] * gv * sg).astype(BF16)

    bspec = pl.BlockSpec((per, tm, fw), lambda i, k: (k, i, 0))
    return pl.pallas_call(
        body, grid=(t // tm, nb // per),
        in_specs=[pl.BlockSpec((tm, d), lambda i, k: (i, 0)),
                  pl.BlockSpec((per, None, f8, d), lambda i, k: (k, layer, 0, 0)), bspec],
        out_specs=bspec, out_shape=_sds((nb, t, fw), BF16),
        compiler_params=_params("parallel", "parallel"), name=name)(dm, wd, gu)


def ffn_bwd_dh(name, dgu, wgu, layer):
    return _mm_all_blocks(name, dgu, wgu, layer, NT, 512)


def ffn_dw_in(name, h, dact):
    t, d = h.shape
    nb, _, f8 = dact.shape
    tk = _row_tile(t, 2048)
    return _gmm(name, h, dact, grid=(nb, t // tk),
                a_blk=(tk, d), a_idx=lambda b, s: (s, 0), b_blk=(None, tk, f8), b_idx=lambda b, s: (b, s, 0),
                o_blk=(None, d, f8), o_idx=lambda b, s: (b, 0, 0), out_shape=(nb, d, f8), out_dtype=F32, dn=TN,
                acc_shape=(d, f8))


def ffn_dw_down(name, a, dm):
    nb, t, f8 = a.shape
    d = dm.shape[1]
    tk = _row_tile(t, 2048)
    return _gmm(name, a, dm, grid=(nb, t // tk),
                a_blk=(None, tk, f8), a_idx=lambda b, s: (b, s, 0), b_blk=(tk, d), b_idx=lambda b, s: (s, 0),
                o_blk=(None, f8, d), o_idx=lambda b, s: (b, 0, 0), out_shape=(nb, f8, d), out_dtype=F32, dn=TN,
                acc_shape=(f8, d))


GELU_C = math.sqrt(2.0 / math.pi)
GELU_A = 0.044715


def _gelu(x):
    return 0.5 * x * (1.0 + jnp.tanh(GELU_C * (x + GELU_A * x * x * x)))


def _gelu_grad(x):
    th = jnp.tanh(GELU_C * (x + GELU_A * x * x * x))
    return 0.5 * (1.0 + th) + 0.5 * x * (1.0 - th * th) * GELU_C * (1.0 + 3.0 * GELU_A * x * x)


def _neg_expm1(x):
    series = x * (1.0 + x * (0.5 + x * (1.0 / 6.0 + x * (1.0 / 24.0 + x * (1.0 / 120.0 + x * (1.0 / 720.0))))))
    return -jnp.where(x > -0.25, series, jnp.exp(x) - 1.0)


CONV_TR = 256
CONV_TAPS = 4
HALO = 8


def _shifted(ext, k, tr, back):
    if back:
        return pltpu.roll(ext, k, 0)[HALO:, :] if k else ext[HALO:, :]
    return pltpu.roll(ext, tr + HALO - k, 0)[:tr, :] if k else ext[:tr, :]


def conv4_fwd(name, src, cb, w, b):
    t, c = src.shape[0], w.shape[1]
    tr = _row_tile(t, CONV_TR)
    hb = tr // HALO

    def body(x_ref, h_ref, w_ref, b_ref, o_ref):
        i = pl.program_id(0)
        ext = jnp.concatenate([jnp.where(i == 0, 0.0, h_ref[...]), x_ref[...]], axis=0)
        acc = b_ref[...]
        for k in range(CONV_TAPS):
            acc = acc + w_ref[CONV_TAPS - 1 - k:CONV_TAPS - k, :] * _shifted(ext, k, tr, True)
        o_ref[...] = acc

    return pl.pallas_call(
        body, grid=(t // tr,),
        in_specs=[pl.BlockSpec((tr, c), lambda i: (i, cb)),
                  pl.BlockSpec((HALO, c), lambda i: (jnp.maximum(i * hb - 1, 0), cb)),
                  pl.BlockSpec((CONV_TAPS, c), lambda i: (0, 0)), pl.BlockSpec((1, c), lambda i: (0, 0))],
        out_specs=pl.BlockSpec((tr, c), lambda i: (i, 0)), out_shape=_sds((t, c), F32),
        compiler_params=_params("parallel"), name=name)(src, src, w, b)


def conv4_bwd_x(name, dy, w):
    t, c = dy.shape
    tr = _row_tile(t, CONV_TR)
    hb = tr // HALO
    last = t // tr - 1

    def body(y_ref, h_ref, w_ref, o_ref):
        i = pl.program_id(0)
        ext = jnp.concatenate([y_ref[...], jnp.where(i == last, 0.0, h_ref[...])], axis=0)
        acc = w_ref[CONV_TAPS - 1:CONV_TAPS, :] * y_ref[...]
        for k in range(1, CONV_TAPS):
            acc = acc + w_ref[CONV_TAPS - 1 - k:CONV_TAPS - k, :] * _shifted(ext, k, tr, False)
        o_ref[...] = acc

    return pl.pallas_call(
        body, grid=(t // tr,),
        in_specs=[pl.BlockSpec((tr, c), lambda i: (i, 0)),
                  pl.BlockSpec((HALO, c), lambda i: (jnp.minimum((i + 1) * hb, t // HALO - 1), 0)),
                  pl.BlockSpec((CONV_TAPS, c), lambda i: (0, 0))],
        out_specs=pl.BlockSpec((tr, c), lambda i: (i, 0)), out_shape=_sds((t, c), F32),
        compiler_params=_params("parallel"), name=name)(dy, dy, w)


def conv4_bwd_w(name, src, cb, dy):
    t, c = dy.shape
    tr = _row_tile(t, CONV_TR)
    hb = tr // HALO

    def body(x_ref, h_ref, dy_ref, dw_ref, db_ref):
        i = pl.program_id(0)

        @pl.when(i == 0)
        def _():
            dw_ref[...] = jnp.zeros_like(dw_ref)
            db_ref[...] = jnp.zeros_like(db_ref)

        ext = jnp.concatenate([jnp.where(i == 0, 0.0, h_ref[...]), x_ref[...]], axis=0)
        dyv = dy_ref[...]
        db_ref[...] += jnp.sum(dyv, axis=0, keepdims=True)
        for k in range(CONV_TAPS):
            dw_ref[CONV_TAPS - 1 - k:CONV_TAPS - k, :] += jnp.sum(dyv * _shifted(ext, k, tr, True), axis=0,
                                                                   keepdims=True)

    return pl.pallas_call(
        body, grid=(t // tr,),
        in_specs=[pl.BlockSpec((tr, c), lambda i: (i, cb)),
                  pl.BlockSpec((HALO, c), lambda i: (jnp.maximum(i * hb - 1, 0), cb)),
                  pl.BlockSpec((tr, c), lambda i: (i, 0))],
        out_specs=[pl.BlockSpec((CONV_TAPS, c), lambda i: (0, 0)), pl.BlockSpec((1, c), lambda i: (0, 0))],
        out_shape=[_sds((CONV_TAPS, c), F32), _sds((1, c), F32)],
        compiler_params=_params("arbitrary"), name=name)(src, src, dy)


def _rg_gate_values(xcv, wa_ref, wi_ref, ba_ref, bi_ref, lam_ref):
    xb = xcv.astype(BF16)
    r = _sigmoid(jnp.dot(xb, wa_ref[...], preferred_element_type=F32) + ba_ref[...])
    ig = _sigmoid(jnp.dot(xb, wi_ref[...], preferred_element_type=F32) + bi_ref[...])
    lam = lam_ref[...]
    sp = jnp.maximum(-lam, 0.0) + jnp.log(1.0 + jnp.exp(-jnp.abs(lam)))
    log_a = -LRU_C * r * sp
    a = jnp.exp(log_a)
    mult = jnp.sqrt(_neg_expm1(2.0 * log_a))
    return xb, r, ig, sp, a, mult


def rg_gates_fwd(name, xc, wa, wi, ba, bi, lam):
    t, c = xc.shape
    nb, cb, _ = wa.shape
    tm = _row_tile(t, 512)

    def body(xc_ref, wa_ref, wi_ref, ba_ref, bi_ref, lam_ref, a_ref, u_ref):
        xcv = xc_ref[...]
        _, _, ig, _, a, mult = _rg_gate_values(xcv, wa_ref, wi_ref, ba_ref, bi_ref, lam_ref)
        a_ref[...] = a
        u_ref[...] = mult * (ig * xcv)

    blk = pl.BlockSpec((tm, cb), lambda i, n: (i, n))
    wsp = pl.BlockSpec((None, cb, cb), lambda i, n: (n, 0, 0))
    vec = pl.BlockSpec((1, cb), lambda i, n: (0, n))
    return pl.pallas_call(
        body, grid=(t // tm, nb), in_specs=[blk, wsp, wsp, vec, vec, vec], out_specs=[blk, blk],
        out_shape=[_sds((t, c), F32)] * 2, compiler_params=_params("parallel", "parallel"), name=name,
    )(xc, wa, wi, ba, bi, lam)


def rg_gates_bwd(name, xc, gu, hs, wa, wi, ba, bi, lam):
    t, c = xc.shape
    nb, cb, _ = wa.shape
    tm = _row_tile(t, 512)
    hb = tm // HALO

    def body(xc_ref, gu_ref, h_ref, halo_ref, wa_ref, wi_ref, ba_ref, bi_ref, lam_ref,
             dxc_ref, dwa_ref, dwi_ref, dba_ref, dbi_ref, dlam_ref):
        i = pl.program_id(1)
        hprev = _shifted(jnp.concatenate([jnp.where(i == 0, 0.0, halo_ref[...]), h_ref[...]], axis=0), 1, tm, True)

        @pl.when(i == 0)
        def _():
            for ref in (dwa_ref, dwi_ref, dba_ref, dbi_ref, dlam_ref):
                ref[...] = jnp.zeros_like(ref)

        xcv = xc_ref[...]
        xb, r, ig, sp, a, mult = _rg_gate_values(xcv, wa_ref, wi_ref, ba_ref, bi_ref, lam_ref)
        gv = gu_ref[...]
        d_ixc = gv * mult
        d_i = d_ixc * xcv
        d_mult = gv * ig * xcv
        d_a = gv * hprev - d_mult * a / mult
        d_log_a = d_a * a
        d_r = d_log_a * (-LRU_C * sp)
        sig_neg_lam = 1.0 / (1.0 + jnp.exp(lam_ref[...]))
        dlam_ref[...] += jnp.sum(d_log_a * r, axis=0, keepdims=True) * (LRU_C * sig_neg_lam)
        dpa = d_r * r * (1.0 - r)
        dpi = d_i * ig * (1.0 - ig)
        dba_ref[...] += jnp.sum(dpa, axis=0, keepdims=True)
        dbi_ref[...] += jnp.sum(dpi, axis=0, keepdims=True)
        dpab, dpib = dpa.astype(BF16), dpi.astype(BF16)
        dxc_ref[...] = (d_ixc * ig + lax.dot_general(dpab, wa_ref[...], NT, preferred_element_type=F32)
                        + lax.dot_general(dpib, wi_ref[...], NT, preferred_element_type=F32))
        dwa_ref[...] += lax.dot_general(xb, dpab, TN, preferred_element_type=F32)
        dwi_ref[...] += lax.dot_general(xb, dpib, TN, preferred_element_type=F32)

    blk = pl.BlockSpec((tm, cb), lambda n, i: (i, n))
    wsp = pl.BlockSpec((None, cb, cb), lambda n, i: (n, 0, 0))
    vec = pl.BlockSpec((1, cb), lambda n, i: (0, n))
    halo = pl.BlockSpec((HALO, cb), lambda n, i: (jnp.maximum(i * hb - 1, 0), n))
    return pl.pallas_call(
        body, grid=(nb, t // tm), in_specs=[blk, blk, blk, halo, wsp, wsp, vec, vec, vec],
        out_specs=[blk, wsp, wsp, vec, vec, vec],
        out_shape=[_sds((t, c), F32), _sds((nb, cb, cb), F32), _sds((nb, cb, cb), F32),
                   _sds((1, c), F32), _sds((1, c), F32), _sds((1, c), F32)],
        compiler_params=_params("parallel", "arbitrary"), name=name)(xc, gu, hs, hs, wa, wi, ba, bi, lam)


SCAN_TS = 256
SCAN_TC = 512


def _tile_scan(a, b, reverse):
    ts = a.shape[0]
    row = lax.broadcasted_iota(jnp.int32, a.shape, 0)
    d = 1
    while d < ts:
        if reverse:
            inside = row < ts - d
            a_sh = jnp.where(inside, pltpu.roll(a, ts - d, 0), 1.0)
            b_sh = jnp.where(inside, pltpu.roll(b, ts - d, 0), 0.0)
        else:
            inside = row >= d
            a_sh = jnp.where(inside, pltpu.roll(a, d, 0), 1.0)
            b_sh = jnp.where(inside, pltpu.roll(b, d, 0), 0.0)
        b = b + a * b_sh
        a = a * a_sh
        d *= 2
    return a, b


def rg_scan_fwd(name, a, u, gate_pre):
    t, c = a.shape
    ts, tc = _row_tile(t, SCAN_TS), _row_tile(c, SCAN_TC)

    def body(a_ref, u_ref, g_ref, h_ref, z_ref, carry_ref):
        s = pl.program_id(1)

        @pl.when(s == 0)
        def _():
            carry_ref[...] = jnp.zeros_like(carry_ref)

        ac, bc = _tile_scan(a_ref[...], u_ref[...], False)
        h = bc + ac * carry_ref[0:1, :]
        h_ref[...] = h
        z_ref[...] = (h * _gelu(g_ref[...])).astype(BF16)
        carry_ref[0:1, :] = h[ts - 1:ts, :]

    blk = pl.BlockSpec((ts, tc), lambda j, s: (s, j))
    return pl.pallas_call(
        body, grid=(c // tc, t // ts), in_specs=[blk, blk, blk], out_specs=[blk, blk],
        out_shape=[_sds((t, c), F32), _sds((t, c), BF16)], scratch_shapes=[pltpu.VMEM((8, tc), F32)],
        compiler_params=_params("parallel", "arbitrary"), name=name)(a, u, gate_pre)


def rg_scan_bwd(name, a, hs, gate_pre, dz):
    t, c = hs.shape
    ts, tc = _row_tile(t, SCAN_TS), _row_tile(c, SCAN_TC)
    nt = t // ts
    hb = ts // HALO

    def body(a_ref, halo_ref, h_ref, g_ref, dz_ref, gu_ref, dgate_ref, carry_ref):
        s = pl.program_id(1)

        @pl.when(s == 0)
        def _():
            carry_ref[...] = jnp.zeros_like(carry_ref)

        a_next = _shifted(jnp.concatenate([a_ref[...], jnp.where(s == 0, 0.0, halo_ref[...])], axis=0), 1, ts, False)
        gate = g_ref[...]
        dzv = dz_ref[...]
        dgate_ref[...] = (dzv * h_ref[...] * _gelu_grad(gate)).astype(BF16)
        ac, bc = _tile_scan(a_next, dzv * _gelu(gate), True)
        gu = bc + ac * carry_ref[0:1, :]
        gu_ref[...] = gu
        carry_ref[0:1, :] = gu[0:1, :]

    blk = pl.BlockSpec((ts, tc), lambda j, s: (nt - 1 - s, j))
    halo = pl.BlockSpec((HALO, tc), lambda j, s: (jnp.minimum((nt - s) * hb, t // HALO - 1), j))
    return pl.pallas_call(
        body, grid=(c // tc, nt), in_specs=[blk, halo, blk, blk, blk], out_specs=[blk, blk],
        out_shape=[_sds((t, c), F32), _sds((t, c), BF16)], scratch_shapes=[pltpu.VMEM((8, tc), F32)],
        compiler_params=_params("parallel", "arbitrary"), name=name)(a, a, hs, gate_pre, dz)


QA_BLK, KA_BLK, VA_BLK, QS_BLK, KS_BLK, VS_BLK = (g * N_PAIRS for g in range(6))


TOEP_W = 640
TOEP_FLAT = 320
TABLE_LOW = 193


def rel_bias_matrix(name, table):
    h = table.shape[0]
    diag = jnp.concatenate([jnp.repeat(table[:, 2 * REL_CLIP:], TOEP_FLAT, axis=1),
                            jnp.flip(table[:, TABLE_LOW:2 * REL_CLIP], axis=1),
                            jnp.zeros((h, 1), table.dtype)], axis=1)[:, None, :]

    def body(v_ref, o_ref):
        rows = jnp.broadcast_to(v_ref[...], (CHUNK, TOEP_W))
        o_ref[...] = pltpu.roll(rows, TOEP_W - (CHUNK - 1), 1, stride=1, stride_axis=0)

    out = pl.pallas_call(
        body, grid=(h,), in_specs=[pl.BlockSpec((None, 1, TOEP_W), lambda hh: (hh, 0, 0))],
        out_specs=pl.BlockSpec((None, CHUNK, TOEP_W), lambda hh: (hh, 0, 0)),
        out_shape=_sds((h, CHUNK, TOEP_W), F32), compiler_params=_params("parallel"), name=name)(diag)
    return out[:, :, :BAND]


def rel_bias_grad(name, dbias):
    h = dbias.shape[0]
    flipped = jnp.pad(jnp.flip(dbias, axis=1), ((0, 0), (0, 0), (0, TOEP_W - BAND)))

    def body(x_ref, o_ref):
        skew = pltpu.roll(x_ref[...], 0, 1, stride=1, stride_axis=0)
        col = jnp.sum(skew, axis=0, keepdims=True)
        lane = lax.broadcasted_iota(jnp.int32, col.shape, 1)
        flat = jnp.sum(jnp.where(lane < TOEP_FLAT, col, 0.0), axis=1, keepdims=True)
        o_ref[...] = jnp.where(lane == TOEP_W - 1, flat, col)

    out = pl.pallas_call(
        body, grid=(h,), in_specs=[pl.BlockSpec((None, CHUNK, TOEP_W), lambda hh: (hh, 0, 0))],
        out_specs=pl.BlockSpec((None, 1, TOEP_W), lambda hh: (hh, 0, 0)),
        out_shape=_sds((h, 1, TOEP_W), F32), compiler_params=_params("parallel"), name=name)(flipped)[:, 0, :]
    return jnp.concatenate([jnp.zeros((h, TABLE_LOW), F32), jnp.flip(out[:, TOEP_FLAT:TOEP_W - 1], axis=1),
                            out[:, TOEP_W - 1:]], axis=1)


def attn_layer_fwd(tag, x, h, w, g_next):
    proj = mm_nn_wblk(tag + "_proj", h, w["w_in"], w["idx"], BF16)
    width = N_PAIRS * PAIR
    pad = lambda a: jnp.pad(a, ((PAD_KEYS, 0), (0, 0)))
    kap, vap = pad(proj[:, width:2 * width]), pad(proj[:, 2 * width:3 * width])
    bias = rel_bias_matrix(tag + "_bias", w["rel_bias"])
    oa, lse = attn_a_fwd(tag + "_a", proj, kap, vap, bias, QA_BLK)
    ob = sb_fwd(tag + "_sb", proj, QS_BLK, KS_BLK, VS_BLK)
    o = jnp.concatenate([oa, ob], axis=1).astype(BF16)
    m = mm_nn(tag + "_out", o, w["w_out"], F32)
    x1, h_next = resid_norm_fwd(tag + "_res", x, m, w["g_post"], g_next)
    return x1, h_next, (x, h, proj, kap, vap, bias, oa, lse, ob, o, m)


def attn_layer_bwd(tag, dm, dx1, saved, w, prev):
    x, h, proj, kap, vap, bias, oa, lse, ob, o, m = saved
    d_w_out = mm_tn(tag + "_dwout", o, dm, F32)
    do = mm_nt(tag + "_do", dm, w["w_out"], BF16)
    dqa, dkap, dvap, dbias = attn_a_bwd(tag + "_da", proj, kap, vap, bias, oa, lse, do, QA_BLK, 0)
    dqs, dks, dvs = sb_bwd(tag + "_dsb", proj, ob, do, QS_BLK, KS_BLK, VS_BLK, N_PAIRS)
    d_rel = rel_bias_grad(tag + "_dbias", dbias)
    dproj = jnp.concatenate([dqa, dkap[PAD_KEYS:], dvap[PAD_KEYS:], dqs, dks, dvs], axis=1).astype(BF16)
    d_w_in = mm_tn_oblk(tag + "_dwin", h, dproj, w["w_in"].shape[3], F32)
    dh = mm_nt_wblk(tag + "_dh", dproj, w["w_in"], w["idx"], F32)
    dx, dg_pre, dm_prev, dg_post_prev = close_bwd(tag, dh, x, w["g_pre"], dx1, prev)
    return dx, dm_prev, dg_post_prev, dict(w_in=d_w_in, w_out=d_w_out, rel_bias=d_rel, g_pre=dg_pre)


def rg_layer_fwd(tag, x, h, w, g_next):
    proj = mm_nn_wblk(tag + "_proj", h, w["w_in"], w["idx"], F32)
    xc = conv4_fwd(tag + "_conv", proj, 1, w["conv_w"], w["conv_b"])
    a, u = rg_gates_fwd(tag + "_gates", xc, w["w_a"], w["w_i"], w["b_a"], w["b_i"], w["lam"])
    hs, z = rg_scan_fwd(tag + "_scan", a, u, proj)
    m = mm_nn(tag + "_out", z, w["w_out"], F32)
    x1, h_next = resid_norm_fwd(tag + "_res", x, m, w["g_post"], g_next)
    return x1, h_next, (x, h, proj, xc, a, hs, z, m)


def rg_layer_bwd(tag, dm, dx1, saved, w, prev):
    x, h, proj, xc, a, hs, z, m = saved
    d_w_out = mm_tn(tag + "_dwout", z, dm, F32)
    dz = mm_nt(tag + "_dz", dm, w["w_out"], F32)
    gu, dgate = rg_scan_bwd(tag + "_dscan", a, hs, proj, dz)
    dxc, d_w_a, d_w_i, d_b_a, d_b_i, d_lam = rg_gates_bwd(
        tag + "_dgates", xc, gu, hs, w["w_a"], w["w_i"], w["b_a"], w["b_i"], w["lam"])
    d_conv_w, d_conv_b = conv4_bwd_w(tag + "_dconvw", proj, 1, dxc)
    dxr = conv4_bwd_x(tag + "_dconv", dxc, w["conv_w"])
    dproj = jnp.concatenate([dgate, dxr.astype(BF16)], axis=1)
    d_w_in = mm_tn_oblk(tag + "_dwin", h, dproj, w["w_in"].shape[3], F32)
    dh = mm_nt_wblk(tag + "_dh", dproj, w["w_in"], w["idx"], F32)
    dx, dg_pre, dm_prev, dg_post_prev = close_bwd(tag, dh, x, w["g_pre"], dx1, prev)
    return dx, dm_prev, dg_post_prev, dict(w_in=d_w_in, w_out=d_w_out, conv_w=d_conv_w, conv_b=d_conv_b, w_a=d_w_a,
                                           w_i=d_w_i, b_a=d_b_a, b_i=d_b_i, lam=d_lam, g_pre=dg_pre)


def ffn_layer_fwd(tag, x, h, w, g_next):
    f8 = w["w_down"].shape[2]
    gu, a = ffn_up(tag + "_up", h, w["w_gu"], f8, w["idx"])
    f = ffn_down(tag + "_down", a, w["w_down"], w["idx"])
    x1, h_next = resid_norm_fwd(tag + "_res", x, f, w["g_post"], g_next)
    return x1, h_next, (x, h, gu, a, f)


def ffn_layer_bwd(tag, dm, dx1, saved, w, prev):
    x, h, gu, a, f = saved
    f8 = w["w_down"].shape[2]
    d_w_down = ffn_dw_down(tag + "_dwdown", a, dm)
    dgu = ffn_bwd_act(tag + "_dact", dm, w["w_down"], gu, f8, w["idx"])
    d_w_gu = ffn_dw_in(tag + "_dwgu", h, dgu)
    dh = ffn_bwd_dh(tag + "_dh", dgu, w["w_gu"], w["idx"])
    dx, dg_pre, dm_prev, dg_post_prev = close_bwd(tag, dh, x, w["g_pre"], dx1, prev)
    return dx, dm_prev, dg_post_prev, dict(w_gu=d_w_gu, w_down=d_w_down, g_pre=dg_pre)


def _place():
    return lax.axis_index("x"), lax.axis_index("y"), lax.axis_index("c")


def all_gather(name, blks):
    n = len(blks)

    def body(*refs):
        x_refs, out_refs = refs[:n], refs[n:2 * n]
        send_sems, recv_sems, local_sems = refs[2 * n:]
        x, y, cc = _place()
        me, sibling = (x, y, cc), (x, y, 1 - cc)
        chips = [(1 - x, y), (x, 1 - y), (1 - x, 1 - y)]
        south = cc == 0
        via = (jnp.where(south, 1 - x, x), jnp.where(south, y, 1 - y))
        onward = (jnp.where(south, x, 1 - x), jnp.where(south, 1 - y, y))
        k_via, k_onward = 1 + cc, 2 - cc

        def slot(a, px, py, pc):
            return out_refs[a].at[4 * px + 2 * py + pc]

        def copy(a, k, block, to, src=None):
            return pltpu.make_async_remote_copy(
                src_ref=slot(a, *block) if src is None else src, dst_ref=slot(a, *block),
                send_sem=send_sems.at[7 * a + k], recv_sem=recv_sems.at[7 * a + k], device_id=to, device_id_type=MESH)

        mine = [pltpu.make_async_copy(x_refs[a], slot(a, *me), local_sems.at[a]) for a in range(n)]
        sends = []
        for a in range(n):
            mine[a].start()
            sends.append(copy(a, 0, me, sibling, src=x_refs[a]))
            sends += [copy(a, 1 + j, me, (*chips[j], cc), src=x_refs[a]) for j in range(2)]
        for cp in sends:
            cp.start()
        for a in range(n):
            copy(a, k_via, (*via, cc), me).wait_recv()
            sends.append(copy(a, 3, (*via, cc), (*onward, cc)))
            sends.append(copy(a, 3 + k_via, (*via, cc), sibling))
            sends[-2].start()
            sends[-1].start()
        for a in range(n):
            copy(a, k_onward, (*onward, cc), me).wait_recv()
            sends.append(copy(a, 3 + k_onward, (*onward, cc), sibling))
            sends[-1].start()
        for a in range(n):
            copy(a, 3, (*chips[2], cc), me).wait_recv()
            sends.append(copy(a, 6, (*chips[2], cc), sibling))
            sends[-1].start()
        for a in range(n):
            copy(a, 0, sibling, me).wait_recv()
            for j, chip in enumerate(chips):
                copy(a, 4 + j, (*chip, 1 - cc), me).wait_recv()
        for cp in sends:
            cp.wait_send()
        for cp in mine:
            cp.wait()

    return pl.pallas_call(
        body, out_shape=[_sds((N_DEV,) + b.shape, b.dtype) for b in blks], in_specs=[ANY] * n, out_specs=[ANY] * n,
        scratch_shapes=[pltpu.SemaphoreType.DMA((7 * n,)), pltpu.SemaphoreType.DMA((7 * n,)),
                        pltpu.SemaphoreType.DMA((n,))],
        name=name)(*blks)


def exchange_pair(name, gs):
    n = len(gs)
    nchip = 4

    def body(*refs):
        g_refs, land_refs = refs[:n], refs[n:2 * n]
        send_sems, recv_sems = refs[2 * n:]
        x, y, cc = _place()
        copies = [pltpu.make_async_remote_copy(
            src_ref=g_refs[a].at[j, 1 - cc], dst_ref=land_refs[a].at[j], send_sem=send_sems.at[nchip * a + j],
            recv_sem=recv_sems.at[nchip * a + j], device_id=(x, y, 1 - cc), device_id_type=MESH)
            for a in range(n) for j in range(nchip)]
        for cp in copies:
            cp.start()
        for cp in copies:
            cp.wait()

    return pl.pallas_call(
        body, out_shape=[_sds((nchip,) + g.shape[2:], g.dtype) for g in gs], in_specs=[ANY] * n, out_specs=[ANY] * n,
        scratch_shapes=[pltpu.SemaphoreType.DMA((nchip * n,)), pltpu.SemaphoreType.DMA((nchip * n,))],
        name=name)(*gs)


def pair_sum(name, g, land, core, out_dtype):
    nchip, _, r, c = g.shape
    tr = _divisor_tile(r, 1024, 16)

    def body(core_ref, g_ref, l_ref, o_ref):
        o_ref[...] = (g_ref[...] + l_ref[...]).astype(o_ref.dtype)

    return pl.pallas_call(
        body,
        grid_spec=pltpu.PrefetchScalarGridSpec(
            num_scalar_prefetch=1, grid=(nchip, r // tr),
            in_specs=[pl.BlockSpec((None, None, tr, c), lambda j, i, core_ref: (j, core_ref[0], i, 0)),
                      pl.BlockSpec((None, tr, c), lambda j, i, core_ref: (j, i, 0))],
            out_specs=pl.BlockSpec((None, tr, c), lambda j, i, core_ref: (j, i, 0))),
        out_shape=_sds((nchip, r, c), out_dtype), compiler_params=_params("parallel", "parallel"), name=name,
    )(core, g, land)


def exchange_chips(name, ps):
    n = len(ps)

    def body(*refs):
        p_refs, land_refs = refs[:n], refs[n:2 * n]
        send_sems, recv_sems, local_sems = refs[2 * n:]
        x, y, cc = _place()
        mine = 2 * x + y
        chips = [(1 - x, y), (x, 1 - y), (1 - x, 1 - y)]
        own = [pltpu.make_async_copy(p_refs[a].at[mine], land_refs[a].at[mine], local_sems.at[a]) for a in range(n)]
        for cp in own:
            cp.start()
        sends = [pltpu.make_async_remote_copy(
            src_ref=p_refs[a].at[2 * px + py], dst_ref=land_refs[a].at[mine], send_sem=send_sems.at[3 * a + k],
            recv_sem=recv_sems.at[3 * a + k], device_id=(px, py, cc), device_id_type=MESH)
            for a in range(n) for k, (px, py) in enumerate(chips)]
        for cp in sends:
            cp.start()
        for a in range(n):
            for k, (px, py) in enumerate(chips):
                pltpu.make_async_remote_copy(
                    src_ref=p_refs[a].at[mine], dst_ref=land_refs[a].at[2 * px + py], send_sem=send_sems.at[3 * a + k],
                    recv_sem=recv_sems.at[3 * a + k], device_id=(px, py, cc), device_id_type=MESH).wait_recv()
        for cp in sends:
            cp.wait_send()
        for cp in own:
            cp.wait()

    return pl.pallas_call(
        body, out_shape=[_sds(p.shape, p.dtype) for p in ps], in_specs=[ANY] * n, out_specs=[ANY] * n,
        scratch_shapes=[pltpu.SemaphoreType.DMA((3 * n,)), pltpu.SemaphoreType.DMA((3 * n,)),
                        pltpu.SemaphoreType.DMA((n,))],
        name=name)(*ps)


def adamw(name, parts, w, m, v):
    npart, r, c = parts.shape
    tr = _divisor_tile(r, 512, 16)
    c1 = 1.0 / (1.0 - ADAM_B1 ** ADAM_STEP)
    c2 = 1.0 / (1.0 - ADAM_B2 ** ADAM_STEP)

    def body(p_ref, w_ref, m_ref, v_ref, g_ref, d_ref, nm_ref, nv_ref):
        g = p_ref[0].astype(F32)
        for j in range(1, npart):
            g = g + p_ref[j].astype(F32)
        nm = ADAM_B1 * m_ref[...] + (1.0 - ADAM_B1) * g
        nv = ADAM_B2 * v_ref[...] + (1.0 - ADAM_B2) * (g * g)
        g_ref[...] = g
        nm_ref[...] = nm
        nv_ref[...] = nv
        d_ref[...] = -ADAM_LR * ((nm * c1) / (jnp.sqrt(nv * c2) + ADAM_EPS) + ADAM_WD * w_ref[...])

    row = pl.BlockSpec((tr, c), lambda i: (i, 0))
    return pl.pallas_call(
        body, grid=(r // tr,), in_specs=[pl.BlockSpec((npart, tr, c), lambda i: (0, i, 0)), row, row, row],
        out_specs=[row] * 4, out_shape=[_sds((r, c), F32)] * 4, compiler_params=_params("parallel"), name=name,
    )(parts, w, m, v)


def _pack(arrays, dtype, row_multiple):
    flat = jnp.concatenate([a.astype(dtype).reshape(-1) for a in arrays])
    per = row_multiple * LANES
    total = -(-flat.shape[0] // per) * per
    return jnp.pad(flat, (0, total - flat.shape[0])).reshape(total // LANES, LANES)


def _pack_blocked(arrays, dtype, row_multiple):
    flat = jnp.concatenate([a.astype(dtype).reshape(N_DEV, -1) for a in arrays], axis=1)
    per = row_multiple * LANES
    total = -(-flat.shape[1] // per) * per
    return jnp.pad(flat, ((0, 0), (0, total - flat.shape[1]))).reshape(N_DEV, total // LANES, LANES)


def _unpack(buf, shapes, lead=()):
    flat = buf.reshape(lead + (-1,))
    out, off = [], 0
    for s in shapes:
        n = math.prod(s)
        out.append(flat[..., off:off + n].reshape(lead + tuple(s)))
        off += n
    return out


def _to_blocked(full, ax):
    s = full.shape
    return jnp.moveaxis(full.reshape(s[:ax] + (N_DEV, s[ax] // N_DEV) + s[ax + 1:]), ax, 0)


def _from_blocked(blk, ax):
    moved = jnp.moveaxis(blk, 0, ax)
    s = moved.shape
    return moved.reshape(s[:ax] + (s[ax] * s[ax + 1],) + s[ax + 2:])


SMALL = ("rg_conv_w", "rg_conv_b", "rg_b_a", "rg_b_i", "rg_lambda")
GU = "ffn_w_gu"
BIG = ("attn_w_in", "attn_w_out", "rg_w_in", "rg_w_a", "rg_w_i", "rg_w_out", GU, "ffn_w_down")


def kernel(x, attn_w_in, attn_rel_bias, attn_w_out, rg_w_in, rg_conv_w, rg_conv_b, rg_w_a, rg_b_a, rg_w_i, rg_b_i, rg_lambda, rg_w_out, norm_mix_pre, norm_mix_post, norm_ffn_pre, norm_ffn_post, ffn_w_gate, ffn_w_up, ffn_w_down, loss_target, m_attn_w_in, m_attn_rel_bias, m_attn_w_out, m_rg_w_in, m_rg_conv_w, m_rg_conv_b, m_rg_w_a, m_rg_b_a, m_rg_w_i, m_rg_b_i, m_rg_lambda, m_rg_w_out, m_norm_mix_pre, m_norm_mix_post, m_norm_ffn_pre, m_norm_ffn_post, m_ffn_w_gate, m_ffn_w_up, m_ffn_w_down, v_attn_w_in, v_attn_rel_bias, v_attn_w_out, v_rg_w_in, v_rg_conv_w, v_rg_conv_b, v_rg_w_a, v_rg_b_a, v_rg_w_i, v_rg_b_i, v_rg_lambda, v_rg_w_out, v_norm_mix_pre, v_norm_mix_post, v_norm_ffn_pre, v_norm_ffn_post, v_ffn_w_gate, v_ffn_w_up, v_ffn_w_down):
    w_loc = dict(attn_w_in=attn_w_in, attn_rel_bias=attn_rel_bias, attn_w_out=attn_w_out, rg_w_in=rg_w_in,
                 rg_conv_w=rg_conv_w, rg_conv_b=rg_conv_b, rg_w_a=rg_w_a, rg_b_a=rg_b_a, rg_w_i=rg_w_i, rg_b_i=rg_b_i,
                 rg_lambda=rg_lambda, rg_w_out=rg_w_out, norm_mix_pre=norm_mix_pre, norm_mix_post=norm_mix_post,
                 norm_ffn_pre=norm_ffn_pre, norm_ffn_post=norm_ffn_post, ffn_w_gate=ffn_w_gate, ffn_w_up=ffn_w_up,
                 ffn_w_down=ffn_w_down)
    m_loc = dict(attn_w_in=m_attn_w_in, attn_rel_bias=m_attn_rel_bias, attn_w_out=m_attn_w_out, rg_w_in=m_rg_w_in,
                 rg_conv_w=m_rg_conv_w, rg_conv_b=m_rg_conv_b, rg_w_a=m_rg_w_a, rg_b_a=m_rg_b_a, rg_w_i=m_rg_w_i,
                 rg_b_i=m_rg_b_i, rg_lambda=m_rg_lambda, rg_w_out=m_rg_w_out, norm_mix_pre=m_norm_mix_pre,
                 norm_mix_post=m_norm_mix_post, norm_ffn_pre=m_norm_ffn_pre, norm_ffn_post=m_norm_ffn_post,
                 ffn_w_gate=m_ffn_w_gate, ffn_w_up=m_ffn_w_up, ffn_w_down=m_ffn_w_down)
    v_loc = dict(attn_w_in=v_attn_w_in, attn_rel_bias=v_attn_rel_bias, attn_w_out=v_attn_w_out, rg_w_in=v_rg_w_in,
                 rg_conv_w=v_rg_conv_w, rg_conv_b=v_rg_conv_b, rg_w_a=v_rg_w_a, rg_b_a=v_rg_b_a, rg_w_i=v_rg_w_i,
                 rg_b_i=v_rg_b_i, rg_lambda=v_rg_lambda, rg_w_out=v_rg_w_out, norm_mix_pre=v_norm_mix_pre,
                 norm_mix_post=v_norm_mix_post, norm_ffn_pre=v_norm_ffn_pre, norm_ffn_post=v_norm_ffn_post,
                 ffn_w_gate=v_ffn_w_gate, ffn_w_up=v_ffn_w_up, ffn_w_down=v_ffn_w_down)
    axis_of = dict(SHARDED)
    xt, target = x[0], loss_target[0]
    d_model = xt.shape[1]
    rows2d = lambda a: a.reshape(-1, a.shape[-1])
    small_shapes = [w_loc[n].shape for n in SMALL]
    f8 = ffn_w_gate.shape[-1]
    for d in (w_loc, m_loc, v_loc):
        d[GU] = merge_gu(d["ffn_w_gate"], d["ffn_w_up"])

    gathered = all_gather("gather_weights", [rows2d(w_loc[n]).astype(BF16) for n in BIG]
                          + [_pack([w_loc[n] for n in SMALL], F32, 8)])
    blocked = {n: g.reshape((N_DEV,) + w_loc[n].shape) for n, g in zip(BIG, gathered)}
    blocked.update(zip(SMALL, _unpack(gathered[-1], small_shapes, (N_DEV,))))
    full = {n: _from_blocked(blocked[n], axis_of[n]) for n in SMALL}
    row = lambda a: a.reshape(1, -1).astype(F32)
    square = lambda rows8: rows8.reshape(-1, rows8.shape[-1])
    gates = lambda g: jnp.swapaxes(g, 0, 1).reshape(LRU_BLOCKS, -1, g.shape[-1])

    def layer_weights(layer):
        j = layer // 2
        norms = dict(g_pre=row(norm_mix_pre[layer]), g_post=row(norm_mix_post[layer]), idx=j)
        if layer % 2 == 0:
            mix = dict(w_in=blocked["attn_w_in"], w_out=square(blocked["attn_w_out"][:, j]),
                       rel_bias=attn_rel_bias[j], **norms)
        else:
            mix = dict(w_in=blocked["rg_w_in"], w_out=square(blocked["rg_w_out"][:, j]),
                       conv_w=full["rg_conv_w"][j][:, 0, :], conv_b=row(full["rg_conv_b"][j]),
                       w_a=gates(blocked["rg_w_a"][:, j]), w_i=gates(blocked["rg_w_i"][:, j]),
                       b_a=row(full["rg_b_a"][j]), b_i=row(full["rg_b_i"][j]), lam=row(full["rg_lambda"][j]), **norms)
        ffn = dict(w_gu=blocked[GU], w_down=blocked["ffn_w_down"], idx=layer,
                   g_pre=row(norm_ffn_pre[layer]), g_post=row(norm_ffn_post[layer]))
        return mix, ffn

    weights = [layer_weights(layer) for layer in range(DEPTH)]
    act, tape = xt, []
    h = rmsnorm_fwd("l0_mix_norm", act, weights[0][0]["g_pre"])
    for layer in range(DEPTH):
        mix_w, ffn_w = weights[layer]
        mixer_fwd = attn_layer_fwd if layer % 2 == 0 else rg_layer_fwd
        act, h, saved_mix = mixer_fwd(f"l{layer}_mix", act, h, mix_w, ffn_w["g_pre"])
        g_next = weights[layer + 1][0]["g_pre"] if layer + 1 < DEPTH else None
        act, h, saved_ffn = ffn_layer_fwd(f"l{layer}_ffn", act, h, ffn_w, g_next)
        tape.append((mix_w, ffn_w, saved_mix, saved_ffn))
    dact, sq = loss_grad("loss", act, target)
    loss_part = (0.5 * jnp.sum(sq) / d_model).reshape(1)

    grads = {}
    last = tape[DEPTH - 1]
    dm, grads[("ffn_post", DEPTH - 1)] = norm_bwd(f"l{DEPTH - 1}_ffn_dpost", dact, last[3][-1], last[1]["g_post"],
                                                 None, BF16)
    for layer in reversed(range(DEPTH)):
        mix_w, ffn_w, saved_mix, saved_ffn = tape[layer]
        dact, dm, grads[("mix_post", layer)], grads[("ffn", layer)] = ffn_layer_bwd(
            f"l{layer}_ffn", dm, dact, saved_ffn, ffn_w, (saved_mix[-1], mix_w["g_post"]))
        mixer_bwd = attn_layer_bwd if layer % 2 == 0 else rg_layer_bwd
        prev = (tape[layer - 1][3][-1], tape[layer - 1][1]["g_post"]) if layer else None
        dact, dm, grads[("ffn_post", layer - 1)], grads[("mix", layer)] = mixer_bwd(
            f"l{layer}_mix", dm, dact, saved_mix, mix_w, prev)
    attn_g = [grads[("mix", l)] for l in range(0, DEPTH, 2)]
    rg_g = [grads[("mix", l)] for l in range(1, DEPTH, 2)]
    ffn_g = [grads[("ffn", l)] for l in range(DEPTH)]
    stack = lambda gs, key: jnp.stack([g[key] for g in gs])
    by_owner = lambda gs, key, f: jnp.stack([f(g[key]) for g in gs], axis=1)
    rows8 = lambda a: a.reshape(N_DEV, -1, a.shape[-1])
    ungates = lambda a: jnp.swapaxes(a.reshape(LRU_BLOCKS, N_DEV, -1, a.shape[-1]), 0, 1)
    same = lambda a: a
    blocked_g = dict(
        attn_w_in=by_owner(attn_g, "w_in", same), attn_w_out=by_owner(attn_g, "w_out", rows8),
        rg_w_in=by_owner(rg_g, "w_in", same), rg_w_out=by_owner(rg_g, "w_out", rows8),
        rg_w_a=by_owner(rg_g, "w_a", ungates), rg_w_i=by_owner(rg_g, "w_i", ungates),
        **{GU: by_owner(ffn_g, "w_gu", same)},
        ffn_w_down=by_owner(ffn_g, "w_down", same))
    contrib = dict(
        attn_rel_bias=stack(attn_g, "rel_bias"), rg_conv_w=stack(rg_g, "conv_w")[:, :, None, :],
        rg_conv_b=stack(rg_g, "conv_b")[:, 0], rg_b_a=stack(rg_g, "b_a").reshape(rg_b_a.shape[0], LRU_BLOCKS, -1),
        rg_b_i=stack(rg_g, "b_i").reshape(rg_b_i.shape[0], LRU_BLOCKS, -1), rg_lambda=stack(rg_g, "lam")[:, 0],
        norm_mix_pre=jnp.concatenate([grads[("mix", l)]["g_pre"] for l in range(DEPTH)]),
        norm_mix_post=jnp.concatenate([grads[("mix_post", l)] for l in range(DEPTH)]),
        norm_ffn_pre=jnp.concatenate([g["g_pre"] for g in ffn_g]),
        norm_ffn_post=jnp.concatenate([grads[("ffn_post", l)] for l in range(DEPTH)]),
    )
    small_g = _pack_blocked([_to_blocked(contrib[n], axis_of[n]) for n in SMALL], F32, 8)

    slabs = [blocked_g[n].reshape(4, 2, -1, blocked_g[n].shape[-1]) for n in BIG] + [small_g.reshape(4, 2, -1, LANES)]
    core = lax.axis_index("c").astype(jnp.int32).reshape(1)
    from_sibling = exchange_pair("rs_pair", slabs)
    pairs = [pair_sum(f"rs_pair_sum_{i}", g, l, core, BF16 if i < len(BIG) else F32)
             for i, (g, l) in enumerate(zip(slabs, from_sibling))]
    by_chip = exchange_chips("rs_chips", pairs)
    result = {}
    kinds = ("grad", "delta", "new_m", "new_v")
    for n, parts in zip(BIG, by_chip):
        outs = adamw("adamw_" + n, parts, *[rows2d(d[n]) for d in (w_loc, m_loc, v_loc)])
        for kind, a in zip(kinds, outs):
            result[(kind, n)] = a.reshape(w_loc[n].shape)
    for kind in kinds:
        result[(kind, "ffn_w_gate")], result[(kind, "ffn_w_up")] = split_gu(result.pop((kind, GU)), f8)
    outs = adamw("adamw_small", by_chip[-1], *[_pack([d[n] for n in SMALL], F32, 8) for d in (w_loc, m_loc, v_loc)])
    for kind, buf in zip(kinds, outs):
        result.update({(kind, n): a for n, a in zip(SMALL, _unpack(buf, small_shapes))})
    rep_shapes = [w_loc[n].shape for n in REPLICATED] + [(1,)]
    rep_parts, = all_gather("gather_rep_grads", [_pack([contrib[n] for n in REPLICATED] + [loss_part], F32, 8)])
    outs = adamw("adamw_replicated", rep_parts, *[_pack([d[n] for n in REPLICATED] + [jnp.zeros((1,), F32)], F32, 8)
                                                  for d in (w_loc, m_loc, v_loc)])
    for kind, buf in zip(kinds, outs):
        result.update({(kind, n): a for n, a in zip(REPLICATED + ("loss",), _unpack(buf, rep_shapes))})
    loss = result[("grad", "loss")][0]
    return (loss, dact[None], *[result[(kind, n)] for kind in kinds for n in WEIGHTS])
```

```python
import functools
import math

import jax
import jax.numpy as jnp
from jax import lax
from jax.experimental import pallas as pl
from jax.experimental.pallas import tpu as pltpu

F32 = jnp.float32
BF16 = jnp.bfloat16

N_DEV = 8
DEPTH = 4
CHUNK = 64
N_LEFT = 8
BAND = (N_LEFT + 1) * CHUNK
PAD_KEYS = N_LEFT * CHUNK
HEAD_DIM = 64
N_HEADS = 8
REL_CLIP = 256
LRU_BLOCKS = 4
LRU_C = 8.0
RMS_EPS = 1e-6
QK_SCALE = HEAD_DIM ** -0.5

ADAM_LR = 0.001
ADAM_B1 = 0.9
ADAM_B2 = 0.999
ADAM_EPS = 1e-08
ADAM_WD = 0.01
ADAM_STEP = 10

LANES = 1024
V7X_VMEM_LIMIT = 56 * 1024 * 1024

MESH = pl.DeviceIdType.MESH
ANY = pl.BlockSpec(memory_space=pl.ANY)

SHARDED = (
    ("attn_w_in", 2), ("attn_w_out", 1), ("rg_w_in", 2), ("rg_conv_w", 3), ("rg_conv_b", 1),
    ("rg_w_a", 2), ("rg_b_a", 2), ("rg_w_i", 2), ("rg_b_i", 2), ("rg_lambda", 1), ("rg_w_out", 1),
    ("ffn_w_gate", 2), ("ffn_w_up", 2), ("ffn_w_down", 1),
)
REPLICATED = ("attn_rel_bias", "norm_mix_pre", "norm_mix_post", "norm_ffn_pre", "norm_ffn_post")
WEIGHTS = ("attn_w_in", "attn_rel_bias", "attn_w_out", "rg_w_in", "rg_conv_w", "rg_conv_b", "rg_w_a", "rg_b_a",
           "rg_w_i", "rg_b_i", "rg_lambda", "rg_w_out", "norm_mix_pre", "norm_mix_post", "norm_ffn_pre",
           "norm_ffn_post", "ffn_w_gate", "ffn_w_up", "ffn_w_down")


def _params(*dims):
    return pltpu.CompilerParams(dimension_semantics=dims or None, vmem_limit_bytes=V7X_VMEM_LIMIT)


def _sds(shape, dtype):
    return jax.ShapeDtypeStruct(tuple(shape), dtype)


def _row_tile(n, pref):
    t = min(n, pref)
    assert n % t == 0, (n, pref)
    return t


def _divisor_tile(n, limit, multiple):
    if n <= limit:
        return n
    best = max(t for t in range(multiple, limit + 1, multiple) if n % t == 0)
    return best


NN = (((1,), (0,)), ((), ()))
NT = (((1,), (1,)), ((), ()))
TN = (((0,), (0,)), ((), ()))


def _gmm(name, a, b, *, grid, a_blk, a_idx, b_blk, b_idx, o_blk, o_idx, out_shape, out_dtype, dn, acc_shape):
    nk = grid[-1]
    kax = len(grid) - 1

    def body(a_ref, b_ref, o_ref, acc_ref):
        part = lax.dot_general(a_ref[...], b_ref[...], dn, preferred_element_type=F32)
        if nk == 1:
            o_ref[...] = part.astype(o_ref.dtype)
            return
        k = pl.program_id(kax)

        @pl.when(k == 0)
        def _():
            acc_ref[...] = part

        @pl.when(k > 0)
        def _():
            acc_ref[...] += part

        @pl.when(k == nk - 1)
        def _():
            o_ref[...] = acc_ref[...].astype(o_ref.dtype)

    return pl.pallas_call(
        body, grid=grid,
        in_specs=[pl.BlockSpec(a_blk, a_idx), pl.BlockSpec(b_blk, b_idx)],
        out_specs=pl.BlockSpec(o_blk, o_idx),
        out_shape=_sds(out_shape, out_dtype),
        scratch_shapes=[pltpu.VMEM(acc_shape, F32)],
        compiler_params=_params(*(["parallel"] * kax + ["arbitrary"])),
        name=name,
    )(a, b)


def mm_nn(name, a, b, out_dtype, tm=1024, tn=512, tk=1024):
    (m, k), (_, n) = a.shape, b.shape
    tm, tn, tk = _row_tile(m, tm), _row_tile(n, tn), _row_tile(k, tk)
    return _gmm(name, a, b, grid=(m // tm, n // tn, k // tk),
                a_blk=(tm, tk), a_idx=lambda i, j, kk: (i, kk), b_blk=(tk, tn), b_idx=lambda i, j, kk: (kk, j),
                o_blk=(tm, tn), o_idx=lambda i, j, kk: (i, j), out_shape=(m, n), out_dtype=out_dtype, dn=NN,
                acc_shape=(tm, tn))


def mm_nt(name, a, b, out_dtype, tm=1024, tn=512, tk=1024):
    (m, k), (n, _) = a.shape, b.shape
    tm, tn, tk = _row_tile(m, tm), _row_tile(n, tn), _row_tile(k, tk)
    return _gmm(name, a, b, grid=(m // tm, n // tn, k // tk),
                a_blk=(tm, tk), a_idx=lambda i, j, kk: (i, kk), b_blk=(tn, tk), b_idx=lambda i, j, kk: (j, kk),
                o_blk=(tm, tn), o_idx=lambda i, j, kk: (i, j), out_shape=(m, n), out_dtype=out_dtype, dn=NT,
                acc_shape=(tm, tn))


def mm_tn(name, a, b, out_dtype, tm=1024, tn=512, tk=2048):
    (k, m), (_, n) = a.shape, b.shape
    tm, tn, tk = _row_tile(m, tm), _row_tile(n, tn), _row_tile(k, tk)
    return _gmm(name, a, b, grid=(m // tm, n // tn, k // tk),
                a_blk=(tk, tm), a_idx=lambda i, j, kk: (kk, i), b_blk=(tk, tn), b_idx=lambda i, j, kk: (kk, j),
                o_blk=(tm, tn), o_idx=lambda i, j, kk: (i, j), out_shape=(m, n), out_dtype=out_dtype, dn=TN,
                acc_shape=(tm, tn))


MXU_WIDTH = 256


def _blocks_per_step(n8):
    return 1 if n8 % MXU_WIDTH == 0 else 2


def _side_by_side(w_ref, j0, per):
    return w_ref[j0] if per == 1 else jnp.concatenate([w_ref[j0 + j] for j in range(per)], axis=1)


def mm_nn_wblk(name, a, wb, layer, out_dtype, tm=1024):
    (m, k), (nb, _, _, n8) = a.shape, wb.shape
    tm = _row_tile(m, tm)
    per = _blocks_per_step(n8)

    def body(a_ref, w_ref, o_ref):
        o_ref[...] = jnp.dot(a_ref[...], _side_by_side(w_ref, 0, per), preferred_element_type=F32).astype(o_ref.dtype)

    return pl.pallas_call(
        body, grid=(m // tm, nb // per),
        in_specs=[pl.BlockSpec((tm, k), lambda i, j: (i, 0)),
                  pl.BlockSpec((per, None, k, n8), lambda i, j: (j, layer, 0, 0))],
        out_specs=pl.BlockSpec((tm, per * n8), lambda i, j: (i, j)), out_shape=_sds((m, nb * n8), out_dtype),
        compiler_params=_params("parallel", "parallel"), name=name)(a, wb)


def mm_nt_wblk(name, a, wb, layer, out_dtype, tm=1024, tn=512):
    m = a.shape[0]
    nb, _, k, n8 = wb.shape
    tm, tn = _row_tile(m, tm), _row_tile(k, tn)
    per = _blocks_per_step(n8)

    def body(a_ref, b_ref, o_ref):
        acc = None
        for j in range(0, nb, per):
            part = lax.dot_general(a_ref[:, j * n8:(j + per) * n8], _side_by_side(b_ref, j, per), NT,
                                   preferred_element_type=F32)
            acc = part if acc is None else acc + part
        o_ref[...] = acc.astype(o_ref.dtype)

    return pl.pallas_call(
        body, grid=(m // tm, k // tn),
        in_specs=[pl.BlockSpec((tm, nb * n8), lambda i, j: (i, 0)),
                  pl.BlockSpec((nb, None, tn, n8), lambda i, j: (0, layer, j, 0))],
        out_specs=pl.BlockSpec((tm, tn), lambda i, j: (i, j)), out_shape=_sds((m, k), out_dtype),
        compiler_params=_params("parallel", "parallel"), name=name)(a, wb)


def mm_tn_oblk(name, a, b, n8, out_dtype, tk=2048):
    (t, k), nb = a.shape, b.shape[1] // n8
    tk = _row_tile(t, tk)
    per = _blocks_per_step(n8)
    nk = t // tk

    def body(a_ref, b_ref, o_ref, acc_ref):
        s = pl.program_id(1)
        part = lax.dot_general(a_ref[...], b_ref[...], TN, preferred_element_type=F32)

        @pl.when(s == 0)
        def _():
            acc_ref[...] = part

        @pl.when(s > 0)
        def _():
            acc_ref[...] += part

        @pl.when(s == nk - 1)
        def _():
            for j in range(per):
                o_ref[j] = acc_ref[:, j * n8:(j + 1) * n8].astype(o_ref.dtype)

    return pl.pallas_call(
        body, grid=(nb // per, nk),
        in_specs=[pl.BlockSpec((tk, k), lambda j, s: (s, 0)), pl.BlockSpec((tk, per * n8), lambda j, s: (s, j))],
        out_specs=pl.BlockSpec((per, k, n8), lambda j, s: (j, 0, 0)), out_shape=_sds((nb, k, n8), out_dtype),
        scratch_shapes=[pltpu.VMEM((k, per * n8), F32)],
        compiler_params=_params("parallel", "arbitrary"), name=name)(a, b)


def rmsnorm_fwd(name, x, g):
    t, d = x.shape
    tr = _row_tile(t, 512)

    def body(x_ref, g_ref, o_ref):
        xv = x_ref[...]
        r = lax.rsqrt(jnp.mean(xv * xv, axis=-1, keepdims=True) + RMS_EPS)
        o_ref[...] = (xv * r * g_ref[...]).astype(o_ref.dtype)

    return pl.pallas_call(
        body, grid=(t // tr,),
        in_specs=[pl.BlockSpec((tr, d), lambda i: (i, 0)), pl.BlockSpec((1, d), lambda i: (0, 0))],
        out_specs=pl.BlockSpec((tr, d), lambda i: (i, 0)),
        out_shape=_sds((t, d), BF16), compiler_params=_params("parallel"), name=name)(x, g)


def resid_norm_fwd(name, x, m, g, g_next):
    t, d = x.shape
    tr = _row_tile(t, 512)
    chained = g_next is not None

    def body(*refs):
        x_ref, m_ref, g_ref = refs[:3]
        mv = m_ref[...]
        r = lax.rsqrt(jnp.mean(mv * mv, axis=-1, keepdims=True) + RMS_EPS)
        x1 = x_ref[...] + mv * r * g_ref[...]
        if chained:
            gn_ref, o_ref, h_ref = refs[3:]
            r1 = lax.rsqrt(jnp.mean(x1 * x1, axis=-1, keepdims=True) + RMS_EPS)
            h_ref[...] = (x1 * r1 * gn_ref[...]).astype(BF16)
        else:
            o_ref, = refs[3:]
        o_ref[...] = x1

    row = pl.BlockSpec((tr, d), lambda i: (i, 0))
    vec = pl.BlockSpec((1, d), lambda i: (0, 0))
    out = pl.pallas_call(
        body, grid=(t // tr,),
        in_specs=[row, row, vec] + ([vec] if chained else []),
        out_specs=[row, row] if chained else [row],
        out_shape=[_sds((t, d), F32)] + ([_sds((t, d), BF16)] if chained else []),
        compiler_params=_params("parallel"), name=name)(*([x, m, g] + ([g_next] if chained else [])))
    return (out[0], out[1]) if chained else (out[0], None)


def norm_bwd(name, dy, x, g, resid, out_dtype):
    t, d = x.shape
    tr = _row_tile(t, 512)
    has_res = resid is not None

    def body(*refs):
        if has_res:
            dy_ref, x_ref, g_ref, r_ref, dx_ref, dg_ref = refs
        else:
            dy_ref, x_ref, g_ref, dx_ref, dg_ref = refs
        i = pl.program_id(0)
        xv = x_ref[...]
        dyv = dy_ref[...].astype(F32)
        r = lax.rsqrt(jnp.mean(xv * xv, axis=-1, keepdims=True) + RMS_EPS)
        xh = xv * r
        dxh = dyv * g_ref[...]
        dx = r * (dxh - xh * jnp.mean(dxh * xh, axis=-1, keepdims=True))
        if has_res:
            dx = dx + r_ref[...]
        dx_ref[...] = dx.astype(dx_ref.dtype)
        part = jnp.sum(dyv * xh, axis=0, keepdims=True)

        @pl.when(i == 0)
        def _():
            dg_ref[...] = part

        @pl.when(i > 0)
        def _():
            dg_ref[...] += part

    row = pl.BlockSpec((tr, d), lambda i: (i, 0))
    vec = pl.BlockSpec((1, d), lambda i: (0, 0))
    ins = [dy, x, g] + ([resid] if has_res else [])
    return pl.pallas_call(
        body, grid=(t // tr,),
        in_specs=[row, row, vec] + ([row] if has_res else []),
        out_specs=[row, vec],
        out_shape=[_sds((t, d), out_dtype), _sds((1, d), F32)],
        compiler_params=_params("arbitrary"), name=name)(*ins)


def norm_bwd_chain(name, dh, x, g, resid, m_prev, g_prev):
    t, d = x.shape
    tr = _row_tile(t, 512)

    def body(dh_ref, x_ref, g_ref, r_ref, m_ref, gp_ref, dx_ref, dg_ref, dm_ref, dgp_ref):
        i = pl.program_id(0)
        xv = x_ref[...]
        dhv = dh_ref[...]
        r = lax.rsqrt(jnp.mean(xv * xv, axis=-1, keepdims=True) + RMS_EPS)
        xh = xv * r
        dxh = dhv * g_ref[...]
        dx = r * (dxh - xh * jnp.mean(dxh * xh, axis=-1, keepdims=True)) + r_ref[...]
        dx_ref[...] = dx
        mv = m_ref[...]
        rm = lax.rsqrt(jnp.mean(mv * mv, axis=-1, keepdims=True) + RMS_EPS)
        mh = mv * rm
        dmh = dx * gp_ref[...]
        dm_ref[...] = (rm * (dmh - mh * jnp.mean(dmh * mh, axis=-1, keepdims=True))).astype(BF16)
        part = jnp.sum(dhv * xh, axis=0, keepdims=True)
        part_prev = jnp.sum(dx * mh, axis=0, keepdims=True)

        @pl.when(i == 0)
        def _():
            dg_ref[...] = part
            dgp_ref[...] = part_prev

        @pl.when(i > 0)
        def _():
            dg_ref[...] += part
            dgp_ref[...] += part_prev

    row = pl.BlockSpec((tr, d), lambda i: (i, 0))
    vec = pl.BlockSpec((1, d), lambda i: (0, 0))
    return pl.pallas_call(
        body, grid=(t // tr,), in_specs=[row, row, vec, row, row, vec], out_specs=[row, vec, row, vec],
        out_shape=[_sds((t, d), F32), _sds((1, d), F32), _sds((t, d), BF16), _sds((1, d), F32)],
        compiler_params=_params("arbitrary"), name=name)(dh, x, g, resid, m_prev, g_prev)


def close_bwd(tag, dh, x, g_pre, dx1, prev):
    if prev is None:
        dx, dg_pre = norm_bwd(tag + "_dpre", dh, x, g_pre, dx1, F32)
        return dx, dg_pre, None, None
    return norm_bwd_chain(tag + "_dpre", dh, x, g_pre, dx1, *prev)


def loss_grad(name, y, target):
    t, d = y.shape
    tr = _row_tile(t, 512)

    def body(y_ref, t_ref, dy_ref, s_ref):
        i = pl.program_id(0)
        err = y_ref[...] - t_ref[...]
        dy_ref[...] = err * (1.0 / d)
        part = jnp.sum(err * err, axis=0, keepdims=True)

        @pl.when(i == 0)
        def _():
            s_ref[...] = part

        @pl.when(i > 0)
        def _():
            s_ref[...] += part

    row = pl.BlockSpec((tr, d), lambda i: (i, 0))
    vec = pl.BlockSpec((1, d), lambda i: (0, 0))
    return pl.pallas_call(
        body, grid=(t // tr,), in_specs=[row, row], out_specs=[row, vec],
        out_shape=[_sds((t, d), F32), _sds((1, d), F32)],
        compiler_params=_params("arbitrary"), name=name)(y, target)


PAIR = 2 * HEAD_DIM
N_PAIRS = N_HEADS // 2
A_TQ = 512
A_UNROLL_FWD = 8
A_UNROLL_BWD = 4


def _halves(x):
    lane = lax.broadcasted_iota(jnp.int32, x.shape, x.ndim - 1)
    zero = jnp.zeros_like(x)
    return jnp.where(lane < HEAD_DIM, x, zero), jnp.where(lane >= HEAD_DIM, x, zero)


def _merge(a, b):
    lane = lax.broadcasted_iota(jnp.int32, a.shape, a.ndim - 1)
    return jnp.where(lane < HEAD_DIM, a, b)


def _a_valid(c):
    col = lax.broadcasted_iota(jnp.int32, (CHUNK, BAND), 1)
    return col >= (N_LEFT - c) * CHUNK


def attn_a_fwd(name, proj, kp, vp, bias, q_blk):
    t = proj.shape[0]
    tq = _row_tile(t, A_TQ)
    ncs = tq // CHUNK
    un = math.gcd(A_UNROLL_FWD, ncs)

    def body(q_ref, k_ref, v_ref, b_ref, o_ref, l_ref):
        i = pl.program_id(1)

        def group(gg, carry):
            cs = [i * ncs + gg * un + u for u in range(un)]
            r0s = [pl.multiple_of((gg * un + u) * CHUNK, CHUNK) for u in range(un)]
            k0s = [pl.multiple_of(c * CHUNK, CHUNK) for c in cs]
            ss = []
            for u in range(un):
                qh = _halves(q_ref[pl.ds(r0s[u], CHUNK), :] * QK_SCALE)
                kwin = k_ref[pl.ds(k0s[u], BAND), :]
                valid = _a_valid(cs[u])
                for hh in range(2):
                    s = lax.dot_general(qh[hh], kwin, NT, preferred_element_type=F32) + b_ref[hh]
                    ss.append(jnp.where(valid, s, -1e30))
            ps, lses = [], []
            for s in ss:
                mx = jnp.max(s, axis=-1, keepdims=True)
                p = jnp.exp(s - mx)
                den = jnp.sum(p, axis=-1, keepdims=True)
                ps.append((p * (1.0 / den)).astype(BF16))
                lses.append(mx + jnp.log(den))
            for u in range(un):
                vwin = v_ref[pl.ds(k0s[u], BAND), :]
                o0 = jnp.dot(ps[2 * u], vwin, preferred_element_type=F32)
                o1 = jnp.dot(ps[2 * u + 1], vwin, preferred_element_type=F32)
                o_ref[pl.ds(r0s[u], CHUNK), :] = _merge(o0, o1)
                l_ref[pl.ds(r0s[u], CHUNK), :] = jnp.concatenate([lses[2 * u], lses[2 * u + 1]], axis=1)
            return carry

        lax.fori_loop(0, ncs // un, group, 0)

    return pl.pallas_call(
        body, grid=(N_PAIRS, t // tq),
        in_specs=[pl.BlockSpec((tq, PAIR), lambda p, i: (i, q_blk + p)),
                  pl.BlockSpec((t + PAD_KEYS, PAIR), lambda p, i: (0, p)),
                  pl.BlockSpec((t + PAD_KEYS, PAIR), lambda p, i: (0, p)),
                  pl.BlockSpec((2, CHUNK, BAND), lambda p, i: (p, 0, 0))],
        out_specs=[pl.BlockSpec((tq, PAIR), lambda p, i: (i, p)),
                   pl.BlockSpec((None, tq, 2), lambda p, i: (p, i, 0))],
        out_shape=[_sds((t, N_PAIRS * PAIR), F32), _sds((N_PAIRS, t, 2), F32)],
        compiler_params=_params("parallel", "parallel"), name=name)(proj, kp, vp, bias)


def attn_a_bwd(name, proj, kp, vp, bias, o, lse, do, q_blk, do_blk):
    t = proj.shape[0]
    tq = _row_tile(t, A_TQ)
    ncs = tq // CHUNK
    un = math.gcd(A_UNROLL_BWD, ncs)

    def body(q_ref, k_ref, v_ref, b_ref, o_ref, l_ref, do_ref, dq_ref, dk_ref, dv_ref, db_ref):
        i = pl.program_id(1)

        @pl.when(i == 0)
        def _():
            dk_ref[...] = jnp.zeros_like(dk_ref)
            dv_ref[...] = jnp.zeros_like(dv_ref)
            db_ref[...] = jnp.zeros_like(db_ref)

        def group(gg, carry):
            cs = [i * ncs + gg * un + u for u in range(un)]
            r0s = [pl.multiple_of((gg * un + u) * CHUNK, CHUNK) for u in range(un)]
            k0s = [pl.multiple_of(c * CHUNK, CHUNK) for c in cs]
            qhs, dohs, ps, dps, deltas = [], [], [], [], []
            for u in range(un):
                rows = pl.ds(r0s[u], CHUNK)
                qh = _halves(q_ref[rows, :] * QK_SCALE)
                doh = _halves(do_ref[rows, :])
                kwin = k_ref[pl.ds(k0s[u], BAND), :]
                vwin = v_ref[pl.ds(k0s[u], BAND), :]
                valid = _a_valid(cs[u])
                dl = _halves(do_ref[rows, :].astype(F32) * o_ref[rows, :])
                for hh in range(2):
                    s = lax.dot_general(qh[hh], kwin, NT, preferred_element_type=F32) + b_ref[hh]
                    ps.append(jnp.where(valid, jnp.exp(s - l_ref[rows, hh:hh + 1]), 0.0))
                    dps.append(lax.dot_general(doh[hh], vwin, NT, preferred_element_type=F32))
                    deltas.append(jnp.sum(dl[hh], axis=-1, keepdims=True))
                qhs.append(qh)
                dohs.append(doh)
            dss = [p * (dp - dl) for p, dp, dl in zip(ps, dps, deltas)]
            for hh in range(2):
                tot = dss[hh]
                for u in range(1, un):
                    tot = tot + dss[2 * u + hh]
                db_ref[hh] += tot
            for u in range(un):
                kwin = k_ref[pl.ds(k0s[u], BAND), :]
                ds0, ds1 = dss[2 * u].astype(BF16), dss[2 * u + 1].astype(BF16)
                dq_ref[pl.ds(r0s[u], CHUNK), :] = _merge(jnp.dot(ds0, kwin, preferred_element_type=F32),
                                                         jnp.dot(ds1, kwin, preferred_element_type=F32)) * QK_SCALE
                dk_ref[pl.ds(k0s[u], BAND), :] += (lax.dot_general(ds0, qhs[u][0], TN, preferred_element_type=F32)
                                                   + lax.dot_general(ds1, qhs[u][1], TN, preferred_element_type=F32))
                dv_ref[pl.ds(k0s[u], BAND), :] += (
                    lax.dot_general(ps[2 * u].astype(BF16), dohs[u][0], TN, preferred_element_type=F32)
                    + lax.dot_general(ps[2 * u + 1].astype(BF16), dohs[u][1], TN, preferred_element_type=F32))
            return carry

        lax.fori_loop(0, ncs // un, group, 0)

    tile = lambda blk: pl.BlockSpec((tq, PAIR), lambda p, i: (i, blk + p))
    whole = pl.BlockSpec((t + PAD_KEYS, PAIR), lambda p, i: (0, p))
    bspec = pl.BlockSpec((2, CHUNK, BAND), lambda p, i: (p, 0, 0))
    return pl.pallas_call(
        body, grid=(N_PAIRS, t // tq),
        in_specs=[tile(q_blk), whole, whole, bspec, tile(0), pl.BlockSpec((None, tq, 2), lambda p, i: (p, i, 0)),
                  tile(do_blk)],
        out_specs=[tile(0), whole, whole, bspec],
        out_shape=[_sds((t, N_PAIRS * PAIR), F32), _sds((t + PAD_KEYS, N_PAIRS * PAIR), F32),
                   _sds((t + PAD_KEYS, N_PAIRS * PAIR), F32), _sds((2 * N_PAIRS, CHUNK, BAND), F32)],
        compiler_params=_params("parallel", "arbitrary"), name=name)(proj, kp, vp, bias, o, lse, do)


SB_TQ = 256
SB_TK = 256
SB_DEAD = -125.0


def _tri(n, strict):
    j = lax.broadcasted_iota(jnp.int32, (n, n), 0)
    s = lax.broadcasted_iota(jnp.int32, (n, n), 1)
    return jnp.where((j > s) if strict else (j >= s), 1.0, 0.0).astype(BF16)


def _suffix_sum(x, tri, exact):
    hi = x.astype(BF16)
    out = jnp.dot(hi, tri, preferred_element_type=F32)
    if exact:
        lo = (x - hi.astype(F32)).astype(BF16)
        out = out + jnp.dot(lo, tri, preferred_element_type=F32)
    return out


def _sb_scores(qh, ks, causal):
    z = lax.dot_general(qh, ks, NT, preferred_element_type=F32)
    lb = jnp.minimum(z, 0.0) - jnp.log(1.0 + jnp.exp(-jnp.abs(z)))
    m = lb - z
    if causal is not None:
        m = jnp.where(causal, m, 0.0)
    return lb, m


def _causal(tq, tk, off):
    return (lax.broadcasted_iota(jnp.int32, (tq, tk), 1) + off * tk) < lax.broadcasted_iota(jnp.int32, (tq, tk), 0)


def sb_fwd(name, proj, q_blk, k_blk, v_blk):
    t = proj.shape[0]
    tq = _row_tile(t, SB_TQ)
    tk = min(SB_TK, tq)
    per = tq // tk

    def body(q_ref, k_ref, v_ref, o_ref):
        i = pl.program_id(1)
        tri = _tri(tk, True)
        qh = _halves(q_ref[...] * QK_SCALE)

        def blocks(kb, carry, off):
            k0 = pl.multiple_of(kb * tk, tk)
            ks, vs = k_ref[pl.ds(k0, tk), :], v_ref[pl.ds(k0, tk), :]
            causal = None if off is None else _causal(tq, tk, off)
            lbm = [_sb_scores(qh[hh], ks, causal) for hh in range(2)]
            afters = [_suffix_sum(lbm[hh][1], tri, False) for hh in range(2)]
            out = []
            for hh in range(2):
                acc, cm = carry[2 * hh], carry[2 * hh + 1]
                w = jnp.exp(lbm[hh][0] + afters[hh] + cm)
                if causal is not None:
                    w = jnp.where(causal, w, 0.0)
                out += [acc + jnp.dot(w.astype(BF16), vs, preferred_element_type=F32),
                        cm + jnp.sum(lbm[hh][1], axis=-1, keepdims=True)]
            return tuple(out)

        def alive(carry):
            return jnp.maximum(jnp.max(carry[1]), jnp.max(carry[3])) > SB_DEAD

        carry = (jnp.zeros((tq, PAIR), F32), jnp.zeros((tq, 1), F32)) * 2
        for off in reversed(range(per)):
            carry = blocks(i * per + off, carry, off)

        def step(c):
            new = blocks(i * per - 1 - c[0], c[2:], None)
            return (c[0] + 1, alive(new)) + new

        out = lax.while_loop(lambda c: jnp.logical_and(c[0] < i * per, c[1]), step,
                             (jnp.int32(0), alive(carry)) + carry)
        o_ref[...] = _merge(out[2], out[4])

    return pl.pallas_call(
        body, grid=(N_PAIRS, t // tq),
        in_specs=[pl.BlockSpec((tq, PAIR), lambda p, i: (i, q_blk + p)),
                  pl.BlockSpec((t, PAIR), lambda p, i: (0, k_blk + p)),
                  pl.BlockSpec((t, PAIR), lambda p, i: (0, v_blk + p))],
        out_specs=pl.BlockSpec((tq, PAIR), lambda p, i: (i, p)),
        out_shape=_sds((t, N_PAIRS * PAIR), F32), compiler_params=_params("parallel", "parallel"), name=name,
    )(proj, proj, proj)


def sb_bwd(name, proj, o, do, q_blk, k_blk, v_blk, do_blk):
    t = proj.shape[0]
    tq = _row_tile(t, SB_TQ)
    tk = min(SB_TK, tq)
    per = tq // tk

    def body(q_ref, k_ref, v_ref, o_ref, do_ref, dq_ref, dk_ref, dv_ref):
        i = pl.program_id(1)

        @pl.when(i == 0)
        def _():
            dk_ref[...] = jnp.zeros_like(dk_ref)
            dv_ref[...] = jnp.zeros_like(dv_ref)

        tri_s, tri_i = _tri(tk, True), _tri(tk, False)
        qh = _halves(q_ref[...] * QK_SCALE)
        doh = _halves(do_ref[...])
        deltas = [jnp.sum(x, axis=-1, keepdims=True) for x in _halves(do_ref[...].astype(F32) * o_ref[...])]

        def blocks(kb, carry, off):
            k0 = pl.multiple_of(kb * tk, tk)
            ks, vs = k_ref[pl.ds(k0, tk), :], v_ref[pl.ds(k0, tk), :]
            causal = None if off is None else _causal(tq, tk, off)
            lbm = [_sb_scores(qh[hh], ks, causal) for hh in range(2)]
            dws = [lax.dot_general(doh[hh], vs, NT, preferred_element_type=F32) for hh in range(2)]
            afters = [_suffix_sum(lbm[hh][1], tri_s, False) for hh in range(2)]
            wbs, es = [], []
            for hh in range(2):
                w = jnp.exp(lbm[hh][0] + afters[hh] + carry[3 * hh + 1])
                if causal is not None:
                    w = jnp.where(causal, w, 0.0)
                wbs.append(w.astype(BF16))
                es.append(wbs[hh].astype(F32) * dws[hh])
            sfx = [_suffix_sum(es[hh], tri_i, True) for hh in range(2)]
            dzs = []
            for hh in range(2):
                left = deltas[hh] - (sfx[hh] + carry[3 * hh + 2])
                sig = jnp.exp(lbm[hh][0])
                dz = es[hh] * (1.0 - sig) - left * sig
                if causal is not None:
                    dz = jnp.where(causal, dz, 0.0)
                dzs.append(dz.astype(BF16))
            dk_ref[pl.ds(k0, tk), :] += (lax.dot_general(dzs[0], qh[0], TN, preferred_element_type=F32)
                                         + lax.dot_general(dzs[1], qh[1], TN, preferred_element_type=F32))
            dv_ref[pl.ds(k0, tk), :] += (lax.dot_general(wbs[0], doh[0], TN, preferred_element_type=F32)
                                         + lax.dot_general(wbs[1], doh[1], TN, preferred_element_type=F32))
            out = []
            for hh in range(2):
                out += [carry[3 * hh] + jnp.dot(dzs[hh], ks, preferred_element_type=F32),
                        carry[3 * hh + 1] + jnp.sum(lbm[hh][1], axis=-1, keepdims=True),
                        carry[3 * hh + 2] + jnp.sum(es[hh], axis=-1, keepdims=True)]
            return tuple(out)

        def alive(carry):
            return jnp.maximum(jnp.max(carry[1]), jnp.max(carry[4])) > SB_DEAD

        zero = jnp.zeros((tq, 1), F32)
        carry = (jnp.zeros((tq, PAIR), F32), zero, zero) * 2
        for off in reversed(range(per)):
            carry = blocks(i * per + off, carry, off)

        def step(c):
            new = blocks(i * per - 1 - c[0], c[2:], None)
            return (c[0] + 1, alive(new)) + new

        out = lax.while_loop(lambda c: jnp.logical_and(c[0] < i * per, c[1]), step,
                             (jnp.int32(0), alive(carry)) + carry)
        dq_ref[...] = _merge(out[2], out[5]) * QK_SCALE

    tile = lambda blk: pl.BlockSpec((tq, PAIR), lambda p, i: (i, blk + p))
    whole = lambda blk: pl.BlockSpec((t, PAIR), lambda p, i: (0, blk + p))
    return pl.pallas_call(
        body, grid=(N_PAIRS, t // tq),
        in_specs=[tile(q_blk), whole(k_blk), whole(v_blk), tile(0), tile(do_blk)],
        out_specs=[tile(0), whole(0), whole(0)],
        out_shape=[_sds((t, N_PAIRS * PAIR), F32)] * 3,
        compiler_params=_params("parallel", "arbitrary"), name=name)(proj, proj, proj, o, do)


def _sigmoid(x):
    return 1.0 / (1.0 + jnp.exp(-x))


def gu_gap(f8):
    return -(-f8 // 128) * 128


def merge_gu(gate, up):
    f8 = gate.shape[-1]
    pad = jnp.zeros(gate.shape[:-1] + (gu_gap(f8) - f8,), gate.dtype)
    return jnp.concatenate([gate, pad, up], axis=-1)


def split_gu(gu, f8):
    return gu[..., :f8], gu[..., gu_gap(f8):]


PER = 2


def _stacked_pair(w_ref, p, f8):
    zeros = jnp.zeros((gu_gap(f8) - f8, w_ref.shape[-1]), w_ref.dtype)
    return jnp.concatenate([w_ref[PER * p], zeros, w_ref[PER * p + 1]], axis=0)


def ffn_up(name, h, wgu, f8, layer):
    t, d = h.shape
    nb, _, _, fw = wgu.shape
    gap = gu_gap(f8)
    tm = _row_tile(t, 1024)

    def body(h_ref, w_ref, gu_ref, a_ref):
        hv = h_ref[...]
        rs = [jnp.dot(hv, w_ref[j], preferred_element_type=F32) for j in range(PER)]
        a_ref[...] = jnp.zeros_like(a_ref)
        for j in range(PER):
            gu_ref[j] = rs[j].astype(BF16)
            g, u = rs[j][:, :f8], rs[j][:, gap:]
            a_ref[:, j * gap:j * gap + f8] = (g * _sigmoid(g) * u).astype(BF16)

    return pl.pallas_call(
        body, grid=(t // tm, nb // PER),
        in_specs=[pl.BlockSpec((tm, d), lambda i, k: (i, 0)),
                  pl.BlockSpec((PER, None, d, fw), lambda i, k: (k, layer, 0, 0))],
        out_specs=[pl.BlockSpec((PER, tm, fw), lambda i, k: (k, i, 0)),
                   pl.BlockSpec((None, tm, fw), lambda i, k: (k, i, 0))],
        out_shape=[_sds((nb, t, fw), BF16), _sds((nb // PER, t, fw), BF16)],
        compiler_params=_params("parallel", "parallel"), name=name)(h, wgu)


def _mm_all_blocks(name, a, w, layer, dn, tm):
    nb, t, f = a.shape
    wshape = w.shape[2:]
    d = wshape[1] if dn == NN else wshape[0]
    tm = _row_tile(t, tm)

    def body(a_ref, w_ref, o_ref):
        acc = lax.dot_general(a_ref[0], w_ref[0], dn, preferred_element_type=F32)
        for k in range(1, nb):
            acc = acc + lax.dot_general(a_ref[k], w_ref[k], dn, preferred_element_type=F32)
        o_ref[...] = acc

    return pl.pallas_call(
        body, grid=(t // tm,),
        in_specs=[pl.BlockSpec((nb, tm, f), lambda i: (0, i, 0)),
                  pl.BlockSpec((nb, None) + wshape, lambda i: (0, layer, 0, 0))],
        out_specs=pl.BlockSpec((tm, d), lambda i: (i, 0)), out_shape=_sds((t, d), F32),
        compiler_params=_params("parallel"), name=name)(a, w)


def ffn_down(name, a, wd, f8, layer):
    npair, t, fw = a.shape
    nb, _, _, d = wd.shape
    tm = _row_tile(t, 512)

    def body(a_ref, w_ref, o_ref):
        acc = jnp.dot(a_ref[0], _stacked_pair(w_ref, 0, f8), preferred_element_type=F32)
        for p in range(1, npair):
            acc = acc + jnp.dot(a_ref[p], _stacked_pair(w_ref, p, f8), preferred_element_type=F32)
        o_ref[...] = acc

    return pl.pallas_call(
        body, grid=(t // tm,),
        in_specs=[pl.BlockSpec((npair, tm, fw), lambda i: (0, i, 0)),
                  pl.BlockSpec((nb, None, f8, d), lambda i: (0, layer, 0, 0))],
        out_specs=pl.BlockSpec((tm, d), lambda i: (i, 0)), out_shape=_sds((t, d), F32),
        compiler_params=_params("parallel"), name=name)(a, wd)


def ffn_bwd_act(name, dm, wd, gu, f8, layer):
    t, d = dm.shape
    nb, _, fw = gu.shape
    gap = gu_gap(f8)
    tm = _row_tile(t, 1024)

    def body(dm_ref, wd_ref, gu_ref, o_ref):
        da = lax.dot_general(dm_ref[...], _stacked_pair(wd_ref, 0, f8), NT, preferred_element_type=F32)
        o_ref[...] = jnp.zeros_like(o_ref)
        for j in range(PER):
            daj = da[:, j * gap:j * gap + f8]
            gv = gu_ref[j, :, :f8].astype(F32)
            uv = gu_ref[j, :, gap:].astype(F32)
            sg = _sigmoid(gv)
            o_ref[j, :, :f8] = (daj * uv * sg * (1.0 + gv * (1.0 - sg))).astype(BF16)
            o_ref[j, :, gap:] = (daj * gv * sg).astype(BF16)

    bspec = pl.BlockSpec((PER, tm, fw), lambda i, k: (k, i, 0))
    return pl.pallas_call(
        body, grid=(t // tm, nb // PER),
        in_specs=[pl.BlockSpec((tm, d), lambda i, k: (i, 0)),
                  pl.BlockSpec((PER, None, f8, d), lambda i, k: (k, layer, 0, 0)), bspec],
        out_specs=bspec, out_shape=_sds((nb, t, fw), BF16),
        compiler_params=_params("parallel", "parallel"), name=name)(dm, wd, gu)


def ffn_bwd_dh(name, dgu, wgu, layer):
    return _mm_all_blocks(name, dgu, wgu, layer, NT, 512)


def ffn_dw_in(name, h, dact):
    t, d = h.shape
    nb, _, f8 = dact.shape
    tk = _row_tile(t, 2048)
    return _gmm(name, h, dact, grid=(nb, t // tk),
                a_blk=(tk, d), a_idx=lambda b, s: (s, 0), b_blk=(None, tk, f8), b_idx=lambda b, s: (b, s, 0),
                o_blk=(None, d, f8), o_idx=lambda b, s: (b, 0, 0), out_shape=(nb, d, f8), out_dtype=F32, dn=TN,
                acc_shape=(d, f8))


def ffn_dw_down(name, a, dm, f8):
    npair, t, fw = a.shape
    d = dm.shape[1]
    gap = gu_gap(f8)
    tk = _row_tile(t, 2048)
    nk = t // tk

    def body(a_ref, dm_ref, o_ref, acc_ref):
        s = pl.program_id(1)
        part = lax.dot_general(a_ref[...], dm_ref[...], TN, preferred_element_type=F32)

        @pl.when(s == 0)
        def _():
            acc_ref[...] = part

        @pl.when(s > 0)
        def _():
            acc_ref[...] += part

        @pl.when(s == nk - 1)
        def _():
            for j in range(PER):
                o_ref[j] = acc_ref[j * gap:j * gap + f8, :]

    return pl.pallas_call(
        body, grid=(npair, nk),
        in_specs=[pl.BlockSpec((None, tk, fw), lambda p, s: (p, s, 0)), pl.BlockSpec((tk, d), lambda p, s: (s, 0))],
        out_specs=pl.BlockSpec((PER, f8, d), lambda p, s: (p, 0, 0)), out_shape=_sds((PER * npair, f8, d), F32),
        scratch_shapes=[pltpu.VMEM((fw, d), F32)],
        compiler_params=_params("parallel", "arbitrary"), name=name)(a, dm)


GELU_C = math.sqrt(2.0 / math.pi)
GELU_A = 0.044715


def _gelu(x):
    return 0.5 * x * (1.0 + jnp.tanh(GELU_C * (x + GELU_A * x * x * x)))


def _gelu_grad(x):
    th = jnp.tanh(GELU_C * (x + GELU_A * x * x * x))
    return 0.5 * (1.0 + th) + 0.5 * x * (1.0 - th * th) * GELU_C * (1.0 + 3.0 * GELU_A * x * x)


def _neg_expm1(x):
    series = x * (1.0 + x * (0.5 + x * (1.0 / 6.0 + x * (1.0 / 24.0 + x * (1.0 / 120.0 + x * (1.0 / 720.0))))))
    return -jnp.where(x > -0.25, series, jnp.exp(x) - 1.0)


CONV_TR = 256
CONV_TAPS = 4
HALO = 8


def _shifted(ext, k, tr, back):
    if back:
        return pltpu.roll(ext, k, 0)[HALO:, :] if k else ext[HALO:, :]
    return pltpu.roll(ext, tr + HALO - k, 0)[:tr, :] if k else ext[:tr, :]


def conv4_fwd(name, src, cb, w, b):
    t, c = src.shape[0], w.shape[1]
    tr = _row_tile(t, CONV_TR)
    hb = tr // HALO

    def body(x_ref, h_ref, w_ref, b_ref, o_ref):
        i = pl.program_id(0)
        ext = jnp.concatenate([jnp.where(i == 0, 0.0, h_ref[...]), x_ref[...]], axis=0)
        acc = b_ref[...]
        for k in range(CONV_TAPS):
            acc = acc + w_ref[CONV_TAPS - 1 - k:CONV_TAPS - k, :] * _shifted(ext, k, tr, True)
        o_ref[...] = acc

    return pl.pallas_call(
        body, grid=(t // tr,),
        in_specs=[pl.BlockSpec((tr, c), lambda i: (i, cb)),
                  pl.BlockSpec((HALO, c), lambda i: (jnp.maximum(i * hb - 1, 0), cb)),
                  pl.BlockSpec((CONV_TAPS, c), lambda i: (0, 0)), pl.BlockSpec((1, c), lambda i: (0, 0))],
        out_specs=pl.BlockSpec((tr, c), lambda i: (i, 0)), out_shape=_sds((t, c), F32),
        compiler_params=_params("parallel"), name=name)(src, src, w, b)


def conv4_bwd_x(name, dy, w):
    t, c = dy.shape
    tr = _row_tile(t, CONV_TR)
    hb = tr // HALO
    last = t // tr - 1

    def body(y_ref, h_ref, w_ref, o_ref):
        i = pl.program_id(0)
        ext = jnp.concatenate([y_ref[...], jnp.where(i == last, 0.0, h_ref[...])], axis=0)
        acc = w_ref[CONV_TAPS - 1:CONV_TAPS, :] * y_ref[...]
        for k in range(1, CONV_TAPS):
            acc = acc + w_ref[CONV_TAPS - 1 - k:CONV_TAPS - k, :] * _shifted(ext, k, tr, False)
        o_ref[...] = acc

    return pl.pallas_call(
        body, grid=(t // tr,),
        in_specs=[pl.BlockSpec((tr, c), lambda i: (i, 0)),
                  pl.BlockSpec((HALO, c), lambda i: (jnp.minimum((i + 1) * hb, t // HALO - 1), 0)),
                  pl.BlockSpec((CONV_TAPS, c), lambda i: (0, 0))],
        out_specs=pl.BlockSpec((tr, c), lambda i: (i, 0)), out_shape=_sds((t, c), F32),
        compiler_params=_params("parallel"), name=name)(dy, dy, w)


def conv4_bwd_w(name, src, cb, dy):
    t, c = dy.shape
    tr = _row_tile(t, CONV_TR)
    hb = tr // HALO

    def body(x_ref, h_ref, dy_ref, dw_ref, db_ref):
        i = pl.program_id(0)

        @pl.when(i == 0)
        def _():
            dw_ref[...] = jnp.zeros_like(dw_ref)
            db_ref[...] = jnp.zeros_like(db_ref)

        ext = jnp.concatenate([jnp.where(i == 0, 0.0, h_ref[...]), x_ref[...]], axis=0)
        dyv = dy_ref[...]
        db_ref[...] += jnp.sum(dyv, axis=0, keepdims=True)
        for k in range(CONV_TAPS):
            dw_ref[CONV_TAPS - 1 - k:CONV_TAPS - k, :] += jnp.sum(dyv * _shifted(ext, k, tr, True), axis=0,
                                                                   keepdims=True)

    return pl.pallas_call(
        body, grid=(t // tr,),
        in_specs=[pl.BlockSpec((tr, c), lambda i: (i, cb)),
                  pl.BlockSpec((HALO, c), lambda i: (jnp.maximum(i * hb - 1, 0), cb)),
                  pl.BlockSpec((tr, c), lambda i: (i, 0))],
        out_specs=[pl.BlockSpec((CONV_TAPS, c), lambda i: (0, 0)), pl.BlockSpec((1, c), lambda i: (0, 0))],
        out_shape=[_sds((CONV_TAPS, c), F32), _sds((1, c), F32)],
        compiler_params=_params("arbitrary"), name=name)(src, src, dy)


def _rg_gate_values(xcv, wa_ref, wi_ref, ba_ref, bi_ref, lam_ref):
    xb = xcv.astype(BF16)
    r = _sigmoid(jnp.dot(xb, wa_ref[...], preferred_element_type=F32) + ba_ref[...])
    ig = _sigmoid(jnp.dot(xb, wi_ref[...], preferred_element_type=F32) + bi_ref[...])
    lam = lam_ref[...]
    sp = jnp.maximum(-lam, 0.0) + jnp.log(1.0 + jnp.exp(-jnp.abs(lam)))
    log_a = -LRU_C * r * sp
    a = jnp.exp(log_a)
    mult = jnp.sqrt(_neg_expm1(2.0 * log_a))
    return xb, r, ig, sp, a, mult


def rg_gates_fwd(name, xc, wa, wi, ba, bi, lam):
    t, c = xc.shape
    nb, cb, _ = wa.shape
    tm = _row_tile(t, 512)

    def body(xc_ref, wa_ref, wi_ref, ba_ref, bi_ref, lam_ref, a_ref, u_ref):
        xcv = xc_ref[...]
        _, _, ig, _, a, mult = _rg_gate_values(xcv, wa_ref, wi_ref, ba_ref, bi_ref, lam_ref)
        a_ref[...] = a
        u_ref[...] = mult * (ig * xcv)

    blk = pl.BlockSpec((tm, cb), lambda i, n: (i, n))
    wsp = pl.BlockSpec((None, cb, cb), lambda i, n: (n, 0, 0))
    vec = pl.BlockSpec((1, cb), lambda i, n: (0, n))
    return pl.pallas_call(
        body, grid=(t // tm, nb), in_specs=[blk, wsp, wsp, vec, vec, vec], out_specs=[blk, blk],
        out_shape=[_sds((t, c), F32)] * 2, compiler_params=_params("parallel", "parallel"), name=name,
    )(xc, wa, wi, ba, bi, lam)


def rg_gates_bwd(name, xc, gu, hs, wa, wi, ba, bi, lam):
    t, c = xc.shape
    nb, cb, _ = wa.shape
    tm = _row_tile(t, 512)
    hb = tm // HALO

    def body(xc_ref, gu_ref, h_ref, halo_ref, wa_ref, wi_ref, ba_ref, bi_ref, lam_ref,
             dxc_ref, dwa_ref, dwi_ref, dba_ref, dbi_ref, dlam_ref):
        i = pl.program_id(1)
        hprev = _shifted(jnp.concatenate([jnp.where(i == 0, 0.0, halo_ref[...]), h_ref[...]], axis=0), 1, tm, True)

        @pl.when(i == 0)
        def _():
            for ref in (dwa_ref, dwi_ref, dba_ref, dbi_ref, dlam_ref):
                ref[...] = jnp.zeros_like(ref)

        xcv = xc_ref[...]
        xb, r, ig, sp, a, mult = _rg_gate_values(xcv, wa_ref, wi_ref, ba_ref, bi_ref, lam_ref)
        gv = gu_ref[...]
        d_ixc = gv * mult
        d_i = d_ixc * xcv
        d_mult = gv * ig * xcv
        d_a = gv * hprev - d_mult * a / mult
        d_log_a = d_a * a
        d_r = d_log_a * (-LRU_C * sp)
        sig_neg_lam = 1.0 / (1.0 + jnp.exp(lam_ref[...]))
        dlam_ref[...] += jnp.sum(d_log_a * r, axis=0, keepdims=True) * (LRU_C * sig_neg_lam)
        dpa = d_r * r * (1.0 - r)
        dpi = d_i * ig * (1.0 - ig)
        dba_ref[...] += jnp.sum(dpa, axis=0, keepdims=True)
        dbi_ref[...] += jnp.sum(dpi, axis=0, keepdims=True)
        dpab, dpib = dpa.astype(BF16), dpi.astype(BF16)
        dxc_ref[...] = (d_ixc * ig + lax.dot_general(dpab, wa_ref[...], NT, preferred_element_type=F32)
                        + lax.dot_general(dpib, wi_ref[...], NT, preferred_element_type=F32))
        dwa_ref[...] += lax.dot_general(xb, dpab, TN, preferred_element_type=F32)
        dwi_ref[...] += lax.dot_general(xb, dpib, TN, preferred_element_type=F32)

    blk = pl.BlockSpec((tm, cb), lambda n, i: (i, n))
    wsp = pl.BlockSpec((None, cb, cb), lambda n, i: (n, 0, 0))
    vec = pl.BlockSpec((1, cb), lambda n, i: (0, n))
    halo = pl.BlockSpec((HALO, cb), lambda n, i: (jnp.maximum(i * hb - 1, 0), n))
    return pl.pallas_call(
        body, grid=(nb, t // tm), in_specs=[blk, blk, blk, halo, wsp, wsp, vec, vec, vec],
        out_specs=[blk, wsp, wsp, vec, vec, vec],
        out_shape=[_sds((t, c), F32), _sds((nb, cb, cb), F32), _sds((nb, cb, cb), F32),
                   _sds((1, c), F32), _sds((1, c), F32), _sds((1, c), F32)],
        compiler_params=_params("parallel", "arbitrary"), name=name)(xc, gu, hs, hs, wa, wi, ba, bi, lam)


SCAN_TS = 256
SCAN_TC = 512


def _tile_scan(a, b, reverse):
    ts = a.shape[0]
    row = lax.broadcasted_iota(jnp.int32, a.shape, 0)
    d = 1
    while d < ts:
        if reverse:
            inside = row < ts - d
            a_sh = jnp.where(inside, pltpu.roll(a, ts - d, 0), 1.0)
            b_sh = jnp.where(inside, pltpu.roll(b, ts - d, 0), 0.0)
        else:
            inside = row >= d
            a_sh = jnp.where(inside, pltpu.roll(a, d, 0), 1.0)
            b_sh = jnp.where(inside, pltpu.roll(b, d, 0), 0.0)
        b = b + a * b_sh
        a = a * a_sh
        d *= 2
    return a, b


def rg_scan_fwd(name, a, u, gate_pre):
    t, c = a.shape
    ts, tc = _row_tile(t, SCAN_TS), _row_tile(c, SCAN_TC)

    def body(a_ref, u_ref, g_ref, h_ref, z_ref, carry_ref):
        s = pl.program_id(1)

        @pl.when(s == 0)
        def _():
            carry_ref[...] = jnp.zeros_like(carry_ref)

        ac, bc = _tile_scan(a_ref[...], u_ref[...], False)
        h = bc + ac * carry_ref[0:1, :]
        h_ref[...] = h
        z_ref[...] = (h * _gelu(g_ref[...])).astype(BF16)
        carry_ref[0:1, :] = h[ts - 1:ts, :]

    blk = pl.BlockSpec((ts, tc), lambda j, s: (s, j))
    return pl.pallas_call(
        body, grid=(c // tc, t // ts), in_specs=[blk, blk, blk], out_specs=[blk, blk],
        out_shape=[_sds((t, c), F32), _sds((t, c), BF16)], scratch_shapes=[pltpu.VMEM((8, tc), F32)],
        compiler_params=_params("parallel", "arbitrary"), name=name)(a, u, gate_pre)


def rg_scan_bwd(name, a, hs, gate_pre, dz):
    t, c = hs.shape
    ts, tc = _row_tile(t, SCAN_TS), _row_tile(c, SCAN_TC)
    nt = t // ts
    hb = ts // HALO

    def body(a_ref, halo_ref, h_ref, g_ref, dz_ref, gu_ref, dgate_ref, carry_ref):
        s = pl.program_id(1)

        @pl.when(s == 0)
        def _():
            carry_ref[...] = jnp.zeros_like(carry_ref)

        a_next = _shifted(jnp.concatenate([a_ref[...], jnp.where(s == 0, 0.0, halo_ref[...])], axis=0), 1, ts, False)
        gate = g_ref[...]
        dzv = dz_ref[...]
        dgate_ref[...] = (dzv * h_ref[...] * _gelu_grad(gate)).astype(BF16)
        ac, bc = _tile_scan(a_next, dzv * _gelu(gate), True)
        gu = bc + ac * carry_ref[0:1, :]
        gu_ref[...] = gu
        carry_ref[0:1, :] = gu[0:1, :]

    blk = pl.BlockSpec((ts, tc), lambda j, s: (nt - 1 - s, j))
    halo = pl.BlockSpec((HALO, tc), lambda j, s: (jnp.minimum((nt - s) * hb, t // HALO - 1), j))
    return pl.pallas_call(
        body, grid=(c // tc, nt), in_specs=[blk, halo, blk, blk, blk], out_specs=[blk, blk],
        out_shape=[_sds((t, c), F32), _sds((t, c), BF16)], scratch_shapes=[pltpu.VMEM((8, tc), F32)],
        compiler_params=_params("parallel", "arbitrary"), name=name)(a, a, hs, gate_pre, dz)


QA_BLK, KA_BLK, VA_BLK, QS_BLK, KS_BLK, VS_BLK = (g * N_PAIRS for g in range(6))


TOEP_W = 640
TOEP_FLAT = 320
TABLE_LOW = 193


def rel_bias_matrix(name, table):
    h = table.shape[0]
    diag = jnp.concatenate([jnp.repeat(table[:, 2 * REL_CLIP:], TOEP_FLAT, axis=1),
                            jnp.flip(table[:, TABLE_LOW:2 * REL_CLIP], axis=1),
                            jnp.zeros((h, 1), table.dtype)], axis=1)[:, None, :]

    def body(v_ref, o_ref):
        rows = jnp.broadcast_to(v_ref[...], (CHUNK, TOEP_W))
        o_ref[...] = pltpu.roll(rows, TOEP_W - (CHUNK - 1), 1, stride=1, stride_axis=0)

    out = pl.pallas_call(
        body, grid=(h,), in_specs=[pl.BlockSpec((None, 1, TOEP_W), lambda hh: (hh, 0, 0))],
        out_specs=pl.BlockSpec((None, CHUNK, TOEP_W), lambda hh: (hh, 0, 0)),
        out_shape=_sds((h, CHUNK, TOEP_W), F32), compiler_params=_params("parallel"), name=name)(diag)
    return out[:, :, :BAND]


def rel_bias_grad(name, dbias):
    h = dbias.shape[0]
    flipped = jnp.pad(jnp.flip(dbias, axis=1), ((0, 0), (0, 0), (0, TOEP_W - BAND)))

    def body(x_ref, o_ref):
        skew = pltpu.roll(x_ref[...], 0, 1, stride=1, stride_axis=0)
        col = jnp.sum(skew, axis=0, keepdims=True)
        lane = lax.broadcasted_iota(jnp.int32, col.shape, 1)
        flat = jnp.sum(jnp.where(lane < TOEP_FLAT, col, 0.0), axis=1, keepdims=True)
        o_ref[...] = jnp.where(lane == TOEP_W - 1, flat, col)

    out = pl.pallas_call(
        body, grid=(h,), in_specs=[pl.BlockSpec((None, CHUNK, TOEP_W), lambda hh: (hh, 0, 0))],
        out_specs=pl.BlockSpec((None, 1, TOEP_W), lambda hh: (hh, 0, 0)),
        out_shape=_sds((h, 1, TOEP_W), F32), compiler_params=_params("parallel"), name=name)(flipped)[:, 0, :]
    return jnp.concatenate([jnp.zeros((h, TABLE_LOW), F32), jnp.flip(out[:, TOEP_FLAT:TOEP_W - 1], axis=1),
                            out[:, TOEP_W - 1:]], axis=1)


def attn_layer_fwd(tag, x, h, w, g_next):
    proj = mm_nn_wblk(tag + "_proj", h, w["w_in"], w["idx"], BF16)
    width = N_PAIRS * PAIR
    pad = lambda a: jnp.pad(a, ((PAD_KEYS, 0), (0, 0)))
    kap, vap = pad(proj[:, width:2 * width]), pad(proj[:, 2 * width:3 * width])
    bias = rel_bias_matrix(tag + "_bias", w["rel_bias"])
    oa, lse = attn_a_fwd(tag + "_a", proj, kap, vap, bias, QA_BLK)
    ob = sb_fwd(tag + "_sb", proj, QS_BLK, KS_BLK, VS_BLK)
    o = jnp.concatenate([oa, ob], axis=1).astype(BF16)
    m = mm_nn(tag + "_out", o, w["w_out"], F32)
    x1, h_next = resid_norm_fwd(tag + "_res", x, m, w["g_post"], g_next)
    return x1, h_next, (x, h, proj, kap, vap, bias, oa, lse, ob, o, m)


def attn_layer_bwd(tag, dm, dx1, saved, w, prev):
    x, h, proj, kap, vap, bias, oa, lse, ob, o, m = saved
    d_w_out = mm_tn(tag + "_dwout", o, dm, F32)
    do = mm_nt(tag + "_do", dm, w["w_out"], BF16)
    dqa, dkap, dvap, dbias = attn_a_bwd(tag + "_da", proj, kap, vap, bias, oa, lse, do, QA_BLK, 0)
    dqs, dks, dvs = sb_bwd(tag + "_dsb", proj, ob, do, QS_BLK, KS_BLK, VS_BLK, N_PAIRS)
    d_rel = rel_bias_grad(tag + "_dbias", dbias)
    dproj = jnp.concatenate([dqa, dkap[PAD_KEYS:], dvap[PAD_KEYS:], dqs, dks, dvs], axis=1).astype(BF16)
    d_w_in = mm_tn_oblk(tag + "_dwin", h, dproj, w["w_in"].shape[3], F32)
    dh = mm_nt_wblk(tag + "_dh", dproj, w["w_in"], w["idx"], F32)
    dx, dg_pre, dm_prev, dg_post_prev = close_bwd(tag, dh, x, w["g_pre"], dx1, prev)
    return dx, dm_prev, dg_post_prev, dict(w_in=d_w_in, w_out=d_w_out, rel_bias=d_rel, g_pre=dg_pre)


def rg_layer_fwd(tag, x, h, w, g_next):
    proj = mm_nn_wblk(tag + "_proj", h, w["w_in"], w["idx"], F32)
    xc = conv4_fwd(tag + "_conv", proj, 1, w["conv_w"], w["conv_b"])
    a, u = rg_gates_fwd(tag + "_gates", xc, w["w_a"], w["w_i"], w["b_a"], w["b_i"], w["lam"])
    hs, z = rg_scan_fwd(tag + "_scan", a, u, proj)
    m = mm_nn(tag + "_out", z, w["w_out"], F32)
    x1, h_next = resid_norm_fwd(tag + "_res", x, m, w["g_post"], g_next)
    return x1, h_next, (x, h, proj, xc, a, hs, z, m)


def rg_layer_bwd(tag, dm, dx1, saved, w, prev):
    x, h, proj, xc, a, hs, z, m = saved
    d_w_out = mm_tn(tag + "_dwout", z, dm, F32)
    dz = mm_nt(tag + "_dz", dm, w["w_out"], F32)
    gu, dgate = rg_scan_bwd(tag + "_dscan", a, hs, proj, dz)
    dxc, d_w_a, d_w_i, d_b_a, d_b_i, d_lam = rg_gates_bwd(
        tag + "_dgates", xc, gu, hs, w["w_a"], w["w_i"], w["b_a"], w["b_i"], w["lam"])
    d_conv_w, d_conv_b = conv4_bwd_w(tag + "_dconvw", proj, 1, dxc)
    dxr = conv4_bwd_x(tag + "_dconv", dxc, w["conv_w"])
    dproj = jnp.concatenate([dgate, dxr.astype(BF16)], axis=1)
    d_w_in = mm_tn_oblk(tag + "_dwin", h, dproj, w["w_in"].shape[3], F32)
    dh = mm_nt_wblk(tag + "_dh", dproj, w["w_in"], w["idx"], F32)
    dx, dg_pre, dm_prev, dg_post_prev = close_bwd(tag, dh, x, w["g_pre"], dx1, prev)
    return dx, dm_prev, dg_post_prev, dict(w_in=d_w_in, w_out=d_w_out, conv_w=d_conv_w, conv_b=d_conv_b, w_a=d_w_a,
                                           w_i=d_w_i, b_a=d_b_a, b_i=d_b_i, lam=d_lam, g_pre=dg_pre)


def ffn_layer_fwd(tag, x, h, w, g_next):
    f8 = w["w_down"].shape[2]
    gu, a = ffn_up(tag + "_up", h, w["w_gu"], f8, w["idx"])
    f = ffn_down(tag + "_down", a, w["w_down"], f8, w["idx"])
    x1, h_next = resid_norm_fwd(tag + "_res", x, f, w["g_post"], g_next)
    return x1, h_next, (x, h, gu, a, f)


def ffn_layer_bwd(tag, dm, dx1, saved, w, prev):
    x, h, gu, a, f = saved
    f8 = w["w_down"].shape[2]
    d_w_down = ffn_dw_down(tag + "_dwdown", a, dm, f8)
    dgu = ffn_bwd_act(tag + "_dact", dm, w["w_down"], gu, f8, w["idx"])
    d_w_gu = ffn_dw_in(tag + "_dwgu", h, dgu)
    dh = ffn_bwd_dh(tag + "_dh", dgu, w["w_gu"], w["idx"])
    dx, dg_pre, dm_prev, dg_post_prev = close_bwd(tag, dh, x, w["g_pre"], dx1, prev)
    return dx, dm_prev, dg_post_prev, dict(w_gu=d_w_gu, w_down=d_w_down, g_pre=dg_pre)


def _place():
    return lax.axis_index("x"), lax.axis_index("y"), lax.axis_index("c")


def all_gather(name, blks):
    n = len(blks)

    def body(*refs):
        x_refs, out_refs = refs[:n], refs[n:2 * n]
        send_sems, recv_sems, local_sems = refs[2 * n:]
        x, y, cc = _place()
        me, sibling = (x, y, cc), (x, y, 1 - cc)
        chips = [(1 - x, y), (x, 1 - y), (1 - x, 1 - y)]
        south = cc == 0
        via = (jnp.where(south, 1 - x, x), jnp.where(south, y, 1 - y))
        onward = (jnp.where(south, x, 1 - x), jnp.where(south, 1 - y, y))
        k_via, k_onward = 1 + cc, 2 - cc

        def slot(a, px, py, pc):
            return out_refs[a].at[4 * px + 2 * py + pc]

        def copy(a, k, block, to, src=None):
            return pltpu.make_async_remote_copy(
                src_ref=slot(a, *block) if src is None else src, dst_ref=slot(a, *block),
                send_sem=send_sems.at[7 * a + k], recv_sem=recv_sems.at[7 * a + k], device_id=to, device_id_type=MESH)

        mine = [pltpu.make_async_copy(x_refs[a], slot(a, *me), local_sems.at[a]) for a in range(n)]
        sends = []
        for a in range(n):
            mine[a].start()
            sends.append(copy(a, 0, me, sibling, src=x_refs[a]))
            sends += [copy(a, 1 + j, me, (*chips[j], cc), src=x_refs[a]) for j in range(2)]
        for cp in sends:
            cp.start()
        for a in range(n):
            copy(a, k_via, (*via, cc), me).wait_recv()
            sends.append(copy(a, 3, (*via, cc), (*onward, cc)))
            sends.append(copy(a, 3 + k_via, (*via, cc), sibling))
            sends[-2].start()
            sends[-1].start()
        for a in range(n):
            copy(a, k_onward, (*onward, cc), me).wait_recv()
            sends.append(copy(a, 3 + k_onward, (*onward, cc), sibling))
            sends[-1].start()
        for a in range(n):
            copy(a, 3, (*chips[2], cc), me).wait_recv()
            sends.append(copy(a, 6, (*chips[2], cc), sibling))
            sends[-1].start()
        for a in range(n):
            copy(a, 0, sibling, me).wait_recv()
            for j, chip in enumerate(chips):
                copy(a, 4 + j, (*chip, 1 - cc), me).wait_recv()
        for cp in sends:
            cp.wait_send()
        for cp in mine:
            cp.wait()

    return pl.pallas_call(
        body, out_shape=[_sds((N_DEV,) + b.shape, b.dtype) for b in blks], in_specs=[ANY] * n, out_specs=[ANY] * n,
        scratch_shapes=[pltpu.SemaphoreType.DMA((7 * n,)), pltpu.SemaphoreType.DMA((7 * n,)),
                        pltpu.SemaphoreType.DMA((n,))],
        name=name)(*blks)


def exchange_pair(name, gs):
    n = len(gs)
    nchip = 4

    def body(*refs):
        g_refs, land_refs = refs[:n], refs[n:2 * n]
        send_sems, recv_sems = refs[2 * n:]
        x, y, cc = _place()
        copies = [pltpu.make_async_remote_copy(
            src_ref=g_refs[a].at[j, 1 - cc], dst_ref=land_refs[a].at[j], send_sem=send_sems.at[nchip * a + j],
            recv_sem=recv_sems.at[nchip * a + j], device_id=(x, y, 1 - cc), device_id_type=MESH)
            for a in range(n) for j in range(nchip)]
        for cp in copies:
            cp.start()
        for cp in copies:
            cp.wait()

    return pl.pallas_call(
        body, out_shape=[_sds((nchip,) + g.shape[2:], g.dtype) for g in gs], in_specs=[ANY] * n, out_specs=[ANY] * n,
        scratch_shapes=[pltpu.SemaphoreType.DMA((nchip * n,)), pltpu.SemaphoreType.DMA((nchip * n,))],
        name=name)(*gs)


def pair_sum(name, g, land, core, out_dtype):
    nchip, _, r, c = g.shape
    tr = _divisor_tile(r, 1024, 16)

    def body(core_ref, g_ref, l_ref, o_ref):
        o_ref[...] = (g_ref[...] + l_ref[...]).astype(o_ref.dtype)

    return pl.pallas_call(
        body,
        grid_spec=pltpu.PrefetchScalarGridSpec(
            num_scalar_prefetch=1, grid=(nchip, r // tr),
            in_specs=[pl.BlockSpec((None, None, tr, c), lambda j, i, core_ref: (j, core_ref[0], i, 0)),
                      pl.BlockSpec((None, tr, c), lambda j, i, core_ref: (j, i, 0))],
            out_specs=pl.BlockSpec((None, tr, c), lambda j, i, core_ref: (j, i, 0))),
        out_shape=_sds((nchip, r, c), out_dtype), compiler_params=_params("parallel", "parallel"), name=name,
    )(core, g, land)


def exchange_chips(name, ps):
    n = len(ps)

    def body(*refs):
        p_refs, land_refs = refs[:n], refs[n:2 * n]
        send_sems, recv_sems, local_sems = refs[2 * n:]
        x, y, cc = _place()
        mine = 2 * x + y
        chips = [(1 - x, y), (x, 1 - y), (1 - x, 1 - y)]
        own = [pltpu.make_async_copy(p_refs[a].at[mine], land_refs[a].at[mine], local_sems.at[a]) for a in range(n)]
        for cp in own:
            cp.start()
        sends = [pltpu.make_async_remote_copy(
            src_ref=p_refs[a].at[2 * px + py], dst_ref=land_refs[a].at[mine], send_sem=send_sems.at[3 * a + k],
            recv_sem=recv_sems.at[3 * a + k], device_id=(px, py, cc), device_id_type=MESH)
            for a in range(n) for k, (px, py) in enumerate(chips)]
        for cp in sends:
            cp.start()
        for a in range(n):
            for k, (px, py) in enumerate(chips):
                pltpu.make_async_remote_copy(
                    src_ref=p_refs[a].at[mine], dst_ref=land_refs[a].at[2 * px + py], send_sem=send_sems.at[3 * a + k],
                    recv_sem=recv_sems.at[3 * a + k], device_id=(px, py, cc), device_id_type=MESH).wait_recv()
        for cp in sends:
            cp.wait_send()
        for cp in own:
            cp.wait()

    return pl.pallas_call(
        body, out_shape=[_sds(p.shape, p.dtype) for p in ps], in_specs=[ANY] * n, out_specs=[ANY] * n,
        scratch_shapes=[pltpu.SemaphoreType.DMA((3 * n,)), pltpu.SemaphoreType.DMA((3 * n,)),
                        pltpu.SemaphoreType.DMA((n,))],
        name=name)(*ps)


def adamw(name, parts, w, m, v):
    npart, r, c = parts.shape
    tr = _divisor_tile(r, 512, 16)
    c1 = 1.0 / (1.0 - ADAM_B1 ** ADAM_STEP)
    c2 = 1.0 / (1.0 - ADAM_B2 ** ADAM_STEP)

    def body(p_ref, w_ref, m_ref, v_ref, g_ref, d_ref, nm_ref, nv_ref):
        g = p_ref[0].astype(F32)
        for j in range(1, npart):
            g = g + p_ref[j].astype(F32)
        nm = ADAM_B1 * m_ref[...] + (1.0 - ADAM_B1) * g
        nv = ADAM_B2 * v_ref[...] + (1.0 - ADAM_B2) * (g * g)
        g_ref[...] = g
        nm_ref[...] = nm
        nv_ref[...] = nv
        d_ref[...] = -ADAM_LR * ((nm * c1) / (jnp.sqrt(nv * c2) + ADAM_EPS) + ADAM_WD * w_ref[...])

    row = pl.BlockSpec((tr, c), lambda i: (i, 0))
    return pl.pallas_call(
        body, grid=(r // tr,), in_specs=[pl.BlockSpec((npart, tr, c), lambda i: (0, i, 0)), row, row, row],
        out_specs=[row] * 4, out_shape=[_sds((r, c), F32)] * 4, compiler_params=_params("parallel"), name=name,
    )(parts, w, m, v)


def _pack(arrays, dtype, row_multiple):
    flat = jnp.concatenate([a.astype(dtype).reshape(-1) for a in arrays])
    per = row_multiple * LANES
    total = -(-flat.shape[0] // per) * per
    return jnp.pad(flat, (0, total - flat.shape[0])).reshape(total // LANES, LANES)


def _pack_blocked(arrays, dtype, row_multiple):
    flat = jnp.concatenate([a.astype(dtype).reshape(N_DEV, -1) for a in arrays], axis=1)
    per = row_multiple * LANES
    total = -(-flat.shape[1] // per) * per
    return jnp.pad(flat, ((0, 0), (0, total - flat.shape[1]))).reshape(N_DEV, total // LANES, LANES)


def _unpack(buf, shapes, lead=()):
    flat = buf.reshape(lead + (-1,))
    out, off = [], 0
    for s in shapes:
        n = math.prod(s)
        out.append(flat[..., off:off + n].reshape(lead + tuple(s)))
        off += n
    return out


def _to_blocked(full, ax):
    s = full.shape
    return jnp.moveaxis(full.reshape(s[:ax] + (N_DEV, s[ax] // N_DEV) + s[ax + 1:]), ax, 0)


def _from_blocked(blk, ax):
    moved = jnp.moveaxis(blk, 0, ax)
    s = moved.shape
    return moved.reshape(s[:ax] + (s[ax] * s[ax + 1],) + s[ax + 2:])


SMALL = ("rg_conv_w", "rg_conv_b", "rg_b_a", "rg_b_i", "rg_lambda")
GU = "ffn_w_gu"
BIG = ("attn_w_in", "attn_w_out", "rg_w_in", "rg_w_a", "rg_w_i", "rg_w_out", GU, "ffn_w_down")


def kernel(x, attn_w_in, attn_rel_bias, attn_w_out, rg_w_in, rg_conv_w, rg_conv_b, rg_w_a, rg_b_a, rg_w_i, rg_b_i, rg_lambda, rg_w_out, norm_mix_pre, norm_mix_post, norm_ffn_pre, norm_ffn_post, ffn_w_gate, ffn_w_up, ffn_w_down, loss_target, m_attn_w_in, m_attn_rel_bias, m_attn_w_out, m_rg_w_in, m_rg_conv_w, m_rg_conv_b, m_rg_w_a, m_rg_b_a, m_rg_w_i, m_rg_b_i, m_rg_lambda, m_rg_w_out, m_norm_mix_pre, m_norm_mix_post, m_norm_ffn_pre, m_norm_ffn_post, m_ffn_w_gate, m_ffn_w_up, m_ffn_w_down, v_attn_w_in, v_attn_rel_bias, v_attn_w_out, v_rg_w_in, v_rg_conv_w, v_rg_conv_b, v_rg_w_a, v_rg_b_a, v_rg_w_i, v_rg_b_i, v_rg_lambda, v_rg_w_out, v_norm_mix_pre, v_norm_mix_post, v_norm_ffn_pre, v_norm_ffn_post, v_ffn_w_gate, v_ffn_w_up, v_ffn_w_down):
    w_loc = dict(attn_w_in=attn_w_in, attn_rel_bias=attn_rel_bias, attn_w_out=attn_w_out, rg_w_in=rg_w_in,
                 rg_conv_w=rg_conv_w, rg_conv_b=rg_conv_b, rg_w_a=rg_w_a, rg_b_a=rg_b_a, rg_w_i=rg_w_i, rg_b_i=rg_b_i,
                 rg_lambda=rg_lambda, rg_w_out=rg_w_out, norm_mix_pre=norm_mix_pre, norm_mix_post=norm_mix_post,
                 norm_ffn_pre=norm_ffn_pre, norm_ffn_post=norm_ffn_post, ffn_w_gate=ffn_w_gate, ffn_w_up=ffn_w_up,
                 ffn_w_down=ffn_w_down)
    m_loc = dict(attn_w_in=m_attn_w_in, attn_rel_bias=m_attn_rel_bias, attn_w_out=m_attn_w_out, rg_w_in=m_rg_w_in,
                 rg_conv_w=m_rg_conv_w, rg_conv_b=m_rg_conv_b, rg_w_a=m_rg_w_a, rg_b_a=m_rg_b_a, rg_w_i=m_rg_w_i,
                 rg_b_i=m_rg_b_i, rg_lambda=m_rg_lambda, rg_w_out=m_rg_w_out, norm_mix_pre=m_norm_mix_pre,
                 norm_mix_post=m_norm_mix_post, norm_ffn_pre=m_norm_ffn_pre, norm_ffn_post=m_norm_ffn_post,
                 ffn_w_gate=m_ffn_w_gate, ffn_w_up=m_ffn_w_up, ffn_w_down=m_ffn_w_down)
    v_loc = dict(attn_w_in=v_attn_w_in, attn_rel_bias=v_attn_rel_bias, attn_w_out=v_attn_w_out, rg_w_in=v_rg_w_in,
                 rg_conv_w=v_rg_conv_w, rg_conv_b=v_rg_conv_b, rg_w_a=v_rg_w_a, rg_b_a=v_rg_b_a, rg_w_i=v_rg_w_i,
                 rg_b_i=v_rg_b_i, rg_lambda=v_rg_lambda, rg_w_out=v_rg_w_out, norm_mix_pre=v_norm_mix_pre,
                 norm_mix_post=v_norm_mix_post, norm_ffn_pre=v_norm_ffn_pre, norm_ffn_post=v_norm_ffn_post,
                 ffn_w_gate=v_ffn_w_gate, ffn_w_up=v_ffn_w_up, ffn_w_down=v_ffn_w_down)
    axis_of = dict(SHARDED)
    xt, target = x[0], loss_target[0]
    d_model = xt.shape[1]
    rows2d = lambda a: a.reshape(-1, a.shape[-1])
    small_shapes = [w_loc[n].shape for n in SMALL]
    f8 = ffn_w_gate.shape[-1]
    for d in (w_loc, m_loc, v_loc):
        d[GU] = merge_gu(d["ffn_w_gate"], d["ffn_w_up"])

    gathered = all_gather("gather_weights", [rows2d(w_loc[n]).astype(BF16) for n in BIG]
                          + [_pack([w_loc[n] for n in SMALL], F32, 8)])
    blocked = {n: g.reshape((N_DEV,) + w_loc[n].shape) for n, g in zip(BIG, gathered)}
    blocked.update(zip(SMALL, _unpack(gathered[-1], small_shapes, (N_DEV,))))
    full = {n: _from_blocked(blocked[n], axis_of[n]) for n in SMALL}
    row = lambda a: a.reshape(1, -1).astype(F32)
    square = lambda rows8: rows8.reshape(-1, rows8.shape[-1])
    gates = lambda g: jnp.swapaxes(g, 0, 1).reshape(LRU_BLOCKS, -1, g.shape[-1])

    def layer_weights(layer):
        j = layer // 2
        norms = dict(g_pre=row(norm_mix_pre[layer]), g_post=row(norm_mix_post[layer]), idx=j)
        if layer % 2 == 0:
            mix = dict(w_in=blocked["attn_w_in"], w_out=square(blocked["attn_w_out"][:, j]),
                       rel_bias=attn_rel_bias[j], **norms)
        else:
            mix = dict(w_in=blocked["rg_w_in"], w_out=square(blocked["rg_w_out"][:, j]),
                       conv_w=full["rg_conv_w"][j][:, 0, :], conv_b=row(full["rg_conv_b"][j]),
                       w_a=gates(blocked["rg_w_a"][:, j]), w_i=gates(blocked["rg_w_i"][:, j]),
                       b_a=row(full["rg_b_a"][j]), b_i=row(full["rg_b_i"][j]), lam=row(full["rg_lambda"][j]), **norms)
        ffn = dict(w_gu=blocked[GU], w_down=blocked["ffn_w_down"], idx=layer,
                   g_pre=row(norm_ffn_pre[layer]), g_post=row(norm_ffn_post[layer]))
        return mix, ffn

    weights = [layer_weights(layer) for layer in range(DEPTH)]
    act, tape = xt, []
    h = rmsnorm_fwd("l0_mix_norm", act, weights[0][0]["g_pre"])
    for layer in range(DEPTH):
        mix_w, ffn_w = weights[layer]
        mixer_fwd = attn_layer_fwd if layer % 2 == 0 else rg_layer_fwd
        act, h, saved_mix = mixer_fwd(f"l{layer}_mix", act, h, mix_w, ffn_w["g_pre"])
        g_next = weights[layer + 1][0]["g_pre"] if layer + 1 < DEPTH else None
        act, h, saved_ffn = ffn_layer_fwd(f"l{layer}_ffn", act, h, ffn_w, g_next)
        tape.append((mix_w, ffn_w, saved_mix, saved_ffn))
    dact, sq = loss_grad("loss", act, target)
    loss_part = (0.5 * jnp.sum(sq) / d_model).reshape(1)

    grads = {}
    last = tape[DEPTH - 1]
    dm, grads[("ffn_post", DEPTH - 1)] = norm_bwd(f"l{DEPTH - 1}_ffn_dpost", dact, last[3][-1], last[1]["g_post"],
                                                 None, BF16)
    for layer in reversed(range(DEPTH)):
        mix_w, ffn_w, saved_mix, saved_ffn = tape[layer]
        dact, dm, grads[("mix_post", layer)], grads[("ffn", layer)] = ffn_layer_bwd(
            f"l{layer}_ffn", dm, dact, saved_ffn, ffn_w, (saved_mix[-1], mix_w["g_post"]))
        mixer_bwd = attn_layer_bwd if layer % 2 == 0 else rg_layer_bwd
        prev = (tape[layer - 1][3][-1], tape[layer - 1][1]["g_post"]) if layer else None
        dact, dm, grads[("ffn_post", layer - 1)], grads[("mix", layer)] = mixer_bwd(
            f"l{layer}_mix", dm, dact, saved_mix, mix_w, prev)
    attn_g = [grads[("mix", l)] for l in range(0, DEPTH, 2)]
    rg_g = [grads[("mix", l)] for l in range(1, DEPTH, 2)]
    ffn_g = [grads[("ffn", l)] for l in range(DEPTH)]
    stack = lambda gs, key: jnp.stack([g[key] for g in gs])
    by_owner = lambda gs, key, f: jnp.stack([f(g[key]) for g in gs], axis=1)
    rows8 = lambda a: a.reshape(N_DEV, -1, a.shape[-1])
    ungates = lambda a: jnp.swapaxes(a.reshape(LRU_BLOCKS, N_DEV, -1, a.shape[-1]), 0, 1)
    same = lambda a: a
    blocked_g = dict(
        attn_w_in=by_owner(attn_g, "w_in", same), attn_w_out=by_owner(attn_g, "w_out", rows8),
        rg_w_in=by_owner(rg_g, "w_in", same), rg_w_out=by_owner(rg_g, "w_out", rows8),
        rg_w_a=by_owner(rg_g, "w_a", ungates), rg_w_i=by_owner(rg_g, "w_i", ungates),
        **{GU: by_owner(ffn_g, "w_gu", same)},
        ffn_w_down=by_owner(ffn_g, "w_down", same))
    contrib = dict(
        attn_rel_bias=stack(attn_g, "rel_bias"), rg_conv_w=stack(rg_g, "conv_w")[:, :, None, :],
        rg_conv_b=stack(rg_g, "conv_b")[:, 0], rg_b_a=stack(rg_g, "b_a").reshape(rg_b_a.shape[0], LRU_BLOCKS, -1),
        rg_b_i=stack(rg_g, "b_i").reshape(rg_b_i.shape[0], LRU_BLOCKS, -1), rg_lambda=stack(rg_g, "lam")[:, 0],
        norm_mix_pre=jnp.concatenate([grads[("mix", l)]["g_pre"] for l in range(DEPTH)]),
        norm_mix_post=jnp.concatenate([grads[("mix_post", l)] for l in range(DEPTH)]),
        norm_ffn_pre=jnp.concatenate([g["g_pre"] for g in ffn_g]),
        norm_ffn_post=jnp.concatenate([grads[("ffn_post", l)] for l in range(DEPTH)]),
    )
    small_g = _pack_blocked([_to_blocked(contrib[n], axis_of[n]) for n in SMALL], F32, 8)

    slabs = [blocked_g[n].reshape(4, 2, -1, blocked_g[n].shape[-1]) for n in BIG] + [small_g.reshape(4, 2, -1, LANES)]
    core = lax.axis_index("c").astype(jnp.int32).reshape(1)
    from_sibling = exchange_pair("rs_pair", slabs)
    pairs = [pair_sum(f"rs_pair_sum_{i}", g, l, core, BF16 if i < len(BIG) else F32)
             for i, (g, l) in enumerate(zip(slabs, from_sibling))]
    by_chip = exchange_chips("rs_chips", pairs)
    result = {}
    kinds = ("grad", "delta", "new_m", "new_v")
    for n, parts in zip(BIG, by_chip):
        outs = adamw("adamw_" + n, parts, *[rows2d(d[n]) for d in (w_loc, m_loc, v_loc)])
        for kind, a in zip(kinds, outs):
            result[(kind, n)] = a.reshape(w_loc[n].shape)
    for kind in kinds:
        result[(kind, "ffn_w_gate")], result[(kind, "ffn_w_up")] = split_gu(result.pop((kind, GU)), f8)
    outs = adamw("adamw_small", by_chip[-1], *[_pack([d[n] for n in SMALL], F32, 8) for d in (w_loc, m_loc, v_loc)])
    for kind, buf in zip(kinds, outs):
        result.update({(kind, n): a for n, a in zip(SMALL, _unpack(buf, small_shapes))})
    rep_shapes = [w_loc[n].shape for n in REPLICATED] + [(1,)]
    rep_parts, = all_gather("gather_rep_grads", [_pack([contrib[n] for n in REPLICATED] + [loss_part], F32, 8)])
    outs = adamw("adamw_replicated", rep_parts, *[_pack([d[n] for n in REPLICATED] + [jnp.zeros((1,), F32)], F32, 8)
                                                  for d in (w_loc, m_loc, v_loc)])
    for kind, buf in zip(kinds, outs):
        result.update({(kind, n): a for n, a in zip(REPLICATED + ("loss",), _unpack(buf, rep_shapes))})
    loss = result[("grad", "loss")][0]
    return (loss, dact[None], *[result[(kind, n)] for kind in kinds for n in WEIGHTS])
```

```python
import math

import jax
import jax.numpy as jnp
from jax import lax
from jax.experimental import pallas as pl
from jax.experimental.pallas import tpu as pltpu

F32 = jnp.float32
BF16 = jnp.bfloat16

N_DEV = 8
DEPTH = 4
CHUNK = 64
N_LEFT = 8
BAND = (N_LEFT + 1) * CHUNK
PAD_KEYS = N_LEFT * CHUNK
HEAD_DIM = 64
N_HEADS = 8
REL_CLIP = 256
LRU_BLOCKS = 4
LRU_C = 8.0
RMS_EPS = 1e-6
QK_SCALE = HEAD_DIM ** -0.5

ADAM_LR = 0.001
ADAM_B1 = 0.9
ADAM_B2 = 0.999
ADAM_EPS = 1e-08
ADAM_WD = 0.01
ADAM_STEP = 10

LANES = 1024
V7X_VMEM_LIMIT = 56 * 1024 * 1024

MESH = pl.DeviceIdType.MESH
ANY = pl.BlockSpec(memory_space=pl.ANY)

SHARDED = (
    ("attn_w_in", 2), ("attn_w_out", 1), ("rg_w_in", 2), ("rg_conv_w", 3), ("rg_conv_b", 1),
    ("rg_w_a", 2), ("rg_b_a", 2), ("rg_w_i", 2), ("rg_b_i", 2), ("rg_lambda", 1), ("rg_w_out", 1),
    ("ffn_w_gate", 2), ("ffn_w_up", 2), ("ffn_w_down", 1),
)
REPLICATED = ("attn_rel_bias", "norm_mix_pre", "norm_mix_post", "norm_ffn_pre", "norm_ffn_post")
WEIGHTS = ("attn_w_in", "attn_rel_bias", "attn_w_out", "rg_w_in", "rg_conv_w", "rg_conv_b", "rg_w_a", "rg_b_a",
           "rg_w_i", "rg_b_i", "rg_lambda", "rg_w_out", "norm_mix_pre", "norm_mix_post", "norm_ffn_pre",
           "norm_ffn_post", "ffn_w_gate", "ffn_w_up", "ffn_w_down")


def _params(*dims):
    return pltpu.CompilerParams(dimension_semantics=dims or None, vmem_limit_bytes=V7X_VMEM_LIMIT)


def _sds(shape, dtype):
    return jax.ShapeDtypeStruct(tuple(shape), dtype)


def _row_tile(n, pref):
    t = min(n, pref)
    assert n % t == 0, (n, pref)
    return t


def _divisor_tile(n, limit, multiple):
    if n <= limit:
        return n
    best = max(t for t in range(multiple, limit + 1, multiple) if n % t == 0)
    return best


NN = (((1,), (0,)), ((), ()))
NT = (((1,), (1,)), ((), ()))
TN = (((0,), (0,)), ((), ()))


def _gmm(name, a, b, *, grid, a_blk, a_idx, b_blk, b_idx, o_blk, o_idx, out_shape, out_dtype, dn, acc_shape):
    nk = grid[-1]
    kax = len(grid) - 1

    def body(a_ref, b_ref, o_ref, acc_ref):
        part = lax.dot_general(a_ref[...], b_ref[...], dn, preferred_element_type=F32)
        if nk == 1:
            o_ref[...] = part.astype(o_ref.dtype)
            return
        k = pl.program_id(kax)

        @pl.when(k == 0)
        def _():
            acc_ref[...] = part

        @pl.when(k > 0)
        def _():
            acc_ref[...] += part

        @pl.when(k == nk - 1)
        def _():
            o_ref[...] = acc_ref[...].astype(o_ref.dtype)

    return pl.pallas_call(
        body, grid=grid,
        in_specs=[pl.BlockSpec(a_blk, a_idx), pl.BlockSpec(b_blk, b_idx)],
        out_specs=pl.BlockSpec(o_blk, o_idx),
        out_shape=_sds(out_shape, out_dtype),
        scratch_shapes=[pltpu.VMEM(acc_shape, F32)],
        compiler_params=_params(*(["parallel"] * kax + ["arbitrary"])),
        name=name,
    )(a, b)


def mm_nn(name, a, b, out_dtype, tm=1024, tn=512, tk=1024):
    (m, k), (_, n) = a.shape, b.shape
    tm, tn, tk = _row_tile(m, tm), _row_tile(n, tn), _row_tile(k, tk)
    return _gmm(name, a, b, grid=(m // tm, n // tn, k // tk),
                a_blk=(tm, tk), a_idx=lambda i, j, kk: (i, kk), b_blk=(tk, tn), b_idx=lambda i, j, kk: (kk, j),
                o_blk=(tm, tn), o_idx=lambda i, j, kk: (i, j), out_shape=(m, n), out_dtype=out_dtype, dn=NN,
                acc_shape=(tm, tn))


def mm_nt(name, a, b, out_dtype, tm=1024, tn=512, tk=1024):
    (m, k), (n, _) = a.shape, b.shape
    tm, tn, tk = _row_tile(m, tm), _row_tile(n, tn), _row_tile(k, tk)
    return _gmm(name, a, b, grid=(m // tm, n // tn, k // tk),
                a_blk=(tm, tk), a_idx=lambda i, j, kk: (i, kk), b_blk=(tn, tk), b_idx=lambda i, j, kk: (j, kk),
                o_blk=(tm, tn), o_idx=lambda i, j, kk: (i, j), out_shape=(m, n), out_dtype=out_dtype, dn=NT,
                acc_shape=(tm, tn))


def mm_tn(name, a, b, out_dtype, tm=1024, tn=512, tk=2048):
    (k, m), (_, n) = a.shape, b.shape
    tm, tn, tk = _row_tile(m, tm), _row_tile(n, tn), _row_tile(k, tk)
    return _gmm(name, a, b, grid=(m // tm, n // tn, k // tk),
                a_blk=(tk, tm), a_idx=lambda i, j, kk: (kk, i), b_blk=(tk, tn), b_idx=lambda i, j, kk: (kk, j),
                o_blk=(tm, tn), o_idx=lambda i, j, kk: (i, j), out_shape=(m, n), out_dtype=out_dtype, dn=TN,
                acc_shape=(tm, tn))


MXU_WIDTH = 256


def _blocks_per_step(n8):
    return 1 if n8 % MXU_WIDTH == 0 else 2


def _side_by_side(w_ref, j0, per):
    return w_ref[j0] if per == 1 else jnp.concatenate([w_ref[j0 + j] for j in range(per)], axis=1)


def mm_nn_wblk(name, a, wb, layer, out_dtype, tm=1024):
    (m, k), (nb, _, _, n8) = a.shape, wb.shape
    tm = _row_tile(m, tm)
    per = _blocks_per_step(n8)

    def body(a_ref, w_ref, o_ref):
        o_ref[...] = jnp.dot(a_ref[...], _side_by_side(w_ref, 0, per), preferred_element_type=F32).astype(o_ref.dtype)

    return pl.pallas_call(
        body, grid=(m // tm, nb // per),
        in_specs=[pl.BlockSpec((tm, k), lambda i, j: (i, 0)),
                  pl.BlockSpec((per, None, k, n8), lambda i, j: (j, layer, 0, 0))],
        out_specs=pl.BlockSpec((tm, per * n8), lambda i, j: (i, j)), out_shape=_sds((m, nb * n8), out_dtype),
        compiler_params=_params("parallel", "parallel"), name=name)(a, wb)


def mm_nt_wblk(name, a, wb, layer, out_dtype, tm=1024, tn=512):
    m = a.shape[0]
    nb, _, k, n8 = wb.shape
    tm, tn = _row_tile(m, tm), _row_tile(k, tn)
    per = _blocks_per_step(n8)

    def body(a_ref, b_ref, o_ref):
        acc = None
        for j in range(0, nb, per):
            part = lax.dot_general(a_ref[:, j * n8:(j + per) * n8], _side_by_side(b_ref, j, per), NT,
                                   preferred_element_type=F32)
            acc = part if acc is None else acc + part
        o_ref[...] = acc.astype(o_ref.dtype)

    return pl.pallas_call(
        body, grid=(m // tm, k // tn),
        in_specs=[pl.BlockSpec((tm, nb * n8), lambda i, j: (i, 0)),
                  pl.BlockSpec((nb, None, tn, n8), lambda i, j: (0, layer, j, 0))],
        out_specs=pl.BlockSpec((tm, tn), lambda i, j: (i, j)), out_shape=_sds((m, k), out_dtype),
        compiler_params=_params("parallel", "parallel"), name=name)(a, wb)


def mm_tn_oblk(name, a, b, n8, out_dtype, tk=4096):
    (t, k), nb = a.shape, b.shape[1] // n8
    tk = _row_tile(t, tk)
    per = _blocks_per_step(n8)
    nk = t // tk

    def body(a_ref, b_ref, o_ref, acc_ref):
        s = pl.program_id(1)
        part = lax.dot_general(a_ref[...], b_ref[...], TN, preferred_element_type=F32)

        @pl.when(s == 0)
        def _():
            acc_ref[...] = part

        @pl.when(s > 0)
        def _():
            acc_ref[...] += part

        @pl.when(s == nk - 1)
        def _():
            for j in range(per):
                o_ref[j] = acc_ref[:, j * n8:(j + 1) * n8].astype(o_ref.dtype)

    return pl.pallas_call(
        body, grid=(nb // per, nk),
        in_specs=[pl.BlockSpec((tk, k), lambda j, s: (s, 0)), pl.BlockSpec((tk, per * n8), lambda j, s: (s, j))],
        out_specs=pl.BlockSpec((per, k, n8), lambda j, s: (j, 0, 0)), out_shape=_sds((nb, k, n8), out_dtype),
        scratch_shapes=[pltpu.VMEM((k, per * n8), F32)],
        compiler_params=_params("parallel", "arbitrary"), name=name)(a, b)


def rmsnorm_fwd(name, x, g):
    t, d = x.shape
    tr = _row_tile(t, 512)

    def body(x_ref, g_ref, o_ref):
        xv = x_ref[...]
        r = lax.rsqrt(jnp.mean(xv * xv, axis=-1, keepdims=True) + RMS_EPS)
        o_ref[...] = (xv * r * g_ref[...]).astype(o_ref.dtype)

    return pl.pallas_call(
        body, grid=(t // tr,),
        in_specs=[pl.BlockSpec((tr, d), lambda i: (i, 0)), pl.BlockSpec((1, d), lambda i: (0, 0))],
        out_specs=pl.BlockSpec((tr, d), lambda i: (i, 0)),
        out_shape=_sds((t, d), BF16), compiler_params=_params("parallel"), name=name)(x, g)


def resid_norm_fwd(name, x, m, g, g_next):
    t, d = x.shape
    tr = _row_tile(t, 512)
    chained = g_next is not None

    def body(*refs):
        x_ref, m_ref, g_ref = refs[:3]
        mv = m_ref[...]
        r = lax.rsqrt(jnp.mean(mv * mv, axis=-1, keepdims=True) + RMS_EPS)
        x1 = x_ref[...] + mv * r * g_ref[...]
        if chained:
            gn_ref, o_ref, h_ref = refs[3:]
            r1 = lax.rsqrt(jnp.mean(x1 * x1, axis=-1, keepdims=True) + RMS_EPS)
            h_ref[...] = (x1 * r1 * gn_ref[...]).astype(BF16)
        else:
            o_ref, = refs[3:]
        o_ref[...] = x1

    row = pl.BlockSpec((tr, d), lambda i: (i, 0))
    vec = pl.BlockSpec((1, d), lambda i: (0, 0))
    out = pl.pallas_call(
        body, grid=(t // tr,),
        in_specs=[row, row, vec] + ([vec] if chained else []),
        out_specs=[row, row] if chained else [row],
        out_shape=[_sds((t, d), F32)] + ([_sds((t, d), BF16)] if chained else []),
        compiler_params=_params("parallel"), name=name)(*([x, m, g] + ([g_next] if chained else [])))
    return (out[0], out[1]) if chained else (out[0], None)


def norm_bwd(name, dy, x, g, resid, out_dtype):
    t, d = x.shape
    tr = _row_tile(t, 512)
    has_res = resid is not None

    def body(*refs):
        if has_res:
            dy_ref, x_ref, g_ref, r_ref, dx_ref, dg_ref = refs
        else:
            dy_ref, x_ref, g_ref, dx_ref, dg_ref = refs
        i = pl.program_id(0)
        xv = x_ref[...]
        dyv = dy_ref[...].astype(F32)
        r = lax.rsqrt(jnp.mean(xv * xv, axis=-1, keepdims=True) + RMS_EPS)
        xh = xv * r
        dxh = dyv * g_ref[...]
        dx = r * (dxh - xh * jnp.mean(dxh * xh, axis=-1, keepdims=True))
        if has_res:
            dx = dx + r_ref[...]
        dx_ref[...] = dx.astype(dx_ref.dtype)
        part = jnp.sum(dyv * xh, axis=0, keepdims=True)

        @pl.when(i == 0)
        def _():
            dg_ref[...] = part

        @pl.when(i > 0)
        def _():
            dg_ref[...] += part

    row = pl.BlockSpec((tr, d), lambda i: (i, 0))
    vec = pl.BlockSpec((1, d), lambda i: (0, 0))
    ins = [dy, x, g] + ([resid] if has_res else [])
    return pl.pallas_call(
        body, grid=(t // tr,),
        in_specs=[row, row, vec] + ([row] if has_res else []),
        out_specs=[row, vec],
        out_shape=[_sds((t, d), out_dtype), _sds((1, d), F32)],
        compiler_params=_params("arbitrary"), name=name)(*ins)


def norm_bwd_chain(name, dh, x, g, resid, m_prev, g_prev):
    t, d = x.shape
    tr = _row_tile(t, 512)

    def body(dh_ref, x_ref, g_ref, r_ref, m_ref, gp_ref, dx_ref, dg_ref, dm_ref, dgp_ref):
        i = pl.program_id(0)
        xv = x_ref[...]
        dhv = dh_ref[...]
        r = lax.rsqrt(jnp.mean(xv * xv, axis=-1, keepdims=True) + RMS_EPS)
        xh = xv * r
        dxh = dhv * g_ref[...]
        dx = r * (dxh - xh * jnp.mean(dxh * xh, axis=-1, keepdims=True)) + r_ref[...]
        dx_ref[...] = dx
        mv = m_ref[...]
        rm = lax.rsqrt(jnp.mean(mv * mv, axis=-1, keepdims=True) + RMS_EPS)
        mh = mv * rm
        dmh = dx * gp_ref[...]
        dm_ref[...] = (rm * (dmh - mh * jnp.mean(dmh * mh, axis=-1, keepdims=True))).astype(BF16)
        part = jnp.sum(dhv * xh, axis=0, keepdims=True)
        part_prev = jnp.sum(dx * mh, axis=0, keepdims=True)

        @pl.when(i == 0)
        def _():
            dg_ref[...] = part
            dgp_ref[...] = part_prev

        @pl.when(i > 0)
        def _():
            dg_ref[...] += part
            dgp_ref[...] += part_prev

    row = pl.BlockSpec((tr, d), lambda i: (i, 0))
    vec = pl.BlockSpec((1, d), lambda i: (0, 0))
    return pl.pallas_call(
        body, grid=(t // tr,), in_specs=[row, row, vec, row, row, vec], out_specs=[row, vec, row, vec],
        out_shape=[_sds((t, d), F32), _sds((1, d), F32), _sds((t, d), BF16), _sds((1, d), F32)],
        compiler_params=_params("arbitrary"), name=name)(dh, x, g, resid, m_prev, g_prev)


def close_bwd(tag, dh, x, g_pre, dx1, prev):
    if prev is None:
        dx, dg_pre = norm_bwd(tag + "_dpre", dh, x, g_pre, dx1, F32)
        return dx, dg_pre, None, None
    return norm_bwd_chain(tag + "_dpre", dh, x, g_pre, dx1, *prev)


def loss_grad(name, y, target):
    t, d = y.shape
    tr = _row_tile(t, 512)

    def body(y_ref, t_ref, dy_ref, s_ref):
        i = pl.program_id(0)
        err = y_ref[...] - t_ref[...]
        dy_ref[...] = err * (1.0 / d)
        part = jnp.sum(err * err, axis=0, keepdims=True)

        @pl.when(i == 0)
        def _():
            s_ref[...] = part

        @pl.when(i > 0)
        def _():
            s_ref[...] += part

    row = pl.BlockSpec((tr, d), lambda i: (i, 0))
    vec = pl.BlockSpec((1, d), lambda i: (0, 0))
    return pl.pallas_call(
        body, grid=(t // tr,), in_specs=[row, row], out_specs=[row, vec],
        out_shape=[_sds((t, d), F32), _sds((1, d), F32)],
        compiler_params=_params("arbitrary"), name=name)(y, target)


PAIR = 2 * HEAD_DIM
N_PAIRS = N_HEADS // 2
A_TQ = 512
A_UNROLL_FWD = 8
A_UNROLL_BWD = 4


def _halves(x):
    lane = lax.broadcasted_iota(jnp.int32, x.shape, x.ndim - 1)
    zero = jnp.zeros_like(x)
    return jnp.where(lane < HEAD_DIM, x, zero), jnp.where(lane >= HEAD_DIM, x, zero)


def _merge(a, b):
    lane = lax.broadcasted_iota(jnp.int32, a.shape, a.ndim - 1)
    return jnp.where(lane < HEAD_DIM, a, b)


def _a_valid(c):
    col = lax.broadcasted_iota(jnp.int32, (CHUNK, BAND), 1)
    return col >= (N_LEFT - c) * CHUNK


def attn_a_fwd(name, proj, kp, vp, bias, q_blk):
    t = proj.shape[0]
    tq = _row_tile(t, A_TQ)
    ncs = tq // CHUNK
    un = math.gcd(A_UNROLL_FWD, ncs)

    def body(q_ref, k_ref, v_ref, b_ref, o_ref, l_ref):
        i = pl.program_id(1)

        def group(gg, carry):
            cs = [i * ncs + gg * un + u for u in range(un)]
            r0s = [pl.multiple_of((gg * un + u) * CHUNK, CHUNK) for u in range(un)]
            k0s = [pl.multiple_of(c * CHUNK, CHUNK) for c in cs]
            ss = []
            for u in range(un):
                qh = _halves(q_ref[pl.ds(r0s[u], CHUNK), :] * QK_SCALE)
                kwin = k_ref[pl.ds(k0s[u], BAND), :]
                valid = _a_valid(cs[u])
                for hh in range(2):
                    s = lax.dot_general(qh[hh], kwin, NT, preferred_element_type=F32) + b_ref[hh]
                    ss.append(jnp.where(valid, s, -1e30))
            ps, lses = [], []
            for s in ss:
                mx = jnp.max(s, axis=-1, keepdims=True)
                p = jnp.exp(s - mx)
                den = jnp.sum(p, axis=-1, keepdims=True)
                ps.append((p * (1.0 / den)).astype(BF16))
                lses.append(mx + jnp.log(den))
            for u in range(un):
                vwin = v_ref[pl.ds(k0s[u], BAND), :]
                o0 = jnp.dot(ps[2 * u], vwin, preferred_element_type=F32)
                o1 = jnp.dot(ps[2 * u + 1], vwin, preferred_element_type=F32)
                o_ref[pl.ds(r0s[u], CHUNK), :] = _merge(o0, o1)
                l_ref[pl.ds(r0s[u], CHUNK), :] = jnp.concatenate([lses[2 * u], lses[2 * u + 1]], axis=1)
            return carry

        lax.fori_loop(0, ncs // un, group, 0)

    return pl.pallas_call(
        body, grid=(N_PAIRS, t // tq),
        in_specs=[pl.BlockSpec((tq, PAIR), lambda p, i: (i, q_blk + p)),
                  pl.BlockSpec((t + PAD_KEYS, PAIR), lambda p, i: (0, p)),
                  pl.BlockSpec((t + PAD_KEYS, PAIR), lambda p, i: (0, p)),
                  pl.BlockSpec((2, CHUNK, BAND), lambda p, i: (p, 0, 0))],
        out_specs=[pl.BlockSpec((tq, PAIR), lambda p, i: (i, p)),
                   pl.BlockSpec((None, tq, 2), lambda p, i: (p, i, 0))],
        out_shape=[_sds((t, N_PAIRS * PAIR), F32), _sds((N_PAIRS, t, 2), F32)],
        compiler_params=_params("parallel", "parallel"), name=name)(proj, kp, vp, bias)


def attn_a_bwd(name, proj, kp, vp, bias, o, lse, do, q_blk, do_blk):
    t = proj.shape[0]
    tq = _row_tile(t, A_TQ)
    ncs = tq // CHUNK
    un = math.gcd(A_UNROLL_BWD, ncs)

    def body(q_ref, k_ref, v_ref, b_ref, o_ref, l_ref, do_ref, dq_ref, dk_ref, dv_ref, db_ref):
        i = pl.program_id(1)

        @pl.when(i == 0)
        def _():
            dk_ref[...] = jnp.zeros_like(dk_ref)
            dv_ref[...] = jnp.zeros_like(dv_ref)
            db_ref[...] = jnp.zeros_like(db_ref)

        def group(gg, carry):
            cs = [i * ncs + gg * un + u for u in range(un)]
            r0s = [pl.multiple_of((gg * un + u) * CHUNK, CHUNK) for u in range(un)]
            k0s = [pl.multiple_of(c * CHUNK, CHUNK) for c in cs]
            qhs, dohs, ps, dps, deltas = [], [], [], [], []
            for u in range(un):
                rows = pl.ds(r0s[u], CHUNK)
                qh = _halves(q_ref[rows, :] * QK_SCALE)
                doh = _halves(do_ref[rows, :])
                kwin = k_ref[pl.ds(k0s[u], BAND), :]
                vwin = v_ref[pl.ds(k0s[u], BAND), :]
                valid = _a_valid(cs[u])
                dl = _halves(do_ref[rows, :].astype(F32) * o_ref[rows, :])
                for hh in range(2):
                    s = lax.dot_general(qh[hh], kwin, NT, preferred_element_type=F32) + b_ref[hh]
                    ps.append(jnp.where(valid, jnp.exp(s - l_ref[rows, hh:hh + 1]), 0.0))
                    dps.append(lax.dot_general(doh[hh], vwin, NT, preferred_element_type=F32))
                    deltas.append(jnp.sum(dl[hh], axis=-1, keepdims=True))
                qhs.append(qh)
                dohs.append(doh)
            dss = [p * (dp - dl) for p, dp, dl in zip(ps, dps, deltas)]
            for hh in range(2):
                tot = dss[hh]
                for u in range(1, un):
                    tot = tot + dss[2 * u + hh]
                db_ref[hh] += tot
            for u in range(un):
                kwin = k_ref[pl.ds(k0s[u], BAND), :]
                ds0, ds1 = dss[2 * u].astype(BF16), dss[2 * u + 1].astype(BF16)
                dq_ref[pl.ds(r0s[u], CHUNK), :] = _merge(jnp.dot(ds0, kwin, preferred_element_type=F32),
                                                         jnp.dot(ds1, kwin, preferred_element_type=F32)) * QK_SCALE
                dk_ref[pl.ds(k0s[u], BAND), :] += (lax.dot_general(ds0, qhs[u][0], TN, preferred_element_type=F32)
                                                   + lax.dot_general(ds1, qhs[u][1], TN, preferred_element_type=F32))
                dv_ref[pl.ds(k0s[u], BAND), :] += (
                    lax.dot_general(ps[2 * u].astype(BF16), dohs[u][0], TN, preferred_element_type=F32)
                    + lax.dot_general(ps[2 * u + 1].astype(BF16), dohs[u][1], TN, preferred_element_type=F32))
            return carry

        lax.fori_loop(0, ncs // un, group, 0)

    tile = lambda blk: pl.BlockSpec((tq, PAIR), lambda p, i: (i, blk + p))
    whole = pl.BlockSpec((t + PAD_KEYS, PAIR), lambda p, i: (0, p))
    bspec = pl.BlockSpec((2, CHUNK, BAND), lambda p, i: (p, 0, 0))
    return pl.pallas_call(
        body, grid=(N_PAIRS, t // tq),
        in_specs=[tile(q_blk), whole, whole, bspec, tile(0), pl.BlockSpec((None, tq, 2), lambda p, i: (p, i, 0)),
                  tile(do_blk)],
        out_specs=[tile(0), whole, whole, bspec],
        out_shape=[_sds((t, N_PAIRS * PAIR), F32), _sds((t + PAD_KEYS, N_PAIRS * PAIR), F32),
                   _sds((t + PAD_KEYS, N_PAIRS * PAIR), F32), _sds((2 * N_PAIRS, CHUNK, BAND), F32)],
        compiler_params=_params("parallel", "arbitrary"), name=name)(proj, kp, vp, bias, o, lse, do)


SB_TQ = 256
SB_TK = 256
SB_DEAD = -125.0


def _tri(n, strict):
    j = lax.broadcasted_iota(jnp.int32, (n, n), 0)
    s = lax.broadcasted_iota(jnp.int32, (n, n), 1)
    return jnp.where((j > s) if strict else (j >= s), 1.0, 0.0).astype(BF16)


def _suffix_sum(x, tri, exact):
    hi = x.astype(BF16)
    out = jnp.dot(hi, tri, preferred_element_type=F32)
    if exact:
        lo = (x - hi.astype(F32)).astype(BF16)
        out = out + jnp.dot(lo, tri, preferred_element_type=F32)
    return out


def _sb_scores(qh, ks, causal):
    z = lax.dot_general(qh, ks, NT, preferred_element_type=F32)
    lb = jnp.minimum(z, 0.0) - jnp.log(1.0 + jnp.exp(-jnp.abs(z)))
    m = lb - z
    if causal is not None:
        m = jnp.where(causal, m, 0.0)
    return lb, m


def _causal(tq, tk, off):
    return (lax.broadcasted_iota(jnp.int32, (tq, tk), 1) + off * tk) < lax.broadcasted_iota(jnp.int32, (tq, tk), 0)


def sb_fwd(name, proj, q_blk, k_blk, v_blk):
    t = proj.shape[0]
    tq = _row_tile(t, SB_TQ)
    tk = min(SB_TK, tq)
    per = tq // tk

    def body(q_ref, k_ref, v_ref, o_ref):
        i = pl.program_id(1)
        tri = _tri(tk, True)
        qh = _halves(q_ref[...] * QK_SCALE)

        def blocks(kb, carry, off):
            k0 = pl.multiple_of(kb * tk, tk)
            ks, vs = k_ref[pl.ds(k0, tk), :], v_ref[pl.ds(k0, tk), :]
            causal = None if off is None else _causal(tq, tk, off)
            lbm = [_sb_scores(qh[hh], ks, causal) for hh in range(2)]
            afters = [_suffix_sum(lbm[hh][1], tri, False) for hh in range(2)]
            out = []
            for hh in range(2):
                acc, cm = carry[2 * hh], carry[2 * hh + 1]
                w = jnp.exp(lbm[hh][0] + afters[hh] + cm)
                if causal is not None:
                    w = jnp.where(causal, w, 0.0)
                out += [acc + jnp.dot(w.astype(BF16), vs, preferred_element_type=F32),
                        cm + jnp.sum(lbm[hh][1], axis=-1, keepdims=True)]
            return tuple(out)

        def alive(carry):
            return jnp.maximum(jnp.max(carry[1]), jnp.max(carry[3])) > SB_DEAD

        carry = (jnp.zeros((tq, PAIR), F32), jnp.zeros((tq, 1), F32)) * 2
        for off in reversed(range(per)):
            carry = blocks(i * per + off, carry, off)

        def step(c):
            new = blocks(i * per - 1 - c[0], c[2:], None)
            return (c[0] + 1, alive(new)) + new

        out = lax.while_loop(lambda c: jnp.logical_and(c[0] < i * per, c[1]), step,
                             (jnp.int32(0), alive(carry)) + carry)
        o_ref[...] = _merge(out[2], out[4])

    return pl.pallas_call(
        body, grid=(N_PAIRS, t // tq),
        in_specs=[pl.BlockSpec((tq, PAIR), lambda p, i: (i, q_blk + p)),
                  pl.BlockSpec((t, PAIR), lambda p, i: (0, k_blk + p)),
                  pl.BlockSpec((t, PAIR), lambda p, i: (0, v_blk + p))],
        out_specs=pl.BlockSpec((tq, PAIR), lambda p, i: (i, p)),
        out_shape=_sds((t, N_PAIRS * PAIR), F32), compiler_params=_params("parallel", "parallel"), name=name,
    )(proj, proj, proj)


def sb_bwd(name, proj, o, do, q_blk, k_blk, v_blk, do_blk):
    t = proj.shape[0]
    tq = _row_tile(t, SB_TQ)
    tk = min(SB_TK, tq)
    per = tq // tk

    def body(q_ref, k_ref, v_ref, o_ref, do_ref, dq_ref, dk_ref, dv_ref):
        i = pl.program_id(1)

        @pl.when(i == 0)
        def _():
            dk_ref[...] = jnp.zeros_like(dk_ref)
            dv_ref[...] = jnp.zeros_like(dv_ref)

        tri_s, tri_i = _tri(tk, True), _tri(tk, False)
        qh = _halves(q_ref[...] * QK_SCALE)
        doh = _halves(do_ref[...])
        deltas = [jnp.sum(x, axis=-1, keepdims=True) for x in _halves(do_ref[...].astype(F32) * o_ref[...])]

        def blocks(kb, carry, off):
            k0 = pl.multiple_of(kb * tk, tk)
            ks, vs = k_ref[pl.ds(k0, tk), :], v_ref[pl.ds(k0, tk), :]
            causal = None if off is None else _causal(tq, tk, off)
            lbm = [_sb_scores(qh[hh], ks, causal) for hh in range(2)]
            dws = [lax.dot_general(doh[hh], vs, NT, preferred_element_type=F32) for hh in range(2)]
            afters = [_suffix_sum(lbm[hh][1], tri_s, False) for hh in range(2)]
            wbs, es = [], []
            for hh in range(2):
                w = jnp.exp(lbm[hh][0] + afters[hh] + carry[3 * hh + 1])
                if causal is not None:
                    w = jnp.where(causal, w, 0.0)
                wbs.append(w.astype(BF16))
                es.append(wbs[hh].astype(F32) * dws[hh])
            sfx = [_suffix_sum(es[hh], tri_i, True) for hh in range(2)]
            dzs = []
            for hh in range(2):
                left = deltas[hh] - (sfx[hh] + carry[3 * hh + 2])
                sig = jnp.exp(lbm[hh][0])
                dz = es[hh] * (1.0 - sig) - left * sig
                if causal is not None:
                    dz = jnp.where(causal, dz, 0.0)
                dzs.append(dz.astype(BF16))
            dk_ref[pl.ds(k0, tk), :] += (lax.dot_general(dzs[0], qh[0], TN, preferred_element_type=F32)
                                         + lax.dot_general(dzs[1], qh[1], TN, preferred_element_type=F32))
            dv_ref[pl.ds(k0, tk), :] += (lax.dot_general(wbs[0], doh[0], TN, preferred_element_type=F32)
                                         + lax.dot_general(wbs[1], doh[1], TN, preferred_element_type=F32))
            out = []
            for hh in range(2):
                out += [carry[3 * hh] + jnp.dot(dzs[hh], ks, preferred_element_type=F32),
                        carry[3 * hh + 1] + jnp.sum(lbm[hh][1], axis=-1, keepdims=True),
                        carry[3 * hh + 2] + jnp.sum(es[hh], axis=-1, keepdims=True)]
            return tuple(out)

        def alive(carry):
            return jnp.maximum(jnp.max(carry[1]), jnp.max(carry[4])) > SB_DEAD

        zero = jnp.zeros((tq, 1), F32)
        carry = (jnp.zeros((tq, PAIR), F32), zero, zero) * 2
        for off in reversed(range(per)):
            carry = blocks(i * per + off, carry, off)

        def step(c):
            new = blocks(i * per - 1 - c[0], c[2:], None)
            return (c[0] + 1, alive(new)) + new

        out = lax.while_loop(lambda c: jnp.logical_and(c[0] < i * per, c[1]), step,
                             (jnp.int32(0), alive(carry)) + carry)
        dq_ref[...] = _merge(out[2], out[5]) * QK_SCALE

    tile = lambda blk: pl.BlockSpec((tq, PAIR), lambda p, i: (i, blk + p))
    whole = lambda blk: pl.BlockSpec((t, PAIR), lambda p, i: (0, blk + p))
    return pl.pallas_call(
        body, grid=(N_PAIRS, t // tq),
        in_specs=[tile(q_blk), whole(k_blk), whole(v_blk), tile(0), tile(do_blk)],
        out_specs=[tile(0), whole(0), whole(0)],
        out_shape=[_sds((t, N_PAIRS * PAIR), F32)] * 3,
        compiler_params=_params("parallel", "arbitrary"), name=name)(proj, proj, proj, o, do)


def _sigmoid(x):
    return 1.0 / (1.0 + jnp.exp(-x))


def gu_gap(f8):
    return -(-f8 // 128) * 128


def merge_gu(gate, up):
    f8 = gate.shape[-1]
    pad = jnp.zeros(gate.shape[:-1] + (gu_gap(f8) - f8,), gate.dtype)
    return jnp.concatenate([gate, pad, up], axis=-1)


def split_gu(gu, f8):
    return gu[..., :f8], gu[..., gu_gap(f8):]


PER = 2


def _stacked_pair(w_ref, p, f8):
    zeros = jnp.zeros((gu_gap(f8) - f8, w_ref.shape[-1]), w_ref.dtype)
    return jnp.concatenate([w_ref[PER * p], zeros, w_ref[PER * p + 1]], axis=0)


def ffn_up(name, h, wgu, f8, layer):
    t, d = h.shape
    nb, _, _, fw = wgu.shape
    gap = gu_gap(f8)
    tm = _row_tile(t, 1024)

    def body(h_ref, w_ref, gu_ref, a_ref):
        hv = h_ref[...]
        rs = [jnp.dot(hv, w_ref[j], preferred_element_type=F32) for j in range(PER)]
        a_ref[...] = jnp.zeros_like(a_ref)
        for j in range(PER):
            gu_ref[j] = rs[j].astype(BF16)
            g, u = rs[j][:, :f8], rs[j][:, gap:]
            a_ref[:, j * gap:j * gap + f8] = (g * _sigmoid(g) * u).astype(BF16)

    return pl.pallas_call(
        body, grid=(t // tm, nb // PER),
        in_specs=[pl.BlockSpec((tm, d), lambda i, k: (i, 0)),
                  pl.BlockSpec((PER, None, d, fw), lambda i, k: (k, layer, 0, 0))],
        out_specs=[pl.BlockSpec((PER, tm, fw), lambda i, k: (k, i, 0)),
                   pl.BlockSpec((None, tm, fw), lambda i, k: (k, i, 0))],
        out_shape=[_sds((nb, t, fw), BF16), _sds((nb // PER, t, fw), BF16)],
        compiler_params=_params("parallel", "parallel"), name=name)(h, wgu)


def _mm_all_blocks(name, a, w, layer, dn, tm):
    nb, t, f = a.shape
    wshape = w.shape[2:]
    d = wshape[1] if dn == NN else wshape[0]
    tm = _row_tile(t, tm)

    def body(a_ref, w_ref, o_ref):
        acc = lax.dot_general(a_ref[0], w_ref[0], dn, preferred_element_type=F32)
        for k in range(1, nb):
            acc = acc + lax.dot_general(a_ref[k], w_ref[k], dn, preferred_element_type=F32)
        o_ref[...] = acc

    return pl.pallas_call(
        body, grid=(t // tm,),
        in_specs=[pl.BlockSpec((nb, tm, f), lambda i: (0, i, 0)),
                  pl.BlockSpec((nb, None) + wshape, lambda i: (0, layer, 0, 0))],
        out_specs=pl.BlockSpec((tm, d), lambda i: (i, 0)), out_shape=_sds((t, d), F32),
        compiler_params=_params("parallel"), name=name)(a, w)


def ffn_down(name, a, wd, f8, layer):
    npair, t, fw = a.shape
    nb, _, _, d = wd.shape
    tm = _row_tile(t, 512)

    def body(a_ref, w_ref, o_ref):
        acc = jnp.dot(a_ref[0], _stacked_pair(w_ref, 0, f8), preferred_element_type=F32)
        for p in range(1, npair):
            acc = acc + jnp.dot(a_ref[p], _stacked_pair(w_ref, p, f8), preferred_element_type=F32)
        o_ref[...] = acc

    return pl.pallas_call(
        body, grid=(t // tm,),
        in_specs=[pl.BlockSpec((npair, tm, fw), lambda i: (0, i, 0)),
                  pl.BlockSpec((nb, None, f8, d), lambda i: (0, layer, 0, 0))],
        out_specs=pl.BlockSpec((tm, d), lambda i: (i, 0)), out_shape=_sds((t, d), F32),
        compiler_params=_params("parallel"), name=name)(a, wd)


def ffn_bwd_act(name, dm, wd, gu, f8, layer):
    t, d = dm.shape
    nb, _, fw = gu.shape
    gap = gu_gap(f8)
    tm = _row_tile(t, 1024)

    def body(dm_ref, wd_ref, gu_ref, o_ref):
        da = lax.dot_general(dm_ref[...], _stacked_pair(wd_ref, 0, f8), NT, preferred_element_type=F32)
        o_ref[...] = jnp.zeros_like(o_ref)
        for j in range(PER):
            daj = da[:, j * gap:j * gap + f8]
            gv = gu_ref[j, :, :f8].astype(F32)
            uv = gu_ref[j, :, gap:].astype(F32)
            sg = _sigmoid(gv)
            o_ref[j, :, :f8] = (daj * uv * sg * (1.0 + gv * (1.0 - sg))).astype(BF16)
            o_ref[j, :, gap:] = (daj * gv * sg).astype(BF16)

    bspec = pl.BlockSpec((PER, tm, fw), lambda i, k: (k, i, 0))
    return pl.pallas_call(
        body, grid=(t // tm, nb // PER),
        in_specs=[pl.BlockSpec((tm, d), lambda i, k: (i, 0)),
                  pl.BlockSpec((PER, None, f8, d), lambda i, k: (k, layer, 0, 0)), bspec],
        out_specs=bspec, out_shape=_sds((nb, t, fw), BF16),
        compiler_params=_params("parallel", "parallel"), name=name)(dm, wd, gu)


def ffn_bwd_dh(name, dgu, wgu, layer):
    return _mm_all_blocks(name, dgu, wgu, layer, NT, 512)


def ffn_dw_in(name, h, dact):
    t, d = h.shape
    nb, _, f8 = dact.shape
    tk = _row_tile(t, 4096)
    return _gmm(name, h, dact, grid=(nb, t // tk),
                a_blk=(tk, d), a_idx=lambda b, s: (s, 0), b_blk=(None, tk, f8), b_idx=lambda b, s: (b, s, 0),
                o_blk=(None, d, f8), o_idx=lambda b, s: (b, 0, 0), out_shape=(nb, d, f8), out_dtype=F32, dn=TN,
                acc_shape=(d, f8))


def ffn_dw_down(name, a, dm, f8):
    npair, t, fw = a.shape
    d = dm.shape[1]
    gap = gu_gap(f8)
    tk = _row_tile(t, 4096)
    nk = t // tk

    def body(a_ref, dm_ref, o_ref, acc_ref):
        s = pl.program_id(1)
        part = lax.dot_general(a_ref[...], dm_ref[...], TN, preferred_element_type=F32)

        @pl.when(s == 0)
        def _():
            acc_ref[...] = part

        @pl.when(s > 0)
        def _():
            acc_ref[...] += part

        @pl.when(s == nk - 1)
        def _():
            for j in range(PER):
                o_ref[j] = acc_ref[j * gap:j * gap + f8, :]

    return pl.pallas_call(
        body, grid=(npair, nk),
        in_specs=[pl.BlockSpec((None, tk, fw), lambda p, s: (p, s, 0)), pl.BlockSpec((tk, d), lambda p, s: (s, 0))],
        out_specs=pl.BlockSpec((PER, f8, d), lambda p, s: (p, 0, 0)), out_shape=_sds((PER * npair, f8, d), F32),
        scratch_shapes=[pltpu.VMEM((fw, d), F32)],
        compiler_params=_params("parallel", "arbitrary"), name=name)(a, dm)


GELU_C = math.sqrt(2.0 / math.pi)
GELU_A = 0.044715


def _gelu(x):
    return 0.5 * x * (1.0 + jnp.tanh(GELU_C * (x + GELU_A * x * x * x)))


def _gelu_grad(x):
    th = jnp.tanh(GELU_C * (x + GELU_A * x * x * x))
    return 0.5 * (1.0 + th) + 0.5 * x * (1.0 - th * th) * GELU_C * (1.0 + 3.0 * GELU_A * x * x)


def _neg_expm1(x):
    series = x * (1.0 + x * (0.5 + x * (1.0 / 6.0 + x * (1.0 / 24.0 + x * (1.0 / 120.0 + x * (1.0 / 720.0))))))
    return -jnp.where(x > -0.25, series, jnp.exp(x) - 1.0)


CONV_TR = 256
CONV_TAPS = 4
HALO = 8


def _shifted(ext, k, tr, back):
    if back:
        return pltpu.roll(ext, k, 0)[HALO:, :] if k else ext[HALO:, :]
    return pltpu.roll(ext, tr + HALO - k, 0)[:tr, :] if k else ext[:tr, :]


def conv4_fwd(name, src, cb, w, b):
    t, c = src.shape[0], w.shape[1]
    tr = _row_tile(t, CONV_TR)
    hb = tr // HALO

    def body(x_ref, h_ref, w_ref, b_ref, o_ref):
        i = pl.program_id(0)
        ext = jnp.concatenate([jnp.where(i == 0, 0.0, h_ref[...]), x_ref[...]], axis=0)
        acc = b_ref[...]
        for k in range(CONV_TAPS):
            acc = acc + w_ref[CONV_TAPS - 1 - k:CONV_TAPS - k, :] * _shifted(ext, k, tr, True)
        o_ref[...] = acc

    return pl.pallas_call(
        body, grid=(t // tr,),
        in_specs=[pl.BlockSpec((tr, c), lambda i: (i, cb)),
                  pl.BlockSpec((HALO, c), lambda i: (jnp.maximum(i * hb - 1, 0), cb)),
                  pl.BlockSpec((CONV_TAPS, c), lambda i: (0, 0)), pl.BlockSpec((1, c), lambda i: (0, 0))],
        out_specs=pl.BlockSpec((tr, c), lambda i: (i, 0)), out_shape=_sds((t, c), F32),
        compiler_params=_params("parallel"), name=name)(src, src, w, b)


def conv4_bwd_x(name, dy, w):
    t, c = dy.shape
    tr = _row_tile(t, CONV_TR)
    hb = tr // HALO
    last = t // tr - 1

    def body(y_ref, h_ref, w_ref, o_ref):
        i = pl.program_id(0)
        ext = jnp.concatenate([y_ref[...], jnp.where(i == last, 0.0, h_ref[...])], axis=0)
        acc = w_ref[CONV_TAPS - 1:CONV_TAPS, :] * y_ref[...]
        for k in range(1, CONV_TAPS):
            acc = acc + w_ref[CONV_TAPS - 1 - k:CONV_TAPS - k, :] * _shifted(ext, k, tr, False)
        o_ref[...] = acc

    return pl.pallas_call(
        body, grid=(t // tr,),
        in_specs=[pl.BlockSpec((tr, c), lambda i: (i, 0)),
                  pl.BlockSpec((HALO, c), lambda i: (jnp.minimum((i + 1) * hb, t // HALO - 1), 0)),
                  pl.BlockSpec((CONV_TAPS, c), lambda i: (0, 0))],
        out_specs=pl.BlockSpec((tr, c), lambda i: (i, 0)), out_shape=_sds((t, c), F32),
        compiler_params=_params("parallel"), name=name)(dy, dy, w)


def conv4_bwd_w(name, src, cb, dy):
    t, c = dy.shape
    tr = _row_tile(t, CONV_TR)
    hb = tr // HALO

    def body(x_ref, h_ref, dy_ref, dw_ref, db_ref):
        i = pl.program_id(0)

        @pl.when(i == 0)
        def _():
            dw_ref[...] = jnp.zeros_like(dw_ref)
            db_ref[...] = jnp.zeros_like(db_ref)

        ext = jnp.concatenate([jnp.where(i == 0, 0.0, h_ref[...]), x_ref[...]], axis=0)
        dyv = dy_ref[...]
        db_ref[...] += jnp.sum(dyv, axis=0, keepdims=True)
        for k in range(CONV_TAPS):
            dw_ref[CONV_TAPS - 1 - k:CONV_TAPS - k, :] += jnp.sum(dyv * _shifted(ext, k, tr, True), axis=0,
                                                                   keepdims=True)

    return pl.pallas_call(
        body, grid=(t // tr,),
        in_specs=[pl.BlockSpec((tr, c), lambda i: (i, cb)),
                  pl.BlockSpec((HALO, c), lambda i: (jnp.maximum(i * hb - 1, 0), cb)),
                  pl.BlockSpec((tr, c), lambda i: (i, 0))],
        out_specs=[pl.BlockSpec((CONV_TAPS, c), lambda i: (0, 0)), pl.BlockSpec((1, c), lambda i: (0, 0))],
        out_shape=[_sds((CONV_TAPS, c), F32), _sds((1, c), F32)],
        compiler_params=_params("arbitrary"), name=name)(src, src, dy)


def _rg_gate_values(xcv, wa_ref, wi_ref, ba_ref, bi_ref, lam_ref):
    xb = xcv.astype(BF16)
    r = _sigmoid(jnp.dot(xb, wa_ref[...], preferred_element_type=F32) + ba_ref[...])
    ig = _sigmoid(jnp.dot(xb, wi_ref[...], preferred_element_type=F32) + bi_ref[...])
    lam = lam_ref[...]
    sp = jnp.maximum(-lam, 0.0) + jnp.log(1.0 + jnp.exp(-jnp.abs(lam)))
    log_a = -LRU_C * r * sp
    a = jnp.exp(log_a)
    mult = jnp.sqrt(_neg_expm1(2.0 * log_a))
    return xb, r, ig, sp, a, mult


def rg_gates_fwd(name, xc, wa, wi, ba, bi, lam):
    t, c = xc.shape
    nb, cb, _ = wa.shape
    tm = _row_tile(t, 512)

    def body(xc_ref, wa_ref, wi_ref, ba_ref, bi_ref, lam_ref, a_ref, u_ref):
        xcv = xc_ref[...]
        _, _, ig, _, a, mult = _rg_gate_values(xcv, wa_ref, wi_ref, ba_ref, bi_ref, lam_ref)
        a_ref[...] = a
        u_ref[...] = mult * (ig * xcv)

    blk = pl.BlockSpec((tm, cb), lambda i, n: (i, n))
    wsp = pl.BlockSpec((None, cb, cb), lambda i, n: (n, 0, 0))
    vec = pl.BlockSpec((1, cb), lambda i, n: (0, n))
    return pl.pallas_call(
        body, grid=(t // tm, nb), in_specs=[blk, wsp, wsp, vec, vec, vec], out_specs=[blk, blk],
        out_shape=[_sds((t, c), F32)] * 2, compiler_params=_params("parallel", "parallel"), name=name,
    )(xc, wa, wi, ba, bi, lam)


def rg_gates_bwd(name, xc, gu, hs, wa, wi, ba, bi, lam):
    t, c = xc.shape
    nb, cb, _ = wa.shape
    tm = _row_tile(t, 512)
    hb = tm // HALO

    def body(xc_ref, gu_ref, h_ref, halo_ref, wa_ref, wi_ref, ba_ref, bi_ref, lam_ref,
             dxc_ref, dwa_ref, dwi_ref, dba_ref, dbi_ref, dlam_ref):
        i = pl.program_id(1)
        hprev = _shifted(jnp.concatenate([jnp.where(i == 0, 0.0, halo_ref[...]), h_ref[...]], axis=0), 1, tm, True)

        @pl.when(i == 0)
        def _():
            for ref in (dwa_ref, dwi_ref, dba_ref, dbi_ref, dlam_ref):
                ref[...] = jnp.zeros_like(ref)

        xcv = xc_ref[...]
        xb, r, ig, sp, a, mult = _rg_gate_values(xcv, wa_ref, wi_ref, ba_ref, bi_ref, lam_ref)
        gv = gu_ref[...]
        d_ixc = gv * mult
        d_i = d_ixc * xcv
        d_mult = gv * ig * xcv
        d_a = gv * hprev - d_mult * a / mult
        d_log_a = d_a * a
        d_r = d_log_a * (-LRU_C * sp)
        sig_neg_lam = 1.0 / (1.0 + jnp.exp(lam_ref[...]))
        dlam_ref[...] += jnp.sum(d_log_a * r, axis=0, keepdims=True) * (LRU_C * sig_neg_lam)
        dpa = d_r * r * (1.0 - r)
        dpi = d_i * ig * (1.0 - ig)
        dba_ref[...] += jnp.sum(dpa, axis=0, keepdims=True)
        dbi_ref[...] += jnp.sum(dpi, axis=0, keepdims=True)
        dpab, dpib = dpa.astype(BF16), dpi.astype(BF16)
        dxc_ref[...] = (d_ixc * ig + lax.dot_general(dpab, wa_ref[...], NT, preferred_element_type=F32)
                        + lax.dot_general(dpib, wi_ref[...], NT, preferred_element_type=F32))
        dwa_ref[...] += lax.dot_general(xb, dpab, TN, preferred_element_type=F32)
        dwi_ref[...] += lax.dot_general(xb, dpib, TN, preferred_element_type=F32)

    blk = pl.BlockSpec((tm, cb), lambda n, i: (i, n))
    wsp = pl.BlockSpec((None, cb, cb), lambda n, i: (n, 0, 0))
    vec = pl.BlockSpec((1, cb), lambda n, i: (0, n))
    halo = pl.BlockSpec((HALO, cb), lambda n, i: (jnp.maximum(i * hb - 1, 0), n))
    return pl.pallas_call(
        body, grid=(nb, t // tm), in_specs=[blk, blk, blk, halo, wsp, wsp, vec, vec, vec],
        out_specs=[blk, wsp, wsp, vec, vec, vec],
        out_shape=[_sds((t, c), F32), _sds((nb, cb, cb), F32), _sds((nb, cb, cb), F32),
                   _sds((1, c), F32), _sds((1, c), F32), _sds((1, c), F32)],
        compiler_params=_params("parallel", "arbitrary"), name=name)(xc, gu, hs, hs, wa, wi, ba, bi, lam)


SCAN_TS = 256
SCAN_TC = 512


def _tile_scan(a, b, reverse):
    ts = a.shape[0]
    row = lax.broadcasted_iota(jnp.int32, a.shape, 0)
    d = 1
    while d < ts:
        if reverse:
            inside = row < ts - d
            a_sh = jnp.where(inside, pltpu.roll(a, ts - d, 0), 1.0)
            b_sh = jnp.where(inside, pltpu.roll(b, ts - d, 0), 0.0)
        else:
            inside = row >= d
            a_sh = jnp.where(inside, pltpu.roll(a, d, 0), 1.0)
            b_sh = jnp.where(inside, pltpu.roll(b, d, 0), 0.0)
        b = b + a * b_sh
        a = a * a_sh
        d *= 2
    return a, b


def rg_scan_fwd(name, a, u, gate_pre):
    t, c = a.shape
    ts, tc = _row_tile(t, SCAN_TS), _row_tile(c, SCAN_TC)

    def body(a_ref, u_ref, g_ref, h_ref, z_ref, carry_ref):
        s = pl.program_id(1)

        @pl.when(s == 0)
        def _():
            carry_ref[...] = jnp.zeros_like(carry_ref)

        ac, bc = _tile_scan(a_ref[...], u_ref[...], False)
        h = bc + ac * carry_ref[0:1, :]
        h_ref[...] = h
        z_ref[...] = (h * _gelu(g_ref[...])).astype(BF16)
        carry_ref[0:1, :] = h[ts - 1:ts, :]

    blk = pl.BlockSpec((ts, tc), lambda j, s: (s, j))
    return pl.pallas_call(
        body, grid=(c // tc, t // ts), in_specs=[blk, blk, blk], out_specs=[blk, blk],
        out_shape=[_sds((t, c), F32), _sds((t, c), BF16)], scratch_shapes=[pltpu.VMEM((8, tc), F32)],
        compiler_params=_params("parallel", "arbitrary"), name=name)(a, u, gate_pre)


def rg_scan_bwd(name, a, hs, gate_pre, dz):
    t, c = hs.shape
    ts, tc = _row_tile(t, SCAN_TS), _row_tile(c, SCAN_TC)
    nt = t // ts
    hb = ts // HALO

    def body(a_ref, halo_ref, h_ref, g_ref, dz_ref, gu_ref, dgate_ref, carry_ref):
        s = pl.program_id(1)

        @pl.when(s == 0)
        def _():
            carry_ref[...] = jnp.zeros_like(carry_ref)

        a_next = _shifted(jnp.concatenate([a_ref[...], jnp.where(s == 0, 0.0, halo_ref[...])], axis=0), 1, ts, False)
        gate = g_ref[...]
        dzv = dz_ref[...]
        dgate_ref[...] = (dzv * h_ref[...] * _gelu_grad(gate)).astype(BF16)
        ac, bc = _tile_scan(a_next, dzv * _gelu(gate), True)
        gu = bc + ac * carry_ref[0:1, :]
        gu_ref[...] = gu
        carry_ref[0:1, :] = gu[0:1, :]

    blk = pl.BlockSpec((ts, tc), lambda j, s: (nt - 1 - s, j))
    halo = pl.BlockSpec((HALO, tc), lambda j, s: (jnp.minimum((nt - s) * hb, t // HALO - 1), j))
    return pl.pallas_call(
        body, grid=(c // tc, nt), in_specs=[blk, halo, blk, blk, blk], out_specs=[blk, blk],
        out_shape=[_sds((t, c), F32), _sds((t, c), BF16)], scratch_shapes=[pltpu.VMEM((8, tc), F32)],
        compiler_params=_params("parallel", "arbitrary"), name=name)(a, a, hs, gate_pre, dz)


QA_BLK, KA_BLK, VA_BLK, QS_BLK, KS_BLK, VS_BLK = (g * N_PAIRS for g in range(6))


TOEP_W = 640
TOEP_FLAT = 320
TABLE_LOW = 193


def rel_bias_matrix(name, table):
    h = table.shape[0]
    diag = jnp.concatenate([jnp.repeat(table[:, 2 * REL_CLIP:], TOEP_FLAT, axis=1),
                            jnp.flip(table[:, TABLE_LOW:2 * REL_CLIP], axis=1),
                            jnp.zeros((h, 1), table.dtype)], axis=1)[:, None, :]

    def body(v_ref, o_ref):
        rows = jnp.broadcast_to(v_ref[...], (CHUNK, TOEP_W))
        o_ref[...] = pltpu.roll(rows, TOEP_W - (CHUNK - 1), 1, stride=1, stride_axis=0)

    out = pl.pallas_call(
        body, grid=(h,), in_specs=[pl.BlockSpec((None, 1, TOEP_W), lambda hh: (hh, 0, 0))],
        out_specs=pl.BlockSpec((None, CHUNK, TOEP_W), lambda hh: (hh, 0, 0)),
        out_shape=_sds((h, CHUNK, TOEP_W), F32), compiler_params=_params("parallel"), name=name)(diag)
    return out[:, :, :BAND]


def rel_bias_grad(name, dbias):
    h = dbias.shape[0]
    flipped = jnp.pad(jnp.flip(dbias, axis=1), ((0, 0), (0, 0), (0, TOEP_W - BAND)))

    def body(x_ref, o_ref):
        skew = pltpu.roll(x_ref[...], 0, 1, stride=1, stride_axis=0)
        col = jnp.sum(skew, axis=0, keepdims=True)
        lane = lax.broadcasted_iota(jnp.int32, col.shape, 1)
        flat = jnp.sum(jnp.where(lane < TOEP_FLAT, col, 0.0), axis=1, keepdims=True)
        o_ref[...] = jnp.where(lane == TOEP_W - 1, flat, col)

    out = pl.pallas_call(
        body, grid=(h,), in_specs=[pl.BlockSpec((None, CHUNK, TOEP_W), lambda hh: (hh, 0, 0))],
        out_specs=pl.BlockSpec((None, 1, TOEP_W), lambda hh: (hh, 0, 0)),
        out_shape=_sds((h, 1, TOEP_W), F32), compiler_params=_params("parallel"), name=name)(flipped)[:, 0, :]
    return jnp.concatenate([jnp.zeros((h, TABLE_LOW), F32), jnp.flip(out[:, TOEP_FLAT:TOEP_W - 1], axis=1),
                            out[:, TOEP_W - 1:]], axis=1)


def attn_layer_fwd(tag, x, h, w, g_next):
    proj = mm_nn_wblk(tag + "_proj", h, w["w_in"], w["idx"], BF16)
    width = N_PAIRS * PAIR
    pad = lambda a: jnp.pad(a, ((PAD_KEYS, 0), (0, 0)))
    kap, vap = pad(proj[:, width:2 * width]), pad(proj[:, 2 * width:3 * width])
    bias = rel_bias_matrix(tag + "_bias", w["rel_bias"])
    oa, lse = attn_a_fwd(tag + "_a", proj, kap, vap, bias, QA_BLK)
    ob = sb_fwd(tag + "_sb", proj, QS_BLK, KS_BLK, VS_BLK)
    o = jnp.concatenate([oa, ob], axis=1).astype(BF16)
    m = mm_nn(tag + "_out", o, w["w_out"], F32)
    x1, h_next = resid_norm_fwd(tag + "_res", x, m, w["g_post"], g_next)
    return x1, h_next, (x, h, proj, kap, vap, bias, oa, lse, ob, o, m)


def attn_layer_bwd(tag, dm, dx1, saved, w, prev):
    x, h, proj, kap, vap, bias, oa, lse, ob, o, m = saved
    d_w_out = mm_tn(tag + "_dwout", o, dm, F32)
    do = mm_nt(tag + "_do", dm, w["w_out"], BF16)
    dqa, dkap, dvap, dbias = attn_a_bwd(tag + "_da", proj, kap, vap, bias, oa, lse, do, QA_BLK, 0)
    dqs, dks, dvs = sb_bwd(tag + "_dsb", proj, ob, do, QS_BLK, KS_BLK, VS_BLK, N_PAIRS)
    d_rel = rel_bias_grad(tag + "_dbias", dbias)
    dproj = jnp.concatenate([dqa, dkap[PAD_KEYS:], dvap[PAD_KEYS:], dqs, dks, dvs], axis=1).astype(BF16)
    d_w_in = mm_tn_oblk(tag + "_dwin", h, dproj, w["w_in"].shape[3], F32)
    dh = mm_nt_wblk(tag + "_dh", dproj, w["w_in"], w["idx"], F32)
    dx, dg_pre, dm_prev, dg_post_prev = close_bwd(tag, dh, x, w["g_pre"], dx1, prev)
    return dx, dm_prev, dg_post_prev, dict(w_in=d_w_in, w_out=d_w_out, rel_bias=d_rel, g_pre=dg_pre)


def rg_layer_fwd(tag, x, h, w, g_next):
    proj = mm_nn_wblk(tag + "_proj", h, w["w_in"], w["idx"], F32)
    xc = conv4_fwd(tag + "_conv", proj, 1, w["conv_w"], w["conv_b"])
    a, u = rg_gates_fwd(tag + "_gates", xc, w["w_a"], w["w_i"], w["b_a"], w["b_i"], w["lam"])
    hs, z = rg_scan_fwd(tag + "_scan", a, u, proj)
    m = mm_nn(tag + "_out", z, w["w_out"], F32)
    x1, h_next = resid_norm_fwd(tag + "_res", x, m, w["g_post"], g_next)
    return x1, h_next, (x, h, proj, xc, a, hs, z, m)


def rg_layer_bwd(tag, dm, dx1, saved, w, prev):
    x, h, proj, xc, a, hs, z, m = saved
    d_w_out = mm_tn(tag + "_dwout", z, dm, F32)
    dz = mm_nt(tag + "_dz", dm, w["w_out"], F32)
    gu, dgate = rg_scan_bwd(tag + "_dscan", a, hs, proj, dz)
    dxc, d_w_a, d_w_i, d_b_a, d_b_i, d_lam = rg_gates_bwd(
        tag + "_dgates", xc, gu, hs, w["w_a"], w["w_i"], w["b_a"], w["b_i"], w["lam"])
    d_conv_w, d_conv_b = conv4_bwd_w(tag + "_dconvw", proj, 1, dxc)
    dxr = conv4_bwd_x(tag + "_dconv", dxc, w["conv_w"])
    dproj = jnp.concatenate([dgate, dxr.astype(BF16)], axis=1)
    d_w_in = mm_tn_oblk(tag + "_dwin", h, dproj, w["w_in"].shape[3], F32)
    dh = mm_nt_wblk(tag + "_dh", dproj, w["w_in"], w["idx"], F32)
    dx, dg_pre, dm_prev, dg_post_prev = close_bwd(tag, dh, x, w["g_pre"], dx1, prev)
    return dx, dm_prev, dg_post_prev, dict(w_in=d_w_in, w_out=d_w_out, conv_w=d_conv_w, conv_b=d_conv_b, w_a=d_w_a,
                                           w_i=d_w_i, b_a=d_b_a, b_i=d_b_i, lam=d_lam, g_pre=dg_pre)


def ffn_layer_fwd(tag, x, h, w, g_next):
    f8 = w["w_down"].shape[2]
    gu, a = ffn_up(tag + "_up", h, w["w_gu"], f8, w["idx"])
    f = ffn_down(tag + "_down", a, w["w_down"], f8, w["idx"])
    x1, h_next = resid_norm_fwd(tag + "_res", x, f, w["g_post"], g_next)
    return x1, h_next, (x, h, gu, a, f)


def ffn_layer_bwd(tag, dm, dx1, saved, w, prev):
    x, h, gu, a, f = saved
    f8 = w["w_down"].shape[2]
    d_w_down = ffn_dw_down(tag + "_dwdown", a, dm, f8)
    dgu = ffn_bwd_act(tag + "_dact", dm, w["w_down"], gu, f8, w["idx"])
    d_w_gu = ffn_dw_in(tag + "_dwgu", h, dgu)
    dh = ffn_bwd_dh(tag + "_dh", dgu, w["w_gu"], w["idx"])
    dx, dg_pre, dm_prev, dg_post_prev = close_bwd(tag, dh, x, w["g_pre"], dx1, prev)
    return dx, dm_prev, dg_post_prev, dict(w_gu=d_w_gu, w_down=d_w_down, g_pre=dg_pre)


def _place():
    return lax.axis_index("x"), lax.axis_index("y"), lax.axis_index("c")


def all_gather(name, blks):
    n = len(blks)

    def body(*refs):
        x_refs, out_refs = refs[:n], refs[n:2 * n]
        send_sems, recv_sems, local_sems = refs[2 * n:]
        x, y, cc = _place()
        me, sibling = (x, y, cc), (x, y, 1 - cc)
        chips = [(1 - x, y), (x, 1 - y), (1 - x, 1 - y)]
        south = cc == 0
        via = (jnp.where(south, 1 - x, x), jnp.where(south, y, 1 - y))
        onward = (jnp.where(south, x, 1 - x), jnp.where(south, 1 - y, y))
        k_via, k_onward = 1 + cc, 2 - cc

        def slot(a, px, py, pc):
            return out_refs[a].at[4 * px + 2 * py + pc]

        def copy(a, k, block, to, src=None):
            return pltpu.make_async_remote_copy(
                src_ref=slot(a, *block) if src is None else src, dst_ref=slot(a, *block),
                send_sem=send_sems.at[7 * a + k], recv_sem=recv_sems.at[7 * a + k], device_id=to, device_id_type=MESH)

        mine = [pltpu.make_async_copy(x_refs[a], slot(a, *me), local_sems.at[a]) for a in range(n)]
        sends = []
        for a in range(n):
            mine[a].start()
            sends.append(copy(a, 0, me, sibling, src=x_refs[a]))
            sends += [copy(a, 1 + j, me, (*chips[j], cc), src=x_refs[a]) for j in range(2)]
        for cp in sends:
            cp.start()
        for a in range(n):
            copy(a, k_via, (*via, cc), me).wait_recv()
            sends.append(copy(a, 3, (*via, cc), (*onward, cc)))
            sends.append(copy(a, 3 + k_via, (*via, cc), sibling))
            sends[-2].start()
            sends[-1].start()
        for a in range(n):
            copy(a, k_onward, (*onward, cc), me).wait_recv()
            sends.append(copy(a, 3 + k_onward, (*onward, cc), sibling))
            sends[-1].start()
        for a in range(n):
            copy(a, 3, (*chips[2], cc), me).wait_recv()
            sends.append(copy(a, 6, (*chips[2], cc), sibling))
            sends[-1].start()
        for a in range(n):
            copy(a, 0, sibling, me).wait_recv()
            for j, chip in enumerate(chips):
                copy(a, 4 + j, (*chip, 1 - cc), me).wait_recv()
        for cp in sends:
            cp.wait_send()
        for cp in mine:
            cp.wait()

    return pl.pallas_call(
        body, out_shape=[_sds((N_DEV,) + b.shape, b.dtype) for b in blks], in_specs=[ANY] * n, out_specs=[ANY] * n,
        scratch_shapes=[pltpu.SemaphoreType.DMA((7 * n,)), pltpu.SemaphoreType.DMA((7 * n,)),
                        pltpu.SemaphoreType.DMA((n,))],
        name=name)(*blks)


def exchange_pair(name, gs):
    n = len(gs)
    nchip = 4

    def body(*refs):
        g_refs, land_refs = refs[:n], refs[n:2 * n]
        send_sems, recv_sems = refs[2 * n:]
        x, y, cc = _place()
        copies = [pltpu.make_async_remote_copy(
            src_ref=g_refs[a].at[j, 1 - cc], dst_ref=land_refs[a].at[j], send_sem=send_sems.at[nchip * a + j],
            recv_sem=recv_sems.at[nchip * a + j], device_id=(x, y, 1 - cc), device_id_type=MESH)
            for a in range(n) for j in range(nchip)]
        for cp in copies:
            cp.start()
        for cp in copies:
            cp.wait()

    return pl.pallas_call(
        body, out_shape=[_sds((nchip,) + g.shape[2:], g.dtype) for g in gs], in_specs=[ANY] * n, out_specs=[ANY] * n,
        scratch_shapes=[pltpu.SemaphoreType.DMA((nchip * n,)), pltpu.SemaphoreType.DMA((nchip * n,))],
        name=name)(*gs)


def pair_sum(name, g, land, core, out_dtype):
    nchip, _, r, c = g.shape
    tr = _divisor_tile(r, 1024, 16)

    def body(core_ref, g_ref, l_ref, o_ref):
        o_ref[...] = (g_ref[...] + l_ref[...]).astype(o_ref.dtype)

    return pl.pallas_call(
        body,
        grid_spec=pltpu.PrefetchScalarGridSpec(
            num_scalar_prefetch=1, grid=(nchip, r // tr),
            in_specs=[pl.BlockSpec((None, None, tr, c), lambda j, i, core_ref: (j, core_ref[0], i, 0)),
                      pl.BlockSpec((None, tr, c), lambda j, i, core_ref: (j, i, 0))],
            out_specs=pl.BlockSpec((None, tr, c), lambda j, i, core_ref: (j, i, 0))),
        out_shape=_sds((nchip, r, c), out_dtype), compiler_params=_params("parallel", "parallel"), name=name,
    )(core, g, land)


def exchange_chips(name, ps):
    n = len(ps)

    def body(*refs):
        p_refs, land_refs = refs[:n], refs[n:2 * n]
        send_sems, recv_sems, local_sems = refs[2 * n:]
        x, y, cc = _place()
        mine = 2 * x + y
        chips = [(1 - x, y), (x, 1 - y), (1 - x, 1 - y)]
        own = [pltpu.make_async_copy(p_refs[a].at[mine], land_refs[a].at[mine], local_sems.at[a]) for a in range(n)]
        for cp in own:
            cp.start()
        sends = [pltpu.make_async_remote_copy(
            src_ref=p_refs[a].at[2 * px + py], dst_ref=land_refs[a].at[mine], send_sem=send_sems.at[3 * a + k],
            recv_sem=recv_sems.at[3 * a + k], device_id=(px, py, cc), device_id_type=MESH)
            for a in range(n) for k, (px, py) in enumerate(chips)]
        for cp in sends:
            cp.start()
        for a in range(n):
            for k, (px, py) in enumerate(chips):
                pltpu.make_async_remote_copy(
                    src_ref=p_refs[a].at[mine], dst_ref=land_refs[a].at[2 * px + py], send_sem=send_sems.at[3 * a + k],
                    recv_sem=recv_sems.at[3 * a + k], device_id=(px, py, cc), device_id_type=MESH).wait_recv()
        for cp in sends:
            cp.wait_send()
        for cp in own:
            cp.wait()

    return pl.pallas_call(
        body, out_shape=[_sds(p.shape, p.dtype) for p in ps], in_specs=[ANY] * n, out_specs=[ANY] * n,
        scratch_shapes=[pltpu.SemaphoreType.DMA((3 * n,)), pltpu.SemaphoreType.DMA((3 * n,)),
                        pltpu.SemaphoreType.DMA((n,))],
        name=name)(*ps)


def adamw(name, parts, w, m, v):
    npart, r, c = parts.shape
    tr = _divisor_tile(r, 512, 16)
    c1 = 1.0 / (1.0 - ADAM_B1 ** ADAM_STEP)
    c2 = 1.0 / (1.0 - ADAM_B2 ** ADAM_STEP)

    def body(p_ref, w_ref, m_ref, v_ref, g_ref, d_ref, nm_ref, nv_ref):
        g = p_ref[0].astype(F32)
        for j in range(1, npart):
            g = g + p_ref[j].astype(F32)
        nm = ADAM_B1 * m_ref[...] + (1.0 - ADAM_B1) * g
        nv = ADAM_B2 * v_ref[...] + (1.0 - ADAM_B2) * (g * g)
        g_ref[...] = g
        nm_ref[...] = nm
        nv_ref[...] = nv
        d_ref[...] = -ADAM_LR * ((nm * c1) / (jnp.sqrt(nv * c2) + ADAM_EPS) + ADAM_WD * w_ref[...])

    row = pl.BlockSpec((tr, c), lambda i: (i, 0))
    return pl.pallas_call(
        body, grid=(r // tr,), in_specs=[pl.BlockSpec((npart, tr, c), lambda i: (0, i, 0)), row, row, row],
        out_specs=[row] * 4, out_shape=[_sds((r, c), F32)] * 4, compiler_params=_params("parallel"), name=name,
    )(parts, w, m, v)


def _pack(arrays, dtype, row_multiple):
    flat = jnp.concatenate([a.astype(dtype).reshape(-1) for a in arrays])
    per = row_multiple * LANES
    total = -(-flat.shape[0] // per) * per
    return jnp.pad(flat, (0, total - flat.shape[0])).reshape(total // LANES, LANES)


def _pack_blocked(arrays, dtype, row_multiple):
    flat = jnp.concatenate([a.astype(dtype).reshape(N_DEV, -1) for a in arrays], axis=1)
    per = row_multiple * LANES
    total = -(-flat.shape[1] // per) * per
    return jnp.pad(flat, ((0, 0), (0, total - flat.shape[1]))).reshape(N_DEV, total // LANES, LANES)


def _unpack(buf, shapes, lead=()):
    flat = buf.reshape(lead + (-1,))
    out, off = [], 0
    for s in shapes:
        n = math.prod(s)
        out.append(flat[..., off:off + n].reshape(lead + tuple(s)))
        off += n
    return out


def _to_blocked(full, ax):
    s = full.shape
    return jnp.moveaxis(full.reshape(s[:ax] + (N_DEV, s[ax] // N_DEV) + s[ax + 1:]), ax, 0)


def _from_blocked(blk, ax):
    moved = jnp.moveaxis(blk, 0, ax)
    s = moved.shape
    return moved.reshape(s[:ax] + (s[ax] * s[ax + 1],) + s[ax + 2:])


SMALL = ("rg_conv_w", "rg_conv_b", "rg_b_a", "rg_b_i", "rg_lambda")
GU = "ffn_w_gu"
BIG = ("attn_w_in", "attn_w_out", "rg_w_in", "rg_w_a", "rg_w_i", "rg_w_out", GU, "ffn_w_down")


def kernel(x, attn_w_in, attn_rel_bias, attn_w_out, rg_w_in, rg_conv_w, rg_conv_b, rg_w_a, rg_b_a, rg_w_i, rg_b_i, rg_lambda, rg_w_out, norm_mix_pre, norm_mix_post, norm_ffn_pre, norm_ffn_post, ffn_w_gate, ffn_w_up, ffn_w_down, loss_target, m_attn_w_in, m_attn_rel_bias, m_attn_w_out, m_rg_w_in, m_rg_conv_w, m_rg_conv_b, m_rg_w_a, m_rg_b_a, m_rg_w_i, m_rg_b_i, m_rg_lambda, m_rg_w_out, m_norm_mix_pre, m_norm_mix_post, m_norm_ffn_pre, m_norm_ffn_post, m_ffn_w_gate, m_ffn_w_up, m_ffn_w_down, v_attn_w_in, v_attn_rel_bias, v_attn_w_out, v_rg_w_in, v_rg_conv_w, v_rg_conv_b, v_rg_w_a, v_rg_b_a, v_rg_w_i, v_rg_b_i, v_rg_lambda, v_rg_w_out, v_norm_mix_pre, v_norm_mix_post, v_norm_ffn_pre, v_norm_ffn_post, v_ffn_w_gate, v_ffn_w_up, v_ffn_w_down):
    w_loc = dict(attn_w_in=attn_w_in, attn_rel_bias=attn_rel_bias, attn_w_out=attn_w_out, rg_w_in=rg_w_in,
                 rg_conv_w=rg_conv_w, rg_conv_b=rg_conv_b, rg_w_a=rg_w_a, rg_b_a=rg_b_a, rg_w_i=rg_w_i, rg_b_i=rg_b_i,
                 rg_lambda=rg_lambda, rg_w_out=rg_w_out, norm_mix_pre=norm_mix_pre, norm_mix_post=norm_mix_post,
                 norm_ffn_pre=norm_ffn_pre, norm_ffn_post=norm_ffn_post, ffn_w_gate=ffn_w_gate, ffn_w_up=ffn_w_up,
                 ffn_w_down=ffn_w_down)
    m_loc = dict(attn_w_in=m_attn_w_in, attn_rel_bias=m_attn_rel_bias, attn_w_out=m_attn_w_out, rg_w_in=m_rg_w_in,
                 rg_conv_w=m_rg_conv_w, rg_conv_b=m_rg_conv_b, rg_w_a=m_rg_w_a, rg_b_a=m_rg_b_a, rg_w_i=m_rg_w_i,
                 rg_b_i=m_rg_b_i, rg_lambda=m_rg_lambda, rg_w_out=m_rg_w_out, norm_mix_pre=m_norm_mix_pre,
                 norm_mix_post=m_norm_mix_post, norm_ffn_pre=m_norm_ffn_pre, norm_ffn_post=m_norm_ffn_post,
                 ffn_w_gate=m_ffn_w_gate, ffn_w_up=m_ffn_w_up, ffn_w_down=m_ffn_w_down)
    v_loc = dict(attn_w_in=v_attn_w_in, attn_rel_bias=v_attn_rel_bias, attn_w_out=v_attn_w_out, rg_w_in=v_rg_w_in,
                 rg_conv_w=v_rg_conv_w, rg_conv_b=v_rg_conv_b, rg_w_a=v_rg_w_a, rg_b_a=v_rg_b_a, rg_w_i=v_rg_w_i,
                 rg_b_i=v_rg_b_i, rg_lambda=v_rg_lambda, rg_w_out=v_rg_w_out, norm_mix_pre=v_norm_mix_pre,
                 norm_mix_post=v_norm_mix_post, norm_ffn_pre=v_norm_ffn_pre, norm_ffn_post=v_norm_ffn_post,
                 ffn_w_gate=v_ffn_w_gate, ffn_w_up=v_ffn_w_up, ffn_w_down=v_ffn_w_down)
    axis_of = dict(SHARDED)
    xt, target = x[0], loss_target[0]
    d_model = xt.shape[1]
    rows2d = lambda a: a.reshape(-1, a.shape[-1])
    small_shapes = [w_loc[n].shape for n in SMALL]
    f8 = ffn_w_gate.shape[-1]
    for d in (w_loc, m_loc, v_loc):
        d[GU] = merge_gu(d["ffn_w_gate"], d["ffn_w_up"])

    gathered = all_gather("gather_weights", [rows2d(w_loc[n]).astype(BF16) for n in BIG]
                          + [_pack([w_loc[n] for n in SMALL], F32, 8)])
    blocked = {n: g.reshape((N_DEV,) + w_loc[n].shape) for n, g in zip(BIG, gathered)}
    blocked.update(zip(SMALL, _unpack(gathered[-1], small_shapes, (N_DEV,))))
    full = {n: _from_blocked(blocked[n], axis_of[n]) for n in SMALL}
    row = lambda a: a.reshape(1, -1).astype(F32)
    square = lambda rows8: rows8.reshape(-1, rows8.shape[-1])
    gates = lambda g: jnp.swapaxes(g, 0, 1).reshape(LRU_BLOCKS, -1, g.shape[-1])

    def layer_weights(layer):
        j = layer // 2
        norms = dict(g_pre=row(norm_mix_pre[layer]), g_post=row(norm_mix_post[layer]), idx=j)
        if layer % 2 == 0:
            mix = dict(w_in=blocked["attn_w_in"], w_out=square(blocked["attn_w_out"][:, j]),
                       rel_bias=attn_rel_bias[j], **norms)
        else:
            mix = dict(w_in=blocked["rg_w_in"], w_out=square(blocked["rg_w_out"][:, j]),
                       conv_w=full["rg_conv_w"][j][:, 0, :], conv_b=row(full["rg_conv_b"][j]),
                       w_a=gates(blocked["rg_w_a"][:, j]), w_i=gates(blocked["rg_w_i"][:, j]),
                       b_a=row(full["rg_b_a"][j]), b_i=row(full["rg_b_i"][j]), lam=row(full["rg_lambda"][j]), **norms)
        ffn = dict(w_gu=blocked[GU], w_down=blocked["ffn_w_down"], idx=layer,
                   g_pre=row(norm_ffn_pre[layer]), g_post=row(norm_ffn_post[layer]))
        return mix, ffn

    weights = [layer_weights(layer) for layer in range(DEPTH)]
    act, tape = xt, []
    h = rmsnorm_fwd("l0_mix_norm", act, weights[0][0]["g_pre"])
    for layer in range(DEPTH):
        mix_w, ffn_w = weights[layer]
        mixer_fwd = attn_layer_fwd if layer % 2 == 0 else rg_layer_fwd
        act, h, saved_mix = mixer_fwd(f"l{layer}_mix", act, h, mix_w, ffn_w["g_pre"])
        g_next = weights[layer + 1][0]["g_pre"] if layer + 1 < DEPTH else None
        act, h, saved_ffn = ffn_layer_fwd(f"l{layer}_ffn", act, h, ffn_w, g_next)
        tape.append((mix_w, ffn_w, saved_mix, saved_ffn))
    dact, sq = loss_grad("loss", act, target)
    loss_part = (0.5 * jnp.sum(sq) / d_model).reshape(1)

    grads = {}
    last = tape[DEPTH - 1]
    dm, grads[("ffn_post", DEPTH - 1)] = norm_bwd(f"l{DEPTH - 1}_ffn_dpost", dact, last[3][-1], last[1]["g_post"],
                                                 None, BF16)
    for layer in reversed(range(DEPTH)):
        mix_w, ffn_w, saved_mix, saved_ffn = tape[layer]
        dact, dm, grads[("mix_post", layer)], grads[("ffn", layer)] = ffn_layer_bwd(
            f"l{layer}_ffn", dm, dact, saved_ffn, ffn_w, (saved_mix[-1], mix_w["g_post"]))
        mixer_bwd = attn_layer_bwd if layer % 2 == 0 else rg_layer_bwd
        prev = (tape[layer - 1][3][-1], tape[layer - 1][1]["g_post"]) if layer else None
        dact, dm, grads[("ffn_post", layer - 1)], grads[("mix", layer)] = mixer_bwd(
            f"l{layer}_mix", dm, dact, saved_mix, mix_w, prev)
    attn_g = [grads[("mix", l)] for l in range(0, DEPTH, 2)]
    rg_g = [grads[("mix", l)] for l in range(1, DEPTH, 2)]
    ffn_g = [grads[("ffn", l)] for l in range(DEPTH)]
    stack = lambda gs, key: jnp.stack([g[key] for g in gs])
    by_owner = lambda gs, key, f: jnp.stack([f(g[key]) for g in gs], axis=1)
    rows8 = lambda a: a.reshape(N_DEV, -1, a.shape[-1])
    ungates = lambda a: jnp.swapaxes(a.reshape(LRU_BLOCKS, N_DEV, -1, a.shape[-1]), 0, 1)
    same = lambda a: a
    blocked_g = dict(
        attn_w_in=by_owner(attn_g, "w_in", same), attn_w_out=by_owner(attn_g, "w_out", rows8),
        rg_w_in=by_owner(rg_g, "w_in", same), rg_w_out=by_owner(rg_g, "w_out", rows8),
        rg_w_a=by_owner(rg_g, "w_a", ungates), rg_w_i=by_owner(rg_g, "w_i", ungates),
        **{GU: by_owner(ffn_g, "w_gu", same)},
        ffn_w_down=by_owner(ffn_g, "w_down", same))
    contrib = dict(
        attn_rel_bias=stack(attn_g, "rel_bias"), rg_conv_w=stack(rg_g, "conv_w")[:, :, None, :],
        rg_conv_b=stack(rg_g, "conv_b")[:, 0], rg_b_a=stack(rg_g, "b_a").reshape(rg_b_a.shape[0], LRU_BLOCKS, -1),
        rg_b_i=stack(rg_g, "b_i").reshape(rg_b_i.shape[0], LRU_BLOCKS, -1), rg_lambda=stack(rg_g, "lam")[:, 0],
        norm_mix_pre=jnp.concatenate([grads[("mix", l)]["g_pre"] for l in range(DEPTH)]),
        norm_mix_post=jnp.concatenate([grads[("mix_post", l)] for l in range(DEPTH)]),
        norm_ffn_pre=jnp.concatenate([g["g_pre"] for g in ffn_g]),
        norm_ffn_post=jnp.concatenate([grads[("ffn_post", l)] for l in range(DEPTH)]),
    )
    small_g = _pack_blocked([_to_blocked(contrib[n], axis_of[n]) for n in SMALL], F32, 8)

    slabs = [blocked_g[n].reshape(4, 2, -1, blocked_g[n].shape[-1]) for n in BIG] + [small_g.reshape(4, 2, -1, LANES)]
    core = lax.axis_index("c").astype(jnp.int32).reshape(1)
    from_sibling = exchange_pair("rs_pair", slabs)
    pairs = [pair_sum(f"rs_pair_sum_{i}", g, l, core, BF16 if i < len(BIG) else F32)
             for i, (g, l) in enumerate(zip(slabs, from_sibling))]
    by_chip = exchange_chips("rs_chips", pairs)
    result = {}
    kinds = ("grad", "delta", "new_m", "new_v")
    for n, parts in zip(BIG, by_chip):
        outs = adamw("adamw_" + n, parts, *[rows2d(d[n]) for d in (w_loc, m_loc, v_loc)])
        for kind, a in zip(kinds, outs):
            result[(kind, n)] = a.reshape(w_loc[n].shape)
    for kind in kinds:
        result[(kind, "ffn_w_gate")], result[(kind, "ffn_w_up")] = split_gu(result.pop((kind, GU)), f8)
    outs = adamw("adamw_small", by_chip[-1], *[_pack([d[n] for n in SMALL], F32, 8) for d in (w_loc, m_loc, v_loc)])
    for kind, buf in zip(kinds, outs):
        result.update({(kind, n): a for n, a in zip(SMALL, _unpack(buf, small_shapes))})
    rep_shapes = [w_loc[n].shape for n in REPLICATED] + [(1,)]
    rep_parts, = all_gather("gather_rep_grads", [_pack([contrib[n] for n in REPLICATED] + [loss_part], F32, 8)])
    outs = adamw("adamw_replicated", rep_parts, *[_pack([d[n] for n in REPLICATED] + [jnp.zeros((1,), F32)], F32, 8)
                                                  for d in (w_loc, m_loc, v_loc)])
    for kind, buf in zip(kinds, outs):
        result.update({(kind, n): a for n, a in zip(REPLICATED + ("loss",), _unpack(buf, rep_shapes))})
    loss = result[("grad", "loss")][0]
    return (loss, dact[None], *[result[(kind, n)] for kind in kinds for n in WEIGHTS])
```

```python
import math

import jax
import jax.numpy as jnp
from jax import lax
from jax.experimental import pallas as pl
from jax.experimental.pallas import tpu as pltpu

F32 = jnp.float32
BF16 = jnp.bfloat16

N_DEV = 8
DEPTH = 4
CHUNK = 64
N_LEFT = 8
BAND = (N_LEFT + 1) * CHUNK
PAD_KEYS = N_LEFT * CHUNK
HEAD_DIM = 64
N_HEADS = 8
REL_CLIP = 256
LRU_BLOCKS = 4
LRU_C = 8.0
RMS_EPS = 1e-6
QK_SCALE = HEAD_DIM ** -0.5

ADAM_LR = 0.001
ADAM_B1 = 0.9
ADAM_B2 = 0.999
ADAM_EPS = 1e-08
ADAM_WD = 0.01
ADAM_STEP = 10

LANES = 1024
V7X_VMEM_LIMIT = 56 * 1024 * 1024

MESH = pl.DeviceIdType.MESH
ANY = pl.BlockSpec(memory_space=pl.ANY)

SHARDED = (
    ("attn_w_in", 2), ("attn_w_out", 1), ("rg_w_in", 2), ("rg_conv_w", 3), ("rg_conv_b", 1),
    ("rg_w_a", 2), ("rg_b_a", 2), ("rg_w_i", 2), ("rg_b_i", 2), ("rg_lambda", 1), ("rg_w_out", 1),
    ("ffn_w_gate", 2), ("ffn_w_up", 2), ("ffn_w_down", 1),
)
REPLICATED = ("attn_rel_bias", "norm_mix_pre", "norm_mix_post", "norm_ffn_pre", "norm_ffn_post")
WEIGHTS = ("attn_w_in", "attn_rel_bias", "attn_w_out", "rg_w_in", "rg_conv_w", "rg_conv_b", "rg_w_a", "rg_b_a",
           "rg_w_i", "rg_b_i", "rg_lambda", "rg_w_out", "norm_mix_pre", "norm_mix_post", "norm_ffn_pre",
           "norm_ffn_post", "ffn_w_gate", "ffn_w_up", "ffn_w_down")


def _params(*dims):
    return pltpu.CompilerParams(dimension_semantics=dims or None, vmem_limit_bytes=V7X_VMEM_LIMIT)


def _sds(shape, dtype):
    return jax.ShapeDtypeStruct(tuple(shape), dtype)


def _row_tile(n, pref):
    t = min(n, pref)
    assert n % t == 0, (n, pref)
    return t


def _divisor_tile(n, limit, multiple):
    if n <= limit:
        return n
    best = max(t for t in range(multiple, limit + 1, multiple) if n % t == 0)
    return best


NN = (((1,), (0,)), ((), ()))
NT = (((1,), (1,)), ((), ()))
TN = (((0,), (0,)), ((), ()))


def _gmm(name, a, b, *, grid, a_blk, a_idx, b_blk, b_idx, o_blk, o_idx, out_shape, out_dtype, dn, acc_shape):
    nk = grid[-1]
    kax = len(grid) - 1

    def body(a_ref, b_ref, o_ref, acc_ref):
        part = lax.dot_general(a_ref[...], b_ref[...], dn, preferred_element_type=F32)
        if nk == 1:
            o_ref[...] = part.astype(o_ref.dtype)
            return
        k = pl.program_id(kax)

        @pl.when(k == 0)
        def _():
            acc_ref[...] = part

        @pl.when(k > 0)
        def _():
            acc_ref[...] += part

        @pl.when(k == nk - 1)
        def _():
            o_ref[...] = acc_ref[...].astype(o_ref.dtype)

    return pl.pallas_call(
        body, grid=grid,
        in_specs=[pl.BlockSpec(a_blk, a_idx), pl.BlockSpec(b_blk, b_idx)],
        out_specs=pl.BlockSpec(o_blk, o_idx),
        out_shape=_sds(out_shape, out_dtype),
        scratch_shapes=[pltpu.VMEM(acc_shape, F32)],
        compiler_params=_params(*(["parallel"] * kax + ["arbitrary"])),
        name=name,
    )(a, b)


def mm_nn(name, a, b, out_dtype, tm=1024, tn=512, tk=1024):
    (m, k), (_, n) = a.shape, b.shape
    tm, tn, tk = _row_tile(m, tm), _row_tile(n, tn), _row_tile(k, tk)
    return _gmm(name, a, b, grid=(m // tm, n // tn, k // tk),
                a_blk=(tm, tk), a_idx=lambda i, j, kk: (i, kk), b_blk=(tk, tn), b_idx=lambda i, j, kk: (kk, j),
                o_blk=(tm, tn), o_idx=lambda i, j, kk: (i, j), out_shape=(m, n), out_dtype=out_dtype, dn=NN,
                acc_shape=(tm, tn))


def mm_nt(name, a, b, out_dtype, tm=1024, tn=512, tk=1024):
    (m, k), (n, _) = a.shape, b.shape
    tm, tn, tk = _row_tile(m, tm), _row_tile(n, tn), _row_tile(k, tk)
    return _gmm(name, a, b, grid=(m // tm, n // tn, k // tk),
                a_blk=(tm, tk), a_idx=lambda i, j, kk: (i, kk), b_blk=(tn, tk), b_idx=lambda i, j, kk: (j, kk),
                o_blk=(tm, tn), o_idx=lambda i, j, kk: (i, j), out_shape=(m, n), out_dtype=out_dtype, dn=NT,
                acc_shape=(tm, tn))


def mm_tn(name, a, b, out_dtype, tm=1024, tn=512, tk=4096):
    (k, m), (_, n) = a.shape, b.shape
    tm, tn, tk = _row_tile(m, tm), _row_tile(n, tn), _row_tile(k, tk)
    return _gmm(name, a, b, grid=(m // tm, n // tn, k // tk),
                a_blk=(tk, tm), a_idx=lambda i, j, kk: (kk, i), b_blk=(tk, tn), b_idx=lambda i, j, kk: (kk, j),
                o_blk=(tm, tn), o_idx=lambda i, j, kk: (i, j), out_shape=(m, n), out_dtype=out_dtype, dn=TN,
                acc_shape=(tm, tn))


MXU_WIDTH = 256


def _blocks_per_step(n8):
    return 1 if n8 % MXU_WIDTH == 0 else 2


def _side_by_side(w_ref, j0, per):
    return w_ref[j0] if per == 1 else jnp.concatenate([w_ref[j0 + j] for j in range(per)], axis=1)


def mm_nn_wblk(name, a, wb, layer, out_dtype, tm=1024):
    (m, k), (nb, _, _, n8) = a.shape, wb.shape
    tm = _row_tile(m, tm)
    per = _blocks_per_step(n8)

    def body(a_ref, w_ref, o_ref):
        o_ref[...] = jnp.dot(a_ref[...], _side_by_side(w_ref, 0, per), preferred_element_type=F32).astype(o_ref.dtype)

    return pl.pallas_call(
        body, grid=(m // tm, nb // per),
        in_specs=[pl.BlockSpec((tm, k), lambda i, j: (i, 0)),
                  pl.BlockSpec((per, None, k, n8), lambda i, j: (j, layer, 0, 0))],
        out_specs=pl.BlockSpec((tm, per * n8), lambda i, j: (i, j)), out_shape=_sds((m, nb * n8), out_dtype),
        compiler_params=_params("parallel", "parallel"), name=name)(a, wb)


def mm_nt_wblk(name, a, wb, layer, out_dtype, tm=1024, tn=512):
    m = a.shape[0]
    nb, _, k, n8 = wb.shape
    tm, tn = _row_tile(m, tm), _row_tile(k, tn)
    per = _blocks_per_step(n8)

    def body(a_ref, b_ref, o_ref):
        acc = None
        for j in range(0, nb, per):
            part = lax.dot_general(a_ref[:, j * n8:(j + per) * n8], _side_by_side(b_ref, j, per), NT,
                                   preferred_element_type=F32)
            acc = part if acc is None else acc + part
        o_ref[...] = acc.astype(o_ref.dtype)

    return pl.pallas_call(
        body, grid=(m // tm, k // tn),
        in_specs=[pl.BlockSpec((tm, nb * n8), lambda i, j: (i, 0)),
                  pl.BlockSpec((nb, None, tn, n8), lambda i, j: (0, layer, j, 0))],
        out_specs=pl.BlockSpec((tm, tn), lambda i, j: (i, j)), out_shape=_sds((m, k), out_dtype),
        compiler_params=_params("parallel", "parallel"), name=name)(a, wb)


def mm_tn_oblk(name, a, b, n8, out_dtype, tk=4096):
    (t, k), nb = a.shape, b.shape[1] // n8
    tk = _row_tile(t, tk)
    per = _blocks_per_step(n8)
    nk = t // tk

    def body(a_ref, b_ref, o_ref, acc_ref):
        s = pl.program_id(1)
        part = lax.dot_general(a_ref[...], b_ref[...], TN, preferred_element_type=F32)

        @pl.when(s == 0)
        def _():
            acc_ref[...] = part

        @pl.when(s > 0)
        def _():
            acc_ref[...] += part

        @pl.when(s == nk - 1)
        def _():
            for j in range(per):
                o_ref[j] = acc_ref[:, j * n8:(j + 1) * n8].astype(o_ref.dtype)

    return pl.pallas_call(
        body, grid=(nb // per, nk),
        in_specs=[pl.BlockSpec((tk, k), lambda j, s: (s, 0)), pl.BlockSpec((tk, per * n8), lambda j, s: (s, j))],
        out_specs=pl.BlockSpec((per, k, n8), lambda j, s: (j, 0, 0)), out_shape=_sds((nb, k, n8), out_dtype),
        scratch_shapes=[pltpu.VMEM((k, per * n8), F32)],
        compiler_params=_params("parallel", "arbitrary"), name=name)(a, b)


def rmsnorm_fwd(name, x, g):
    t, d = x.shape
    tr = _row_tile(t, 512)

    def body(x_ref, g_ref, o_ref):
        xv = x_ref[...]
        r = lax.rsqrt(jnp.mean(xv * xv, axis=-1, keepdims=True) + RMS_EPS)
        o_ref[...] = (xv * r * g_ref[...]).astype(o_ref.dtype)

    return pl.pallas_call(
        body, grid=(t // tr,),
        in_specs=[pl.BlockSpec((tr, d), lambda i: (i, 0)), pl.BlockSpec((1, d), lambda i: (0, 0))],
        out_specs=pl.BlockSpec((tr, d), lambda i: (i, 0)),
        out_shape=_sds((t, d), BF16), compiler_params=_params("parallel"), name=name)(x, g)


def resid_norm_fwd(name, x, m, g, g_next):
    t, d = x.shape
    tr = _row_tile(t, 512)
    chained = g_next is not None

    def body(*refs):
        x_ref, m_ref, g_ref = refs[:3]
        mv = m_ref[...]
        r = lax.rsqrt(jnp.mean(mv * mv, axis=-1, keepdims=True) + RMS_EPS)
        x1 = x_ref[...] + mv * r * g_ref[...]
        if chained:
            gn_ref, o_ref, h_ref = refs[3:]
            r1 = lax.rsqrt(jnp.mean(x1 * x1, axis=-1, keepdims=True) + RMS_EPS)
            h_ref[...] = (x1 * r1 * gn_ref[...]).astype(BF16)
        else:
            o_ref, = refs[3:]
        o_ref[...] = x1

    row = pl.BlockSpec((tr, d), lambda i: (i, 0))
    vec = pl.BlockSpec((1, d), lambda i: (0, 0))
    out = pl.pallas_call(
        body, grid=(t // tr,),
        in_specs=[row, row, vec] + ([vec] if chained else []),
        out_specs=[row, row] if chained else [row],
        out_shape=[_sds((t, d), F32)] + ([_sds((t, d), BF16)] if chained else []),
        compiler_params=_params("parallel"), name=name)(*([x, m, g] + ([g_next] if chained else [])))
    return (out[0], out[1]) if chained else (out[0], None)


def norm_bwd(name, dy, x, g, resid, out_dtype):
    t, d = x.shape
    tr = _row_tile(t, 512)
    has_res = resid is not None

    def body(*refs):
        if has_res:
            dy_ref, x_ref, g_ref, r_ref, dx_ref, dg_ref = refs
        else:
            dy_ref, x_ref, g_ref, dx_ref, dg_ref = refs
        i = pl.program_id(0)
        xv = x_ref[...]
        dyv = dy_ref[...].astype(F32)
        r = lax.rsqrt(jnp.mean(xv * xv, axis=-1, keepdims=True) + RMS_EPS)
        xh = xv * r
        dxh = dyv * g_ref[...]
        dx = r * (dxh - xh * jnp.mean(dxh * xh, axis=-1, keepdims=True))
        if has_res:
            dx = dx + r_ref[...]
        dx_ref[...] = dx.astype(dx_ref.dtype)
        part = jnp.sum(dyv * xh, axis=0, keepdims=True)

        @pl.when(i == 0)
        def _():
            dg_ref[...] = part

        @pl.when(i > 0)
        def _():
            dg_ref[...] += part

    row = pl.BlockSpec((tr, d), lambda i: (i, 0))
    vec = pl.BlockSpec((1, d), lambda i: (0, 0))
    ins = [dy, x, g] + ([resid] if has_res else [])
    return pl.pallas_call(
        body, grid=(t // tr,),
        in_specs=[row, row, vec] + ([row] if has_res else []),
        out_specs=[row, vec],
        out_shape=[_sds((t, d), out_dtype), _sds((1, d), F32)],
        compiler_params=_params("arbitrary"), name=name)(*ins)


def norm_bwd_chain(name, dh, x, g, resid, m_prev, g_prev):
    t, d = x.shape
    tr = _row_tile(t, 512)

    def body(dh_ref, x_ref, g_ref, r_ref, m_ref, gp_ref, dx_ref, dg_ref, dm_ref, dgp_ref):
        i = pl.program_id(0)
        xv = x_ref[...]
        dhv = dh_ref[...]
        r = lax.rsqrt(jnp.mean(xv * xv, axis=-1, keepdims=True) + RMS_EPS)
        xh = xv * r
        dxh = dhv * g_ref[...]
        dx = r * (dxh - xh * jnp.mean(dxh * xh, axis=-1, keepdims=True)) + r_ref[...]
        dx_ref[...] = dx
        mv = m_ref[...]
        rm = lax.rsqrt(jnp.mean(mv * mv, axis=-1, keepdims=True) + RMS_EPS)
        mh = mv * rm
        dmh = dx * gp_ref[...]
        dm_ref[...] = (rm * (dmh - mh * jnp.mean(dmh * mh, axis=-1, keepdims=True))).astype(BF16)
        part = jnp.sum(dhv * xh, axis=0, keepdims=True)
        part_prev = jnp.sum(dx * mh, axis=0, keepdims=True)

        @pl.when(i == 0)
        def _():
            dg_ref[...] = part
            dgp_ref[...] = part_prev

        @pl.when(i > 0)
        def _():
            dg_ref[...] += part
            dgp_ref[...] += part_prev

    row = pl.BlockSpec((tr, d), lambda i: (i, 0))
    vec = pl.BlockSpec((1, d), lambda i: (0, 0))
    return pl.pallas_call(
        body, grid=(t // tr,), in_specs=[row, row, vec, row, row, vec], out_specs=[row, vec, row, vec],
        out_shape=[_sds((t, d), F32), _sds((1, d), F32), _sds((t, d), BF16), _sds((1, d), F32)],
        compiler_params=_params("arbitrary"), name=name)(dh, x, g, resid, m_prev, g_prev)


def close_bwd(tag, dh, x, g_pre, dx1, prev):
    if prev is None:
        dx, dg_pre = norm_bwd(tag + "_dpre", dh, x, g_pre, dx1, F32)
        return dx, dg_pre, None, None
    return norm_bwd_chain(tag + "_dpre", dh, x, g_pre, dx1, *prev)


def loss_grad(name, y, target):
    t, d = y.shape
    tr = _row_tile(t, 512)

    def body(y_ref, t_ref, dy_ref, s_ref):
        i = pl.program_id(0)
        err = y_ref[...] - t_ref[...]
        dy_ref[...] = err * (1.0 / d)
        part = jnp.sum(err * err, axis=0, keepdims=True)

        @pl.when(i == 0)
        def _():
            s_ref[...] = part

        @pl.when(i > 0)
        def _():
            s_ref[...] += part

    row = pl.BlockSpec((tr, d), lambda i: (i, 0))
    vec = pl.BlockSpec((1, d), lambda i: (0, 0))
    return pl.pallas_call(
        body, grid=(t // tr,), in_specs=[row, row], out_specs=[row, vec],
        out_shape=[_sds((t, d), F32), _sds((1, d), F32)],
        compiler_params=_params("arbitrary"), name=name)(y, target)


PAIR = 2 * HEAD_DIM
N_PAIRS = N_HEADS // 2
A_TQ = 512
A_UNROLL_FWD = 8
A_UNROLL_BWD = 4


def _halves(x):
    lane = lax.broadcasted_iota(jnp.int32, x.shape, x.ndim - 1)
    zero = jnp.zeros_like(x)
    return jnp.where(lane < HEAD_DIM, x, zero), jnp.where(lane >= HEAD_DIM, x, zero)


def _merge(a, b):
    lane = lax.broadcasted_iota(jnp.int32, a.shape, a.ndim - 1)
    return jnp.where(lane < HEAD_DIM, a, b)


def _a_valid(c):
    col = lax.broadcasted_iota(jnp.int32, (CHUNK, BAND), 1)
    return col >= (N_LEFT - c) * CHUNK


def attn_a_fwd(name, proj, kp, vp, bias, q_blk):
    t = proj.shape[0]
    tq = _row_tile(t, A_TQ)
    ncs = tq // CHUNK
    un = math.gcd(A_UNROLL_FWD, ncs)

    def body(q_ref, k_ref, v_ref, b_ref, o_ref, l_ref):
        i = pl.program_id(1)

        def group(gg, carry):
            cs = [i * ncs + gg * un + u for u in range(un)]
            r0s = [pl.multiple_of((gg * un + u) * CHUNK, CHUNK) for u in range(un)]
            k0s = [pl.multiple_of(c * CHUNK, CHUNK) for c in cs]
            ss = []
            for u in range(un):
                qh = _halves(q_ref[pl.ds(r0s[u], CHUNK), :] * QK_SCALE)
                kwin = k_ref[pl.ds(k0s[u], BAND), :]
                valid = _a_valid(cs[u])
                for hh in range(2):
                    s = lax.dot_general(qh[hh], kwin, NT, preferred_element_type=F32) + b_ref[hh]
                    ss.append(jnp.where(valid, s, -1e30))
            ps, lses = [], []
            for s in ss:
                mx = jnp.max(s, axis=-1, keepdims=True)
                p = jnp.exp(s - mx)
                den = jnp.sum(p, axis=-1, keepdims=True)
                ps.append((p * (1.0 / den)).astype(BF16))
                lses.append(mx + jnp.log(den))
            for u in range(un):
                vwin = v_ref[pl.ds(k0s[u], BAND), :]
                o0 = jnp.dot(ps[2 * u], vwin, preferred_element_type=F32)
                o1 = jnp.dot(ps[2 * u + 1], vwin, preferred_element_type=F32)
                o_ref[pl.ds(r0s[u], CHUNK), :] = _merge(o0, o1)
                l_ref[pl.ds(r0s[u], CHUNK), :] = jnp.concatenate([lses[2 * u], lses[2 * u + 1]], axis=1)
            return carry

        lax.fori_loop(0, ncs // un, group, 0)

    return pl.pallas_call(
        body, grid=(N_PAIRS, t // tq),
        in_specs=[pl.BlockSpec((tq, PAIR), lambda p, i: (i, q_blk + p)),
                  pl.BlockSpec((t + PAD_KEYS, PAIR), lambda p, i: (0, p)),
                  pl.BlockSpec((t + PAD_KEYS, PAIR), lambda p, i: (0, p)),
                  pl.BlockSpec((2, CHUNK, BAND), lambda p, i: (p, 0, 0))],
        out_specs=[pl.BlockSpec((tq, PAIR), lambda p, i: (i, p)),
                   pl.BlockSpec((None, tq, 2), lambda p, i: (p, i, 0))],
        out_shape=[_sds((t, N_PAIRS * PAIR), F32), _sds((N_PAIRS, t, 2), F32)],
        compiler_params=_params("parallel", "parallel"), name=name)(proj, kp, vp, bias)


def attn_a_bwd(name, proj, kp, vp, bias, o, lse, do, q_blk, do_blk):
    t = proj.shape[0]
    tq = _row_tile(t, A_TQ)
    ncs = tq // CHUNK
    un = math.gcd(A_UNROLL_BWD, ncs)

    def body(q_ref, k_ref, v_ref, b_ref, o_ref, l_ref, do_ref, dq_ref, dk_ref, dv_ref, db_ref):
        i = pl.program_id(1)

        @pl.when(i == 0)
        def _():
            dk_ref[...] = jnp.zeros_like(dk_ref)
            dv_ref[...] = jnp.zeros_like(dv_ref)
            db_ref[...] = jnp.zeros_like(db_ref)

        def group(gg, carry):
            cs = [i * ncs + gg * un + u for u in range(un)]
            r0s = [pl.multiple_of((gg * un + u) * CHUNK, CHUNK) for u in range(un)]
            k0s = [pl.multiple_of(c * CHUNK, CHUNK) for c in cs]
            qhs, dohs, ps, dps, deltas = [], [], [], [], []
            for u in range(un):
                rows = pl.ds(r0s[u], CHUNK)
                qh = _halves(q_ref[rows, :] * QK_SCALE)
                doh = _halves(do_ref[rows, :])
                kwin = k_ref[pl.ds(k0s[u], BAND), :]
                vwin = v_ref[pl.ds(k0s[u], BAND), :]
                valid = _a_valid(cs[u])
                dl = _halves(do_ref[rows, :].astype(F32) * o_ref[rows, :])
                for hh in range(2):
                    s = lax.dot_general(qh[hh], kwin, NT, preferred_element_type=F32) + b_ref[hh]
                    ps.append(jnp.where(valid, jnp.exp(s - l_ref[rows, hh:hh + 1]), 0.0))
                    dps.append(lax.dot_general(doh[hh], vwin, NT, preferred_element_type=F32))
                    deltas.append(jnp.sum(dl[hh], axis=-1, keepdims=True))
                qhs.append(qh)
                dohs.append(doh)
            dss = [p * (dp - dl) for p, dp, dl in zip(ps, dps, deltas)]
            for hh in range(2):
                tot = dss[hh]
                for u in range(1, un):
                    tot = tot + dss[2 * u + hh]
                db_ref[hh] += tot
            for u in range(un):
                kwin = k_ref[pl.ds(k0s[u], BAND), :]
                ds0, ds1 = dss[2 * u].astype(BF16), dss[2 * u + 1].astype(BF16)
                dq_ref[pl.ds(r0s[u], CHUNK), :] = _merge(jnp.dot(ds0, kwin, preferred_element_type=F32),
                                                         jnp.dot(ds1, kwin, preferred_element_type=F32)) * QK_SCALE
                dk_ref[pl.ds(k0s[u], BAND), :] += (lax.dot_general(ds0, qhs[u][0], TN, preferred_element_type=F32)
                                                   + lax.dot_general(ds1, qhs[u][1], TN, preferred_element_type=F32))
                dv_ref[pl.ds(k0s[u], BAND), :] += (
                    lax.dot_general(ps[2 * u].astype(BF16), dohs[u][0], TN, preferred_element_type=F32)
                    + lax.dot_general(ps[2 * u + 1].astype(BF16), dohs[u][1], TN, preferred_element_type=F32))
            return carry

        lax.fori_loop(0, ncs // un, group, 0)

    tile = lambda blk: pl.BlockSpec((tq, PAIR), lambda p, i: (i, blk + p))
    whole = pl.BlockSpec((t + PAD_KEYS, PAIR), lambda p, i: (0, p))
    bspec = pl.BlockSpec((2, CHUNK, BAND), lambda p, i: (p, 0, 0))
    return pl.pallas_call(
        body, grid=(N_PAIRS, t // tq),
        in_specs=[tile(q_blk), whole, whole, bspec, tile(0), pl.BlockSpec((None, tq, 2), lambda p, i: (p, i, 0)),
                  tile(do_blk)],
        out_specs=[tile(0), whole, whole, bspec],
        out_shape=[_sds((t, N_PAIRS * PAIR), F32), _sds((t + PAD_KEYS, N_PAIRS * PAIR), F32),
                   _sds((t + PAD_KEYS, N_PAIRS * PAIR), F32), _sds((2 * N_PAIRS, CHUNK, BAND), F32)],
        compiler_params=_params("parallel", "arbitrary"), name=name)(proj, kp, vp, bias, o, lse, do)


SB_TQ = 256
SB_TK = 256
SB_DEAD = -125.0


def _tri(n, strict):
    j = lax.broadcasted_iota(jnp.int32, (n, n), 0)
    s = lax.broadcasted_iota(jnp.int32, (n, n), 1)
    return jnp.where((j > s) if strict else (j >= s), 1.0, 0.0).astype(BF16)


def _suffix_sum(x, tri, exact):
    hi = x.astype(BF16)
    out = jnp.dot(hi, tri, preferred_element_type=F32)
    if exact:
        lo = (x - hi.astype(F32)).astype(BF16)
        out = out + jnp.dot(lo, tri, preferred_element_type=F32)
    return out


def _sb_scores(qh, ks, causal):
    z = lax.dot_general(qh, ks, NT, preferred_element_type=F32)
    lb = jnp.minimum(z, 0.0) - jnp.log(1.0 + jnp.exp(-jnp.abs(z)))
    m = lb - z
    if causal is not None:
        m = jnp.where(causal, m, 0.0)
    return lb, m


def _causal(tq, tk, off):
    return (lax.broadcasted_iota(jnp.int32, (tq, tk), 1) + off * tk) < lax.broadcasted_iota(jnp.int32, (tq, tk), 0)


def sb_fwd(name, proj, q_blk, k_blk, v_blk):
    t = proj.shape[0]
    tq = _row_tile(t, SB_TQ)
    tk = min(SB_TK, tq)
    per = tq // tk

    def body(q_ref, k_ref, v_ref, o_ref):
        i = pl.program_id(1)
        tri = _tri(tk, True)
        qh = _halves(q_ref[...] * QK_SCALE)

        def blocks(kb, carry, off):
            k0 = pl.multiple_of(kb * tk, tk)
            ks, vs = k_ref[pl.ds(k0, tk), :], v_ref[pl.ds(k0, tk), :]
            causal = None if off is None else _causal(tq, tk, off)
            lbm = [_sb_scores(qh[hh], ks, causal) for hh in range(2)]
            afters = [_suffix_sum(lbm[hh][1], tri, False) for hh in range(2)]
            out = []
            for hh in range(2):
                acc, cm = carry[2 * hh], carry[2 * hh + 1]
                w = jnp.exp(lbm[hh][0] + afters[hh] + cm)
                if causal is not None:
                    w = jnp.where(causal, w, 0.0)
                out += [acc + jnp.dot(w.astype(BF16), vs, preferred_element_type=F32),
                        cm + jnp.sum(lbm[hh][1], axis=-1, keepdims=True)]
            return tuple(out)

        def alive(carry):
            return jnp.maximum(jnp.max(carry[1]), jnp.max(carry[3])) > SB_DEAD

        carry = (jnp.zeros((tq, PAIR), F32), jnp.zeros((tq, 1), F32)) * 2
        for off in reversed(range(per)):
            carry = blocks(i * per + off, carry, off)

        def step(c):
            new = blocks(i * per - 1 - c[0], c[2:], None)
            return (c[0] + 1, alive(new)) + new

        out = lax.while_loop(lambda c: jnp.logical_and(c[0] < i * per, c[1]), step,
                             (jnp.int32(0), alive(carry)) + carry)
        o_ref[...] = _merge(out[2], out[4])

    return pl.pallas_call(
        body, grid=(N_PAIRS, t // tq),
        in_specs=[pl.BlockSpec((tq, PAIR), lambda p, i: (i, q_blk + p)),
                  pl.BlockSpec((t, PAIR), lambda p, i: (0, k_blk + p)),
                  pl.BlockSpec((t, PAIR), lambda p, i: (0, v_blk + p))],
        out_specs=pl.BlockSpec((tq, PAIR), lambda p, i: (i, p)),
        out_shape=_sds((t, N_PAIRS * PAIR), F32), compiler_params=_params("parallel", "parallel"), name=name,
    )(proj, proj, proj)


def sb_bwd(name, proj, o, do, q_blk, k_blk, v_blk, do_blk):
    t = proj.shape[0]
    tq = _row_tile(t, SB_TQ)
    tk = min(SB_TK, tq)
    per = tq // tk

    def body(q_ref, k_ref, v_ref, o_ref, do_ref, dq_ref, dk_ref, dv_ref):
        i = pl.program_id(1)

        @pl.when(i == 0)
        def _():
            dk_ref[...] = jnp.zeros_like(dk_ref)
            dv_ref[...] = jnp.zeros_like(dv_ref)

        tri_s, tri_i = _tri(tk, True), _tri(tk, False)
        qh = _halves(q_ref[...] * QK_SCALE)
        doh = _halves(do_ref[...])
        deltas = [jnp.sum(x, axis=-1, keepdims=True) for x in _halves(do_ref[...].astype(F32) * o_ref[...])]

        def blocks(kb, carry, off):
            k0 = pl.multiple_of(kb * tk, tk)
            ks, vs = k_ref[pl.ds(k0, tk), :], v_ref[pl.ds(k0, tk), :]
            causal = None if off is None else _causal(tq, tk, off)
            lbm = [_sb_scores(qh[hh], ks, causal) for hh in range(2)]
            dws = [lax.dot_general(doh[hh], vs, NT, preferred_element_type=F32) for hh in range(2)]
            afters = [_suffix_sum(lbm[hh][1], tri_s, False) for hh in range(2)]
            wbs, es = [], []
            for hh in range(2):
                w = jnp.exp(lbm[hh][0] + afters[hh] + carry[3 * hh + 1])
                if causal is not None:
                    w = jnp.where(causal, w, 0.0)
                wbs.append(w.astype(BF16))
                es.append(wbs[hh].astype(F32) * dws[hh])
            sfx = [_suffix_sum(es[hh], tri_i, True) for hh in range(2)]
            dzs = []
            for hh in range(2):
                left = deltas[hh] - (sfx[hh] + carry[3 * hh + 2])
                sig = jnp.exp(lbm[hh][0])
                dz = es[hh] * (1.0 - sig) - left * sig
                if causal is not None:
                    dz = jnp.where(causal, dz, 0.0)
                dzs.append(dz.astype(BF16))
            dk_ref[pl.ds(k0, tk), :] += (lax.dot_general(dzs[0], qh[0], TN, preferred_element_type=F32)
                                         + lax.dot_general(dzs[1], qh[1], TN, preferred_element_type=F32))
            dv_ref[pl.ds(k0, tk), :] += (lax.dot_general(wbs[0], doh[0], TN, preferred_element_type=F32)
                                         + lax.dot_general(wbs[1], doh[1], TN, preferred_element_type=F32))
            out = []
            for hh in range(2):
                out += [carry[3 * hh] + jnp.dot(dzs[hh], ks, preferred_element_type=F32),
                        carry[3 * hh + 1] + jnp.sum(lbm[hh][1], axis=-1, keepdims=True),
                        carry[3 * hh + 2] + jnp.sum(es[hh], axis=-1, keepdims=True)]
            return tuple(out)

        def alive(carry):
            return jnp.maximum(jnp.max(carry[1]), jnp.max(carry[4])) > SB_DEAD

        zero = jnp.zeros((tq, 1), F32)
        carry = (jnp.zeros((tq, PAIR), F32), zero, zero) * 2
        for off in reversed(range(per)):
            carry = blocks(i * per + off, carry, off)

        def step(c):
            new = blocks(i * per - 1 - c[0], c[2:], None)
            return (c[0] + 1, alive(new)) + new

        out = lax.while_loop(lambda c: jnp.logical_and(c[0] < i * per, c[1]), step,
                             (jnp.int32(0), alive(carry)) + carry)
        dq_ref[...] = _merge(out[2], out[5]) * QK_SCALE

    tile = lambda blk: pl.BlockSpec((tq, PAIR), lambda p, i: (i, blk + p))
    whole = lambda blk: pl.BlockSpec((t, PAIR), lambda p, i: (0, blk + p))
    return pl.pallas_call(
        body, grid=(N_PAIRS, t // tq),
        in_specs=[tile(q_blk), whole(k_blk), whole(v_blk), tile(0), tile(do_blk)],
        out_specs=[tile(0), whole(0), whole(0)],
        out_shape=[_sds((t, N_PAIRS * PAIR), F32)] * 3,
        compiler_params=_params("parallel", "arbitrary"), name=name)(proj, proj, proj, o, do)


def _sigmoid(x):
    return 1.0 / (1.0 + jnp.exp(-x))


def gu_gap(f8):
    return -(-f8 // 128) * 128


def merge_gu(gate, up):
    f8 = gate.shape[-1]
    pad = jnp.zeros(gate.shape[:-1] + (gu_gap(f8) - f8,), gate.dtype)
    return jnp.concatenate([gate, pad, up], axis=-1)


def split_gu(gu, f8):
    return gu[..., :f8], gu[..., gu_gap(f8):]


PER = 2


def _stacked_pair(w_ref, p, f8):
    zeros = jnp.zeros((gu_gap(f8) - f8, w_ref.shape[-1]), w_ref.dtype)
    return jnp.concatenate([w_ref[PER * p], zeros, w_ref[PER * p + 1]], axis=0)


def ffn_up(name, h, wgu, f8, layer):
    t, d = h.shape
    nb, _, _, fw = wgu.shape
    gap = gu_gap(f8)
    tm = _row_tile(t, 1024)

    def body(h_ref, w_ref, gu_ref, a_ref):
        hv = h_ref[...]
        rs = [jnp.dot(hv, w_ref[j], preferred_element_type=F32) for j in range(PER)]
        a_ref[:, f8:gap] = jnp.zeros((tm, gap - f8), BF16)
        for j in range(PER):
            gu_ref[j] = rs[j].astype(BF16)
            g, u = rs[j][:, :f8], rs[j][:, gap:]
            a_ref[:, j * gap:j * gap + f8] = (g * _sigmoid(g) * u).astype(BF16)

    return pl.pallas_call(
        body, grid=(t // tm, nb // PER),
        in_specs=[pl.BlockSpec((tm, d), lambda i, k: (i, 0)),
                  pl.BlockSpec((PER, None, d, fw), lambda i, k: (k, layer, 0, 0))],
        out_specs=[pl.BlockSpec((PER, tm, fw), lambda i, k: (k, i, 0)),
                   pl.BlockSpec((None, tm, fw), lambda i, k: (k, i, 0))],
        out_shape=[_sds((nb, t, fw), BF16), _sds((nb // PER, t, fw), BF16)],
        compiler_params=_params("parallel", "parallel"), name=name)(h, wgu)


def _mm_all_blocks(name, a, w, layer, dn, tm):
    nb, t, f = a.shape
    wshape = w.shape[2:]
    d = wshape[1] if dn == NN else wshape[0]
    tm = _row_tile(t, tm)

    def body(a_ref, w_ref, o_ref):
        acc = lax.dot_general(a_ref[0], w_ref[0], dn, preferred_element_type=F32)
        for k in range(1, nb):
            acc = acc + lax.dot_general(a_ref[k], w_ref[k], dn, preferred_element_type=F32)
        o_ref[...] = acc

    return pl.pallas_call(
        body, grid=(t // tm,),
        in_specs=[pl.BlockSpec((nb, tm, f), lambda i: (0, i, 0)),
                  pl.BlockSpec((nb, None) + wshape, lambda i: (0, layer, 0, 0))],
        out_specs=pl.BlockSpec((tm, d), lambda i: (i, 0)), out_shape=_sds((t, d), F32),
        compiler_params=_params("parallel"), name=name)(a, w)


def ffn_down(name, a, wd, f8, layer):
    npair, t, fw = a.shape
    nb, _, _, d = wd.shape
    tm = _row_tile(t, 512)

    def body(a_ref, w_ref, o_ref):
        acc = jnp.dot(a_ref[0], _stacked_pair(w_ref, 0, f8), preferred_element_type=F32)
        for p in range(1, npair):
            acc = acc + jnp.dot(a_ref[p], _stacked_pair(w_ref, p, f8), preferred_element_type=F32)
        o_ref[...] = acc

    return pl.pallas_call(
        body, grid=(t // tm,),
        in_specs=[pl.BlockSpec((npair, tm, fw), lambda i: (0, i, 0)),
                  pl.BlockSpec((nb, None, f8, d), lambda i: (0, layer, 0, 0))],
        out_specs=pl.BlockSpec((tm, d), lambda i: (i, 0)), out_shape=_sds((t, d), F32),
        compiler_params=_params("parallel"), name=name)(a, wd)


def ffn_bwd_act(name, dm, wd, gu, f8, layer):
    t, d = dm.shape
    nb, _, fw = gu.shape
    gap = gu_gap(f8)
    tm = _row_tile(t, 1024)

    def body(dm_ref, wd_ref, gu_ref, o_ref):
        da = lax.dot_general(dm_ref[...], _stacked_pair(wd_ref, 0, f8), NT, preferred_element_type=F32)
        o_ref[:, :, f8:gap] = jnp.zeros((PER, tm, gap - f8), BF16)
        for j in range(PER):
            daj = da[:, j * gap:j * gap + f8]
            gv = gu_ref[j, :, :f8].astype(F32)
            uv = gu_ref[j, :, gap:].astype(F32)
            sg = _sigmoid(gv)
            o_ref[j, :, :f8] = (daj * uv * sg * (1.0 + gv * (1.0 - sg))).astype(BF16)
            o_ref[j, :, gap:] = (daj * gv * sg).astype(BF16)

    bspec = pl.BlockSpec((PER, tm, fw), lambda i, k: (k, i, 0))
    return pl.pallas_call(
        body, grid=(t // tm, nb // PER),
        in_specs=[pl.BlockSpec((tm, d), lambda i, k: (i, 0)),
                  pl.BlockSpec((PER, None, f8, d), lambda i, k: (k, layer, 0, 0)), bspec],
        out_specs=bspec, out_shape=_sds((nb, t, fw), BF16),
        compiler_params=_params("parallel", "parallel"), name=name)(dm, wd, gu)


def ffn_bwd_dh(name, dgu, wgu, layer):
    return _mm_all_blocks(name, dgu, wgu, layer, NT, 512)


def ffn_dw_in(name, h, dact):
    t, d = h.shape
    nb, _, f8 = dact.shape
    tk = _row_tile(t, 4096)
    return _gmm(name, h, dact, grid=(nb, t // tk),
                a_blk=(tk, d), a_idx=lambda b, s: (s, 0), b_blk=(None, tk, f8), b_idx=lambda b, s: (b, s, 0),
                o_blk=(None, d, f8), o_idx=lambda b, s: (b, 0, 0), out_shape=(nb, d, f8), out_dtype=F32, dn=TN,
                acc_shape=(d, f8))


def ffn_dw_down(name, a, dm, f8):
    npair, t, fw = a.shape
    d = dm.shape[1]
    gap = gu_gap(f8)
    tk = _row_tile(t, 4096)
    nk = t // tk

    def body(a_ref, dm_ref, o_ref, acc_ref):
        s = pl.program_id(1)
        part = lax.dot_general(a_ref[...], dm_ref[...], TN, preferred_element_type=F32)

        @pl.when(s == 0)
        def _():
            acc_ref[...] = part

        @pl.when(s > 0)
        def _():
            acc_ref[...] += part

        @pl.when(s == nk - 1)
        def _():
            for j in range(PER):
                o_ref[j] = acc_ref[j * gap:j * gap + f8, :]

    return pl.pallas_call(
        body, grid=(npair, nk),
        in_specs=[pl.BlockSpec((None, tk, fw), lambda p, s: (p, s, 0)), pl.BlockSpec((tk, d), lambda p, s: (s, 0))],
        out_specs=pl.BlockSpec((PER, f8, d), lambda p, s: (p, 0, 0)), out_shape=_sds((PER * npair, f8, d), F32),
        scratch_shapes=[pltpu.VMEM((fw, d), F32)],
        compiler_params=_params("parallel", "arbitrary"), name=name)(a, dm)


GELU_C = math.sqrt(2.0 / math.pi)
GELU_A = 0.044715


def _gelu(x):
    return 0.5 * x * (1.0 + jnp.tanh(GELU_C * (x + GELU_A * x * x * x)))


def _gelu_grad(x):
    th = jnp.tanh(GELU_C * (x + GELU_A * x * x * x))
    return 0.5 * (1.0 + th) + 0.5 * x * (1.0 - th * th) * GELU_C * (1.0 + 3.0 * GELU_A * x * x)


def _neg_expm1(x):
    series = x * (1.0 + x * (0.5 + x * (1.0 / 6.0 + x * (1.0 / 24.0 + x * (1.0 / 120.0 + x * (1.0 / 720.0))))))
    return -jnp.where(x > -0.25, series, jnp.exp(x) - 1.0)


CONV_TR = 256
CONV_TAPS = 4
HALO = 8


def _shifted(ext, k, tr, back):
    if back:
        return pltpu.roll(ext, k, 0)[HALO:, :] if k else ext[HALO:, :]
    return pltpu.roll(ext, tr + HALO - k, 0)[:tr, :] if k else ext[:tr, :]


def conv4_fwd(name, src, cb, w, b):
    t, c = src.shape[0], w.shape[1]
    tr = _row_tile(t, CONV_TR)
    hb = tr // HALO

    def body(x_ref, h_ref, w_ref, b_ref, o_ref):
        i = pl.program_id(0)
        ext = jnp.concatenate([jnp.where(i == 0, 0.0, h_ref[...]), x_ref[...]], axis=0)
        acc = b_ref[...]
        for k in range(CONV_TAPS):
            acc = acc + w_ref[CONV_TAPS - 1 - k:CONV_TAPS - k, :] * _shifted(ext, k, tr, True)
        o_ref[...] = acc

    return pl.pallas_call(
        body, grid=(t // tr,),
        in_specs=[pl.BlockSpec((tr, c), lambda i: (i, cb)),
                  pl.BlockSpec((HALO, c), lambda i: (jnp.maximum(i * hb - 1, 0), cb)),
                  pl.BlockSpec((CONV_TAPS, c), lambda i: (0, 0)), pl.BlockSpec((1, c), lambda i: (0, 0))],
        out_specs=pl.BlockSpec((tr, c), lambda i: (i, 0)), out_shape=_sds((t, c), F32),
        compiler_params=_params("parallel"), name=name)(src, src, w, b)


def conv4_bwd_x(name, dy, w):
    t, c = dy.shape
    tr = _row_tile(t, CONV_TR)
    hb = tr // HALO
    last = t // tr - 1

    def body(y_ref, h_ref, w_ref, o_ref):
        i = pl.program_id(0)
        ext = jnp.concatenate([y_ref[...], jnp.where(i == last, 0.0, h_ref[...])], axis=0)
        acc = w_ref[CONV_TAPS - 1:CONV_TAPS, :] * y_ref[...]
        for k in range(1, CONV_TAPS):
            acc = acc + w_ref[CONV_TAPS - 1 - k:CONV_TAPS - k, :] * _shifted(ext, k, tr, False)
        o_ref[...] = acc

    return pl.pallas_call(
        body, grid=(t // tr,),
        in_specs=[pl.BlockSpec((tr, c), lambda i: (i, 0)),
                  pl.BlockSpec((HALO, c), lambda i: (jnp.minimum((i + 1) * hb, t // HALO - 1), 0)),
                  pl.BlockSpec((CONV_TAPS, c), lambda i: (0, 0))],
        out_specs=pl.BlockSpec((tr, c), lambda i: (i, 0)), out_shape=_sds((t, c), F32),
        compiler_params=_params("parallel"), name=name)(dy, dy, w)


def conv4_bwd_w(name, src, cb, dy):
    t, c = dy.shape
    tr = _row_tile(t, CONV_TR)
    hb = tr // HALO

    def body(x_ref, h_ref, dy_ref, dw_ref, db_ref):
        i = pl.program_id(0)

        @pl.when(i == 0)
        def _():
            dw_ref[...] = jnp.zeros_like(dw_ref)
            db_ref[...] = jnp.zeros_like(db_ref)

        ext = jnp.concatenate([jnp.where(i == 0, 0.0, h_ref[...]), x_ref[...]], axis=0)
        dyv = dy_ref[...]
        db_ref[...] += jnp.sum(dyv, axis=0, keepdims=True)
        for k in range(CONV_TAPS):
            dw_ref[CONV_TAPS - 1 - k:CONV_TAPS - k, :] += jnp.sum(dyv * _shifted(ext, k, tr, True), axis=0,
                                                                   keepdims=True)

    return pl.pallas_call(
        body, grid=(t // tr,),
        in_specs=[pl.BlockSpec((tr, c), lambda i: (i, cb)),
                  pl.BlockSpec((HALO, c), lambda i: (jnp.maximum(i * hb - 1, 0), cb)),
                  pl.BlockSpec((tr, c), lambda i: (i, 0))],
        out_specs=[pl.BlockSpec((CONV_TAPS, c), lambda i: (0, 0)), pl.BlockSpec((1, c), lambda i: (0, 0))],
        out_shape=[_sds((CONV_TAPS, c), F32), _sds((1, c), F32)],
        compiler_params=_params("arbitrary"), name=name)(src, src, dy)


def _rg_gate_values(xcv, wa_ref, wi_ref, ba_ref, bi_ref, lam_ref):
    xb = xcv.astype(BF16)
    r = _sigmoid(jnp.dot(xb, wa_ref[...], preferred_element_type=F32) + ba_ref[...])
    ig = _sigmoid(jnp.dot(xb, wi_ref[...], preferred_element_type=F32) + bi_ref[...])
    lam = lam_ref[...]
    sp = jnp.maximum(-lam, 0.0) + jnp.log(1.0 + jnp.exp(-jnp.abs(lam)))
    log_a = -LRU_C * r * sp
    a = jnp.exp(log_a)
    mult = jnp.sqrt(_neg_expm1(2.0 * log_a))
    return xb, r, ig, sp, a, mult


def rg_gates_fwd(name, xc, wa, wi, ba, bi, lam):
    t, c = xc.shape
    nb, cb, _ = wa.shape
    tm = _row_tile(t, 512)

    def body(xc_ref, wa_ref, wi_ref, ba_ref, bi_ref, lam_ref, a_ref, u_ref):
        xcv = xc_ref[...]
        _, _, ig, _, a, mult = _rg_gate_values(xcv, wa_ref, wi_ref, ba_ref, bi_ref, lam_ref)
        a_ref[...] = a
        u_ref[...] = mult * (ig * xcv)

    blk = pl.BlockSpec((tm, cb), lambda i, n: (i, n))
    wsp = pl.BlockSpec((None, cb, cb), lambda i, n: (n, 0, 0))
    vec = pl.BlockSpec((1, cb), lambda i, n: (0, n))
    return pl.pallas_call(
        body, grid=(t // tm, nb), in_specs=[blk, wsp, wsp, vec, vec, vec], out_specs=[blk, blk],
        out_shape=[_sds((t, c), F32)] * 2, compiler_params=_params("parallel", "parallel"), name=name,
    )(xc, wa, wi, ba, bi, lam)


def rg_gates_bwd(name, xc, gu, hs, wa, wi, ba, bi, lam):
    t, c = xc.shape
    nb, cb, _ = wa.shape
    tm = _row_tile(t, 512)
    hb = tm // HALO

    def body(xc_ref, gu_ref, h_ref, halo_ref, wa_ref, wi_ref, ba_ref, bi_ref, lam_ref,
             dxc_ref, dwa_ref, dwi_ref, dba_ref, dbi_ref, dlam_ref):
        i = pl.program_id(1)
        hprev = _shifted(jnp.concatenate([jnp.where(i == 0, 0.0, halo_ref[...]), h_ref[...]], axis=0), 1, tm, True)

        @pl.when(i == 0)
        def _():
            for ref in (dwa_ref, dwi_ref, dba_ref, dbi_ref, dlam_ref):
                ref[...] = jnp.zeros_like(ref)

        xcv = xc_ref[...]
        xb, r, ig, sp, a, mult = _rg_gate_values(xcv, wa_ref, wi_ref, ba_ref, bi_ref, lam_ref)
        gv = gu_ref[...]
        d_ixc = gv * mult
        d_i = d_ixc * xcv
        d_mult = gv * ig * xcv
        d_a = gv * hprev - d_mult * a / mult
        d_log_a = d_a * a
        d_r = d_log_a * (-LRU_C * sp)
        sig_neg_lam = 1.0 / (1.0 + jnp.exp(lam_ref[...]))
        dlam_ref[...] += jnp.sum(d_log_a * r, axis=0, keepdims=True) * (LRU_C * sig_neg_lam)
        dpa = d_r * r * (1.0 - r)
        dpi = d_i * ig * (1.0 - ig)
        dba_ref[...] += jnp.sum(dpa, axis=0, keepdims=True)
        dbi_ref[...] += jnp.sum(dpi, axis=0, keepdims=True)
        dpab, dpib = dpa.astype(BF16), dpi.astype(BF16)
        dxc_ref[...] = (d_ixc * ig + lax.dot_general(dpab, wa_ref[...], NT, preferred_element_type=F32)
                        + lax.dot_general(dpib, wi_ref[...], NT, preferred_element_type=F32))
        dwa_ref[...] += lax.dot_general(xb, dpab, TN, preferred_element_type=F32)
        dwi_ref[...] += lax.dot_general(xb, dpib, TN, preferred_element_type=F32)

    blk = pl.BlockSpec((tm, cb), lambda n, i: (i, n))
    wsp = pl.BlockSpec((None, cb, cb), lambda n, i: (n, 0, 0))
    vec = pl.BlockSpec((1, cb), lambda n, i: (0, n))
    halo = pl.BlockSpec((HALO, cb), lambda n, i: (jnp.maximum(i * hb - 1, 0), n))
    return pl.pallas_call(
        body, grid=(nb, t // tm), in_specs=[blk, blk, blk, halo, wsp, wsp, vec, vec, vec],
        out_specs=[blk, wsp, wsp, vec, vec, vec],
        out_shape=[_sds((t, c), F32), _sds((nb, cb, cb), F32), _sds((nb, cb, cb), F32),
                   _sds((1, c), F32), _sds((1, c), F32), _sds((1, c), F32)],
        compiler_params=_params("parallel", "arbitrary"), name=name)(xc, gu, hs, hs, wa, wi, ba, bi, lam)


SCAN_TS = 256
SCAN_TC = 1024


def _tile_scan(a, b, reverse):
    ts = a.shape[0]
    row = lax.broadcasted_iota(jnp.int32, a.shape, 0)
    d = 1
    while d < ts:
        if reverse:
            inside = row < ts - d
            a_sh = jnp.where(inside, pltpu.roll(a, ts - d, 0), 1.0)
            b_sh = jnp.where(inside, pltpu.roll(b, ts - d, 0), 0.0)
        else:
            inside = row >= d
            a_sh = jnp.where(inside, pltpu.roll(a, d, 0), 1.0)
            b_sh = jnp.where(inside, pltpu.roll(b, d, 0), 0.0)
        b = b + a * b_sh
        a = a * a_sh
        d *= 2
    return a, b


def rg_scan_fwd(name, a, u, gate_pre):
    t, c = a.shape
    ts, tc = _row_tile(t, SCAN_TS), _row_tile(c, SCAN_TC)

    def body(a_ref, u_ref, g_ref, h_ref, z_ref, carry_ref):
        s = pl.program_id(1)

        @pl.when(s == 0)
        def _():
            carry_ref[...] = jnp.zeros_like(carry_ref)

        ac, bc = _tile_scan(a_ref[...], u_ref[...], False)
        h = bc + ac * carry_ref[0:1, :]
        h_ref[...] = h
        z_ref[...] = (h * _gelu(g_ref[...])).astype(BF16)
        carry_ref[0:1, :] = h[ts - 1:ts, :]

    blk = pl.BlockSpec((ts, tc), lambda j, s: (s, j))
    return pl.pallas_call(
        body, grid=(c // tc, t // ts), in_specs=[blk, blk, blk], out_specs=[blk, blk],
        out_shape=[_sds((t, c), F32), _sds((t, c), BF16)], scratch_shapes=[pltpu.VMEM((8, tc), F32)],
        compiler_params=_params("parallel", "arbitrary"), name=name)(a, u, gate_pre)


def rg_scan_bwd(name, a, hs, gate_pre, dz):
    t, c = hs.shape
    ts, tc = _row_tile(t, SCAN_TS), _row_tile(c, SCAN_TC)
    nt = t // ts
    hb = ts // HALO

    def body(a_ref, halo_ref, h_ref, g_ref, dz_ref, gu_ref, dgate_ref, carry_ref):
        s = pl.program_id(1)

        @pl.when(s == 0)
        def _():
            carry_ref[...] = jnp.zeros_like(carry_ref)

        a_next = _shifted(jnp.concatenate([a_ref[...], jnp.where(s == 0, 0.0, halo_ref[...])], axis=0), 1, ts, False)
        gate = g_ref[...]
        dzv = dz_ref[...]
        dgate_ref[...] = (dzv * h_ref[...] * _gelu_grad(gate)).astype(BF16)
        ac, bc = _tile_scan(a_next, dzv * _gelu(gate), True)
        gu = bc + ac * carry_ref[0:1, :]
        gu_ref[...] = gu
        carry_ref[0:1, :] = gu[0:1, :]

    blk = pl.BlockSpec((ts, tc), lambda j, s: (nt - 1 - s, j))
    halo = pl.BlockSpec((HALO, tc), lambda j, s: (jnp.minimum((nt - s) * hb, t // HALO - 1), j))
    return pl.pallas_call(
        body, grid=(c // tc, nt), in_specs=[blk, halo, blk, blk, blk], out_specs=[blk, blk],
        out_shape=[_sds((t, c), F32), _sds((t, c), BF16)], scratch_shapes=[pltpu.VMEM((8, tc), F32)],
        compiler_params=_params("parallel", "arbitrary"), name=name)(a, a, hs, gate_pre, dz)


QA_BLK, KA_BLK, VA_BLK, QS_BLK, KS_BLK, VS_BLK = (g * N_PAIRS for g in range(6))


TOEP_W = 640
TOEP_FLAT = 320
TABLE_LOW = 193


def rel_bias_matrix(name, table):
    h = table.shape[0]
    diag = jnp.concatenate([jnp.repeat(table[:, 2 * REL_CLIP:], TOEP_FLAT, axis=1),
                            jnp.flip(table[:, TABLE_LOW:2 * REL_CLIP], axis=1),
                            jnp.zeros((h, 1), table.dtype)], axis=1)[:, None, :]

    def body(v_ref, o_ref):
        rows = jnp.broadcast_to(v_ref[...], (CHUNK, TOEP_W))
        o_ref[...] = pltpu.roll(rows, TOEP_W - (CHUNK - 1), 1, stride=1, stride_axis=0)

    out = pl.pallas_call(
        body, grid=(h,), in_specs=[pl.BlockSpec((None, 1, TOEP_W), lambda hh: (hh, 0, 0))],
        out_specs=pl.BlockSpec((None, CHUNK, TOEP_W), lambda hh: (hh, 0, 0)),
        out_shape=_sds((h, CHUNK, TOEP_W), F32), compiler_params=_params("parallel"), name=name)(diag)
    return out[:, :, :BAND]


def rel_bias_grad(name, dbias):
    h = dbias.shape[0]
    flipped = jnp.pad(jnp.flip(dbias, axis=1), ((0, 0), (0, 0), (0, TOEP_W - BAND)))

    def body(x_ref, o_ref):
        skew = pltpu.roll(x_ref[...], 0, 1, stride=1, stride_axis=0)
        col = jnp.sum(skew, axis=0, keepdims=True)
        lane = lax.broadcasted_iota(jnp.int32, col.shape, 1)
        flat = jnp.sum(jnp.where(lane < TOEP_FLAT, col, 0.0), axis=1, keepdims=True)
        o_ref[...] = jnp.where(lane == TOEP_W - 1, flat, col)

    out = pl.pallas_call(
        body, grid=(h,), in_specs=[pl.BlockSpec((None, CHUNK, TOEP_W), lambda hh: (hh, 0, 0))],
        out_specs=pl.BlockSpec((None, 1, TOEP_W), lambda hh: (hh, 0, 0)),
        out_shape=_sds((h, 1, TOEP_W), F32), compiler_params=_params("parallel"), name=name)(flipped)[:, 0, :]
    return jnp.concatenate([jnp.zeros((h, TABLE_LOW), F32), jnp.flip(out[:, TOEP_FLAT:TOEP_W - 1], axis=1),
                            out[:, TOEP_W - 1:]], axis=1)


def attn_layer_fwd(tag, x, h, w, g_next):
    proj = mm_nn_wblk(tag + "_proj", h, w["w_in"], w["idx"], BF16)
    width = N_PAIRS * PAIR
    pad = lambda a: jnp.pad(a, ((PAD_KEYS, 0), (0, 0)))
    kap, vap = pad(proj[:, width:2 * width]), pad(proj[:, 2 * width:3 * width])
    bias = rel_bias_matrix(tag + "_bias", w["rel_bias"])
    oa, lse = attn_a_fwd(tag + "_a", proj, kap, vap, bias, QA_BLK)
    ob = sb_fwd(tag + "_sb", proj, QS_BLK, KS_BLK, VS_BLK)
    o = jnp.concatenate([oa, ob], axis=1).astype(BF16)
    m = mm_nn(tag + "_out", o, w["w_out"], F32)
    x1, h_next = resid_norm_fwd(tag + "_res", x, m, w["g_post"], g_next)
    return x1, h_next, (x, h, proj, kap, vap, bias, oa, lse, ob, o, m)


def attn_layer_bwd(tag, dm, dx1, saved, w, prev):
    x, h, proj, kap, vap, bias, oa, lse, ob, o, m = saved
    d_w_out = mm_tn(tag + "_dwout", o, dm, F32)
    do = mm_nt(tag + "_do", dm, w["w_out"], BF16)
    dqa, dkap, dvap, dbias = attn_a_bwd(tag + "_da", proj, kap, vap, bias, oa, lse, do, QA_BLK, 0)
    dqs, dks, dvs = sb_bwd(tag + "_dsb", proj, ob, do, QS_BLK, KS_BLK, VS_BLK, N_PAIRS)
    d_rel = rel_bias_grad(tag + "_dbias", dbias)
    dproj = jnp.concatenate([dqa, dkap[PAD_KEYS:], dvap[PAD_KEYS:], dqs, dks, dvs], axis=1).astype(BF16)
    d_w_in = mm_tn_oblk(tag + "_dwin", h, dproj, w["w_in"].shape[3], F32)
    dh = mm_nt_wblk(tag + "_dh", dproj, w["w_in"], w["idx"], F32)
    dx, dg_pre, dm_prev, dg_post_prev = close_bwd(tag, dh, x, w["g_pre"], dx1, prev)
    return dx, dm_prev, dg_post_prev, dict(w_in=d_w_in, w_out=d_w_out, rel_bias=d_rel, g_pre=dg_pre)


def rg_layer_fwd(tag, x, h, w, g_next):
    proj = mm_nn_wblk(tag + "_proj", h, w["w_in"], w["idx"], F32)
    xc = conv4_fwd(tag + "_conv", proj, 1, w["conv_w"], w["conv_b"])
    a, u = rg_gates_fwd(tag + "_gates", xc, w["w_a"], w["w_i"], w["b_a"], w["b_i"], w["lam"])
    hs, z = rg_scan_fwd(tag + "_scan", a, u, proj)
    m = mm_nn(tag + "_out", z, w["w_out"], F32)
    x1, h_next = resid_norm_fwd(tag + "_res", x, m, w["g_post"], g_next)
    return x1, h_next, (x, h, proj, xc, a, hs, z, m)


def rg_layer_bwd(tag, dm, dx1, saved, w, prev):
    x, h, proj, xc, a, hs, z, m = saved
    d_w_out = mm_tn(tag + "_dwout", z, dm, F32)
    dz = mm_nt(tag + "_dz", dm, w["w_out"], F32)
    gu, dgate = rg_scan_bwd(tag + "_dscan", a, hs, proj, dz)
    dxc, d_w_a, d_w_i, d_b_a, d_b_i, d_lam = rg_gates_bwd(
        tag + "_dgates", xc, gu, hs, w["w_a"], w["w_i"], w["b_a"], w["b_i"], w["lam"])
    d_conv_w, d_conv_b = conv4_bwd_w(tag + "_dconvw", proj, 1, dxc)
    dxr = conv4_bwd_x(tag + "_dconv", dxc, w["conv_w"])
    dproj = jnp.concatenate([dgate, dxr.astype(BF16)], axis=1)
    d_w_in = mm_tn_oblk(tag + "_dwin", h, dproj, w["w_in"].shape[3], F32)
    dh = mm_nt_wblk(tag + "_dh", dproj, w["w_in"], w["idx"], F32)
    dx, dg_pre, dm_prev, dg_post_prev = close_bwd(tag, dh, x, w["g_pre"], dx1, prev)
    return dx, dm_prev, dg_post_prev, dict(w_in=d_w_in, w_out=d_w_out, conv_w=d_conv_w, conv_b=d_conv_b, w_a=d_w_a,
                                           w_i=d_w_i, b_a=d_b_a, b_i=d_b_i, lam=d_lam, g_pre=dg_pre)


def ffn_layer_fwd(tag, x, h, w, g_next):
    f8 = w["w_down"].shape[2]
    gu, a = ffn_up(tag + "_up", h, w["w_gu"], f8, w["idx"])
    f = ffn_down(tag + "_down", a, w["w_down"], f8, w["idx"])
    x1, h_next = resid_norm_fwd(tag + "_res", x, f, w["g_post"], g_next)
    return x1, h_next, (x, h, gu, a, f)


def ffn_layer_bwd(tag, dm, dx1, saved, w, prev):
    x, h, gu, a, f = saved
    f8 = w["w_down"].shape[2]
    d_w_down = ffn_dw_down(tag + "_dwdown", a, dm, f8)
    dgu = ffn_bwd_act(tag + "_dact", dm, w["w_down"], gu, f8, w["idx"])
    d_w_gu = ffn_dw_in(tag + "_dwgu", h, dgu)
    dh = ffn_bwd_dh(tag + "_dh", dgu, w["w_gu"], w["idx"])
    dx, dg_pre, dm_prev, dg_post_prev = close_bwd(tag, dh, x, w["g_pre"], dx1, prev)
    return dx, dm_prev, dg_post_prev, dict(w_gu=d_w_gu, w_down=d_w_down, g_pre=dg_pre)


def _place():
    return lax.axis_index("x"), lax.axis_index("y"), lax.axis_index("c")


def all_gather(name, blks):
    n = len(blks)

    def body(*refs):
        x_refs, out_refs = refs[:n], refs[n:2 * n]
        send_sems, recv_sems, local_sems = refs[2 * n:]
        x, y, cc = _place()
        me, sibling = (x, y, cc), (x, y, 1 - cc)
        chips = [(1 - x, y), (x, 1 - y), (1 - x, 1 - y)]
        south = cc == 0
        via = (jnp.where(south, 1 - x, x), jnp.where(south, y, 1 - y))
        onward = (jnp.where(south, x, 1 - x), jnp.where(south, 1 - y, y))
        k_via, k_onward = 1 + cc, 2 - cc

        def slot(a, px, py, pc):
            return out_refs[a].at[4 * px + 2 * py + pc]

        def copy(a, k, block, to, src=None):
            return pltpu.make_async_remote_copy(
                src_ref=slot(a, *block) if src is None else src, dst_ref=slot(a, *block),
                send_sem=send_sems.at[7 * a + k], recv_sem=recv_sems.at[7 * a + k], device_id=to, device_id_type=MESH)

        mine = [pltpu.make_async_copy(x_refs[a], slot(a, *me), local_sems.at[a]) for a in range(n)]
        sends = []
        for a in range(n):
            mine[a].start()
            sends.append(copy(a, 0, me, sibling, src=x_refs[a]))
            sends += [copy(a, 1 + j, me, (*chips[j], cc), src=x_refs[a]) for j in range(2)]
        for cp in sends:
            cp.start()
        for a in range(n):
            copy(a, k_via, (*via, cc), me).wait_recv()
            sends.append(copy(a, 3, (*via, cc), (*onward, cc)))
            sends.append(copy(a, 3 + k_via, (*via, cc), sibling))
            sends[-2].start()
            sends[-1].start()
        for a in range(n):
            copy(a, k_onward, (*onward, cc), me).wait_recv()
            sends.append(copy(a, 3 + k_onward, (*onward, cc), sibling))
            sends[-1].start()
        for a in range(n):
            copy(a, 3, (*chips[2], cc), me).wait_recv()
            sends.append(copy(a, 6, (*chips[2], cc), sibling))
            sends[-1].start()
        for a in range(n):
            copy(a, 0, sibling, me).wait_recv()
            for j, chip in enumerate(chips):
                copy(a, 4 + j, (*chip, 1 - cc), me).wait_recv()
        for cp in sends:
            cp.wait_send()
        for cp in mine:
            cp.wait()

    return pl.pallas_call(
        body, out_shape=[_sds((N_DEV,) + b.shape, b.dtype) for b in blks], in_specs=[ANY] * n, out_specs=[ANY] * n,
        scratch_shapes=[pltpu.SemaphoreType.DMA((7 * n,)), pltpu.SemaphoreType.DMA((7 * n,)),
                        pltpu.SemaphoreType.DMA((n,))],
        name=name)(*blks)


def exchange_pair(name, gs):
    n = len(gs)
    nchip = 4

    def body(*refs):
        g_refs, land_refs = refs[:n], refs[n:2 * n]
        send_sems, recv_sems = refs[2 * n:]
        x, y, cc = _place()
        copies = [pltpu.make_async_remote_copy(
            src_ref=g_refs[a].at[j, 1 - cc], dst_ref=land_refs[a].at[j], send_sem=send_sems.at[nchip * a + j],
            recv_sem=recv_sems.at[nchip * a + j], device_id=(x, y, 1 - cc), device_id_type=MESH)
            for a in range(n) for j in range(nchip)]
        for cp in copies:
            cp.start()
        for cp in copies:
            cp.wait()

    return pl.pallas_call(
        body, out_shape=[_sds((nchip,) + g.shape[2:], g.dtype) for g in gs], in_specs=[ANY] * n, out_specs=[ANY] * n,
        scratch_shapes=[pltpu.SemaphoreType.DMA((nchip * n,)), pltpu.SemaphoreType.DMA((nchip * n,))],
        name=name)(*gs)


def pair_sum(name, g, land, core, out_dtype):
    nchip, _, r, c = g.shape
    tr = _divisor_tile(r, 1024, 16)

    def body(core_ref, g_ref, l_ref, o_ref):
        o_ref[...] = (g_ref[...] + l_ref[...]).astype(o_ref.dtype)

    return pl.pallas_call(
        body,
        grid_spec=pltpu.PrefetchScalarGridSpec(
            num_scalar_prefetch=1, grid=(nchip, r // tr),
            in_specs=[pl.BlockSpec((None, None, tr, c), lambda j, i, core_ref: (j, core_ref[0], i, 0)),
                      pl.BlockSpec((None, tr, c), lambda j, i, core_ref: (j, i, 0))],
            out_specs=pl.BlockSpec((None, tr, c), lambda j, i, core_ref: (j, i, 0))),
        out_shape=_sds((nchip, r, c), out_dtype), compiler_params=_params("parallel", "parallel"), name=name,
    )(core, g, land)


def exchange_chips(name, ps):
    n = len(ps)

    def body(*refs):
        p_refs, land_refs = refs[:n], refs[n:2 * n]
        send_sems, recv_sems, local_sems = refs[2 * n:]
        x, y, cc = _place()
        mine = 2 * x + y
        chips = [(1 - x, y), (x, 1 - y), (1 - x, 1 - y)]
        own = [pltpu.make_async_copy(p_refs[a].at[mine], land_refs[a].at[mine], local_sems.at[a]) for a in range(n)]
        for cp in own:
            cp.start()
        sends = [pltpu.make_async_remote_copy(
            src_ref=p_refs[a].at[2 * px + py], dst_ref=land_refs[a].at[mine], send_sem=send_sems.at[3 * a + k],
            recv_sem=recv_sems.at[3 * a + k], device_id=(px, py, cc), device_id_type=MESH)
            for a in range(n) for k, (px, py) in enumerate(chips)]
        for cp in sends:
            cp.start()
        for a in range(n):
            for k, (px, py) in enumerate(chips):
                pltpu.make_async_remote_copy(
                    src_ref=p_refs[a].at[mine], dst_ref=land_refs[a].at[2 * px + py], send_sem=send_sems.at[3 * a + k],
                    recv_sem=recv_sems.at[3 * a + k], device_id=(px, py, cc), device_id_type=MESH).wait_recv()
        for cp in sends:
            cp.wait_send()
        for cp in own:
            cp.wait()

    return pl.pallas_call(
        body, out_shape=[_sds(p.shape, p.dtype) for p in ps], in_specs=[ANY] * n, out_specs=[ANY] * n,
        scratch_shapes=[pltpu.SemaphoreType.DMA((3 * n,)), pltpu.SemaphoreType.DMA((3 * n,)),
                        pltpu.SemaphoreType.DMA((n,))],
        name=name)(*ps)


def adamw(name, parts, w, m, v):
    npart, r, c = parts.shape
    tr = _divisor_tile(r, 512, 16)
    c1 = 1.0 / (1.0 - ADAM_B1 ** ADAM_STEP)
    c2 = 1.0 / (1.0 - ADAM_B2 ** ADAM_STEP)

    def body(p_ref, w_ref, m_ref, v_ref, g_ref, d_ref, nm_ref, nv_ref):
        g = p_ref[0].astype(F32)
        for j in range(1, npart):
            g = g + p_ref[j].astype(F32)
        nm = ADAM_B1 * m_ref[...] + (1.0 - ADAM_B1) * g
        nv = ADAM_B2 * v_ref[...] + (1.0 - ADAM_B2) * (g * g)
        g_ref[...] = g
        nm_ref[...] = nm
        nv_ref[...] = nv
        d_ref[...] = -ADAM_LR * ((nm * c1) / (jnp.sqrt(nv * c2) + ADAM_EPS) + ADAM_WD * w_ref[...])

    row = pl.BlockSpec((tr, c), lambda i: (i, 0))
    return pl.pallas_call(
        body, grid=(r // tr,), in_specs=[pl.BlockSpec((npart, tr, c), lambda i: (0, i, 0)), row, row, row],
        out_specs=[row] * 4, out_shape=[_sds((r, c), F32)] * 4, compiler_params=_params("parallel"), name=name,
    )(parts, w, m, v)


def _pack(arrays, dtype, row_multiple):
    flat = jnp.concatenate([a.astype(dtype).reshape(-1) for a in arrays])
    per = row_multiple * LANES
    total = -(-flat.shape[0] // per) * per
    return jnp.pad(flat, (0, total - flat.shape[0])).reshape(total // LANES, LANES)


def _pack_blocked(arrays, dtype, row_multiple):
    flat = jnp.concatenate([a.astype(dtype).reshape(N_DEV, -1) for a in arrays], axis=1)
    per = row_multiple * LANES
    total = -(-flat.shape[1] // per) * per
    return jnp.pad(flat, ((0, 0), (0, total - flat.shape[1]))).reshape(N_DEV, total // LANES, LANES)


def _unpack(buf, shapes, lead=()):
    flat = buf.reshape(lead + (-1,))
    out, off = [], 0
    for s in shapes:
        n = math.prod(s)
        out.append(flat[..., off:off + n].reshape(lead + tuple(s)))
        off += n
    return out


def _to_blocked(full, ax):
    s = full.shape
    return jnp.moveaxis(full.reshape(s[:ax] + (N_DEV, s[ax] // N_DEV) + s[ax + 1:]), ax, 0)


def _from_blocked(blk, ax):
    moved = jnp.moveaxis(blk, 0, ax)
    s = moved.shape
    return moved.reshape(s[:ax] + (s[ax] * s[ax + 1],) + s[ax + 2:])


SMALL = ("rg_conv_w", "rg_conv_b", "rg_b_a", "rg_b_i", "rg_lambda")
GU = "ffn_w_gu"
BIG = ("attn_w_in", "attn_w_out", "rg_w_in", "rg_w_a", "rg_w_i", "rg_w_out", GU, "ffn_w_down")


def kernel(x, attn_w_in, attn_rel_bias, attn_w_out, rg_w_in, rg_conv_w, rg_conv_b, rg_w_a, rg_b_a, rg_w_i, rg_b_i, rg_lambda, rg_w_out, norm_mix_pre, norm_mix_post, norm_ffn_pre, norm_ffn_post, ffn_w_gate, ffn_w_up, ffn_w_down, loss_target, m_attn_w_in, m_attn_rel_bias, m_attn_w_out, m_rg_w_in, m_rg_conv_w, m_rg_conv_b, m_rg_w_a, m_rg_b_a, m_rg_w_i, m_rg_b_i, m_rg_lambda, m_rg_w_out, m_norm_mix_pre, m_norm_mix_post, m_norm_ffn_pre, m_norm_ffn_post, m_ffn_w_gate, m_ffn_w_up, m_ffn_w_down, v_attn_w_in, v_attn_rel_bias, v_attn_w_out, v_rg_w_in, v_rg_conv_w, v_rg_conv_b, v_rg_w_a, v_rg_b_a, v_rg_w_i, v_rg_b_i, v_rg_lambda, v_rg_w_out, v_norm_mix_pre, v_norm_mix_post, v_norm_ffn_pre, v_norm_ffn_post, v_ffn_w_gate, v_ffn_w_up, v_ffn_w_down):
    w_loc = dict(attn_w_in=attn_w_in, attn_rel_bias=attn_rel_bias, attn_w_out=attn_w_out, rg_w_in=rg_w_in,
                 rg_conv_w=rg_conv_w, rg_conv_b=rg_conv_b, rg_w_a=rg_w_a, rg_b_a=rg_b_a, rg_w_i=rg_w_i, rg_b_i=rg_b_i,
                 rg_lambda=rg_lambda, rg_w_out=rg_w_out, norm_mix_pre=norm_mix_pre, norm_mix_post=norm_mix_post,
                 norm_ffn_pre=norm_ffn_pre, norm_ffn_post=norm_ffn_post, ffn_w_gate=ffn_w_gate, ffn_w_up=ffn_w_up,
                 ffn_w_down=ffn_w_down)
    m_loc = dict(attn_w_in=m_attn_w_in, attn_rel_bias=m_attn_rel_bias, attn_w_out=m_attn_w_out, rg_w_in=m_rg_w_in,
                 rg_conv_w=m_rg_conv_w, rg_conv_b=m_rg_conv_b, rg_w_a=m_rg_w_a, rg_b_a=m_rg_b_a, rg_w_i=m_rg_w_i,
                 rg_b_i=m_rg_b_i, rg_lambda=m_rg_lambda, rg_w_out=m_rg_w_out, norm_mix_pre=m_norm_mix_pre,
                 norm_mix_post=m_norm_mix_post, norm_ffn_pre=m_norm_ffn_pre, norm_ffn_post=m_norm_ffn_post,
                 ffn_w_gate=m_ffn_w_gate, ffn_w_up=m_ffn_w_up, ffn_w_down=m_ffn_w_down)
    v_loc = dict(attn_w_in=v_attn_w_in, attn_rel_bias=v_attn_rel_bias, attn_w_out=v_attn_w_out, rg_w_in=v_rg_w_in,
                 rg_conv_w=v_rg_conv_w, rg_conv_b=v_rg_conv_b, rg_w_a=v_rg_w_a, rg_b_a=v_rg_b_a, rg_w_i=v_rg_w_i,
                 rg_b_i=v_rg_b_i, rg_lambda=v_rg_lambda, rg_w_out=v_rg_w_out, norm_mix_pre=v_norm_mix_pre,
                 norm_mix_post=v_norm_mix_post, norm_ffn_pre=v_norm_ffn_pre, norm_ffn_post=v_norm_ffn_post,
                 ffn_w_gate=v_ffn_w_gate, ffn_w_up=v_ffn_w_up, ffn_w_down=v_ffn_w_down)
    axis_of = dict(SHARDED)
    xt, target = x[0], loss_target[0]
    d_model = xt.shape[1]
    rows2d = lambda a: a.reshape(-1, a.shape[-1])
    small_shapes = [w_loc[n].shape for n in SMALL]
    f8 = ffn_w_gate.shape[-1]
    for d in (w_loc, m_loc, v_loc):
        d[GU] = merge_gu(d["ffn_w_gate"], d["ffn_w_up"])

    gathered = all_gather("gather_weights", [rows2d(w_loc[n]).astype(BF16) for n in BIG]
                          + [_pack([w_loc[n] for n in SMALL], F32, 8)])
    blocked = {n: g.reshape((N_DEV,) + w_loc[n].shape) for n, g in zip(BIG, gathered)}
    blocked.update(zip(SMALL, _unpack(gathered[-1], small_shapes, (N_DEV,))))
    full = {n: _from_blocked(blocked[n], axis_of[n]) for n in SMALL}
    row = lambda a: a.reshape(1, -1).astype(F32)
    square = lambda rows8: rows8.reshape(-1, rows8.shape[-1])
    gates = lambda g: jnp.swapaxes(g, 0, 1).reshape(LRU_BLOCKS, -1, g.shape[-1])

    def layer_weights(layer):
        j = layer // 2
        norms = dict(g_pre=row(norm_mix_pre[layer]), g_post=row(norm_mix_post[layer]), idx=j)
        if layer % 2 == 0:
            mix = dict(w_in=blocked["attn_w_in"], w_out=square(blocked["attn_w_out"][:, j]),
                       rel_bias=attn_rel_bias[j], **norms)
        else:
            mix = dict(w_in=blocked["rg_w_in"], w_out=square(blocked["rg_w_out"][:, j]),
                       conv_w=full["rg_conv_w"][j][:, 0, :], conv_b=row(full["rg_conv_b"][j]),
                       w_a=gates(blocked["rg_w_a"][:, j]), w_i=gates(blocked["rg_w_i"][:, j]),
                       b_a=row(full["rg_b_a"][j]), b_i=row(full["rg_b_i"][j]), lam=row(full["rg_lambda"][j]), **norms)
        ffn = dict(w_gu=blocked[GU], w_down=blocked["ffn_w_down"], idx=layer,
                   g_pre=row(norm_ffn_pre[layer]), g_post=row(norm_ffn_post[layer]))
        return mix, ffn

    weights = [layer_weights(layer) for layer in range(DEPTH)]
    act, tape = xt, []
    h = rmsnorm_fwd("l0_mix_norm", act, weights[0][0]["g_pre"])
    for layer in range(DEPTH):
        mix_w, ffn_w = weights[layer]
        mixer_fwd = attn_layer_fwd if layer % 2 == 0 else rg_layer_fwd
        act, h, saved_mix = mixer_fwd(f"l{layer}_mix", act, h, mix_w, ffn_w["g_pre"])
        g_next = weights[layer + 1][0]["g_pre"] if layer + 1 < DEPTH else None
        act, h, saved_ffn = ffn_layer_fwd(f"l{layer}_ffn", act, h, ffn_w, g_next)
        tape.append((mix_w, ffn_w, saved_mix, saved_ffn))
    dact, sq = loss_grad("loss", act, target)
    loss_part = (0.5 * jnp.sum(sq) / d_model).reshape(1)

    grads = {}
    last = tape[DEPTH - 1]
    dm, grads[("ffn_post", DEPTH - 1)] = norm_bwd(f"l{DEPTH - 1}_ffn_dpost", dact, last[3][-1], last[1]["g_post"],
                                                 None, BF16)
    for layer in reversed(range(DEPTH)):
        mix_w, ffn_w, saved_mix, saved_ffn = tape[layer]
        dact, dm, grads[("mix_post", layer)], grads[("ffn", layer)] = ffn_layer_bwd(
            f"l{layer}_ffn", dm, dact, saved_ffn, ffn_w, (saved_mix[-1], mix_w["g_post"]))
        mixer_bwd = attn_layer_bwd if layer % 2 == 0 else rg_layer_bwd
        prev = (tape[layer - 1][3][-1], tape[layer - 1][1]["g_post"]) if layer else None
        dact, dm, grads[("ffn_post", layer - 1)], grads[("mix", layer)] = mixer_bwd(
            f"l{layer}_mix", dm, dact, saved_mix, mix_w, prev)
    attn_g = [grads[("mix", l)] for l in range(0, DEPTH, 2)]
    rg_g = [grads[("mix", l)] for l in range(1, DEPTH, 2)]
    ffn_g = [grads[("ffn", l)] for l in range(DEPTH)]
    stack = lambda gs, key: jnp.stack([g[key] for g in gs])
    by_owner = lambda gs, key, f: jnp.stack([f(g[key]) for g in gs], axis=1)
    rows8 = lambda a: a.reshape(N_DEV, -1, a.shape[-1])
    ungates = lambda a: jnp.swapaxes(a.reshape(LRU_BLOCKS, N_DEV, -1, a.shape[-1]), 0, 1)
    same = lambda a: a
    blocked_g = dict(
        attn_w_in=by_owner(attn_g, "w_in", same), attn_w_out=by_owner(attn_g, "w_out", rows8),
        rg_w_in=by_owner(rg_g, "w_in", same), rg_w_out=by_owner(rg_g, "w_out", rows8),
        rg_w_a=by_owner(rg_g, "w_a", ungates), rg_w_i=by_owner(rg_g, "w_i", ungates),
        **{GU: by_owner(ffn_g, "w_gu", same)},
        ffn_w_down=by_owner(ffn_g, "w_down", same))
    contrib = dict(
        attn_rel_bias=stack(attn_g, "rel_bias"), rg_conv_w=stack(rg_g, "conv_w")[:, :, None, :],
        rg_conv_b=stack(rg_g, "conv_b")[:, 0], rg_b_a=stack(rg_g, "b_a").reshape(rg_b_a.shape[0], LRU_BLOCKS, -1),
        rg_b_i=stack(rg_g, "b_i").reshape(rg_b_i.shape[0], LRU_BLOCKS, -1), rg_lambda=stack(rg_g, "lam")[:, 0],
        norm_mix_pre=jnp.concatenate([grads[("mix", l)]["g_pre"] for l in range(DEPTH)]),
        norm_mix_post=jnp.concatenate([grads[("mix_post", l)] for l in range(DEPTH)]),
        norm_ffn_pre=jnp.concatenate([g["g_pre"] for g in ffn_g]),
        norm_ffn_post=jnp.concatenate([grads[("ffn_post", l)] for l in range(DEPTH)]),
    )
    small_g = _pack_blocked([_to_blocked(contrib[n], axis_of[n]) for n in SMALL], F32, 8)

    slabs = [blocked_g[n].reshape(4, 2, -1, blocked_g[n].shape[-1]) for n in BIG] + [small_g.reshape(4, 2, -1, LANES)]
    core = lax.axis_index("c").astype(jnp.int32).reshape(1)
    from_sibling = exchange_pair("rs_pair", slabs)
    pairs = [pair_sum(f"rs_pair_sum_{i}", g, l, core, BF16 if i < len(BIG) else F32)
             for i, (g, l) in enumerate(zip(slabs, from_sibling))]
    by_chip = exchange_chips("rs_chips", pairs)
    result = {}
    kinds = ("grad", "delta", "new_m", "new_v")
    for n, parts in zip(BIG, by_chip):
        outs = adamw("adamw_" + n, parts, *[rows2d(d[n]) for d in (w_loc, m_loc, v_loc)])
        for kind, a in zip(kinds, outs):
            result[(kind, n)] = a.reshape(w_loc[n].shape)
    for kind in kinds:
        result[(kind, "ffn_w_gate")], result[(kind, "ffn_w_up")] = split_gu(result.pop((kind, GU)), f8)
    outs = adamw("adamw_small", by_chip[-1], *[_pack([d[n] for n in SMALL], F32, 8) for d in (w_loc, m_loc, v_loc)])
    for kind, buf in zip(kinds, outs):
        result.update({(kind, n): a for n, a in zip(SMALL, _unpack(buf, small_shapes))})
    rep_shapes = [w_loc[n].shape for n in REPLICATED] + [(1,)]
    rep_parts, = all_gather("gather_rep_grads", [_pack([contrib[n] for n in REPLICATED] + [loss_part], F32, 8)])
    outs = adamw("adamw_replicated", rep_parts, *[_pack([d[n] for n in REPLICATED] + [jnp.zeros((1,), F32)], F32, 8)
                                                  for d in (w_loc, m_loc, v_loc)])
    for kind, buf in zip(kinds, outs):
        result.update({(kind, n): a for n, a in zip(REPLICATED + ("loss",), _unpack(buf, rep_shapes))})
    loss = result[("grad", "loss")][0]
    return (loss, dact[None], *[result[(kind, n)] for kind in kinds for n in WEIGHTS])
```

```python
import math

import jax
import jax.numpy as jnp
from jax import lax
from jax.experimental import pallas as pl
from jax.experimental.pallas import tpu as pltpu

F32 = jnp.float32
BF16 = jnp.bfloat16

N_DEV = 8
DEPTH = 4
CHUNK = 64
N_LEFT = 8
BAND = (N_LEFT + 1) * CHUNK
PAD_KEYS = N_LEFT * CHUNK
HEAD_DIM = 64
N_HEADS = 8
REL_CLIP = 256
LRU_BLOCKS = 4
LRU_C = 8.0
RMS_EPS = 1e-6
QK_SCALE = HEAD_DIM ** -0.5

ADAM_LR = 0.001
ADAM_B1 = 0.9
ADAM_B2 = 0.999
ADAM_EPS = 1e-08
ADAM_WD = 0.01
ADAM_STEP = 10

LANES = 1024
V7X_VMEM_LIMIT = 56 * 1024 * 1024

MESH = pl.DeviceIdType.MESH
ANY = pl.BlockSpec(memory_space=pl.ANY)

SHARDED = (
    ("attn_w_in", 2), ("attn_w_out", 1), ("rg_w_in", 2), ("rg_conv_w", 3), ("rg_conv_b", 1),
    ("rg_w_a", 2), ("rg_b_a", 2), ("rg_w_i", 2), ("rg_b_i", 2), ("rg_lambda", 1), ("rg_w_out", 1),
    ("ffn_w_gate", 2), ("ffn_w_up", 2), ("ffn_w_down", 1),
)
REPLICATED = ("attn_rel_bias", "norm_mix_pre", "norm_mix_post", "norm_ffn_pre", "norm_ffn_post")
WEIGHTS = ("attn_w_in", "attn_rel_bias", "attn_w_out", "rg_w_in", "rg_conv_w", "rg_conv_b", "rg_w_a", "rg_b_a",
           "rg_w_i", "rg_b_i", "rg_lambda", "rg_w_out", "norm_mix_pre", "norm_mix_post", "norm_ffn_pre",
           "norm_ffn_post", "ffn_w_gate", "ffn_w_up", "ffn_w_down")


def _params(*dims):
    return pltpu.CompilerParams(dimension_semantics=dims or None, vmem_limit_bytes=V7X_VMEM_LIMIT)


def _sds(shape, dtype):
    return jax.ShapeDtypeStruct(tuple(shape), dtype)


def _row_tile(n, pref):
    t = min(n, pref)
    assert n % t == 0, (n, pref)
    return t


def _divisor_tile(n, limit, multiple):
    if n <= limit:
        return n
    best = max(t for t in range(multiple, limit + 1, multiple) if n % t == 0)
    return best


NN = (((1,), (0,)), ((), ()))
NT = (((1,), (1,)), ((), ()))
TN = (((0,), (0,)), ((), ()))


def _gmm(name, a, b, *, grid, a_blk, a_idx, b_blk, b_idx, o_blk, o_idx, out_shape, out_dtype, dn, acc_shape):
    nk = grid[-1]
    kax = len(grid) - 1

    def body(a_ref, b_ref, o_ref, acc_ref):
        part = lax.dot_general(a_ref[...], b_ref[...], dn, preferred_element_type=F32)
        if nk == 1:
            o_ref[...] = part.astype(o_ref.dtype)
            return
        k = pl.program_id(kax)

        @pl.when(k == 0)
        def _():
            acc_ref[...] = part

        @pl.when(k > 0)
        def _():
            acc_ref[...] += part

        @pl.when(k == nk - 1)
        def _():
            o_ref[...] = acc_ref[...].astype(o_ref.dtype)

    return pl.pallas_call(
        body, grid=grid,
        in_specs=[pl.BlockSpec(a_blk, a_idx), pl.BlockSpec(b_blk, b_idx)],
        out_specs=pl.BlockSpec(o_blk, o_idx),
        out_shape=_sds(out_shape, out_dtype),
        scratch_shapes=[pltpu.VMEM(acc_shape, F32)],
        compiler_params=_params(*(["parallel"] * kax + ["arbitrary"])),
        name=name,
    )(a, b)


def mm_nn(name, a, b, out_dtype, tm=1024, tn=512, tk=1024):
    (m, k), (_, n) = a.shape, b.shape
    tm, tn, tk = _row_tile(m, tm), _row_tile(n, tn), _row_tile(k, tk)
    return _gmm(name, a, b, grid=(m // tm, n // tn, k // tk),
                a_blk=(tm, tk), a_idx=lambda i, j, kk: (i, kk), b_blk=(tk, tn), b_idx=lambda i, j, kk: (kk, j),
                o_blk=(tm, tn), o_idx=lambda i, j, kk: (i, j), out_shape=(m, n), out_dtype=out_dtype, dn=NN,
                acc_shape=(tm, tn))


def mm_nt(name, a, b, out_dtype, tm=1024, tn=512, tk=1024):
    (m, k), (n, _) = a.shape, b.shape
    tm, tn, tk = _row_tile(m, tm), _row_tile(n, tn), _row_tile(k, tk)
    return _gmm(name, a, b, grid=(m // tm, n // tn, k // tk),
                a_blk=(tm, tk), a_idx=lambda i, j, kk: (i, kk), b_blk=(tn, tk), b_idx=lambda i, j, kk: (j, kk),
                o_blk=(tm, tn), o_idx=lambda i, j, kk: (i, j), out_shape=(m, n), out_dtype=out_dtype, dn=NT,
                acc_shape=(tm, tn))


def mm_tn(name, a, b, out_dtype, tm=1024, tn=512, tk=4096):
    (k, m), (_, n) = a.shape, b.shape
    tm, tn, tk = _row_tile(m, tm), _row_tile(n, tn), _row_tile(k, tk)
    return _gmm(name, a, b, grid=(m // tm, n // tn, k // tk),
                a_blk=(tk, tm), a_idx=lambda i, j, kk: (kk, i), b_blk=(tk, tn), b_idx=lambda i, j, kk: (kk, j),
                o_blk=(tm, tn), o_idx=lambda i, j, kk: (i, j), out_shape=(m, n), out_dtype=out_dtype, dn=TN,
                acc_shape=(tm, tn))


MXU_WIDTH = 256


def _blocks_per_step(n8):
    return 1 if n8 % MXU_WIDTH == 0 else 2


def _side_by_side(w_ref, j0, per):
    return w_ref[j0] if per == 1 else jnp.concatenate([w_ref[j0 + j] for j in range(per)], axis=1)


def mm_nn_wblk(name, a, wb, layer, out_dtype, tm=1024):
    (m, k), (nb, _, _, n8) = a.shape, wb.shape
    tm = _row_tile(m, tm)
    per = _blocks_per_step(n8)

    def body(a_ref, w_ref, o_ref):
        o_ref[...] = jnp.dot(a_ref[...], _side_by_side(w_ref, 0, per), preferred_element_type=F32).astype(o_ref.dtype)

    return pl.pallas_call(
        body, grid=(m // tm, nb // per),
        in_specs=[pl.BlockSpec((tm, k), lambda i, j: (i, 0)),
                  pl.BlockSpec((per, None, k, n8), lambda i, j: (j, layer, 0, 0))],
        out_specs=pl.BlockSpec((tm, per * n8), lambda i, j: (i, j)), out_shape=_sds((m, nb * n8), out_dtype),
        compiler_params=_params("parallel", "parallel"), name=name)(a, wb)


def mm_nt_wblk(name, a, wb, layer, out_dtype, tm=1024, tn=512):
    m = a.shape[0]
    nb, _, k, n8 = wb.shape
    tm, tn = _row_tile(m, tm), _row_tile(k, tn)
    per = _blocks_per_step(n8)

    def body(a_ref, b_ref, o_ref):
        acc = None
        for j in range(0, nb, per):
            part = lax.dot_general(a_ref[:, j * n8:(j + per) * n8], _side_by_side(b_ref, j, per), NT,
                                   preferred_element_type=F32)
            acc = part if acc is None else acc + part
        o_ref[...] = acc.astype(o_ref.dtype)

    return pl.pallas_call(
        body, grid=(m // tm, k // tn),
        in_specs=[pl.BlockSpec((tm, nb * n8), lambda i, j: (i, 0)),
                  pl.BlockSpec((nb, None, tn, n8), lambda i, j: (0, layer, j, 0))],
        out_specs=pl.BlockSpec((tm, tn), lambda i, j: (i, j)), out_shape=_sds((m, k), out_dtype),
        compiler_params=_params("parallel", "parallel"), name=name)(a, wb)


def mm_tn_oblk(name, a, b, n8, out_dtype, tk=4096):
    (t, k), nb = a.shape, b.shape[1] // n8
    tk = _row_tile(t, tk)
    per = _blocks_per_step(n8)
    nk = t // tk

    def body(a_ref, b_ref, o_ref, acc_ref):
        s = pl.program_id(1)
        part = lax.dot_general(a_ref[...], b_ref[...], TN, preferred_element_type=F32)

        @pl.when(s == 0)
        def _():
            acc_ref[...] = part

        @pl.when(s > 0)
        def _():
            acc_ref[...] += part

        @pl.when(s == nk - 1)
        def _():
            for j in range(per):
                o_ref[j] = acc_ref[:, j * n8:(j + 1) * n8].astype(o_ref.dtype)

    return pl.pallas_call(
        body, grid=(nb // per, nk),
        in_specs=[pl.BlockSpec((tk, k), lambda j, s: (s, 0)), pl.BlockSpec((tk, per * n8), lambda j, s: (s, j))],
        out_specs=pl.BlockSpec((per, k, n8), lambda j, s: (j, 0, 0)), out_shape=_sds((nb, k, n8), out_dtype),
        scratch_shapes=[pltpu.VMEM((k, per * n8), F32)],
        compiler_params=_params("parallel", "arbitrary"), name=name)(a, b)


def rmsnorm_fwd(name, x, g):
    t, d = x.shape
    tr = _row_tile(t, 512)

    def body(x_ref, g_ref, o_ref):
        xv = x_ref[...]
        r = lax.rsqrt(jnp.mean(xv * xv, axis=-1, keepdims=True) + RMS_EPS)
        o_ref[...] = (xv * r * g_ref[...]).astype(o_ref.dtype)

    return pl.pallas_call(
        body, grid=(t // tr,),
        in_specs=[pl.BlockSpec((tr, d), lambda i: (i, 0)), pl.BlockSpec((1, d), lambda i: (0, 0))],
        out_specs=pl.BlockSpec((tr, d), lambda i: (i, 0)),
        out_shape=_sds((t, d), BF16), compiler_params=_params("parallel"), name=name)(x, g)


def resid_norm_fwd(name, x, m, g, g_next):
    t, d = x.shape
    tr = _row_tile(t, 512)
    chained = g_next is not None

    def body(*refs):
        x_ref, m_ref, g_ref = refs[:3]
        mv = m_ref[...]
        r = lax.rsqrt(jnp.mean(mv * mv, axis=-1, keepdims=True) + RMS_EPS)
        x1 = x_ref[...] + mv * r * g_ref[...]
        if chained:
            gn_ref, o_ref, h_ref = refs[3:]
            r1 = lax.rsqrt(jnp.mean(x1 * x1, axis=-1, keepdims=True) + RMS_EPS)
            h_ref[...] = (x1 * r1 * gn_ref[...]).astype(BF16)
        else:
            o_ref, = refs[3:]
        o_ref[...] = x1

    row = pl.BlockSpec((tr, d), lambda i: (i, 0))
    vec = pl.BlockSpec((1, d), lambda i: (0, 0))
    out = pl.pallas_call(
        body, grid=(t // tr,),
        in_specs=[row, row, vec] + ([vec] if chained else []),
        out_specs=[row, row] if chained else [row],
        out_shape=[_sds((t, d), F32)] + ([_sds((t, d), BF16)] if chained else []),
        compiler_params=_params("parallel"), name=name)(*([x, m, g] + ([g_next] if chained else [])))
    return (out[0], out[1]) if chained else (out[0], None)


def norm_bwd(name, dy, x, g, resid, out_dtype):
    t, d = x.shape
    tr = _row_tile(t, 512)
    has_res = resid is not None

    def body(*refs):
        if has_res:
            dy_ref, x_ref, g_ref, r_ref, dx_ref, dg_ref = refs
        else:
            dy_ref, x_ref, g_ref, dx_ref, dg_ref = refs
        i = pl.program_id(0)
        xv = x_ref[...]
        dyv = dy_ref[...].astype(F32)
        r = lax.rsqrt(jnp.mean(xv * xv, axis=-1, keepdims=True) + RMS_EPS)
        xh = xv * r
        dxh = dyv * g_ref[...]
        dx = r * (dxh - xh * jnp.mean(dxh * xh, axis=-1, keepdims=True))
        if has_res:
            dx = dx + r_ref[...]
        dx_ref[...] = dx.astype(dx_ref.dtype)
        part = jnp.sum(dyv * xh, axis=0, keepdims=True)

        @pl.when(i == 0)
        def _():
            dg_ref[...] = part

        @pl.when(i > 0)
        def _():
            dg_ref[...] += part

    row = pl.BlockSpec((tr, d), lambda i: (i, 0))
    vec = pl.BlockSpec((1, d), lambda i: (0, 0))
    ins = [dy, x, g] + ([resid] if has_res else [])
    return pl.pallas_call(
        body, grid=(t // tr,),
        in_specs=[row, row, vec] + ([row] if has_res else []),
        out_specs=[row, vec],
        out_shape=[_sds((t, d), out_dtype), _sds((1, d), F32)],
        compiler_params=_params("arbitrary"), name=name)(*ins)


def norm_bwd_chain(name, dh, x, g, resid, m_prev, g_prev):
    t, d = x.shape
    tr = _row_tile(t, 512)

    def body(dh_ref, x_ref, g_ref, r_ref, m_ref, gp_ref, dx_ref, dg_ref, dm_ref, dgp_ref):
        i = pl.program_id(0)
        xv = x_ref[...]
        dhv = dh_ref[...]
        r = lax.rsqrt(jnp.mean(xv * xv, axis=-1, keepdims=True) + RMS_EPS)
        xh = xv * r
        dxh = dhv * g_ref[...]
        dx = r * (dxh - xh * jnp.mean(dxh * xh, axis=-1, keepdims=True)) + r_ref[...]
        dx_ref[...] = dx
        mv = m_ref[...]
        rm = lax.rsqrt(jnp.mean(mv * mv, axis=-1, keepdims=True) + RMS_EPS)
        mh = mv * rm
        dmh = dx * gp_ref[...]
        dm_ref[...] = (rm * (dmh - mh * jnp.mean(dmh * mh, axis=-1, keepdims=True))).astype(BF16)
        part = jnp.sum(dhv * xh, axis=0, keepdims=True)
        part_prev = jnp.sum(dx * mh, axis=0, keepdims=True)

        @pl.when(i == 0)
        def _():
            dg_ref[...] = part
            dgp_ref[...] = part_prev

        @pl.when(i > 0)
        def _():
            dg_ref[...] += part
            dgp_ref[...] += part_prev

    row = pl.BlockSpec((tr, d), lambda i: (i, 0))
    vec = pl.BlockSpec((1, d), lambda i: (0, 0))
    return pl.pallas_call(
        body, grid=(t // tr,), in_specs=[row, row, vec, row, row, vec], out_specs=[row, vec, row, vec],
        out_shape=[_sds((t, d), F32), _sds((1, d), F32), _sds((t, d), BF16), _sds((1, d), F32)],
        compiler_params=_params("arbitrary"), name=name)(dh, x, g, resid, m_prev, g_prev)


def close_bwd(tag, dh, x, g_pre, dx1, prev):
    if prev is None:
        dx, dg_pre = norm_bwd(tag + "_dpre", dh, x, g_pre, dx1, F32)
        return dx, dg_pre, None, None
    return norm_bwd_chain(tag + "_dpre", dh, x, g_pre, dx1, *prev)


def loss_grad(name, y, target):
    t, d = y.shape
    tr = _row_tile(t, 512)

    def body(y_ref, t_ref, dy_ref, s_ref):
        i = pl.program_id(0)
        err = y_ref[...] - t_ref[...]
        dy_ref[...] = err * (1.0 / d)
        part = jnp.sum(err * err, axis=0, keepdims=True)

        @pl.when(i == 0)
        def _():
            s_ref[...] = part

        @pl.when(i > 0)
        def _():
            s_ref[...] += part

    row = pl.BlockSpec((tr, d), lambda i: (i, 0))
    vec = pl.BlockSpec((1, d), lambda i: (0, 0))
    return pl.pallas_call(
        body, grid=(t // tr,), in_specs=[row, row], out_specs=[row, vec],
        out_shape=[_sds((t, d), F32), _sds((1, d), F32)],
        compiler_params=_params("arbitrary"), name=name)(y, target)


PAIR = 2 * HEAD_DIM
N_PAIRS = N_HEADS // 2
A_TQ = 512
A_UNROLL_FWD = 8
A_UNROLL_BWD = 4


def _halves(x):
    lane = lax.broadcasted_iota(jnp.int32, x.shape, x.ndim - 1)
    zero = jnp.zeros_like(x)
    return jnp.where(lane < HEAD_DIM, x, zero), jnp.where(lane >= HEAD_DIM, x, zero)


def _merge(a, b):
    lane = lax.broadcasted_iota(jnp.int32, a.shape, a.ndim - 1)
    return jnp.where(lane < HEAD_DIM, a, b)


def _a_valid(c):
    col = lax.broadcasted_iota(jnp.int32, (CHUNK, BAND), 1)
    return col >= (N_LEFT - c) * CHUNK


def attn_a_fwd(name, proj, kp, vp, bias, q_blk):
    t = proj.shape[0]
    tq = _row_tile(t, A_TQ)
    ncs = tq // CHUNK
    un = math.gcd(A_UNROLL_FWD, ncs)

    def body(q_ref, k_ref, v_ref, b_ref, o_ref, l_ref):
        i = pl.program_id(1)

        def group(gg, carry):
            cs = [i * ncs + gg * un + u for u in range(un)]
            r0s = [pl.multiple_of((gg * un + u) * CHUNK, CHUNK) for u in range(un)]
            k0s = [pl.multiple_of(c * CHUNK, CHUNK) for c in cs]
            ss = []
            for u in range(un):
                qh = _halves(q_ref[pl.ds(r0s[u], CHUNK), :] * QK_SCALE)
                kwin = k_ref[pl.ds(k0s[u], BAND), :]
                valid = _a_valid(cs[u])
                for hh in range(2):
                    s = lax.dot_general(qh[hh], kwin, NT, preferred_element_type=F32) + b_ref[hh]
                    ss.append(jnp.where(valid, s, -1e30))
            ps, lses = [], []
            for s in ss:
                mx = jnp.max(s, axis=-1, keepdims=True)
                p = jnp.exp(s - mx)
                den = jnp.sum(p, axis=-1, keepdims=True)
                ps.append((p * (1.0 / den)).astype(BF16))
                lses.append(mx + jnp.log(den))
            for u in range(un):
                vwin = v_ref[pl.ds(k0s[u], BAND), :]
                o0 = jnp.dot(ps[2 * u], vwin, preferred_element_type=F32)
                o1 = jnp.dot(ps[2 * u + 1], vwin, preferred_element_type=F32)
                o_ref[pl.ds(r0s[u], CHUNK), :] = _merge(o0, o1)
                l_ref[pl.ds(r0s[u], CHUNK), :] = jnp.concatenate([lses[2 * u], lses[2 * u + 1]], axis=1)
            return carry

        lax.fori_loop(0, ncs // un, group, 0)

    return pl.pallas_call(
        body, grid=(N_PAIRS, t // tq),
        in_specs=[pl.BlockSpec((tq, PAIR), lambda p, i: (i, q_blk + p)),
                  pl.BlockSpec((t + PAD_KEYS, PAIR), lambda p, i: (0, p)),
                  pl.BlockSpec((t + PAD_KEYS, PAIR), lambda p, i: (0, p)),
                  pl.BlockSpec((2, CHUNK, BAND), lambda p, i: (p, 0, 0))],
        out_specs=[pl.BlockSpec((tq, PAIR), lambda p, i: (i, p)),
                   pl.BlockSpec((None, tq, 2), lambda p, i: (p, i, 0))],
        out_shape=[_sds((t, N_PAIRS * PAIR), F32), _sds((N_PAIRS, t, 2), F32)],
        compiler_params=_params("parallel", "parallel"), name=name)(proj, kp, vp, bias)


def attn_a_bwd(name, proj, kp, vp, bias, o, lse, do, q_blk, do_blk):
    t = proj.shape[0]
    tq = _row_tile(t, A_TQ)
    ncs = tq // CHUNK
    un = math.gcd(A_UNROLL_BWD, ncs)

    def body(q_ref, k_ref, v_ref, b_ref, o_ref, l_ref, do_ref, dq_ref, dk_ref, dv_ref, db_ref):
        i = pl.program_id(1)

        @pl.when(i == 0)
        def _():
            dk_ref[...] = jnp.zeros_like(dk_ref)
            dv_ref[...] = jnp.zeros_like(dv_ref)
            db_ref[...] = jnp.zeros_like(db_ref)

        def group(gg, carry):
            cs = [i * ncs + gg * un + u for u in range(un)]
            r0s = [pl.multiple_of((gg * un + u) * CHUNK, CHUNK) for u in range(un)]
            k0s = [pl.multiple_of(c * CHUNK, CHUNK) for c in cs]
            qhs, dohs, ps, dps, deltas = [], [], [], [], []
            for u in range(un):
                rows = pl.ds(r0s[u], CHUNK)
                qh = _halves(q_ref[rows, :] * QK_SCALE)
                doh = _halves(do_ref[rows, :])
                kwin = k_ref[pl.ds(k0s[u], BAND), :]
                vwin = v_ref[pl.ds(k0s[u], BAND), :]
                valid = _a_valid(cs[u])
                dl = _halves(do_ref[rows, :].astype(F32) * o_ref[rows, :])
                for hh in range(2):
                    s = lax.dot_general(qh[hh], kwin, NT, preferred_element_type=F32) + b_ref[hh]
                    ps.append(jnp.where(valid, jnp.exp(s - l_ref[rows, hh:hh + 1]), 0.0))
                    dps.append(lax.dot_general(doh[hh], vwin, NT, preferred_element_type=F32))
                    deltas.append(jnp.sum(dl[hh], axis=-1, keepdims=True))
                qhs.append(qh)
                dohs.append(doh)
            dss = [p * (dp - dl) for p, dp, dl in zip(ps, dps, deltas)]
            for hh in range(2):
                tot = dss[hh]
                for u in range(1, un):
                    tot = tot + dss[2 * u + hh]
                db_ref[hh] += tot
            for u in range(un):
                kwin = k_ref[pl.ds(k0s[u], BAND), :]
                ds0, ds1 = dss[2 * u].astype(BF16), dss[2 * u + 1].astype(BF16)
                dq_ref[pl.ds(r0s[u], CHUNK), :] = _merge(jnp.dot(ds0, kwin, preferred_element_type=F32),
                                                         jnp.dot(ds1, kwin, preferred_element_type=F32)) * QK_SCALE
                dk_ref[pl.ds(k0s[u], BAND), :] += (lax.dot_general(ds0, qhs[u][0], TN, preferred_element_type=F32)
                                                   + lax.dot_general(ds1, qhs[u][1], TN, preferred_element_type=F32))
                dv_ref[pl.ds(k0s[u], BAND), :] += (
                    lax.dot_general(ps[2 * u].astype(BF16), dohs[u][0], TN, preferred_element_type=F32)
                    + lax.dot_general(ps[2 * u + 1].astype(BF16), dohs[u][1], TN, preferred_element_type=F32))
            return carry

        lax.fori_loop(0, ncs // un, group, 0)

    tile = lambda blk: pl.BlockSpec((tq, PAIR), lambda p, i: (i, blk + p))
    whole = pl.BlockSpec((t + PAD_KEYS, PAIR), lambda p, i: (0, p))
    bspec = pl.BlockSpec((2, CHUNK, BAND), lambda p, i: (p, 0, 0))
    return pl.pallas_call(
        body, grid=(N_PAIRS, t // tq),
        in_specs=[tile(q_blk), whole, whole, bspec, tile(0), pl.BlockSpec((None, tq, 2), lambda p, i: (p, i, 0)),
                  tile(do_blk)],
        out_specs=[tile(0), whole, whole, bspec],
        out_shape=[_sds((t, N_PAIRS * PAIR), F32), _sds((t + PAD_KEYS, N_PAIRS * PAIR), F32),
                   _sds((t + PAD_KEYS, N_PAIRS * PAIR), F32), _sds((2 * N_PAIRS, CHUNK, BAND), F32)],
        compiler_params=_params("parallel", "arbitrary"), name=name)(proj, kp, vp, bias, o, lse, do)


SB_TQ = 256
SB_TK = 256
SB_DEAD = -125.0


def _tri(n, strict):
    j = lax.broadcasted_iota(jnp.int32, (n, n), 0)
    s = lax.broadcasted_iota(jnp.int32, (n, n), 1)
    return jnp.where((j > s) if strict else (j >= s), 1.0, 0.0).astype(BF16)


def _suffix_sum(x, tri, exact):
    hi = x.astype(BF16)
    out = jnp.dot(hi, tri, preferred_element_type=F32)
    if exact:
        lo = (x - hi.astype(F32)).astype(BF16)
        out = out + jnp.dot(lo, tri, preferred_element_type=F32)
    return out


def _sb_scores(qh, ks, causal):
    z = lax.dot_general(qh, ks, NT, preferred_element_type=F32)
    lb = jnp.minimum(z, 0.0) - jnp.log(1.0 + jnp.exp(-jnp.abs(z)))
    m = lb - z
    if causal is not None:
        m = jnp.where(causal, m, 0.0)
    return lb, m


def _causal(tq, tk, off):
    return (lax.broadcasted_iota(jnp.int32, (tq, tk), 1) + off * tk) < lax.broadcasted_iota(jnp.int32, (tq, tk), 0)


def sb_fwd(name, proj, q_blk, k_blk, v_blk):
    t = proj.shape[0]
    tq = _row_tile(t, SB_TQ)
    tk = min(SB_TK, tq)
    per = tq // tk

    def body(q_ref, k_ref, v_ref, o_ref):
        i = pl.program_id(1)
        tri = _tri(tk, True)
        qh = _halves(q_ref[...] * QK_SCALE)

        def blocks(kb, carry, off):
            k0 = pl.multiple_of(kb * tk, tk)
            ks, vs = k_ref[pl.ds(k0, tk), :], v_ref[pl.ds(k0, tk), :]
            causal = None if off is None else _causal(tq, tk, off)
            lbm = [_sb_scores(qh[hh], ks, causal) for hh in range(2)]
            afters = [_suffix_sum(lbm[hh][1], tri, False) for hh in range(2)]
            out = []
            for hh in range(2):
                acc, cm = carry[2 * hh], carry[2 * hh + 1]
                w = jnp.exp(lbm[hh][0] + afters[hh] + cm)
                if causal is not None:
                    w = jnp.where(causal, w, 0.0)
                out += [acc + jnp.dot(w.astype(BF16), vs, preferred_element_type=F32),
                        cm + jnp.sum(lbm[hh][1], axis=-1, keepdims=True)]
            return tuple(out)

        def alive(carry):
            return jnp.maximum(jnp.max(carry[1]), jnp.max(carry[3])) > SB_DEAD

        carry = (jnp.zeros((tq, PAIR), F32), jnp.zeros((tq, 1), F32)) * 2
        for off in reversed(range(per)):
            carry = blocks(i * per + off, carry, off)

        def step(c):
            new = blocks(i * per - 1 - c[0], c[2:], None)
            return (c[0] + 1, alive(new)) + new

        out = lax.while_loop(lambda c: jnp.logical_and(c[0] < i * per, c[1]), step,
                             (jnp.int32(0), alive(carry)) + carry)
        o_ref[...] = _merge(out[2], out[4])

    return pl.pallas_call(
        body, grid=(N_PAIRS, t // tq),
        in_specs=[pl.BlockSpec((tq, PAIR), lambda p, i: (i, q_blk + p)),
                  pl.BlockSpec((t, PAIR), lambda p, i: (0, k_blk + p)),
                  pl.BlockSpec((t, PAIR), lambda p, i: (0, v_blk + p))],
        out_specs=pl.BlockSpec((tq, PAIR), lambda p, i: (i, p)),
        out_shape=_sds((t, N_PAIRS * PAIR), F32), compiler_params=_params("parallel", "parallel"), name=name,
    )(proj, proj, proj)


def sb_bwd(name, proj, o, do, q_blk, k_blk, v_blk, do_blk):
    t = proj.shape[0]
    tq = _row_tile(t, SB_TQ)
    tk = min(SB_TK, tq)
    per = tq // tk

    def body(q_ref, k_ref, v_ref, o_ref, do_ref, dq_ref, dk_ref, dv_ref):
        i = pl.program_id(1)

        @pl.when(i == 0)
        def _():
            dk_ref[...] = jnp.zeros_like(dk_ref)
            dv_ref[...] = jnp.zeros_like(dv_ref)

        tri_s, tri_i = _tri(tk, True), _tri(tk, False)
        qh = _halves(q_ref[...] * QK_SCALE)
        doh = _halves(do_ref[...])
        deltas = [jnp.sum(x, axis=-1, keepdims=True) for x in _halves(do_ref[...].astype(F32) * o_ref[...])]

        def blocks(kb, carry, off):
            k0 = pl.multiple_of(kb * tk, tk)
            ks, vs = k_ref[pl.ds(k0, tk), :], v_ref[pl.ds(k0, tk), :]
            causal = None if off is None else _causal(tq, tk, off)
            lbm = [_sb_scores(qh[hh], ks, causal) for hh in range(2)]
            dws = [lax.dot_general(doh[hh], vs, NT, preferred_element_type=F32) for hh in range(2)]
            afters = [_suffix_sum(lbm[hh][1], tri_s, False) for hh in range(2)]
            wbs, es = [], []
            for hh in range(2):
                w = jnp.exp(lbm[hh][0] + afters[hh] + carry[3 * hh + 1])
                if causal is not None:
                    w = jnp.where(causal, w, 0.0)
                wbs.append(w.astype(BF16))
                es.append(wbs[hh].astype(F32) * dws[hh])
            sfx = [_suffix_sum(es[hh], tri_i, True) for hh in range(2)]
            dzs = []
            for hh in range(2):
                left = deltas[hh] - (sfx[hh] + carry[3 * hh + 2])
                sig = jnp.exp(lbm[hh][0])
                dz = es[hh] * (1.0 - sig) - left * sig
                if causal is not None:
                    dz = jnp.where(causal, dz, 0.0)
                dzs.append(dz.astype(BF16))
            dk_ref[pl.ds(k0, tk), :] += (lax.dot_general(dzs[0], qh[0], TN, preferred_element_type=F32)
                                         + lax.dot_general(dzs[1], qh[1], TN, preferred_element_type=F32))
            dv_ref[pl.ds(k0, tk), :] += (lax.dot_general(wbs[0], doh[0], TN, preferred_element_type=F32)
                                         + lax.dot_general(wbs[1], doh[1], TN, preferred_element_type=F32))
            out = []
            for hh in range(2):
                out += [carry[3 * hh] + jnp.dot(dzs[hh], ks, preferred_element_type=F32),
                        carry[3 * hh + 1] + jnp.sum(lbm[hh][1], axis=-1, keepdims=True),
                        carry[3 * hh + 2] + jnp.sum(es[hh], axis=-1, keepdims=True)]
            return tuple(out)

        def alive(carry):
            return jnp.maximum(jnp.max(carry[1]), jnp.max(carry[4])) > SB_DEAD

        zero = jnp.zeros((tq, 1), F32)
        carry = (jnp.zeros((tq, PAIR), F32), zero, zero) * 2
        for off in reversed(range(per)):
            carry = blocks(i * per + off, carry, off)

        def step(c):
            new = blocks(i * per - 1 - c[0], c[2:], None)
            return (c[0] + 1, alive(new)) + new

        out = lax.while_loop(lambda c: jnp.logical_and(c[0] < i * per, c[1]), step,
                             (jnp.int32(0), alive(carry)) + carry)
        dq_ref[...] = _merge(out[2], out[5]) * QK_SCALE

    tile = lambda blk: pl.BlockSpec((tq, PAIR), lambda p, i: (i, blk + p))
    whole = lambda blk: pl.BlockSpec((t, PAIR), lambda p, i: (0, blk + p))
    return pl.pallas_call(
        body, grid=(N_PAIRS, t // tq),
        in_specs=[tile(q_blk), whole(k_blk), whole(v_blk), tile(0), tile(do_blk)],
        out_specs=[tile(0), whole(0), whole(0)],
        out_shape=[_sds((t, N_PAIRS * PAIR), F32)] * 3,
        compiler_params=_params("parallel", "arbitrary"), name=name)(proj, proj, proj, o, do)


def _sigmoid(x):
    return 1.0 / (1.0 + jnp.exp(-x))


def gu_gap(f8):
    return -(-f8 // 128) * 128


def merge_gu(gate, up):
    f8 = gate.shape[-1]
    pad = jnp.zeros(gate.shape[:-1] + (gu_gap(f8) - f8,), gate.dtype)
    return jnp.concatenate([gate, pad, up], axis=-1)


def split_gu(gu, f8):
    return gu[..., :f8], gu[..., gu_gap(f8):]


PER = 2


def _stacked_pair(w_ref, p, f8):
    zeros = jnp.zeros((gu_gap(f8) - f8, w_ref.shape[-1]), w_ref.dtype)
    return jnp.concatenate([w_ref[PER * p], zeros, w_ref[PER * p + 1]], axis=0)


def ffn_up(name, h, wgu, f8, layer):
    t, d = h.shape
    nb, _, _, fw = wgu.shape
    gap = gu_gap(f8)
    tm = _row_tile(t, 1024)

    def body(h_ref, w_ref, gu_ref, a_ref):
        hv = h_ref[...]
        rs = [jnp.dot(hv, w_ref[j], preferred_element_type=F32) for j in range(PER)]
        a_ref[:, f8:gap] = jnp.zeros((tm, gap - f8), BF16)
        for j in range(PER):
            gu_ref[j] = rs[j].astype(BF16)
            g, u = rs[j][:, :f8], rs[j][:, gap:]
            a_ref[:, j * gap:j * gap + f8] = (g * _sigmoid(g) * u).astype(BF16)

    return pl.pallas_call(
        body, grid=(t // tm, nb // PER),
        in_specs=[pl.BlockSpec((tm, d), lambda i, k: (i, 0)),
                  pl.BlockSpec((PER, None, d, fw), lambda i, k: (k, layer, 0, 0))],
        out_specs=[pl.BlockSpec((PER, tm, fw), lambda i, k: (k, i, 0)),
                   pl.BlockSpec((None, tm, fw), lambda i, k: (k, i, 0))],
        out_shape=[_sds((nb, t, fw), BF16), _sds((nb // PER, t, fw), BF16)],
        compiler_params=_params("parallel", "parallel"), name=name)(h, wgu)


def _mm_all_blocks(name, a, w, layer, dn, tm):
    nb, t, f = a.shape
    wshape = w.shape[2:]
    d = wshape[1] if dn == NN else wshape[0]
    tm = _row_tile(t, tm)

    def body(a_ref, w_ref, o_ref):
        acc = lax.dot_general(a_ref[0], w_ref[0], dn, preferred_element_type=F32)
        for k in range(1, nb):
            acc = acc + lax.dot_general(a_ref[k], w_ref[k], dn, preferred_element_type=F32)
        o_ref[...] = acc

    return pl.pallas_call(
        body, grid=(t // tm,),
        in_specs=[pl.BlockSpec((nb, tm, f), lambda i: (0, i, 0)),
                  pl.BlockSpec((nb, None) + wshape, lambda i: (0, layer, 0, 0))],
        out_specs=pl.BlockSpec((tm, d), lambda i: (i, 0)), out_shape=_sds((t, d), F32),
        compiler_params=_params("parallel"), name=name)(a, w)


def ffn_down(name, a, wd, f8, layer):
    npair, t, fw = a.shape
    nb, _, _, d = wd.shape
    tm = _row_tile(t, 512)

    def body(a_ref, w_ref, o_ref):
        acc = jnp.dot(a_ref[0], _stacked_pair(w_ref, 0, f8), preferred_element_type=F32)
        for p in range(1, npair):
            acc = acc + jnp.dot(a_ref[p], _stacked_pair(w_ref, p, f8), preferred_element_type=F32)
        o_ref[...] = acc

    return pl.pallas_call(
        body, grid=(t // tm,),
        in_specs=[pl.BlockSpec((npair, tm, fw), lambda i: (0, i, 0)),
                  pl.BlockSpec((nb, None, f8, d), lambda i: (0, layer, 0, 0))],
        out_specs=pl.BlockSpec((tm, d), lambda i: (i, 0)), out_shape=_sds((t, d), F32),
        compiler_params=_params("parallel"), name=name)(a, wd)


def ffn_bwd_act(name, dm, wd, gu, f8, layer):
    t, d = dm.shape
    nb, _, fw = gu.shape
    gap = gu_gap(f8)
    tm = _row_tile(t, 1024)

    def body(dm_ref, wd_ref, gu_ref, o_ref):
        da = lax.dot_general(dm_ref[...], _stacked_pair(wd_ref, 0, f8), NT, preferred_element_type=F32)
        o_ref[:, :, f8:gap] = jnp.zeros((PER, tm, gap - f8), BF16)
        for j in range(PER):
            daj = da[:, j * gap:j * gap + f8]
            gv = gu_ref[j, :, :f8].astype(F32)
            uv = gu_ref[j, :, gap:].astype(F32)
            sg = _sigmoid(gv)
            o_ref[j, :, :f8] = (daj * uv * sg * (1.0 + gv * (1.0 - sg))).astype(BF16)
            o_ref[j, :, gap:] = (daj * gv * sg).astype(BF16)

    bspec = pl.BlockSpec((PER, tm, fw), lambda i, k: (k, i, 0))
    return pl.pallas_call(
        body, grid=(t // tm, nb // PER),
        in_specs=[pl.BlockSpec((tm, d), lambda i, k: (i, 0)),
                  pl.BlockSpec((PER, None, f8, d), lambda i, k: (k, layer, 0, 0)), bspec],
        out_specs=bspec, out_shape=_sds((nb, t, fw), BF16),
        compiler_params=_params("parallel", "parallel"), name=name)(dm, wd, gu)


def ffn_bwd_dh(name, dgu, wgu, layer):
    return _mm_all_blocks(name, dgu, wgu, layer, NT, 512)


def ffn_dw_in(name, h, dact):
    t, d = h.shape
    nb, _, f8 = dact.shape
    tk = _row_tile(t, 4096)
    return _gmm(name, h, dact, grid=(nb, t // tk),
                a_blk=(tk, d), a_idx=lambda b, s: (s, 0), b_blk=(None, tk, f8), b_idx=lambda b, s: (b, s, 0),
                o_blk=(None, d, f8), o_idx=lambda b, s: (b, 0, 0), out_shape=(nb, d, f8), out_dtype=F32, dn=TN,
                acc_shape=(d, f8))


def ffn_dw_down(name, a, dm, f8):
    npair, t, fw = a.shape
    d = dm.shape[1]
    gap = gu_gap(f8)
    tk = _row_tile(t, 4096)
    nk = t // tk

    def body(a_ref, dm_ref, o_ref, acc_ref):
        s = pl.program_id(1)
        part = lax.dot_general(a_ref[...], dm_ref[...], TN, preferred_element_type=F32)

        @pl.when(s == 0)
        def _():
            acc_ref[...] = part

        @pl.when(s > 0)
        def _():
            acc_ref[...] += part

        @pl.when(s == nk - 1)
        def _():
            for j in range(PER):
                o_ref[j] = acc_ref[j * gap:j * gap + f8, :]

    return pl.pallas_call(
        body, grid=(npair, nk),
        in_specs=[pl.BlockSpec((None, tk, fw), lambda p, s: (p, s, 0)), pl.BlockSpec((tk, d), lambda p, s: (s, 0))],
        out_specs=pl.BlockSpec((PER, f8, d), lambda p, s: (p, 0, 0)), out_shape=_sds((PER * npair, f8, d), F32),
        scratch_shapes=[pltpu.VMEM((fw, d), F32)],
        compiler_params=_params("parallel", "arbitrary"), name=name)(a, dm)


GELU_C = math.sqrt(2.0 / math.pi)
GELU_A = 0.044715


def _gelu(x):
    return 0.5 * x * (1.0 + jnp.tanh(GELU_C * (x + GELU_A * x * x * x)))


def _gelu_grad(x):
    th = jnp.tanh(GELU_C * (x + GELU_A * x * x * x))
    return 0.5 * (1.0 + th) + 0.5 * x * (1.0 - th * th) * GELU_C * (1.0 + 3.0 * GELU_A * x * x)


def _neg_expm1(x):
    series = x * (1.0 + x * (0.5 + x * (1.0 / 6.0 + x * (1.0 / 24.0 + x * (1.0 / 120.0 + x * (1.0 / 720.0))))))
    return -jnp.where(x > -0.25, series, jnp.exp(x) - 1.0)


CONV_TR = 256
CONV_TAPS = 4
HALO = 8


def _shifted(ext, k, tr, back):
    if back:
        return pltpu.roll(ext, k, 0)[HALO:, :] if k else ext[HALO:, :]
    return pltpu.roll(ext, tr + HALO - k, 0)[:tr, :] if k else ext[:tr, :]


def conv4_fwd(name, src, cb, w, b):
    t, c = src.shape[0], w.shape[1]
    tr = _row_tile(t, CONV_TR)
    hb = tr // HALO

    def body(x_ref, h_ref, w_ref, b_ref, o_ref):
        i = pl.program_id(0)
        ext = jnp.concatenate([jnp.where(i == 0, 0.0, h_ref[...]), x_ref[...]], axis=0)
        acc = b_ref[...]
        for k in range(CONV_TAPS):
            acc = acc + w_ref[CONV_TAPS - 1 - k:CONV_TAPS - k, :] * _shifted(ext, k, tr, True)
        o_ref[...] = acc

    return pl.pallas_call(
        body, grid=(t // tr,),
        in_specs=[pl.BlockSpec((tr, c), lambda i: (i, cb)),
                  pl.BlockSpec((HALO, c), lambda i: (jnp.maximum(i * hb - 1, 0), cb)),
                  pl.BlockSpec((CONV_TAPS, c), lambda i: (0, 0)), pl.BlockSpec((1, c), lambda i: (0, 0))],
        out_specs=pl.BlockSpec((tr, c), lambda i: (i, 0)), out_shape=_sds((t, c), F32),
        compiler_params=_params("parallel"), name=name)(src, src, w, b)


def conv4_bwd_x(name, dy, w):
    t, c = dy.shape
    tr = _row_tile(t, CONV_TR)
    hb = tr // HALO
    last = t // tr - 1

    def body(y_ref, h_ref, w_ref, o_ref):
        i = pl.program_id(0)
        ext = jnp.concatenate([y_ref[...], jnp.where(i == last, 0.0, h_ref[...])], axis=0)
        acc = w_ref[CONV_TAPS - 1:CONV_TAPS, :] * y_ref[...]
        for k in range(1, CONV_TAPS):
            acc = acc + w_ref[CONV_TAPS - 1 - k:CONV_TAPS - k, :] * _shifted(ext, k, tr, False)
        o_ref[...] = acc

    return pl.pallas_call(
        body, grid=(t // tr,),
        in_specs=[pl.BlockSpec((tr, c), lambda i: (i, 0)),
                  pl.BlockSpec((HALO, c), lambda i: (jnp.minimum((i + 1) * hb, t // HALO - 1), 0)),
                  pl.BlockSpec((CONV_TAPS, c), lambda i: (0, 0))],
        out_specs=pl.BlockSpec((tr, c), lambda i: (i, 0)), out_shape=_sds((t, c), F32),
        compiler_params=_params("parallel"), name=name)(dy, dy, w)


def conv4_bwd_w(name, src, cb, dy):
    t, c = dy.shape
    tr = _row_tile(t, CONV_TR)
    hb = tr // HALO

    def body(x_ref, h_ref, dy_ref, dw_ref, db_ref):
        i = pl.program_id(0)

        @pl.when(i == 0)
        def _():
            dw_ref[...] = jnp.zeros_like(dw_ref)
            db_ref[...] = jnp.zeros_like(db_ref)

        ext = jnp.concatenate([jnp.where(i == 0, 0.0, h_ref[...]), x_ref[...]], axis=0)
        dyv = dy_ref[...]
        db_ref[...] += jnp.sum(dyv, axis=0, keepdims=True)
        for k in range(CONV_TAPS):
            dw_ref[CONV_TAPS - 1 - k:CONV_TAPS - k, :] += jnp.sum(dyv * _shifted(ext, k, tr, True), axis=0,
                                                                   keepdims=True)

    return pl.pallas_call(
        body, grid=(t // tr,),
        in_specs=[pl.BlockSpec((tr, c), lambda i: (i, cb)),
                  pl.BlockSpec((HALO, c), lambda i: (jnp.maximum(i * hb - 1, 0), cb)),
                  pl.BlockSpec((tr, c), lambda i: (i, 0))],
        out_specs=[pl.BlockSpec((CONV_TAPS, c), lambda i: (0, 0)), pl.BlockSpec((1, c), lambda i: (0, 0))],
        out_shape=[_sds((CONV_TAPS, c), F32), _sds((1, c), F32)],
        compiler_params=_params("arbitrary"), name=name)(src, src, dy)


def _rg_gate_values(xcv, wa_ref, wi_ref, ba_ref, bi_ref, lam_ref):
    xb = xcv.astype(BF16)
    r = _sigmoid(jnp.dot(xb, wa_ref[...], preferred_element_type=F32) + ba_ref[...])
    ig = _sigmoid(jnp.dot(xb, wi_ref[...], preferred_element_type=F32) + bi_ref[...])
    lam = lam_ref[...]
    sp = jnp.maximum(-lam, 0.0) + jnp.log(1.0 + jnp.exp(-jnp.abs(lam)))
    log_a = -LRU_C * r * sp
    a = jnp.exp(log_a)
    mult = jnp.sqrt(_neg_expm1(2.0 * log_a))
    return xb, r, ig, sp, a, mult


def rg_gates_fwd(name, xc, wa, wi, ba, bi, lam):
    t, c = xc.shape
    nb, cb, _ = wa.shape
    tm = _row_tile(t, 512)

    def body(xc_ref, wa_ref, wi_ref, ba_ref, bi_ref, lam_ref, a_ref, u_ref):
        for n in range(nb):
            cols = pl.ds(n * cb, cb)
            xcv = xc_ref[:, cols]
            _, _, ig, _, a, mult = _rg_gate_values(xcv, wa_ref.at[n], wi_ref.at[n], ba_ref.at[:, cols],
                                                   bi_ref.at[:, cols], lam_ref.at[:, cols])
            a_ref[:, cols] = a
            u_ref[:, cols] = mult * (ig * xcv)

    row = pl.BlockSpec((tm, c), lambda i: (i, 0))
    wsp = pl.BlockSpec((nb, cb, cb), lambda i: (0, 0, 0))
    vec = pl.BlockSpec((1, c), lambda i: (0, 0))
    return pl.pallas_call(
        body, grid=(t // tm,), in_specs=[row, wsp, wsp, vec, vec, vec], out_specs=[row, row],
        out_shape=[_sds((t, c), F32)] * 2, compiler_params=_params("parallel"), name=name,
    )(xc, wa, wi, ba, bi, lam)


def rg_gates_bwd(name, xc, gu, hs, wa, wi, ba, bi, lam):
    t, c = xc.shape
    nb, cb, _ = wa.shape
    tm = _row_tile(t, 512)
    hb = tm // HALO

    def body(xc_ref, gu_ref, h_ref, halo_ref, wa_ref, wi_ref, ba_ref, bi_ref, lam_ref,
             dxc_ref, dwa_ref, dwi_ref, dba_ref, dbi_ref, dlam_ref):
        i = pl.program_id(0)

        @pl.when(i == 0)
        def _():
            for ref in (dwa_ref, dwi_ref, dba_ref, dbi_ref, dlam_ref):
                ref[...] = jnp.zeros_like(ref)

        for n in range(nb):
            cols = pl.ds(n * cb, cb)
            hprev = _shifted(jnp.concatenate([jnp.where(i == 0, 0.0, halo_ref[:, cols]), h_ref[:, cols]], axis=0),
                             1, tm, True)
            xcv = xc_ref[:, cols]
            wa_n, wi_n, lam_n = wa_ref.at[n], wi_ref.at[n], lam_ref.at[:, cols]
            xb, r, ig, sp, a, mult = _rg_gate_values(xcv, wa_n, wi_n, ba_ref.at[:, cols], bi_ref.at[:, cols], lam_n)
            gv = gu_ref[:, cols]
            d_ixc = gv * mult
            d_i = d_ixc * xcv
            d_mult = gv * ig * xcv
            d_a = gv * hprev - d_mult * a / mult
            d_log_a = d_a * a
            d_r = d_log_a * (-LRU_C * sp)
            sig_neg_lam = 1.0 / (1.0 + jnp.exp(lam_n[...]))
            dlam_ref[:, cols] += jnp.sum(d_log_a * r, axis=0, keepdims=True) * (LRU_C * sig_neg_lam)
            dpa = d_r * r * (1.0 - r)
            dpi = d_i * ig * (1.0 - ig)
            dba_ref[:, cols] += jnp.sum(dpa, axis=0, keepdims=True)
            dbi_ref[:, cols] += jnp.sum(dpi, axis=0, keepdims=True)
            dpab, dpib = dpa.astype(BF16), dpi.astype(BF16)
            dxc_ref[:, cols] = (d_ixc * ig + lax.dot_general(dpab, wa_n[...], NT, preferred_element_type=F32)
                                + lax.dot_general(dpib, wi_n[...], NT, preferred_element_type=F32))
            dwa_ref[n] += lax.dot_general(xb, dpab, TN, preferred_element_type=F32)
            dwi_ref[n] += lax.dot_general(xb, dpib, TN, preferred_element_type=F32)

    row = pl.BlockSpec((tm, c), lambda i: (i, 0))
    wsp = pl.BlockSpec((nb, cb, cb), lambda i: (0, 0, 0))
    vec = pl.BlockSpec((1, c), lambda i: (0, 0))
    halo = pl.BlockSpec((HALO, c), lambda i: (jnp.maximum(i * hb - 1, 0), 0))
    return pl.pallas_call(
        body, grid=(t // tm,), in_specs=[row, row, row, halo, wsp, wsp, vec, vec, vec],
        out_specs=[row, wsp, wsp, vec, vec, vec],
        out_shape=[_sds((t, c), F32), _sds((nb, cb, cb), F32), _sds((nb, cb, cb), F32),
                   _sds((1, c), F32), _sds((1, c), F32), _sds((1, c), F32)],
        compiler_params=_params("arbitrary"), name=name)(xc, gu, hs, hs, wa, wi, ba, bi, lam)


SCAN_TS = 256
SCAN_TC = 1024


def _tile_scan(a, b, reverse):
    ts = a.shape[0]
    row = lax.broadcasted_iota(jnp.int32, a.shape, 0)
    d = 1
    while d < ts:
        if reverse:
            inside = row < ts - d
            a_sh = jnp.where(inside, pltpu.roll(a, ts - d, 0), 1.0)
            b_sh = jnp.where(inside, pltpu.roll(b, ts - d, 0), 0.0)
        else:
            inside = row >= d
            a_sh = jnp.where(inside, pltpu.roll(a, d, 0), 1.0)
            b_sh = jnp.where(inside, pltpu.roll(b, d, 0), 0.0)
        b = b + a * b_sh
        a = a * a_sh
        d *= 2
    return a, b


def rg_scan_fwd(name, a, u, gate_pre):
    t, c = a.shape
    ts, tc = _row_tile(t, SCAN_TS), _row_tile(c, SCAN_TC)

    def body(a_ref, u_ref, g_ref, h_ref, z_ref, carry_ref):
        s = pl.program_id(1)

        @pl.when(s == 0)
        def _():
            carry_ref[...] = jnp.zeros_like(carry_ref)

        ac, bc = _tile_scan(a_ref[...], u_ref[...], False)
        h = bc + ac * carry_ref[0:1, :]
        h_ref[...] = h
        z_ref[...] = (h * _gelu(g_ref[...])).astype(BF16)
        carry_ref[0:1, :] = h[ts - 1:ts, :]

    blk = pl.BlockSpec((ts, tc), lambda j, s: (s, j))
    return pl.pallas_call(
        body, grid=(c // tc, t // ts), in_specs=[blk, blk, blk], out_specs=[blk, blk],
        out_shape=[_sds((t, c), F32), _sds((t, c), BF16)], scratch_shapes=[pltpu.VMEM((8, tc), F32)],
        compiler_params=_params("parallel", "arbitrary"), name=name)(a, u, gate_pre)


def rg_scan_bwd(name, a, hs, gate_pre, dz):
    t, c = hs.shape
    ts, tc = _row_tile(t, SCAN_TS), _row_tile(c, SCAN_TC)
    nt = t // ts
    hb = ts // HALO

    def body(a_ref, halo_ref, h_ref, g_ref, dz_ref, gu_ref, dgate_ref, carry_ref):
        s = pl.program_id(1)

        @pl.when(s == 0)
        def _():
            carry_ref[...] = jnp.zeros_like(carry_ref)

        a_next = _shifted(jnp.concatenate([a_ref[...], jnp.where(s == 0, 0.0, halo_ref[...])], axis=0), 1, ts, False)
        gate = g_ref[...]
        dzv = dz_ref[...]
        dgate_ref[...] = (dzv * h_ref[...] * _gelu_grad(gate)).astype(BF16)
        ac, bc = _tile_scan(a_next, dzv * _gelu(gate), True)
        gu = bc + ac * carry_ref[0:1, :]
        gu_ref[...] = gu
        carry_ref[0:1, :] = gu[0:1, :]

    blk = pl.BlockSpec((ts, tc), lambda j, s: (nt - 1 - s, j))
    halo = pl.BlockSpec((HALO, tc), lambda j, s: (jnp.minimum((nt - s) * hb, t // HALO - 1), j))
    return pl.pallas_call(
        body, grid=(c // tc, nt), in_specs=[blk, halo, blk, blk, blk], out_specs=[blk, blk],
        out_shape=[_sds((t, c), F32), _sds((t, c), BF16)], scratch_shapes=[pltpu.VMEM((8, tc), F32)],
        compiler_params=_params("parallel", "arbitrary"), name=name)(a, a, hs, gate_pre, dz)


QA_BLK, KA_BLK, VA_BLK, QS_BLK, KS_BLK, VS_BLK = (g * N_PAIRS for g in range(6))


TOEP_W = 640
TOEP_FLAT = 320
TABLE_LOW = 193


def rel_bias_matrix(name, table):
    h = table.shape[0]
    diag = jnp.concatenate([jnp.repeat(table[:, 2 * REL_CLIP:], TOEP_FLAT, axis=1),
                            jnp.flip(table[:, TABLE_LOW:2 * REL_CLIP], axis=1),
                            jnp.zeros((h, 1), table.dtype)], axis=1)[:, None, :]

    def body(v_ref, o_ref):
        rows = jnp.broadcast_to(v_ref[...], (CHUNK, TOEP_W))
        o_ref[...] = pltpu.roll(rows, TOEP_W - (CHUNK - 1), 1, stride=1, stride_axis=0)

    out = pl.pallas_call(
        body, grid=(h,), in_specs=[pl.BlockSpec((None, 1, TOEP_W), lambda hh: (hh, 0, 0))],
        out_specs=pl.BlockSpec((None, CHUNK, TOEP_W), lambda hh: (hh, 0, 0)),
        out_shape=_sds((h, CHUNK, TOEP_W), F32), compiler_params=_params("parallel"), name=name)(diag)
    return out[:, :, :BAND]


def rel_bias_grad(name, dbias):
    h = dbias.shape[0]
    flipped = jnp.pad(jnp.flip(dbias, axis=1), ((0, 0), (0, 0), (0, TOEP_W - BAND)))

    def body(x_ref, o_ref):
        skew = pltpu.roll(x_ref[...], 0, 1, stride=1, stride_axis=0)
        col = jnp.sum(skew, axis=0, keepdims=True)
        lane = lax.broadcasted_iota(jnp.int32, col.shape, 1)
        flat = jnp.sum(jnp.where(lane < TOEP_FLAT, col, 0.0), axis=1, keepdims=True)
        o_ref[...] = jnp.where(lane == TOEP_W - 1, flat, col)

    out = pl.pallas_call(
        body, grid=(h,), in_specs=[pl.BlockSpec((None, CHUNK, TOEP_W), lambda hh: (hh, 0, 0))],
        out_specs=pl.BlockSpec((None, 1, TOEP_W), lambda hh: (hh, 0, 0)),
        out_shape=_sds((h, 1, TOEP_W), F32), compiler_params=_params("parallel"), name=name)(flipped)[:, 0, :]
    return jnp.concatenate([jnp.zeros((h, TABLE_LOW), F32), jnp.flip(out[:, TOEP_FLAT:TOEP_W - 1], axis=1),
                            out[:, TOEP_W - 1:]], axis=1)


def attn_layer_fwd(tag, x, h, w, g_next):
    proj = mm_nn_wblk(tag + "_proj", h, w["w_in"], w["idx"], BF16)
    width = N_PAIRS * PAIR
    pad = lambda a: jnp.pad(a, ((PAD_KEYS, 0), (0, 0)))
    kap, vap = pad(proj[:, width:2 * width]), pad(proj[:, 2 * width:3 * width])
    bias = rel_bias_matrix(tag + "_bias", w["rel_bias"])
    oa, lse = attn_a_fwd(tag + "_a", proj, kap, vap, bias, QA_BLK)
    ob = sb_fwd(tag + "_sb", proj, QS_BLK, KS_BLK, VS_BLK)
    o = jnp.concatenate([oa, ob], axis=1).astype(BF16)
    m = mm_nn(tag + "_out", o, w["w_out"], F32)
    x1, h_next = resid_norm_fwd(tag + "_res", x, m, w["g_post"], g_next)
    return x1, h_next, (x, h, proj, kap, vap, bias, oa, lse, ob, o, m)


def attn_layer_bwd(tag, dm, dx1, saved, w, prev):
    x, h, proj, kap, vap, bias, oa, lse, ob, o, m = saved
    d_w_out = mm_tn(tag + "_dwout", o, dm, F32)
    do = mm_nt(tag + "_do", dm, w["w_out"], BF16)
    dqa, dkap, dvap, dbias = attn_a_bwd(tag + "_da", proj, kap, vap, bias, oa, lse, do, QA_BLK, 0)
    dqs, dks, dvs = sb_bwd(tag + "_dsb", proj, ob, do, QS_BLK, KS_BLK, VS_BLK, N_PAIRS)
    d_rel = rel_bias_grad(tag + "_dbias", dbias)
    dproj = jnp.concatenate([dqa, dkap[PAD_KEYS:], dvap[PAD_KEYS:], dqs, dks, dvs], axis=1).astype(BF16)
    d_w_in = mm_tn_oblk(tag + "_dwin", h, dproj, w["w_in"].shape[3], F32)
    dh = mm_nt_wblk(tag + "_dh", dproj, w["w_in"], w["idx"], F32)
    dx, dg_pre, dm_prev, dg_post_prev = close_bwd(tag, dh, x, w["g_pre"], dx1, prev)
    return dx, dm_prev, dg_post_prev, dict(w_in=d_w_in, w_out=d_w_out, rel_bias=d_rel, g_pre=dg_pre)


def rg_layer_fwd(tag, x, h, w, g_next):
    proj = mm_nn_wblk(tag + "_proj", h, w["w_in"], w["idx"], F32)
    xc = conv4_fwd(tag + "_conv", proj, 1, w["conv_w"], w["conv_b"])
    a, u = rg_gates_fwd(tag + "_gates", xc, w["w_a"], w["w_i"], w["b_a"], w["b_i"], w["lam"])
    hs, z = rg_scan_fwd(tag + "_scan", a, u, proj)
    m = mm_nn(tag + "_out", z, w["w_out"], F32)
    x1, h_next = resid_norm_fwd(tag + "_res", x, m, w["g_post"], g_next)
    return x1, h_next, (x, h, proj, xc, a, hs, z, m)


def rg_layer_bwd(tag, dm, dx1, saved, w, prev):
    x, h, proj, xc, a, hs, z, m = saved
    d_w_out = mm_tn(tag + "_dwout", z, dm, F32)
    dz = mm_nt(tag + "_dz", dm, w["w_out"], F32)
    gu, dgate = rg_scan_bwd(tag + "_dscan", a, hs, proj, dz)
    dxc, d_w_a, d_w_i, d_b_a, d_b_i, d_lam = rg_gates_bwd(
        tag + "_dgates", xc, gu, hs, w["w_a"], w["w_i"], w["b_a"], w["b_i"], w["lam"])
    d_conv_w, d_conv_b = conv4_bwd_w(tag + "_dconvw", proj, 1, dxc)
    dxr = conv4_bwd_x(tag + "_dconv", dxc, w["conv_w"])
    dproj = jnp.concatenate([dgate, dxr.astype(BF16)], axis=1)
    d_w_in = mm_tn_oblk(tag + "_dwin", h, dproj, w["w_in"].shape[3], F32)
    dh = mm_nt_wblk(tag + "_dh", dproj, w["w_in"], w["idx"], F32)
    dx, dg_pre, dm_prev, dg_post_prev = close_bwd(tag, dh, x, w["g_pre"], dx1, prev)
    return dx, dm_prev, dg_post_prev, dict(w_in=d_w_in, w_out=d_w_out, conv_w=d_conv_w, conv_b=d_conv_b, w_a=d_w_a,
                                           w_i=d_w_i, b_a=d_b_a, b_i=d_b_i, lam=d_lam, g_pre=dg_pre)


def ffn_layer_fwd(tag, x, h, w, g_next):
    f8 = w["w_down"].shape[2]
    gu, a = ffn_up(tag + "_up", h, w["w_gu"], f8, w["idx"])
    f = ffn_down(tag + "_down", a, w["w_down"], f8, w["idx"])
    x1, h_next = resid_norm_fwd(tag + "_res", x, f, w["g_post"], g_next)
    return x1, h_next, (x, h, gu, a, f)


def ffn_layer_bwd(tag, dm, dx1, saved, w, prev):
    x, h, gu, a, f = saved
    f8 = w["w_down"].shape[2]
    d_w_down = ffn_dw_down(tag + "_dwdown", a, dm, f8)
    dgu = ffn_bwd_act(tag + "_dact", dm, w["w_down"], gu, f8, w["idx"])
    d_w_gu = ffn_dw_in(tag + "_dwgu", h, dgu)
    dh = ffn_bwd_dh(tag + "_dh", dgu, w["w_gu"], w["idx"])
    dx, dg_pre, dm_prev, dg_post_prev = close_bwd(tag, dh, x, w["g_pre"], dx1, prev)
    return dx, dm_prev, dg_post_prev, dict(w_gu=d_w_gu, w_down=d_w_down, g_pre=dg_pre)


def _place():
    return lax.axis_index("x"), lax.axis_index("y"), lax.axis_index("c")


def all_gather(name, blks):
    n = len(blks)

    def body(*refs):
        x_refs, out_refs = refs[:n], refs[n:2 * n]
        send_sems, recv_sems, local_sems = refs[2 * n:]
        x, y, cc = _place()
        me, sibling = (x, y, cc), (x, y, 1 - cc)
        chips = [(1 - x, y), (x, 1 - y), (1 - x, 1 - y)]
        south = cc == 0
        via = (jnp.where(south, 1 - x, x), jnp.where(south, y, 1 - y))
        onward = (jnp.where(south, x, 1 - x), jnp.where(south, 1 - y, y))
        k_via, k_onward = 1 + cc, 2 - cc

        def slot(a, px, py, pc):
            return out_refs[a].at[4 * px + 2 * py + pc]

        def copy(a, k, block, to, src=None):
            return pltpu.make_async_remote_copy(
                src_ref=slot(a, *block) if src is None else src, dst_ref=slot(a, *block),
                send_sem=send_sems.at[7 * a + k], recv_sem=recv_sems.at[7 * a + k], device_id=to, device_id_type=MESH)

        mine = [pltpu.make_async_copy(x_refs[a], slot(a, *me), local_sems.at[a]) for a in range(n)]
        sends = []
        for a in range(n):
            mine[a].start()
            sends.append(copy(a, 0, me, sibling, src=x_refs[a]))
            sends += [copy(a, 1 + j, me, (*chips[j], cc), src=x_refs[a]) for j in range(2)]
        for cp in sends:
            cp.start()
        for a in range(n):
            copy(a, k_via, (*via, cc), me).wait_recv()
            sends.append(copy(a, 3, (*via, cc), (*onward, cc)))
            sends.append(copy(a, 3 + k_via, (*via, cc), sibling))
            sends[-2].start()
            sends[-1].start()
        for a in range(n):
            copy(a, k_onward, (*onward, cc), me).wait_recv()
            sends.append(copy(a, 3 + k_onward, (*onward, cc), sibling))
            sends[-1].start()
        for a in range(n):
            copy(a, 3, (*chips[2], cc), me).wait_recv()
            sends.append(copy(a, 6, (*chips[2], cc), sibling))
            sends[-1].start()
        for a in range(n):
            copy(a, 0, sibling, me).wait_recv()
            for j, chip in enumerate(chips):
                copy(a, 4 + j, (*chip, 1 - cc), me).wait_recv()
        for cp in sends:
            cp.wait_send()
        for cp in mine:
            cp.wait()

    return pl.pallas_call(
        body, out_shape=[_sds((N_DEV,) + b.shape, b.dtype) for b in blks], in_specs=[ANY] * n, out_specs=[ANY] * n,
        scratch_shapes=[pltpu.SemaphoreType.DMA((7 * n,)), pltpu.SemaphoreType.DMA((7 * n,)),
                        pltpu.SemaphoreType.DMA((n,))],
        name=name)(*blks)


def exchange_pair(name, gs):
    n = len(gs)
    nchip = 4

    def body(*refs):
        g_refs, land_refs = refs[:n], refs[n:2 * n]
        send_sems, recv_sems = refs[2 * n:]
        x, y, cc = _place()
        copies = [pltpu.make_async_remote_copy(
            src_ref=g_refs[a].at[j, 1 - cc], dst_ref=land_refs[a].at[j], send_sem=send_sems.at[nchip * a + j],
            recv_sem=recv_sems.at[nchip * a + j], device_id=(x, y, 1 - cc), device_id_type=MESH)
            for a in range(n) for j in range(nchip)]
        for cp in copies:
            cp.start()
        for cp in copies:
            cp.wait()

    return pl.pallas_call(
        body, out_shape=[_sds((nchip,) + g.shape[2:], g.dtype) for g in gs], in_specs=[ANY] * n, out_specs=[ANY] * n,
        scratch_shapes=[pltpu.SemaphoreType.DMA((nchip * n,)), pltpu.SemaphoreType.DMA((nchip * n,))],
        name=name)(*gs)


def pair_sum(name, g, land, core, out_dtype):
    nchip, _, r, c = g.shape
    tr = _divisor_tile(r, 1024, 16)

    def body(core_ref, g_ref, l_ref, o_ref):
        o_ref[...] = (g_ref[...] + l_ref[...]).astype(o_ref.dtype)

    return pl.pallas_call(
        body,
        grid_spec=pltpu.PrefetchScalarGridSpec(
            num_scalar_prefetch=1, grid=(nchip, r // tr),
            in_specs=[pl.BlockSpec((None, None, tr, c), lambda j, i, core_ref: (j, core_ref[0], i, 0)),
                      pl.BlockSpec((None, tr, c), lambda j, i, core_ref: (j, i, 0))],
            out_specs=pl.BlockSpec((None, tr, c), lambda j, i, core_ref: (j, i, 0))),
        out_shape=_sds((nchip, r, c), out_dtype), compiler_params=_params("parallel", "parallel"), name=name,
    )(core, g, land)


def exchange_chips(name, ps):
    n = len(ps)

    def body(*refs):
        p_refs, land_refs = refs[:n], refs[n:2 * n]
        send_sems, recv_sems, local_sems = refs[2 * n:]
        x, y, cc = _place()
        mine = 2 * x + y
        chips = [(1 - x, y), (x, 1 - y), (1 - x, 1 - y)]
        own = [pltpu.make_async_copy(p_refs[a].at[mine], land_refs[a].at[mine], local_sems.at[a]) for a in range(n)]
        for cp in own:
            cp.start()
        sends = [pltpu.make_async_remote_copy(
            src_ref=p_refs[a].at[2 * px + py], dst_ref=land_refs[a].at[mine], send_sem=send_sems.at[3 * a + k],
            recv_sem=recv_sems.at[3 * a + k], device_id=(px, py, cc), device_id_type=MESH)
            for a in range(n) for k, (px, py) in enumerate(chips)]
        for cp in sends:
            cp.start()
        for a in range(n):
            for k, (px, py) in enumerate(chips):
                pltpu.make_async_remote_copy(
                    src_ref=p_refs[a].at[mine], dst_ref=land_refs[a].at[2 * px + py], send_sem=send_sems.at[3 * a + k],
                    recv_sem=recv_sems.at[3 * a + k], device_id=(px, py, cc), device_id_type=MESH).wait_recv()
        for cp in sends:
            cp.wait_send()
        for cp in own:
            cp.wait()

    return pl.pallas_call(
        body, out_shape=[_sds(p.shape, p.dtype) for p in ps], in_specs=[ANY] * n, out_specs=[ANY] * n,
        scratch_shapes=[pltpu.SemaphoreType.DMA((3 * n,)), pltpu.SemaphoreType.DMA((3 * n,)),
                        pltpu.SemaphoreType.DMA((n,))],
        name=name)(*ps)


def adamw(name, parts, w, m, v):
    npart, r, c = parts.shape
    tr = _divisor_tile(r, 512, 16)
    c1 = 1.0 / (1.0 - ADAM_B1 ** ADAM_STEP)
    c2 = 1.0 / (1.0 - ADAM_B2 ** ADAM_STEP)

    def body(p_ref, w_ref, m_ref, v_ref, g_ref, d_ref, nm_ref, nv_ref):
        g = p_ref[0].astype(F32)
        for j in range(1, npart):
            g = g + p_ref[j].astype(F32)
        nm = ADAM_B1 * m_ref[...] + (1.0 - ADAM_B1) * g
        nv = ADAM_B2 * v_ref[...] + (1.0 - ADAM_B2) * (g * g)
        g_ref[...] = g
        nm_ref[...] = nm
        nv_ref[...] = nv
        d_ref[...] = -ADAM_LR * ((nm * c1) / (jnp.sqrt(nv * c2) + ADAM_EPS) + ADAM_WD * w_ref[...])

    row = pl.BlockSpec((tr, c), lambda i: (i, 0))
    return pl.pallas_call(
        body, grid=(r // tr,), in_specs=[pl.BlockSpec((npart, tr, c), lambda i: (0, i, 0)), row, row, row],
        out_specs=[row] * 4, out_shape=[_sds((r, c), F32)] * 4, compiler_params=_params("parallel"), name=name,
    )(parts, w, m, v)


def _pack(arrays, dtype, row_multiple):
    flat = jnp.concatenate([a.astype(dtype).reshape(-1) for a in arrays])
    per = row_multiple * LANES
    total = -(-flat.shape[0] // per) * per
    return jnp.pad(flat, (0, total - flat.shape[0])).reshape(total // LANES, LANES)


def _pack_blocked(arrays, dtype, row_multiple):
    flat = jnp.concatenate([a.astype(dtype).reshape(N_DEV, -1) for a in arrays], axis=1)
    per = row_multiple * LANES
    total = -(-flat.shape[1] // per) * per
    return jnp.pad(flat, ((0, 0), (0, total - flat.shape[1]))).reshape(N_DEV, total // LANES, LANES)


def _unpack(buf, shapes, lead=()):
    flat = buf.reshape(lead + (-1,))
    out, off = [], 0
    for s in shapes:
        n = math.prod(s)
        out.append(flat[..., off:off + n].reshape(lead + tuple(s)))
        off += n
    return out


def _to_blocked(full, ax):
    s = full.shape
    return jnp.moveaxis(full.reshape(s[:ax] + (N_DEV, s[ax] // N_DEV) + s[ax + 1:]), ax, 0)


def _from_blocked(blk, ax):
    moved = jnp.moveaxis(blk, 0, ax)
    s = moved.shape
    return moved.reshape(s[:ax] + (s[ax] * s[ax + 1],) + s[ax + 2:])


SMALL = ("rg_conv_w", "rg_conv_b", "rg_b_a", "rg_b_i", "rg_lambda")
GU = "ffn_w_gu"
BIG = ("attn_w_in", "attn_w_out", "rg_w_in", "rg_w_a", "rg_w_i", "rg_w_out", GU, "ffn_w_down")


def kernel(x, attn_w_in, attn_rel_bias, attn_w_out, rg_w_in, rg_conv_w, rg_conv_b, rg_w_a, rg_b_a, rg_w_i, rg_b_i, rg_lambda, rg_w_out, norm_mix_pre, norm_mix_post, norm_ffn_pre, norm_ffn_post, ffn_w_gate, ffn_w_up, ffn_w_down, loss_target, m_attn_w_in, m_attn_rel_bias, m_attn_w_out, m_rg_w_in, m_rg_conv_w, m_rg_conv_b, m_rg_w_a, m_rg_b_a, m_rg_w_i, m_rg_b_i, m_rg_lambda, m_rg_w_out, m_norm_mix_pre, m_norm_mix_post, m_norm_ffn_pre, m_norm_ffn_post, m_ffn_w_gate, m_ffn_w_up, m_ffn_w_down, v_attn_w_in, v_attn_rel_bias, v_attn_w_out, v_rg_w_in, v_rg_conv_w, v_rg_conv_b, v_rg_w_a, v_rg_b_a, v_rg_w_i, v_rg_b_i, v_rg_lambda, v_rg_w_out, v_norm_mix_pre, v_norm_mix_post, v_norm_ffn_pre, v_norm_ffn_post, v_ffn_w_gate, v_ffn_w_up, v_ffn_w_down):
    w_loc = dict(attn_w_in=attn_w_in, attn_rel_bias=attn_rel_bias, attn_w_out=attn_w_out, rg_w_in=rg_w_in,
                 rg_conv_w=rg_conv_w, rg_conv_b=rg_conv_b, rg_w_a=rg_w_a, rg_b_a=rg_b_a, rg_w_i=rg_w_i, rg_b_i=rg_b_i,
                 rg_lambda=rg_lambda, rg_w_out=rg_w_out, norm_mix_pre=norm_mix_pre, norm_mix_post=norm_mix_post,
                 norm_ffn_pre=norm_ffn_pre, norm_ffn_post=norm_ffn_post, ffn_w_gate=ffn_w_gate, ffn_w_up=ffn_w_up,
                 ffn_w_down=ffn_w_down)
    m_loc = dict(attn_w_in=m_attn_w_in, attn_rel_bias=m_attn_rel_bias, attn_w_out=m_attn_w_out, rg_w_in=m_rg_w_in,
                 rg_conv_w=m_rg_conv_w, rg_conv_b=m_rg_conv_b, rg_w_a=m_rg_w_a, rg_b_a=m_rg_b_a, rg_w_i=m_rg_w_i,
                 rg_b_i=m_rg_b_i, rg_lambda=m_rg_lambda, rg_w_out=m_rg_w_out, norm_mix_pre=m_norm_mix_pre,
                 norm_mix_post=m_norm_mix_post, norm_ffn_pre=m_norm_ffn_pre, norm_ffn_post=m_norm_ffn_post,
                 ffn_w_gate=m_ffn_w_gate, ffn_w_up=m_ffn_w_up, ffn_w_down=m_ffn_w_down)
    v_loc = dict(attn_w_in=v_attn_w_in, attn_rel_bias=v_attn_rel_bias, attn_w_out=v_attn_w_out, rg_w_in=v_rg_w_in,
                 rg_conv_w=v_rg_conv_w, rg_conv_b=v_rg_conv_b, rg_w_a=v_rg_w_a, rg_b_a=v_rg_b_a, rg_w_i=v_rg_w_i,
                 rg_b_i=v_rg_b_i, rg_lambda=v_rg_lambda, rg_w_out=v_rg_w_out, norm_mix_pre=v_norm_mix_pre,
                 norm_mix_post=v_norm_mix_post, norm_ffn_pre=v_norm_ffn_pre, norm_ffn_post=v_norm_ffn_post,
                 ffn_w_gate=v_ffn_w_gate, ffn_w_up=v_ffn_w_up, ffn_w_down=v_ffn_w_down)
    axis_of = dict(SHARDED)
    xt, target = x[0], loss_target[0]
    d_model = xt.shape[1]
    rows2d = lambda a: a.reshape(-1, a.shape[-1])
    small_shapes = [w_loc[n].shape for n in SMALL]
    f8 = ffn_w_gate.shape[-1]
    for d in (w_loc, m_loc, v_loc):
        d[GU] = merge_gu(d["ffn_w_gate"], d["ffn_w_up"])

    gathered = all_gather("gather_weights", [rows2d(w_loc[n]).astype(BF16) for n in BIG]
                          + [_pack([w_loc[n] for n in SMALL], F32, 8)])
    blocked = {n: g.reshape((N_DEV,) + w_loc[n].shape) for n, g in zip(BIG, gathered)}
    blocked.update(zip(SMALL, _unpack(gathered[-1], small_shapes, (N_DEV,))))
    full = {n: _from_blocked(blocked[n], axis_of[n]) for n in SMALL}
    row = lambda a: a.reshape(1, -1).astype(F32)
    square = lambda rows8: rows8.reshape(-1, rows8.shape[-1])
    gates = lambda g: jnp.swapaxes(g, 0, 1).reshape(LRU_BLOCKS, -1, g.shape[-1])

    def layer_weights(layer):
        j = layer // 2
        norms = dict(g_pre=row(norm_mix_pre[layer]), g_post=row(norm_mix_post[layer]), idx=j)
        if layer % 2 == 0:
            mix = dict(w_in=blocked["attn_w_in"], w_out=square(blocked["attn_w_out"][:, j]),
                       rel_bias=attn_rel_bias[j], **norms)
        else:
            mix = dict(w_in=blocked["rg_w_in"], w_out=square(blocked["rg_w_out"][:, j]),
                       conv_w=full["rg_conv_w"][j][:, 0, :], conv_b=row(full["rg_conv_b"][j]),
                       w_a=gates(blocked["rg_w_a"][:, j]), w_i=gates(blocked["rg_w_i"][:, j]),
                       b_a=row(full["rg_b_a"][j]), b_i=row(full["rg_b_i"][j]), lam=row(full["rg_lambda"][j]), **norms)
        ffn = dict(w_gu=blocked[GU], w_down=blocked["ffn_w_down"], idx=layer,
                   g_pre=row(norm_ffn_pre[layer]), g_post=row(norm_ffn_post[layer]))
        return mix, ffn

    weights = [layer_weights(layer) for layer in range(DEPTH)]
    act, tape = xt, []
    h = rmsnorm_fwd("l0_mix_norm", act, weights[0][0]["g_pre"])
    for layer in range(DEPTH):
        mix_w, ffn_w = weights[layer]
        mixer_fwd = attn_layer_fwd if layer % 2 == 0 else rg_layer_fwd
        act, h, saved_mix = mixer_fwd(f"l{layer}_mix", act, h, mix_w, ffn_w["g_pre"])
        g_next = weights[layer + 1][0]["g_pre"] if layer + 1 < DEPTH else None
        act, h, saved_ffn = ffn_layer_fwd(f"l{layer}_ffn", act, h, ffn_w, g_next)
        tape.append((mix_w, ffn_w, saved_mix, saved_ffn))
    dact, sq = loss_grad("loss", act, target)
    loss_part = (0.5 * jnp.sum(sq) / d_model).reshape(1)

    grads = {}
    last = tape[DEPTH - 1]
    dm, grads[("ffn_post", DEPTH - 1)] = norm_bwd(f"l{DEPTH - 1}_ffn_dpost", dact, last[3][-1], last[1]["g_post"],
                                                 None, BF16)
    for layer in reversed(range(DEPTH)):
        mix_w, ffn_w, saved_mix, saved_ffn = tape[layer]
        dact, dm, grads[("mix_post", layer)], grads[("ffn", layer)] = ffn_layer_bwd(
            f"l{layer}_ffn", dm, dact, saved_ffn, ffn_w, (saved_mix[-1], mix_w["g_post"]))
        mixer_bwd = attn_layer_bwd if layer % 2 == 0 else rg_layer_bwd
        prev = (tape[layer - 1][3][-1], tape[layer - 1][1]["g_post"]) if layer else None
        dact, dm, grads[("ffn_post", layer - 1)], grads[("mix", layer)] = mixer_bwd(
            f"l{layer}_mix", dm, dact, saved_mix, mix_w, prev)
    attn_g = [grads[("mix", l)] for l in range(0, DEPTH, 2)]
    rg_g = [grads[("mix", l)] for l in range(1, DEPTH, 2)]
    ffn_g = [grads[("ffn", l)] for l in range(DEPTH)]
    stack = lambda gs, key: jnp.stack([g[key] for g in gs])
    by_owner = lambda gs, key, f: jnp.stack([f(g[key]) for g in gs], axis=1)
    rows8 = lambda a: a.reshape(N_DEV, -1, a.shape[-1])
    ungates = lambda a: jnp.swapaxes(a.reshape(LRU_BLOCKS, N_DEV, -1, a.shape[-1]), 0, 1)
    same = lambda a: a
    blocked_g = dict(
        attn_w_in=by_owner(attn_g, "w_in", same), attn_w_out=by_owner(attn_g, "w_out", rows8),
        rg_w_in=by_owner(rg_g, "w_in", same), rg_w_out=by_owner(rg_g, "w_out", rows8),
        rg_w_a=by_owner(rg_g, "w_a", ungates), rg_w_i=by_owner(rg_g, "w_i", ungates),
        **{GU: by_owner(ffn_g, "w_gu", same)},
        ffn_w_down=by_owner(ffn_g, "w_down", same))
    contrib = dict(
        attn_rel_bias=stack(attn_g, "rel_bias"), rg_conv_w=stack(rg_g, "conv_w")[:, :, None, :],
        rg_conv_b=stack(rg_g, "conv_b")[:, 0], rg_b_a=stack(rg_g, "b_a").reshape(rg_b_a.shape[0], LRU_BLOCKS, -1),
        rg_b_i=stack(rg_g, "b_i").reshape(rg_b_i.shape[0], LRU_BLOCKS, -1), rg_lambda=stack(rg_g, "lam")[:, 0],
        norm_mix_pre=jnp.concatenate([grads[("mix", l)]["g_pre"] for l in range(DEPTH)]),
        norm_mix_post=jnp.concatenate([grads[("mix_post", l)] for l in range(DEPTH)]),
        norm_ffn_pre=jnp.concatenate([g["g_pre"] for g in ffn_g]),
        norm_ffn_post=jnp.concatenate([grads[("ffn_post", l)] for l in range(DEPTH)]),
    )
    small_g = _pack_blocked([_to_blocked(contrib[n], axis_of[n]) for n in SMALL], F32, 8)

    slabs = [blocked_g[n].reshape(4, 2, -1, blocked_g[n].shape[-1]) for n in BIG] + [small_g.reshape(4, 2, -1, LANES)]
    core = lax.axis_index("c").astype(jnp.int32).reshape(1)
    from_sibling = exchange_pair("rs_pair", slabs)
    pairs = [pair_sum(f"rs_pair_sum_{i}", g, l, core, BF16 if i < len(BIG) else F32)
             for i, (g, l) in enumerate(zip(slabs, from_sibling))]
    by_chip = exchange_chips("rs_chips", pairs)
    result = {}
    kinds = ("grad", "delta", "new_m", "new_v")
    for n, parts in zip(BIG, by_chip):
        outs = adamw("adamw_" + n, parts, *[rows2d(d[n]) for d in (w_loc, m_loc, v_loc)])
        for kind, a in zip(kinds, outs):
            result[(kind, n)] = a.reshape(w_loc[n].shape)
    for kind in kinds:
        result[(kind, "ffn_w_gate")], result[(kind, "ffn_w_up")] = split_gu(result.pop((kind, GU)), f8)
    outs = adamw("adamw_small", by_chip[-1], *[_pack([d[n] for n in SMALL], F32, 8) for d in (w_loc, m_loc, v_loc)])
    for kind, buf in zip(kinds, outs):
        result.update({(kind, n): a for n, a in zip(SMALL, _unpack(buf, small_shapes))})
    rep_shapes = [w_loc[n].shape for n in REPLICATED] + [(1,)]
    rep_parts, = all_gather("gather_rep_grads", [_pack([contrib[n] for n in REPLICATED] + [loss_part], F32, 8)])
    outs = adamw("adamw_replicated", rep_parts, *[_pack([d[n] for n in REPLICATED] + [jnp.zeros((1,), F32)], F32, 8)
                                                  for d in (w_loc, m_loc, v_loc)])
    for kind, buf in zip(kinds, outs):
        result.update({(kind, n): a for n, a in zip(REPLICATED + ("loss",), _unpack(buf, rep_shapes))})
    loss = result[("grad", "loss")][0]
    return (loss, dact[None], *[result[(kind, n)] for kind in kinds for n in WEIGHTS])
```

```python
import math

import jax
import jax.numpy as jnp
from jax import lax
from jax.experimental import pallas as pl
from jax.experimental.pallas import tpu as pltpu

F32 = jnp.float32
BF16 = jnp.bfloat16

N_DEV = 8
DEPTH = 4
CHUNK = 64
N_LEFT = 8
BAND = (N_LEFT + 1) * CHUNK
PAD_KEYS = N_LEFT * CHUNK
HEAD_DIM = 64
N_HEADS = 8
REL_CLIP = 256
LRU_BLOCKS = 4
LRU_C = 8.0
RMS_EPS = 1e-6
QK_SCALE = HEAD_DIM ** -0.5

ADAM_LR = 0.001
ADAM_B1 = 0.9
ADAM_B2 = 0.999
ADAM_EPS = 1e-08
ADAM_WD = 0.01
ADAM_STEP = 10

LANES = 1024
V7X_VMEM_LIMIT = 56 * 1024 * 1024

MESH = pl.DeviceIdType.MESH
ANY = pl.BlockSpec(memory_space=pl.ANY)

SHARDED = (
    ("attn_w_in", 2), ("attn_w_out", 1), ("rg_w_in", 2), ("rg_conv_w", 3), ("rg_conv_b", 1),
    ("rg_w_a", 2), ("rg_b_a", 2), ("rg_w_i", 2), ("rg_b_i", 2), ("rg_lambda", 1), ("rg_w_out", 1),
    ("ffn_w_gate", 2), ("ffn_w_up", 2), ("ffn_w_down", 1),
)
REPLICATED = ("attn_rel_bias", "norm_mix_pre", "norm_mix_post", "norm_ffn_pre", "norm_ffn_post")
WEIGHTS = ("attn_w_in", "attn_rel_bias", "attn_w_out", "rg_w_in", "rg_conv_w", "rg_conv_b", "rg_w_a", "rg_b_a",
           "rg_w_i", "rg_b_i", "rg_lambda", "rg_w_out", "norm_mix_pre", "norm_mix_post", "norm_ffn_pre",
           "norm_ffn_post", "ffn_w_gate", "ffn_w_up", "ffn_w_down")


def _params(*dims):
    return pltpu.CompilerParams(dimension_semantics=dims or None, vmem_limit_bytes=V7X_VMEM_LIMIT)


def _sds(shape, dtype):
    return jax.ShapeDtypeStruct(tuple(shape), dtype)


def _row_tile(n, pref):
    t = min(n, pref)
    assert n % t == 0, (n, pref)
    return t


def _divisor_tile(n, limit, multiple):
    if n <= limit:
        return n
    best = max(t for t in range(multiple, limit + 1, multiple) if n % t == 0)
    return best


NN = (((1,), (0,)), ((), ()))
NT = (((1,), (1,)), ((), ()))
TN = (((0,), (0,)), ((), ()))


def _gmm(name, a, b, *, grid, a_blk, a_idx, b_blk, b_idx, o_blk, o_idx, out_shape, out_dtype, dn, acc_shape):
    nk = grid[-1]
    kax = len(grid) - 1

    def body(a_ref, b_ref, o_ref, acc_ref):
        part = lax.dot_general(a_ref[...], b_ref[...], dn, preferred_element_type=F32)
        if nk == 1:
            o_ref[...] = part.astype(o_ref.dtype)
            return
        k = pl.program_id(kax)

        @pl.when(k == 0)
        def _():
            acc_ref[...] = part

        @pl.when(k > 0)
        def _():
            acc_ref[...] += part

        @pl.when(k == nk - 1)
        def _():
            o_ref[...] = acc_ref[...].astype(o_ref.dtype)

    return pl.pallas_call(
        body, grid=grid,
        in_specs=[pl.BlockSpec(a_blk, a_idx), pl.BlockSpec(b_blk, b_idx)],
        out_specs=pl.BlockSpec(o_blk, o_idx),
        out_shape=_sds(out_shape, out_dtype),
        scratch_shapes=[pltpu.VMEM(acc_shape, F32)],
        compiler_params=_params(*(["parallel"] * kax + ["arbitrary"])),
        name=name,
    )(a, b)


def mm_nn(name, a, b, out_dtype, tm=1024, tn=512, tk=1024):
    (m, k), (_, n) = a.shape, b.shape
    tm, tn, tk = _row_tile(m, tm), _row_tile(n, tn), _row_tile(k, tk)
    return _gmm(name, a, b, grid=(m // tm, n // tn, k // tk),
                a_blk=(tm, tk), a_idx=lambda i, j, kk: (i, kk), b_blk=(tk, tn), b_idx=lambda i, j, kk: (kk, j),
                o_blk=(tm, tn), o_idx=lambda i, j, kk: (i, j), out_shape=(m, n), out_dtype=out_dtype, dn=NN,
                acc_shape=(tm, tn))


def mm_nt(name, a, b, out_dtype, tm=1024, tn=512, tk=1024):
    (m, k), (n, _) = a.shape, b.shape
    tm, tn, tk = _row_tile(m, tm), _row_tile(n, tn), _row_tile(k, tk)
    return _gmm(name, a, b, grid=(m // tm, n // tn, k // tk),
                a_blk=(tm, tk), a_idx=lambda i, j, kk: (i, kk), b_blk=(tn, tk), b_idx=lambda i, j, kk: (j, kk),
                o_blk=(tm, tn), o_idx=lambda i, j, kk: (i, j), out_shape=(m, n), out_dtype=out_dtype, dn=NT,
                acc_shape=(tm, tn))


def mm_tn(name, a, b, out_dtype, tm=1024, tn=512, tk=4096):
    (k, m), (_, n) = a.shape, b.shape
    tm, tn, tk = _row_tile(m, tm), _row_tile(n, tn), _row_tile(k, tk)
    return _gmm(name, a, b, grid=(m // tm, n // tn, k // tk),
                a_blk=(tk, tm), a_idx=lambda i, j, kk: (kk, i), b_blk=(tk, tn), b_idx=lambda i, j, kk: (kk, j),
                o_blk=(tm, tn), o_idx=lambda i, j, kk: (i, j), out_shape=(m, n), out_dtype=out_dtype, dn=TN,
                acc_shape=(tm, tn))


MXU_WIDTH = 256


def _blocks_per_step(n8):
    return 1 if n8 % MXU_WIDTH == 0 else 2


def _side_by_side(w_ref, j0, per):
    return w_ref[j0] if per == 1 else jnp.concatenate([w_ref[j0 + j] for j in range(per)], axis=1)


def mm_nn_wblk(name, a, wb, layer, out_dtype, tm=1024):
    (m, k), (nb, _, _, n8) = a.shape, wb.shape
    tm = _row_tile(m, tm)
    per = _blocks_per_step(n8)

    def body(a_ref, w_ref, o_ref):
        o_ref[...] = jnp.dot(a_ref[...], _side_by_side(w_ref, 0, per), preferred_element_type=F32).astype(o_ref.dtype)

    return pl.pallas_call(
        body, grid=(m // tm, nb // per),
        in_specs=[pl.BlockSpec((tm, k), lambda i, j: (i, 0)),
                  pl.BlockSpec((per, None, k, n8), lambda i, j: (j, layer, 0, 0))],
        out_specs=pl.BlockSpec((tm, per * n8), lambda i, j: (i, j)), out_shape=_sds((m, nb * n8), out_dtype),
        compiler_params=_params("parallel", "parallel"), name=name)(a, wb)


def mm_nt_wblk(name, a, wb, layer, out_dtype, tm=1024, tn=512):
    m = a.shape[0]
    nb, _, k, n8 = wb.shape
    tm, tn = _row_tile(m, tm), _row_tile(k, tn)
    per = _blocks_per_step(n8)

    def body(a_ref, b_ref, o_ref):
        acc = None
        for j in range(0, nb, per):
            part = lax.dot_general(a_ref[:, j * n8:(j + per) * n8], _side_by_side(b_ref, j, per), NT,
                                   preferred_element_type=F32)
            acc = part if acc is None else acc + part
        o_ref[...] = acc.astype(o_ref.dtype)

    return pl.pallas_call(
        body, grid=(m // tm, k // tn),
        in_specs=[pl.BlockSpec((tm, nb * n8), lambda i, j: (i, 0)),
                  pl.BlockSpec((nb, None, tn, n8), lambda i, j: (0, layer, j, 0))],
        out_specs=pl.BlockSpec((tm, tn), lambda i, j: (i, j)), out_shape=_sds((m, k), out_dtype),
        compiler_params=_params("parallel", "parallel"), name=name)(a, wb)


def mm_tn_oblk(name, a, b, n8, out_dtype, tk=4096):
    (t, k), nb = a.shape, b.shape[1] // n8
    tk = _row_tile(t, tk)
    per = _blocks_per_step(n8)
    nk = t // tk

    def body(a_ref, b_ref, o_ref, acc_ref):
        s = pl.program_id(1)
        part = lax.dot_general(a_ref[...], b_ref[...], TN, preferred_element_type=F32)

        @pl.when(s == 0)
        def _():
            acc_ref[...] = part

        @pl.when(s > 0)
        def _():
            acc_ref[...] += part

        @pl.when(s == nk - 1)
        def _():
            for j in range(per):
                o_ref[j] = acc_ref[:, j * n8:(j + 1) * n8].astype(o_ref.dtype)

    return pl.pallas_call(
        body, grid=(nb // per, nk),
        in_specs=[pl.BlockSpec((tk, k), lambda j, s: (s, 0)), pl.BlockSpec((tk, per * n8), lambda j, s: (s, j))],
        out_specs=pl.BlockSpec((per, k, n8), lambda j, s: (j, 0, 0)), out_shape=_sds((nb, k, n8), out_dtype),
        scratch_shapes=[pltpu.VMEM((k, per * n8), F32)],
        compiler_params=_params("parallel", "arbitrary"), name=name)(a, b)


def rmsnorm_fwd(name, x, g):
    t, d = x.shape
    tr = _row_tile(t, 512)

    def body(x_ref, g_ref, o_ref):
        xv = x_ref[...]
        r = lax.rsqrt(jnp.mean(xv * xv, axis=-1, keepdims=True) + RMS_EPS)
        o_ref[...] = (xv * r * g_ref[...]).astype(o_ref.dtype)

    return pl.pallas_call(
        body, grid=(t // tr,),
        in_specs=[pl.BlockSpec((tr, d), lambda i: (i, 0)), pl.BlockSpec((1, d), lambda i: (0, 0))],
        out_specs=pl.BlockSpec((tr, d), lambda i: (i, 0)),
        out_shape=_sds((t, d), BF16), compiler_params=_params("parallel"), name=name)(x, g)


def resid_norm_fwd(name, x, m, g, g_next):
    t, d = x.shape
    tr = _row_tile(t, 512)
    chained = g_next is not None

    def body(*refs):
        x_ref, m_ref, g_ref = refs[:3]
        mv = m_ref[...]
        r = lax.rsqrt(jnp.mean(mv * mv, axis=-1, keepdims=True) + RMS_EPS)
        x1 = x_ref[...] + mv * r * g_ref[...]
        if chained:
            gn_ref, o_ref, h_ref = refs[3:]
            r1 = lax.rsqrt(jnp.mean(x1 * x1, axis=-1, keepdims=True) + RMS_EPS)
            h_ref[...] = (x1 * r1 * gn_ref[...]).astype(BF16)
        else:
            o_ref, = refs[3:]
        o_ref[...] = x1

    row = pl.BlockSpec((tr, d), lambda i: (i, 0))
    vec = pl.BlockSpec((1, d), lambda i: (0, 0))
    out = pl.pallas_call(
        body, grid=(t // tr,),
        in_specs=[row, row, vec] + ([vec] if chained else []),
        out_specs=[row, row] if chained else [row],
        out_shape=[_sds((t, d), F32)] + ([_sds((t, d), BF16)] if chained else []),
        compiler_params=_params("parallel"), name=name)(*([x, m, g] + ([g_next] if chained else [])))
    return (out[0], out[1]) if chained else (out[0], None)


def norm_bwd(name, dy, x, g, resid, out_dtype):
    t, d = x.shape
    tr = _row_tile(t, 512)
    has_res = resid is not None

    def body(*refs):
        if has_res:
            dy_ref, x_ref, g_ref, r_ref, dx_ref, dg_ref = refs
        else:
            dy_ref, x_ref, g_ref, dx_ref, dg_ref = refs
        i = pl.program_id(0)
        xv = x_ref[...]
        dyv = dy_ref[...].astype(F32)
        r = lax.rsqrt(jnp.mean(xv * xv, axis=-1, keepdims=True) + RMS_EPS)
        xh = xv * r
        dxh = dyv * g_ref[...]
        dx = r * (dxh - xh * jnp.mean(dxh * xh, axis=-1, keepdims=True))
        if has_res:
            dx = dx + r_ref[...]
        dx_ref[...] = dx.astype(dx_ref.dtype)
        part = jnp.sum(dyv * xh, axis=0, keepdims=True)

        @pl.when(i == 0)
        def _():
            dg_ref[...] = part

        @pl.when(i > 0)
        def _():
            dg_ref[...] += part

    row = pl.BlockSpec((tr, d), lambda i: (i, 0))
    vec = pl.BlockSpec((1, d), lambda i: (0, 0))
    ins = [dy, x, g] + ([resid] if has_res else [])
    return pl.pallas_call(
        body, grid=(t // tr,),
        in_specs=[row, row, vec] + ([row] if has_res else []),
        out_specs=[row, vec],
        out_shape=[_sds((t, d), out_dtype), _sds((1, d), F32)],
        compiler_params=_params("arbitrary"), name=name)(*ins)


def norm_bwd_chain(name, dh, x, g, resid, m_prev, g_prev):
    t, d = x.shape
    tr = _row_tile(t, 512)

    def body(dh_ref, x_ref, g_ref, r_ref, m_ref, gp_ref, dx_ref, dg_ref, dm_ref, dgp_ref):
        i = pl.program_id(0)
        xv = x_ref[...]
        dhv = dh_ref[...]
        r = lax.rsqrt(jnp.mean(xv * xv, axis=-1, keepdims=True) + RMS_EPS)
        xh = xv * r
        dxh = dhv * g_ref[...]
        dx = r * (dxh - xh * jnp.mean(dxh * xh, axis=-1, keepdims=True)) + r_ref[...]
        dx_ref[...] = dx
        mv = m_ref[...]
        rm = lax.rsqrt(jnp.mean(mv * mv, axis=-1, keepdims=True) + RMS_EPS)
        mh = mv * rm
        dmh = dx * gp_ref[...]
        dm_ref[...] = (rm * (dmh - mh * jnp.mean(dmh * mh, axis=-1, keepdims=True))).astype(BF16)
        part = jnp.sum(dhv * xh, axis=0, keepdims=True)
        part_prev = jnp.sum(dx * mh, axis=0, keepdims=True)

        @pl.when(i == 0)
        def _():
            dg_ref[...] = part
            dgp_ref[...] = part_prev

        @pl.when(i > 0)
        def _():
            dg_ref[...] += part
            dgp_ref[...] += part_prev

    row = pl.BlockSpec((tr, d), lambda i: (i, 0))
    vec = pl.BlockSpec((1, d), lambda i: (0, 0))
    return pl.pallas_call(
        body, grid=(t // tr,), in_specs=[row, row, vec, row, row, vec], out_specs=[row, vec, row, vec],
        out_shape=[_sds((t, d), F32), _sds((1, d), F32), _sds((t, d), BF16), _sds((1, d), F32)],
        compiler_params=_params("arbitrary"), name=name)(dh, x, g, resid, m_prev, g_prev)


def close_bwd(tag, dh, x, g_pre, dx1, prev):
    if prev is None:
        dx, dg_pre = norm_bwd(tag + "_dpre", dh, x, g_pre, dx1, F32)
        return dx, dg_pre, None, None
    return norm_bwd_chain(tag + "_dpre", dh, x, g_pre, dx1, *prev)


def loss_grad(name, y, target):
    t, d = y.shape
    tr = _row_tile(t, 512)

    def body(y_ref, t_ref, dy_ref, s_ref):
        i = pl.program_id(0)
        err = y_ref[...] - t_ref[...]
        dy_ref[...] = err * (1.0 / d)
        part = jnp.sum(err * err, axis=0, keepdims=True)

        @pl.when(i == 0)
        def _():
            s_ref[...] = part

        @pl.when(i > 0)
        def _():
            s_ref[...] += part

    row = pl.BlockSpec((tr, d), lambda i: (i, 0))
    vec = pl.BlockSpec((1, d), lambda i: (0, 0))
    return pl.pallas_call(
        body, grid=(t // tr,), in_specs=[row, row], out_specs=[row, vec],
        out_shape=[_sds((t, d), F32), _sds((1, d), F32)],
        compiler_params=_params("arbitrary"), name=name)(y, target)


PAIR = 2 * HEAD_DIM
N_PAIRS = N_HEADS // 2
A_TQ = 512
A_UNROLL_FWD = 8
A_UNROLL_BWD = 4


def _halves(x):
    lane = lax.broadcasted_iota(jnp.int32, x.shape, x.ndim - 1)
    zero = jnp.zeros_like(x)
    return jnp.where(lane < HEAD_DIM, x, zero), jnp.where(lane >= HEAD_DIM, x, zero)


def _merge(a, b):
    lane = lax.broadcasted_iota(jnp.int32, a.shape, a.ndim - 1)
    return jnp.where(lane < HEAD_DIM, a, b)


def _a_valid(c):
    col = lax.broadcasted_iota(jnp.int32, (CHUNK, BAND), 1)
    return col >= (N_LEFT - c) * CHUNK


def attn_a_fwd(name, proj, kp, vp, bias, q_blk):
    t = proj.shape[0]
    tq = _row_tile(t, A_TQ)
    ncs = tq // CHUNK
    un = math.gcd(A_UNROLL_FWD, ncs)

    def body(q_ref, k_ref, v_ref, b_ref, o_ref, l_ref):
        i = pl.program_id(1)

        def group(gg, carry):
            cs = [i * ncs + gg * un + u for u in range(un)]
            r0s = [pl.multiple_of((gg * un + u) * CHUNK, CHUNK) for u in range(un)]
            k0s = [pl.multiple_of(c * CHUNK, CHUNK) for c in cs]
            ss = []
            for u in range(un):
                qh = _halves(q_ref[pl.ds(r0s[u], CHUNK), :] * QK_SCALE)
                kwin = k_ref[pl.ds(k0s[u], BAND), :]
                valid = _a_valid(cs[u])
                for hh in range(2):
                    s = lax.dot_general(qh[hh], kwin, NT, preferred_element_type=F32) + b_ref[hh]
                    ss.append(jnp.where(valid, s, -1e30))
            ps, lses = [], []
            for s in ss:
                mx = jnp.max(s, axis=-1, keepdims=True)
                p = jnp.exp(s - mx)
                den = jnp.sum(p, axis=-1, keepdims=True)
                ps.append((p * (1.0 / den)).astype(BF16))
                lses.append(mx + jnp.log(den))
            for u in range(un):
                vwin = v_ref[pl.ds(k0s[u], BAND), :]
                o0 = jnp.dot(ps[2 * u], vwin, preferred_element_type=F32)
                o1 = jnp.dot(ps[2 * u + 1], vwin, preferred_element_type=F32)
                o_ref[pl.ds(r0s[u], CHUNK), :] = _merge(o0, o1)
                l_ref[pl.ds(r0s[u], CHUNK), :] = jnp.concatenate([lses[2 * u], lses[2 * u + 1]], axis=1)
            return carry

        lax.fori_loop(0, ncs // un, group, 0)

    return pl.pallas_call(
        body, grid=(N_PAIRS, t // tq),
        in_specs=[pl.BlockSpec((tq, PAIR), lambda p, i: (i, q_blk + p)),
                  pl.BlockSpec((t + PAD_KEYS, PAIR), lambda p, i: (0, p)),
                  pl.BlockSpec((t + PAD_KEYS, PAIR), lambda p, i: (0, p)),
                  pl.BlockSpec((2, CHUNK, BAND), lambda p, i: (p, 0, 0))],
        out_specs=[pl.BlockSpec((tq, PAIR), lambda p, i: (i, p)),
                   pl.BlockSpec((None, tq, 2), lambda p, i: (p, i, 0))],
        out_shape=[_sds((t, N_PAIRS * PAIR), F32), _sds((N_PAIRS, t, 2), F32)],
        compiler_params=_params("parallel", "parallel"), name=name)(proj, kp, vp, bias)


def attn_a_bwd(name, proj, kp, vp, bias, o, lse, do, q_blk, do_blk):
    t = proj.shape[0]
    tq = _row_tile(t, A_TQ)
    ncs = tq // CHUNK
    un = math.gcd(A_UNROLL_BWD, ncs)

    def body(q_ref, k_ref, v_ref, b_ref, o_ref, l_ref, do_ref, dq_ref, dk_ref, dv_ref, db_ref):
        i = pl.program_id(1)

        @pl.when(i == 0)
        def _():
            dk_ref[...] = jnp.zeros_like(dk_ref)
            dv_ref[...] = jnp.zeros_like(dv_ref)
            db_ref[...] = jnp.zeros_like(db_ref)

        def group(gg, carry):
            cs = [i * ncs + gg * un + u for u in range(un)]
            r0s = [pl.multiple_of((gg * un + u) * CHUNK, CHUNK) for u in range(un)]
            k0s = [pl.multiple_of(c * CHUNK, CHUNK) for c in cs]
            qhs, dohs, ps, dps, deltas = [], [], [], [], []
            for u in range(un):
                rows = pl.ds(r0s[u], CHUNK)
                qh = _halves(q_ref[rows, :] * QK_SCALE)
                doh = _halves(do_ref[rows, :])
                kwin = k_ref[pl.ds(k0s[u], BAND), :]
                vwin = v_ref[pl.ds(k0s[u], BAND), :]
                valid = _a_valid(cs[u])
                dl = _halves(do_ref[rows, :].astype(F32) * o_ref[rows, :])
                for hh in range(2):
                    s = lax.dot_general(qh[hh], kwin, NT, preferred_element_type=F32) + b_ref[hh]
                    ps.append(jnp.where(valid, jnp.exp(s - l_ref[rows, hh:hh + 1]), 0.0))
                    dps.append(lax.dot_general(doh[hh], vwin, NT, preferred_element_type=F32))
                    deltas.append(jnp.sum(dl[hh], axis=-1, keepdims=True))
                qhs.append(qh)
                dohs.append(doh)
            dss = [p * (dp - dl) for p, dp, dl in zip(ps, dps, deltas)]
            for hh in range(2):
                tot = dss[hh]
                for u in range(1, un):
                    tot = tot + dss[2 * u + hh]
                db_ref[hh] += tot
            for u in range(un):
                kwin = k_ref[pl.ds(k0s[u], BAND), :]
                ds0, ds1 = dss[2 * u].astype(BF16), dss[2 * u + 1].astype(BF16)
                dq_ref[pl.ds(r0s[u], CHUNK), :] = _merge(jnp.dot(ds0, kwin, preferred_element_type=F32),
                                                         jnp.dot(ds1, kwin, preferred_element_type=F32)) * QK_SCALE
                dk_ref[pl.ds(k0s[u], BAND), :] += (lax.dot_general(ds0, qhs[u][0], TN, preferred_element_type=F32)
                                                   + lax.dot_general(ds1, qhs[u][1], TN, preferred_element_type=F32))
                dv_ref[pl.ds(k0s[u], BAND), :] += (
                    lax.dot_general(ps[2 * u].astype(BF16), dohs[u][0], TN, preferred_element_type=F32)
                    + lax.dot_general(ps[2 * u + 1].astype(BF16), dohs[u][1], TN, preferred_element_type=F32))
            return carry

        lax.fori_loop(0, ncs // un, group, 0)

    tile = lambda blk: pl.BlockSpec((tq, PAIR), lambda p, i: (i, blk + p))
    whole = pl.BlockSpec((t + PAD_KEYS, PAIR), lambda p, i: (0, p))
    bspec = pl.BlockSpec((2, CHUNK, BAND), lambda p, i: (p, 0, 0))
    return pl.pallas_call(
        body, grid=(N_PAIRS, t // tq),
        in_specs=[tile(q_blk), whole, whole, bspec, tile(0), pl.BlockSpec((None, tq, 2), lambda p, i: (p, i, 0)),
                  tile(do_blk)],
        out_specs=[tile(0), whole, whole, bspec],
        out_shape=[_sds((t, N_PAIRS * PAIR), F32), _sds((t + PAD_KEYS, N_PAIRS * PAIR), F32),
                   _sds((t + PAD_KEYS, N_PAIRS * PAIR), F32), _sds((2 * N_PAIRS, CHUNK, BAND), F32)],
        compiler_params=_params("parallel", "arbitrary"), name=name)(proj, kp, vp, bias, o, lse, do)


SB_TQ = 256
SB_TK = 256
SB_DEAD = -125.0


def _tri(n, strict):
    j = lax.broadcasted_iota(jnp.int32, (n, n), 0)
    s = lax.broadcasted_iota(jnp.int32, (n, n), 1)
    return jnp.where((j > s) if strict else (j >= s), 1.0, 0.0).astype(BF16)


def _suffix_sum(x, tri, exact):
    hi = x.astype(BF16)
    out = jnp.dot(hi, tri, preferred_element_type=F32)
    if exact:
        lo = (x - hi.astype(F32)).astype(BF16)
        out = out + jnp.dot(lo, tri, preferred_element_type=F32)
    return out


def _sb_scores(qh, ks, causal):
    z = lax.dot_general(qh, ks, NT, preferred_element_type=F32)
    lb = jnp.minimum(z, 0.0) - jnp.log(1.0 + jnp.exp(-jnp.abs(z)))
    m = lb - z
    if causal is not None:
        m = jnp.where(causal, m, 0.0)
    return lb, m


def _causal(tq, tk, off):
    return (lax.broadcasted_iota(jnp.int32, (tq, tk), 1) + off * tk) < lax.broadcasted_iota(jnp.int32, (tq, tk), 0)


def sb_fwd(name, proj, q_blk, k_blk, v_blk):
    t = proj.shape[0]
    tq = _row_tile(t, SB_TQ)
    tk = min(SB_TK, tq)
    per = tq // tk

    def body(q_ref, k_ref, v_ref, o_ref):
        i = pl.program_id(1)
        tri = _tri(tk, True)
        qh = _halves(q_ref[...] * QK_SCALE)

        def blocks(kb, carry, off):
            k0 = pl.multiple_of(kb * tk, tk)
            ks, vs = k_ref[pl.ds(k0, tk), :], v_ref[pl.ds(k0, tk), :]
            causal = None if off is None else _causal(tq, tk, off)
            lbm = [_sb_scores(qh[hh], ks, causal) for hh in range(2)]
            afters = [_suffix_sum(lbm[hh][1], tri, False) for hh in range(2)]
            out = []
            for hh in range(2):
                acc, cm = carry[2 * hh], carry[2 * hh + 1]
                w = jnp.exp(lbm[hh][0] + afters[hh] + cm)
                if causal is not None:
                    w = jnp.where(causal, w, 0.0)
                out += [acc + jnp.dot(w.astype(BF16), vs, preferred_element_type=F32),
                        cm + jnp.sum(lbm[hh][1], axis=-1, keepdims=True)]
            return tuple(out)

        def alive(carry):
            return jnp.maximum(jnp.max(carry[1]), jnp.max(carry[3])) > SB_DEAD

        carry = (jnp.zeros((tq, PAIR), F32), jnp.zeros((tq, 1), F32)) * 2
        for off in reversed(range(per)):
            carry = blocks(i * per + off, carry, off)

        def step(c):
            new = blocks(i * per - 1 - c[0], c[2:], None)
            return (c[0] + 1, alive(new)) + new

        out = lax.while_loop(lambda c: jnp.logical_and(c[0] < i * per, c[1]), step,
                             (jnp.int32(0), alive(carry)) + carry)
        o_ref[...] = _merge(out[2], out[4])

    return pl.pallas_call(
        body, grid=(N_PAIRS, t // tq),
        in_specs=[pl.BlockSpec((tq, PAIR), lambda p, i: (i, q_blk + p)),
                  pl.BlockSpec((t, PAIR), lambda p, i: (0, k_blk + p)),
                  pl.BlockSpec((t, PAIR), lambda p, i: (0, v_blk + p))],
        out_specs=pl.BlockSpec((tq, PAIR), lambda p, i: (i, p)),
        out_shape=_sds((t, N_PAIRS * PAIR), F32), compiler_params=_params("parallel", "parallel"), name=name,
    )(proj, proj, proj)


def sb_bwd(name, proj, o, do, q_blk, k_blk, v_blk, do_blk):
    t = proj.shape[0]
    tq = _row_tile(t, SB_TQ)
    tk = min(SB_TK, tq)
    per = tq // tk

    def body(q_ref, k_ref, v_ref, o_ref, do_ref, dq_ref, dk_ref, dv_ref):
        i = pl.program_id(1)

        @pl.when(i == 0)
        def _():
            dk_ref[...] = jnp.zeros_like(dk_ref)
            dv_ref[...] = jnp.zeros_like(dv_ref)

        tri_s, tri_i = _tri(tk, True), _tri(tk, False)
        qh = _halves(q_ref[...] * QK_SCALE)
        doh = _halves(do_ref[...])
        deltas = [jnp.sum(x, axis=-1, keepdims=True) for x in _halves(do_ref[...].astype(F32) * o_ref[...])]

        def blocks(kb, carry, off):
            k0 = pl.multiple_of(kb * tk, tk)
            ks, vs = k_ref[pl.ds(k0, tk), :], v_ref[pl.ds(k0, tk), :]
            causal = None if off is None else _causal(tq, tk, off)
            lbm = [_sb_scores(qh[hh], ks, causal) for hh in range(2)]
            dws = [lax.dot_general(doh[hh], vs, NT, preferred_element_type=F32) for hh in range(2)]
            afters = [_suffix_sum(lbm[hh][1], tri_s, False) for hh in range(2)]
            wbs, es = [], []
            for hh in range(2):
                w = jnp.exp(lbm[hh][0] + afters[hh] + carry[3 * hh + 1])
                if causal is not None:
                    w = jnp.where(causal, w, 0.0)
                wbs.append(w.astype(BF16))
                es.append(wbs[hh].astype(F32) * dws[hh])
            sfx = [_suffix_sum(es[hh], tri_i, True) for hh in range(2)]
            dzs = []
            for hh in range(2):
                left = deltas[hh] - (sfx[hh] + carry[3 * hh + 2])
                sig = jnp.exp(lbm[hh][0])
                dz = es[hh] * (1.0 - sig) - left * sig
                if causal is not None:
                    dz = jnp.where(causal, dz, 0.0)
                dzs.append(dz.astype(BF16))
            dk_ref[pl.ds(k0, tk), :] += (lax.dot_general(dzs[0], qh[0], TN, preferred_element_type=F32)
                                         + lax.dot_general(dzs[1], qh[1], TN, preferred_element_type=F32))
            dv_ref[pl.ds(k0, tk), :] += (lax.dot_general(wbs[0], doh[0], TN, preferred_element_type=F32)
                                         + lax.dot_general(wbs[1], doh[1], TN, preferred_element_type=F32))
            out = []
            for hh in range(2):
                out += [carry[3 * hh] + jnp.dot(dzs[hh], ks, preferred_element_type=F32),
                        carry[3 * hh + 1] + jnp.sum(lbm[hh][1], axis=-1, keepdims=True),
                        carry[3 * hh + 2] + jnp.sum(es[hh], axis=-1, keepdims=True)]
            return tuple(out)

        def alive(carry):
            return jnp.maximum(jnp.max(carry[1]), jnp.max(carry[4])) > SB_DEAD

        zero = jnp.zeros((tq, 1), F32)
        carry = (jnp.zeros((tq, PAIR), F32), zero, zero) * 2
        for off in reversed(range(per)):
            carry = blocks(i * per + off, carry, off)

        def step(c):
            new = blocks(i * per - 1 - c[0], c[2:], None)
            return (c[0] + 1, alive(new)) + new

        out = lax.while_loop(lambda c: jnp.logical_and(c[0] < i * per, c[1]), step,
                             (jnp.int32(0), alive(carry)) + carry)
        dq_ref[...] = _merge(out[2], out[5]) * QK_SCALE

    tile = lambda blk: pl.BlockSpec((tq, PAIR), lambda p, i: (i, blk + p))
    whole = lambda blk: pl.BlockSpec((t, PAIR), lambda p, i: (0, blk + p))
    return pl.pallas_call(
        body, grid=(N_PAIRS, t // tq),
        in_specs=[tile(q_blk), whole(k_blk), whole(v_blk), tile(0), tile(do_blk)],
        out_specs=[tile(0), whole(0), whole(0)],
        out_shape=[_sds((t, N_PAIRS * PAIR), F32)] * 3,
        compiler_params=_params("parallel", "arbitrary"), name=name)(proj, proj, proj, o, do)


def _sigmoid(x):
    return 1.0 / (1.0 + jnp.exp(-x))


def gu_gap(f8):
    return -(-f8 // 128) * 128


def merge_gu(gate, up):
    f8 = gate.shape[-1]
    pad = jnp.zeros(gate.shape[:-1] + (gu_gap(f8) - f8,), gate.dtype)
    return jnp.concatenate([gate, pad, up], axis=-1)


def split_gu(gu, f8):
    return gu[..., :f8], gu[..., gu_gap(f8):]


PER = 2


def _stacked_pair(w_ref, p, f8):
    zeros = jnp.zeros((gu_gap(f8) - f8, w_ref.shape[-1]), w_ref.dtype)
    return jnp.concatenate([w_ref[PER * p], zeros, w_ref[PER * p + 1]], axis=0)


def ffn_up(name, h, wgu, f8, layer):
    t, d = h.shape
    nb, _, _, fw = wgu.shape
    gap = gu_gap(f8)
    tm = _row_tile(t, 1024)

    def body(h_ref, w_ref, gu_ref, a_ref):
        hv = h_ref[...]
        rs = [jnp.dot(hv, w_ref[j], preferred_element_type=F32) for j in range(PER)]
        a_ref[:, f8:gap] = jnp.zeros((tm, gap - f8), BF16)
        for j in range(PER):
            gu_ref[j] = rs[j].astype(BF16)
            g, u = rs[j][:, :f8], rs[j][:, gap:]
            a_ref[:, j * gap:j * gap + f8] = (g * _sigmoid(g) * u).astype(BF16)

    return pl.pallas_call(
        body, grid=(t // tm, nb // PER),
        in_specs=[pl.BlockSpec((tm, d), lambda i, k: (i, 0)),
                  pl.BlockSpec((PER, None, d, fw), lambda i, k: (k, layer, 0, 0))],
        out_specs=[pl.BlockSpec((PER, tm, fw), lambda i, k: (k, i, 0)),
                   pl.BlockSpec((None, tm, fw), lambda i, k: (k, i, 0))],
        out_shape=[_sds((nb, t, fw), BF16), _sds((nb // PER, t, fw), BF16)],
        compiler_params=_params("parallel", "parallel"), name=name)(h, wgu)


def _mm_all_blocks(name, a, w, layer, dn, tm):
    nb, t, f = a.shape
    wshape = w.shape[2:]
    d = wshape[1] if dn == NN else wshape[0]
    tm = _row_tile(t, tm)

    def body(a_ref, w_ref, o_ref):
        acc = lax.dot_general(a_ref[0], w_ref[0], dn, preferred_element_type=F32)
        for k in range(1, nb):
            acc = acc + lax.dot_general(a_ref[k], w_ref[k], dn, preferred_element_type=F32)
        o_ref[...] = acc

    return pl.pallas_call(
        body, grid=(t // tm,),
        in_specs=[pl.BlockSpec((nb, tm, f), lambda i: (0, i, 0)),
                  pl.BlockSpec((nb, None) + wshape, lambda i: (0, layer, 0, 0))],
        out_specs=pl.BlockSpec((tm, d), lambda i: (i, 0)), out_shape=_sds((t, d), F32),
        compiler_params=_params("parallel"), name=name)(a, w)


def ffn_down(name, a, wd, f8, layer):
    npair, t, fw = a.shape
    nb, _, _, d = wd.shape
    tm = _row_tile(t, 512)

    def body(a_ref, w_ref, o_ref):
        acc = jnp.dot(a_ref[0], _stacked_pair(w_ref, 0, f8), preferred_element_type=F32)
        for p in range(1, npair):
            acc = acc + jnp.dot(a_ref[p], _stacked_pair(w_ref, p, f8), preferred_element_type=F32)
        o_ref[...] = acc

    return pl.pallas_call(
        body, grid=(t // tm,),
        in_specs=[pl.BlockSpec((npair, tm, fw), lambda i: (0, i, 0)),
                  pl.BlockSpec((nb, None, f8, d), lambda i: (0, layer, 0, 0))],
        out_specs=pl.BlockSpec((tm, d), lambda i: (i, 0)), out_shape=_sds((t, d), F32),
        compiler_params=_params("parallel"), name=name)(a, wd)


def ffn_bwd_act(name, dm, wd, gu, f8, layer):
    t, d = dm.shape
    nb, _, fw = gu.shape
    gap = gu_gap(f8)
    tm = _row_tile(t, 1024)

    rows = _row_tile(tm, 256)

    def body(dm_ref, wd_ref, gu_ref, o_ref, da_ref):
        da_ref[...] = lax.dot_general(dm_ref[...], _stacked_pair(wd_ref, 0, f8), NT, preferred_element_type=F32)
        o_ref[:, :, f8:gap] = jnp.zeros((PER, tm, gap - f8), BF16)
        for r0 in range(0, tm, rows):
            rs = pl.ds(r0, rows)
            for j in range(PER):
                daj = da_ref[rs, j * gap:j * gap + f8]
                gv = gu_ref[j, rs, :f8].astype(F32)
                uv = gu_ref[j, rs, gap:].astype(F32)
                sg = _sigmoid(gv)
                o_ref[j, rs, :f8] = (daj * uv * sg * (1.0 + gv * (1.0 - sg))).astype(BF16)
                o_ref[j, rs, gap:] = (daj * gv * sg).astype(BF16)

    bspec = pl.BlockSpec((PER, tm, fw), lambda i, k: (k, i, 0))
    return pl.pallas_call(
        body, grid=(t // tm, nb // PER),
        in_specs=[pl.BlockSpec((tm, d), lambda i, k: (i, 0)),
                  pl.BlockSpec((PER, None, f8, d), lambda i, k: (k, layer, 0, 0)), bspec],
        out_specs=bspec, out_shape=_sds((nb, t, fw), BF16), scratch_shapes=[pltpu.VMEM((tm, fw), F32)],
        compiler_params=_params("parallel", "parallel"), name=name)(dm, wd, gu)


def ffn_bwd_dh(name, dgu, wgu, layer):
    return _mm_all_blocks(name, dgu, wgu, layer, NT, 512)


def ffn_dw_in(name, h, dact):
    t, d = h.shape
    nb, _, f8 = dact.shape
    tk = _row_tile(t, 4096)
    return _gmm(name, h, dact, grid=(nb, t // tk),
                a_blk=(tk, d), a_idx=lambda b, s: (s, 0), b_blk=(None, tk, f8), b_idx=lambda b, s: (b, s, 0),
                o_blk=(None, d, f8), o_idx=lambda b, s: (b, 0, 0), out_shape=(nb, d, f8), out_dtype=F32, dn=TN,
                acc_shape=(d, f8))


def ffn_dw_down(name, a, dm, f8):
    npair, t, fw = a.shape
    d = dm.shape[1]
    gap = gu_gap(f8)
    tk = _row_tile(t, 4096)
    nk = t // tk

    def body(a_ref, dm_ref, o_ref, acc_ref):
        s = pl.program_id(1)
        part = lax.dot_general(a_ref[...], dm_ref[...], TN, preferred_element_type=F32)

        @pl.when(s == 0)
        def _():
            acc_ref[...] = part

        @pl.when(s > 0)
        def _():
            acc_ref[...] += part

        @pl.when(s == nk - 1)
        def _():
            for j in range(PER):
                o_ref[j] = acc_ref[j * gap:j * gap + f8, :]

    return pl.pallas_call(
        body, grid=(npair, nk),
        in_specs=[pl.BlockSpec((None, tk, fw), lambda p, s: (p, s, 0)), pl.BlockSpec((tk, d), lambda p, s: (s, 0))],
        out_specs=pl.BlockSpec((PER, f8, d), lambda p, s: (p, 0, 0)), out_shape=_sds((PER * npair, f8, d), F32),
        scratch_shapes=[pltpu.VMEM((fw, d), F32)],
        compiler_params=_params("parallel", "arbitrary"), name=name)(a, dm)


GELU_C = math.sqrt(2.0 / math.pi)
GELU_A = 0.044715


def _gelu(x):
    return 0.5 * x * (1.0 + jnp.tanh(GELU_C * (x + GELU_A * x * x * x)))


def _gelu_grad(x):
    th = jnp.tanh(GELU_C * (x + GELU_A * x * x * x))
    return 0.5 * (1.0 + th) + 0.5 * x * (1.0 - th * th) * GELU_C * (1.0 + 3.0 * GELU_A * x * x)


def _neg_expm1(x):
    series = x * (1.0 + x * (0.5 + x * (1.0 / 6.0 + x * (1.0 / 24.0 + x * (1.0 / 120.0 + x * (1.0 / 720.0))))))
    return -jnp.where(x > -0.25, series, jnp.exp(x) - 1.0)


CONV_TR = 256
CONV_TAPS = 4
HALO = 8


def _shifted(ext, k, tr, back):
    if back:
        return pltpu.roll(ext, k, 0)[HALO:, :] if k else ext[HALO:, :]
    return pltpu.roll(ext, tr + HALO - k, 0)[:tr, :] if k else ext[:tr, :]


def conv4_fwd(name, src, cb, w, b):
    t, c = src.shape[0], w.shape[1]
    tr = _row_tile(t, CONV_TR)
    hb = tr // HALO

    def body(x_ref, h_ref, w_ref, b_ref, o_ref):
        i = pl.program_id(0)
        ext = jnp.concatenate([jnp.where(i == 0, 0.0, h_ref[...]), x_ref[...]], axis=0)
        acc = b_ref[...]
        for k in range(CONV_TAPS):
            acc = acc + w_ref[CONV_TAPS - 1 - k:CONV_TAPS - k, :] * _shifted(ext, k, tr, True)
        o_ref[...] = acc

    return pl.pallas_call(
        body, grid=(t // tr,),
        in_specs=[pl.BlockSpec((tr, c), lambda i: (i, cb)),
                  pl.BlockSpec((HALO, c), lambda i: (jnp.maximum(i * hb - 1, 0), cb)),
                  pl.BlockSpec((CONV_TAPS, c), lambda i: (0, 0)), pl.BlockSpec((1, c), lambda i: (0, 0))],
        out_specs=pl.BlockSpec((tr, c), lambda i: (i, 0)), out_shape=_sds((t, c), F32),
        compiler_params=_params("parallel"), name=name)(src, src, w, b)


def conv4_bwd_x(name, dy, w):
    t, c = dy.shape
    tr = _row_tile(t, CONV_TR)
    hb = tr // HALO
    last = t // tr - 1

    def body(y_ref, h_ref, w_ref, o_ref):
        i = pl.program_id(0)
        ext = jnp.concatenate([y_ref[...], jnp.where(i == last, 0.0, h_ref[...])], axis=0)
        acc = w_ref[CONV_TAPS - 1:CONV_TAPS, :] * y_ref[...]
        for k in range(1, CONV_TAPS):
            acc = acc + w_ref[CONV_TAPS - 1 - k:CONV_TAPS - k, :] * _shifted(ext, k, tr, False)
        o_ref[...] = acc

    return pl.pallas_call(
        body, grid=(t // tr,),
        in_specs=[pl.BlockSpec((tr, c), lambda i: (i, 0)),
                  pl.BlockSpec((HALO, c), lambda i: (jnp.minimum((i + 1) * hb, t // HALO - 1), 0)),
                  pl.BlockSpec((CONV_TAPS, c), lambda i: (0, 0))],
        out_specs=pl.BlockSpec((tr, c), lambda i: (i, 0)), out_shape=_sds((t, c), F32),
        compiler_params=_params("parallel"), name=name)(dy, dy, w)


def conv4_bwd_w(name, src, cb, dy):
    t, c = dy.shape
    tr = _row_tile(t, CONV_TR)
    hb = tr // HALO

    def body(x_ref, h_ref, dy_ref, dw_ref, db_ref):
        i = pl.program_id(0)

        @pl.when(i == 0)
        def _():
            dw_ref[...] = jnp.zeros_like(dw_ref)
            db_ref[...] = jnp.zeros_like(db_ref)

        ext = jnp.concatenate([jnp.where(i == 0, 0.0, h_ref[...]), x_ref[...]], axis=0)
        dyv = dy_ref[...]
        db_ref[...] += jnp.sum(dyv, axis=0, keepdims=True)
        for k in range(CONV_TAPS):
            dw_ref[CONV_TAPS - 1 - k:CONV_TAPS - k, :] += jnp.sum(dyv * _shifted(ext, k, tr, True), axis=0,
                                                                   keepdims=True)

    return pl.pallas_call(
        body, grid=(t // tr,),
        in_specs=[pl.BlockSpec((tr, c), lambda i: (i, cb)),
                  pl.BlockSpec((HALO, c), lambda i: (jnp.maximum(i * hb - 1, 0), cb)),
                  pl.BlockSpec((tr, c), lambda i: (i, 0))],
        out_specs=[pl.BlockSpec((CONV_TAPS, c), lambda i: (0, 0)), pl.BlockSpec((1, c), lambda i: (0, 0))],
        out_shape=[_sds((CONV_TAPS, c), F32), _sds((1, c), F32)],
        compiler_params=_params("arbitrary"), name=name)(src, src, dy)


def _rg_gate_values(xcv, wa_ref, wi_ref, ba_ref, bi_ref, lam_ref):
    xb = xcv.astype(BF16)
    r = _sigmoid(jnp.dot(xb, wa_ref[...], preferred_element_type=F32) + ba_ref[...])
    ig = _sigmoid(jnp.dot(xb, wi_ref[...], preferred_element_type=F32) + bi_ref[...])
    lam = lam_ref[...]
    sp = jnp.maximum(-lam, 0.0) + jnp.log(1.0 + jnp.exp(-jnp.abs(lam)))
    log_a = -LRU_C * r * sp
    a = jnp.exp(log_a)
    mult = jnp.sqrt(_neg_expm1(2.0 * log_a))
    return xb, r, ig, sp, a, mult


def rg_gates_fwd(name, xc, wa, wi, ba, bi, lam):
    t, c = xc.shape
    nb, cb, _ = wa.shape
    tm = _row_tile(t, 512)

    def body(xc_ref, wa_ref, wi_ref, ba_ref, bi_ref, lam_ref, a_ref, u_ref):
        for n in range(nb):
            cols = pl.ds(n * cb, cb)
            xcv = xc_ref[:, cols]
            _, _, ig, _, a, mult = _rg_gate_values(xcv, wa_ref.at[n], wi_ref.at[n], ba_ref.at[:, cols],
                                                   bi_ref.at[:, cols], lam_ref.at[:, cols])
            a_ref[:, cols] = a
            u_ref[:, cols] = mult * (ig * xcv)

    row = pl.BlockSpec((tm, c), lambda i: (i, 0))
    wsp = pl.BlockSpec((nb, cb, cb), lambda i: (0, 0, 0))
    vec = pl.BlockSpec((1, c), lambda i: (0, 0))
    return pl.pallas_call(
        body, grid=(t // tm,), in_specs=[row, wsp, wsp, vec, vec, vec], out_specs=[row, row],
        out_shape=[_sds((t, c), F32)] * 2, compiler_params=_params("parallel"), name=name,
    )(xc, wa, wi, ba, bi, lam)


def rg_gates_bwd(name, xc, gu, hs, wa, wi, ba, bi, lam):
    t, c = xc.shape
    nb, cb, _ = wa.shape
    tm = _row_tile(t, 512)
    hb = tm // HALO

    def body(xc_ref, gu_ref, h_ref, halo_ref, wa_ref, wi_ref, ba_ref, bi_ref, lam_ref,
             dxc_ref, dwa_ref, dwi_ref, dba_ref, dbi_ref, dlam_ref):
        i = pl.program_id(0)

        @pl.when(i == 0)
        def _():
            for ref in (dwa_ref, dwi_ref, dba_ref, dbi_ref, dlam_ref):
                ref[...] = jnp.zeros_like(ref)

        for n in range(nb):
            cols = pl.ds(n * cb, cb)
            hprev = _shifted(jnp.concatenate([jnp.where(i == 0, 0.0, halo_ref[:, cols]), h_ref[:, cols]], axis=0),
                             1, tm, True)
            xcv = xc_ref[:, cols]
            wa_n, wi_n, lam_n = wa_ref.at[n], wi_ref.at[n], lam_ref.at[:, cols]
            xb, r, ig, sp, a, mult = _rg_gate_values(xcv, wa_n, wi_n, ba_ref.at[:, cols], bi_ref.at[:, cols], lam_n)
            gv = gu_ref[:, cols]
            d_ixc = gv * mult
            d_i = d_ixc * xcv
            d_mult = gv * ig * xcv
            d_a = gv * hprev - d_mult * a / mult
            d_log_a = d_a * a
            d_r = d_log_a * (-LRU_C * sp)
            sig_neg_lam = 1.0 / (1.0 + jnp.exp(lam_n[...]))
            dlam_ref[:, cols] += jnp.sum(d_log_a * r, axis=0, keepdims=True) * (LRU_C * sig_neg_lam)
            dpa = d_r * r * (1.0 - r)
            dpi = d_i * ig * (1.0 - ig)
            dba_ref[:, cols] += jnp.sum(dpa, axis=0, keepdims=True)
            dbi_ref[:, cols] += jnp.sum(dpi, axis=0, keepdims=True)
            dpab, dpib = dpa.astype(BF16), dpi.astype(BF16)
            dxc_ref[:, cols] = (d_ixc * ig + lax.dot_general(dpab, wa_n[...], NT, preferred_element_type=F32)
                                + lax.dot_general(dpib, wi_n[...], NT, preferred_element_type=F32))
            dwa_ref[n] += lax.dot_general(xb, dpab, TN, preferred_element_type=F32)
            dwi_ref[n] += lax.dot_general(xb, dpib, TN, preferred_element_type=F32)

    row = pl.BlockSpec((tm, c), lambda i: (i, 0))
    wsp = pl.BlockSpec((nb, cb, cb), lambda i: (0, 0, 0))
    vec = pl.BlockSpec((1, c), lambda i: (0, 0))
    halo = pl.BlockSpec((HALO, c), lambda i: (jnp.maximum(i * hb - 1, 0), 0))
    return pl.pallas_call(
        body, grid=(t // tm,), in_specs=[row, row, row, halo, wsp, wsp, vec, vec, vec],
        out_specs=[row, wsp, wsp, vec, vec, vec],
        out_shape=[_sds((t, c), F32), _sds((nb, cb, cb), F32), _sds((nb, cb, cb), F32),
                   _sds((1, c), F32), _sds((1, c), F32), _sds((1, c), F32)],
        compiler_params=_params("arbitrary"), name=name)(xc, gu, hs, hs, wa, wi, ba, bi, lam)


SCAN_TS = 256
SCAN_TC = 1024


def _tile_scan(a, b, reverse):
    ts = a.shape[0]
    row = lax.broadcasted_iota(jnp.int32, a.shape, 0)
    d = 1
    while d < ts:
        if reverse:
            inside = row < ts - d
            a_sh = jnp.where(inside, pltpu.roll(a, ts - d, 0), 1.0)
            b_sh = jnp.where(inside, pltpu.roll(b, ts - d, 0), 0.0)
        else:
            inside = row >= d
            a_sh = jnp.where(inside, pltpu.roll(a, d, 0), 1.0)
            b_sh = jnp.where(inside, pltpu.roll(b, d, 0), 0.0)
        b = b + a * b_sh
        a = a * a_sh
        d *= 2
    return a, b


def rg_scan_fwd(name, a, u, gate_pre):
    t, c = a.shape
    ts, tc = _row_tile(t, SCAN_TS), _row_tile(c, SCAN_TC)

    def body(a_ref, u_ref, g_ref, h_ref, z_ref, carry_ref):
        s = pl.program_id(1)

        @pl.when(s == 0)
        def _():
            carry_ref[...] = jnp.zeros_like(carry_ref)

        ac, bc = _tile_scan(a_ref[...], u_ref[...], False)
        h = bc + ac * carry_ref[0:1, :]
        h_ref[...] = h
        z_ref[...] = (h * _gelu(g_ref[...])).astype(BF16)
        carry_ref[0:1, :] = h[ts - 1:ts, :]

    blk = pl.BlockSpec((ts, tc), lambda j, s: (s, j))
    return pl.pallas_call(
        body, grid=(c // tc, t // ts), in_specs=[blk, blk, blk], out_specs=[blk, blk],
        out_shape=[_sds((t, c), F32), _sds((t, c), BF16)], scratch_shapes=[pltpu.VMEM((8, tc), F32)],
        compiler_params=_params("parallel", "arbitrary"), name=name)(a, u, gate_pre)


def rg_scan_bwd(name, a, hs, gate_pre, dz):
    t, c = hs.shape
    ts, tc = _row_tile(t, SCAN_TS), _row_tile(c, SCAN_TC)
    nt = t // ts
    hb = ts // HALO

    def body(a_ref, halo_ref, h_ref, g_ref, dz_ref, gu_ref, dgate_ref, carry_ref):
        s = pl.program_id(1)

        @pl.when(s == 0)
        def _():
            carry_ref[...] = jnp.zeros_like(carry_ref)

        a_next = _shifted(jnp.concatenate([a_ref[...], jnp.where(s == 0, 0.0, halo_ref[...])], axis=0), 1, ts, False)
        gate = g_ref[...]
        dzv = dz_ref[...]
        dgate_ref[...] = (dzv * h_ref[...] * _gelu_grad(gate)).astype(BF16)
        ac, bc = _tile_scan(a_next, dzv * _gelu(gate), True)
        gu = bc + ac * carry_ref[0:1, :]
        gu_ref[...] = gu
        carry_ref[0:1, :] = gu[0:1, :]

    blk = pl.BlockSpec((ts, tc), lambda j, s: (nt - 1 - s, j))
    halo = pl.BlockSpec((HALO, tc), lambda j, s: (jnp.minimum((nt - s) * hb, t // HALO - 1), j))
    return pl.pallas_call(
        body, grid=(c // tc, nt), in_specs=[blk, halo, blk, blk, blk], out_specs=[blk, blk],
        out_shape=[_sds((t, c), F32), _sds((t, c), BF16)], scratch_shapes=[pltpu.VMEM((8, tc), F32)],
        compiler_params=_params("parallel", "arbitrary"), name=name)(a, a, hs, gate_pre, dz)


QA_BLK, KA_BLK, VA_BLK, QS_BLK, KS_BLK, VS_BLK = (g * N_PAIRS for g in range(6))


TOEP_W = 640
TOEP_FLAT = 320
TABLE_LOW = 193


def rel_bias_matrix(name, table):
    h = table.shape[0]
    diag = jnp.concatenate([jnp.repeat(table[:, 2 * REL_CLIP:], TOEP_FLAT, axis=1),
                            jnp.flip(table[:, TABLE_LOW:2 * REL_CLIP], axis=1),
                            jnp.zeros((h, 1), table.dtype)], axis=1)[:, None, :]

    def body(v_ref, o_ref):
        rows = jnp.broadcast_to(v_ref[...], (CHUNK, TOEP_W))
        o_ref[...] = pltpu.roll(rows, TOEP_W - (CHUNK - 1), 1, stride=1, stride_axis=0)

    out = pl.pallas_call(
        body, grid=(h,), in_specs=[pl.BlockSpec((None, 1, TOEP_W), lambda hh: (hh, 0, 0))],
        out_specs=pl.BlockSpec((None, CHUNK, TOEP_W), lambda hh: (hh, 0, 0)),
        out_shape=_sds((h, CHUNK, TOEP_W), F32), compiler_params=_params("parallel"), name=name)(diag)
    return out[:, :, :BAND]


def rel_bias_grad(name, dbias):
    h = dbias.shape[0]
    flipped = jnp.pad(jnp.flip(dbias, axis=1), ((0, 0), (0, 0), (0, TOEP_W - BAND)))

    def body(x_ref, o_ref):
        skew = pltpu.roll(x_ref[...], 0, 1, stride=1, stride_axis=0)
        col = jnp.sum(skew, axis=0, keepdims=True)
        lane = lax.broadcasted_iota(jnp.int32, col.shape, 1)
        flat = jnp.sum(jnp.where(lane < TOEP_FLAT, col, 0.0), axis=1, keepdims=True)
        o_ref[...] = jnp.where(lane == TOEP_W - 1, flat, col)

    out = pl.pallas_call(
        body, grid=(h,), in_specs=[pl.BlockSpec((None, CHUNK, TOEP_W), lambda hh: (hh, 0, 0))],
        out_specs=pl.BlockSpec((None, 1, TOEP_W), lambda hh: (hh, 0, 0)),
        out_shape=_sds((h, 1, TOEP_W), F32), compiler_params=_params("parallel"), name=name)(flipped)[:, 0, :]
    return jnp.concatenate([jnp.zeros((h, TABLE_LOW), F32), jnp.flip(out[:, TOEP_FLAT:TOEP_W - 1], axis=1),
                            out[:, TOEP_W - 1:]], axis=1)


def attn_layer_fwd(tag, x, h, w, g_next):
    proj = mm_nn_wblk(tag + "_proj", h, w["w_in"], w["idx"], BF16)
    width = N_PAIRS * PAIR
    pad = lambda a: jnp.pad(a, ((PAD_KEYS, 0), (0, 0)))
    kap, vap = pad(proj[:, width:2 * width]), pad(proj[:, 2 * width:3 * width])
    bias = rel_bias_matrix(tag + "_bias", w["rel_bias"])
    oa, lse = attn_a_fwd(tag + "_a", proj, kap, vap, bias, QA_BLK)
    ob = sb_fwd(tag + "_sb", proj, QS_BLK, KS_BLK, VS_BLK)
    o = jnp.concatenate([oa, ob], axis=1).astype(BF16)
    m = mm_nn(tag + "_out", o, w["w_out"], F32)
    x1, h_next = resid_norm_fwd(tag + "_res", x, m, w["g_post"], g_next)
    return x1, h_next, (x, h, proj, kap, vap, bias, oa, lse, ob, o, m)


def attn_layer_bwd(tag, dm, dx1, saved, w, prev):
    x, h, proj, kap, vap, bias, oa, lse, ob, o, m = saved
    d_w_out = mm_tn(tag + "_dwout", o, dm, F32)
    do = mm_nt(tag + "_do", dm, w["w_out"], BF16)
    dqa, dkap, dvap, dbias = attn_a_bwd(tag + "_da", proj, kap, vap, bias, oa, lse, do, QA_BLK, 0)
    dqs, dks, dvs = sb_bwd(tag + "_dsb", proj, ob, do, QS_BLK, KS_BLK, VS_BLK, N_PAIRS)
    d_rel = rel_bias_grad(tag + "_dbias", dbias)
    dproj = jnp.concatenate([dqa, dkap[PAD_KEYS:], dvap[PAD_KEYS:], dqs, dks, dvs], axis=1).astype(BF16)
    d_w_in = mm_tn_oblk(tag + "_dwin", h, dproj, w["w_in"].shape[3], F32)
    dh = mm_nt_wblk(tag + "_dh", dproj, w["w_in"], w["idx"], F32)
    dx, dg_pre, dm_prev, dg_post_prev = close_bwd(tag, dh, x, w["g_pre"], dx1, prev)
    return dx, dm_prev, dg_post_prev, dict(w_in=d_w_in, w_out=d_w_out, rel_bias=d_rel, g_pre=dg_pre)


def rg_layer_fwd(tag, x, h, w, g_next):
    proj = mm_nn_wblk(tag + "_proj", h, w["w_in"], w["idx"], F32)
    xc = conv4_fwd(tag + "_conv", proj, 1, w["conv_w"], w["conv_b"])
    a, u = rg_gates_fwd(tag + "_gates", xc, w["w_a"], w["w_i"], w["b_a"], w["b_i"], w["lam"])
    hs, z = rg_scan_fwd(tag + "_scan", a, u, proj)
    m = mm_nn(tag + "_out", z, w["w_out"], F32)
    x1, h_next = resid_norm_fwd(tag + "_res", x, m, w["g_post"], g_next)
    return x1, h_next, (x, h, proj, xc, a, hs, z, m)


def rg_layer_bwd(tag, dm, dx1, saved, w, prev):
    x, h, proj, xc, a, hs, z, m = saved
    d_w_out = mm_tn(tag + "_dwout", z, dm, F32)
    dz = mm_nt(tag + "_dz", dm, w["w_out"], F32)
    gu, dgate = rg_scan_bwd(tag + "_dscan", a, hs, proj, dz)
    dxc, d_w_a, d_w_i, d_b_a, d_b_i, d_lam = rg_gates_bwd(
        tag + "_dgates", xc, gu, hs, w["w_a"], w["w_i"], w["b_a"], w["b_i"], w["lam"])
    d_conv_w, d_conv_b = conv4_bwd_w(tag + "_dconvw", proj, 1, dxc)
    dxr = conv4_bwd_x(tag + "_dconv", dxc, w["conv_w"])
    dproj = jnp.concatenate([dgate, dxr.astype(BF16)], axis=1)
    d_w_in = mm_tn_oblk(tag + "_dwin", h, dproj, w["w_in"].shape[3], F32)
    dh = mm_nt_wblk(tag + "_dh", dproj, w["w_in"], w["idx"], F32)
    dx, dg_pre, dm_prev, dg_post_prev = close_bwd(tag, dh, x, w["g_pre"], dx1, prev)
    return dx, dm_prev, dg_post_prev, dict(w_in=d_w_in, w_out=d_w_out, conv_w=d_conv_w, conv_b=d_conv_b, w_a=d_w_a,
                                           w_i=d_w_i, b_a=d_b_a, b_i=d_b_i, lam=d_lam, g_pre=dg_pre)


def ffn_layer_fwd(tag, x, h, w, g_next):
    f8 = w["w_down"].shape[2]
    gu, a = ffn_up(tag + "_up", h, w["w_gu"], f8, w["idx"])
    f = ffn_down(tag + "_down", a, w["w_down"], f8, w["idx"])
    x1, h_next = resid_norm_fwd(tag + "_res", x, f, w["g_post"], g_next)
    return x1, h_next, (x, h, gu, a, f)


def ffn_layer_bwd(tag, dm, dx1, saved, w, prev):
    x, h, gu, a, f = saved
    f8 = w["w_down"].shape[2]
    d_w_down = ffn_dw_down(tag + "_dwdown", a, dm, f8)
    dgu = ffn_bwd_act(tag + "_dact", dm, w["w_down"], gu, f8, w["idx"])
    d_w_gu = ffn_dw_in(tag + "_dwgu", h, dgu)
    dh = ffn_bwd_dh(tag + "_dh", dgu, w["w_gu"], w["idx"])
    dx, dg_pre, dm_prev, dg_post_prev = close_bwd(tag, dh, x, w["g_pre"], dx1, prev)
    return dx, dm_prev, dg_post_prev, dict(w_gu=d_w_gu, w_down=d_w_down, g_pre=dg_pre)


def _place():
    return lax.axis_index("x"), lax.axis_index("y"), lax.axis_index("c")


def all_gather(name, blks):
    n = len(blks)

    def body(*refs):
        x_refs, out_refs = refs[:n], refs[n:2 * n]
        send_sems, recv_sems, local_sems = refs[2 * n:]
        x, y, cc = _place()
        me, sibling = (x, y, cc), (x, y, 1 - cc)
        chips = [(1 - x, y), (x, 1 - y), (1 - x, 1 - y)]
        south = cc == 0
        via = (jnp.where(south, 1 - x, x), jnp.where(south, y, 1 - y))
        onward = (jnp.where(south, x, 1 - x), jnp.where(south, 1 - y, y))
        k_via, k_onward = 1 + cc, 2 - cc

        def slot(a, px, py, pc):
            return out_refs[a].at[4 * px + 2 * py + pc]

        def copy(a, k, block, to, src=None):
            return pltpu.make_async_remote_copy(
                src_ref=slot(a, *block) if src is None else src, dst_ref=slot(a, *block),
                send_sem=send_sems.at[7 * a + k], recv_sem=recv_sems.at[7 * a + k], device_id=to, device_id_type=MESH)

        mine = [pltpu.make_async_copy(x_refs[a], slot(a, *me), local_sems.at[a]) for a in range(n)]
        sends = []
        for a in range(n):
            mine[a].start()
            sends.append(copy(a, 0, me, sibling, src=x_refs[a]))
            sends += [copy(a, 1 + j, me, (*chips[j], cc), src=x_refs[a]) for j in range(2)]
        for cp in sends:
            cp.start()
        for a in range(n):
            copy(a, k_via, (*via, cc), me).wait_recv()
            sends.append(copy(a, 3, (*via, cc), (*onward, cc)))
            sends.append(copy(a, 3 + k_via, (*via, cc), sibling))
            sends[-2].start()
            sends[-1].start()
        for a in range(n):
            copy(a, k_onward, (*onward, cc), me).wait_recv()
            sends.append(copy(a, 3 + k_onward, (*onward, cc), sibling))
            sends[-1].start()
        for a in range(n):
            copy(a, 3, (*chips[2], cc), me).wait_recv()
            sends.append(copy(a, 6, (*chips[2], cc), sibling))
            sends[-1].start()
        for a in range(n):
            copy(a, 0, sibling, me).wait_recv()
            for j, chip in enumerate(chips):
                copy(a, 4 + j, (*chip, 1 - cc), me).wait_recv()
        for cp in sends:
            cp.wait_send()
        for cp in mine:
            cp.wait()

    return pl.pallas_call(
        body, out_shape=[_sds((N_DEV,) + b.shape, b.dtype) for b in blks], in_specs=[ANY] * n, out_specs=[ANY] * n,
        scratch_shapes=[pltpu.SemaphoreType.DMA((7 * n,)), pltpu.SemaphoreType.DMA((7 * n,)),
                        pltpu.SemaphoreType.DMA((n,))],
        name=name)(*blks)


def exchange_pair(name, gs):
    n = len(gs)
    nchip = 4

    def body(*refs):
        g_refs, land_refs = refs[:n], refs[n:2 * n]
        send_sems, recv_sems = refs[2 * n:]
        x, y, cc = _place()
        copies = [pltpu.make_async_remote_copy(
            src_ref=g_refs[a].at[j, 1 - cc], dst_ref=land_refs[a].at[j], send_sem=send_sems.at[nchip * a + j],
            recv_sem=recv_sems.at[nchip * a + j], device_id=(x, y, 1 - cc), device_id_type=MESH)
            for a in range(n) for j in range(nchip)]
        for cp in copies:
            cp.start()
        for cp in copies:
            cp.wait()

    return pl.pallas_call(
        body, out_shape=[_sds((nchip,) + g.shape[2:], g.dtype) for g in gs], in_specs=[ANY] * n, out_specs=[ANY] * n,
        scratch_shapes=[pltpu.SemaphoreType.DMA((nchip * n,)), pltpu.SemaphoreType.DMA((nchip * n,))],
        name=name)(*gs)


def pair_sum(name, g, land, core, out_dtype):
    nchip, _, r, c = g.shape
    tr = _divisor_tile(r, 1024, 16)

    def body(core_ref, g_ref, l_ref, o_ref):
        o_ref[...] = (g_ref[...] + l_ref[...]).astype(o_ref.dtype)

    return pl.pallas_call(
        body,
        grid_spec=pltpu.PrefetchScalarGridSpec(
            num_scalar_prefetch=1, grid=(nchip, r // tr),
            in_specs=[pl.BlockSpec((None, None, tr, c), lambda j, i, core_ref: (j, core_ref[0], i, 0)),
                      pl.BlockSpec((None, tr, c), lambda j, i, core_ref: (j, i, 0))],
            out_specs=pl.BlockSpec((None, tr, c), lambda j, i, core_ref: (j, i, 0))),
        out_shape=_sds((nchip, r, c), out_dtype), compiler_params=_params("parallel", "parallel"), name=name,
    )(core, g, land)


def exchange_chips(name, ps):
    n = len(ps)

    def body(*refs):
        p_refs, land_refs = refs[:n], refs[n:2 * n]
        send_sems, recv_sems, local_sems = refs[2 * n:]
        x, y, cc = _place()
        mine = 2 * x + y
        chips = [(1 - x, y), (x, 1 - y), (1 - x, 1 - y)]
        own = [pltpu.make_async_copy(p_refs[a].at[mine], land_refs[a].at[mine], local_sems.at[a]) for a in range(n)]
        for cp in own:
            cp.start()
        sends = [pltpu.make_async_remote_copy(
            src_ref=p_refs[a].at[2 * px + py], dst_ref=land_refs[a].at[mine], send_sem=send_sems.at[3 * a + k],
            recv_sem=recv_sems.at[3 * a + k], device_id=(px, py, cc), device_id_type=MESH)
            for a in range(n) for k, (px, py) in enumerate(chips)]
        for cp in sends:
            cp.start()
        for a in range(n):
            for k, (px, py) in enumerate(chips):
                pltpu.make_async_remote_copy(
                    src_ref=p_refs[a].at[mine], dst_ref=land_refs[a].at[2 * px + py], send_sem=send_sems.at[3 * a + k],
                    recv_sem=recv_sems.at[3 * a + k], device_id=(px, py, cc), device_id_type=MESH).wait_recv()
        for cp in sends:
            cp.wait_send()
        for cp in own:
            cp.wait()

    return pl.pallas_call(
        body, out_shape=[_sds(p.shape, p.dtype) for p in ps], in_specs=[ANY] * n, out_specs=[ANY] * n,
        scratch_shapes=[pltpu.SemaphoreType.DMA((3 * n,)), pltpu.SemaphoreType.DMA((3 * n,)),
                        pltpu.SemaphoreType.DMA((n,))],
        name=name)(*ps)


def adamw(name, parts, w, m, v):
    npart, r, c = parts.shape
    tr = _divisor_tile(r, 512, 16)
    c1 = 1.0 / (1.0 - ADAM_B1 ** ADAM_STEP)
    c2 = 1.0 / (1.0 - ADAM_B2 ** ADAM_STEP)

    def body(p_ref, w_ref, m_ref, v_ref, g_ref, d_ref, nm_ref, nv_ref):
        g = p_ref[0].astype(F32)
        for j in range(1, npart):
            g = g + p_ref[j].astype(F32)
        nm = ADAM_B1 * m_ref[...] + (1.0 - ADAM_B1) * g
        nv = ADAM_B2 * v_ref[...] + (1.0 - ADAM_B2) * (g * g)
        g_ref[...] = g
        nm_ref[...] = nm
        nv_ref[...] = nv
        d_ref[...] = -ADAM_LR * ((nm * c1) / (jnp.sqrt(nv * c2) + ADAM_EPS) + ADAM_WD * w_ref[...])

    row = pl.BlockSpec((tr, c), lambda i: (i, 0))
    return pl.pallas_call(
        body, grid=(r // tr,), in_specs=[pl.BlockSpec((npart, tr, c), lambda i: (0, i, 0)), row, row, row],
        out_specs=[row] * 4, out_shape=[_sds((r, c), F32)] * 4, compiler_params=_params("parallel"), name=name,
    )(parts, w, m, v)


def _pack(arrays, dtype, row_multiple):
    flat = jnp.concatenate([a.astype(dtype).reshape(-1) for a in arrays])
    per = row_multiple * LANES
    total = -(-flat.shape[0] // per) * per
    return jnp.pad(flat, (0, total - flat.shape[0])).reshape(total // LANES, LANES)


def _pack_blocked(arrays, dtype, row_multiple):
    flat = jnp.concatenate([a.astype(dtype).reshape(N_DEV, -1) for a in arrays], axis=1)
    per = row_multiple * LANES
    total = -(-flat.shape[1] // per) * per
    return jnp.pad(flat, ((0, 0), (0, total - flat.shape[1]))).reshape(N_DEV, total // LANES, LANES)


def _unpack(buf, shapes, lead=()):
    flat = buf.reshape(lead + (-1,))
    out, off = [], 0
    for s in shapes:
        n = math.prod(s)
        out.append(flat[..., off:off + n].reshape(lead + tuple(s)))
        off += n
    return out


def _to_blocked(full, ax):
    s = full.shape
    return jnp.moveaxis(full.reshape(s[:ax] + (N_DEV, s[ax] // N_DEV) + s[ax + 1:]), ax, 0)


def _from_blocked(blk, ax):
    moved = jnp.moveaxis(blk, 0, ax)
    s = moved.shape
    return moved.reshape(s[:ax] + (s[ax] * s[ax + 1],) + s[ax + 2:])


SMALL = ("rg_conv_w", "rg_conv_b", "rg_b_a", "rg_b_i", "rg_lambda")
GU = "ffn_w_gu"
BIG = ("attn_w_in", "attn_w_out", "rg_w_in", "rg_w_a", "rg_w_i", "rg_w_out", GU, "ffn_w_down")


def kernel(x, attn_w_in, attn_rel_bias, attn_w_out, rg_w_in, rg_conv_w, rg_conv_b, rg_w_a, rg_b_a, rg_w_i, rg_b_i, rg_lambda, rg_w_out, norm_mix_pre, norm_mix_post, norm_ffn_pre, norm_ffn_post, ffn_w_gate, ffn_w_up, ffn_w_down, loss_target, m_attn_w_in, m_attn_rel_bias, m_attn_w_out, m_rg_w_in, m_rg_conv_w, m_rg_conv_b, m_rg_w_a, m_rg_b_a, m_rg_w_i, m_rg_b_i, m_rg_lambda, m_rg_w_out, m_norm_mix_pre, m_norm_mix_post, m_norm_ffn_pre, m_norm_ffn_post, m_ffn_w_gate, m_ffn_w_up, m_ffn_w_down, v_attn_w_in, v_attn_rel_bias, v_attn_w_out, v_rg_w_in, v_rg_conv_w, v_rg_conv_b, v_rg_w_a, v_rg_b_a, v_rg_w_i, v_rg_b_i, v_rg_lambda, v_rg_w_out, v_norm_mix_pre, v_norm_mix_post, v_norm_ffn_pre, v_norm_ffn_post, v_ffn_w_gate, v_ffn_w_up, v_ffn_w_down):
    w_loc = dict(attn_w_in=attn_w_in, attn_rel_bias=attn_rel_bias, attn_w_out=attn_w_out, rg_w_in=rg_w_in,
                 rg_conv_w=rg_conv_w, rg_conv_b=rg_conv_b, rg_w_a=rg_w_a, rg_b_a=rg_b_a, rg_w_i=rg_w_i, rg_b_i=rg_b_i,
                 rg_lambda=rg_lambda, rg_w_out=rg_w_out, norm_mix_pre=norm_mix_pre, norm_mix_post=norm_mix_post,
                 norm_ffn_pre=norm_ffn_pre, norm_ffn_post=norm_ffn_post, ffn_w_gate=ffn_w_gate, ffn_w_up=ffn_w_up,
                 ffn_w_down=ffn_w_down)
    m_loc = dict(attn_w_in=m_attn_w_in, attn_rel_bias=m_attn_rel_bias, attn_w_out=m_attn_w_out, rg_w_in=m_rg_w_in,
                 rg_conv_w=m_rg_conv_w, rg_conv_b=m_rg_conv_b, rg_w_a=m_rg_w_a, rg_b_a=m_rg_b_a, rg_w_i=m_rg_w_i,
                 rg_b_i=m_rg_b_i, rg_lambda=m_rg_lambda, rg_w_out=m_rg_w_out, norm_mix_pre=m_norm_mix_pre,
                 norm_mix_post=m_norm_mix_post, norm_ffn_pre=m_norm_ffn_pre, norm_ffn_post=m_norm_ffn_post,
                 ffn_w_gate=m_ffn_w_gate, ffn_w_up=m_ffn_w_up, ffn_w_down=m_ffn_w_down)
    v_loc = dict(attn_w_in=v_attn_w_in, attn_rel_bias=v_attn_rel_bias, attn_w_out=v_attn_w_out, rg_w_in=v_rg_w_in,
                 rg_conv_w=v_rg_conv_w, rg_conv_b=v_rg_conv_b, rg_w_a=v_rg_w_a, rg_b_a=v_rg_b_a, rg_w_i=v_rg_w_i,
                 rg_b_i=v_rg_b_i, rg_lambda=v_rg_lambda, rg_w_out=v_rg_w_out, norm_mix_pre=v_norm_mix_pre,
                 norm_mix_post=v_norm_mix_post, norm_ffn_pre=v_norm_ffn_pre, norm_ffn_post=v_norm_ffn_post,
                 ffn_w_gate=v_ffn_w_gate, ffn_w_up=v_ffn_w_up, ffn_w_down=v_ffn_w_down)
    axis_of = dict(SHARDED)
    xt, target = x[0], loss_target[0]
    d_model = xt.shape[1]
    rows2d = lambda a: a.reshape(-1, a.shape[-1])
    small_shapes = [w_loc[n].shape for n in SMALL]
    f8 = ffn_w_gate.shape[-1]
    for d in (w_loc, m_loc, v_loc):
        d[GU] = merge_gu(d["ffn_w_gate"], d["ffn_w_up"])

    gathered = all_gather("gather_weights", [rows2d(w_loc[n]).astype(BF16) for n in BIG]
                          + [_pack([w_loc[n] for n in SMALL], F32, 8)])
    blocked = {n: g.reshape((N_DEV,) + w_loc[n].shape) for n, g in zip(BIG, gathered)}
    blocked.update(zip(SMALL, _unpack(gathered[-1], small_shapes, (N_DEV,))))
    full = {n: _from_blocked(blocked[n], axis_of[n]) for n in SMALL}
    row = lambda a: a.reshape(1, -1).astype(F32)
    square = lambda rows8: rows8.reshape(-1, rows8.shape[-1])
    gates = lambda g: jnp.swapaxes(g, 0, 1).reshape(LRU_BLOCKS, -1, g.shape[-1])

    def layer_weights(layer):
        j = layer // 2
        norms = dict(g_pre=row(norm_mix_pre[layer]), g_post=row(norm_mix_post[layer]), idx=j)
        if layer % 2 == 0:
            mix = dict(w_in=blocked["attn_w_in"], w_out=square(blocked["attn_w_out"][:, j]),
                       rel_bias=attn_rel_bias[j], **norms)
        else:
            mix = dict(w_in=blocked["rg_w_in"], w_out=square(blocked["rg_w_out"][:, j]),
                       conv_w=full["rg_conv_w"][j][:, 0, :], conv_b=row(full["rg_conv_b"][j]),
                       w_a=gates(blocked["rg_w_a"][:, j]), w_i=gates(blocked["rg_w_i"][:, j]),
                       b_a=row(full["rg_b_a"][j]), b_i=row(full["rg_b_i"][j]), lam=row(full["rg_lambda"][j]), **norms)
        ffn = dict(w_gu=blocked[GU], w_down=blocked["ffn_w_down"], idx=layer,
                   g_pre=row(norm_ffn_pre[layer]), g_post=row(norm_ffn_post[layer]))
        return mix, ffn

    weights = [layer_weights(layer) for layer in range(DEPTH)]
    act, tape = xt, []
    h = rmsnorm_fwd("l0_mix_norm", act, weights[0][0]["g_pre"])
    for layer in range(DEPTH):
        mix_w, ffn_w = weights[layer]
        mixer_fwd = attn_layer_fwd if layer % 2 == 0 else rg_layer_fwd
        act, h, saved_mix = mixer_fwd(f"l{layer}_mix", act, h, mix_w, ffn_w["g_pre"])
        g_next = weights[layer + 1][0]["g_pre"] if layer + 1 < DEPTH else None
        act, h, saved_ffn = ffn_layer_fwd(f"l{layer}_ffn", act, h, ffn_w, g_next)
        tape.append((mix_w, ffn_w, saved_mix, saved_ffn))
    dact, sq = loss_grad("loss", act, target)
    loss_part = (0.5 * jnp.sum(sq) / d_model).reshape(1)

    grads = {}
    last = tape[DEPTH - 1]
    dm, grads[("ffn_post", DEPTH - 1)] = norm_bwd(f"l{DEPTH - 1}_ffn_dpost", dact, last[3][-1], last[1]["g_post"],
                                                 None, BF16)
    for layer in reversed(range(DEPTH)):
        mix_w, ffn_w, saved_mix, saved_ffn = tape[layer]
        dact, dm, grads[("mix_post", layer)], grads[("ffn", layer)] = ffn_layer_bwd(
            f"l{layer}_ffn", dm, dact, saved_ffn, ffn_w, (saved_mix[-1], mix_w["g_post"]))
        mixer_bwd = attn_layer_bwd if layer % 2 == 0 else rg_layer_bwd
        prev = (tape[layer - 1][3][-1], tape[layer - 1][1]["g_post"]) if layer else None
        dact, dm, grads[("ffn_post", layer - 1)], grads[("mix", layer)] = mixer_bwd(
            f"l{layer}_mix", dm, dact, saved_mix, mix_w, prev)
    attn_g = [grads[("mix", l)] for l in range(0, DEPTH, 2)]
    rg_g = [grads[("mix", l)] for l in range(1, DEPTH, 2)]
    ffn_g = [grads[("ffn", l)] for l in range(DEPTH)]
    stack = lambda gs, key: jnp.stack([g[key] for g in gs])
    by_owner = lambda gs, key, f: jnp.stack([f(g[key]) for g in gs], axis=1)
    rows8 = lambda a: a.reshape(N_DEV, -1, a.shape[-1])
    ungates = lambda a: jnp.swapaxes(a.reshape(LRU_BLOCKS, N_DEV, -1, a.shape[-1]), 0, 1)
    same = lambda a: a
    blocked_g = dict(
        attn_w_in=by_owner(attn_g, "w_in", same), attn_w_out=by_owner(attn_g, "w_out", rows8),
        rg_w_in=by_owner(rg_g, "w_in", same), rg_w_out=by_owner(rg_g, "w_out", rows8),
        rg_w_a=by_owner(rg_g, "w_a", ungates), rg_w_i=by_owner(rg_g, "w_i", ungates),
        **{GU: by_owner(ffn_g, "w_gu", same)},
        ffn_w_down=by_owner(ffn_g, "w_down", same))
    contrib = dict(
        attn_rel_bias=stack(attn_g, "rel_bias"), rg_conv_w=stack(rg_g, "conv_w")[:, :, None, :],
        rg_conv_b=stack(rg_g, "conv_b")[:, 0], rg_b_a=stack(rg_g, "b_a").reshape(rg_b_a.shape[0], LRU_BLOCKS, -1),
        rg_b_i=stack(rg_g, "b_i").reshape(rg_b_i.shape[0], LRU_BLOCKS, -1), rg_lambda=stack(rg_g, "lam")[:, 0],
        norm_mix_pre=jnp.concatenate([grads[("mix", l)]["g_pre"] for l in range(DEPTH)]),
        norm_mix_post=jnp.concatenate([grads[("mix_post", l)] for l in range(DEPTH)]),
        norm_ffn_pre=jnp.concatenate([g["g_pre"] for g in ffn_g]),
        norm_ffn_post=jnp.concatenate([grads[("ffn_post", l)] for l in range(DEPTH)]),
    )
    small_g = _pack_blocked([_to_blocked(contrib[n], axis_of[n]) for n in SMALL], F32, 8)

    slabs = [blocked_g[n].reshape(4, 2, -1, blocked_g[n].shape[-1]) for n in BIG] + [small_g.reshape(4, 2, -1, LANES)]
    core = lax.axis_index("c").astype(jnp.int32).reshape(1)
    from_sibling = exchange_pair("rs_pair", slabs)
    pairs = [pair_sum(f"rs_pair_sum_{i}", g, l, core, BF16 if i < len(BIG) else F32)
             for i, (g, l) in enumerate(zip(slabs, from_sibling))]
    by_chip = exchange_chips("rs_chips", pairs)
    result = {}
    kinds = ("grad", "delta", "new_m", "new_v")
    for n, parts in zip(BIG, by_chip):
        outs = adamw("adamw_" + n, parts, *[rows2d(d[n]) for d in (w_loc, m_loc, v_loc)])
        for kind, a in zip(kinds, outs):
            result[(kind, n)] = a.reshape(w_loc[n].shape)
    for kind in kinds:
        result[(kind, "ffn_w_gate")], result[(kind, "ffn_w_up")] = split_gu(result.pop((kind, GU)), f8)
    outs = adamw("adamw_small", by_chip[-1], *[_pack([d[n] for n in SMALL], F32, 8) for d in (w_loc, m_loc, v_loc)])
    for kind, buf in zip(kinds, outs):
        result.update({(kind, n): a for n, a in zip(SMALL, _unpack(buf, small_shapes))})
    rep_shapes = [w_loc[n].shape for n in REPLICATED] + [(1,)]
    rep_parts, = all_gather("gather_rep_grads", [_pack([contrib[n] for n in REPLICATED] + [loss_part], F32, 8)])
    outs = adamw("adamw_replicated", rep_parts, *[_pack([d[n] for n in REPLICATED] + [jnp.zeros((1,), F32)], F32, 8)
                                                  for d in (w_loc, m_loc, v_loc)])
    for kind, buf in zip(kinds, outs):
        result.update({(kind, n): a for n, a in zip(REPLICATED + ("loss",), _unpack(buf, rep_shapes))})
    loss = result[("grad", "loss")][0]
    return (loss, dact[None], *[result[(kind, n)] for kind in kinds for n in WEIGHTS])
```
